```python
import jax, jax.numpy as jnp
from jax import lax
import numpy as np


D_MODEL = 1024
BATCH = 8
SEQ = 2048
DEPTH = 2

BRANCH_WIDTH = 512
N_BRANCH = 4
EPS = 1e-6
S5_GROUP = 16
S5_GROUPS = BRANCH_WIDTH // S5_GROUP
S5_STATE = 64
S5_STEP_MIN = 1e-3
S5_STEP_MAX = 1e-1
SGU_CHUNK = 128
SGU_HEADS = 8
SGU_HEAD_DIM = BRANCH_WIDTH // SGU_HEADS
M2_HEAD_DIM = 64
M2_HEADS = BRANCH_WIDTH // M2_HEAD_DIM
M2_GROUPS = 2
M2_STATE = 128
M2_CONV = 4
M2_CHUNK = 128
M2_CONV_CH = BRANCH_WIDTH + 2 * M2_GROUPS * M2_STATE
M2_DT_MIN = 1e-3
M2_DT_MAX = 1e-1
SC_CONV = 3

IN_SIZES = (
    BRANCH_WIDTH, BRANCH_WIDTH,
    BRANCH_WIDTH, BRANCH_WIDTH, BRANCH_WIDTH,
    BRANCH_WIDTH, M2_CONV_CH, M2_HEADS,
    BRANCH_WIDTH, BRANCH_WIDTH, BRANCH_WIDTH, BRANCH_WIDTH,
    N_BRANCH * D_MODEL,
)
IN_DIM = int(sum(IN_SIZES))
IN_SPLITS = [int(v) for v in np.cumsum(IN_SIZES)[:-1]]

kernel_name = 'hybrid_s5_sgu_ssd_shortconv_gated_merge'


def rmsnorm(x, w):
    x32 = x.astype(jnp.float32)
    y = x32 * lax.rsqrt(jnp.mean(x32 * x32, axis=-1, keepdims=True) + EPS)
    return (y * w.astype(jnp.float32)).astype(x.dtype)


def causal_depthwise_conv(x, w):
    k, c = w.shape
    return lax.conv_general_dilated(
        x, w[:, None, :].astype(x.dtype), window_strides=(1,), padding=[(k - 1, 0)],
        dimension_numbers=('NWC', 'WIO', 'NWC'), feature_group_count=c)


def _complex_affine_combine(e1, e2):
    a1r, a1i, b1r, b1i = e1
    a2r, a2i, b2r, b2i = e2
    ar = a1r * a2r - a1i * a2i
    ai = a1r * a2i + a1i * a2r
    br = a2r * b1r - a2i * b1i + b2r
    bi = a2r * b1i + a2i * b1r + b2i
    return (ar, ai, br, bi)


def s5_branch(u, gate, lam_re, lam_im, b_re, b_im, c_re, c_im, d, log_step, w_glu):
    bsz, seq_len, _ = u.shape
    f32 = jnp.float32
    u32 = u.astype(f32).reshape(bsz, seq_len, S5_GROUPS, S5_GROUP)
    step = jnp.exp(log_step.astype(f32))[:, None]
    lr, li = lam_re.astype(f32), lam_im.astype(f32)
    mag = jnp.exp(lr * step)
    ab_re, ab_im = mag * jnp.cos(li * step), mag * jnp.sin(li * step)
    den = lr * lr + li * li
    nr = ab_re - 1.0
    coef_re = (nr * lr + ab_im * li) / den
    coef_im = (ab_im * lr - nr * li) / den
    br, bi = b_re.astype(f32), b_im.astype(f32)
    bb_re = coef_re[..., None] * br - coef_im[..., None] * bi
    bb_im = coef_re[..., None] * bi + coef_im[..., None] * br
    bu_re = jnp.einsum('blgp,gnp->blgn', u32, bb_re)
    bu_im = jnp.einsum('blgp,gnp->blgn', u32, bb_im)
    a_re = jnp.broadcast_to(ab_re, bu_re.shape)
    a_im = jnp.broadcast_to(ab_im, bu_re.shape)
    _, _, s_re, s_im = lax.associative_scan(
        _complex_affine_combine, (a_re, a_im, bu_re, bu_im), axis=1)
    y = (jnp.einsum('blgn,gpn->blgp', s_re, c_re.astype(f32))
         - jnp.einsum('blgn,gpn->blgp', s_im, c_im.astype(f32))
         + d.astype(f32) * u32)
    y = jax.nn.gelu(y.reshape(bsz, seq_len, BRANCH_WIDTH))
    y = y * jax.nn.sigmoid(y @ w_glu.astype(f32))
    return (y * jax.nn.silu(gate.astype(f32))).astype(u.dtype)


def sgu_branch(u, v, gate, ln_w, ln_b, w_s, b_s):
    bsz, seq_len, _ = u.shape
    f32 = jnp.float32
    u32 = jax.nn.gelu(u.astype(f32))
    v32 = jax.nn.gelu(v.astype(f32))
    mu = jnp.mean(v32, axis=-1, keepdims=True)
    var = jnp.mean(jnp.square(v32 - mu), axis=-1, keepdims=True)
    vn = (v32 - mu) * lax.rsqrt(var + EPS) * ln_w.astype(f32) + ln_b.astype(f32)
    vn = vn.reshape(bsz, seq_len // SGU_CHUNK, SGU_CHUNK, SGU_HEADS, SGU_HEAD_DIM)
    mask = jnp.tril(jnp.ones((SGU_CHUNK, SGU_CHUNK), dtype=bool))
    w_m = jnp.where(mask, w_s.astype(f32), 0.0)
    s = jnp.einsum('hts,bcshe->bcthe', w_m, vn) + b_s.astype(f32).T[:, :, None]
    out = u32 * s.reshape(bsz, seq_len, BRANCH_WIDTH)
    return (out * jax.nn.silu(gate.astype(f32))).astype(u.dtype)


def segsum(a):
    t = a.shape[-1]
    cs = jnp.cumsum(a, axis=-1)
    diff = cs[..., :, None] - cs[..., None, :]
    mask = jnp.tril(jnp.ones((t, t), dtype=bool))
    return jnp.where(mask, diff, -jnp.inf)


def mamba2_branch(z, xbc, dt_raw, conv_w, conv_b, dt_bias, a_log, d, norm_w):
    bsz, seq_len, _ = z.shape
    f32 = jnp.float32
    nc, q = seq_len // M2_CHUNK, M2_CHUNK
    xbc = jax.nn.silu((causal_depthwise_conv(xbc, conv_w) + conv_b).astype(f32))
    x, bm, cm = jnp.split(xbc, [BRANCH_WIDTH, BRANCH_WIDTH + M2_GROUPS * M2_STATE], axis=-1)
    rep = M2_HEADS // M2_GROUPS
    x = x.reshape(bsz, nc, q, M2_HEADS, M2_HEAD_DIM)
    bm = jnp.repeat(bm.reshape(bsz, seq_len, M2_GROUPS, M2_STATE), rep, axis=2).reshape(bsz, nc, q, M2_HEADS, M2_STATE)
    cm = jnp.repeat(cm.reshape(bsz, seq_len, M2_GROUPS, M2_STATE), rep, axis=2).reshape(bsz, nc, q, M2_HEADS, M2_STATE)
    dt = jax.nn.softplus(dt_raw.astype(f32) + dt_bias.astype(f32))
    a = -jnp.exp(a_log.astype(f32))
    da = (dt * a).reshape(bsz, nc, q, M2_HEADS).transpose(0, 3, 1, 2)
    a_cs = jnp.cumsum(da, axis=-1)
    xdt = x * dt.reshape(bsz, nc, q, M2_HEADS)[..., None]
    scores = jnp.einsum('bclhn,bcshn->bhcls', cm, bm) * jnp.exp(segsum(da))
    y_diag = jnp.einsum('bhcls,bcshp->bclhp', scores, xdt)
    decay_states = jnp.exp(a_cs[..., -1:] - a_cs)
    states = jnp.einsum('bclhn,bhcl,bclhp->bchpn', bm, decay_states, xdt)
    states = jnp.concatenate([jnp.zeros_like(states[:, :1]), states], axis=1)
    decay_chunk = jnp.exp(segsum(jnp.pad(a_cs[..., -1], ((0, 0), (0, 0), (1, 0)))))
    states = jnp.einsum('bhzc,bchpn->bzhpn', decay_chunk, states)[:, :-1]
    y_off = jnp.einsum('bclhn,bchpn,bhcl->bclhp', cm, states, jnp.exp(a_cs))
    y = y_diag + y_off + d.astype(f32)[:, None] * x
    y = y.reshape(bsz, seq_len, BRANCH_WIDTH) * jax.nn.silu(z.astype(f32))
    y = y * lax.rsqrt(jnp.mean(y * y, axis=-1, keepdims=True) + EPS) * norm_w.astype(f32)
    return y.astype(z.dtype)


def shortconv_branch(bg, cg, h, gate, conv_w):
    y = bg * causal_depthwise_conv(cg * h, conv_w)
    return y * jax.nn.silu(gate)


def _fwd_setup_inputs(seed: int = 0) -> dict:
    key = jax.random.key(seed)
    ks = jax.random.split(key, 32)
    f32 = jnp.float32
    W, D, G, N, P = BRANCH_WIDTH, D_MODEL, S5_GROUPS, S5_STATE, S5_GROUP
    nrm = lambda k, shape, s: jax.random.normal(k, shape, f32) * s
    x = jax.random.normal(ks[0], (BATCH, SEQ, D), f32)
    norm_w = 1.0 + nrm(ks[1], (DEPTH, D), 0.01)
    w_in = nrm(ks[2], (DEPTH, D, IN_DIM), D ** -0.5)
    s5_lambda_re = -0.5 + nrm(ks[3], (DEPTH, G, N), 0.01)
    s5_lambda_im = jnp.pi * jnp.arange(N, dtype=f32)[None, None, :] + nrm(ks[4], (DEPTH, G, N), 0.01)
    s5_b_re = nrm(ks[5], (DEPTH, G, N, P), (2.0 * P) ** -0.5)
    s5_b_im = nrm(ks[6], (DEPTH, G, N, P), (2.0 * P) ** -0.5)
    s5_c_re = nrm(ks[7], (DEPTH, G, P, N), (2.0 * N) ** -0.5)
    s5_c_im = nrm(ks[8], (DEPTH, G, P, N), (2.0 * N) ** -0.5)
    s5_d = nrm(ks[9], (DEPTH, G, P), 1.0)
    s5_log_step = jax.random.uniform(ks[10], (DEPTH, G), f32, np.log(S5_STEP_MIN), np.log(S5_STEP_MAX))
    s5_w_glu = nrm(ks[11], (DEPTH, W, W), W ** -0.5)
    sgu_ln_w = 1.0 + nrm(ks[12], (DEPTH, W), 0.01)
    sgu_ln_b = nrm(ks[13], (DEPTH, W), 0.01)
    sgu_w = nrm(ks[14], (DEPTH, SGU_HEADS, SGU_CHUNK, SGU_CHUNK), SGU_CHUNK ** -0.5)
    sgu_b = 1.0 + nrm(ks[15], (DEPTH, SGU_HEADS, SGU_CHUNK), 0.1)
    m2_conv_w = nrm(ks[16], (DEPTH, M2_CONV, M2_CONV_CH), M2_CONV ** -0.5)
    m2_conv_b = nrm(ks[17], (DEPTH, M2_CONV_CH), 0.01)
    dt0 = jnp.exp(jax.random.uniform(ks[18], (DEPTH, M2_HEADS), f32, np.log(M2_DT_MIN), np.log(M2_DT_MAX)))
    m2_dt_bias = dt0 + jnp.log(-jnp.expm1(-dt0))
    m2_a_log = jnp.log(jax.random.uniform(ks[19], (DEPTH, M2_HEADS), f32, 1.0, 16.0))
    m2_d = 1.0 + nrm(ks[20], (DEPTH, M2_HEADS), 0.01)
    m2_norm_w = 1.0 + nrm(ks[21], (DEPTH, W), 0.01)
    sc_conv_w = nrm(ks[22], (DEPTH, SC_CONV, W), SC_CONV ** -0.5)
    merge_b = nrm(ks[23], (DEPTH, N_BRANCH, D), 0.01)
    w_branch = nrm(ks[24], (DEPTH, N_BRANCH, W, D), W ** -0.5)
    w_out = nrm(ks[25], (DEPTH, D, D), D ** -0.5)
    final_norm_w = 1.0 + nrm(ks[26], (D,), 0.01)
    return {'x': x, 'norm_w': norm_w, 'w_in': w_in,
            's5_lambda_re': s5_lambda_re, 's5_lambda_im': s5_lambda_im,
            's5_b_re': s5_b_re, 's5_b_im': s5_b_im, 's5_c_re': s5_c_re, 's5_c_im': s5_c_im,
            's5_d': s5_d, 's5_log_step': s5_log_step, 's5_w_glu': s5_w_glu,
            'sgu_ln_w': sgu_ln_w, 'sgu_ln_b': sgu_ln_b, 'sgu_w': sgu_w, 'sgu_b': sgu_b,
            'm2_conv_w': m2_conv_w, 'm2_conv_b': m2_conv_b, 'm2_dt_bias': m2_dt_bias,
            'm2_a_log': m2_a_log, 'm2_d': m2_d, 'm2_norm_w': m2_norm_w,
            'sc_conv_w': sc_conv_w, 'merge_b': merge_b, 'w_branch': w_branch,
            'w_out': w_out, 'final_norm_w': final_norm_w}


def _fwd_reference(x, norm_w, w_in, s5_lambda_re, s5_lambda_im, s5_b_re, s5_b_im, s5_c_re, s5_c_im,
              s5_d, s5_log_step, s5_w_glu, sgu_ln_w, sgu_ln_b, sgu_w, sgu_b,
              m2_conv_w, m2_conv_b, m2_dt_bias, m2_a_log, m2_d, m2_norm_w,
              sc_conv_w, merge_b, w_branch, w_out, final_norm_w):
    bsz, seq_len, _ = x.shape
    for i in range(DEPTH):
        h = rmsnorm(x, norm_w[i])
        (s5_u, s5_g, sgu_u, sgu_v, sgu_g, m2_z, m2_xbc, m2_dt,
         sc_b, sc_c, sc_h, sc_g, merge_logits) = jnp.split(h @ w_in[i], IN_SPLITS, axis=-1)
        y_a = s5_branch(s5_u, s5_g, s5_lambda_re[i], s5_lambda_im[i], s5_b_re[i], s5_b_im[i],
                        s5_c_re[i], s5_c_im[i], s5_d[i], s5_log_step[i], s5_w_glu[i])
        y_b = sgu_branch(sgu_u, sgu_v, sgu_g, sgu_ln_w[i], sgu_ln_b[i], sgu_w[i], sgu_b[i])
        y_c = mamba2_branch(m2_z, m2_xbc, m2_dt, m2_conv_w[i], m2_conv_b[i], m2_dt_bias[i],
                            m2_a_log[i], m2_d[i], m2_norm_w[i])
        y_d = shortconv_branch(sc_b, sc_c, sc_h, sc_g, sc_conv_w[i])
        branches = jnp.stack([y_a, y_b, y_c, y_d], axis=2)
        branch_out = jnp.einsum('blkw,kwd->blkd', branches, w_branch[i])
        gates = jax.nn.sigmoid(
            merge_logits.reshape(bsz, seq_len, N_BRANCH, D_MODEL).astype(jnp.float32)
            + merge_b[i].astype(jnp.float32))
        merged = jnp.einsum('blkd,blkd->bld', gates, branch_out.astype(jnp.float32)).astype(x.dtype)
        x = x + merged @ w_out[i]
    return rmsnorm(x, final_norm_w)


import jax as _jax
import jax.numpy as _jnp

TWIN_FORMAT = 'train_step'
FWD_PARAMS = ['x', 'norm_w', 'w_in', 's5_lambda_re', 's5_lambda_im', 's5_b_re', 's5_b_im', 's5_c_re', 's5_c_im', 's5_d', 's5_log_step', 's5_w_glu', 'sgu_ln_w', 'sgu_ln_b', 'sgu_w', 'sgu_b', 'm2_conv_w', 'm2_conv_b', 'm2_dt_bias', 'm2_a_log', 'm2_d', 'm2_norm_w', 'sc_conv_w', 'merge_b', 'w_branch', 'w_out', 'final_norm_w']
TWIN_WEIGHTS = ['norm_w', 'w_in', 's5_lambda_re', 's5_lambda_im', 's5_b_re', 's5_b_im', 's5_c_re', 's5_c_im', 's5_d', 's5_log_step', 's5_w_glu', 'sgu_ln_w', 'sgu_ln_b', 'sgu_w', 'sgu_b', 'm2_conv_w', 'm2_conv_b', 'm2_dt_bias', 'm2_a_log', 'm2_d', 'm2_norm_w', 'sc_conv_w', 'merge_b', 'w_branch', 'w_out', 'final_norm_w']
TWIN_DIFF_INPUT = 'x'
TWIN_INPUTS = ['x', 'norm_w', 'w_in', 's5_lambda_re', 's5_lambda_im', 's5_b_re', 's5_b_im', 's5_c_re', 's5_c_im', 's5_d', 's5_log_step', 's5_w_glu', 'sgu_ln_w', 'sgu_ln_b', 'sgu_w', 'sgu_b', 'm2_conv_w', 'm2_conv_b', 'm2_dt_bias', 'm2_a_log', 'm2_d', 'm2_norm_w', 'sc_conv_w', 'merge_b', 'w_branch', 'w_out', 'final_norm_w', 'loss_target', 'm_norm_w', 'm_w_in', 'm_s5_lambda_re', 'm_s5_lambda_im', 'm_s5_b_re', 'm_s5_b_im', 'm_s5_c_re', 'm_s5_c_im', 'm_s5_d', 'm_s5_log_step', 'm_s5_w_glu', 'm_sgu_ln_w', 'm_sgu_ln_b', 'm_sgu_w', 'm_sgu_b', 'm_m2_conv_w', 'm_m2_conv_b', 'm_m2_dt_bias', 'm_m2_a_log', 'm_m2_d', 'm_m2_norm_w', 'm_sc_conv_w', 'm_merge_b', 'm_w_branch', 'm_w_out', 'm_final_norm_w', 'v_norm_w', 'v_w_in', 'v_s5_lambda_re', 'v_s5_lambda_im', 'v_s5_b_re', 'v_s5_b_im', 'v_s5_c_re', 'v_s5_c_im', 'v_s5_d', 'v_s5_log_step', 'v_s5_w_glu', 'v_sgu_ln_w', 'v_sgu_ln_b', 'v_sgu_w', 'v_sgu_b', 'v_m2_conv_w', 'v_m2_conv_b', 'v_m2_dt_bias', 'v_m2_a_log', 'v_m2_d', 'v_m2_norm_w', 'v_sc_conv_w', 'v_merge_b', 'v_w_branch', 'v_w_out', 'v_final_norm_w']
TWIN_OUTPUTS = ['loss', 'grad_x', 'grad_norm_w', 'grad_w_in', 'grad_s5_lambda_re', 'grad_s5_lambda_im', 'grad_s5_b_re', 'grad_s5_b_im', 'grad_s5_c_re', 'grad_s5_c_im', 'grad_s5_d', 'grad_s5_log_step', 'grad_s5_w_glu', 'grad_sgu_ln_w', 'grad_sgu_ln_b', 'grad_sgu_w', 'grad_sgu_b', 'grad_m2_conv_w', 'grad_m2_conv_b', 'grad_m2_dt_bias', 'grad_m2_a_log', 'grad_m2_d', 'grad_m2_norm_w', 'grad_sc_conv_w', 'grad_merge_b', 'grad_w_branch', 'grad_w_out', 'grad_final_norm_w', 'delta_norm_w', 'delta_w_in', 'delta_s5_lambda_re', 'delta_s5_lambda_im', 'delta_s5_b_re', 'delta_s5_b_im', 'delta_s5_c_re', 'delta_s5_c_im', 'delta_s5_d', 'delta_s5_log_step', 'delta_s5_w_glu', 'delta_sgu_ln_w', 'delta_sgu_ln_b', 'delta_sgu_w', 'delta_sgu_b', 'delta_m2_conv_w', 'delta_m2_conv_b', 'delta_m2_dt_bias', 'delta_m2_a_log', 'delta_m2_d', 'delta_m2_norm_w', 'delta_sc_conv_w', 'delta_merge_b', 'delta_w_branch', 'delta_w_out', 'delta_final_norm_w', 'new_m_norm_w', 'new_m_w_in', 'new_m_s5_lambda_re', 'new_m_s5_lambda_im', 'new_m_s5_b_re', 'new_m_s5_b_im', 'new_m_s5_c_re', 'new_m_s5_c_im', 'new_m_s5_d', 'new_m_s5_log_step', 'new_m_s5_w_glu', 'new_m_sgu_ln_w', 'new_m_sgu_ln_b', 'new_m_sgu_w', 'new_m_sgu_b', 'new_m_m2_conv_w', 'new_m_m2_conv_b', 'new_m_m2_dt_bias', 'new_m_m2_a_log', 'new_m_m2_d', 'new_m_m2_norm_w', 'new_m_sc_conv_w', 'new_m_merge_b', 'new_m_w_branch', 'new_m_w_out', 'new_m_final_norm_w', 'new_v_norm_w', 'new_v_w_in', 'new_v_s5_lambda_re', 'new_v_s5_lambda_im', 'new_v_s5_b_re', 'new_v_s5_b_im', 'new_v_s5_c_re', 'new_v_s5_c_im', 'new_v_s5_d', 'new_v_s5_log_step', 'new_v_s5_w_glu', 'new_v_sgu_ln_w', 'new_v_sgu_ln_b', 'new_v_sgu_w', 'new_v_sgu_b', 'new_v_m2_conv_w', 'new_v_m2_conv_b', 'new_v_m2_dt_bias', 'new_v_m2_a_log', 'new_v_m2_d', 'new_v_m2_norm_w', 'new_v_sc_conv_w', 'new_v_merge_b', 'new_v_w_branch', 'new_v_w_out', 'new_v_final_norm_w']
TWIN_LEAF_KINDS = {'loss': 'loss', 'grad_x': 'grad_x', 'grad_norm_w': 'grad_w', 'grad_w_in': 'grad_w', 'grad_s5_lambda_re': 'grad_w', 'grad_s5_lambda_im': 'grad_w', 'grad_s5_b_re': 'grad_w', 'grad_s5_b_im': 'grad_w', 'grad_s5_c_re': 'grad_w', 'grad_s5_c_im': 'grad_w', 'grad_s5_d': 'grad_w', 'grad_s5_log_step': 'grad_w', 'grad_s5_w_glu': 'grad_w', 'grad_sgu_ln_w': 'grad_w', 'grad_sgu_ln_b': 'grad_w', 'grad_sgu_w': 'grad_w', 'grad_sgu_b': 'grad_w', 'grad_m2_conv_w': 'grad_w', 'grad_m2_conv_b': 'grad_w', 'grad_m2_dt_bias': 'grad_w', 'grad_m2_a_log': 'grad_w', 'grad_m2_d': 'grad_w', 'grad_m2_norm_w': 'grad_w', 'grad_sc_conv_w': 'grad_w', 'grad_merge_b': 'grad_w', 'grad_w_branch': 'grad_w', 'grad_w_out': 'grad_w', 'grad_final_norm_w': 'grad_w', 'delta_norm_w': 'delta_w', 'delta_w_in': 'delta_w', 'delta_s5_lambda_re': 'delta_w', 'delta_s5_lambda_im': 'delta_w', 'delta_s5_b_re': 'delta_w', 'delta_s5_b_im': 'delta_w', 'delta_s5_c_re': 'delta_w', 'delta_s5_c_im': 'delta_w', 'delta_s5_d': 'delta_w', 'delta_s5_log_step': 'delta_w', 'delta_s5_w_glu': 'delta_w', 'delta_sgu_ln_w': 'delta_w', 'delta_sgu_ln_b': 'delta_w', 'delta_sgu_w': 'delta_w', 'delta_sgu_b': 'delta_w', 'delta_m2_conv_w': 'delta_w', 'delta_m2_conv_b': 'delta_w', 'delta_m2_dt_bias': 'delta_w', 'delta_m2_a_log': 'delta_w', 'delta_m2_d': 'delta_w', 'delta_m2_norm_w': 'delta_w', 'delta_sc_conv_w': 'delta_w', 'delta_merge_b': 'delta_w', 'delta_w_branch': 'delta_w', 'delta_w_out': 'delta_w', 'delta_final_norm_w': 'delta_w', 'new_m_norm_w': 'new_m', 'new_m_w_in': 'new_m', 'new_m_s5_lambda_re': 'new_m', 'new_m_s5_lambda_im': 'new_m', 'new_m_s5_b_re': 'new_m', 'new_m_s5_b_im': 'new_m', 'new_m_s5_c_re': 'new_m', 'new_m_s5_c_im': 'new_m', 'new_m_s5_d': 'new_m', 'new_m_s5_log_step': 'new_m', 'new_m_s5_w_glu': 'new_m', 'new_m_sgu_ln_w': 'new_m', 'new_m_sgu_ln_b': 'new_m', 'new_m_sgu_w': 'new_m', 'new_m_sgu_b': 'new_m', 'new_m_m2_conv_w': 'new_m', 'new_m_m2_conv_b': 'new_m', 'new_m_m2_dt_bias': 'new_m', 'new_m_m2_a_log': 'new_m', 'new_m_m2_d': 'new_m', 'new_m_m2_norm_w': 'new_m', 'new_m_sc_conv_w': 'new_m', 'new_m_merge_b': 'new_m', 'new_m_w_branch': 'new_m', 'new_m_w_out': 'new_m', 'new_m_final_norm_w': 'new_m', 'new_v_norm_w': 'new_v', 'new_v_w_in': 'new_v', 'new_v_s5_lambda_re': 'new_v', 'new_v_s5_lambda_im': 'new_v', 'new_v_s5_b_re': 'new_v', 'new_v_s5_b_im': 'new_v', 'new_v_s5_c_re': 'new_v', 'new_v_s5_c_im': 'new_v', 'new_v_s5_d': 'new_v', 'new_v_s5_log_step': 'new_v', 'new_v_s5_w_glu': 'new_v', 'new_v_sgu_ln_w': 'new_v', 'new_v_sgu_ln_b': 'new_v', 'new_v_sgu_w': 'new_v', 'new_v_sgu_b': 'new_v', 'new_v_m2_conv_w': 'new_v', 'new_v_m2_conv_b': 'new_v', 'new_v_m2_dt_bias': 'new_v', 'new_v_m2_a_log': 'new_v', 'new_v_m2_d': 'new_v', 'new_v_m2_norm_w': 'new_v', 'new_v_sc_conv_w': 'new_v', 'new_v_merge_b': 'new_v', 'new_v_w_branch': 'new_v', 'new_v_w_out': 'new_v', 'new_v_final_norm_w': 'new_v'}


def _forward(args):
    return _fwd_reference(*[args[k] for k in FWD_PARAMS])


def _output_shape():
    out = _jax.eval_shape(lambda: _forward(_fwd_setup_inputs(0)))
    return out.shape, out.dtype

N_MICROBATCH = 1
ADAM_LR = 0.001
ADAM_B1 = 0.9
ADAM_B2 = 0.999
ADAM_EPS = 1e-08
ADAM_WD = 0.01
ADAM_STEP = 10
PER_EXAMPLE_BATCH_AXIS = {'x': 0, 'loss_target': 0}
SHARED_INPUTS = []
_WEIGHT_DTYPES = {'norm_w': _jnp.float32, 'w_in': _jnp.float32, 's5_lambda_re': _jnp.float32, 's5_lambda_im': _jnp.float32, 's5_b_re': _jnp.float32, 's5_b_im': _jnp.float32, 's5_c_re': _jnp.float32, 's5_c_im': _jnp.float32, 's5_d': _jnp.float32, 's5_log_step': _jnp.float32, 's5_w_glu': _jnp.float32, 'sgu_ln_w': _jnp.float32, 'sgu_ln_b': _jnp.float32, 'sgu_w': _jnp.float32, 'sgu_b': _jnp.float32, 'm2_conv_w': _jnp.float32, 'm2_conv_b': _jnp.float32, 'm2_dt_bias': _jnp.float32, 'm2_a_log': _jnp.float32, 'm2_d': _jnp.float32, 'm2_norm_w': _jnp.float32, 'sc_conv_w': _jnp.float32, 'merge_b': _jnp.float32, 'w_branch': _jnp.float32, 'w_out': _jnp.float32, 'final_norm_w': _jnp.float32}
MOMENT_SCALE = {'norm_w': 1.308815e-01, 'w_in': 4.019845e-02, 's5_lambda_re': 8.344069e-04, 's5_lambda_im': 7.880025e-04, 's5_b_re': 5.525262e-04, 's5_b_im': 5.583841e-04, 's5_c_re': 1.106960e-03, 's5_c_im': 1.125221e-03, 's5_d': 1.869913e-02, 's5_log_step': 6.553393e-01, 's5_w_glu': 5.237449e-03, 'sgu_ln_w': 2.220113e-02, 'sgu_ln_b': 2.278997e-02, 'sgu_w': 1.626101e-02, 'sgu_b': 2.340687e-02, 'm2_conv_w': 6.343152e-02, 'm2_conv_b': 8.003949e-02, 'm2_dt_bias': 1.934907e-01, 'm2_a_log': 2.628183e-01, 'm2_d': 4.095495e-01, 'm2_norm_w': 8.421434e-02, 'sc_conv_w': 4.893299e-02, 'merge_b': 1.461068e-02, 'w_branch': 3.747250e-02, 'w_out': 7.468005e-02, 'final_norm_w': 1.601545e+01}


def _to_microbatches(a, axis):
    t = _jnp.moveaxis(a, axis, 0)
    t = t.reshape((N_MICROBATCH, t.shape[0] // N_MICROBATCH) + t.shape[1:])
    return _jnp.moveaxis(t, 1, axis + 1)


def setup_inputs(seed: int = 0) -> dict:
    inp = _fwd_setup_inputs(seed)
    key = _jax.random.fold_in(_jax.random.key(seed), 7919)
    shape, _ = _output_shape()
    out = dict(inp)
    out["loss_target"] = _jax.random.normal(_jax.random.fold_in(key, 0), shape, _jnp.float32)
    for i, name in enumerate(TWIN_WEIGHTS):
        w = inp[name].astype(_jnp.float32)
        if MOMENT_SCALE is None:
            s = _jnp.sqrt(_jnp.mean(_jnp.square(w)) + 1e-30)
        else:
            s = MOMENT_SCALE[name]
        km, kv = _jax.random.split(_jax.random.fold_in(key, i + 1))
        out[name] = w
        out["m_" + name] = s * _jax.random.normal(km, w.shape, _jnp.float32)
        out["v_" + name] = (s * s) * _jax.random.uniform(kv, w.shape, _jnp.float32, 0.5, 1.5)
    if N_MICROBATCH > 1:
        for name, axis in PER_EXAMPLE_BATCH_AXIS.items():
            out[name] = _to_microbatches(out[name], axis)
    return {'x': out['x'], 'norm_w': out['norm_w'], 'w_in': out['w_in'], 's5_lambda_re': out['s5_lambda_re'], 's5_lambda_im': out['s5_lambda_im'], 's5_b_re': out['s5_b_re'], 's5_b_im': out['s5_b_im'], 's5_c_re': out['s5_c_re'], 's5_c_im': out['s5_c_im'], 's5_d': out['s5_d'], 's5_log_step': out['s5_log_step'], 's5_w_glu': out['s5_w_glu'], 'sgu_ln_w': out['sgu_ln_w'], 'sgu_ln_b': out['sgu_ln_b'], 'sgu_w': out['sgu_w'], 'sgu_b': out['sgu_b'], 'm2_conv_w': out['m2_conv_w'], 'm2_conv_b': out['m2_conv_b'], 'm2_dt_bias': out['m2_dt_bias'], 'm2_a_log': out['m2_a_log'], 'm2_d': out['m2_d'], 'm2_norm_w': out['m2_norm_w'], 'sc_conv_w': out['sc_conv_w'], 'merge_b': out['merge_b'], 'w_branch': out['w_branch'], 'w_out': out['w_out'], 'final_norm_w': out['final_norm_w'], 'loss_target': out['loss_target'], 'm_norm_w': out['m_norm_w'], 'm_w_in': out['m_w_in'], 'm_s5_lambda_re': out['m_s5_lambda_re'], 'm_s5_lambda_im': out['m_s5_lambda_im'], 'm_s5_b_re': out['m_s5_b_re'], 'm_s5_b_im': out['m_s5_b_im'], 'm_s5_c_re': out['m_s5_c_re'], 'm_s5_c_im': out['m_s5_c_im'], 'm_s5_d': out['m_s5_d'], 'm_s5_log_step': out['m_s5_log_step'], 'm_s5_w_glu': out['m_s5_w_glu'], 'm_sgu_ln_w': out['m_sgu_ln_w'], 'm_sgu_ln_b': out['m_sgu_ln_b'], 'm_sgu_w': out['m_sgu_w'], 'm_sgu_b': out['m_sgu_b'], 'm_m2_conv_w': out['m_m2_conv_w'], 'm_m2_conv_b': out['m_m2_conv_b'], 'm_m2_dt_bias': out['m_m2_dt_bias'], 'm_m2_a_log': out['m_m2_a_log'], 'm_m2_d': out['m_m2_d'], 'm_m2_norm_w': out['m_m2_norm_w'], 'm_sc_conv_w': out['m_sc_conv_w'], 'm_merge_b': out['m_merge_b'], 'm_w_branch': out['m_w_branch'], 'm_w_out': out['m_w_out'], 'm_final_norm_w': out['m_final_norm_w'], 'v_norm_w': out['v_norm_w'], 'v_w_in': out['v_w_in'], 'v_s5_lambda_re': out['v_s5_lambda_re'], 'v_s5_lambda_im': out['v_s5_lambda_im'], 'v_s5_b_re': out['v_s5_b_re'], 'v_s5_b_im': out['v_s5_b_im'], 'v_s5_c_re': out['v_s5_c_re'], 'v_s5_c_im': out['v_s5_c_im'], 'v_s5_d': out['v_s5_d'], 'v_s5_log_step': out['v_s5_log_step'], 'v_s5_w_glu': out['v_s5_w_glu'], 'v_sgu_ln_w': out['v_sgu_ln_w'], 'v_sgu_ln_b': out['v_sgu_ln_b'], 'v_sgu_w': out['v_sgu_w'], 'v_sgu_b': out['v_sgu_b'], 'v_m2_conv_w': out['v_m2_conv_w'], 'v_m2_conv_b': out['v_m2_conv_b'], 'v_m2_dt_bias': out['v_m2_dt_bias'], 'v_m2_a_log': out['v_m2_a_log'], 'v_m2_d': out['v_m2_d'], 'v_m2_norm_w': out['v_m2_norm_w'], 'v_sc_conv_w': out['v_sc_conv_w'], 'v_merge_b': out['v_merge_b'], 'v_w_branch': out['v_w_branch'], 'v_w_out': out['v_w_out'], 'v_final_norm_w': out['v_final_norm_w']}


def _loss(weights, diff, rest, loss_target):
    with _jax.named_scope("forward"):
        args = {**rest, TWIN_DIFF_INPUT: diff, **{k: w.astype(_WEIGHT_DTYPES[k]) for k, w in weights.items()}}
        y = _forward(args)
    with _jax.named_scope("loss_head"):
        err = _jnp.square(y.astype(_jnp.float32) - loss_target)
        return 0.5 * _jnp.sum(_jnp.mean(err, axis=-1)) if err.ndim else 0.5 * err


def _adamw(w, g, m, v):
    m = ADAM_B1 * m + (1.0 - ADAM_B1) * g
    v = ADAM_B2 * v + (1.0 - ADAM_B2) * _jnp.square(g)
    m_hat = m / (1.0 - ADAM_B1 ** ADAM_STEP)
    v_hat = v / (1.0 - ADAM_B2 ** ADAM_STEP)
    delta = -ADAM_LR * (m_hat / (_jnp.sqrt(v_hat) + ADAM_EPS) + ADAM_WD * w)
    return delta, m, v


def reference(x, norm_w, w_in, s5_lambda_re, s5_lambda_im, s5_b_re, s5_b_im, s5_c_re, s5_c_im, s5_d, s5_log_step, s5_w_glu, sgu_ln_w, sgu_ln_b, sgu_w, sgu_b, m2_conv_w, m2_conv_b, m2_dt_bias, m2_a_log, m2_d, m2_norm_w, sc_conv_w, merge_b, w_branch, w_out, final_norm_w, loss_target, m_norm_w, m_w_in, m_s5_lambda_re, m_s5_lambda_im, m_s5_b_re, m_s5_b_im, m_s5_c_re, m_s5_c_im, m_s5_d, m_s5_log_step, m_s5_w_glu, m_sgu_ln_w, m_sgu_ln_b, m_sgu_w, m_sgu_b, m_m2_conv_w, m_m2_conv_b, m_m2_dt_bias, m_m2_a_log, m_m2_d, m_m2_norm_w, m_sc_conv_w, m_merge_b, m_w_branch, m_w_out, m_final_norm_w, v_norm_w, v_w_in, v_s5_lambda_re, v_s5_lambda_im, v_s5_b_re, v_s5_b_im, v_s5_c_re, v_s5_c_im, v_s5_d, v_s5_log_step, v_s5_w_glu, v_sgu_ln_w, v_sgu_ln_b, v_sgu_w, v_sgu_b, v_m2_conv_w, v_m2_conv_b, v_m2_dt_bias, v_m2_a_log, v_m2_d, v_m2_norm_w, v_sc_conv_w, v_merge_b, v_w_branch, v_w_out, v_final_norm_w):
    given = dict(x=x, norm_w=norm_w, w_in=w_in, s5_lambda_re=s5_lambda_re, s5_lambda_im=s5_lambda_im, s5_b_re=s5_b_re, s5_b_im=s5_b_im, s5_c_re=s5_c_re, s5_c_im=s5_c_im, s5_d=s5_d, s5_log_step=s5_log_step, s5_w_glu=s5_w_glu, sgu_ln_w=sgu_ln_w, sgu_ln_b=sgu_ln_b, sgu_w=sgu_w, sgu_b=sgu_b, m2_conv_w=m2_conv_w, m2_conv_b=m2_conv_b, m2_dt_bias=m2_dt_bias, m2_a_log=m2_a_log, m2_d=m2_d, m2_norm_w=m2_norm_w, sc_conv_w=sc_conv_w, merge_b=merge_b, w_branch=w_branch, w_out=w_out, final_norm_w=final_norm_w, loss_target=loss_target, m_norm_w=m_norm_w, m_w_in=m_w_in, m_s5_lambda_re=m_s5_lambda_re, m_s5_lambda_im=m_s5_lambda_im, m_s5_b_re=m_s5_b_re, m_s5_b_im=m_s5_b_im, m_s5_c_re=m_s5_c_re, m_s5_c_im=m_s5_c_im, m_s5_d=m_s5_d, m_s5_log_step=m_s5_log_step, m_s5_w_glu=m_s5_w_glu, m_sgu_ln_w=m_sgu_ln_w, m_sgu_ln_b=m_sgu_ln_b, m_sgu_w=m_sgu_w, m_sgu_b=m_sgu_b, m_m2_conv_w=m_m2_conv_w, m_m2_conv_b=m_m2_conv_b, m_m2_dt_bias=m_m2_dt_bias, m_m2_a_log=m_m2_a_log, m_m2_d=m_m2_d, m_m2_norm_w=m_m2_norm_w, m_sc_conv_w=m_sc_conv_w, m_merge_b=m_merge_b, m_w_branch=m_w_branch, m_w_out=m_w_out, m_final_norm_w=m_final_norm_w, v_norm_w=v_norm_w, v_w_in=v_w_in, v_s5_lambda_re=v_s5_lambda_re, v_s5_lambda_im=v_s5_lambda_im, v_s5_b_re=v_s5_b_re, v_s5_b_im=v_s5_b_im, v_s5_c_re=v_s5_c_re, v_s5_c_im=v_s5_c_im, v_s5_d=v_s5_d, v_s5_log_step=v_s5_log_step, v_s5_w_glu=v_s5_w_glu, v_sgu_ln_w=v_sgu_ln_w, v_sgu_ln_b=v_sgu_ln_b, v_sgu_w=v_sgu_w, v_sgu_b=v_sgu_b, v_m2_conv_w=v_m2_conv_w, v_m2_conv_b=v_m2_conv_b, v_m2_dt_bias=v_m2_dt_bias, v_m2_a_log=v_m2_a_log, v_m2_d=v_m2_d, v_m2_norm_w=v_m2_norm_w, v_sc_conv_w=v_sc_conv_w, v_merge_b=v_merge_b, v_w_branch=v_w_branch, v_w_out=v_w_out, v_final_norm_w=v_final_norm_w)
    weights = {n: given[n] for n in TWIN_WEIGHTS}
    shared = {n: given[n] for n in SHARED_INPUTS}
    per_example = {n: given[n] for n in ['x']}
    grad_fn = _jax.value_and_grad(_loss, argnums=(0, 1))

    def one_microbatch(ex, loss_target):
        ex = dict(ex)
        diff = ex.pop(TWIN_DIFF_INPUT)
        return grad_fn(weights, diff, {**shared, **ex}, loss_target)

    if N_MICROBATCH == 1:
        loss, (grad_w, grad_x) = one_microbatch(per_example, given["loss_target"])
    else:
        def body(carry, xs):
            loss_sum, grad_sum = carry
            l_k, (gw_k, gx_k) = one_microbatch(xs[0], xs[1])
            with _jax.named_scope("update"):
                return (loss_sum + l_k, _jax.tree.map(_jnp.add, grad_sum, gw_k)), gx_k

        init = (_jnp.zeros((), _jnp.float32), _jax.tree.map(_jnp.zeros_like, weights))
        (loss, grad_w), grad_x = _jax.lax.scan(body, init, (per_example, given["loss_target"]))
    with _jax.named_scope("update"):
        delta_w, new_m, new_v = {}, {}, {}
        for n in TWIN_WEIGHTS:
            delta_w[n], new_m[n], new_v[n] = _adamw(weights[n], grad_w[n], given["m_" + n], given["v_" + n])
    return (loss, grad_x, *[grad_w[n] for n in TWIN_WEIGHTS], *[delta_w[n] for n in TWIN_WEIGHTS],
            *[new_m[n] for n in TWIN_WEIGHTS], *[new_v[n] for n in TWIN_WEIGHTS])
```

```python
import functools
from typing import Any, Callable, NamedTuple

import numpy as np
import jax
import jax.numpy as jnp
from jax import lax
from jax.experimental import pallas as pl
from jax.experimental.pallas import tpu as pltpu

f32 = jnp.float32
bf16 = jnp.bfloat16

D_MODEL = 1024
BW = 512
N_BRANCH = 4
EPS = 1e-6
S5_GROUPS, S5_P, S5_N = 32, 16, 64
S5_NS = S5_GROUPS * S5_N
CHUNK = 128
M2_HEADS, M2_HEAD_DIM, M2_GROUPS, M2_STATE = 8, 64, 2, 128
IN_DIM = 10248
PW = 10368
PW_MAIN = 10240
LANES = 128
VMEM_LIMIT = 60 * 1024 * 1024

ADAM_LR, ADAM_B1, ADAM_B2, ADAM_EPS, ADAM_WD, ADAM_STEP = 0.001, 0.9, 0.999, 1e-08, 0.01, 10

C_MERGE = 0
C_SC = 4096
C_SGU = 6144
C_S5G = 7680
C_S5U = 8192
C_M2Z = 8704
C_XBC = 9216
C_DT = 10240


def _col_segments():
    segs = [(6152, 4096)]
    for j in range(4):
        segs += [(4104 + 128 * j, 128), (4616 + 128 * j, 128), (5128 + 128 * j, 128), (5640 + 128 * j, 128)]
    segs += [(1024, 1536), (512, 512), (0, 512), (2560, 512), (3072, 1024), (4096, 8)]
    return segs


def _to_kernel_cols(w):
    parts = [w[:, s:s + n] for s, n in _col_segments()]
    parts.append(jnp.zeros((w.shape[0], PW - IN_DIM), w.dtype))
    return jnp.concatenate(parts, axis=1)


def _from_kernel_cols(wp):
    out, pos = {}, 0
    for s, n in _col_segments():
        out[s] = wp[:, pos:pos + n]
        pos += n
    return jnp.concatenate([out[s] for s in sorted(out)], axis=1)


NN = ((1,), (0,))
NT = ((1,), (1,))
TN = ((0,), (0,))


def _bd(a, b, dims):
    return lax.dot_general(a.astype(bf16), b.astype(bf16), (dims, ((), ())), preferred_element_type=f32)


def _hd(a, b, dims):
    return lax.dot_general(a, b, (dims, ((), ())), precision=lax.Precision.HIGHEST, preferred_element_type=f32)


def _make_dots(raw):
    @jax.custom_vjp
    def nn(a, b):
        return raw(a, b, NN)
    nn.defvjp(lambda a, b: (raw(a, b, NN), (a, b)), lambda r, g: (raw(g, r[1], NT), raw(r[0], g, TN)))

    @jax.custom_vjp
    def nt(a, b):
        return raw(a, b, NT)
    nt.defvjp(lambda a, b: (raw(a, b, NT), (a, b)), lambda r, g: (raw(g, r[1], NN), raw(g, r[0], TN)))

    @jax.custom_vjp
    def tn(a, b):
        return raw(a, b, TN)
    tn.defvjp(lambda a, b: (raw(a, b, TN), (a, b)), lambda r, g: (raw(r[1], g, NT), raw(r[0], g, NN)))
    return nn, nt, tn


bdot, bdot_nt, bdot_tn = _make_dots(_bd)
hdot, hdot_nt, hdot_tn = _make_dots(_hd)


@jax.custom_vjp
def bdot_w(a, w, shadow):
    return _bd(a, w, NN)


bdot_w.defvjp(lambda a, w, s: (_bd(a, w, NN), (a, w)),
              lambda r, g: (_bd(g, r[1], NT), jnp.zeros_like(r[1]), _bd(r[0], g, TN)))


def _rows(shape):
    return lax.broadcasted_iota(jnp.int32, shape, 0)


def _cols(shape):
    return lax.broadcasted_iota(jnp.int32, shape, 1)


def _shift_down(x, s):
    return jnp.where(_rows(x.shape) < s, 0.0, pltpu.roll(x, s, 0))


def _shift_up(x, s):
    n = x.shape[0]
    return jnp.where(_rows(x.shape) >= n - s, 0.0, pltpu.roll(x, n - s, 0))


@functools.partial(jax.custom_vjp, nondiff_argnums=(1,))
def shift(x, s):
    return _shift_down(x, s) if s else x


shift.defvjp(lambda x, s: (shift(x, s), None), lambda s, _, g: (_shift_up(g, s) if s else g,))


def _row_of(w, k):
    return jnp.sum(jnp.where(_rows(w.shape) == k, w, 0.0), axis=0, keepdims=True)


def _lane_mask(width, lo, hi):
    c = _cols((1, width))
    return ((c >= lo) & (c < hi)).astype(f32)


def _expand(rows, width, per):
    return (_cols((rows, width)) // per == _rows((rows, width))).astype(f32)


class A(NamedTuple):
    arr: Any
    block: tuple
    imap: Callable
    shadow: bool = False


class O(NamedTuple):
    shape: tuple
    dtype: Any
    block: tuple
    imap: Callable
    alias: Any = None


class R(NamedTuple):
    arg: int
    out: int
    off: Any = None
    acc: bool = False


def _cparams(n_grid):
    return pltpu.CompilerParams(dimension_semantics=("arbitrary",) * n_grid, vmem_limit_bytes=VMEM_LIMIT)


def _ispec(block, imap, n, rev):
    if rev:
        return pl.BlockSpec(block, lambda i: imap(n - 1 - i))
    return pl.BlockSpec(block, imap)


def _load(ref, a):
    v = ref[...]
    if a.shadow:
        return (v, jnp.zeros(v.shape, f32))
    return v.astype(f32)


def _save_spec(shape, n, rev):
    nd = len(shape)
    return _ispec((None,) + tuple(shape), lambda i: (i,) + (0,) * nd, n, rev)


def block_fwd(name, f, n, args, outs, carries=()):
    n_in, n_out, n_c = len(args), len(outs), len(carries)

    def body(*refs):
        ins, out_r = refs[:n_in], refs[n_in:n_in + n_out]
        saves, cs = refs[n_in + n_out:n_in + n_out + n_c], refs[n_in + n_out + n_c:]
        if n_c:
            @pl.when(pl.program_id(0) == 0)
            def _():
                for c in cs:
                    c[...] = jnp.zeros(c.shape, f32)
        vals = [_load(r, a) for r, a in zip(ins, args)]
        cv = [c[...] for c in cs]
        for s, v in zip(saves, cv):
            s[...] = v
        res = f(*vals, *cv)
        for r, v in zip(out_r, res[:n_out]):
            r[...] = v.astype(r.dtype)
        for c, v in zip(cs, res[n_out:]):
            c[...] = v

    out_shape = [jax.ShapeDtypeStruct(o.shape, o.dtype) for o in outs]
    out_specs = [pl.BlockSpec(o.block, o.imap) for o in outs]
    for shp in carries:
        out_shape.append(jax.ShapeDtypeStruct((n,) + tuple(shp), f32))
        out_specs.append(_save_spec(shp, n, False))
    return pl.pallas_call(
        body, name=name, grid=(n,),
        in_specs=[pl.BlockSpec(a.block, a.imap) for a in args],
        out_specs=out_specs, out_shape=out_shape,
        scratch_shapes=[pltpu.VMEM(tuple(shp), f32) for shp in carries],
        compiler_params=_cparams(1),
    )(*[a.arr for a in args])


def block_bwd(name, f, n, args, cots, gouts, routes, saved=(), rev=False):
    n_in, n_cot, n_c, n_go = len(args), len(cots), len(saved), len(gouts)
    diff = []
    for r in routes:
        if r.arg not in diff:
            diff.append(r.arg)
    aliases = [(k, o.alias) for k, o in enumerate(gouts) if o.alias is not None]

    def body(*refs):
        ins = refs[:n_in]
        cot_r = refs[n_in:n_in + n_cot]
        sav_r = refs[n_in + n_cot:n_in + n_cot + n_c]
        base = n_in + n_cot + n_c + len(aliases)
        go_r = refs[base:base + n_go]
        dcs = refs[base + n_go:]
        step = pl.program_id(0)
        if n_c:
            @pl.when(step == 0)
            def _():
                for d in dcs:
                    d[...] = jnp.zeros(d.shape, f32)
        vals = [_load(r, a) for r, a in zip(ins, args)]
        cv = [s[...] for s in sav_r]
        nd = len(diff)

        def g(*dv):
            full = list(vals)
            for idx, v in zip(diff, dv[:nd]):
                full[idx] = (vals[idx][0], v) if args[idx].shadow else v
            return tuple(f(*full, *dv[nd:]))

        primals = [vals[i][1] if args[i].shadow else vals[i] for i in diff] + cv
        _, vjp_fn = jax.vjp(g, *primals)
        ct = tuple([r[...].astype(f32) for r in cot_r] + [d[...] for d in dcs])
        grads = vjp_fn(ct)
        for r in routes:
            gr = grads[diff.index(r.arg)]
            ref = go_r[r.out]
            if r.acc:
                @pl.when(step == 0)
                def _(ref=ref, gr=gr):
                    ref[...] = gr.astype(ref.dtype)

                @pl.when(step > 0)
                def _(ref=ref, gr=gr):
                    ref[...] += gr.astype(ref.dtype)
            elif r.off is None:
                ref[...] = gr.astype(ref.dtype)
            else:
                ref[:, r.off:r.off + gr.shape[1]] = gr.astype(ref.dtype)
        for d, gr in zip(dcs, grads[nd:]):
            d[...] = gr

    in_specs = [_ispec(a.block, a.imap, n, rev) for a in list(args) + list(cots)]
    in_specs += [_save_spec(s.shape[1:], n, rev) for s in saved]
    in_specs += [pl.BlockSpec(memory_space=pl.ANY) for _ in aliases]
    operands = [a.arr for a in list(args) + list(cots)] + list(saved) + [arr for _, arr in aliases]
    io_alias = {n_in + n_cot + n_c + j: k for j, (k, _) in enumerate(aliases)}
    return pl.pallas_call(
        body, name=name, grid=(n,),
        in_specs=in_specs,
        out_specs=[_ispec(o.block, o.imap, n, rev) for o in gouts],
        out_shape=[jax.ShapeDtypeStruct(o.shape, o.dtype) for o in gouts],
        scratch_shapes=[pltpu.VMEM(tuple(s.shape[1:]), f32) for s in saved],
        input_output_aliases=io_alias,
        compiler_params=_cparams(1),
    )(*operands)


class Op(NamedTuple):
    arr: Any
    row: int = 0
    col: int = 0


def mm(name, mode, pairs, m, n, tm, tn, out_dtype=f32, add=None, out=None, out_col=0):
    tm, tn = min(tm, m), min(tn, n)
    assert m % tm == 0 and n % tn == 0
    in_specs, operands = [], []
    for a, b, k, _ in pairs:
        if mode == TN:
            assert a.row % k == 0 and a.col % tm == 0
            in_specs.append(pl.BlockSpec((k, tm), lambda j, i, a=a, k=k: (a.row // k, i + a.col // tm)))
        else:
            assert a.col % k == 0 and a.row % tm == 0
            in_specs.append(pl.BlockSpec((tm, k), lambda j, i, a=a, k=k: (i + a.row // tm, a.col // k)))
        if mode == NT:
            assert b.col % k == 0 and b.row % tn == 0
            in_specs.append(pl.BlockSpec((tn, k), lambda j, i, b=b, k=k: (j + b.row // tn, b.col // k)))
        else:
            assert b.row % k == 0 and b.col % tn == 0
            in_specs.append(pl.BlockSpec((k, tn), lambda j, i, b=b, k=k: (b.row // k, j + b.col // tn)))
        operands += [a.arr, b.arr]
    n_p = len(pairs)
    if add is not None:
        assert add.col % tn == 0
        in_specs.append(pl.BlockSpec((tm, tn), lambda j, i: (i, j + add.col // tn)))
        operands.append(add.arr)
    io_alias = {}
    if out is not None:
        assert out_col % tn == 0
        in_specs.append(pl.BlockSpec(memory_space=pl.ANY))
        operands.append(out)
        io_alias = {len(operands) - 1: 0}
        out_shape = jax.ShapeDtypeStruct(out.shape, out.dtype)
    else:
        out_shape = jax.ShapeDtypeStruct((m, n), out_dtype)
    signs = [p[3] for p in pairs]

    def body(*refs):
        o = refs[-1]
        acc = None
        for p in range(n_p):
            t = _bd(refs[2 * p][...], refs[2 * p + 1][...], mode)
            t = t if signs[p] > 0 else -t
            acc = t if acc is None else acc + t
        if add is not None:
            acc = acc + refs[2 * n_p][...].astype(f32)
        o[...] = acc.astype(o.dtype)

    return pl.pallas_call(
        body, name=name, grid=(n // tn, m // tm),
        in_specs=in_specs,
        out_specs=pl.BlockSpec((tm, tn), lambda j, i: (i, j + out_col // tn)),
        out_shape=out_shape, input_output_aliases=io_alias,
        compiler_params=_cparams(2),
    )(*operands)


SCAN_LANES = 256
SUBL = 8


def _cmul(p, q):
    return (p[0] * q[0] - p[1] * q[1], p[0] * q[1] + p[1] * q[0])


def _powers(a):
    a2 = _cmul(a, a)
    a4 = _cmul(a2, a2)
    a6 = _cmul(a4, a2)
    return [a, a2, _cmul(a2, a), a4, _cmul(a4, a), a6, _cmul(a6, a), _cmul(a4, a4)]


def _table(pw, order, w):
    row = _rows((SUBL, w))
    re = sum(jnp.where(row == t, pw[k][0], 0.0) for t, k in enumerate(order))
    im = sum(jnp.where(row == t, pw[k][1], 0.0) for t, k in enumerate(order))
    return re, im


def _pick_row(x, t):
    return jnp.sum(jnp.where(_rows(x.shape) == t, x, 0.0), axis=0, keepdims=True)


def s5_scan_fwd(bu, a_re, a_im):
    seq, ns = bu.shape[0], a_re.shape[1]
    w, nb = SCAN_LANES, ns // SCAN_LANES

    def body(b_re, b_im, ar, ai, s_re, s_im):
        a = (ar[...], ai[...])
        pw = _powers(a)
        tab = _table(pw, list(range(SUBL)), w)
        row = _rows((SUBL, w))

        def step(i, carry):
            t0 = pl.multiple_of(i * SUBL, SUBL)
            x = (b_re[pl.ds(t0, SUBL), :], b_im[pl.ds(t0, SUBL), :])
            for d, k in ((1, 0), (2, 1), (4, 3)):
                sh = (jnp.where(row < d, 0.0, pltpu.roll(x[0], d, 0)), jnp.where(row < d, 0.0, pltpu.roll(x[1], d, 0)))
                t = _cmul(pw[k], sh)
                x = (x[0] + t[0], x[1] + t[1])
            t = _cmul(tab, carry)
            x = (x[0] + t[0], x[1] + t[1])
            s_re[pl.ds(t0, SUBL), :] = x[0]
            s_im[pl.ds(t0, SUBL), :] = x[1]
            return (_pick_row(x[0], SUBL - 1), _pick_row(x[1], SUBL - 1))

        z = jnp.zeros((1, w), f32)
        lax.fori_loop(0, seq // SUBL, step, (z, z))

    strip = lambda off: pl.BlockSpec((seq, w), lambda j: (0, j + off))
    lane = pl.BlockSpec((1, w), lambda j: (0, j))
    return pl.pallas_call(
        body, name="s5_scan_fwd", grid=(nb,),
        in_specs=[strip(0), strip(nb), lane, lane],
        out_specs=[strip(0), strip(0)],
        out_shape=[jax.ShapeDtypeStruct((seq, ns), f32)] * 2,
        compiler_params=_cparams(1),
    )(bu, bu, a_re, a_im)


def s5_scan_bwd(ds_re, ds_im, s_re, s_im, a_re, a_im):
    seq, ns = ds_re.shape
    w, nb = SCAN_LANES, ns // SCAN_LANES
    nblk = seq // SUBL

    def body(g_re, g_im, sr, si, ar, ai, l_re, l_im, da_re, da_im):
        a = (ar[...], -ai[...])
        pw = _powers(a)
        tab = _table(pw, [SUBL - 1 - t for t in range(SUBL)], w)
        row = _rows((SUBL, w))

        def step(kk, carry):
            c_re, c_im, acc_re, acc_im = carry
            i = nblk - 1 - kk
            t0 = pl.multiple_of(i * SUBL, SUBL)
            x = (g_re[pl.ds(t0, SUBL), :], g_im[pl.ds(t0, SUBL), :])
            for d, k in ((1, 0), (2, 1), (4, 3)):
                sh = (jnp.where(row >= SUBL - d, 0.0, pltpu.roll(x[0], SUBL - d, 0)),
                      jnp.where(row >= SUBL - d, 0.0, pltpu.roll(x[1], SUBL - d, 0)))
                t = _cmul(pw[k], sh)
                x = (x[0] + t[0], x[1] + t[1])
            t = _cmul(tab, (c_re, c_im))
            x = (x[0] + t[0], x[1] + t[1])
            l_re[pl.ds(t0, SUBL), :] = x[0]
            l_im[pl.ds(t0, SUBL), :] = x[1]
            tp = pl.multiple_of(jnp.maximum(i - 1, 0) * SUBL, SUBL)
            live = (i > 0).astype(f32)
            p_re = _pick_row(sr[pl.ds(tp, SUBL), :], SUBL - 1) * live
            p_im = _pick_row(si[pl.ds(tp, SUBL), :], SUBL - 1) * live
            sp_re = jnp.where(row == 0, p_re, pltpu.roll(sr[pl.ds(t0, SUBL), :], 1, 0))
            sp_im = jnp.where(row == 0, p_im, pltpu.roll(si[pl.ds(t0, SUBL), :], 1, 0))
            acc_re = acc_re + x[0] * sp_re + x[1] * sp_im
            acc_im = acc_im + x[1] * sp_re - x[0] * sp_im
            return (_pick_row(x[0], 0), _pick_row(x[1], 0), acc_re, acc_im)

        z1 = jnp.zeros((1, w), f32)
        z8 = jnp.zeros((SUBL, w), f32)
        _, _, acc_re, acc_im = lax.fori_loop(0, nblk, step, (z1, z1, z8, z8))
        da_re[...] = jnp.sum(acc_re, axis=0, keepdims=True)
        da_im[...] = jnp.sum(acc_im, axis=0, keepdims=True)

    strip = pl.BlockSpec((seq, w), lambda j: (0, j))
    lane = pl.BlockSpec((1, w), lambda j: (0, j))
    return pl.pallas_call(
        body, name="s5_scan_bwd", grid=(nb,),
        in_specs=[strip, strip, strip, strip, lane, lane],
        out_specs=[strip, strip, lane, lane],
        out_shape=[jax.ShapeDtypeStruct((seq, ns), f32)] * 2 + [jax.ShapeDtypeStruct((1, ns), f32)] * 2,
        compiler_params=_cparams(1),
    )(ds_re, ds_im, s_re, s_im, a_re, a_im)


def _rms(x, w):
    return x * lax.rsqrt(jnp.mean(x * x, axis=-1, keepdims=True) + EPS) * w


def f_rms(x, w):
    return (_rms(x, w),)


def f_rms_res(x, w):
    return (_rms(x, w), x)


def f_s5_prep(lam_re, lam_im, log_step, b_re, b_im):
    e = _expand(log_step.shape[1], S5_NS, S5_N)
    step = hdot(jnp.exp(log_step), e)
    mag = jnp.exp(lam_re * step)
    ab_re, ab_im = mag * jnp.cos(lam_im * step), mag * jnp.sin(lam_im * step)
    den = lam_re * lam_re + lam_im * lam_im
    nr = ab_re - 1.0
    coef_re = (nr * lam_re + ab_im * lam_im) / den
    coef_im = (ab_im * lam_re - nr * lam_im) / den
    return (ab_re, ab_im, coef_re * b_re - coef_im * b_im, coef_re * b_im + coef_im * b_re)


def f_s5_act(y_lin, u, gate, d, w_glu):
    y = jax.nn.gelu(y_lin + d * u)
    y = y * jax.nn.sigmoid(bdot_w(y, *w_glu))
    return (y * jax.nn.silu(gate),)


def f_sgu(u, v, gate, ln_w, ln_b, *rest):
    w_s, b_pad = rest[:8], rest[8]
    t = u.shape[0]
    u32, v32 = jax.nn.gelu(u), jax.nn.gelu(v)
    mu = jnp.mean(v32, axis=-1, keepdims=True)
    var = jnp.mean(jnp.square(v32 - mu), axis=-1, keepdims=True)
    vn = (v32 - mu) * lax.rsqrt(var + EPS) * ln_w + ln_b
    tri = _rows((t, t)) >= _cols((t, t))
    s = hdot_tn(b_pad, _expand(LANES, BW, BW // 8))
    for h in range(8):
        s = s + bdot(jnp.where(tri, w_s[h], 0.0), vn) * _lane_mask(BW, 64 * h, 64 * h + 64)
    return (u32 * s * jax.nn.silu(gate),)


def f_m2_conv(x, w, b):
    return (sum(_row_of(w, k) * shift(x, 3 - k) for k in range(4)) + b,)


def f_sc(bg, cg, h, gate, w):
    z = cg * h
    conv = sum(_row_of(w, k) * shift(z, 2 - k) for k in range(3))
    return (bg * conv * jax.nn.silu(gate),)


def f_m2(z, xc, b0, b1, c0, c1, dt_raw, dt_bias, a_log, d_par, norm_w, st):
    q = z.shape[0]
    x = jax.nn.silu(xc)
    bm, cm = (jax.nn.silu(b0), jax.nn.silu(b1)), (jax.nn.silu(c0), jax.nn.silu(c1))
    dt = jax.nn.softplus(dt_raw + dt_bias)
    da = dt * (-jnp.exp(a_log))
    tri = _rows((q, q)) >= _cols((q, q))
    acs = hdot(tri.astype(f32), da)
    e = _expand(LANES, BW, M2_HEAD_DIM)
    dt_f, acs_f = hdot(dt, e), hdot(acs, e)
    last = _rows((q, BW)) == q - 1
    alast_f = jnp.sum(jnp.where(last, acs_f, 0.0), axis=0, keepdims=True)
    xdt = x * dt_f
    xdec = xdt * jnp.exp(alast_f - acs_f)
    acs_t = acs.T
    st_new = st * jnp.exp(alast_f)
    y_diag, y_off = 0.0, 0.0
    for g in range(M2_GROUPS):
        gm = _lane_mask(BW, 256 * g, 256 * g + 256)
        cb = bdot_nt(cm[g], bm[g])
        st_new = st_new + bdot_tn(bm[g], xdec * gm)
        y_off = y_off + bdot(cm[g], st) * gm
        for hh in range(M2_HEADS // M2_GROUPS):
            h = g * (M2_HEADS // M2_GROUPS) + hh
            col = jnp.sum(jnp.where(_cols((q, LANES)) == h, acs, 0.0), axis=1, keepdims=True)
            row = jnp.sum(jnp.where(_rows((LANES, q)) == h, acs_t, 0.0), axis=0, keepdims=True)
            decay = jnp.exp(jnp.where(tri, col - row, -1e30))
            y_diag = y_diag + bdot(cb * decay, xdt) * _lane_mask(BW, 64 * h, 64 * h + 64)
    d_f = sum(jnp.sum(jnp.where(_cols((1, LANES)) == h, d_par, 0.0), axis=1, keepdims=True)
              * _lane_mask(BW, 64 * h, 64 * h + 64) for h in range(M2_HEADS))
    y = y_diag + y_off * jnp.exp(acs_f) + d_f * x
    y = y * jax.nn.silu(z)
    return (_rms(y, norm_w), st_new)


def f_gate_mix(*v):
    bo, lg, mb = v[0:4], v[4:8], v[8:12]
    return (sum(jax.nn.sigmoid(lg[k] + mb[k]) * bo[k] for k in range(N_BRANCH)),)


def _full(shape):
    nd = len(shape)
    return dict(block=tuple(shape), imap=lambda i: (0,) * nd)


def _param(arr, shadow=False):
    return A(arr, tuple(arr.shape), lambda i, nd=arr.ndim: (0,) * nd, shadow)


def _tb(arr, t, width, colblk):
    return A(arr, (t, width), lambda i: (i, colblk))


def _strip(arr, seq, colblk0, stride=1):
    return A(arr, (seq, LANES), lambda j: (0, colblk0 + stride * j))


def _block_diag(t, transpose):
    if transpose:
        t = jnp.swapaxes(t, 1, 2)
    g, a, b = t.shape
    eye = jnp.eye(g, dtype=bool)
    full = jnp.where(eye[:, None, :, None], t[:, :, None, :], jnp.zeros((), t.dtype))
    return full.reshape(g * a, g * b)


def _diag_blocks(m, g, transpose):
    a, b = m.shape[0] // g, m.shape[1] // g
    blocks = jnp.stack([m[k * a:(k + 1) * a, k * b:(k + 1) * b] for k in range(g)])
    return jnp.swapaxes(blocks, 1, 2) if transpose else blocks


def _s5_lane_params(p):
    lam_re = p["s5_lambda_re"].reshape(1, S5_NS)
    lam_im = p["s5_lambda_im"].reshape(1, S5_NS)
    log_step = jnp.pad(p["s5_log_step"].reshape(1, S5_GROUPS), ((0, 0), (0, LANES - S5_GROUPS)))
    to_lanes = lambda b: jnp.transpose(b, (2, 0, 1)).reshape(S5_P, S5_NS)
    return lam_re, lam_im, log_step, to_lanes(p["s5_b_re"]), to_lanes(p["s5_b_im"])


def _s5_prep_args(p):
    return [_param(v) for v in _s5_lane_params(p)]


def layer_fwd(x, p):
    seq = x.shape[0]
    nt = seq // CHUNK
    t2 = 256
    sv = {}
    hb = block_fwd("rms_in", f_rms, seq // t2, [_tb(x, t2, D_MODEL, 0), _param(p["norm_w"].reshape(1, D_MODEL))],
                   [O((seq, D_MODEL), bf16, (t2, D_MODEL), lambda i: (i, 0))])[0]
    proj = mm("mm_in", NN, [(Op(hb), Op(p["w_in"]), D_MODEL, 1)], seq, PW, 512, 1152)
    ab_re, ab_im, bb_re, bb_im = block_fwd(
        "s5_prep", f_s5_prep, 1, _s5_prep_args(p),
        [O((1, S5_NS), f32, (1, S5_NS), lambda i: (0, 0))] * 2 + [O((S5_P, S5_NS), f32, (S5_P, S5_NS), lambda i: (0, 0))] * 2)
    bd = lambda t: _block_diag(jnp.transpose(t.reshape(S5_P, S5_GROUPS, S5_N), (1, 0, 2)), False)
    bb_bd = jnp.concatenate([bd(bb_re), bd(bb_im)], axis=1).astype(bf16)
    c_bd = jnp.concatenate([_block_diag(p["s5_c_re"], True), _block_diag(p["s5_c_im"], True)], axis=0).astype(bf16)
    bu = mm("mm_bu", NN, [(Op(proj, 0, C_S5U), Op(bb_bd), BW, 1)], seq, 2 * S5_NS, 512, 1024)
    s_re, s_im = s5_scan_fwd(bu, ab_re, ab_im)
    y_lin = mm("mm_s5y", NN, [(Op(s_re), Op(c_bd), S5_NS, 1), (Op(s_im), Op(c_bd, S5_NS), S5_NS, -1)], seq, BW, 256, BW)
    s5_act_args = [_tb(y_lin, CHUNK, BW, 0), _tb(proj, CHUNK, BW, C_S5U // BW), _tb(proj, CHUNK, BW, C_S5G // BW),
                   _param(p["s5_d"].reshape(1, BW)), _param(p["s5_w_glu"], True)]
    out_bw = O((seq, BW), f32, (CHUNK, BW), lambda i: (i, 0))
    y_a = block_fwd("s5_act", f_s5_act, nt, s5_act_args, [out_bw])[0]
    y_b = block_fwd("sgu", f_sgu, nt, _sgu_args(proj, p), [out_bw])[0]
    xc = block_fwd("m2_conv", f_m2_conv, 8, _m2_conv_args(proj, p, seq),
                   [O((seq, 2 * BW), f32, (seq, LANES), lambda j: (0, j))])[0]
    y_c, st_saved = block_fwd("m2_ssd", f_m2, nt, _m2_args(proj, xc, p), [out_bw], carries=[(M2_STATE, BW)])
    y_d = block_fwd("sc", f_sc, 4, _sc_args(proj, p, seq), [O((seq, BW), f32, (seq, LANES), lambda j: (0, j))])[0]
    ys = [y_a, y_b, y_c, y_d]
    bo = [mm(f"mm_branch{k}", NN, [(Op(ys[k]), Op(p["w_branch"], k * BW, 0), BW, 1)], seq, D_MODEL, 512, D_MODEL)
          for k in range(N_BRANCH)]
    merged = block_fwd("gate_mix", f_gate_mix, seq // t2, _mix_args(bo, proj, p, t2),
                       [O((seq, D_MODEL), bf16, (t2, D_MODEL), lambda i: (i, 0))])[0]
    x_new = mm("mm_out", NN, [(Op(merged), Op(p["w_out"]), D_MODEL, 1)], seq, D_MODEL, 512, D_MODEL, add=Op(x))
    sv.update(x=x, hb=hb, proj=proj, ab=(ab_re, ab_im), bb_bd=bb_bd, c_bd=c_bd, s=(s_re, s_im), y_lin=y_lin,
              xc=xc, st=st_saved, ys=ys, bo=bo, merged=merged)
    return x_new, sv


def _sgu_args(proj, p):
    c0 = C_SGU // BW
    args = [_tb(proj, CHUNK, BW, c0), _tb(proj, CHUNK, BW, c0 + 1), _tb(proj, CHUNK, BW, c0 + 2),
            _param(p["sgu_ln_w"].reshape(1, BW)), _param(p["sgu_ln_b"].reshape(1, BW))]
    args += [A(p["sgu_w"], (None, CHUNK, CHUNK), lambda i, h=h: (h, 0, 0)) for h in range(8)]
    args.append(_param(jnp.pad(p["sgu_b"], ((0, LANES - 8), (0, 0)))))
    return args


def _m2_conv_args(proj, p, seq):
    return [_strip(proj, seq, C_XBC // LANES), A(p["m2_conv_w"], (4, LANES), lambda j: (0, j)),
            A(p["m2_conv_b"].reshape(1, 2 * BW), (1, LANES), lambda j: (0, j))]


def _pad_lanes(v):
    return jnp.pad(v.reshape(1, -1), ((0, 0), (0, LANES - v.size)))


def _m2_args(proj, xc, p):
    args = [_tb(proj, CHUNK, BW, C_M2Z // BW), _tb(xc, CHUNK, BW, 0)]
    args += [_tb(xc, CHUNK, LANES, 4 + k) for k in range(4)]
    args.append(_tb(proj, CHUNK, LANES, C_DT // LANES))
    args += [_param(_pad_lanes(p["m2_dt_bias"])), _param(_pad_lanes(p["m2_a_log"])), _param(_pad_lanes(p["m2_d"])),
             _param(p["m2_norm_w"].reshape(1, BW))]
    return args


def _sc_args(proj, p, seq):
    c0 = C_SC // LANES
    return [_strip(proj, seq, c0 + k, 4) for k in range(4)] + [A(p["sc_conv_w"], (3, LANES), lambda j: (0, j))]


def _mix_args(bo, proj, p, t):
    args = [_tb(bo[k], t, D_MODEL, 0) for k in range(N_BRANCH)]
    args += [_tb(proj, t, D_MODEL, k) for k in range(N_BRANCH)]
    mb = p["merge_b"].reshape(N_BRANCH, 1, D_MODEL)
    args += [A(mb, (None, 1, D_MODEL), lambda i, k=k: (k, 0, 0)) for k in range(N_BRANCH)]
    return args


def layer_bwd(d_out, p, sv):
    seq = d_out.shape[0]
    nt = seq // CHUNK
    t2 = 256
    proj, ys, bo = sv["proj"], sv["ys"], sv["bo"]
    g = {}
    acc = lambda shape: O(tuple(shape), f32, tuple(shape), lambda i, nd=len(shape): (0,) * nd)
    d_merged = mm("mm_out_dx", NT, [(Op(d_out), Op(p["w_out"]), D_MODEL, 1)], seq, D_MODEL, 512, D_MODEL, out_dtype=f32)
    g["w_out"] = mm("mm_out_dw", TN, [(Op(sv["merged"]), Op(d_out), seq, 1)], D_MODEL, D_MODEL, 256, D_MODEL)
    gouts = [O((seq, D_MODEL), bf16, (t2, D_MODEL), lambda i: (i, 0)) for _ in range(N_BRANCH)]
    gouts.append(O((seq, PW_MAIN), bf16, (t2, N_BRANCH * D_MODEL), lambda i: (i, 0)))
    gouts += [acc((1, D_MODEL)) for _ in range(N_BRANCH)]
    routes = [R(k, k) for k in range(N_BRANCH)] + [R(4 + k, 4, k * D_MODEL) for k in range(N_BRANCH)]
    routes += [R(8 + k, 5 + k, acc=True) for k in range(N_BRANCH)]
    res = block_bwd("gate_mix_bwd", f_gate_mix, seq // t2, _mix_args(bo, proj, p, t2),
                    [_tb(d_merged, t2, D_MODEL, 0)], gouts, routes)
    dbo, dproj = res[:4], res[4]
    g["merge_b"] = jnp.concatenate(res[5:9], axis=0)
    dys = [mm(f"mm_branch{k}_dx", NT, [(Op(dbo[k]), Op(p["w_branch"], k * BW, 0), D_MODEL, 1)], seq, BW, 512, BW)
           for k in range(N_BRANCH)]
    g["w_branch"] = jnp.stack([mm(f"mm_branch{k}_dw", TN, [(Op(ys[k]), Op(dbo[k]), seq, 1)], BW, D_MODEL, 256, D_MODEL)
                               for k in range(N_BRANCH)])
    res = block_bwd("sc_bwd", f_sc, 4, _sc_args(proj, p, seq), [_strip(dys[3], seq, 0)],
                    [O((seq, PW_MAIN), bf16, (seq, 4 * LANES), lambda j: (0, C_SC // (4 * LANES) + j), alias=dproj),
                     O((3, BW), f32, (3, LANES), lambda j: (0, j))],
                    [R(k, 0, k * LANES) for k in range(4)] + [R(4, 1)])
    dproj, g["sc_conv_w"] = res
    res = block_bwd("m2_ssd_bwd", f_m2, nt, _m2_args(proj, sv["xc"], p), [_tb(dys[2], CHUNK, BW, 0)],
                    [O((seq, PW_MAIN), bf16, (CHUNK, BW), lambda i: (i, C_M2Z // BW), alias=dproj),
                     O((seq, 2 * BW), f32, (CHUNK, 2 * BW), lambda i: (i, 0)),
                     O((seq, LANES), bf16, (CHUNK, LANES), lambda i: (i, 0)),
                     acc((1, LANES)), acc((1, LANES)), acc((1, LANES)), acc((1, BW))],
                    [R(0, 0), R(1, 1, 0)] + [R(2 + k, 1, BW + k * LANES) for k in range(4)] + [R(6, 2)]
                    + [R(7, 3, acc=True), R(8, 4, acc=True), R(9, 5, acc=True), R(10, 6, acc=True)],
                    saved=[sv["st"]], rev=True)
    dproj, dxc, d_dt = res[0], res[1], res[2]
    g["m2_dt_bias"], g["m2_a_log"], g["m2_d"] = (r[0, :M2_HEADS] for r in res[3:6])
    g["m2_norm_w"] = res[6].reshape(BW)
    res = block_bwd("m2_conv_bwd", f_m2_conv, 8, _m2_conv_args(proj, p, seq), [_strip(dxc, seq, 0)],
                    [O((seq, PW_MAIN), bf16, (seq, LANES), lambda j: (0, C_XBC // LANES + j), alias=dproj),
                     O((4, 2 * BW), f32, (4, LANES), lambda j: (0, j)), O((1, 2 * BW), f32, (1, LANES), lambda j: (0, j))],
                    [R(0, 0), R(1, 1), R(2, 2)])
    dproj, g["m2_conv_w"], cb = res
    g["m2_conv_b"] = cb.reshape(2 * BW)
    res = block_bwd("sgu_bwd", f_sgu, nt, _sgu_args(proj, p), [_tb(dys[1], CHUNK, BW, 0)],
                    [O((seq, PW_MAIN), bf16, (CHUNK, 3 * BW), lambda i: (i, C_SGU // (3 * BW)), alias=dproj),
                     acc((1, BW)), acc((1, BW))] + [acc((CHUNK, CHUNK)) for _ in range(8)] + [acc((LANES, CHUNK))],
                    [R(0, 0, 0), R(1, 0, BW), R(2, 0, 2 * BW), R(3, 1, acc=True), R(4, 2, acc=True)]
                    + [R(5 + h, 3 + h, acc=True) for h in range(8)] + [R(13, 11, acc=True)])
    dproj = res[0]
    g["sgu_ln_w"], g["sgu_ln_b"] = res[1].reshape(BW), res[2].reshape(BW)
    g["sgu_w"] = jnp.stack(res[3:11])
    g["sgu_b"] = res[11][:8]
    y_lin, (s_re, s_im), (ab_re, ab_im) = sv["y_lin"], sv["s"], sv["ab"]
    s5_act_args = [_tb(y_lin, CHUNK, BW, 0), _tb(proj, CHUNK, BW, C_S5U // BW), _tb(proj, CHUNK, BW, C_S5G // BW),
                   _param(p["s5_d"].reshape(1, BW)), _param(p["s5_w_glu"], True)]
    res = block_bwd("s5_act_bwd", f_s5_act, nt, s5_act_args, [_tb(dys[0], CHUNK, BW, 0)],
                    [O((seq, BW), bf16, (CHUNK, BW), lambda i: (i, 0)), O((seq, BW), f32, (CHUNK, BW), lambda i: (i, 0)),
                     O((seq, PW_MAIN), bf16, (CHUNK, BW), lambda i: (i, C_S5G // BW), alias=dproj),
                     acc((1, BW)), acc((BW, BW))],
                    [R(0, 0), R(1, 1), R(2, 2), R(3, 3, acc=True), R(4, 4, acc=True)])
    dy_lin, du1, dproj = res[0], res[1], res[2]
    g["s5_d"] = res[3].reshape(S5_GROUPS, S5_P)
    g["s5_w_glu"] = res[4]
    c_bd, bb_bd = sv["c_bd"], sv["bb_bd"]
    ds_re = mm("mm_s5y_dre", NT, [(Op(dy_lin), Op(c_bd), BW, 1)], seq, S5_NS, 512, 1024)
    ds_im = mm("mm_s5y_dim", NT, [(Op(dy_lin), Op(c_bd, S5_NS), BW, -1)], seq, S5_NS, 512, 1024)
    dc_re = mm("mm_s5y_dcre", TN, [(Op(s_re), Op(dy_lin), seq, 1)], S5_NS, BW, 512, BW)
    dc_im = mm("mm_s5y_dcim", TN, [(Op(s_im), Op(dy_lin), seq, -1)], S5_NS, BW, 512, BW)
    g["s5_c_re"], g["s5_c_im"] = _diag_blocks(dc_re, S5_GROUPS, True), _diag_blocks(dc_im, S5_GROUPS, True)
    l_re, l_im, da_re, da_im = s5_scan_bwd(ds_re, ds_im, s_re, s_im, ab_re, ab_im)
    dproj = mm("mm_bu_dx", NT, [(Op(l_re), Op(bb_bd), S5_NS, 1), (Op(l_im), Op(bb_bd, 0, S5_NS), S5_NS, 1)],
               seq, BW, 512, BW, add=Op(du1), out=dproj, out_col=C_S5U)
    dbb = [mm(f"mm_bu_dw{n}", TN, [(Op(proj, 0, C_S5U), Op(l), seq, 1)], BW, S5_NS, 256, 1024)
           for n, l in (("re", l_re), ("im", l_im))]
    from_bd = lambda m: jnp.transpose(_diag_blocks(m, S5_GROUPS, False), (1, 0, 2)).reshape(S5_P, S5_NS)
    gouts = [acc((1, S5_NS)), acc((1, S5_NS)), acc((1, LANES)), acc((S5_P, S5_NS)), acc((S5_P, S5_NS))]
    res = block_bwd("s5_prep_bwd", f_s5_prep, 1, _s5_prep_args(p),
                    [_param(da_re), _param(da_im), _param(from_bd(dbb[0])), _param(from_bd(dbb[1]))],
                    gouts, [R(k, k, acc=True) for k in range(5)])
    g["s5_lambda_re"], g["s5_lambda_im"] = res[0].reshape(S5_GROUPS, S5_N), res[1].reshape(S5_GROUPS, S5_N)
    g["s5_log_step"] = res[2][0, :S5_GROUPS]
    from_lanes = lambda b: jnp.transpose(b.reshape(S5_P, S5_GROUPS, S5_N), (1, 2, 0))
    g["s5_b_re"], g["s5_b_im"] = from_lanes(res[3]), from_lanes(res[4])
    hb, w_in = sv["hb"], p["w_in"]
    dh = mm("mm_in_dx", NT, [(Op(dproj), Op(w_in), PW_MAIN, 1), (Op(d_dt), Op(w_in, 0, PW_MAIN), LANES, 1)],
            seq, D_MODEL, 256, 256)
    g["w_in"] = jnp.concatenate(
        [mm("mm_in_dw", TN, [(Op(hb), Op(dproj), seq, 1)], D_MODEL, PW_MAIN, 256, 1024),
         mm("mm_in_dwdt", TN, [(Op(hb), Op(d_dt), seq, 1)], D_MODEL, LANES, 256, LANES)], axis=1)
    dx, dnw = block_bwd("rms_in_bwd", f_rms_res, seq // t2, [_tb(sv["x"], t2, D_MODEL, 0), _param(p["norm_w"].reshape(1, D_MODEL))],
                        [_tb(dh, t2, D_MODEL, 0), _tb(d_out, t2, D_MODEL, 0)],
                        [O((seq, D_MODEL), f32, (t2, D_MODEL), lambda i: (i, 0)), acc((1, D_MODEL))],
                        [R(0, 0), R(1, 1, acc=True)])
    g["norm_w"] = dnw.reshape(D_MODEL)
    return dx, g


def loss_head(x, w, target):
    seq = x.shape[0]
    t = 256

    def body(x_ref, w_ref, t_ref, loss_ref, dx_ref, dw_ref):
        step = pl.program_id(0)

        def f(xv, wv):
            err = _rms(xv, wv) - t_ref[...]
            return 0.5 * jnp.sum(jnp.mean(err * err, axis=-1, keepdims=True), axis=0, keepdims=True)

        val, vjp_fn = jax.vjp(f, x_ref[...], w_ref[...])
        dx, dw = vjp_fn(jnp.ones((1, 1), f32))
        dx_ref[...] = dx

        @pl.when(step == 0)
        def _():
            loss_ref[...] = jnp.broadcast_to(val, loss_ref.shape)
            dw_ref[...] = dw

        @pl.when(step > 0)
        def _():
            loss_ref[...] += jnp.broadcast_to(val, loss_ref.shape)
            dw_ref[...] += dw

    blk = pl.BlockSpec((t, D_MODEL), lambda i: (i, 0))
    row = pl.BlockSpec((1, D_MODEL), lambda i: (0, 0))
    return pl.pallas_call(
        body, name="loss_head", grid=(seq // t,),
        in_specs=[blk, row, blk],
        out_specs=[pl.BlockSpec((1, LANES), lambda i: (0, 0)), blk, row],
        out_shape=[jax.ShapeDtypeStruct((1, LANES), f32), jax.ShapeDtypeStruct((seq, D_MODEL), f32),
                   jax.ShapeDtypeStruct((1, D_MODEL), f32)],
        compiler_params=_cparams(1),
    )(x, w.reshape(1, D_MODEL), target)


LAYER_KEYS = ("norm_w", "w_in", "s5_lambda_re", "s5_lambda_im", "s5_b_re", "s5_b_im", "s5_c_re", "s5_c_im", "s5_d",
              "s5_log_step", "s5_w_glu", "sgu_ln_w", "sgu_ln_b", "sgu_w", "sgu_b", "m2_conv_w", "m2_conv_b",
              "m2_dt_bias", "m2_a_log", "m2_d", "m2_norm_w", "sc_conv_w", "merge_b", "w_branch", "w_out")


def local_step(x, target, layers, final_norm_w):
    saved = []
    for p in layers:
        x, sv = layer_fwd(x, p)
        saved.append(sv)
    loss, dx, dfw = loss_head(x, final_norm_w, target)
    grads = []
    for p, sv in zip(reversed(layers), reversed(saved)):
        dx, g = layer_bwd(dx, p, sv)
        grads.append(g)
    return loss[0, 0], dx, grads[::-1], dfw.reshape(D_MODEL)


MESH = pl.DeviceIdType.MESH
ANY = pl.BlockSpec(memory_space=pl.ANY)


def _me():
    return lax.axis_index("x"), lax.axis_index("y"), lax.axis_index("c")


def _other_chips(x, y):
    return [(1 - x, y), (x, 1 - y), (1 - x, 1 - y)]


def _rcopy(src, dst, send, recv, dev):
    return pltpu.make_async_remote_copy(src_ref=src, dst_ref=dst, send_sem=send, recv_sem=recv,
                                        device_id=dev, device_id_type=MESH)


def _comm_call(name, body, arrs, out_shape, n_remote, n_local):
    n = len(arrs)
    return pl.pallas_call(
        body, name=name, in_specs=[ANY] * n, out_specs=[ANY] * len(out_shape), out_shape=out_shape,
        scratch_shapes=[pltpu.SemaphoreType.DMA((n, n_remote)), pltpu.SemaphoreType.DMA((n, n_remote)),
                        pltpu.SemaphoreType.DMA((n, n_local))],
        compiler_params=pltpu.CompilerParams(has_side_effects=True),
    )(*arrs)


def gather_chips(name, arrs):
    n = len(arrs)

    def body(*refs):
        ins, outs = refs[:n], refs[n:2 * n]
        send, recv, loc = refs[2 * n:]
        x, y, c = _me()
        jme = 2 * x + y
        chips = _other_chips(x, y)
        local = [pltpu.make_async_copy(ins[a], outs[a].at[jme], loc.at[a, 0]) for a in range(n)]
        for cp in local:
            cp.start()
        first = [_rcopy(ins[a].at[c], outs[a].at[jme, c], send.at[a, k], recv.at[a, k], (cx, cy, c))
                 for a in range(n) for k, (cx, cy) in enumerate(chips)]
        for cp in first:
            cp.start()
        passed = []
        for k, (cx, cy) in enumerate(chips):
            jk = 2 * cx + cy
            for a in range(n):
                _rcopy(ins[a].at[c], outs[a].at[jk, c], send.at[a, k], recv.at[a, k], (cx, cy, c)).wait_recv()
                fw = _rcopy(outs[a].at[jk, c], outs[a].at[jk, c], send.at[a, 3 + k], recv.at[a, 3 + k], (x, y, 1 - c))
                fw.start()
                passed.append(fw)
        for k, (cx, cy) in enumerate(chips):
            jk = 2 * cx + cy
            for a in range(n):
                _rcopy(outs[a].at[jk, 1 - c], outs[a].at[jk, 1 - c], send.at[a, 3 + k], recv.at[a, 3 + k],
                       (x, y, 1 - c)).wait_recv()
        for cp in first + passed:
            cp.wait_send()
        for cp in local:
            cp.wait()

    out_shape = [jax.ShapeDtypeStruct((4,) + a.shape, a.dtype) for a in arrs]
    return _comm_call(name, body, arrs, out_shape, 6, 1)


def swap_halves(name, arrs):
    n = len(arrs)

    def body(*refs):
        ins, outs = refs[:n], refs[n:3 * n]
        send, recv, loc = refs[3 * n:]
        x, y, c = _me()
        local = [pltpu.make_async_copy(ins[a].at[c], outs[2 * a], loc.at[a, 0]) for a in range(n)]
        remote = [_rcopy(ins[a].at[1 - c], outs[2 * a + 1], send.at[a, 0], recv.at[a, 0], (x, y, 1 - c)) for a in range(n)]
        for cp in local + remote:
            cp.start()
        for cp in remote:
            cp.wait()
        for cp in local:
            cp.wait()

    out_shape = []
    for a in arrs:
        out_shape += [jax.ShapeDtypeStruct(a.shape[1:], a.dtype)] * 2
    res = _comm_call(name, body, arrs, out_shape, 1, 1)
    return [(res[2 * a], res[2 * a + 1]) for a in range(n)]


def exchange_chips(name, arrs):
    n = len(arrs)

    def body(*refs):
        ins, outs = refs[:n], refs[n:3 * n]
        send, recv, loc = refs[3 * n:]
        x, y, c = _me()
        jme = 2 * x + y
        local = [pltpu.make_async_copy(ins[a].at[jme], outs[2 * a], loc.at[a, 0]) for a in range(n)]
        remote = [_rcopy(ins[a].at[2 * cx + cy], outs[2 * a + 1].at[k], send.at[a, k], recv.at[a, k], (cx, cy, c))
                  for a in range(n) for k, (cx, cy) in enumerate(_other_chips(x, y))]
        for cp in local + remote:
            cp.start()
        for cp in remote:
            cp.wait()
        for cp in local:
            cp.wait()

    out_shape = []
    for a in arrs:
        out_shape += [jax.ShapeDtypeStruct(a.shape[1:], a.dtype), jax.ShapeDtypeStruct((3,) + a.shape[1:], a.dtype)]
    res = _comm_call(name, body, arrs, out_shape, 3, 1)
    return [(res[2 * a], res[2 * a + 1]) for a in range(n)]


def gather_cores(name, arrs):
    n = len(arrs)

    def body(*refs):
        ins, outs = refs[:n], refs[n:2 * n]
        send, recv, loc = refs[2 * n:]
        x, y, c = _me()
        local = [pltpu.make_async_copy(ins[a], outs[a].at[c], loc.at[a, 0]) for a in range(n)]
        remote = [_rcopy(ins[a], outs[a].at[c], send.at[a, 0], recv.at[a, 0], (x, y, 1 - c)) for a in range(n)]
        for cp in local + remote:
            cp.start()
        for a in range(n):
            _rcopy(ins[a], outs[a].at[1 - c], send.at[a, 0], recv.at[a, 0], (x, y, 1 - c)).wait_recv()
        for cp in remote:
            cp.wait_send()
        for cp in local:
            cp.wait()

    out_shape = [jax.ShapeDtypeStruct((2,) + a.shape, a.dtype) for a in arrs]
    return _comm_call(name, body, arrs, out_shape, 1, 1)


ROW_BLOCK = 256


def esum(name, terms, rows, width, out_dtype):
    def body(*refs):
        acc = refs[0][...].astype(f32)
        for r in refs[1:-1]:
            acc = acc + r[...].astype(f32)
        refs[-1][...] = acc.astype(out_dtype)

    specs = []
    for arr, lead in terms:
        if lead is None:
            specs.append(pl.BlockSpec((ROW_BLOCK, width), lambda i: (i, 0)))
        else:
            specs.append(pl.BlockSpec((None, ROW_BLOCK, width), lambda i, lead=lead: (lead, i, 0)))
    return pl.pallas_call(
        body, name=name, grid=(rows // ROW_BLOCK,), in_specs=specs,
        out_specs=pl.BlockSpec((ROW_BLOCK, width), lambda i: (i, 0)),
        out_shape=jax.ShapeDtypeStruct((rows, width), out_dtype), compiler_params=_cparams(1),
    )(*[t[0] for t in terms])


def reduce_to_shards(packs):
    tags = [str(k) for k in range(len(packs))]
    pairs = swap_halves("rs_swap", packs)
    t1 = []
    for tag, p, (mine, theirs) in zip(tags, packs, pairs):
        _, _, h, w = p.shape
        t1.append(esum("rs_add_cores" + tag, [(mine.reshape(4 * h, w), None), (theirs.reshape(4 * h, w), None)],
                       4 * h, w, p.dtype).reshape(4, h, w))
    landed = exchange_chips("rs_exchange", t1)
    red = []
    for tag, p, (own, got) in zip(tags, packs, landed):
        _, _, h, w = p.shape
        red.append(esum("rs_add_chips" + tag, [(own, None), (got, 0), (got, 1), (got, 2)], h, w, f32))
    return gather_cores("rs_gather", red)


def adamw(name, w, g, m, v):
    rows, width = w.shape
    tr = ROW_BLOCK if rows % ROW_BLOCK == 0 else rows

    def body(w_ref, g_ref, m_ref, v_ref, d_ref, nm_ref, nv_ref):
        gv = g_ref[...]
        nm = ADAM_B1 * m_ref[...] + (1.0 - ADAM_B1) * gv
        nv = ADAM_B2 * v_ref[...] + (1.0 - ADAM_B2) * jnp.square(gv)
        m_hat = nm / (1.0 - ADAM_B1 ** ADAM_STEP)
        v_hat = nv / (1.0 - ADAM_B2 ** ADAM_STEP)
        d_ref[...] = -ADAM_LR * (m_hat / (jnp.sqrt(v_hat) + ADAM_EPS) + ADAM_WD * w_ref[...])
        nm_ref[...] = nm
        nv_ref[...] = nv

    spec = pl.BlockSpec((tr, width), lambda i: (i, 0))
    return pl.pallas_call(
        body, name=name, grid=(rows // tr,), in_specs=[spec] * 4, out_specs=[spec] * 3,
        out_shape=[jax.ShapeDtypeStruct((rows, width), f32)] * 3, compiler_params=_cparams(1),
    )(w, g, m, v)


PACK_W = 1024


def _pack(parts, dtype, halves):
    flat = jnp.concatenate([p.reshape(-1).astype(dtype) for p in parts])
    per = halves * ROW_BLOCK * PACK_W
    total = -(-flat.size // per) * per
    flat = jnp.pad(flat, (0, total - flat.size))
    return flat.reshape(halves, total // (halves * PACK_W), PACK_W)


def _unpack(flat, shapes):
    out, pos = [], 0
    for s in shapes:
        n = int(np.prod(s))
        out.append(flat[pos:pos + n].reshape(s))
        pos += n
    return out


SHARDED_BIG = ("w_in", "w_branch", "w_out", "s5_w_glu")
SHARDED_SMALL = ("m2_conv_w", "sc_conv_w", "merge_b")
SHARD_AXIS = {"w_in": 2, "w_branch": 3, "w_out": 1, "s5_w_glu": 1, "m2_conv_w": 2, "sc_conv_w": 2, "merge_b": 2}
REPLICATED = ("norm_w", "s5_lambda_re", "s5_lambda_im", "s5_b_re", "s5_b_im", "s5_c_re", "s5_c_im", "s5_d", "s5_log_step",
              "sgu_ln_w", "sgu_ln_b", "sgu_w", "sgu_b", "m2_conv_b", "m2_dt_bias", "m2_a_log", "m2_d", "m2_norm_w")
WEIGHTS = ("norm_w", "w_in", "s5_lambda_re", "s5_lambda_im", "s5_b_re", "s5_b_im", "s5_c_re", "s5_c_im", "s5_d",
           "s5_log_step", "s5_w_glu", "sgu_ln_w", "sgu_ln_b", "sgu_w", "sgu_b", "m2_conv_w", "m2_conv_b", "m2_dt_bias",
           "m2_a_log", "m2_d", "m2_norm_w", "sc_conv_w", "merge_b", "w_branch", "w_out", "final_norm_w")
N_LAYERS = 2


def _gather_weights(w):
    big = _pack([w[k] for k in SHARDED_BIG], bf16, 2)
    small = _pack([w[k] for k in SHARDED_SMALL], f32, 2)
    big_all, small_all = gather_chips("ag_weights", [big, small])
    full = {}
    for names, allv in ((SHARDED_BIG, big_all), (SHARDED_SMALL, small_all)):
        per_chip = [_unpack(allv[j].reshape(-1), [w[k].shape for k in names]) for j in range(4)]
        for i, k in enumerate(names):
            full[k] = jnp.concatenate([per_chip[j][i] for j in range(4)], axis=SHARD_AXIS[k])
    layers = []
    for i in range(N_LAYERS):
        p = {k: w[k][i] for k in REPLICATED}
        p.update({k: full[k][i] for k in SHARDED_SMALL})
        p["w_in"] = _to_kernel_cols(full["w_in"][i])
        p["w_branch"] = full["w_branch"][i].reshape(N_BRANCH * BW, D_MODEL)
        p["w_out"] = full["w_out"][i]
        p["s5_w_glu"] = full["s5_w_glu"][i]
        layers.append(p)
    return layers


def _reduce_grads(grads, d_final, w):
    for g in grads:
        g["w_in"] = _from_kernel_cols(g["w_in"])
        g["w_branch"] = g["w_branch"].reshape(N_BRANCH, BW, D_MODEL)
    full = {k: jnp.stack([g[k] for g in grads]) for k in LAYER_KEYS}
    rep = jnp.concatenate([full[k].reshape(-1) for k in REPLICATED] + [d_final.reshape(-1)])
    quarter = -(-rep.size // (4 * 2 * ROW_BLOCK * PACK_W)) * (2 * ROW_BLOCK * PACK_W)
    rep = jnp.pad(rep, (0, 4 * quarter - rep.size))

    def shard(k, j):
        n = w[k].shape[SHARD_AXIS[k]]
        return lax.slice_in_dim(full[k], j * n, (j + 1) * n, axis=SHARD_AXIS[k])

    big = jnp.stack([_pack([shard(k, j) for k in SHARDED_BIG], bf16, 2) for j in range(4)], axis=1)
    small = jnp.stack([_pack([shard(k, j) for k in SHARDED_SMALL] + [rep[j * quarter:(j + 1) * quarter]], f32, 2)
                       for j in range(4)], axis=1)
    big_red, small_red = reduce_to_shards([big, small])
    out = dict(zip(SHARDED_BIG, _unpack(big_red.reshape(-1), [w[k].shape for k in SHARDED_BIG])))
    small_flat = small_red.reshape(-1)
    n_small = sum(int(np.prod(w[k].shape)) for k in SHARDED_SMALL)
    out.update(zip(SHARDED_SMALL, _unpack(small_flat, [w[k].shape for k in SHARDED_SMALL])))
    mine = small_flat[n_small:n_small + quarter].reshape(2, quarter // (2 * PACK_W), PACK_W)
    rep_all = gather_chips("ag_small_grads", [mine])[0].reshape(-1)
    out.update(zip(REPLICATED + ("final_norm_w",), _unpack(rep_all, [w[k].shape for k in REPLICATED + ("final_norm_w",)])))
    return out


def _update(w, g, m, v):
    d, nm, nv = {}, {}, {}
    flat2 = lambda a: a.reshape(-1, a.shape[-1])
    res = adamw("adamw_w_in", *[flat2(t["w_in"]) for t in (w, g, m, v)])
    d["w_in"], nm["w_in"], nv["w_in"] = (r.reshape(w["w_in"].shape) for r in res)
    rest = [k for k in WEIGHTS if k != "w_in"]
    packs = [_pack([t[k] for k in rest], f32, 1)[0] for t in (w, g, m, v)]
    res = adamw("adamw_rest", *packs)
    for tgt, r in zip((d, nm, nv), res):
        tgt.update(zip(rest, _unpack(r.reshape(-1), [w[k].shape for k in rest])))
    return d, nm, nv


def kernel(x, norm_w, w_in, s5_lambda_re, s5_lambda_im, s5_b_re, s5_b_im, s5_c_re, s5_c_im, s5_d, s5_log_step, s5_w_glu, sgu_ln_w, sgu_ln_b, sgu_w, sgu_b, m2_conv_w, m2_conv_b, m2_dt_bias, m2_a_log, m2_d, m2_norm_w, sc_conv_w, merge_b, w_branch, w_out, final_norm_w, loss_target, m_norm_w, m_w_in, m_s5_lambda_re, m_s5_lambda_im, m_s5_b_re, m_s5_b_im, m_s5_c_re, m_s5_c_im, m_s5_d, m_s5_log_step, m_s5_w_glu, m_sgu_ln_w, m_sgu_ln_b, m_sgu_w, m_sgu_b, m_m2_conv_w, m_m2_conv_b, m_m2_dt_bias, m_m2_a_log, m_m2_d, m_m2_norm_w, m_sc_conv_w, m_merge_b, m_w_branch, m_w_out, m_final_norm_w, v_norm_w, v_w_in, v_s5_lambda_re, v_s5_lambda_im, v_s5_b_re, v_s5_b_im, v_s5_c_re, v_s5_c_im, v_s5_d, v_s5_log_step, v_s5_w_glu, v_sgu_ln_w, v_sgu_ln_b, v_sgu_w, v_sgu_b, v_m2_conv_w, v_m2_conv_b, v_m2_dt_bias, v_m2_a_log, v_m2_d, v_m2_norm_w, v_sc_conv_w, v_merge_b, v_w_branch, v_w_out, v_final_norm_w):
    given = dict(locals())
    w = {k: given[k] for k in WEIGHTS}
    m = {k: given["m_" + k] for k in WEIGHTS}
    v = {k: given["v_" + k] for k in WEIGHTS}
    layers = _gather_weights(w)
    loss, dx, grads, d_final = local_step(x[0], loss_target[0], layers, final_norm_w)
    loss = lax.psum(loss, ("x", "y", "c"))
    g = _reduce_grads(grads, d_final, w)
    d, nm, nv = _update(w, g, m, v)
    return (loss, dx[None], *[g[k] for k in WEIGHTS], *[d[k] for k in WEIGHTS],
            *[nm[k] for k in WEIGHTS], *[nv[k] for k in WEIGHTS])
```

```python
import functools
from typing import Any, Callable, NamedTuple

import numpy as np
import jax
import jax.numpy as jnp
from jax import lax
from jax.experimental import pallas as pl
from jax.experimental.pallas import tpu as pltpu

f32 = jnp.float32
bf16 = jnp.bfloat16

D_MODEL = 1024
BW = 512
N_BRANCH = 4
EPS = 1e-6
S5_GROUPS, S5_P, S5_N = 32, 16, 64
S5_NS = S5_GROUPS * S5_N
CHUNK = 128
M2_HEADS, M2_HEAD_DIM, M2_GROUPS, M2_STATE = 8, 64, 2, 128
IN_DIM = 10248
PW = 10368
PW_MAIN = 10240
LANES = 128
VMEM_LIMIT = 60 * 1024 * 1024

ADAM_LR, ADAM_B1, ADAM_B2, ADAM_EPS, ADAM_WD, ADAM_STEP = 0.001, 0.9, 0.999, 1e-08, 0.01, 10

C_MERGE = 0
C_SC = 4096
C_SGU = 6144
C_S5G = 7680
C_S5U = 8192
C_M2Z = 8704
C_XBC = 9216
C_DT = 10240


def _col_segments():
    segs = [(6152, 4096)]
    for j in range(4):
        segs += [(4104 + 128 * j, 128), (4616 + 128 * j, 128), (5128 + 128 * j, 128), (5640 + 128 * j, 128)]
    segs += [(1024, 1536), (512, 512), (0, 512), (2560, 512), (3072, 1024), (4096, 8)]
    return segs


def _to_kernel_cols(w):
    parts = [w[:, s:s + n] for s, n in _col_segments()]
    parts.append(jnp.zeros((w.shape[0], PW - IN_DIM), w.dtype))
    return jnp.concatenate(parts, axis=1)


def _from_kernel_cols(wp):
    out, pos = {}, 0
    for s, n in _col_segments():
        out[s] = wp[:, pos:pos + n]
        pos += n
    return jnp.concatenate([out[s] for s in sorted(out)], axis=1)


NN = ((1,), (0,))
NT = ((1,), (1,))
TN = ((0,), (0,))


def _bd(a, b, dims):
    return lax.dot_general(a.astype(bf16), b.astype(bf16), (dims, ((), ())), preferred_element_type=f32)


def _hd(a, b, dims):
    return lax.dot_general(a, b, (dims, ((), ())), precision=lax.Precision.HIGHEST, preferred_element_type=f32)


def _make_dots(raw):
    @jax.custom_vjp
    def nn(a, b):
        return raw(a, b, NN)
    nn.defvjp(lambda a, b: (raw(a, b, NN), (a, b)), lambda r, g: (raw(g, r[1], NT), raw(r[0], g, TN)))

    @jax.custom_vjp
    def nt(a, b):
        return raw(a, b, NT)
    nt.defvjp(lambda a, b: (raw(a, b, NT), (a, b)), lambda r, g: (raw(g, r[1], NN), raw(g, r[0], TN)))

    @jax.custom_vjp
    def tn(a, b):
        return raw(a, b, TN)
    tn.defvjp(lambda a, b: (raw(a, b, TN), (a, b)), lambda r, g: (raw(r[1], g, NT), raw(r[0], g, NN)))
    return nn, nt, tn


bdot, bdot_nt, bdot_tn = _make_dots(_bd)
hdot, hdot_nt, hdot_tn = _make_dots(_hd)


@jax.custom_vjp
def bdot_w(a, w, shadow):
    return _bd(a, w, NN)


bdot_w.defvjp(lambda a, w, s: (_bd(a, w, NN), (a, w)),
              lambda r, g: (_bd(g, r[1], NT), jnp.zeros_like(r[1]), _bd(r[0], g, TN)))


def _rows(shape):
    return lax.broadcasted_iota(jnp.int32, shape, 0)


def _cols(shape):
    return lax.broadcasted_iota(jnp.int32, shape, 1)


def _shift_down(x, s):
    return jnp.where(_rows(x.shape) < s, 0.0, pltpu.roll(x, s, 0))


def _shift_up(x, s):
    n = x.shape[0]
    return jnp.where(_rows(x.shape) >= n - s, 0.0, pltpu.roll(x, n - s, 0))


@functools.partial(jax.custom_vjp, nondiff_argnums=(1,))
def shift(x, s):
    return _shift_down(x, s) if s else x


shift.defvjp(lambda x, s: (shift(x, s), None), lambda s, _, g: (_shift_up(g, s) if s else g,))


def _row_of(w, k):
    return jnp.sum(jnp.where(_rows(w.shape) == k, w, 0.0), axis=0, keepdims=True)


def _lane_mask(width, lo, hi):
    c = _cols((1, width))
    return ((c >= lo) & (c < hi)).astype(f32)


def _expand(rows, width, per):
    return (_cols((rows, width)) // per == _rows((rows, width))).astype(f32)


class A(NamedTuple):
    arr: Any
    block: tuple
    imap: Callable
    shadow: bool = False


class O(NamedTuple):
    shape: tuple
    dtype: Any
    block: tuple
    imap: Callable
    alias: Any = None


class R(NamedTuple):
    arg: int
    out: int
    off: Any = None
    acc: bool = False


def _cparams(n_grid):
    return pltpu.CompilerParams(dimension_semantics=("arbitrary",) * n_grid, vmem_limit_bytes=VMEM_LIMIT)


def _ispec(block, imap, n, rev):
    if rev:
        return pl.BlockSpec(block, lambda i: imap(n - 1 - i))
    return pl.BlockSpec(block, imap)


def _load(ref, a):
    v = ref[...]
    if a.shadow:
        return (v, jnp.zeros(v.shape, f32))
    return v.astype(f32)


def _save_spec(shape, n, rev):
    nd = len(shape)
    return _ispec((None,) + tuple(shape), lambda i: (i,) + (0,) * nd, n, rev)


def block_fwd(name, f, n, args, outs, carries=()):
    n_in, n_out, n_c = len(args), len(outs), len(carries)

    def body(*refs):
        ins, out_r = refs[:n_in], refs[n_in:n_in + n_out]
        saves, cs = refs[n_in + n_out:n_in + n_out + n_c], refs[n_in + n_out + n_c:]
        if n_c:
            @pl.when(pl.program_id(0) == 0)
            def _():
                for c in cs:
                    c[...] = jnp.zeros(c.shape, f32)
        vals = [_load(r, a) for r, a in zip(ins, args)]
        cv = [c[...] for c in cs]
        for s, v in zip(saves, cv):
            s[...] = v
        res = f(*vals, *cv)
        for r, v in zip(out_r, res[:n_out]):
            r[...] = v.astype(r.dtype)
        for c, v in zip(cs, res[n_out:]):
            c[...] = v

    out_shape = [jax.ShapeDtypeStruct(o.shape, o.dtype) for o in outs]
    out_specs = [pl.BlockSpec(o.block, o.imap) for o in outs]
    for shp in carries:
        out_shape.append(jax.ShapeDtypeStruct((n,) + tuple(shp), f32))
        out_specs.append(_save_spec(shp, n, False))
    return pl.pallas_call(
        body, name=name, grid=(n,),
        in_specs=[pl.BlockSpec(a.block, a.imap) for a in args],
        out_specs=out_specs, out_shape=out_shape,
        scratch_shapes=[pltpu.VMEM(tuple(shp), f32) for shp in carries],
        compiler_params=_cparams(1),
    )(*[a.arr for a in args])


def block_bwd(name, f, n, args, cots, gouts, routes, saved=(), rev=False):
    n_in, n_cot, n_c, n_go = len(args), len(cots), len(saved), len(gouts)
    diff = []
    for r in routes:
        if r.arg not in diff:
            diff.append(r.arg)
    aliases = [(k, o.alias) for k, o in enumerate(gouts) if o.alias is not None]

    def body(*refs):
        ins = refs[:n_in]
        cot_r = refs[n_in:n_in + n_cot]
        sav_r = refs[n_in + n_cot:n_in + n_cot + n_c]
        base = n_in + n_cot + n_c + len(aliases)
        go_r = refs[base:base + n_go]
        dcs = refs[base + n_go:]
        step = pl.program_id(0)
        if n_c:
            @pl.when(step == 0)
            def _():
                for d in dcs:
                    d[...] = jnp.zeros(d.shape, f32)
        vals = [_load(r, a) for r, a in zip(ins, args)]
        cv = [s[...] for s in sav_r]
        nd = len(diff)

        def g(*dv):
            full = list(vals)
            for idx, v in zip(diff, dv[:nd]):
                full[idx] = (vals[idx][0], v) if args[idx].shadow else v
            return tuple(f(*full, *dv[nd:]))

        primals = [vals[i][1] if args[i].shadow else vals[i] for i in diff] + cv
        _, vjp_fn = jax.vjp(g, *primals)
        ct = tuple([r[...].astype(f32) for r in cot_r] + [d[...] for d in dcs])
        grads = vjp_fn(ct)
        for r in routes:
            gr = grads[diff.index(r.arg)]
            ref = go_r[r.out]
            if r.acc:
                @pl.when(step == 0)
                def _(ref=ref, gr=gr):
                    ref[...] = gr.astype(ref.dtype)

                @pl.when(step > 0)
                def _(ref=ref, gr=gr):
                    ref[...] += gr.astype(ref.dtype)
            elif r.off is None:
                ref[...] = gr.astype(ref.dtype)
            else:
                ref[:, r.off:r.off + gr.shape[1]] = gr.astype(ref.dtype)
        for d, gr in zip(dcs, grads[nd:]):
            d[...] = gr

    in_specs = [_ispec(a.block, a.imap, n, rev) for a in list(args) + list(cots)]
    in_specs += [_save_spec(s.shape[1:], n, rev) for s in saved]
    in_specs += [pl.BlockSpec(memory_space=pl.ANY) for _ in aliases]
    operands = [a.arr for a in list(args) + list(cots)] + list(saved) + [arr for _, arr in aliases]
    io_alias = {n_in + n_cot + n_c + j: k for j, (k, _) in enumerate(aliases)}
    return pl.pallas_call(
        body, name=name, grid=(n,),
        in_specs=in_specs,
        out_specs=[_ispec(o.block, o.imap, n, rev) for o in gouts],
        out_shape=[jax.ShapeDtypeStruct(o.shape, o.dtype) for o in gouts],
        scratch_shapes=[pltpu.VMEM(tuple(s.shape[1:]), f32) for s in saved],
        input_output_aliases=io_alias,
        compiler_params=_cparams(1),
    )(*operands)


class Op(NamedTuple):
    arr: Any
    row: int = 0
    col: int = 0


def mm(name, mode, pairs, m, n, tm, tn, out_dtype=f32, add=None, out=None, out_col=0):
    tm, tn = min(tm, m), min(tn, n)
    assert m % tm == 0 and n % tn == 0
    in_specs, operands = [], []
    for a, b, k, _ in pairs:
        if mode == TN:
            assert a.row % k == 0 and a.col % tm == 0
            in_specs.append(pl.BlockSpec((k, tm), lambda j, i, a=a, k=k: (a.row // k, i + a.col // tm)))
        else:
            assert a.col % k == 0 and a.row % tm == 0
            in_specs.append(pl.BlockSpec((tm, k), lambda j, i, a=a, k=k: (i + a.row // tm, a.col // k)))
        if mode == NT:
            assert b.col % k == 0 and b.row % tn == 0
            in_specs.append(pl.BlockSpec((tn, k), lambda j, i, b=b, k=k: (j + b.row // tn, b.col // k)))
        else:
            assert b.row % k == 0 and b.col % tn == 0
            in_specs.append(pl.BlockSpec((k, tn), lambda j, i, b=b, k=k: (b.row // k, j + b.col // tn)))
        operands += [a.arr, b.arr]
    n_p = len(pairs)
    if add is not None:
        assert add.col % tn == 0
        in_specs.append(pl.BlockSpec((tm, tn), lambda j, i: (i, j + add.col // tn)))
        operands.append(add.arr)
    io_alias = {}
    if out is not None:
        assert out_col % tn == 0
        in_specs.append(pl.BlockSpec(memory_space=pl.ANY))
        operands.append(out)
        io_alias = {len(operands) - 1: 0}
        out_shape = jax.ShapeDtypeStruct(out.shape, out.dtype)
    else:
        out_shape = jax.ShapeDtypeStruct((m, n), out_dtype)
    signs = [p[3] for p in pairs]

    def body(*refs):
        o = refs[-1]
        acc = None
        for p in range(n_p):
            t = _bd(refs[2 * p][...], refs[2 * p + 1][...], mode)
            t = t if signs[p] > 0 else -t
            acc = t if acc is None else acc + t
        if add is not None:
            acc = acc + refs[2 * n_p][...].astype(f32)
        o[...] = acc.astype(o.dtype)

    return pl.pallas_call(
        body, name=name, grid=(n // tn, m // tm),
        in_specs=in_specs,
        out_specs=pl.BlockSpec((tm, tn), lambda j, i: (i, j + out_col // tn)),
        out_shape=out_shape, input_output_aliases=io_alias,
        compiler_params=_cparams(2),
    )(*operands)


SCAN_LANES = 256
SUBL = 8


def _cmul(p, q):
    return (p[0] * q[0] - p[1] * q[1], p[0] * q[1] + p[1] * q[0])


def _powers(a):
    a2 = _cmul(a, a)
    a4 = _cmul(a2, a2)
    a6 = _cmul(a4, a2)
    return [a, a2, _cmul(a2, a), a4, _cmul(a4, a), a6, _cmul(a6, a), _cmul(a4, a4)]


def _table(pw, order, w):
    row = _rows((SUBL, w))
    re = sum(jnp.where(row == t, pw[k][0], 0.0) for t, k in enumerate(order))
    im = sum(jnp.where(row == t, pw[k][1], 0.0) for t, k in enumerate(order))
    return re, im


def _pick_row(x, t):
    return jnp.sum(jnp.where(_rows(x.shape) == t, x, 0.0), axis=0, keepdims=True)


def s5_scan_fwd(bu, a_re, a_im):
    seq, ns = bu.shape[0], a_re.shape[1]
    w, nb = SCAN_LANES, ns // SCAN_LANES

    def body(b_re, b_im, ar, ai, s_re, s_im):
        a = (ar[...], ai[...])
        pw = _powers(a)
        tab = _table(pw, list(range(SUBL)), w)
        row = _rows((SUBL, w))

        def step(i, carry):
            t0 = pl.multiple_of(i * SUBL, SUBL)
            x = (b_re[pl.ds(t0, SUBL), :], b_im[pl.ds(t0, SUBL), :])
            for d, k in ((1, 0), (2, 1), (4, 3)):
                sh = (jnp.where(row < d, 0.0, pltpu.roll(x[0], d, 0)), jnp.where(row < d, 0.0, pltpu.roll(x[1], d, 0)))
                t = _cmul(pw[k], sh)
                x = (x[0] + t[0], x[1] + t[1])
            t = _cmul(tab, carry)
            x = (x[0] + t[0], x[1] + t[1])
            s_re[pl.ds(t0, SUBL), :] = x[0]
            s_im[pl.ds(t0, SUBL), :] = x[1]
            return (_pick_row(x[0], SUBL - 1), _pick_row(x[1], SUBL - 1))

        z = jnp.zeros((1, w), f32)
        lax.fori_loop(0, seq // SUBL, step, (z, z))

    strip = lambda off: pl.BlockSpec((seq, w), lambda j: (0, j + off))
    lane = pl.BlockSpec((1, w), lambda j: (0, j))
    return pl.pallas_call(
        body, name="s5_scan_fwd", grid=(nb,),
        in_specs=[strip(0), strip(nb), lane, lane],
        out_specs=[strip(0), strip(0)],
        out_shape=[jax.ShapeDtypeStruct((seq, ns), f32)] * 2,
        compiler_params=_cparams(1),
    )(bu, bu, a_re, a_im)


def s5_scan_bwd(ds_re, ds_im, s_re, s_im, a_re, a_im):
    seq, ns = ds_re.shape
    w, nb = SCAN_LANES, ns // SCAN_LANES
    nblk = seq // SUBL

    def body(g_re, g_im, sr, si, ar, ai, l_re, l_im, da_re, da_im):
        a = (ar[...], -ai[...])
        pw = _powers(a)
        tab = _table(pw, [SUBL - 1 - t for t in range(SUBL)], w)
        row = _rows((SUBL, w))

        def step(kk, carry):
            c_re, c_im, acc_re, acc_im = carry
            i = nblk - 1 - kk
            t0 = pl.multiple_of(i * SUBL, SUBL)
            x = (g_re[pl.ds(t0, SUBL), :], g_im[pl.ds(t0, SUBL), :])
            for d, k in ((1, 0), (2, 1), (4, 3)):
                sh = (jnp.where(row >= SUBL - d, 0.0, pltpu.roll(x[0], SUBL - d, 0)),
                      jnp.where(row >= SUBL - d, 0.0, pltpu.roll(x[1], SUBL - d, 0)))
                t = _cmul(pw[k], sh)
                x = (x[0] + t[0], x[1] + t[1])
            t = _cmul(tab, (c_re, c_im))
            x = (x[0] + t[0], x[1] + t[1])
            l_re[pl.ds(t0, SUBL), :] = x[0]
            l_im[pl.ds(t0, SUBL), :] = x[1]
            tp = pl.multiple_of(jnp.maximum(i - 1, 0) * SUBL, SUBL)
            live = (i > 0).astype(f32)
            p_re = _pick_row(sr[pl.ds(tp, SUBL), :], SUBL - 1) * live
            p_im = _pick_row(si[pl.ds(tp, SUBL), :], SUBL - 1) * live
            sp_re = jnp.where(row == 0, p_re, pltpu.roll(sr[pl.ds(t0, SUBL), :], 1, 0))
            sp_im = jnp.where(row == 0, p_im, pltpu.roll(si[pl.ds(t0, SUBL), :], 1, 0))
            acc_re = acc_re + x[0] * sp_re + x[1] * sp_im
            acc_im = acc_im + x[1] * sp_re - x[0] * sp_im
            return (_pick_row(x[0], 0), _pick_row(x[1], 0), acc_re, acc_im)

        z1 = jnp.zeros((1, w), f32)
        z8 = jnp.zeros((SUBL, w), f32)
        _, _, acc_re, acc_im = lax.fori_loop(0, nblk, step, (z1, z1, z8, z8))
        da_re[...] = jnp.sum(acc_re, axis=0, keepdims=True)
        da_im[...] = jnp.sum(acc_im, axis=0, keepdims=True)

    strip = pl.BlockSpec((seq, w), lambda j: (0, j))
    lane = pl.BlockSpec((1, w), lambda j: (0, j))
    return pl.pallas_call(
        body, name="s5_scan_bwd", grid=(nb,),
        in_specs=[strip, strip, strip, strip, lane, lane],
        out_specs=[strip, strip, lane, lane],
        out_shape=[jax.ShapeDtypeStruct((seq, ns), f32)] * 2 + [jax.ShapeDtypeStruct((1, ns), f32)] * 2,
        compiler_params=_cparams(1),
    )(ds_re, ds_im, s_re, s_im, a_re, a_im)


def _rms(x, w):
    return x * lax.rsqrt(jnp.mean(x * x, axis=-1, keepdims=True) + EPS) * w


def f_rms(x, w):
    return (_rms(x, w),)


def f_rms_res(x, w):
    return (_rms(x, w), x)


def f_s5_prep(lam_re, lam_im, log_step, b_re, b_im):
    e = _expand(log_step.shape[1], S5_NS, S5_N)
    step = hdot(jnp.exp(log_step), e)
    mag = jnp.exp(lam_re * step)
    ab_re, ab_im = mag * jnp.cos(lam_im * step), mag * jnp.sin(lam_im * step)
    den = lam_re * lam_re + lam_im * lam_im
    nr = ab_re - 1.0
    coef_re = (nr * lam_re + ab_im * lam_im) / den
    coef_im = (ab_im * lam_re - nr * lam_im) / den
    return (ab_re, ab_im, coef_re * b_re - coef_im * b_im, coef_re * b_im + coef_im * b_re)


def f_s5_act(y_lin, u, gate, d, w_glu):
    y = jax.nn.gelu(y_lin + d * u)
    y = y * jax.nn.sigmoid(bdot_w(y, *w_glu))
    return (y * jax.nn.silu(gate),)


def f_sgu(u, v, gate, ln_w, ln_b, *rest):
    w_s, b_pad = rest[:8], rest[8]
    t = u.shape[0]
    u32, v32 = jax.nn.gelu(u), jax.nn.gelu(v)
    mu = jnp.mean(v32, axis=-1, keepdims=True)
    var = jnp.mean(jnp.square(v32 - mu), axis=-1, keepdims=True)
    vn = (v32 - mu) * lax.rsqrt(var + EPS) * ln_w + ln_b
    tri = _rows((t, t)) >= _cols((t, t))
    s = hdot_tn(b_pad, _expand(LANES, BW, BW // 8))
    for h in range(8):
        s = s + bdot(jnp.where(tri, w_s[h], 0.0), vn) * _lane_mask(BW, 64 * h, 64 * h + 64)
    return (u32 * s * jax.nn.silu(gate),)


def f_m2_conv(x, w, b):
    return (sum(_row_of(w, k) * shift(x, 3 - k) for k in range(4)) + b,)


def f_sc(bg, cg, h, gate, w):
    z = cg * h
    conv = sum(_row_of(w, k) * shift(z, 2 - k) for k in range(3))
    return (bg * conv * jax.nn.silu(gate),)


def f_m2(z, xc, b0, b1, c0, c1, dt_raw, dt_bias, a_log, d_par, norm_w, st):
    q = z.shape[0]
    x = jax.nn.silu(xc)
    bm, cm = (jax.nn.silu(b0), jax.nn.silu(b1)), (jax.nn.silu(c0), jax.nn.silu(c1))
    dt = jax.nn.softplus(dt_raw + dt_bias)
    da = dt * (-jnp.exp(a_log))
    tri = _rows((q, q)) >= _cols((q, q))
    acs = hdot(tri.astype(f32), da)
    e = _expand(LANES, BW, M2_HEAD_DIM)
    dt_f, acs_f = hdot(dt, e), hdot(acs, e)
    last = _rows((q, BW)) == q - 1
    alast_f = jnp.sum(jnp.where(last, acs_f, 0.0), axis=0, keepdims=True)
    xdt = x * dt_f
    xdec = xdt * jnp.exp(alast_f - acs_f)
    acs_t = acs.T
    st_new = st * jnp.exp(alast_f)
    y_diag, y_off = 0.0, 0.0
    for g in range(M2_GROUPS):
        gm = _lane_mask(BW, 256 * g, 256 * g + 256)
        cb = bdot_nt(cm[g], bm[g])
        st_new = st_new + bdot_tn(bm[g], xdec * gm)
        y_off = y_off + bdot(cm[g], st) * gm
        for hh in range(M2_HEADS // M2_GROUPS):
            h = g * (M2_HEADS // M2_GROUPS) + hh
            col = jnp.sum(jnp.where(_cols((q, LANES)) == h, acs, 0.0), axis=1, keepdims=True)
            row = jnp.sum(jnp.where(_rows((LANES, q)) == h, acs_t, 0.0), axis=0, keepdims=True)
            decay = jnp.exp(jnp.where(tri, col - row, -1e30))
            y_diag = y_diag + bdot(cb * decay, xdt) * _lane_mask(BW, 64 * h, 64 * h + 64)
    d_f = sum(jnp.sum(jnp.where(_cols((1, LANES)) == h, d_par, 0.0), axis=1, keepdims=True)
              * _lane_mask(BW, 64 * h, 64 * h + 64) for h in range(M2_HEADS))
    y = y_diag + y_off * jnp.exp(acs_f) + d_f * x
    y = y * jax.nn.silu(z)
    return (_rms(y, norm_w), st_new)


def f_gate_mix(*v):
    bo, lg, mb = v[0:4], v[4:8], v[8:12]
    return (sum(jax.nn.sigmoid(lg[k] + mb[k]) * bo[k] for k in range(N_BRANCH)),)


def _full(shape):
    nd = len(shape)
    return dict(block=tuple(shape), imap=lambda i: (0,) * nd)


def _param(arr, shadow=False):
    return A(arr, tuple(arr.shape), lambda i, nd=arr.ndim: (0,) * nd, shadow)


def _tb(arr, t, width, colblk):
    return A(arr, (t, width), lambda i: (i, colblk))


def _strip(arr, seq, colblk0, stride=1):
    return A(arr, (seq, LANES), lambda j: (0, colblk0 + stride * j))


def _block_diag(t, transpose):
    if transpose:
        t = jnp.swapaxes(t, 1, 2)
    g, a, b = t.shape
    eye = jnp.eye(g, dtype=bool)
    full = jnp.where(eye[:, None, :, None], t[:, :, None, :], jnp.zeros((), t.dtype))
    return full.reshape(g * a, g * b)


def _diag_blocks(m, g, transpose):
    a, b = m.shape[0] // g, m.shape[1] // g
    blocks = jnp.stack([m[k * a:(k + 1) * a, k * b:(k + 1) * b] for k in range(g)])
    return jnp.swapaxes(blocks, 1, 2) if transpose else blocks


def _s5_lane_params(p):
    lam_re = p["s5_lambda_re"].reshape(1, S5_NS)
    lam_im = p["s5_lambda_im"].reshape(1, S5_NS)
    log_step = jnp.pad(p["s5_log_step"].reshape(1, S5_GROUPS), ((0, 0), (0, LANES - S5_GROUPS)))
    to_lanes = lambda b: jnp.transpose(b, (2, 0, 1)).reshape(S5_P, S5_NS)
    return lam_re, lam_im, log_step, to_lanes(p["s5_b_re"]), to_lanes(p["s5_b_im"])


def _s5_prep_args(p):
    return [_param(v) for v in _s5_lane_params(p)]


def layer_fwd(x, p):
    seq = x.shape[0]
    nt = seq // CHUNK
    t2 = 256
    sv = {}
    hb = block_fwd("rms_in", f_rms, seq // t2, [_tb(x, t2, D_MODEL, 0), _param(p["norm_w"].reshape(1, D_MODEL))],
                   [O((seq, D_MODEL), bf16, (t2, D_MODEL), lambda i: (i, 0))])[0]
    proj = mm("mm_in", NN, [(Op(hb), Op(p["w_in"]), D_MODEL, 1)], seq, PW, 512, 1152)
    ab_re, ab_im, bb_re, bb_im = block_fwd(
        "s5_prep", f_s5_prep, 1, _s5_prep_args(p),
        [O((1, S5_NS), f32, (1, S5_NS), lambda i: (0, 0))] * 2 + [O((S5_P, S5_NS), f32, (S5_P, S5_NS), lambda i: (0, 0))] * 2)
    bd = lambda t: _block_diag(jnp.transpose(t.reshape(S5_P, S5_GROUPS, S5_N), (1, 0, 2)), False)
    bb_bd = jnp.concatenate([bd(bb_re), bd(bb_im)], axis=1).astype(bf16)
    c_bd = jnp.concatenate([_block_diag(p["s5_c_re"], True), _block_diag(p["s5_c_im"], True)], axis=0).astype(bf16)
    bu = mm("mm_bu", NN, [(Op(proj, 0, C_S5U), Op(bb_bd), BW, 1)], seq, 2 * S5_NS, 512, 1024)
    s_re, s_im = s5_scan_fwd(bu, ab_re, ab_im)
    y_lin = mm("mm_s5y", NN, [(Op(s_re), Op(c_bd), S5_NS, 1), (Op(s_im), Op(c_bd, S5_NS), S5_NS, -1)], seq, BW, 256, BW)
    s5_act_args = [_tb(y_lin, CHUNK, BW, 0), _tb(proj, CHUNK, BW, C_S5U // BW), _tb(proj, CHUNK, BW, C_S5G // BW),
                   _param(p["s5_d"].reshape(1, BW)), _param(p["s5_w_glu"], True)]
    out_bw = O((seq, BW), f32, (CHUNK, BW), lambda i: (i, 0))
    y_a = block_fwd("s5_act", f_s5_act, nt, s5_act_args, [out_bw])[0]
    y_b = block_fwd("sgu", f_sgu, nt, _sgu_args(proj, p), [out_bw])[0]
    xc = block_fwd("m2_conv", f_m2_conv, 8, _m2_conv_args(proj, p, seq),
                   [O((seq, 2 * BW), f32, (seq, LANES), lambda j: (0, j))])[0]
    y_c, st_saved = block_fwd("m2_ssd", f_m2, nt, _m2_args(proj, xc, p), [out_bw], carries=[(M2_STATE, BW)])
    y_d = block_fwd("sc", f_sc, 4, _sc_args(proj, p, seq), [O((seq, BW), f32, (seq, LANES), lambda j: (0, j))])[0]
    ys = [y_a, y_b, y_c, y_d]
    bo = [mm(f"mm_branch{k}", NN, [(Op(ys[k]), Op(p["w_branch"], k * BW, 0), BW, 1)], seq, D_MODEL, 512, D_MODEL)
          for k in range(N_BRANCH)]
    merged = block_fwd("gate_mix", f_gate_mix, seq // t2, _mix_args(bo, proj, p, t2),
                       [O((seq, D_MODEL), bf16, (t2, D_MODEL), lambda i: (i, 0))])[0]
    x_new = mm("mm_out", NN, [(Op(merged), Op(p["w_out"]), D_MODEL, 1)], seq, D_MODEL, 512, D_MODEL, add=Op(x))
    sv.update(x=x, hb=hb, proj=proj, ab=(ab_re, ab_im), bb_bd=bb_bd, c_bd=c_bd, s=(s_re, s_im), y_lin=y_lin,
              xc=xc, st=st_saved, ys=ys, bo=bo, merged=merged)
    return x_new, sv


def _sgu_args(proj, p):
    c0 = C_SGU // BW
    args = [_tb(proj, CHUNK, BW, c0), _tb(proj, CHUNK, BW, c0 + 1), _tb(proj, CHUNK, BW, c0 + 2),
            _param(p["sgu_ln_w"].reshape(1, BW)), _param(p["sgu_ln_b"].reshape(1, BW))]
    args += [A(p["sgu_w"], (None, CHUNK, CHUNK), lambda i, h=h: (h, 0, 0)) for h in range(8)]
    args.append(_param(jnp.pad(p["sgu_b"], ((0, LANES - 8), (0, 0)))))
    return args


def _m2_conv_args(proj, p, seq):
    return [_strip(proj, seq, C_XBC // LANES), A(p["m2_conv_w"], (4, LANES), lambda j: (0, j)),
            A(p["m2_conv_b"].reshape(1, 2 * BW), (1, LANES), lambda j: (0, j))]


def _pad_lanes(v):
    return jnp.pad(v.reshape(1, -1), ((0, 0), (0, LANES - v.size)))


def _m2_args(proj, xc, p):
    args = [_tb(proj, CHUNK, BW, C_M2Z // BW), _tb(xc, CHUNK, BW, 0)]
    args += [_tb(xc, CHUNK, LANES, 4 + k) for k in range(4)]
    args.append(_tb(proj, CHUNK, LANES, C_DT // LANES))
    args += [_param(_pad_lanes(p["m2_dt_bias"])), _param(_pad_lanes(p["m2_a_log"])), _param(_pad_lanes(p["m2_d"])),
             _param(p["m2_norm_w"].reshape(1, BW))]
    return args


def _sc_args(proj, p, seq):
    c0 = C_SC // LANES
    return [_strip(proj, seq, c0 + k, 4) for k in range(4)] + [A(p["sc_conv_w"], (3, LANES), lambda j: (0, j))]


def _mix_args(bo, proj, p, t):
    args = [_tb(bo[k], t, D_MODEL, 0) for k in range(N_BRANCH)]
    args += [_tb(proj, t, D_MODEL, k) for k in range(N_BRANCH)]
    mb = p["merge_b"].reshape(N_BRANCH, 1, D_MODEL)
    args += [A(mb, (None, 1, D_MODEL), lambda i, k=k: (k, 0, 0)) for k in range(N_BRANCH)]
    return args


def layer_bwd(d_out, p, sv):
    seq = d_out.shape[0]
    nt = seq // CHUNK
    t2 = 256
    proj, ys, bo = sv["proj"], sv["ys"], sv["bo"]
    g = {}
    acc = lambda shape: O(tuple(shape), f32, tuple(shape), lambda i, nd=len(shape): (0,) * nd)
    d_merged = mm("mm_out_dx", NT, [(Op(d_out), Op(p["w_out"]), D_MODEL, 1)], seq, D_MODEL, 512, D_MODEL, out_dtype=f32)
    g["w_out"] = mm("mm_out_dw", TN, [(Op(sv["merged"]), Op(d_out), seq, 1)], D_MODEL, D_MODEL, 256, D_MODEL)
    gouts = [O((seq, D_MODEL), bf16, (t2, D_MODEL), lambda i: (i, 0)) for _ in range(N_BRANCH)]
    gouts.append(O((seq, PW_MAIN), bf16, (t2, N_BRANCH * D_MODEL), lambda i: (i, 0)))
    gouts += [acc((1, D_MODEL)) for _ in range(N_BRANCH)]
    routes = [R(k, k) for k in range(N_BRANCH)] + [R(4 + k, 4, k * D_MODEL) for k in range(N_BRANCH)]
    routes += [R(8 + k, 5 + k, acc=True) for k in range(N_BRANCH)]
    res = block_bwd("gate_mix_bwd", f_gate_mix, seq // t2, _mix_args(bo, proj, p, t2),
                    [_tb(d_merged, t2, D_MODEL, 0)], gouts, routes)
    dbo, dproj = res[:4], res[4]
    g["merge_b"] = jnp.concatenate(res[5:9], axis=0)
    dys = [mm(f"mm_branch{k}_dx", NT, [(Op(dbo[k]), Op(p["w_branch"], k * BW, 0), D_MODEL, 1)], seq, BW, 512, BW)
           for k in range(N_BRANCH)]
    g["w_branch"] = jnp.stack([mm(f"mm_branch{k}_dw", TN, [(Op(ys[k]), Op(dbo[k]), seq, 1)], BW, D_MODEL, 256, D_MODEL)
                               for k in range(N_BRANCH)])
    res = block_bwd("sc_bwd", f_sc, 4, _sc_args(proj, p, seq), [_strip(dys[3], seq, 0)],
                    [O((seq, PW_MAIN), bf16, (seq, 4 * LANES), lambda j: (0, C_SC // (4 * LANES) + j), alias=dproj),
                     O((3, BW), f32, (3, LANES), lambda j: (0, j))],
                    [R(k, 0, k * LANES) for k in range(4)] + [R(4, 1)])
    dproj, g["sc_conv_w"] = res
    res = block_bwd("m2_ssd_bwd", f_m2, nt, _m2_args(proj, sv["xc"], p), [_tb(dys[2], CHUNK, BW, 0)],
                    [O((seq, PW_MAIN), bf16, (CHUNK, BW), lambda i: (i, C_M2Z // BW), alias=dproj),
                     O((seq, 2 * BW), f32, (CHUNK, 2 * BW), lambda i: (i, 0)),
                     O((seq, LANES), bf16, (CHUNK, LANES), lambda i: (i, 0)),
                     acc((1, LANES)), acc((1, LANES)), acc((1, LANES)), acc((1, BW))],
                    [R(0, 0), R(1, 1, 0)] + [R(2 + k, 1, BW + k * LANES) for k in range(4)] + [R(6, 2)]
                    + [R(7, 3, acc=True), R(8, 4, acc=True), R(9, 5, acc=True), R(10, 6, acc=True)],
                    saved=[sv["st"]], rev=True)
    dproj, dxc, d_dt = res[0], res[1], res[2]
    g["m2_dt_bias"], g["m2_a_log"], g["m2_d"] = (r[0, :M2_HEADS] for r in res[3:6])
    g["m2_norm_w"] = res[6].reshape(BW)
    res = block_bwd("m2_conv_bwd", f_m2_conv, 8, _m2_conv_args(proj, p, seq), [_strip(dxc, seq, 0)],
                    [O((seq, PW_MAIN), bf16, (seq, LANES), lambda j: (0, C_XBC // LANES + j), alias=dproj),
                     O((4, 2 * BW), f32, (4, LANES), lambda j: (0, j)), O((1, 2 * BW), f32, (1, LANES), lambda j: (0, j))],
                    [R(0, 0), R(1, 1), R(2, 2)])
    dproj, g["m2_conv_w"], cb = res
    g["m2_conv_b"] = cb.reshape(2 * BW)
    res = block_bwd("sgu_bwd", f_sgu, nt, _sgu_args(proj, p), [_tb(dys[1], CHUNK, BW, 0)],
                    [O((seq, PW_MAIN), bf16, (CHUNK, 3 * BW), lambda i: (i, C_SGU // (3 * BW)), alias=dproj),
                     acc((1, BW)), acc((1, BW))] + [acc((CHUNK, CHUNK)) for _ in range(8)] + [acc((LANES, CHUNK))],
                    [R(0, 0, 0), R(1, 0, BW), R(2, 0, 2 * BW), R(3, 1, acc=True), R(4, 2, acc=True)]
                    + [R(5 + h, 3 + h, acc=True) for h in range(8)] + [R(13, 11, acc=True)])
    dproj = res[0]
    g["sgu_ln_w"], g["sgu_ln_b"] = res[1].reshape(BW), res[2].reshape(BW)
    g["sgu_w"] = jnp.stack(res[3:11])
    g["sgu_b"] = res[11][:8]
    y_lin, (s_re, s_im), (ab_re, ab_im) = sv["y_lin"], sv["s"], sv["ab"]
    s5_act_args = [_tb(y_lin, CHUNK, BW, 0), _tb(proj, CHUNK, BW, C_S5U // BW), _tb(proj, CHUNK, BW, C_S5G // BW),
                   _param(p["s5_d"].reshape(1, BW)), _param(p["s5_w_glu"], True)]
    res = block_bwd("s5_act_bwd", f_s5_act, nt, s5_act_args, [_tb(dys[0], CHUNK, BW, 0)],
                    [O((seq, BW), bf16, (CHUNK, BW), lambda i: (i, 0)), O((seq, BW), f32, (CHUNK, BW), lambda i: (i, 0)),
                     O((seq, PW_MAIN), bf16, (CHUNK, BW), lambda i: (i, C_S5G // BW), alias=dproj),
                     acc((1, BW)), acc((BW, BW))],
                    [R(0, 0), R(1, 1), R(2, 2), R(3, 3, acc=True), R(4, 4, acc=True)])
    dy_lin, du1, dproj = res[0], res[1], res[2]
    g["s5_d"] = res[3].reshape(S5_GROUPS, S5_P)
    g["s5_w_glu"] = res[4]
    c_bd, bb_bd = sv["c_bd"], sv["bb_bd"]
    ds_re = mm("mm_s5y_dre", NT, [(Op(dy_lin), Op(c_bd), BW, 1)], seq, S5_NS, 512, 1024)
    ds_im = mm("mm_s5y_dim", NT, [(Op(dy_lin), Op(c_bd, S5_NS), BW, -1)], seq, S5_NS, 512, 1024)
    dc_re = mm("mm_s5y_dcre", TN, [(Op(s_re), Op(dy_lin), seq, 1)], S5_NS, BW, 512, BW)
    dc_im = mm("mm_s5y_dcim", TN, [(Op(s_im), Op(dy_lin), seq, -1)], S5_NS, BW, 512, BW)
    g["s5_c_re"], g["s5_c_im"] = _diag_blocks(dc_re, S5_GROUPS, True), _diag_blocks(dc_im, S5_GROUPS, True)
    l_re, l_im, da_re, da_im = s5_scan_bwd(ds_re, ds_im, s_re, s_im, ab_re, ab_im)
    dproj = mm("mm_bu_dx", NT, [(Op(l_re), Op(bb_bd), S5_NS, 1), (Op(l_im), Op(bb_bd, 0, S5_NS), S5_NS, 1)],
               seq, BW, 512, BW, add=Op(du1), out=dproj, out_col=C_S5U)
    dbb = [mm(f"mm_bu_dw{n}", TN, [(Op(proj, 0, C_S5U), Op(l), seq, 1)], BW, S5_NS, 256, 1024)
           for n, l in (("re", l_re), ("im", l_im))]
    from_bd = lambda m: jnp.transpose(_diag_blocks(m, S5_GROUPS, False), (1, 0, 2)).reshape(S5_P, S5_NS)
    gouts = [acc((1, S5_NS)), acc((1, S5_NS)), acc((1, LANES)), acc((S5_P, S5_NS)), acc((S5_P, S5_NS))]
    res = block_bwd("s5_prep_bwd", f_s5_prep, 1, _s5_prep_args(p),
                    [_param(da_re), _param(da_im), _param(from_bd(dbb[0])), _param(from_bd(dbb[1]))],
                    gouts, [R(k, k, acc=True) for k in range(5)])
    g["s5_lambda_re"], g["s5_lambda_im"] = res[0].reshape(S5_GROUPS, S5_N), res[1].reshape(S5_GROUPS, S5_N)
    g["s5_log_step"] = res[2][0, :S5_GROUPS]
    from_lanes = lambda b: jnp.transpose(b.reshape(S5_P, S5_GROUPS, S5_N), (1, 2, 0))
    g["s5_b_re"], g["s5_b_im"] = from_lanes(res[3]), from_lanes(res[4])
    hb, w_in = sv["hb"], p["w_in"]
    dh = mm("mm_in_dx", NT, [(Op(dproj), Op(w_in), PW_MAIN, 1), (Op(d_dt), Op(w_in, 0, PW_MAIN), LANES, 1)],
            seq, D_MODEL, 256, 256)
    g["w_in"] = (mm("mm_in_dw", TN, [(Op(hb), Op(dproj), seq, 1)], D_MODEL, PW_MAIN, 256, 1024),
                 mm("mm_in_dwdt", TN, [(Op(hb), Op(d_dt), seq, 1)], D_MODEL, LANES, 256, LANES))
    dx, dnw = block_bwd("rms_in_bwd", f_rms_res, seq // t2, [_tb(sv["x"], t2, D_MODEL, 0), _param(p["norm_w"].reshape(1, D_MODEL))],
                        [_tb(dh, t2, D_MODEL, 0), _tb(d_out, t2, D_MODEL, 0)],
                        [O((seq, D_MODEL), f32, (t2, D_MODEL), lambda i: (i, 0)), acc((1, D_MODEL))],
                        [R(0, 0), R(1, 1, acc=True)])
    g["norm_w"] = dnw.reshape(D_MODEL)
    return dx, g


def loss_head(x, w, target):
    seq = x.shape[0]
    t = 256

    def body(x_ref, w_ref, t_ref, loss_ref, dx_ref, dw_ref):
        step = pl.program_id(0)

        def f(xv, wv):
            err = _rms(xv, wv) - t_ref[...]
            return 0.5 * jnp.sum(jnp.mean(err * err, axis=-1, keepdims=True), axis=0, keepdims=True)

        val, vjp_fn = jax.vjp(f, x_ref[...], w_ref[...])
        dx, dw = vjp_fn(jnp.ones((1, 1), f32))
        dx_ref[...] = dx

        @pl.when(step == 0)
        def _():
            loss_ref[...] = jnp.broadcast_to(val, loss_ref.shape)
            dw_ref[...] = dw

        @pl.when(step > 0)
        def _():
            loss_ref[...] += jnp.broadcast_to(val, loss_ref.shape)
            dw_ref[...] += dw

    blk = pl.BlockSpec((t, D_MODEL), lambda i: (i, 0))
    row = pl.BlockSpec((1, D_MODEL), lambda i: (0, 0))
    return pl.pallas_call(
        body, name="loss_head", grid=(seq // t,),
        in_specs=[blk, row, blk],
        out_specs=[pl.BlockSpec((1, LANES), lambda i: (0, 0)), blk, row],
        out_shape=[jax.ShapeDtypeStruct((1, LANES), f32), jax.ShapeDtypeStruct((seq, D_MODEL), f32),
                   jax.ShapeDtypeStruct((1, D_MODEL), f32)],
        compiler_params=_cparams(1),
    )(x, w.reshape(1, D_MODEL), target)


LAYER_KEYS = ("norm_w", "w_in", "s5_lambda_re", "s5_lambda_im", "s5_b_re", "s5_b_im", "s5_c_re", "s5_c_im", "s5_d",
              "s5_log_step", "s5_w_glu", "sgu_ln_w", "sgu_ln_b", "sgu_w", "sgu_b", "m2_conv_w", "m2_conv_b",
              "m2_dt_bias", "m2_a_log", "m2_d", "m2_norm_w", "sc_conv_w", "merge_b", "w_branch", "w_out")


def local_step(x, target, layers, final_norm_w):
    saved = []
    for p in layers:
        x, sv = layer_fwd(x, p)
        saved.append(sv)
    loss, dx, dfw = loss_head(x, final_norm_w, target)
    grads = []
    for p, sv in zip(reversed(layers), reversed(saved)):
        dx, g = layer_bwd(dx, p, sv)
        grads.append(g)
    return loss[0, 0], dx, grads[::-1], dfw.reshape(D_MODEL)


MESH = pl.DeviceIdType.MESH
ANY = pl.BlockSpec(memory_space=pl.ANY)


def _me():
    return lax.axis_index("x"), lax.axis_index("y"), lax.axis_index("c")


def _other_chips(x, y):
    return [(1 - x, y), (x, 1 - y), (1 - x, 1 - y)]


def _rcopy(src, dst, send, recv, dev):
    return pltpu.make_async_remote_copy(src_ref=src, dst_ref=dst, send_sem=send, recv_sem=recv,
                                        device_id=dev, device_id_type=MESH)


def _comm_call(name, body, arrs, out_shape, n_remote, aliases=None):
    n = len(arrs)
    return pl.pallas_call(
        body, name=name, in_specs=[ANY] * n, out_specs=[ANY] * len(out_shape), out_shape=out_shape,
        scratch_shapes=[pltpu.SemaphoreType.DMA((n, n_remote)), pltpu.SemaphoreType.DMA((n, n_remote))],
        input_output_aliases=aliases or {},
        compiler_params=pltpu.CompilerParams(has_side_effects=True),
    )(*arrs)


def gather_chips(name, arrs):
    n = len(arrs)

    def body(*refs):
        ins, outs = refs[:n], refs[n:2 * n]
        send, recv = refs[2 * n:]
        x, y, c = _me()
        jme = 2 * x + y
        chips = _other_chips(x, y)
        first = [_rcopy(ins[a].at[c], outs[a].at[jme, c], send.at[a, k], recv.at[a, k], (cx, cy, c))
                 for a in range(n) for k, (cx, cy) in enumerate(chips)]
        for cp in first:
            cp.start()
        passed = []
        for k, (cx, cy) in enumerate(chips):
            jk = 2 * cx + cy
            for a in range(n):
                _rcopy(ins[a].at[c], outs[a].at[jk, c], send.at[a, k], recv.at[a, k], (cx, cy, c)).wait_recv()
                fw = _rcopy(outs[a].at[jk, c], outs[a].at[jk, c], send.at[a, 3 + k], recv.at[a, 3 + k], (x, y, 1 - c))
                fw.start()
                passed.append(fw)
        for k, (cx, cy) in enumerate(chips):
            jk = 2 * cx + cy
            for a in range(n):
                _rcopy(outs[a].at[jk, 1 - c], outs[a].at[jk, 1 - c], send.at[a, 3 + k], recv.at[a, 3 + k],
                       (x, y, 1 - c)).wait_recv()
        for cp in first + passed:
            cp.wait_send()

    out_shape = [jax.ShapeDtypeStruct((4,) + a.shape, a.dtype) for a in arrs]
    got = _comm_call(name, body, arrs, out_shape, 6)
    jme = 2 * lax.axis_index("x") + lax.axis_index("y")
    return [lax.dynamic_update_index_in_dim(g, a, jme, 0) for g, a in zip(got, arrs)]


def swap_halves(name, arrs):
    n = len(arrs)

    def body(*refs):
        ins, outs = refs[:n], refs[n:2 * n]
        send, recv = refs[2 * n:]
        x, y, c = _me()
        remote = [_rcopy(ins[a].at[1 - c], outs[a], send.at[a, 0], recv.at[a, 0], (x, y, 1 - c)) for a in range(n)]
        for cp in remote:
            cp.start()
        for cp in remote:
            cp.wait()

    return _comm_call(name, body, arrs, [jax.ShapeDtypeStruct(a.shape[1:], a.dtype) for a in arrs], 1)


def exchange_chips(name, arrs):
    n = len(arrs)

    def body(*refs):
        ins, outs = refs[:n], refs[n:2 * n]
        send, recv = refs[2 * n:]
        x, y, c = _me()
        remote = [_rcopy(ins[a].at[2 * cx + cy], outs[a].at[k], send.at[a, k], recv.at[a, k], (cx, cy, c))
                  for a in range(n) for k, (cx, cy) in enumerate(_other_chips(x, y))]
        for cp in remote:
            cp.start()
        for cp in remote:
            cp.wait()

    return _comm_call(name, body, arrs, [jax.ShapeDtypeStruct((3,) + a.shape[1:], a.dtype) for a in arrs], 3)


def gather_cores(name, arrs):
    n = len(arrs)

    def body(*refs):
        bufs = refs[n:2 * n]
        send, recv = refs[2 * n:]
        x, y, c = _me()
        remote = [_rcopy(bufs[a].at[c], bufs[a].at[c], send.at[a, 0], recv.at[a, 0], (x, y, 1 - c)) for a in range(n)]
        for cp in remote:
            cp.start()
        for a in range(n):
            _rcopy(bufs[a].at[1 - c], bufs[a].at[1 - c], send.at[a, 0], recv.at[a, 0], (x, y, 1 - c)).wait_recv()
        for cp in remote:
            cp.wait_send()

    return _comm_call(name, body, arrs, [jax.ShapeDtypeStruct(a.shape, a.dtype) for a in arrs], 1,
                      aliases={a: a for a in range(n)})


ROW_BLOCK = 256


def esum(name, terms, rows, width, out_dtype, out_slots=None):
    tr = next((t for t in range(min(rows, ROW_BLOCK), 0, -SUBL) if rows % t == 0 and t % SUBL == 0), rows)
    where =jnp.stack([lax.axis_index("c"), 2 * lax.axis_index("x") + lax.axis_index("y")]).astype(jnp.int32)
    pick = {"c": 0, "j": 1}

    def body(s_ref, *refs):
        acc = refs[0][...].astype(f32)
        for r in refs[1:-1]:
            acc = acc + r[...].astype(f32)
        refs[-1][...] = acc.astype(out_dtype)

    specs = []
    for arr, lead in terms:
        if lead is None:
            specs.append(pl.BlockSpec((tr, width), lambda i, s: (i, 0)))
        elif isinstance(lead, str):
            specs.append(pl.BlockSpec((None, tr, width), lambda i, s, lead=lead: (s[pick[lead]], i, 0)))
        else:
            specs.append(pl.BlockSpec((None, tr, width), lambda i, s, lead=lead: (lead, i, 0)))
    if out_slots is None:
        out_spec = pl.BlockSpec((tr, width), lambda i, s: (i, 0))
        out_shape = jax.ShapeDtypeStruct((rows, width), out_dtype)
    else:
        out_spec = pl.BlockSpec((None, tr, width), lambda i, s: (s[0], i, 0))
        out_shape = jax.ShapeDtypeStruct((out_slots, rows, width), out_dtype)
    return pl.pallas_call(
        body, name=name,
        grid_spec=pltpu.PrefetchScalarGridSpec(num_scalar_prefetch=1, grid=(rows // tr,), in_specs=specs, out_specs=out_spec),
        out_shape=out_shape, compiler_params=_cparams(1),
    )(where, *[t[0] for t in terms])


def reduce_to_shards(parts):
    tags = [str(k) for k in range(len(parts))]
    theirs = swap_halves("rs_swap", parts)
    t1 = []
    for tag, p, th in zip(tags, parts, theirs):
        _, _, h, w = p.shape
        t1.append(esum("rs_add_cores" + tag, [(p.reshape(2, 4 * h, w), "c"), (th.reshape(4 * h, w), None)],
                       4 * h, w, p.dtype).reshape(4, h, w))
    landed = exchange_chips("rs_exchange", t1)
    red = []
    for tag, p, t, got in zip(tags, parts, t1, landed):
        _, _, h, w = p.shape
        red.append(esum("rs_add_chips" + tag, [(t, "j"), (got, 0), (got, 1), (got, 2)], h, w, f32, out_slots=2))
    return gather_cores("rs_gather", red)


def adamw(name, w, g, m, v):
    rows, width = w.shape
    tr = ROW_BLOCK if rows % ROW_BLOCK == 0 else rows

    def body(w_ref, g_ref, m_ref, v_ref, d_ref, nm_ref, nv_ref):
        gv = g_ref[...]
        nm = ADAM_B1 * m_ref[...] + (1.0 - ADAM_B1) * gv
        nv = ADAM_B2 * v_ref[...] + (1.0 - ADAM_B2) * jnp.square(gv)
        m_hat = nm / (1.0 - ADAM_B1 ** ADAM_STEP)
        v_hat = nv / (1.0 - ADAM_B2 ** ADAM_STEP)
        d_ref[...] = -ADAM_LR * (m_hat / (jnp.sqrt(v_hat) + ADAM_EPS) + ADAM_WD * w_ref[...])
        nm_ref[...] = nm
        nv_ref[...] = nv

    spec = pl.BlockSpec((tr, width), lambda i: (i, 0))
    return pl.pallas_call(
        body, name=name, grid=(rows // tr,), in_specs=[spec] * 4, out_specs=[spec] * 3,
        out_shape=[jax.ShapeDtypeStruct((rows, width), f32)] * 3, compiler_params=_cparams(1),
    )(w, g, m, v)


PACK_W = 1024


def _pack(parts, halves, row_mult):
    flat = jnp.concatenate([p.reshape(-1) for p in parts])
    per = halves * row_mult * PACK_W
    total = -(-flat.size // per) * per
    flat = jnp.pad(flat, (0, total - flat.size))
    return flat.reshape(halves, total // (halves * PACK_W), PACK_W)


def _unpack(flat, shapes):
    out, pos = [], 0
    for s in shapes:
        n = int(np.prod(s))
        out.append(flat[pos:pos + n].reshape(s))
        pos += n
    return out


SHARDED_BIG = ("w_in", "w_branch", "w_out", "s5_w_glu")
SHARDED_SMALL = ("m2_conv_w", "sc_conv_w", "merge_b")
SHARD_AXIS = {"w_in": 2, "w_branch": 3, "w_out": 1, "s5_w_glu": 1, "m2_conv_w": 2, "sc_conv_w": 2, "merge_b": 2}
REPLICATED = ("norm_w", "s5_lambda_re", "s5_lambda_im", "s5_b_re", "s5_b_im", "s5_c_re", "s5_c_im", "s5_d", "s5_log_step",
              "sgu_ln_w", "sgu_ln_b", "sgu_w", "sgu_b", "m2_conv_b", "m2_dt_bias", "m2_a_log", "m2_d", "m2_norm_w")
WEIGHTS = ("norm_w", "w_in", "s5_lambda_re", "s5_lambda_im", "s5_b_re", "s5_b_im", "s5_c_re", "s5_c_im", "s5_d",
           "s5_log_step", "s5_w_glu", "sgu_ln_w", "sgu_ln_b", "sgu_w", "sgu_b", "m2_conv_w", "m2_conv_b", "m2_dt_bias",
           "m2_a_log", "m2_d", "m2_norm_w", "sc_conv_w", "merge_b", "w_branch", "w_out", "final_norm_w")
N_LAYERS = 2


SHARD_W = IN_DIM // 4


def _kernel_pieces():
    out, pos = [], 0
    for s, n in _col_segments():
        while n:
            take = min(n, SHARD_W - s % SHARD_W)
            out.append((pos, s, take))
            pos, s, n = pos + take, s + take, n - take
    return out


def _kernel_cols_from_shards(blocks):
    parts = [blocks[s // SHARD_W][:, s % SHARD_W:s % SHARD_W + n] for _, s, n in _kernel_pieces()]
    parts.append(jnp.zeros((blocks.shape[1], PW - IN_DIM), blocks.dtype))
    return jnp.concatenate(parts, axis=1)


def _shards_from_kernel_cols(main, dt, dtype):
    blocks = []
    for j in range(4):
        mine = sorted((s, pos, n) for pos, s, n in _kernel_pieces() if s // SHARD_W == j)
        parts = [(main[:, pos:pos + n] if pos < PW_MAIN else dt[:, pos - PW_MAIN:pos - PW_MAIN + n]).astype(dtype)
                 for _, pos, n in mine]
        blocks.append(jnp.concatenate(parts, axis=1))
    return jnp.stack(blocks)


def _gather_weights(w):
    arrs = [w["w_in"].astype(bf16), w["w_branch"].reshape(N_LAYERS, N_BRANCH * BW, -1).astype(bf16),
            w["w_out"].astype(bf16), w["s5_w_glu"].astype(bf16), w["m2_conv_w"], w["sc_conv_w"], w["merge_b"]]
    got = gather_chips("ag_weights", arrs)
    cols = lambda t: jnp.transpose(t, (1, 0, 2)).reshape(t.shape[1], -1)
    layers = []
    for i in range(N_LAYERS):
        p = {k: w[k][i] for k in REPLICATED}
        p["w_in"] = _kernel_cols_from_shards(got[0][:, i])
        p["w_branch"] = cols(got[1][:, i])
        p["w_out"] = got[2][:, i].reshape(D_MODEL, D_MODEL)
        p["s5_w_glu"] = got[3][:, i].reshape(BW, BW)
        p["m2_conv_w"], p["sc_conv_w"], p["merge_b"] = cols(got[4][:, i]), cols(got[5][:, i]), cols(got[6][:, i])
        layers.append(p)
    return layers


def _reduce_grads(grads, d_final, w):
    to_chips = lambda t: jnp.transpose(t.reshape(t.shape[0], 4, -1), (1, 0, 2))
    stack = lambda f: jnp.stack([f(g) for g in grads])
    parts = [stack(lambda g: _shards_from_kernel_cols(g["w_in"][0], g["w_in"][1], bf16)),
             stack(lambda g: to_chips(g["w_branch"].reshape(N_BRANCH * BW, D_MODEL)).astype(bf16)),
             stack(lambda g: g["w_out"].reshape(4, D_MODEL // 4, D_MODEL).astype(bf16)),
             stack(lambda g: g["s5_w_glu"].reshape(4, BW // 4, BW).astype(bf16))]
    rep = jnp.concatenate([stack(lambda g: g[k]).reshape(-1) for k in REPLICATED] + [d_final.reshape(-1)])
    quarter = -(-rep.size // (4 * 2 * SUBL * PACK_W)) * (2 * SUBL * PACK_W)
    rep = jnp.pad(rep, (0, 4 * quarter - rep.size))
    small = []
    for j in range(4):
        sharded = [stack(lambda g: to_chips(g[k])[j]) for k in SHARDED_SMALL]
        small.append(_pack(sharded + [rep[j * quarter:(j + 1) * quarter]], 2, SUBL))
    parts.append(jnp.stack(small, axis=1))
    red = reduce_to_shards(parts)
    out = {"w_in": red[0], "w_branch": red[1].reshape(w["w_branch"].shape), "w_out": red[2], "s5_w_glu": red[3]}
    small_flat = red[4].reshape(-1)
    n_small = sum(int(np.prod(w[k].shape)) for k in SHARDED_SMALL)
    out.update(zip(SHARDED_SMALL, _unpack(small_flat, [w[k].shape for k in SHARDED_SMALL])))
    mine = small_flat[n_small:n_small + quarter].reshape(2, quarter // (2 * PACK_W), PACK_W)
    rep_all = gather_chips("ag_small_grads", [mine])[0].reshape(-1)
    names = REPLICATED + ("final_norm_w",)
    out.update(zip(names, _unpack(rep_all, [w[k].shape for k in names])))
    return out


def _update(w, g, m, v):
    d, nm, nv = {}, {}, {}
    flat2 = lambda a: a.reshape(-1, a.shape[-1])
    for k in SHARDED_BIG:
        res = adamw("adamw_" + k, *[flat2(t[k]) for t in (w, g, m, v)])
        d[k], nm[k], nv[k] = (r.reshape(w[k].shape) for r in res)
    rest = [k for k in WEIGHTS if k not in SHARDED_BIG]
    packs = [_pack([t[k] for k in rest], 1, ROW_BLOCK)[0] for t in (w, g, m, v)]
    res = adamw("adamw_rest", *packs)
    for tgt, r in zip((d, nm, nv), res):
        tgt.update(zip(rest, _unpack(r.reshape(-1), [w[k].shape for k in rest])))
    return d, nm, nv


def kernel(x, norm_w, w_in, s5_lambda_re, s5_lambda_im, s5_b_re, s5_b_im, s5_c_re, s5_c_im, s5_d, s5_log_step, s5_w_glu, sgu_ln_w, sgu_ln_b, sgu_w, sgu_b, m2_conv_w, m2_conv_b, m2_dt_bias, m2_a_log, m2_d, m2_norm_w, sc_conv_w, merge_b, w_branch, w_out, final_norm_w, loss_target, m_norm_w, m_w_in, m_s5_lambda_re, m_s5_lambda_im, m_s5_b_re, m_s5_b_im, m_s5_c_re, m_s5_c_im, m_s5_d, m_s5_log_step, m_s5_w_glu, m_sgu_ln_w, m_sgu_ln_b, m_sgu_w, m_sgu_b, m_m2_conv_w, m_m2_conv_b, m_m2_dt_bias, m_m2_a_log, m_m2_d, m_m2_norm_w, m_sc_conv_w, m_merge_b, m_w_branch, m_w_out, m_final_norm_w, v_norm_w, v_w_in, v_s5_lambda_re, v_s5_lambda_im, v_s5_b_re, v_s5_b_im, v_s5_c_re, v_s5_c_im, v_s5_d, v_s5_log_step, v_s5_w_glu, v_sgu_ln_w, v_sgu_ln_b, v_sgu_w, v_sgu_b, v_m2_conv_w, v_m2_conv_b, v_m2_dt_bias, v_m2_a_log, v_m2_d, v_m2_norm_w, v_sc_conv_w, v_merge_b, v_w_branch, v_w_out, v_final_norm_w):
    given = dict(locals())
    w = {k: given[k] for k in WEIGHTS}
    m = {k: given["m_" + k] for k in WEIGHTS}
    v = {k: given["v_" + k] for k in WEIGHTS}
    layers = _gather_weights(w)
    loss, dx, grads, d_final = local_step(x[0], loss_target[0], layers, final_norm_w)
    loss = lax.psum(loss, ("x", "y", "c"))
    g = _reduce_grads(grads, d_final, w)
    d, nm, nv = _update(w, g, m, v)
    return (loss, dx[None], *[g[k] for k in WEIGHTS], *[d[k] for k in WEIGHTS],
            *[nm[k] for k in WEIGHTS], *[nv[k] for k in WEIGHTS])
```

```python
import functools
from typing import Any, Callable, NamedTuple

import numpy as np
import jax
import jax.numpy as jnp
from jax import lax
from jax.experimental import pallas as pl
from jax.experimental.pallas import tpu as pltpu

f32 = jnp.float32
bf16 = jnp.bfloat16

D_MODEL = 1024
BW = 512
N_BRANCH = 4
EPS = 1e-6
S5_GROUPS, S5_P, S5_N = 32, 16, 64
S5_NS = S5_GROUPS * S5_N
CHUNK = 128
M2_HEADS, M2_HEAD_DIM, M2_GROUPS, M2_STATE = 8, 64, 2, 128
IN_DIM = 10248
PW = 10368
PW_MAIN = 10240
LANES = 128
VMEM_LIMIT = 60 * 1024 * 1024

ADAM_LR, ADAM_B1, ADAM_B2, ADAM_EPS, ADAM_WD, ADAM_STEP = 0.001, 0.9, 0.999, 1e-08, 0.01, 10

C_MERGE = 0
C_SC = 4096
C_SGU = 6144
C_S5G = 7680
C_S5U = 8192
C_M2Z = 8704
C_XBC = 9216
C_DT = 10240


def _col_segments():
    segs = [(6152, 4096)]
    for j in range(4):
        segs += [(4104 + 128 * j, 128), (4616 + 128 * j, 128), (5128 + 128 * j, 128), (5640 + 128 * j, 128)]
    segs += [(1024, 1536), (512, 512), (0, 512), (2560, 512), (3072, 1024), (4096, 8)]
    return segs


def _to_kernel_cols(w):
    parts = [w[:, s:s + n] for s, n in _col_segments()]
    parts.append(jnp.zeros((w.shape[0], PW - IN_DIM), w.dtype))
    return jnp.concatenate(parts, axis=1)


def _from_kernel_cols(wp):
    out, pos = {}, 0
    for s, n in _col_segments():
        out[s] = wp[:, pos:pos + n]
        pos += n
    return jnp.concatenate([out[s] for s in sorted(out)], axis=1)


NN = ((1,), (0,))
NT = ((1,), (1,))
TN = ((0,), (0,))


def _bd(a, b, dims):
    return lax.dot_general(a.astype(bf16), b.astype(bf16), (dims, ((), ())), preferred_element_type=f32)


def _hd(a, b, dims):
    return lax.dot_general(a, b, (dims, ((), ())), precision=lax.Precision.HIGHEST, preferred_element_type=f32)


def _make_dots(raw):
    @jax.custom_vjp
    def nn(a, b):
        return raw(a, b, NN)
    nn.defvjp(lambda a, b: (raw(a, b, NN), (a, b)), lambda r, g: (raw(g, r[1], NT), raw(r[0], g, TN)))

    @jax.custom_vjp
    def nt(a, b):
        return raw(a, b, NT)
    nt.defvjp(lambda a, b: (raw(a, b, NT), (a, b)), lambda r, g: (raw(g, r[1], NN), raw(g, r[0], TN)))

    @jax.custom_vjp
    def tn(a, b):
        return raw(a, b, TN)
    tn.defvjp(lambda a, b: (raw(a, b, TN), (a, b)), lambda r, g: (raw(r[1], g, NT), raw(r[0], g, NN)))
    return nn, nt, tn


bdot, bdot_nt, bdot_tn = _make_dots(_bd)
hdot, hdot_nt, hdot_tn = _make_dots(_hd)


@jax.custom_vjp
def bdot_w(a, w, shadow):
    return _bd(a, w, NN)


bdot_w.defvjp(lambda a, w, s: (_bd(a, w, NN), (a, w)),
              lambda r, g: (_bd(g, r[1], NT), jnp.zeros_like(r[1]), _bd(r[0], g, TN)))


def _rows(shape):
    return lax.broadcasted_iota(jnp.int32, shape, 0)


def _cols(shape):
    return lax.broadcasted_iota(jnp.int32, shape, 1)


def _shift_down(x, s):
    return jnp.where(_rows(x.shape) < s, 0.0, pltpu.roll(x, s, 0))


def _shift_up(x, s):
    n = x.shape[0]
    return jnp.where(_rows(x.shape) >= n - s, 0.0, pltpu.roll(x, n - s, 0))


@functools.partial(jax.custom_vjp, nondiff_argnums=(1,))
def shift(x, s):
    return _shift_down(x, s) if s else x


shift.defvjp(lambda x, s: (shift(x, s), None), lambda s, _, g: (_shift_up(g, s) if s else g,))


def _row_of(w, k):
    return jnp.sum(jnp.where(_rows(w.shape) == k, w, 0.0), axis=0, keepdims=True)


def _lane_mask(width, lo, hi):
    c = _cols((1, width))
    return ((c >= lo) & (c < hi)).astype(f32)


def _expand(rows, width, per):
    return (_cols((rows, width)) // per == _rows((rows, width))).astype(f32)


class A(NamedTuple):
    arr: Any
    block: tuple
    imap: Callable
    shadow: bool = False


class O(NamedTuple):
    shape: tuple
    dtype: Any
    block: tuple
    imap: Callable
    alias: Any = None


class R(NamedTuple):
    arg: int
    out: int
    off: Any = None
    acc: bool = False


def _cparams(n_grid):
    return pltpu.CompilerParams(dimension_semantics=("arbitrary",) * n_grid, vmem_limit_bytes=VMEM_LIMIT)


def _ispec(block, imap, n, rev):
    if rev:
        return pl.BlockSpec(block, lambda i: imap(n - 1 - i))
    return pl.BlockSpec(block, imap)


def _load(ref, a):
    v = ref[...]
    if a.shadow:
        return (v, jnp.zeros(v.shape, f32))
    return v.astype(f32)


def _save_spec(shape, n, rev):
    nd = len(shape)
    return _ispec((None,) + tuple(shape), lambda i: (i,) + (0,) * nd, n, rev)


def block_fwd(name, f, n, args, outs, carries=()):
    n_in, n_out, n_c = len(args), len(outs), len(carries)

    def body(*refs):
        ins, out_r = refs[:n_in], refs[n_in:n_in + n_out]
        saves, cs = refs[n_in + n_out:n_in + n_out + n_c], refs[n_in + n_out + n_c:]
        if n_c:
            @pl.when(pl.program_id(0) == 0)
            def _():
                for c in cs:
                    c[...] = jnp.zeros(c.shape, f32)
        vals = [_load(r, a) for r, a in zip(ins, args)]
        cv = [c[...] for c in cs]
        for s, v in zip(saves, cv):
            s[...] = v
        res = f(*vals, *cv)
        for r, v in zip(out_r, res[:n_out]):
            r[...] = v.astype(r.dtype)
        for c, v in zip(cs, res[n_out:]):
            c[...] = v

    out_shape = [jax.ShapeDtypeStruct(o.shape, o.dtype) for o in outs]
    out_specs = [pl.BlockSpec(o.block, o.imap) for o in outs]
    for shp in carries:
        out_shape.append(jax.ShapeDtypeStruct((n,) + tuple(shp), f32))
        out_specs.append(_save_spec(shp, n, False))
    return pl.pallas_call(
        body, name=name, grid=(n,),
        in_specs=[pl.BlockSpec(a.block, a.imap) for a in args],
        out_specs=out_specs, out_shape=out_shape,
        scratch_shapes=[pltpu.VMEM(tuple(shp), f32) for shp in carries],
        compiler_params=_cparams(1),
    )(*[a.arr for a in args])


def block_bwd(name, f, n, args, cots, gouts, routes, saved=(), rev=False):
    n_in, n_cot, n_c, n_go = len(args), len(cots), len(saved), len(gouts)
    diff = []
    for r in routes:
        if r.arg not in diff:
            diff.append(r.arg)
    aliases = [(k, o.alias) for k, o in enumerate(gouts) if o.alias is not None]

    def body(*refs):
        ins = refs[:n_in]
        cot_r = refs[n_in:n_in + n_cot]
        sav_r = refs[n_in + n_cot:n_in + n_cot + n_c]
        base = n_in + n_cot + n_c + len(aliases)
        go_r = refs[base:base + n_go]
        dcs = refs[base + n_go:]
        step = pl.program_id(0)
        if n_c:
            @pl.when(step == 0)
            def _():
                for d in dcs:
                    d[...] = jnp.zeros(d.shape, f32)
        vals = [_load(r, a) for r, a in zip(ins, args)]
        cv = [s[...] for s in sav_r]
        nd = len(diff)

        def g(*dv):
            full = list(vals)
            for idx, v in zip(diff, dv[:nd]):
                full[idx] = (vals[idx][0], v) if args[idx].shadow else v
            return tuple(f(*full, *dv[nd:]))

        primals = [vals[i][1] if args[i].shadow else vals[i] for i in diff] + cv
        _, vjp_fn = jax.vjp(g, *primals)
        ct = tuple([r[...].astype(f32) for r in cot_r] + [d[...] for d in dcs])
        grads = vjp_fn(ct)
        for r in routes:
            gr = grads[diff.index(r.arg)]
            ref = go_r[r.out]
            if r.acc:
                @pl.when(step == 0)
                def _(ref=ref, gr=gr):
                    ref[...] = gr.astype(ref.dtype)

                @pl.when(step > 0)
                def _(ref=ref, gr=gr):
                    ref[...] += gr.astype(ref.dtype)
            elif r.off is None:
                ref[...] = gr.astype(ref.dtype)
            else:
                ref[:, r.off:r.off + gr.shape[1]] = gr.astype(ref.dtype)
        for d, gr in zip(dcs, grads[nd:]):
            d[...] = gr

    in_specs = [_ispec(a.block, a.imap, n, rev) for a in list(args) + list(cots)]
    in_specs += [_save_spec(s.shape[1:], n, rev) for s in saved]
    in_specs += [pl.BlockSpec(memory_space=pl.ANY) for _ in aliases]
    operands = [a.arr for a in list(args) + list(cots)] + list(saved) + [arr for _, arr in aliases]
    io_alias = {n_in + n_cot + n_c + j: k for j, (k, _) in enumerate(aliases)}
    return pl.pallas_call(
        body, name=name, grid=(n,),
        in_specs=in_specs,
        out_specs=[_ispec(o.block, o.imap, n, rev) for o in gouts],
        out_shape=[jax.ShapeDtypeStruct(o.shape, o.dtype) for o in gouts],
        scratch_shapes=[pltpu.VMEM(tuple(s.shape[1:]), f32) for s in saved],
        input_output_aliases=io_alias,
        compiler_params=_cparams(1),
    )(*operands)


class Op(NamedTuple):
    arr: Any
    row: int = 0
    col: int = 0


def mm(name, mode, pairs, m, n, tm, tn, out_dtype=f32, add=None, out=None, out_col=0):
    tm, tn = min(tm, m), min(tn, n)
    assert m % tm == 0 and n % tn == 0
    in_specs, operands = [], []
    for a, b, k, _ in pairs:
        if mode == TN:
            assert a.row % k == 0 and a.col % tm == 0
            in_specs.append(pl.BlockSpec((k, tm), lambda j, i, a=a, k=k: (a.row // k, i + a.col // tm)))
        else:
            assert a.col % k == 0 and a.row % tm == 0
            in_specs.append(pl.BlockSpec((tm, k), lambda j, i, a=a, k=k: (i + a.row // tm, a.col // k)))
        if mode == NT:
            assert b.col % k == 0 and b.row % tn == 0
            in_specs.append(pl.BlockSpec((tn, k), lambda j, i, b=b, k=k: (j + b.row // tn, b.col // k)))
        else:
            assert b.row % k == 0 and b.col % tn == 0
            in_specs.append(pl.BlockSpec((k, tn), lambda j, i, b=b, k=k: (b.row // k, j + b.col // tn)))
        operands += [a.arr, b.arr]
    n_p = len(pairs)
    if add is not None:
        assert add.col % tn == 0
        in_specs.append(pl.BlockSpec((tm, tn), lambda j, i: (i, j + add.col // tn)))
        operands.append(add.arr)
    io_alias = {}
    if out is not None:
        assert out_col % tn == 0
        in_specs.append(pl.BlockSpec(memory_space=pl.ANY))
        operands.append(out)
        io_alias = {len(operands) - 1: 0}
        out_shape = jax.ShapeDtypeStruct(out.shape, out.dtype)
    else:
        out_shape = jax.ShapeDtypeStruct((m, n), out_dtype)
    signs = [p[3] for p in pairs]

    def body(*refs):
        o = refs[-1]
        acc = None
        for p in range(n_p):
            t = _bd(refs[2 * p][...], refs[2 * p + 1][...], mode)
            t = t if signs[p] > 0 else -t
            acc = t if acc is None else acc + t
        if add is not None:
            acc = acc + refs[2 * n_p][...].astype(f32)
        o[...] = acc.astype(o.dtype)

    return pl.pallas_call(
        body, name=name, grid=(n // tn, m // tm),
        in_specs=in_specs,
        out_specs=pl.BlockSpec((tm, tn), lambda j, i: (i, j + out_col // tn)),
        out_shape=out_shape, input_output_aliases=io_alias,
        compiler_params=_cparams(2),
    )(*operands)


SCAN_LANES = 256
SUBL = 8


def _cmul(p, q):
    return (p[0] * q[0] - p[1] * q[1], p[0] * q[1] + p[1] * q[0])


def _powers(a):
    a2 = _cmul(a, a)
    a4 = _cmul(a2, a2)
    a6 = _cmul(a4, a2)
    return [a, a2, _cmul(a2, a), a4, _cmul(a4, a), a6, _cmul(a6, a), _cmul(a4, a4)]


def _table(pw, order, w):
    row = _rows((SUBL, w))
    re = sum(jnp.where(row == t, pw[k][0], 0.0) for t, k in enumerate(order))
    im = sum(jnp.where(row == t, pw[k][1], 0.0) for t, k in enumerate(order))
    return re, im


def _pick_row(x, t):
    return jnp.sum(jnp.where(_rows(x.shape) == t, x, 0.0), axis=0, keepdims=True)


def s5_scan_fwd(bu_re, bu_im, a_re, a_im):
    seq, ns = bu_re.shape
    w, nb = SCAN_LANES, ns // SCAN_LANES

    def body(b_re, b_im, ar, ai, s_re, s_im):
        a = (ar[...], ai[...])
        pw = _powers(a)
        tab = _table(pw, list(range(SUBL)), w)
        row = _rows((SUBL, w))

        def step(i, carry):
            t0 = pl.multiple_of(i * SUBL, SUBL)
            x = (b_re[pl.ds(t0, SUBL), :], b_im[pl.ds(t0, SUBL), :])
            for d, k in ((1, 0), (2, 1), (4, 3)):
                sh = (jnp.where(row < d, 0.0, pltpu.roll(x[0], d, 0)), jnp.where(row < d, 0.0, pltpu.roll(x[1], d, 0)))
                t = _cmul(pw[k], sh)
                x = (x[0] + t[0], x[1] + t[1])
            t = _cmul(tab, carry)
            x = (x[0] + t[0], x[1] + t[1])
            s_re[pl.ds(t0, SUBL), :] = x[0]
            s_im[pl.ds(t0, SUBL), :] = x[1]
            return (_pick_row(x[0], SUBL - 1), _pick_row(x[1], SUBL - 1))

        z = jnp.zeros((1, w), f32)
        lax.fori_loop(0, seq // SUBL, step, (z, z))

    strip = pl.BlockSpec((seq, w), lambda j: (0, j))
    lane = pl.BlockSpec((1, w), lambda j: (0, j))
    return pl.pallas_call(
        body, name="s5_scan_fwd", grid=(nb,),
        in_specs=[strip, strip, lane, lane],
        out_specs=[strip, strip],
        out_shape=[jax.ShapeDtypeStruct((seq, ns), f32)] * 2,
        compiler_params=_cparams(1),
    )(bu_re, bu_im, a_re, a_im)


def s5_scan_bwd(ds_re, ds_im, s_re, s_im, a_re, a_im):
    seq, ns = ds_re.shape
    w, nb = SCAN_LANES, ns // SCAN_LANES
    nblk = seq // SUBL

    def body(g_re, g_im, sr, si, ar, ai, l_re, l_im, da_re, da_im):
        a = (ar[...], -ai[...])
        pw = _powers(a)
        tab = _table(pw, [SUBL - 1 - t for t in range(SUBL)], w)
        row = _rows((SUBL, w))

        def step(kk, carry):
            c_re, c_im, acc_re, acc_im = carry
            i = nblk - 1 - kk
            t0 = pl.multiple_of(i * SUBL, SUBL)
            x = (g_re[pl.ds(t0, SUBL), :], g_im[pl.ds(t0, SUBL), :])
            for d, k in ((1, 0), (2, 1), (4, 3)):
                sh = (jnp.where(row >= SUBL - d, 0.0, pltpu.roll(x[0], SUBL - d, 0)),
                      jnp.where(row >= SUBL - d, 0.0, pltpu.roll(x[1], SUBL - d, 0)))
                t = _cmul(pw[k], sh)
                x = (x[0] + t[0], x[1] + t[1])
            t = _cmul(tab, (c_re, c_im))
            x = (x[0] + t[0], x[1] + t[1])
            l_re[pl.ds(t0, SUBL), :] = x[0]
            l_im[pl.ds(t0, SUBL), :] = x[1]
            tp = pl.multiple_of(jnp.maximum(i - 1, 0) * SUBL, SUBL)
            live = (i > 0).astype(f32)
            p_re = _pick_row(sr[pl.ds(tp, SUBL), :], SUBL - 1) * live
            p_im = _pick_row(si[pl.ds(tp, SUBL), :], SUBL - 1) * live
            sp_re = jnp.where(row == 0, p_re, pltpu.roll(sr[pl.ds(t0, SUBL), :], 1, 0))
            sp_im = jnp.where(row == 0, p_im, pltpu.roll(si[pl.ds(t0, SUBL), :], 1, 0))
            acc_re = acc_re + x[0] * sp_re + x[1] * sp_im
            acc_im = acc_im + x[1] * sp_re - x[0] * sp_im
            return (_pick_row(x[0], 0), _pick_row(x[1], 0), acc_re, acc_im)

        z1 = jnp.zeros((1, w), f32)
        z8 = jnp.zeros((SUBL, w), f32)
        _, _, acc_re, acc_im = lax.fori_loop(0, nblk, step, (z1, z1, z8, z8))
        da_re[...] = jnp.sum(acc_re, axis=0, keepdims=True)
        da_im[...] = jnp.sum(acc_im, axis=0, keepdims=True)

    strip = pl.BlockSpec((seq, w), lambda j: (0, j))
    lane = pl.BlockSpec((1, w), lambda j: (0, j))
    return pl.pallas_call(
        body, name="s5_scan_bwd", grid=(nb,),
        in_specs=[strip, strip, strip, strip, lane, lane],
        out_specs=[strip, strip, lane, lane],
        out_shape=[jax.ShapeDtypeStruct((seq, ns), f32)] * 2 + [jax.ShapeDtypeStruct((1, ns), f32)] * 2,
        compiler_params=_cparams(1),
    )(ds_re, ds_im, s_re, s_im, a_re, a_im)


def _rms(x, w):
    return x * lax.rsqrt(jnp.mean(x * x, axis=-1, keepdims=True) + EPS) * w


def f_rms(x, w):
    return (_rms(x, w),)


def f_rms_res(x, w):
    return (_rms(x, w), x)


def f_s5_prep(lam_re, lam_im, log_step, b_re, b_im):
    e = _expand(log_step.shape[1], S5_NS, S5_N)
    step = hdot(jnp.exp(log_step), e)
    mag = jnp.exp(lam_re * step)
    ab_re, ab_im = mag * jnp.cos(lam_im * step), mag * jnp.sin(lam_im * step)
    den = lam_re * lam_re + lam_im * lam_im
    nr = ab_re - 1.0
    coef_re = (nr * lam_re + ab_im * lam_im) / den
    coef_im = (ab_im * lam_re - nr * lam_im) / den
    bb_re, bb_im = coef_re * b_re - coef_im * b_im, coef_re * b_im + coef_im * b_re
    sel = (_rows((BW, S5_P)) % S5_P == _cols((BW, S5_P))).astype(f32)
    blk = _rows((BW, S5_NS)) // S5_P == _cols((BW, S5_NS)) // S5_N
    rows_bd = lambda t: jnp.where(blk, hdot(sel, t), 0.0)
    return (ab_re, ab_im, rows_bd(bb_re), rows_bd(bb_im))


def f_s5_c(c_re, c_im):
    sel_t = (_cols((S5_P, BW)) % S5_P == _rows((S5_P, BW))).astype(f32)
    blk_t = _rows((S5_NS, BW)) // S5_N == _cols((S5_NS, BW)) // S5_P
    cols_bd = lambda t: jnp.where(blk_t, hdot_tn(t, sel_t), 0.0)
    return (cols_bd(c_re), cols_bd(c_im))


def f_s5_act(y_lin, u, gate, d, w_glu):
    y = jax.nn.gelu(y_lin + d * u)
    y = y * jax.nn.sigmoid(bdot_w(y, *w_glu))
    return (y * jax.nn.silu(gate),)


def f_sgu(u, v, gate, ln_w, ln_b, *rest):
    w_s, b_pad = rest[:8], rest[8]
    t = u.shape[0]
    u32, v32 = jax.nn.gelu(u), jax.nn.gelu(v)
    mu = jnp.mean(v32, axis=-1, keepdims=True)
    var = jnp.mean(jnp.square(v32 - mu), axis=-1, keepdims=True)
    vn = (v32 - mu) * lax.rsqrt(var + EPS) * ln_w + ln_b
    tri = _rows((t, t)) >= _cols((t, t))
    s = hdot_tn(b_pad, _expand(LANES, BW, BW // 8))
    for h in range(8):
        s = s + bdot(jnp.where(tri, w_s[h], 0.0), vn) * _lane_mask(BW, 64 * h, 64 * h + 64)
    return (u32 * s * jax.nn.silu(gate),)


def f_m2_conv(x, w, b):
    return (sum(_row_of(w, k) * shift(x, 3 - k) for k in range(4)) + b,)


def f_sc(bg, cg, h, gate, w):
    z = cg * h
    conv = sum(_row_of(w, k) * shift(z, 2 - k) for k in range(3))
    return (bg * conv * jax.nn.silu(gate),)


def f_m2(z, xc, b0, b1, c0, c1, dt_raw, dt_bias, a_log, d_par, norm_w, st):
    q = z.shape[0]
    x = jax.nn.silu(xc)
    bm, cm = (jax.nn.silu(b0), jax.nn.silu(b1)), (jax.nn.silu(c0), jax.nn.silu(c1))
    dt = jax.nn.softplus(dt_raw + dt_bias)
    da = dt * (-jnp.exp(a_log))
    tri = _rows((q, q)) >= _cols((q, q))
    acs = hdot(tri.astype(f32), da)
    e = _expand(LANES, BW, M2_HEAD_DIM)
    dt_f, acs_f = hdot(dt, e), hdot(acs, e)
    last = _rows((q, BW)) == q - 1
    alast_f = jnp.sum(jnp.where(last, acs_f, 0.0), axis=0, keepdims=True)
    xdt = x * dt_f
    xdec = xdt * jnp.exp(alast_f - acs_f)
    acs_t = acs.T
    st_new = st * jnp.exp(alast_f)
    y_diag, y_off = 0.0, 0.0
    for g in range(M2_GROUPS):
        gm = _lane_mask(BW, 256 * g, 256 * g + 256)
        cb = bdot_nt(cm[g], bm[g])
        st_new = st_new + bdot_tn(bm[g], xdec * gm)
        y_off = y_off + bdot(cm[g], st) * gm
        for hh in range(M2_HEADS // M2_GROUPS):
            h = g * (M2_HEADS // M2_GROUPS) + hh
            col = jnp.sum(jnp.where(_cols((q, LANES)) == h, acs, 0.0), axis=1, keepdims=True)
            row = jnp.sum(jnp.where(_rows((LANES, q)) == h, acs_t, 0.0), axis=0, keepdims=True)
            decay = jnp.exp(jnp.where(tri, col - row, -1e30))
            y_diag = y_diag + bdot(cb * decay, xdt) * _lane_mask(BW, 64 * h, 64 * h + 64)
    d_f = sum(jnp.sum(jnp.where(_cols((1, LANES)) == h, d_par, 0.0), axis=1, keepdims=True)
              * _lane_mask(BW, 64 * h, 64 * h + 64) for h in range(M2_HEADS))
    y = y_diag + y_off * jnp.exp(acs_f) + d_f * x
    y = y * jax.nn.silu(z)
    return (_rms(y, norm_w), st_new)


def f_gate_mix(*v):
    bo, lg, mb = v[0:4], v[4:8], v[8:12]
    return (sum(jax.nn.sigmoid(lg[k] + mb[k]) * bo[k] for k in range(N_BRANCH)),)


def _full(shape):
    nd = len(shape)
    return dict(block=tuple(shape), imap=lambda i: (0,) * nd)


def _param(arr, shadow=False):
    return A(arr, tuple(arr.shape), lambda i, nd=arr.ndim: (0,) * nd, shadow)


def _tb(arr, t, width, colblk):
    return A(arr, (t, width), lambda i: (i, colblk))


def _strip(arr, seq, colblk0, stride=1):
    return A(arr, (seq, LANES), lambda j: (0, colblk0 + stride * j))


def _s5_prep_args(p):
    lam_re = p["s5_lambda_re"].reshape(1, S5_NS)
    lam_im = p["s5_lambda_im"].reshape(1, S5_NS)
    log_step = jnp.pad(p["s5_log_step"].reshape(1, S5_GROUPS), ((0, 0), (0, LANES - S5_GROUPS)))
    b_lanes = lambda b: jnp.transpose(b, (2, 0, 1)).reshape(S5_P, S5_NS)
    return [_param(v) for v in (lam_re, lam_im, log_step, b_lanes(p["s5_b_re"]), b_lanes(p["s5_b_im"]))]


def _s5_c_args(p):
    c_lanes = lambda c: jnp.transpose(c, (1, 0, 2)).reshape(S5_P, S5_NS)
    return [_param(c_lanes(p["s5_c_re"])), _param(c_lanes(p["s5_c_im"]))]


def layer_fwd(x, p):
    seq = x.shape[0]
    nt = seq // CHUNK
    t2 = 256
    sv = {}
    hb = block_fwd("rms_in", f_rms, seq // t2, [_tb(x, t2, D_MODEL, 0), _param(p["norm_w"].reshape(1, D_MODEL))],
                   [O((seq, D_MODEL), bf16, (t2, D_MODEL), lambda i: (i, 0))])[0]
    proj = mm("mm_in", NN, [(Op(hb), Op(p["w_in"]), D_MODEL, 1)], seq, PW, 512, 1152)
    whole = lambda shape, dt: O(shape, dt, shape, lambda i: (0, 0))
    ab_re, ab_im, bb_re, bb_im = block_fwd("s5_prep", f_s5_prep, 1, _s5_prep_args(p),
                                           [whole((1, S5_NS), f32)] * 2 + [whole((BW, S5_NS), bf16)] * 2)
    cc_re, cc_im = block_fwd("s5_c", f_s5_c, 1, _s5_c_args(p), [whole((S5_NS, BW), bf16)] * 2)
    bu_re = mm("mm_bu_re", NN, [(Op(proj, 0, C_S5U), Op(bb_re), BW, 1)], seq, S5_NS, 512, 1024)
    bu_im = mm("mm_bu_im", NN, [(Op(proj, 0, C_S5U), Op(bb_im), BW, 1)], seq, S5_NS, 512, 1024)
    s_re, s_im = s5_scan_fwd(bu_re, bu_im, ab_re, ab_im)
    y_lin = mm("mm_s5y", NN, [(Op(s_re), Op(cc_re), S5_NS, 1), (Op(s_im), Op(cc_im), S5_NS, -1)], seq, BW, 256, BW)
    s5_act_args = [_tb(y_lin, CHUNK, BW, 0), _tb(proj, CHUNK, BW, C_S5U // BW), _tb(proj, CHUNK, BW, C_S5G // BW),
                   _param(p["s5_d"].reshape(1, BW)), _param(p["s5_w_glu"], True)]
    out_bw = O((seq, BW), f32, (CHUNK, BW), lambda i: (i, 0))
    y_a = block_fwd("s5_act", f_s5_act, nt, s5_act_args, [out_bw])[0]
    y_b = block_fwd("sgu", f_sgu, nt, _sgu_args(proj, p), [out_bw])[0]
    xc = block_fwd("m2_conv", f_m2_conv, 8, _m2_conv_args(proj, p, seq),
                   [O((seq, 2 * BW), f32, (seq, LANES), lambda j: (0, j))])[0]
    y_c, st_saved = block_fwd("m2_ssd", f_m2, nt, _m2_args(proj, xc, p), [out_bw], carries=[(M2_STATE, BW)])
    y_d = block_fwd("sc", f_sc, 4, _sc_args(proj, p, seq), [O((seq, BW), f32, (seq, LANES), lambda j: (0, j))])[0]
    ys = [y_a, y_b, y_c, y_d]
    bo = [mm(f"mm_branch{k}", NN, [(Op(ys[k]), Op(p["w_branch"], k * BW, 0), BW, 1)], seq, D_MODEL, 512, D_MODEL)
          for k in range(N_BRANCH)]
    merged = block_fwd("gate_mix", f_gate_mix, seq // t2, _mix_args(bo, proj, p, t2),
                       [O((seq, D_MODEL), bf16, (t2, D_MODEL), lambda i: (i, 0))])[0]
    x_new = mm("mm_out", NN, [(Op(merged), Op(p["w_out"]), D_MODEL, 1)], seq, D_MODEL, 512, D_MODEL, add=Op(x))
    sv.update(x=x, hb=hb, proj=proj, ab=(ab_re, ab_im), bb=(bb_re, bb_im), cc=(cc_re, cc_im), s=(s_re, s_im), y_lin=y_lin,
              xc=xc, st=st_saved, ys=ys, bo=bo, merged=merged)
    return x_new, sv


def _sgu_args(proj, p):
    c0 = C_SGU // BW
    args = [_tb(proj, CHUNK, BW, c0), _tb(proj, CHUNK, BW, c0 + 1), _tb(proj, CHUNK, BW, c0 + 2),
            _param(p["sgu_ln_w"].reshape(1, BW)), _param(p["sgu_ln_b"].reshape(1, BW))]
    args += [A(p["sgu_w"], (None, CHUNK, CHUNK), lambda i, h=h: (h, 0, 0)) for h in range(8)]
    args.append(_param(jnp.pad(p["sgu_b"], ((0, LANES - 8), (0, 0)))))
    return args


def _m2_conv_args(proj, p, seq):
    return [_strip(proj, seq, C_XBC // LANES), A(p["m2_conv_w"], (4, LANES), lambda j: (0, j)),
            A(p["m2_conv_b"].reshape(1, 2 * BW), (1, LANES), lambda j: (0, j))]


def _pad_lanes(v):
    return jnp.pad(v.reshape(1, -1), ((0, 0), (0, LANES - v.size)))


def _m2_args(proj, xc, p):
    args = [_tb(proj, CHUNK, BW, C_M2Z // BW), _tb(xc, CHUNK, BW, 0)]
    args += [_tb(xc, CHUNK, LANES, 4 + k) for k in range(4)]
    args.append(_tb(proj, CHUNK, LANES, C_DT // LANES))
    args += [_param(_pad_lanes(p["m2_dt_bias"])), _param(_pad_lanes(p["m2_a_log"])), _param(_pad_lanes(p["m2_d"])),
             _param(p["m2_norm_w"].reshape(1, BW))]
    return args


def _sc_args(proj, p, seq):
    c0 = C_SC // LANES
    return [_strip(proj, seq, c0 + k, 4) for k in range(4)] + [A(p["sc_conv_w"], (3, LANES), lambda j: (0, j))]


def _mix_args(bo, proj, p, t):
    args = [_tb(bo[k], t, D_MODEL, 0) for k in range(N_BRANCH)]
    args += [_tb(proj, t, D_MODEL, k) for k in range(N_BRANCH)]
    mb = p["merge_b"].reshape(N_BRANCH, 1, D_MODEL)
    args += [A(mb, (None, 1, D_MODEL), lambda i, k=k: (k, 0, 0)) for k in range(N_BRANCH)]
    return args


def layer_bwd(d_out, p, sv):
    seq = d_out.shape[0]
    nt = seq // CHUNK
    t2 = 256
    proj, ys, bo = sv["proj"], sv["ys"], sv["bo"]
    g = {}
    acc = lambda shape: O(tuple(shape), f32, tuple(shape), lambda i, nd=len(shape): (0,) * nd)
    d_merged = mm("mm_out_dx", NT, [(Op(d_out), Op(p["w_out"]), D_MODEL, 1)], seq, D_MODEL, 512, D_MODEL, out_dtype=f32)
    g["w_out"] = mm("mm_out_dw", TN, [(Op(sv["merged"]), Op(d_out), seq, 1)], D_MODEL, D_MODEL, 256, D_MODEL, out_dtype=bf16)
    gouts = [O((seq, D_MODEL), bf16, (t2, D_MODEL), lambda i: (i, 0)) for _ in range(N_BRANCH)]
    gouts.append(O((seq, PW_MAIN), bf16, (t2, N_BRANCH * D_MODEL), lambda i: (i, 0)))
    gouts += [acc((1, D_MODEL)) for _ in range(N_BRANCH)]
    routes = [R(k, k) for k in range(N_BRANCH)] + [R(4 + k, 4, k * D_MODEL) for k in range(N_BRANCH)]
    routes += [R(8 + k, 5 + k, acc=True) for k in range(N_BRANCH)]
    res = block_bwd("gate_mix_bwd", f_gate_mix, seq // t2, _mix_args(bo, proj, p, t2),
                    [_tb(d_merged, t2, D_MODEL, 0)], gouts, routes)
    dbo, dproj = res[:4], res[4]
    g["merge_b"] = jnp.concatenate(res[5:9], axis=0)
    dys = [mm(f"mm_branch{k}_dx", NT, [(Op(dbo[k]), Op(p["w_branch"], k * BW, 0), D_MODEL, 1)], seq, BW, 512, BW)
           for k in range(N_BRANCH)]
    g["w_branch"] = jnp.stack([mm(f"mm_branch{k}_dw", TN, [(Op(ys[k]), Op(dbo[k]), seq, 1)], BW, D_MODEL, 256, D_MODEL, out_dtype=bf16)
                               for k in range(N_BRANCH)])
    res = block_bwd("sc_bwd", f_sc, 4, _sc_args(proj, p, seq), [_strip(dys[3], seq, 0)],
                    [O((seq, PW_MAIN), bf16, (seq, 4 * LANES), lambda j: (0, C_SC // (4 * LANES) + j), alias=dproj),
                     O((3, BW), f32, (3, LANES), lambda j: (0, j))],
                    [R(k, 0, k * LANES) for k in range(4)] + [R(4, 1)])
    dproj, g["sc_conv_w"] = res
    res = block_bwd("m2_ssd_bwd", f_m2, nt, _m2_args(proj, sv["xc"], p), [_tb(dys[2], CHUNK, BW, 0)],
                    [O((seq, PW_MAIN), bf16, (CHUNK, BW), lambda i: (i, C_M2Z // BW), alias=dproj),
                     O((seq, 2 * BW), f32, (CHUNK, 2 * BW), lambda i: (i, 0)),
                     O((seq, LANES), bf16, (CHUNK, LANES), lambda i: (i, 0)),
                     acc((1, LANES)), acc((1, LANES)), acc((1, LANES)), acc((1, BW))],
                    [R(0, 0), R(1, 1, 0)] + [R(2 + k, 1, BW + k * LANES) for k in range(4)] + [R(6, 2)]
                    + [R(7, 3, acc=True), R(8, 4, acc=True), R(9, 5, acc=True), R(10, 6, acc=True)],
                    saved=[sv["st"]], rev=True)
    dproj, dxc, d_dt = res[0], res[1], res[2]
    g["m2_dt_bias"], g["m2_a_log"], g["m2_d"] = (r[0, :M2_HEADS] for r in res[3:6])
    g["m2_norm_w"] = res[6].reshape(BW)
    res = block_bwd("m2_conv_bwd", f_m2_conv, 8, _m2_conv_args(proj, p, seq), [_strip(dxc, seq, 0)],
                    [O((seq, PW_MAIN), bf16, (seq, LANES), lambda j: (0, C_XBC // LANES + j), alias=dproj),
                     O((4, 2 * BW), f32, (4, LANES), lambda j: (0, j)), O((1, 2 * BW), f32, (1, LANES), lambda j: (0, j))],
                    [R(0, 0), R(1, 1), R(2, 2)])
    dproj, g["m2_conv_w"], cb = res
    g["m2_conv_b"] = cb.reshape(2 * BW)
    res = block_bwd("sgu_bwd", f_sgu, nt, _sgu_args(proj, p), [_tb(dys[1], CHUNK, BW, 0)],
                    [O((seq, PW_MAIN), bf16, (CHUNK, 3 * BW), lambda i: (i, C_SGU // (3 * BW)), alias=dproj),
                     acc((1, BW)), acc((1, BW))] + [acc((CHUNK, CHUNK)) for _ in range(8)] + [acc((LANES, CHUNK))],
                    [R(0, 0, 0), R(1, 0, BW), R(2, 0, 2 * BW), R(3, 1, acc=True), R(4, 2, acc=True)]
                    + [R(5 + h, 3 + h, acc=True) for h in range(8)] + [R(13, 11, acc=True)])
    dproj = res[0]
    g["sgu_ln_w"], g["sgu_ln_b"] = res[1].reshape(BW), res[2].reshape(BW)
    g["sgu_w"] = jnp.stack(res[3:11])
    g["sgu_b"] = res[11][:8]
    y_lin, (s_re, s_im), (ab_re, ab_im) = sv["y_lin"], sv["s"], sv["ab"]
    s5_act_args = [_tb(y_lin, CHUNK, BW, 0), _tb(proj, CHUNK, BW, C_S5U // BW), _tb(proj, CHUNK, BW, C_S5G // BW),
                   _param(p["s5_d"].reshape(1, BW)), _param(p["s5_w_glu"], True)]
    res = block_bwd("s5_act_bwd", f_s5_act, nt, s5_act_args, [_tb(dys[0], CHUNK, BW, 0)],
                    [O((seq, BW), bf16, (CHUNK, BW), lambda i: (i, 0)), O((seq, BW), f32, (CHUNK, BW), lambda i: (i, 0)),
                     O((seq, PW_MAIN), bf16, (CHUNK, BW), lambda i: (i, C_S5G // BW), alias=dproj),
                     acc((1, BW)), acc((BW, BW))],
                    [R(0, 0), R(1, 1), R(2, 2), R(3, 3, acc=True), R(4, 4, acc=True)])
    dy_lin, du1, dproj = res[0], res[1], res[2]
    g["s5_d"] = res[3].reshape(S5_GROUPS, S5_P)
    g["s5_w_glu"] = res[4]
    (bb_re, bb_im), (cc_re, cc_im) = sv["bb"], sv["cc"]
    ds_re = mm("mm_s5y_dre", NT, [(Op(dy_lin), Op(cc_re), BW, 1)], seq, S5_NS, 512, 1024)
    ds_im = mm("mm_s5y_dim", NT, [(Op(dy_lin), Op(cc_im), BW, -1)], seq, S5_NS, 512, 1024)
    dc_re = mm("mm_s5y_dcre", TN, [(Op(s_re), Op(dy_lin), seq, 1)], S5_NS, BW, 512, BW)
    dc_im = mm("mm_s5y_dcim", TN, [(Op(s_im), Op(dy_lin), seq, -1)], S5_NS, BW, 512, BW)
    l_re, l_im, da_re, da_im = s5_scan_bwd(ds_re, ds_im, s_re, s_im, ab_re, ab_im)
    dproj = mm("mm_bu_dx", NT, [(Op(l_re), Op(bb_re), S5_NS, 1), (Op(l_im), Op(bb_im), S5_NS, 1)],
               seq, BW, 512, BW, add=Op(du1), out=dproj, out_col=C_S5U)
    dbb = [mm(f"mm_bu_dw{n}", TN, [(Op(proj, 0, C_S5U), Op(l), seq, 1)], BW, S5_NS, 256, 1024)
           for n, l in (("re", l_re), ("im", l_im))]
    gouts = [acc((1, S5_NS)), acc((1, S5_NS)), acc((1, LANES))] + [acc((S5_P, S5_NS))] * 2
    res = block_bwd("s5_prep_bwd", f_s5_prep, 1, _s5_prep_args(p),
                    [_param(v) for v in (da_re, da_im, dbb[0], dbb[1])], gouts, [R(k, k, acc=True) for k in range(5)])
    g["s5_lambda_re"], g["s5_lambda_im"] = res[0].reshape(S5_GROUPS, S5_N), res[1].reshape(S5_GROUPS, S5_N)
    g["s5_log_step"] = res[2][0, :S5_GROUPS]
    b_natural = lambda b: jnp.transpose(b.reshape(S5_P, S5_GROUPS, S5_N), (1, 2, 0))
    c_natural = lambda c: jnp.transpose(c.reshape(S5_P, S5_GROUPS, S5_N), (1, 0, 2))
    g["s5_b_re"], g["s5_b_im"] = b_natural(res[3]), b_natural(res[4])
    res = block_bwd("s5_c_bwd", f_s5_c, 1, _s5_c_args(p), [_param(dc_re), _param(dc_im)],
                    [acc((S5_P, S5_NS))] * 2, [R(0, 0, acc=True), R(1, 1, acc=True)])
    g["s5_c_re"], g["s5_c_im"] = c_natural(res[0]), c_natural(res[1])
    hb, w_in = sv["hb"], p["w_in"]
    dh = mm("mm_in_dx", NT, [(Op(dproj), Op(w_in), PW_MAIN, 1), (Op(d_dt), Op(w_in, 0, PW_MAIN), LANES, 1)],
            seq, D_MODEL, 256, 256)
    g["w_in"] = (mm("mm_in_dw", TN, [(Op(hb), Op(dproj), seq, 1)], D_MODEL, PW_MAIN, 256, 1024, out_dtype=bf16),
                 mm("mm_in_dwdt", TN, [(Op(hb), Op(d_dt), seq, 1)], D_MODEL, LANES, 256, LANES, out_dtype=bf16))
    dx, dnw = block_bwd("rms_in_bwd", f_rms_res, seq // t2, [_tb(sv["x"], t2, D_MODEL, 0), _param(p["norm_w"].reshape(1, D_MODEL))],
                        [_tb(dh, t2, D_MODEL, 0), _tb(d_out, t2, D_MODEL, 0)],
                        [O((seq, D_MODEL), f32, (t2, D_MODEL), lambda i: (i, 0)), acc((1, D_MODEL))],
                        [R(0, 0), R(1, 1, acc=True)])
    g["norm_w"] = dnw.reshape(D_MODEL)
    return dx, g


def loss_head(x, w, target):
    seq = x.shape[0]
    t = 256

    def body(x_ref, w_ref, t_ref, loss_ref, dx_ref, dw_ref):
        step = pl.program_id(0)

        def f(xv, wv):
            err = _rms(xv, wv) - t_ref[...]
            return 0.5 * jnp.sum(jnp.mean(err * err, axis=-1, keepdims=True), axis=0, keepdims=True)

        val, vjp_fn = jax.vjp(f, x_ref[...], w_ref[...])
        dx, dw = vjp_fn(jnp.ones((1, 1), f32))
        dx_ref[...] = dx

        @pl.when(step == 0)
        def _():
            loss_ref[...] = jnp.broadcast_to(val, loss_ref.shape)
            dw_ref[...] = dw

        @pl.when(step > 0)
        def _():
            loss_ref[...] += jnp.broadcast_to(val, loss_ref.shape)
            dw_ref[...] += dw

    blk = pl.BlockSpec((t, D_MODEL), lambda i: (i, 0))
    row = pl.BlockSpec((1, D_MODEL), lambda i: (0, 0))
    return pl.pallas_call(
        body, name="loss_head", grid=(seq // t,),
        in_specs=[blk, row, blk],
        out_specs=[pl.BlockSpec((1, LANES), lambda i: (0, 0)), blk, row],
        out_shape=[jax.ShapeDtypeStruct((1, LANES), f32), jax.ShapeDtypeStruct((seq, D_MODEL), f32),
                   jax.ShapeDtypeStruct((1, D_MODEL), f32)],
        compiler_params=_cparams(1),
    )(x, w.reshape(1, D_MODEL), target)


LAYER_KEYS = ("norm_w", "w_in", "s5_lambda_re", "s5_lambda_im", "s5_b_re", "s5_b_im", "s5_c_re", "s5_c_im", "s5_d",
              "s5_log_step", "s5_w_glu", "sgu_ln_w", "sgu_ln_b", "sgu_w", "sgu_b", "m2_conv_w", "m2_conv_b",
              "m2_dt_bias", "m2_a_log", "m2_d", "m2_norm_w", "sc_conv_w", "merge_b", "w_branch", "w_out")


def local_step(x, target, layers, final_norm_w):
    saved = []
    for p in layers:
        x, sv = layer_fwd(x, p)
        saved.append(sv)
    loss, dx, dfw = loss_head(x, final_norm_w, target)
    grads = []
    for p, sv in zip(reversed(layers), reversed(saved)):
        dx, g = layer_bwd(dx, p, sv)
        grads.append(g)
    return loss[0, 0], dx, grads[::-1], dfw.reshape(D_MODEL)


MESH = pl.DeviceIdType.MESH
ANY = pl.BlockSpec(memory_space=pl.ANY)


def _me():
    return lax.axis_index("x"), lax.axis_index("y"), lax.axis_index("c")


def _other_chips(x, y):
    return [(1 - x, y), (x, 1 - y), (1 - x, 1 - y)]


def _rcopy(src, dst, send, recv, dev):
    return pltpu.make_async_remote_copy(src_ref=src, dst_ref=dst, send_sem=send, recv_sem=recv,
                                        device_id=dev, device_id_type=MESH)


def _comm_call(name, body, arrs, out_shape, n_remote, aliases=None):
    n = len(arrs)
    return pl.pallas_call(
        body, name=name, in_specs=[ANY] * n, out_specs=[ANY] * len(out_shape), out_shape=out_shape,
        scratch_shapes=[pltpu.SemaphoreType.DMA((n, n_remote)), pltpu.SemaphoreType.DMA((n, n_remote))],
        input_output_aliases=aliases or {},
        compiler_params=pltpu.CompilerParams(has_side_effects=True),
    )(*arrs)


def gather_chips(name, arrs):
    n = len(arrs)

    def body(*refs):
        ins, outs = refs[:n], refs[n:2 * n]
        send, recv = refs[2 * n:]
        x, y, c = _me()
        jme = 2 * x + y
        chips = _other_chips(x, y)
        first = [_rcopy(ins[a].at[c], outs[a].at[jme, c], send.at[a, k], recv.at[a, k], (cx, cy, c))
                 for a in range(n) for k, (cx, cy) in enumerate(chips)]
        for cp in first:
            cp.start()
        passed = []
        for k, (cx, cy) in enumerate(chips):
            jk = 2 * cx + cy
            for a in range(n):
                _rcopy(ins[a].at[c], outs[a].at[jk, c], send.at[a, k], recv.at[a, k], (cx, cy, c)).wait_recv()
                fw = _rcopy(outs[a].at[jk, c], outs[a].at[jk, c], send.at[a, 3 + k], recv.at[a, 3 + k], (x, y, 1 - c))
                fw.start()
                passed.append(fw)
        for k, (cx, cy) in enumerate(chips):
            jk = 2 * cx + cy
            for a in range(n):
                _rcopy(outs[a].at[jk, 1 - c], outs[a].at[jk, 1 - c], send.at[a, 3 + k], recv.at[a, 3 + k],
                       (x, y, 1 - c)).wait_recv()
        for cp in first + passed:
            cp.wait_send()

    out_shape = [jax.ShapeDtypeStruct((4,) + a.shape, a.dtype) for a in arrs]
    got = _comm_call(name, body, arrs, out_shape, 6)
    jme = 2 * lax.axis_index("x") + lax.axis_index("y")
    return [lax.dynamic_update_index_in_dim(g, a, jme, 0) for g, a in zip(got, arrs)]


def swap_halves(name, arrs):
    n = len(arrs)

    def body(*refs):
        ins, outs = refs[:n], refs[n:2 * n]
        send, recv = refs[2 * n:]
        x, y, c = _me()
        remote = [_rcopy(ins[a].at[1 - c], outs[a], send.at[a, 0], recv.at[a, 0], (x, y, 1 - c)) for a in range(n)]
        for cp in remote:
            cp.start()
        for cp in remote:
            cp.wait()

    return _comm_call(name, body, arrs, [jax.ShapeDtypeStruct(a.shape[1:], a.dtype) for a in arrs], 1)


def exchange_chips(name, arrs):
    n = len(arrs)

    def body(*refs):
        ins, outs = refs[:n], refs[n:2 * n]
        send, recv = refs[2 * n:]
        x, y, c = _me()
        remote = [_rcopy(ins[a].at[2 * cx + cy], outs[a].at[k], send.at[a, k], recv.at[a, k], (cx, cy, c))
                  for a in range(n) for k, (cx, cy) in enumerate(_other_chips(x, y))]
        for cp in remote:
            cp.start()
        for cp in remote:
            cp.wait()

    return _comm_call(name, body, arrs, [jax.ShapeDtypeStruct((3,) + a.shape[1:], a.dtype) for a in arrs], 3)


def gather_cores(name, arrs):
    n = len(arrs)

    def body(*refs):
        bufs = refs[n:2 * n]
        send, recv = refs[2 * n:]
        x, y, c = _me()
        remote = [_rcopy(bufs[a].at[c], bufs[a].at[c], send.at[a, 0], recv.at[a, 0], (x, y, 1 - c)) for a in range(n)]
        for cp in remote:
            cp.start()
        for a in range(n):
            _rcopy(bufs[a].at[1 - c], bufs[a].at[1 - c], send.at[a, 0], recv.at[a, 0], (x, y, 1 - c)).wait_recv()
        for cp in remote:
            cp.wait_send()

    return _comm_call(name, body, arrs, [jax.ShapeDtypeStruct(a.shape, a.dtype) for a in arrs], 1,
                      aliases={a: a for a in range(n)})


ROW_BLOCK = 256


def esum(name, terms, rows, width, out_dtype, out_slots=None):
    tr = next((t for t in range(min(rows, ROW_BLOCK), 0, -SUBL) if rows % t == 0 and t % SUBL == 0), rows)
    where =jnp.stack([lax.axis_index("c"), 2 * lax.axis_index("x") + lax.axis_index("y")]).astype(jnp.int32)
    pick = {"c": 0, "j": 1}

    def body(s_ref, *refs):
        acc = refs[0][...].astype(f32)
        for r in refs[1:-1]:
            acc = acc + r[...].astype(f32)
        refs[-1][...] = acc.astype(out_dtype)

    specs = []
    for arr, lead in terms:
        if lead is None:
            specs.append(pl.BlockSpec((tr, width), lambda i, s: (i, 0)))
        elif isinstance(lead, str):
            specs.append(pl.BlockSpec((None, tr, width), lambda i, s, lead=lead: (s[pick[lead]], i, 0)))
        else:
            specs.append(pl.BlockSpec((None, tr, width), lambda i, s, lead=lead: (lead, i, 0)))
    if out_slots is None:
        out_spec = pl.BlockSpec((tr, width), lambda i, s: (i, 0))
        out_shape = jax.ShapeDtypeStruct((rows, width), out_dtype)
    else:
        out_spec = pl.BlockSpec((None, tr, width), lambda i, s: (s[0], i, 0))
        out_shape = jax.ShapeDtypeStruct((out_slots, rows, width), out_dtype)
    return pl.pallas_call(
        body, name=name,
        grid_spec=pltpu.PrefetchScalarGridSpec(num_scalar_prefetch=1, grid=(rows // tr,), in_specs=specs, out_specs=out_spec),
        out_shape=out_shape, compiler_params=_cparams(1),
    )(where, *[t[0] for t in terms])


def reduce_to_shards(parts):
    tags = [str(k) for k in range(len(parts))]
    theirs = swap_halves("rs_swap", parts)
    t1 = []
    for tag, p, th in zip(tags, parts, theirs):
        _, _, h, w = p.shape
        t1.append(esum("rs_add_cores" + tag, [(p.reshape(2, 4 * h, w), "c"), (th.reshape(4 * h, w), None)],
                       4 * h, w, p.dtype).reshape(4, h, w))
    landed = exchange_chips("rs_exchange", t1)
    red = []
    for tag, p, t, got in zip(tags, parts, t1, landed):
        _, _, h, w = p.shape
        red.append(esum("rs_add_chips" + tag, [(t, "j"), (got, 0), (got, 1), (got, 2)], h, w, f32, out_slots=2))
    return gather_cores("rs_gather", red)


def _adamw_step(w_ref, g_ref, m_ref, v_ref, d_ref, nm_ref, nv_ref):
    gv = g_ref[...]
    nm = ADAM_B1 * m_ref[...] + (1.0 - ADAM_B1) * gv
    nv = ADAM_B2 * v_ref[...] + (1.0 - ADAM_B2) * jnp.square(gv)
    m_hat = nm / (1.0 - ADAM_B1 ** ADAM_STEP)
    v_hat = nv / (1.0 - ADAM_B2 ** ADAM_STEP)
    d_ref[...] = -ADAM_LR * (m_hat / (jnp.sqrt(v_hat) + ADAM_EPS) + ADAM_WD * w_ref[...])
    nm_ref[...] = nm
    nv_ref[...] = nv


def adamw(name, w, g, m, v):
    rows, width = w.shape
    tr = ROW_BLOCK if rows % ROW_BLOCK == 0 else rows

    def body(*refs):
        _adamw_step(*refs)

    spec = pl.BlockSpec((tr, width), lambda i: (i, 0))
    return pl.pallas_call(
        body, name=name, grid=(rows // tr,), in_specs=[spec] * 4, out_specs=[spec] * 3,
        out_shape=[jax.ShapeDtypeStruct((rows, width), f32)] * 3, compiler_params=_cparams(1),
    )(w, g, m, v)


def adamw_many(name, ws, gs, ms, vs):
    n = len(ws)

    def body(*refs):
        ins, outs = refs[:4 * n], refs[4 * n:]
        for k in range(n):
            _adamw_step(ins[k], ins[n + k], ins[2 * n + k], ins[3 * n + k], outs[k], outs[n + k], outs[2 * n + k])

    vmem = pl.BlockSpec(memory_space=pltpu.VMEM)
    res = pl.pallas_call(
        body, name=name, in_specs=[vmem] * (4 * n), out_specs=[vmem] * (3 * n),
        out_shape=[jax.ShapeDtypeStruct(w.shape, f32) for w in ws] * 3,
        compiler_params=pltpu.CompilerParams(vmem_limit_bytes=VMEM_LIMIT),
    )(*ws, *gs, *ms, *vs)
    return res[:n], res[n:2 * n], res[2 * n:]


PACK_W = 1024


def _pack(parts, halves, row_mult):
    flat = jnp.concatenate([p.reshape(-1) for p in parts])
    per = halves * row_mult * PACK_W
    total = -(-flat.size // per) * per
    flat = jnp.pad(flat, (0, total - flat.size))
    return flat.reshape(halves, total // (halves * PACK_W), PACK_W)


def _unpack(flat, shapes):
    out, pos = [], 0
    for s in shapes:
        n = int(np.prod(s))
        out.append(flat[pos:pos + n].reshape(s))
        pos += n
    return out


SHARDED_BIG = ("w_in", "w_branch", "w_out", "s5_w_glu")
SHARDED_SMALL = ("m2_conv_w", "sc_conv_w", "merge_b")
SHARD_AXIS = {"w_in": 2, "w_branch": 3, "w_out": 1, "s5_w_glu": 1, "m2_conv_w": 2, "sc_conv_w": 2, "merge_b": 2}
REPLICATED = ("norm_w", "s5_lambda_re", "s5_lambda_im", "s5_b_re", "s5_b_im", "s5_c_re", "s5_c_im", "s5_d", "s5_log_step",
              "sgu_ln_w", "sgu_ln_b", "sgu_w", "sgu_b", "m2_conv_b", "m2_dt_bias", "m2_a_log", "m2_d", "m2_norm_w")
WEIGHTS = ("norm_w", "w_in", "s5_lambda_re", "s5_lambda_im", "s5_b_re", "s5_b_im", "s5_c_re", "s5_c_im", "s5_d",
           "s5_log_step", "s5_w_glu", "sgu_ln_w", "sgu_ln_b", "sgu_w", "sgu_b", "m2_conv_w", "m2_conv_b", "m2_dt_bias",
           "m2_a_log", "m2_d", "m2_norm_w", "sc_conv_w", "merge_b", "w_branch", "w_out", "final_norm_w")
N_LAYERS = 2


SHARD_W = IN_DIM // 4


def _kernel_pieces():
    out, pos = [], 0
    for s, n in _col_segments():
        while n:
            take = min(n, SHARD_W - s % SHARD_W)
            out.append((pos, s, take))
            pos, s, n = pos + take, s + take, n - take
    return out


def _kernel_cols_from_shards(blocks):
    parts = [blocks[s // SHARD_W][:, s % SHARD_W:s % SHARD_W + n] for _, s, n in _kernel_pieces()]
    parts.append(jnp.zeros((blocks.shape[1], PW - IN_DIM), blocks.dtype))
    return jnp.concatenate(parts, axis=1)


def _shards_from_kernel_cols(main, dt, dtype):
    blocks = []
    for j in range(4):
        mine = sorted((s, pos, n) for pos, s, n in _kernel_pieces() if s // SHARD_W == j)
        parts = [(main[:, pos:pos + n] if pos < PW_MAIN else dt[:, pos - PW_MAIN:pos - PW_MAIN + n]).astype(dtype)
                 for _, pos, n in mine]
        blocks.append(jnp.concatenate(parts, axis=1))
    return jnp.stack(blocks)


def _gather_weights(w):
    arrs = [w["w_in"].astype(bf16), w["w_branch"].reshape(N_LAYERS, N_BRANCH * BW, -1).astype(bf16),
            w["w_out"].astype(bf16), w["s5_w_glu"].astype(bf16), w["m2_conv_w"], w["sc_conv_w"], w["merge_b"]]
    got = gather_chips("ag_weights", arrs)
    cols = lambda t: jnp.transpose(t, (1, 0, 2)).reshape(t.shape[1], -1)
    layers = []
    for i in range(N_LAYERS):
        p = {k: w[k][i] for k in REPLICATED}
        p["w_in"] = _kernel_cols_from_shards(got[0][:, i])
        p["w_branch"] = cols(got[1][:, i])
        p["w_out"] = got[2][:, i].reshape(D_MODEL, D_MODEL)
        p["s5_w_glu"] = got[3][:, i].reshape(BW, BW)
        p["m2_conv_w"], p["sc_conv_w"], p["merge_b"] = cols(got[4][:, i]), cols(got[5][:, i]), cols(got[6][:, i])
        layers.append(p)
    return layers


def _reduce_grads(grads, d_final, w):
    to_chips = lambda t: jnp.transpose(t.reshape(t.shape[0], 4, -1), (1, 0, 2))
    stack = lambda f: jnp.stack([f(g) for g in grads])
    parts = [stack(lambda g: _shards_from_kernel_cols(g["w_in"][0], g["w_in"][1], bf16)),
             stack(lambda g: to_chips(g["w_branch"].reshape(N_BRANCH * BW, D_MODEL)).astype(bf16)),
             stack(lambda g: g["w_out"].reshape(4, D_MODEL // 4, D_MODEL).astype(bf16)),
             stack(lambda g: g["s5_w_glu"].reshape(4, BW // 4, BW).astype(bf16))]
    rep = jnp.concatenate([stack(lambda g: g[k]).reshape(-1) for k in REPLICATED] + [d_final.reshape(-1)])
    quarter = -(-rep.size // (4 * 2 * SUBL * PACK_W)) * (2 * SUBL * PACK_W)
    rep = jnp.pad(rep, (0, 4 * quarter - rep.size))
    small = []
    for j in range(4):
        sharded = [stack(lambda g: to_chips(g[k])[j]) for k in SHARDED_SMALL]
        small.append(_pack(sharded + [rep[j * quarter:(j + 1) * quarter]], 2, SUBL))
    parts.append(jnp.stack(small, axis=1))
    red = reduce_to_shards(parts)
    out = {"w_in": red[0], "w_branch": red[1].reshape(w["w_branch"].shape), "w_out": red[2], "s5_w_glu": red[3]}
    small_flat = red[4].reshape(-1)
    n_small = sum(int(np.prod(w[k].shape)) for k in SHARDED_SMALL)
    out.update(zip(SHARDED_SMALL, _unpack(small_flat, [w[k].shape for k in SHARDED_SMALL])))
    mine = small_flat[n_small:n_small + quarter].reshape(2, quarter // (2 * PACK_W), PACK_W)
    rep_all = gather_chips("ag_small_grads", [mine])[0].reshape(-1)
    names = REPLICATED + ("final_norm_w",)
    out.update(zip(names, _unpack(rep_all, [w[k].shape for k in names])))
    return out


def _update(w, g, m, v):
    d, nm, nv = {}, {}, {}
    flat2 = lambda a: a.reshape(-1, a.shape[-1])
    for k in SHARDED_BIG:
        res = adamw("adamw_" + k, *[flat2(t[k]) for t in (w, g, m, v)])
        d[k], nm[k], nv[k] = (r.reshape(w[k].shape) for r in res)
    for k in ("s5_b_re", "s5_b_im"):
        res = adamw("adamw_" + k, *[flat2(t[k]) for t in (w, g, m, v)])
        d[k], nm[k], nv[k] = (r.reshape(w[k].shape) for r in res)
    rest = [k for k in WEIGHTS if k not in SHARDED_BIG + ("s5_b_re", "s5_b_im")]
    two_d = lambda a: a.reshape(1, -1) if a.ndim == 1 else a
    res = adamw_many("adamw_rest", *[[two_d(t[k]) for k in rest] for t in (w, g, m, v)])
    for tgt, rs in zip((d, nm, nv), res):
        tgt.update({k: r.reshape(w[k].shape) for k, r in zip(rest, rs)})
    return d, nm, nv


def kernel(x, norm_w, w_in, s5_lambda_re, s5_lambda_im, s5_b_re, s5_b_im, s5_c_re, s5_c_im, s5_d, s5_log_step, s5_w_glu, sgu_ln_w, sgu_ln_b, sgu_w, sgu_b, m2_conv_w, m2_conv_b, m2_dt_bias, m2_a_log, m2_d, m2_norm_w, sc_conv_w, merge_b, w_branch, w_out, final_norm_w, loss_target, m_norm_w, m_w_in, m_s5_lambda_re, m_s5_lambda_im, m_s5_b_re, m_s5_b_im, m_s5_c_re, m_s5_c_im, m_s5_d, m_s5_log_step, m_s5_w_glu, m_sgu_ln_w, m_sgu_ln_b, m_sgu_w, m_sgu_b, m_m2_conv_w, m_m2_conv_b, m_m2_dt_bias, m_m2_a_log, m_m2_d, m_m2_norm_w, m_sc_conv_w, m_merge_b, m_w_branch, m_w_out, m_final_norm_w, v_norm_w, v_w_in, v_s5_lambda_re, v_s5_lambda_im, v_s5_b_re, v_s5_b_im, v_s5_c_re, v_s5_c_im, v_s5_d, v_s5_log_step, v_s5_w_glu, v_sgu_ln_w, v_sgu_ln_b, v_sgu_w, v_sgu_b, v_m2_conv_w, v_m2_conv_b, v_m2_dt_bias, v_m2_a_log, v_m2_d, v_m2_norm_w, v_sc_conv_w, v_merge_b, v_w_branch, v_w_out, v_final_norm_w):
    given = dict(locals())
    w = {k: given[k] for k in WEIGHTS}
    m = {k: given["m_" + k] for k in WEIGHTS}
    v = {k: given["v_" + k] for k in WEIGHTS}
    layers = _gather_weights(w)
    loss, dx, grads, d_final = local_step(x[0], loss_target[0], layers, final_norm_w)
    loss = lax.psum(loss, ("x", "y", "c"))
    g = _reduce_grads(grads, d_final, w)
    d, nm, nv = _update(w, g, m, v)
    return (loss, dx[None], *[g[k] for k in WEIGHTS], *[d[k] for k in WEIGHTS],
            *[nm[k] for k in WEIGHTS], *[nv[k] for k in WEIGHTS])
```

```python
import functools
from typing import Any, Callable, NamedTuple

import numpy as np
import jax
import jax.numpy as jnp
from jax import lax
from jax.experimental import pallas as pl
from jax.experimental.pallas import tpu as pltpu

f32 = jnp.float32
bf16 = jnp.bfloat16

D_MODEL = 1024
BW = 512
N_BRANCH = 4
EPS = 1e-6
S5_GROUPS, S5_P, S5_N = 32, 16, 64
S5_NS = S5_GROUPS * S5_N
CHUNK = 128
M2_HEADS, M2_HEAD_DIM, M2_GROUPS, M2_STATE = 8, 64, 2, 128
IN_DIM = 10248
PW = 10368
PW_MAIN = 10240
LANES = 128
VMEM_LIMIT = 60 * 1024 * 1024

ADAM_LR, ADAM_B1, ADAM_B2, ADAM_EPS, ADAM_WD, ADAM_STEP = 0.001, 0.9, 0.999, 1e-08, 0.01, 10

C_MERGE = 0
C_SC = 4096
C_SGU = 6144
C_S5G = 7680
C_S5U = 8192
C_M2Z = 8704
C_XBC = 9216
C_DT = 10240


def _col_segments():
    segs = [(6152, 4096)]
    for j in range(4):
        segs += [(4104 + 128 * j, 128), (4616 + 128 * j, 128), (5128 + 128 * j, 128), (5640 + 128 * j, 128)]
    segs += [(1024, 1536), (512, 512), (0, 512), (2560, 512), (3072, 1024), (4096, 8)]
    return segs


def _to_kernel_cols(w):
    parts = [w[:, s:s + n] for s, n in _col_segments()]
    parts.append(jnp.zeros((w.shape[0], PW - IN_DIM), w.dtype))
    return jnp.concatenate(parts, axis=1)


def _from_kernel_cols(wp):
    out, pos = {}, 0
    for s, n in _col_segments():
        out[s] = wp[:, pos:pos + n]
        pos += n
    return jnp.concatenate([out[s] for s in sorted(out)], axis=1)


NN = ((1,), (0,))
NT = ((1,), (1,))
TN = ((0,), (0,))


def _bd(a, b, dims):
    return lax.dot_general(a.astype(bf16), b.astype(bf16), (dims, ((), ())), preferred_element_type=f32)


def _hd(a, b, dims):
    return lax.dot_general(a, b, (dims, ((), ())), precision=lax.Precision.HIGHEST, preferred_element_type=f32)


def _make_dots(raw):
    @jax.custom_vjp
    def nn(a, b):
        return raw(a, b, NN)
    nn.defvjp(lambda a, b: (raw(a, b, NN), (a, b)), lambda r, g: (raw(g, r[1], NT), raw(r[0], g, TN)))

    @jax.custom_vjp
    def nt(a, b):
        return raw(a, b, NT)
    nt.defvjp(lambda a, b: (raw(a, b, NT), (a, b)), lambda r, g: (raw(g, r[1], NN), raw(g, r[0], TN)))

    @jax.custom_vjp
    def tn(a, b):
        return raw(a, b, TN)
    tn.defvjp(lambda a, b: (raw(a, b, TN), (a, b)), lambda r, g: (raw(r[1], g, NT), raw(r[0], g, NN)))
    return nn, nt, tn


bdot, bdot_nt, bdot_tn = _make_dots(_bd)
hdot, hdot_nt, hdot_tn = _make_dots(_hd)


@jax.custom_vjp
def bdot_w(a, w, shadow):
    return _bd(a, w, NN)


bdot_w.defvjp(lambda a, w, s: (_bd(a, w, NN), (a, w)),
              lambda r, g: (_bd(g, r[1], NT), jnp.zeros_like(r[1]), _bd(r[0], g, TN)))


def _rows(shape):
    return lax.broadcasted_iota(jnp.int32, shape, 0)


def _cols(shape):
    return lax.broadcasted_iota(jnp.int32, shape, 1)


def _shift_down(x, s):
    return jnp.where(_rows(x.shape) < s, 0.0, pltpu.roll(x, s, 0))


def _shift_up(x, s):
    n = x.shape[0]
    return jnp.where(_rows(x.shape) >= n - s, 0.0, pltpu.roll(x, n - s, 0))


@functools.partial(jax.custom_vjp, nondiff_argnums=(1,))
def shift(x, s):
    return _shift_down(x, s) if s else x


shift.defvjp(lambda x, s: (shift(x, s), None), lambda s, _, g: (_shift_up(g, s) if s else g,))


def _row_of(w, k):
    return jnp.sum(jnp.where(_rows(w.shape) == k, w, 0.0), axis=0, keepdims=True)


def _lane_mask(width, lo, hi):
    c = _cols((1, width))
    return ((c >= lo) & (c < hi)).astype(f32)


def _expand(rows, width, per):
    return (_cols((rows, width)) // per == _rows((rows, width))).astype(f32)


class A(NamedTuple):
    arr: Any
    block: tuple
    imap: Callable
    shadow: bool = False


class O(NamedTuple):
    shape: tuple
    dtype: Any
    block: tuple
    imap: Callable
    alias: Any = None


class R(NamedTuple):
    arg: int
    out: int
    off: Any = None
    acc: bool = False


def _cparams(n_grid):
    return pltpu.CompilerParams(dimension_semantics=("arbitrary",) * n_grid, vmem_limit_bytes=VMEM_LIMIT)


def _ispec(block, imap, n, rev):
    if rev:
        return pl.BlockSpec(block, lambda i: imap(n - 1 - i))
    return pl.BlockSpec(block, imap)


def _load(ref, a):
    v = ref[...]
    if a.shadow:
        return (v, jnp.zeros(v.shape, f32))
    return v.astype(f32)


def _save_spec(shape, n, rev):
    nd = len(shape)
    return _ispec((None,) + tuple(shape), lambda i: (i,) + (0,) * nd, n, rev)


def block_fwd(name, f, n, args, outs, carries=()):
    n_in, n_out, n_c = len(args), len(outs), len(carries)

    def body(*refs):
        ins, out_r = refs[:n_in], refs[n_in:n_in + n_out]
        saves, cs = refs[n_in + n_out:n_in + n_out + n_c], refs[n_in + n_out + n_c:]
        if n_c:
            @pl.when(pl.program_id(0) == 0)
            def _():
                for c in cs:
                    c[...] = jnp.zeros(c.shape, f32)
        vals = [_load(r, a) for r, a in zip(ins, args)]
        cv = [c[...] for c in cs]
        for s, v in zip(saves, cv):
            s[...] = v
        res = f(*vals, *cv)
        for r, v in zip(out_r, res[:n_out]):
            r[...] = v.astype(r.dtype)
        for c, v in zip(cs, res[n_out:]):
            c[...] = v

    out_shape = [jax.ShapeDtypeStruct(o.shape, o.dtype) for o in outs]
    out_specs = [pl.BlockSpec(o.block, o.imap) for o in outs]
    for shp in carries:
        out_shape.append(jax.ShapeDtypeStruct((n,) + tuple(shp), f32))
        out_specs.append(_save_spec(shp, n, False))
    return pl.pallas_call(
        body, name=name, grid=(n,),
        in_specs=[pl.BlockSpec(a.block, a.imap) for a in args],
        out_specs=out_specs, out_shape=out_shape,
        scratch_shapes=[pltpu.VMEM(tuple(shp), f32) for shp in carries],
        compiler_params=_cparams(1),
    )(*[a.arr for a in args])


def block_bwd(name, f, n, args, cots, gouts, routes, saved=(), rev=False):
    n_in, n_cot, n_c, n_go = len(args), len(cots), len(saved), len(gouts)
    diff = []
    for r in routes:
        if r.arg not in diff:
            diff.append(r.arg)
    aliases = [(k, o.alias) for k, o in enumerate(gouts) if o.alias is not None]

    def body(*refs):
        ins = refs[:n_in]
        cot_r = refs[n_in:n_in + n_cot]
        sav_r = refs[n_in + n_cot:n_in + n_cot + n_c]
        base = n_in + n_cot + n_c + len(aliases)
        go_r = refs[base:base + n_go]
        dcs = refs[base + n_go:]
        step = pl.program_id(0)
        if n_c:
            @pl.when(step == 0)
            def _():
                for d in dcs:
                    d[...] = jnp.zeros(d.shape, f32)
        vals = [_load(r, a) for r, a in zip(ins, args)]
        cv = [s[...] for s in sav_r]
        nd = len(diff)

        def g(*dv):
            full = list(vals)
            for idx, v in zip(diff, dv[:nd]):
                full[idx] = (vals[idx][0], v) if args[idx].shadow else v
            return tuple(f(*full, *dv[nd:]))

        primals = [vals[i][1] if args[i].shadow else vals[i] for i in diff] + cv
        _, vjp_fn = jax.vjp(g, *primals)
        ct = tuple([r[...].astype(f32) for r in cot_r] + [d[...] for d in dcs])
        grads = vjp_fn(ct)
        for r in routes:
            gr = grads[diff.index(r.arg)]
            ref = go_r[r.out]
            if r.acc:
                @pl.when(step == 0)
                def _(ref=ref, gr=gr):
                    ref[...] = gr.astype(ref.dtype)

                @pl.when(step > 0)
                def _(ref=ref, gr=gr):
                    ref[...] += gr.astype(ref.dtype)
            elif r.off is None:
                ref[...] = gr.astype(ref.dtype)
            else:
                ref[:, r.off:r.off + gr.shape[1]] = gr.astype(ref.dtype)
        for d, gr in zip(dcs, grads[nd:]):
            d[...] = gr

    in_specs = [_ispec(a.block, a.imap, n, rev) for a in list(args) + list(cots)]
    in_specs += [_save_spec(s.shape[1:], n, rev) for s in saved]
    in_specs += [pl.BlockSpec(memory_space=pl.ANY) for _ in aliases]
    operands = [a.arr for a in list(args) + list(cots)] + list(saved) + [arr for _, arr in aliases]
    io_alias = {n_in + n_cot + n_c + j: k for j, (k, _) in enumerate(aliases)}
    return pl.pallas_call(
        body, name=name, grid=(n,),
        in_specs=in_specs,
        out_specs=[_ispec(o.block, o.imap, n, rev) for o in gouts],
        out_shape=[jax.ShapeDtypeStruct(o.shape, o.dtype) for o in gouts],
        scratch_shapes=[pltpu.VMEM(tuple(s.shape[1:]), f32) for s in saved],
        input_output_aliases=io_alias,
        compiler_params=_cparams(1),
    )(*operands)


class Op(NamedTuple):
    arr: Any
    row: int = 0
    col: int = 0


def mm(name, mode, pairs, m, n, tm, tn, out_dtype=f32, add=None, out=None, out_col=0):
    tm, tn = min(tm, m), min(tn, n)
    assert m % tm == 0 and n % tn == 0
    in_specs, operands = [], []
    for a, b, k, _ in pairs:
        if mode == TN:
            assert a.row % k == 0 and a.col % tm == 0
            in_specs.append(pl.BlockSpec((k, tm), lambda j, i, a=a, k=k: (a.row // k, i + a.col // tm)))
        else:
            assert a.col % k == 0 and a.row % tm == 0
            in_specs.append(pl.BlockSpec((tm, k), lambda j, i, a=a, k=k: (i + a.row // tm, a.col // k)))
        if mode == NT:
            assert b.col % k == 0 and b.row % tn == 0
            in_specs.append(pl.BlockSpec((tn, k), lambda j, i, b=b, k=k: (j + b.row // tn, b.col // k)))
        else:
            assert b.row % k == 0 and b.col % tn == 0
            in_specs.append(pl.BlockSpec((k, tn), lambda j, i, b=b, k=k: (b.row // k, j + b.col // tn)))
        operands += [a.arr, b.arr]
    n_p = len(pairs)
    if add is not None:
        assert add.col % tn == 0
        in_specs.append(pl.BlockSpec((tm, tn), lambda j, i: (i, j + add.col // tn)))
        operands.append(add.arr)
    io_alias = {}
    if out is not None:
        assert out_col % tn == 0
        in_specs.append(pl.BlockSpec(memory_space=pl.ANY))
        operands.append(out)
        io_alias = {len(operands) - 1: 0}
        out_shape = jax.ShapeDtypeStruct(out.shape, out.dtype)
    else:
        out_shape = jax.ShapeDtypeStruct((m, n), out_dtype)
    signs = [p[3] for p in pairs]

    def body(*refs):
        o = refs[-1]
        acc = None
        for p in range(n_p):
            t = _bd(refs[2 * p][...], refs[2 * p + 1][...], mode)
            t = t if signs[p] > 0 else -t
            acc = t if acc is None else acc + t
        if add is not None:
            acc = acc + refs[2 * n_p][...].astype(f32)
        o[...] = acc.astype(o.dtype)

    return pl.pallas_call(
        body, name=name, grid=(n // tn, m // tm),
        in_specs=in_specs,
        out_specs=pl.BlockSpec((tm, tn), lambda j, i: (i, j + out_col // tn)),
        out_shape=out_shape, input_output_aliases=io_alias,
        compiler_params=_cparams(2),
    )(*operands)


SCAN_LANES = 512
SUBL = 8


def _cmul(p, q):
    return (p[0] * q[0] - p[1] * q[1], p[0] * q[1] + p[1] * q[0])


def _powers(a):
    a2 = _cmul(a, a)
    a4 = _cmul(a2, a2)
    a6 = _cmul(a4, a2)
    return [a, a2, _cmul(a2, a), a4, _cmul(a4, a), a6, _cmul(a6, a), _cmul(a4, a4)]


def _table(pw, order, w):
    row = _rows((SUBL, w))
    re = sum(jnp.where(row == t, pw[k][0], 0.0) for t, k in enumerate(order))
    im = sum(jnp.where(row == t, pw[k][1], 0.0) for t, k in enumerate(order))
    return re, im


def _pick_row(x, t):
    return jnp.sum(jnp.where(_rows(x.shape) == t, x, 0.0), axis=0, keepdims=True)


def s5_scan_fwd(bu_re, bu_im, a_re, a_im):
    seq, ns = bu_re.shape
    w, nb = SCAN_LANES, ns // SCAN_LANES

    def body(b_re, b_im, ar, ai, s_re, s_im):
        a = (ar[...], ai[...])
        pw = _powers(a)
        tab = _table(pw, list(range(SUBL)), w)
        row = _rows((SUBL, w))

        def step(i, carry):
            t0 = pl.multiple_of(i * SUBL, SUBL)
            x = (b_re[pl.ds(t0, SUBL), :], b_im[pl.ds(t0, SUBL), :])
            for d, k in ((1, 0), (2, 1), (4, 3)):
                sh = (jnp.where(row < d, 0.0, pltpu.roll(x[0], d, 0)), jnp.where(row < d, 0.0, pltpu.roll(x[1], d, 0)))
                t = _cmul(pw[k], sh)
                x = (x[0] + t[0], x[1] + t[1])
            t = _cmul(tab, carry)
            x = (x[0] + t[0], x[1] + t[1])
            s_re[pl.ds(t0, SUBL), :] = x[0]
            s_im[pl.ds(t0, SUBL), :] = x[1]
            return (x[0][SUBL - 1:, :], x[1][SUBL - 1:, :])

        z = jnp.zeros((1, w), f32)
        lax.fori_loop(0, seq // SUBL, step, (z, z))

    strip = pl.BlockSpec((seq, w), lambda j: (0, j))
    lane = pl.BlockSpec((1, w), lambda j: (0, j))
    return pl.pallas_call(
        body, name="s5_scan_fwd", grid=(nb,),
        in_specs=[strip, strip, lane, lane],
        out_specs=[strip, strip],
        out_shape=[jax.ShapeDtypeStruct((seq, ns), f32)] * 2,
        compiler_params=_cparams(1),
    )(bu_re, bu_im, a_re, a_im)


def s5_scan_bwd(ds_re, ds_im, s_re, s_im, a_re, a_im):
    seq, ns = ds_re.shape
    w, nb = SCAN_LANES, ns // SCAN_LANES
    nblk = seq // SUBL

    def body(g_re, g_im, sr, si, ar, ai, l_re, l_im, da_re, da_im):
        a = (ar[...], -ai[...])
        pw = _powers(a)
        tab = _table(pw, [SUBL - 1 - t for t in range(SUBL)], w)
        row = _rows((SUBL, w))

        def step(kk, carry):
            c_re, c_im, acc_re, acc_im = carry
            i = nblk - 1 - kk
            t0 = pl.multiple_of(i * SUBL, SUBL)
            x = (g_re[pl.ds(t0, SUBL), :], g_im[pl.ds(t0, SUBL), :])
            for d, k in ((1, 0), (2, 1), (4, 3)):
                sh = (jnp.where(row >= SUBL - d, 0.0, pltpu.roll(x[0], SUBL - d, 0)),
                      jnp.where(row >= SUBL - d, 0.0, pltpu.roll(x[1], SUBL - d, 0)))
                t = _cmul(pw[k], sh)
                x = (x[0] + t[0], x[1] + t[1])
            t = _cmul(tab, (c_re, c_im))
            x = (x[0] + t[0], x[1] + t[1])
            l_re[pl.ds(t0, SUBL), :] = x[0]
            l_im[pl.ds(t0, SUBL), :] = x[1]
            tp = pl.multiple_of(jnp.maximum(i - 1, 0) * SUBL, SUBL)
            live = (i > 0).astype(f32)
            p_re = _pick_row(sr[pl.ds(tp, SUBL), :], SUBL - 1) * live
            p_im = _pick_row(si[pl.ds(tp, SUBL), :], SUBL - 1) * live
            sp_re = jnp.where(row == 0, p_re, pltpu.roll(sr[pl.ds(t0, SUBL), :], 1, 0))
            sp_im = jnp.where(row == 0, p_im, pltpu.roll(si[pl.ds(t0, SUBL), :], 1, 0))
            acc_re = acc_re + x[0] * sp_re + x[1] * sp_im
            acc_im = acc_im + x[1] * sp_re - x[0] * sp_im
            return (x[0][:1, :], x[1][:1, :], acc_re, acc_im)

        z1 = jnp.zeros((1, w), f32)
        z8 = jnp.zeros((SUBL, w), f32)
        _, _, acc_re, acc_im = lax.fori_loop(0, nblk, step, (z1, z1, z8, z8))
        da_re[...] = jnp.sum(acc_re, axis=0, keepdims=True)
        da_im[...] = jnp.sum(acc_im, axis=0, keepdims=True)

    strip = pl.BlockSpec((seq, w), lambda j: (0, j))
    lane = pl.BlockSpec((1, w), lambda j: (0, j))
    return pl.pallas_call(
        body, name="s5_scan_bwd", grid=(nb,),
        in_specs=[strip, strip, strip, strip, lane, lane],
        out_specs=[strip, strip, lane, lane],
        out_shape=[jax.ShapeDtypeStruct((seq, ns), f32)] * 2 + [jax.ShapeDtypeStruct((1, ns), f32)] * 2,
        compiler_params=_cparams(1),
    )(ds_re, ds_im, s_re, s_im, a_re, a_im)


def _rms(x, w):
    return x * lax.rsqrt(jnp.mean(x * x, axis=-1, keepdims=True) + EPS) * w


def f_rms(x, w):
    return (_rms(x, w),)


def f_rms_res(x, w):
    return (_rms(x, w), x)


def f_s5_prep(lam_re, lam_im, log_step, b_re, b_im):
    e = _expand(log_step.shape[1], S5_NS, S5_N)
    step = hdot(jnp.exp(log_step), e)
    mag = jnp.exp(lam_re * step)
    ab_re, ab_im = mag * jnp.cos(lam_im * step), mag * jnp.sin(lam_im * step)
    den = lam_re * lam_re + lam_im * lam_im
    nr = ab_re - 1.0
    coef_re = (nr * lam_re + ab_im * lam_im) / den
    coef_im = (ab_im * lam_re - nr * lam_im) / den
    bb_re, bb_im = coef_re * b_re - coef_im * b_im, coef_re * b_im + coef_im * b_re
    sel = (_rows((BW, S5_P)) % S5_P == _cols((BW, S5_P))).astype(f32)
    blk = _rows((BW, S5_NS)) // S5_P == _cols((BW, S5_NS)) // S5_N
    rows_bd = lambda t: jnp.where(blk, hdot(sel, t), 0.0)
    return (ab_re, ab_im, rows_bd(bb_re), rows_bd(bb_im))


def f_s5_c(c_re, c_im):
    sel_t = (_cols((S5_P, BW)) % S5_P == _rows((S5_P, BW))).astype(f32)
    blk_t = _rows((S5_NS, BW)) // S5_N == _cols((S5_NS, BW)) // S5_P
    cols_bd = lambda t: jnp.where(blk_t, hdot_tn(t, sel_t), 0.0)
    return (cols_bd(c_re), cols_bd(c_im))


def f_s5_act(y_lin, u, gate, d, w_glu):
    y = jax.nn.gelu(y_lin + d * u)
    y = y * jax.nn.sigmoid(bdot_w(y, *w_glu))
    return (y * jax.nn.silu(gate),)


def f_sgu(u, v, gate, ln_w, ln_b, *rest):
    w_s, b_pad = rest[:8], rest[8]
    t = u.shape[0]
    u32, v32 = jax.nn.gelu(u), jax.nn.gelu(v)
    mu = jnp.mean(v32, axis=-1, keepdims=True)
    var = jnp.mean(jnp.square(v32 - mu), axis=-1, keepdims=True)
    vn = (v32 - mu) * lax.rsqrt(var + EPS) * ln_w + ln_b
    tri = _rows((t, t)) >= _cols((t, t))
    s = hdot_tn(b_pad, _expand(LANES, BW, BW // 8))
    for h in range(8):
        s = s + bdot(jnp.where(tri, w_s[h], 0.0), vn) * _lane_mask(BW, 64 * h, 64 * h + 64)
    return (u32 * s * jax.nn.silu(gate),)


def f_m2_conv(x, w, b):
    return (sum(_row_of(w, k) * shift(x, 3 - k) for k in range(4)) + b,)


def f_sc(bg, cg, h, gate, w):
    z = cg * h
    conv = sum(_row_of(w, k) * shift(z, 2 - k) for k in range(3))
    return (bg * conv * jax.nn.silu(gate),)


def f_m2(z, xc, b0, b1, c0, c1, dt_raw, dt_bias, a_log, d_par, norm_w, st):
    q = z.shape[0]
    x = jax.nn.silu(xc)
    bm, cm = (jax.nn.silu(b0), jax.nn.silu(b1)), (jax.nn.silu(c0), jax.nn.silu(c1))
    dt = jax.nn.softplus(dt_raw + dt_bias)
    da = dt * (-jnp.exp(a_log))
    tri = _rows((q, q)) >= _cols((q, q))
    acs = hdot(tri.astype(f32), da)
    e = _expand(LANES, BW, M2_HEAD_DIM)
    dt_f, acs_f = hdot(dt, e), hdot(acs, e)
    last = _rows((q, BW)) == q - 1
    alast_f = jnp.sum(jnp.where(last, acs_f, 0.0), axis=0, keepdims=True)
    xdt = x * dt_f
    xdec = xdt * jnp.exp(alast_f - acs_f)
    acs_t = acs.T
    st_new = st * jnp.exp(alast_f)
    y_diag, y_off = 0.0, 0.0
    for g in range(M2_GROUPS):
        gm = _lane_mask(BW, 256 * g, 256 * g + 256)
        cb = bdot_nt(cm[g], bm[g])
        st_new = st_new + bdot_tn(bm[g], xdec * gm)
        y_off = y_off + bdot(cm[g], st) * gm
        for hh in range(M2_HEADS // M2_GROUPS):
            h = g * (M2_HEADS // M2_GROUPS) + hh
            col = jnp.sum(jnp.where(_cols((q, LANES)) == h, acs, 0.0), axis=1, keepdims=True)
            row = jnp.sum(jnp.where(_rows((LANES, q)) == h, acs_t, 0.0), axis=0, keepdims=True)
            decay = jnp.exp(jnp.where(tri, col - row, -1e30))
            y_diag = y_diag + bdot(cb * decay, xdt) * _lane_mask(BW, 64 * h, 64 * h + 64)
    d_f = sum(jnp.sum(jnp.where(_cols((1, LANES)) == h, d_par, 0.0), axis=1, keepdims=True)
              * _lane_mask(BW, 64 * h, 64 * h + 64) for h in range(M2_HEADS))
    y = y_diag + y_off * jnp.exp(acs_f) + d_f * x
    y = y * jax.nn.silu(z)
    return (_rms(y, norm_w), st_new)


def f_gate_mix(*v):
    bo, lg, mb = v[0:4], v[4:8], v[8:12]
    return (sum(jax.nn.sigmoid(lg[k] + mb[k]) * bo[k] for k in range(N_BRANCH)),)


def _full(shape):
    nd = len(shape)
    return dict(block=tuple(shape), imap=lambda i: (0,) * nd)


def _param(arr, shadow=False):
    return A(arr, tuple(arr.shape), lambda i, nd=arr.ndim: (0,) * nd, shadow)


def _tb(arr, t, width, colblk):
    return A(arr, (t, width), lambda i: (i, colblk))


def _strip(arr, seq, colblk0, stride=1):
    return A(arr, (seq, LANES), lambda j: (0, colblk0 + stride * j))


def _s5_prep_args(p):
    lam_re = p["s5_lambda_re"].reshape(1, S5_NS)
    lam_im = p["s5_lambda_im"].reshape(1, S5_NS)
    log_step = jnp.pad(p["s5_log_step"].reshape(1, S5_GROUPS), ((0, 0), (0, LANES - S5_GROUPS)))
    b_lanes = lambda b: jnp.transpose(b, (2, 0, 1)).reshape(S5_P, S5_NS)
    return [_param(v) for v in (lam_re, lam_im, log_step, b_lanes(p["s5_b_re"]), b_lanes(p["s5_b_im"]))]


def _s5_c_args(p):
    c_lanes = lambda c: jnp.transpose(c, (1, 0, 2)).reshape(S5_P, S5_NS)
    return [_param(c_lanes(p["s5_c_re"])), _param(c_lanes(p["s5_c_im"]))]


def layer_fwd(x, p):
    seq = x.shape[0]
    nt = seq // CHUNK
    t2 = 256
    sv = {}
    hb = block_fwd("rms_in", f_rms, seq // t2, [_tb(x, t2, D_MODEL, 0), _param(p["norm_w"].reshape(1, D_MODEL))],
                   [O((seq, D_MODEL), bf16, (t2, D_MODEL), lambda i: (i, 0))])[0]
    proj = mm("mm_in", NN, [(Op(hb), Op(p["w_in"]), D_MODEL, 1)], seq, PW, 512, 1152)
    whole = lambda shape, dt: O(shape, dt, shape, lambda i: (0, 0))
    ab_re, ab_im, bb_re, bb_im = block_fwd("s5_prep", f_s5_prep, 1, _s5_prep_args(p),
                                           [whole((1, S5_NS), f32)] * 2 + [whole((BW, S5_NS), bf16)] * 2)
    cc_re, cc_im = block_fwd("s5_c", f_s5_c, 1, _s5_c_args(p), [whole((S5_NS, BW), bf16)] * 2)
    bu_re = mm("mm_bu_re", NN, [(Op(proj, 0, C_S5U), Op(bb_re), BW, 1)], seq, S5_NS, 512, 1024)
    bu_im = mm("mm_bu_im", NN, [(Op(proj, 0, C_S5U), Op(bb_im), BW, 1)], seq, S5_NS, 512, 1024)
    s_re, s_im = s5_scan_fwd(bu_re, bu_im, ab_re, ab_im)
    y_lin = mm("mm_s5y", NN, [(Op(s_re), Op(cc_re), S5_NS, 1), (Op(s_im), Op(cc_im), S5_NS, -1)], seq, BW, 256, BW)
    s5_act_args = [_tb(y_lin, CHUNK, BW, 0), _tb(proj, CHUNK, BW, C_S5U // BW), _tb(proj, CHUNK, BW, C_S5G // BW),
                   _param(p["s5_d"].reshape(1, BW)), _param(p["s5_w_glu"], True)]
    out_bw = O((seq, BW), f32, (CHUNK, BW), lambda i: (i, 0))
    y_a = block_fwd("s5_act", f_s5_act, nt, s5_act_args, [out_bw])[0]
    y_b = block_fwd("sgu", f_sgu, nt, _sgu_args(proj, p), [out_bw])[0]
    xc = block_fwd("m2_conv", f_m2_conv, 8, _m2_conv_args(proj, p, seq),
                   [O((seq, 2 * BW), f32, (seq, LANES), lambda j: (0, j))])[0]
    y_c, st_saved = block_fwd("m2_ssd", f_m2, nt, _m2_args(proj, xc, p), [out_bw], carries=[(M2_STATE, BW)])
    y_d = block_fwd("sc", f_sc, 4, _sc_args(proj, p, seq), [O((seq, BW), f32, (seq, LANES), lambda j: (0, j))])[0]
    ys = [y_a, y_b, y_c, y_d]
    bo = [mm(f"mm_branch{k}", NN, [(Op(ys[k]), Op(p["w_branch"], k * BW, 0), BW, 1)], seq, D_MODEL, 512, D_MODEL)
          for k in range(N_BRANCH)]
    merged = block_fwd("gate_mix", f_gate_mix, seq // t2, _mix_args(bo, proj, p, t2),
                       [O((seq, D_MODEL), bf16, (t2, D_MODEL), lambda i: (i, 0))])[0]
    x_new = mm("mm_out", NN, [(Op(merged), Op(p["w_out"]), D_MODEL, 1)], seq, D_MODEL, 512, D_MODEL, add=Op(x))
    sv.update(x=x, hb=hb, proj=proj, ab=(ab_re, ab_im), bb=(bb_re, bb_im), cc=(cc_re, cc_im), s=(s_re, s_im), y_lin=y_lin,
              xc=xc, st=st_saved, ys=ys, bo=bo, merged=merged)
    return x_new, sv


def _sgu_args(proj, p):
    c0 = C_SGU // BW
    args = [_tb(proj, CHUNK, BW, c0), _tb(proj, CHUNK, BW, c0 + 1), _tb(proj, CHUNK, BW, c0 + 2),
            _param(p["sgu_ln_w"].reshape(1, BW)), _param(p["sgu_ln_b"].reshape(1, BW))]
    args += [A(p["sgu_w"], (None, CHUNK, CHUNK), lambda i, h=h: (h, 0, 0)) for h in range(8)]
    args.append(_param(jnp.pad(p["sgu_b"], ((0, LANES - 8), (0, 0)))))
    return args


def _m2_conv_args(proj, p, seq):
    return [_strip(proj, seq, C_XBC // LANES), A(p["m2_conv_w"], (4, LANES), lambda j: (0, j)),
            A(p["m2_conv_b"].reshape(1, 2 * BW), (1, LANES), lambda j: (0, j))]


def _pad_lanes(v):
    return jnp.pad(v.reshape(1, -1), ((0, 0), (0, LANES - v.size)))


def _m2_args(proj, xc, p):
    args = [_tb(proj, CHUNK, BW, C_M2Z // BW), _tb(xc, CHUNK, BW, 0)]
    args += [_tb(xc, CHUNK, LANES, 4 + k) for k in range(4)]
    args.append(_tb(proj, CHUNK, LANES, C_DT // LANES))
    args += [_param(_pad_lanes(p["m2_dt_bias"])), _param(_pad_lanes(p["m2_a_log"])), _param(_pad_lanes(p["m2_d"])),
             _param(p["m2_norm_w"].reshape(1, BW))]
    return args


def _sc_args(proj, p, seq):
    c0 = C_SC // LANES
    return [_strip(proj, seq, c0 + k, 4) for k in range(4)] + [A(p["sc_conv_w"], (3, LANES), lambda j: (0, j))]


def _mix_args(bo, proj, p, t):
    args = [_tb(bo[k], t, D_MODEL, 0) for k in range(N_BRANCH)]
    args += [_tb(proj, t, D_MODEL, k) for k in range(N_BRANCH)]
    mb = p["merge_b"].reshape(N_BRANCH, 1, D_MODEL)
    args += [A(mb, (None, 1, D_MODEL), lambda i, k=k: (k, 0, 0)) for k in range(N_BRANCH)]
    return args


def layer_bwd(d_out, p, sv):
    seq = d_out.shape[0]
    nt = seq // CHUNK
    t2 = 256
    proj, ys, bo = sv["proj"], sv["ys"], sv["bo"]
    g = {}
    acc = lambda shape: O(tuple(shape), f32, tuple(shape), lambda i, nd=len(shape): (0,) * nd)
    d_merged = mm("mm_out_dx", NT, [(Op(d_out), Op(p["w_out"]), D_MODEL, 1)], seq, D_MODEL, 512, D_MODEL, out_dtype=f32)
    g["w_out"] = mm("mm_out_dw", TN, [(Op(sv["merged"]), Op(d_out), seq, 1)], D_MODEL, D_MODEL, 256, D_MODEL, out_dtype=bf16)
    gouts = [O((seq, D_MODEL), bf16, (t2, D_MODEL), lambda i: (i, 0)) for _ in range(N_BRANCH)]
    gouts.append(O((seq, PW_MAIN), bf16, (t2, N_BRANCH * D_MODEL), lambda i: (i, 0)))
    gouts += [acc((1, D_MODEL)) for _ in range(N_BRANCH)]
    routes = [R(k, k) for k in range(N_BRANCH)] + [R(4 + k, 4, k * D_MODEL) for k in range(N_BRANCH)]
    routes += [R(8 + k, 5 + k, acc=True) for k in range(N_BRANCH)]
    res = block_bwd("gate_mix_bwd", f_gate_mix, seq // t2, _mix_args(bo, proj, p, t2),
                    [_tb(d_merged, t2, D_MODEL, 0)], gouts, routes)
    dbo, dproj = res[:4], res[4]
    g["merge_b"] = jnp.concatenate(res[5:9], axis=0)
    dys = [mm(f"mm_branch{k}_dx", NT, [(Op(dbo[k]), Op(p["w_branch"], k * BW, 0), D_MODEL, 1)], seq, BW, 512, BW)
           for k in range(N_BRANCH)]
    g["w_branch"] = jnp.stack([mm(f"mm_branch{k}_dw", TN, [(Op(ys[k]), Op(dbo[k]), seq, 1)], BW, D_MODEL, 256, D_MODEL, out_dtype=bf16)
                               for k in range(N_BRANCH)])
    res = block_bwd("sc_bwd", f_sc, 4, _sc_args(proj, p, seq), [_strip(dys[3], seq, 0)],
                    [O((seq, PW_MAIN), bf16, (seq, 4 * LANES), lambda j: (0, C_SC // (4 * LANES) + j), alias=dproj),
                     O((3, BW), f32, (3, LANES), lambda j: (0, j))],
                    [R(k, 0, k * LANES) for k in range(4)] + [R(4, 1)])
    dproj, g["sc_conv_w"] = res
    res = block_bwd("m2_ssd_bwd", f_m2, nt, _m2_args(proj, sv["xc"], p), [_tb(dys[2], CHUNK, BW, 0)],
                    [O((seq, PW_MAIN), bf16, (CHUNK, BW), lambda i: (i, C_M2Z // BW), alias=dproj),
                     O((seq, 2 * BW), f32, (CHUNK, 2 * BW), lambda i: (i, 0)),
                     O((seq, LANES), bf16, (CHUNK, LANES), lambda i: (i, 0)),
                     acc((1, LANES)), acc((1, LANES)), acc((1, LANES)), acc((1, BW))],
                    [R(0, 0), R(1, 1, 0)] + [R(2 + k, 1, BW + k * LANES) for k in range(4)] + [R(6, 2)]
                    + [R(7, 3, acc=True), R(8, 4, acc=True), R(9, 5, acc=True), R(10, 6, acc=True)],
                    saved=[sv["st"]], rev=True)
    dproj, dxc, d_dt = res[0], res[1], res[2]
    g["m2_dt_bias"], g["m2_a_log"], g["m2_d"] = (r[0, :M2_HEADS] for r in res[3:6])
    g["m2_norm_w"] = res[6].reshape(BW)
    res = block_bwd("m2_conv_bwd", f_m2_conv, 8, _m2_conv_args(proj, p, seq), [_strip(dxc, seq, 0)],
                    [O((seq, PW_MAIN), bf16, (seq, LANES), lambda j: (0, C_XBC // LANES + j), alias=dproj),
                     O((4, 2 * BW), f32, (4, LANES), lambda j: (0, j)), O((1, 2 * BW), f32, (1, LANES), lambda j: (0, j))],
                    [R(0, 0), R(1, 1), R(2, 2)])
    dproj, g["m2_conv_w"], cb = res
    g["m2_conv_b"] = cb.reshape(2 * BW)
    res = block_bwd("sgu_bwd", f_sgu, nt, _sgu_args(proj, p), [_tb(dys[1], CHUNK, BW, 0)],
                    [O((seq, PW_MAIN), bf16, (CHUNK, 3 * BW), lambda i: (i, C_SGU // (3 * BW)), alias=dproj),
                     acc((1, BW)), acc((1, BW))] + [acc((CHUNK, CHUNK)) for _ in range(8)] + [acc((LANES, CHUNK))],
                    [R(0, 0, 0), R(1, 0, BW), R(2, 0, 2 * BW), R(3, 1, acc=True), R(4, 2, acc=True)]
                    + [R(5 + h, 3 + h, acc=True) for h in range(8)] + [R(13, 11, acc=True)])
    dproj = res[0]
    g["sgu_ln_w"], g["sgu_ln_b"] = res[1].reshape(BW), res[2].reshape(BW)
    g["sgu_w"] = jnp.stack(res[3:11])
    g["sgu_b"] = res[11][:8]
    y_lin, (s_re, s_im), (ab_re, ab_im) = sv["y_lin"], sv["s"], sv["ab"]
    s5_act_args = [_tb(y_lin, CHUNK, BW, 0), _tb(proj, CHUNK, BW, C_S5U // BW), _tb(proj, CHUNK, BW, C_S5G // BW),
                   _param(p["s5_d"].reshape(1, BW)), _param(p["s5_w_glu"], True)]
    res = block_bwd("s5_act_bwd", f_s5_act, nt, s5_act_args, [_tb(dys[0], CHUNK, BW, 0)],
                    [O((seq, BW), bf16, (CHUNK, BW), lambda i: (i, 0)), O((seq, BW), f32, (CHUNK, BW), lambda i: (i, 0)),
                     O((seq, PW_MAIN), bf16, (CHUNK, BW), lambda i: (i, C_S5G // BW), alias=dproj),
                     acc((1, BW)), acc((BW, BW))],
                    [R(0, 0), R(1, 1), R(2, 2), R(3, 3, acc=True), R(4, 4, acc=True)])
    dy_lin, du1, dproj = res[0], res[1], res[2]
    g["s5_d"] = res[3].reshape(S5_GROUPS, S5_P)
    g["s5_w_glu"] = res[4]
    (bb_re, bb_im), (cc_re, cc_im) = sv["bb"], sv["cc"]
    ds_re = mm("mm_s5y_dre", NT, [(Op(dy_lin), Op(cc_re), BW, 1)], seq, S5_NS, 512, 1024)
    ds_im = mm("mm_s5y_dim", NT, [(Op(dy_lin), Op(cc_im), BW, -1)], seq, S5_NS, 512, 1024)
    dc_re = mm("mm_s5y_dcre", TN, [(Op(s_re), Op(dy_lin), seq, 1)], S5_NS, BW, 512, BW)
    dc_im = mm("mm_s5y_dcim", TN, [(Op(s_im), Op(dy_lin), seq, -1)], S5_NS, BW, 512, BW)
    l_re, l_im, da_re, da_im = s5_scan_bwd(ds_re, ds_im, s_re, s_im, ab_re, ab_im)
    dproj = mm("mm_bu_dx", NT, [(Op(l_re), Op(bb_re), S5_NS, 1), (Op(l_im), Op(bb_im), S5_NS, 1)],
               seq, BW, 512, BW, add=Op(du1), out=dproj, out_col=C_S5U)
    dbb = [mm(f"mm_bu_dw{n}", TN, [(Op(proj, 0, C_S5U), Op(l), seq, 1)], BW, S5_NS, 256, 1024)
           for n, l in (("re", l_re), ("im", l_im))]
    gouts = [acc((1, S5_NS)), acc((1, S5_NS)), acc((1, LANES))] + [acc((S5_P, S5_NS))] * 2
    res = block_bwd("s5_prep_bwd", f_s5_prep, 1, _s5_prep_args(p),
                    [_param(v) for v in (da_re, da_im, dbb[0], dbb[1])], gouts, [R(k, k, acc=True) for k in range(5)])
    g["s5_lambda_re"], g["s5_lambda_im"] = res[0].reshape(S5_GROUPS, S5_N), res[1].reshape(S5_GROUPS, S5_N)
    g["s5_log_step"] = res[2][0, :S5_GROUPS]
    b_natural = lambda b: jnp.transpose(b.reshape(S5_P, S5_GROUPS, S5_N), (1, 2, 0))
    c_natural = lambda c: jnp.transpose(c.reshape(S5_P, S5_GROUPS, S5_N), (1, 0, 2))
    g["s5_b_re"], g["s5_b_im"] = b_natural(res[3]), b_natural(res[4])
    res = block_bwd("s5_c_bwd", f_s5_c, 1, _s5_c_args(p), [_param(dc_re), _param(dc_im)],
                    [acc((S5_P, S5_NS))] * 2, [R(0, 0, acc=True), R(1, 1, acc=True)])
    g["s5_c_re"], g["s5_c_im"] = c_natural(res[0]), c_natural(res[1])
    hb, w_in = sv["hb"], p["w_in"]
    dh = mm("mm_in_dx", NT, [(Op(dproj), Op(w_in), PW_MAIN, 1), (Op(d_dt), Op(w_in, 0, PW_MAIN), LANES, 1)],
            seq, D_MODEL, 256, 256)
    g["w_in"] = (mm("mm_in_dw", TN, [(Op(hb), Op(dproj), seq, 1)], D_MODEL, PW_MAIN, 256, 1024, out_dtype=bf16),
                 mm("mm_in_dwdt", TN, [(Op(hb), Op(d_dt), seq, 1)], D_MODEL, LANES, 256, LANES, out_dtype=bf16))
    dx, dnw = block_bwd("rms_in_bwd", f_rms_res, seq // t2, [_tb(sv["x"], t2, D_MODEL, 0), _param(p["norm_w"].reshape(1, D_MODEL))],
                        [_tb(dh, t2, D_MODEL, 0), _tb(d_out, t2, D_MODEL, 0)],
                        [O((seq, D_MODEL), f32, (t2, D_MODEL), lambda i: (i, 0)), acc((1, D_MODEL))],
                        [R(0, 0), R(1, 1, acc=True)])
    g["norm_w"] = dnw.reshape(D_MODEL)
    return dx, g


def loss_head(x, w, target):
    seq = x.shape[0]
    t = 256

    def body(x_ref, w_ref, t_ref, loss_ref, dx_ref, dw_ref):
        step = pl.program_id(0)

        def f(xv, wv):
            err = _rms(xv, wv) - t_ref[...]
            return 0.5 * jnp.sum(jnp.mean(err * err, axis=-1, keepdims=True), axis=0, keepdims=True)

        val, vjp_fn = jax.vjp(f, x_ref[...], w_ref[...])
        dx, dw = vjp_fn(jnp.ones((1, 1), f32))
        dx_ref[...] = dx

        @pl.when(step == 0)
        def _():
            loss_ref[...] = jnp.broadcast_to(val, loss_ref.shape)
            dw_ref[...] = dw

        @pl.when(step > 0)
        def _():
            loss_ref[...] += jnp.broadcast_to(val, loss_ref.shape)
            dw_ref[...] += dw

    blk = pl.BlockSpec((t, D_MODEL), lambda i: (i, 0))
    row = pl.BlockSpec((1, D_MODEL), lambda i: (0, 0))
    return pl.pallas_call(
        body, name="loss_head", grid=(seq // t,),
        in_specs=[blk, row, blk],
        out_specs=[pl.BlockSpec((1, LANES), lambda i: (0, 0)), blk, row],
        out_shape=[jax.ShapeDtypeStruct((1, LANES), f32), jax.ShapeDtypeStruct((seq, D_MODEL), f32),
                   jax.ShapeDtypeStruct((1, D_MODEL), f32)],
        compiler_params=_cparams(1),
    )(x, w.reshape(1, D_MODEL), target)


LAYER_KEYS = ("norm_w", "w_in", "s5_lambda_re", "s5_lambda_im", "s5_b_re", "s5_b_im", "s5_c_re", "s5_c_im", "s5_d",
              "s5_log_step", "s5_w_glu", "sgu_ln_w", "sgu_ln_b", "sgu_w", "sgu_b", "m2_conv_w", "m2_conv_b",
              "m2_dt_bias", "m2_a_log", "m2_d", "m2_norm_w", "sc_conv_w", "merge_b", "w_branch", "w_out")


def local_step(x, target, layers, final_norm_w):
    saved = []
    for p in layers:
        x, sv = layer_fwd(x, p)
        saved.append(sv)
    loss, dx, dfw = loss_head(x, final_norm_w, target)
    grads = []
    for p, sv in zip(reversed(layers), reversed(saved)):
        dx, g = layer_bwd(dx, p, sv)
        grads.append(g)
    return loss[0, 0], dx, grads[::-1], dfw.reshape(D_MODEL)


MESH = pl.DeviceIdType.MESH
ANY = pl.BlockSpec(memory_space=pl.ANY)


def _me():
    return lax.axis_index("x"), lax.axis_index("y"), lax.axis_index("c")


def _other_chips(x, y):
    return [(1 - x, y), (x, 1 - y), (1 - x, 1 - y)]


def _rcopy(src, dst, send, recv, dev):
    return pltpu.make_async_remote_copy(src_ref=src, dst_ref=dst, send_sem=send, recv_sem=recv,
                                        device_id=dev, device_id_type=MESH)


def _route_cut(rows, dtype):
    tile = 2 * SUBL * (4 // jnp.dtype(dtype).itemsize)
    return rows // 2 if rows % tile == 0 else rows


def _comm_call(name, body, arrs, out_shape, n_remote, aliases=None):
    n = len(arrs)
    return pl.pallas_call(
        body, name=name, in_specs=[ANY] * n, out_specs=[ANY] * len(out_shape), out_shape=out_shape,
        scratch_shapes=[pltpu.SemaphoreType.DMA((n, n_remote)), pltpu.SemaphoreType.DMA((n, n_remote))],
        input_output_aliases=aliases or {},
        compiler_params=pltpu.CompilerParams(has_side_effects=True),
    )(*arrs)


def gather_chips(name, arrs):
    n = len(arrs)
    cut = [_route_cut(a.shape[1], a.dtype) for a in arrs]

    def body(*refs):
        ins, outs = refs[:n], refs[n:2 * n]
        send, recv = refs[2 * n:]
        x, y, c = _me()
        jme, jx, jy, jd = 2 * x + y, 2 * (1 - x) + y, 2 * x + 1 - y, 2 * (1 - x) + 1 - y
        to_x, to_y, sib = (1 - x, y, c), (x, 1 - y, c), (x, y, 1 - c)

        def part(ref, a, hi):
            return ref.at[pl.ds(cut[a], ref.shape[0] - cut[a])] if hi else ref.at[pl.ds(0, cut[a])]

        def cp(a, k, ref, dev):
            return _rcopy(ref, ref, send.at[a, k], recv.at[a, k], dev)

        split = [a for a in range(n) if cut[a] < arrs[a].shape[1]]
        sent = [_rcopy(ins[a].at[c], outs[a].at[jme, c], send.at[a, k], recv.at[a, k], dev)
                for a in range(n) for k, dev in ((0, to_x), (1, to_y))]
        for s in sent:
            s.start()
        for a in range(n):
            blk = outs[a].at[jx, c]
            cp(a, 0, blk, to_x).wait_recv()
            sent += [cp(a, 2, part(blk, a, False), to_y), cp(a, 4, blk, sib)]
            sent[-2].start()
            sent[-1].start()
        for a in range(n):
            blk = outs[a].at[jy, c]
            cp(a, 1, blk, to_y).wait_recv()
            sent.append(cp(a, 5, blk, sib))
            sent[-1].start()
            if a in split:
                sent.append(cp(a, 3, part(blk, a, True), to_x))
                sent[-1].start()
        for a in range(n):
            lo = part(outs[a].at[jd, c], a, False)
            cp(a, 2, lo, to_y).wait_recv()
            sent.append(cp(a, 6, lo, sib))
            sent[-1].start()
        for a in split:
            hi = part(outs[a].at[jd, c], a, True)
            cp(a, 3, hi, to_x).wait_recv()
            sent.append(cp(a, 7, hi, sib))
            sent[-1].start()
        for a in range(n):
            cp(a, 4, outs[a].at[jx, 1 - c], sib).wait_recv()
            cp(a, 5, outs[a].at[jy, 1 - c], sib).wait_recv()
            cp(a, 6, part(outs[a].at[jd, 1 - c], a, False), sib).wait_recv()
        for a in split:
            cp(a, 7, part(outs[a].at[jd, 1 - c], a, True), sib).wait_recv()
        for s in sent:
            s.wait_send()

    out_shape = [jax.ShapeDtypeStruct((4,) + a.shape, a.dtype) for a in arrs]
    got = _comm_call(name, body, arrs, out_shape, 8)
    jme = 2 * lax.axis_index("x") + lax.axis_index("y")
    return [lax.dynamic_update_index_in_dim(g, a, jme, 0) for g, a in zip(got, arrs)]


def swap_halves(name, arrs):
    n = len(arrs)

    def body(*refs):
        ins, outs = refs[:n], refs[n:2 * n]
        send, recv = refs[2 * n:]
        x, y, c = _me()
        remote = [_rcopy(ins[a].at[1 - c], outs[a], send.at[a, 0], recv.at[a, 0], (x, y, 1 - c)) for a in range(n)]
        for cp in remote:
            cp.start()
        for cp in remote:
            cp.wait()

    return _comm_call(name, body, arrs, [jax.ShapeDtypeStruct(a.shape[1:], a.dtype) for a in arrs], 1)


def exchange_chips(name, arrs):
    n = len(arrs)
    cut = [_route_cut(a.shape[1], a.dtype) for a in arrs]
    split = [a for a in range(n) if cut[a] < arrs[a].shape[1]]

    def body(*refs):
        ins, outs, stage = refs[:n], refs[n:2 * n], refs[2 * n:3 * n]
        send, recv = refs[3 * n:]
        x, y, c = _me()
        jx, jy, jd = 2 * (1 - x) + y, 2 * x + 1 - y, 2 * (1 - x) + 1 - y
        to_x, to_y = (1 - x, y, c), (x, 1 - y, c)

        def part(ref, a, hi):
            return ref.at[pl.ds(cut[a], ref.shape[0] - cut[a])] if hi else ref.at[pl.ds(0, cut[a])]

        def cp(a, k, src, dst, dev):
            return _rcopy(src, dst, send.at[a, k], recv.at[a, k], dev)

        sent = []
        for a in range(n):
            sent += [cp(a, 0, ins[a].at[jx], outs[a].at[0], to_x), cp(a, 1, ins[a].at[jy], outs[a].at[1], to_y),
                     cp(a, 2, part(ins[a].at[jd], a, False), part(stage[a].at[0], a, False), to_x)]
            if a in split:
                sent.append(cp(a, 3, part(ins[a].at[jd], a, True), part(stage[a].at[1], a, True), to_y))
        for s in sent:
            s.start()
        for a in range(n):
            lo = part(stage[a].at[0], a, False)
            cp(a, 2, lo, lo, to_x).wait_recv()
            sent.append(cp(a, 4, lo, part(outs[a].at[2], a, False), to_y))
            sent[-1].start()
        for a in split:
            hi = part(stage[a].at[1], a, True)
            cp(a, 3, hi, hi, to_y).wait_recv()
            sent.append(cp(a, 5, hi, part(outs[a].at[2], a, True), to_x))
            sent[-1].start()
        for a in range(n):
            cp(a, 0, outs[a].at[0], outs[a].at[0], to_x).wait_recv()
            cp(a, 1, outs[a].at[1], outs[a].at[1], to_y).wait_recv()
            lo = part(outs[a].at[2], a, False)
            cp(a, 4, lo, lo, to_y).wait_recv()
        for a in split:
            hi = part(outs[a].at[2], a, True)
            cp(a, 5, hi, hi, to_x).wait_recv()
        for s in sent:
            s.wait_send()

    out_shape = [jax.ShapeDtypeStruct((3,) + a.shape[1:], a.dtype) for a in arrs]
    out_shape += [jax.ShapeDtypeStruct((2,) + a.shape[1:], a.dtype) for a in arrs]
    return _comm_call(name, body, arrs, out_shape, 6)[:n]


def gather_cores(name, arrs):
    n = len(arrs)

    def body(*refs):
        bufs = refs[n:2 * n]
        send, recv = refs[2 * n:]
        x, y, c = _me()
        remote = [_rcopy(bufs[a].at[c], bufs[a].at[c], send.at[a, 0], recv.at[a, 0], (x, y, 1 - c)) for a in range(n)]
        for cp in remote:
            cp.start()
        for a in range(n):
            _rcopy(bufs[a].at[1 - c], bufs[a].at[1 - c], send.at[a, 0], recv.at[a, 0], (x, y, 1 - c)).wait_recv()
        for cp in remote:
            cp.wait_send()

    return _comm_call(name, body, arrs, [jax.ShapeDtypeStruct(a.shape, a.dtype) for a in arrs], 1,
                      aliases={a: a for a in range(n)})


ROW_BLOCK = 256


def esum(name, terms, rows, width, out_dtype, out_slots=None):
    tr = next((t for t in range(min(rows, ROW_BLOCK), 0, -SUBL) if rows % t == 0 and t % SUBL == 0), rows)
    where =jnp.stack([lax.axis_index("c"), 2 * lax.axis_index("x") + lax.axis_index("y")]).astype(jnp.int32)
    pick = {"c": 0, "j": 1}

    def body(s_ref, *refs):
        acc = refs[0][...].astype(f32)
        for r in refs[1:-1]:
            acc = acc + r[...].astype(f32)
        refs[-1][...] = acc.astype(out_dtype)

    specs = []
    for arr, lead in terms:
        if lead is None:
            specs.append(pl.BlockSpec((tr, width), lambda i, s: (i, 0)))
        elif isinstance(lead, str):
            specs.append(pl.BlockSpec((None, tr, width), lambda i, s, lead=lead: (s[pick[lead]], i, 0)))
        else:
            specs.append(pl.BlockSpec((None, tr, width), lambda i, s, lead=lead: (lead, i, 0)))
    if out_slots is None:
        out_spec = pl.BlockSpec((tr, width), lambda i, s: (i, 0))
        out_shape = jax.ShapeDtypeStruct((rows, width), out_dtype)
    else:
        out_spec = pl.BlockSpec((None, tr, width), lambda i, s: (s[0], i, 0))
        out_shape = jax.ShapeDtypeStruct((out_slots, rows, width), out_dtype)
    return pl.pallas_call(
        body, name=name,
        grid_spec=pltpu.PrefetchScalarGridSpec(num_scalar_prefetch=1, grid=(rows // tr,), in_specs=specs, out_specs=out_spec),
        out_shape=out_shape, compiler_params=_cparams(1),
    )(where, *[t[0] for t in terms])


def reduce_to_shards(parts):
    tags = [str(k) for k in range(len(parts))]
    theirs = swap_halves("rs_swap", parts)
    t1 = []
    for tag, p, th in zip(tags, parts, theirs):
        _, _, h, w = p.shape
        t1.append(esum("rs_add_cores" + tag, [(p.reshape(2, 4 * h, w), "c"), (th.reshape(4 * h, w), None)],
                       4 * h, w, p.dtype).reshape(4, h, w))
    landed = exchange_chips("rs_exchange", t1)
    red = []
    for tag, p, t, got in zip(tags, parts, t1, landed):
        _, _, h, w = p.shape
        red.append(esum("rs_add_chips" + tag, [(t, "j"), (got, 0), (got, 1), (got, 2)], h, w, f32, out_slots=2))
    return gather_cores("rs_gather", red)


def _adamw_step(w_ref, g_ref, m_ref, v_ref, d_ref, nm_ref, nv_ref):
    gv = g_ref[...]
    nm = ADAM_B1 * m_ref[...] + (1.0 - ADAM_B1) * gv
    nv = ADAM_B2 * v_ref[...] + (1.0 - ADAM_B2) * jnp.square(gv)
    m_hat = nm / (1.0 - ADAM_B1 ** ADAM_STEP)
    v_hat = nv / (1.0 - ADAM_B2 ** ADAM_STEP)
    d_ref[...] = -ADAM_LR * (m_hat / (jnp.sqrt(v_hat) + ADAM_EPS) + ADAM_WD * w_ref[...])
    nm_ref[...] = nm
    nv_ref[...] = nv


def adamw(name, w, g, m, v):
    rows, width = w.shape
    tr = ROW_BLOCK if rows % ROW_BLOCK == 0 else rows

    def body(*refs):
        _adamw_step(*refs)

    spec = pl.BlockSpec((tr, width), lambda i: (i, 0))
    return pl.pallas_call(
        body, name=name, grid=(rows // tr,), in_specs=[spec] * 4, out_specs=[spec] * 3,
        out_shape=[jax.ShapeDtypeStruct((rows, width), f32)] * 3, compiler_params=_cparams(1),
    )(w, g, m, v)


def adamw_many(name, ws, gs, ms, vs):
    n = len(ws)

    def body(*refs):
        ins, outs = refs[:4 * n], refs[4 * n:]
        for k in range(n):
            _adamw_step(ins[k], ins[n + k], ins[2 * n + k], ins[3 * n + k], outs[k], outs[n + k], outs[2 * n + k])

    vmem = pl.BlockSpec(memory_space=pltpu.VMEM)
    res = pl.pallas_call(
        body, name=name, in_specs=[vmem] * (4 * n), out_specs=[vmem] * (3 * n),
        out_shape=[jax.ShapeDtypeStruct(w.shape, f32) for w in ws] * 3,
        compiler_params=pltpu.CompilerParams(vmem_limit_bytes=VMEM_LIMIT),
    )(*ws, *gs, *ms, *vs)
    return res[:n], res[n:2 * n], res[2 * n:]


PACK_W = 1024


def _pack(parts, halves, row_mult):
    flat = jnp.concatenate([p.reshape(-1) for p in parts])
    per = halves * row_mult * PACK_W
    total = -(-flat.size // per) * per
    flat = jnp.pad(flat, (0, total - flat.size))
    return flat.reshape(halves, total // (halves * PACK_W), PACK_W)


def _unpack(flat, shapes):
    out, pos = [], 0
    for s in shapes:
        n = int(np.prod(s))
        out.append(flat[pos:pos + n].reshape(s))
        pos += n
    return out


SHARDED_BIG = ("w_in", "w_branch", "w_out", "s5_w_glu")
SHARDED_SMALL = ("m2_conv_w", "sc_conv_w", "merge_b")
SHARD_AXIS = {"w_in": 2, "w_branch": 3, "w_out": 1, "s5_w_glu": 1, "m2_conv_w": 2, "sc_conv_w": 2, "merge_b": 2}
REPLICATED = ("norm_w", "s5_lambda_re", "s5_lambda_im", "s5_b_re", "s5_b_im", "s5_c_re", "s5_c_im", "s5_d", "s5_log_step",
              "sgu_ln_w", "sgu_ln_b", "sgu_w", "sgu_b", "m2_conv_b", "m2_dt_bias", "m2_a_log", "m2_d", "m2_norm_w")
WEIGHTS = ("norm_w", "w_in", "s5_lambda_re", "s5_lambda_im", "s5_b_re", "s5_b_im", "s5_c_re", "s5_c_im", "s5_d",
           "s5_log_step", "s5_w_glu", "sgu_ln_w", "sgu_ln_b", "sgu_w", "sgu_b", "m2_conv_w", "m2_conv_b", "m2_dt_bias",
           "m2_a_log", "m2_d", "m2_norm_w", "sc_conv_w", "merge_b", "w_branch", "w_out", "final_norm_w")
N_LAYERS = 2


SHARD_W = IN_DIM // 4


def _kernel_pieces():
    out, pos = [], 0
    for s, n in _col_segments():
        while n:
            take = min(n, SHARD_W - s % SHARD_W)
            out.append((pos, s, take))
            pos, s, n = pos + take, s + take, n - take
    return out


def _kernel_cols_from_shards(blocks):
    parts = [blocks[s // SHARD_W][:, s % SHARD_W:s % SHARD_W + n] for _, s, n in _kernel_pieces()]
    parts.append(jnp.zeros((blocks.shape[1], PW - IN_DIM), blocks.dtype))
    return jnp.concatenate(parts, axis=1)


def _shards_from_kernel_cols(main, dt, dtype):
    blocks = []
    for j in range(4):
        mine = sorted((s, pos, n) for pos, s, n in _kernel_pieces() if s // SHARD_W == j)
        parts = [(main[:, pos:pos + n] if pos < PW_MAIN else dt[:, pos - PW_MAIN:pos - PW_MAIN + n]).astype(dtype)
                 for _, pos, n in mine]
        blocks.append(jnp.concatenate(parts, axis=1))
    return jnp.stack(blocks)


def _gather_weights(w):
    arrs = [w["w_in"].astype(bf16), w["w_branch"].reshape(N_LAYERS, N_BRANCH * BW, -1).astype(bf16),
            w["w_out"].astype(bf16), w["s5_w_glu"].astype(bf16), w["m2_conv_w"], w["sc_conv_w"], w["merge_b"]]
    got = gather_chips("ag_weights", arrs)
    cols = lambda t: jnp.transpose(t, (1, 0, 2)).reshape(t.shape[1], -1)
    layers = []
    for i in range(N_LAYERS):
        p = {k: w[k][i] for k in REPLICATED}
        p["w_in"] = _kernel_cols_from_shards(got[0][:, i])
        p["w_branch"] = cols(got[1][:, i])
        p["w_out"] = got[2][:, i].reshape(D_MODEL, D_MODEL)
        p["s5_w_glu"] = got[3][:, i].reshape(BW, BW)
        p["m2_conv_w"], p["sc_conv_w"], p["merge_b"] = cols(got[4][:, i]), cols(got[5][:, i]), cols(got[6][:, i])
        layers.append(p)
    return layers


def _reduce_grads(grads, d_final, w):
    to_chips = lambda t: jnp.transpose(t.reshape(t.shape[0], 4, -1), (1, 0, 2))
    stack = lambda f: jnp.stack([f(g) for g in grads])
    parts = [stack(lambda g: _shards_from_kernel_cols(g["w_in"][0], g["w_in"][1], bf16)),
             stack(lambda g: to_chips(g["w_branch"].reshape(N_BRANCH * BW, D_MODEL)).astype(bf16)),
             stack(lambda g: g["w_out"].reshape(4, D_MODEL // 4, D_MODEL).astype(bf16)),
             stack(lambda g: g["s5_w_glu"].reshape(4, BW // 4, BW).astype(bf16))]
    rep = jnp.concatenate([stack(lambda g: g[k]).reshape(-1) for k in REPLICATED] + [d_final.reshape(-1)])
    quarter = -(-rep.size // (4 * 2 * SUBL * PACK_W)) * (2 * SUBL * PACK_W)
    rep = jnp.pad(rep, (0, 4 * quarter - rep.size))
    small = []
    for j in range(4):
        sharded = [stack(lambda g: to_chips(g[k])[j]) for k in SHARDED_SMALL]
        small.append(_pack(sharded + [rep[j * quarter:(j + 1) * quarter]], 2, SUBL))
    parts.append(jnp.stack(small, axis=1))
    red = reduce_to_shards(parts)
    out = {"w_in": red[0], "w_branch": red[1].reshape(w["w_branch"].shape), "w_out": red[2], "s5_w_glu": red[3]}
    small_flat = red[4].reshape(-1)
    n_small = sum(int(np.prod(w[k].shape)) for k in SHARDED_SMALL)
    out.update(zip(SHARDED_SMALL, _unpack(small_flat, [w[k].shape for k in SHARDED_SMALL])))
    mine = small_flat[n_small:n_small + quarter].reshape(2, quarter // (2 * PACK_W), PACK_W)
    rep_all = gather_chips("ag_small_grads", [mine])[0].reshape(-1)
    names = REPLICATED + ("final_norm_w",)
    out.update(zip(names, _unpack(rep_all, [w[k].shape for k in names])))
    return out


def _update(w, g, m, v):
    d, nm, nv = {}, {}, {}
    flat2 = lambda a: a.reshape(-1, a.shape[-1])
    for k in SHARDED_BIG:
        res = adamw("adamw_" + k, *[flat2(t[k]) for t in (w, g, m, v)])
        d[k], nm[k], nv[k] = (r.reshape(w[k].shape) for r in res)
    for k in ("s5_b_re", "s5_b_im"):
        res = adamw("adamw_" + k, *[flat2(t[k]) for t in (w, g, m, v)])
        d[k], nm[k], nv[k] = (r.reshape(w[k].shape) for r in res)
    rest = [k for k in WEIGHTS if k not in SHARDED_BIG + ("s5_b_re", "s5_b_im")]
    two_d = lambda a: a.reshape(1, -1) if a.ndim == 1 else a
    res = adamw_many("adamw_rest", *[[two_d(t[k]) for k in rest] for t in (w, g, m, v)])
    for tgt, rs in zip((d, nm, nv), res):
        tgt.update({k: r.reshape(w[k].shape) for k, r in zip(rest, rs)})
    return d, nm, nv


def kernel(x, norm_w, w_in, s5_lambda_re, s5_lambda_im, s5_b_re, s5_b_im, s5_c_re, s5_c_im, s5_d, s5_log_step, s5_w_glu, sgu_ln_w, sgu_ln_b, sgu_w, sgu_b, m2_conv_w, m2_conv_b, m2_dt_bias, m2_a_log, m2_d, m2_norm_w, sc_conv_w, merge_b, w_branch, w_out, final_norm_w, loss_target, m_norm_w, m_w_in, m_s5_lambda_re, m_s5_lambda_im, m_s5_b_re, m_s5_b_im, m_s5_c_re, m_s5_c_im, m_s5_d, m_s5_log_step, m_s5_w_glu, m_sgu_ln_w, m_sgu_ln_b, m_sgu_w, m_sgu_b, m_m2_conv_w, m_m2_conv_b, m_m2_dt_bias, m_m2_a_log, m_m2_d, m_m2_norm_w, m_sc_conv_w, m_merge_b, m_w_branch, m_w_out, m_final_norm_w, v_norm_w, v_w_in, v_s5_lambda_re, v_s5_lambda_im, v_s5_b_re, v_s5_b_im, v_s5_c_re, v_s5_c_im, v_s5_d, v_s5_log_step, v_s5_w_glu, v_sgu_ln_w, v_sgu_ln_b, v_sgu_w, v_sgu_b, v_m2_conv_w, v_m2_conv_b, v_m2_dt_bias, v_m2_a_log, v_m2_d, v_m2_norm_w, v_sc_conv_w, v_merge_b, v_w_branch, v_w_out, v_final_norm_w):
    given = dict(locals())
    w = {k: given[k] for k in WEIGHTS}
    m = {k: given["m_" + k] for k in WEIGHTS}
    v = {k: given["v_" + k] for k in WEIGHTS}
    layers = _gather_weights(w)
    loss, dx, grads, d_final = local_step(x[0], loss_target[0], layers, final_norm_w)
    loss = lax.psum(loss, ("x", "y", "c"))
    g = _reduce_grads(grads, d_final, w)
    d, nm, nv = _update(w, g, m, v)
    return (loss, dx[None], *[g[k] for k in WEIGHTS], *[d[k] for k in WEIGHTS],
            *[nm[k] for k in WEIGHTS], *[nv[k] for k in WEIGHTS])
```

```python
import functools
from typing import Any, Callable, NamedTuple

import numpy as np
import jax
import jax.numpy as jnp
from jax import lax
from jax.experimental import pallas as pl
from jax.experimental.pallas import tpu as pltpu

f32 = jnp.float32
bf16 = jnp.bfloat16

D_MODEL = 1024
BW = 512
N_BRANCH = 4
EPS = 1e-6
S5_GROUPS, S5_P, S5_N = 32, 16, 64
S5_NS = S5_GROUPS * S5_N
CHUNK = 128
M2_HEADS, M2_HEAD_DIM, M2_GROUPS, M2_STATE = 8, 64, 2, 128
IN_DIM = 10248
PW = 10368
PW_MAIN = 10240
LANES = 128
VMEM_LIMIT = 60 * 1024 * 1024

ADAM_LR, ADAM_B1, ADAM_B2, ADAM_EPS, ADAM_WD, ADAM_STEP = 0.001, 0.9, 0.999, 1e-08, 0.01, 10

C_MERGE = 0
C_SC = 4096
C_SGU = 6144
C_S5G = 7680
C_S5U = 8192
C_M2Z = 8704
C_XBC = 9216
C_DT = 10240


def _col_segments():
    segs = [(6152, 4096)]
    for j in range(4):
        segs += [(4104 + 128 * j, 128), (4616 + 128 * j, 128), (5128 + 128 * j, 128), (5640 + 128 * j, 128)]
    segs += [(1024, 1536), (512, 512), (0, 512), (2560, 512), (3072, 1024), (4096, 8)]
    return segs


def _to_kernel_cols(w):
    parts = [w[:, s:s + n] for s, n in _col_segments()]
    parts.append(jnp.zeros((w.shape[0], PW - IN_DIM), w.dtype))
    return jnp.concatenate(parts, axis=1)


def _from_kernel_cols(wp):
    out, pos = {}, 0
    for s, n in _col_segments():
        out[s] = wp[:, pos:pos + n]
        pos += n
    return jnp.concatenate([out[s] for s in sorted(out)], axis=1)


NN = ((1,), (0,))
NT = ((1,), (1,))
TN = ((0,), (0,))


def _bd(a, b, dims):
    return lax.dot_general(a.astype(bf16), b.astype(bf16), (dims, ((), ())), preferred_element_type=f32)


def _hd(a, b, dims):
    return lax.dot_general(a, b, (dims, ((), ())), precision=lax.Precision.HIGHEST, preferred_element_type=f32)


def _make_dots(raw):
    @jax.custom_vjp
    def nn(a, b):
        return raw(a, b, NN)
    nn.defvjp(lambda a, b: (raw(a, b, NN), (a, b)), lambda r, g: (raw(g, r[1], NT), raw(r[0], g, TN)))

    @jax.custom_vjp
    def nt(a, b):
        return raw(a, b, NT)
    nt.defvjp(lambda a, b: (raw(a, b, NT), (a, b)), lambda r, g: (raw(g, r[1], NN), raw(g, r[0], TN)))

    @jax.custom_vjp
    def tn(a, b):
        return raw(a, b, TN)
    tn.defvjp(lambda a, b: (raw(a, b, TN), (a, b)), lambda r, g: (raw(r[1], g, NT), raw(r[0], g, NN)))
    return nn, nt, tn


bdot, bdot_nt, bdot_tn = _make_dots(_bd)
hdot, hdot_nt, hdot_tn = _make_dots(_hd)


@jax.custom_vjp
def bdot_w(a, w, shadow):
    return _bd(a, w, NN)


bdot_w.defvjp(lambda a, w, s: (_bd(a, w, NN), (a, w)),
              lambda r, g: (_bd(g, r[1], NT), jnp.zeros_like(r[1]), _bd(r[0], g, TN)))


def _rows(shape):
    return lax.broadcasted_iota(jnp.int32, shape, 0)


def _cols(shape):
    return lax.broadcasted_iota(jnp.int32, shape, 1)


def _shift_down(x, s):
    return jnp.where(_rows(x.shape) < s, 0.0, pltpu.roll(x, s, 0))


def _shift_up(x, s):
    n = x.shape[0]
    return jnp.where(_rows(x.shape) >= n - s, 0.0, pltpu.roll(x, n - s, 0))


@functools.partial(jax.custom_vjp, nondiff_argnums=(1,))
def shift(x, s):
    return _shift_down(x, s) if s else x


shift.defvjp(lambda x, s: (shift(x, s), None), lambda s, _, g: (_shift_up(g, s) if s else g,))


def _row_of(w, k):
    return jnp.sum(jnp.where(_rows(w.shape) == k, w, 0.0), axis=0, keepdims=True)


def _lane_mask(width, lo, hi):
    c = _cols((1, width))
    return ((c >= lo) & (c < hi)).astype(f32)


def _expand(rows, width, per):
    return (_cols((rows, width)) // per == _rows((rows, width))).astype(f32)


class A(NamedTuple):
    arr: Any
    block: tuple
    imap: Callable
    shadow: bool = False


class O(NamedTuple):
    shape: tuple
    dtype: Any
    block: tuple
    imap: Callable
    alias: Any = None


class R(NamedTuple):
    arg: int
    out: int
    off: Any = None
    acc: bool = False


def _cparams(n_grid):
    return pltpu.CompilerParams(dimension_semantics=("arbitrary",) * n_grid, vmem_limit_bytes=VMEM_LIMIT)


def _ispec(block, imap, n, rev):
    if rev:
        return pl.BlockSpec(block, lambda i: imap(n - 1 - i))
    return pl.BlockSpec(block, imap)


def _load(ref, a):
    v = ref[...]
    if a.shadow:
        return (v, jnp.zeros(v.shape, f32))
    return v.astype(f32)


def _save_spec(shape, n, rev):
    nd = len(shape)
    return _ispec((None,) + tuple(shape), lambda i: (i,) + (0,) * nd, n, rev)


def block_fwd(name, f, n, args, outs, carries=()):
    n_in, n_out, n_c = len(args), len(outs), len(carries)

    def body(*refs):
        ins, out_r = refs[:n_in], refs[n_in:n_in + n_out]
        saves, cs = refs[n_in + n_out:n_in + n_out + n_c], refs[n_in + n_out + n_c:]
        if n_c:
            @pl.when(pl.program_id(0) == 0)
            def _():
                for c in cs:
                    c[...] = jnp.zeros(c.shape, f32)
        vals = [_load(r, a) for r, a in zip(ins, args)]
        cv = [c[...] for c in cs]
        for s, v in zip(saves, cv):
            s[...] = v
        res = f(*vals, *cv)
        for r, v in zip(out_r, res[:n_out]):
            r[...] = v.astype(r.dtype)
        for c, v in zip(cs, res[n_out:]):
            c[...] = v

    out_shape = [jax.ShapeDtypeStruct(o.shape, o.dtype) for o in outs]
    out_specs = [pl.BlockSpec(o.block, o.imap) for o in outs]
    for shp in carries:
        out_shape.append(jax.ShapeDtypeStruct((n,) + tuple(shp), f32))
        out_specs.append(_save_spec(shp, n, False))
    return pl.pallas_call(
        body, name=name, grid=(n,),
        in_specs=[pl.BlockSpec(a.block, a.imap) for a in args],
        out_specs=out_specs, out_shape=out_shape,
        scratch_shapes=[pltpu.VMEM(tuple(shp), f32) for shp in carries],
        compiler_params=_cparams(1),
    )(*[a.arr for a in args])


def block_bwd(name, f, n, args, cots, gouts, routes, saved=(), rev=False):
    n_in, n_cot, n_c, n_go = len(args), len(cots), len(saved), len(gouts)
    diff = []
    for r in routes:
        if r.arg not in diff:
            diff.append(r.arg)
    aliases = [(k, o.alias) for k, o in enumerate(gouts) if o.alias is not None]

    def body(*refs):
        ins = refs[:n_in]
        cot_r = refs[n_in:n_in + n_cot]
        sav_r = refs[n_in + n_cot:n_in + n_cot + n_c]
        base = n_in + n_cot + n_c + len(aliases)
        go_r = refs[base:base + n_go]
        dcs = refs[base + n_go:]
        step = pl.program_id(0)
        if n_c:
            @pl.when(step == 0)
            def _():
                for d in dcs:
                    d[...] = jnp.zeros(d.shape, f32)
        vals = [_load(r, a) for r, a in zip(ins, args)]
        cv = [s[...] for s in sav_r]
        nd = len(diff)

        def g(*dv):
            full = list(vals)
            for idx, v in zip(diff, dv[:nd]):
                full[idx] = (vals[idx][0], v) if args[idx].shadow else v
            return tuple(f(*full, *dv[nd:]))

        primals = [vals[i][1] if args[i].shadow else vals[i] for i in diff] + cv
        _, vjp_fn = jax.vjp(g, *primals)
        ct = tuple([r[...].astype(f32) for r in cot_r] + [d[...] for d in dcs])
        grads = vjp_fn(ct)
        for r in routes:
            gr = grads[diff.index(r.arg)]
            ref = go_r[r.out]
            if r.acc:
                @pl.when(step == 0)
                def _(ref=ref, gr=gr):
                    ref[...] = gr.astype(ref.dtype)

                @pl.when(step > 0)
                def _(ref=ref, gr=gr):
                    ref[...] += gr.astype(ref.dtype)
            elif r.off is None:
                ref[...] = gr.astype(ref.dtype)
            else:
                ref[:, r.off:r.off + gr.shape[1]] = gr.astype(ref.dtype)
        for d, gr in zip(dcs, grads[nd:]):
            d[...] = gr

    in_specs = [_ispec(a.block, a.imap, n, rev) for a in list(args) + list(cots)]
    in_specs += [_save_spec(s.shape[1:], n, rev) for s in saved]
    in_specs += [pl.BlockSpec(memory_space=pl.ANY) for _ in aliases]
    operands = [a.arr for a in list(args) + list(cots)] + list(saved) + [arr for _, arr in aliases]
    io_alias = {n_in + n_cot + n_c + j: k for j, (k, _) in enumerate(aliases)}
    return pl.pallas_call(
        body, name=name, grid=(n,),
        in_specs=in_specs,
        out_specs=[_ispec(o.block, o.imap, n, rev) for o in gouts],
        out_shape=[jax.ShapeDtypeStruct(o.shape, o.dtype) for o in gouts],
        scratch_shapes=[pltpu.VMEM(tuple(s.shape[1:]), f32) for s in saved],
        input_output_aliases=io_alias,
        compiler_params=_cparams(1),
    )(*operands)


class Op(NamedTuple):
    arr: Any
    row: int = 0
    col: int = 0


def mm(name, mode, pairs, m, n, tm, tn, out_dtype=f32, add=None, out=None, out_col=0):
    tm, tn = min(tm, m), min(tn, n)
    assert m % tm == 0 and n % tn == 0
    in_specs, operands = [], []
    for a, b, k, _ in pairs:
        if mode == TN:
            assert a.row % k == 0 and a.col % tm == 0
            in_specs.append(pl.BlockSpec((k, tm), lambda j, i, a=a, k=k: (a.row // k, i + a.col // tm)))
        else:
            assert a.col % k == 0 and a.row % tm == 0
            in_specs.append(pl.BlockSpec((tm, k), lambda j, i, a=a, k=k: (i + a.row // tm, a.col // k)))
        if mode == NT:
            assert b.col % k == 0 and b.row % tn == 0
            in_specs.append(pl.BlockSpec((tn, k), lambda j, i, b=b, k=k: (j + b.row // tn, b.col // k)))
        else:
            assert b.row % k == 0 and b.col % tn == 0
            in_specs.append(pl.BlockSpec((k, tn), lambda j, i, b=b, k=k: (b.row // k, j + b.col // tn)))
        operands += [a.arr, b.arr]
    n_p = len(pairs)
    if add is not None:
        assert add.col % tn == 0
        in_specs.append(pl.BlockSpec((tm, tn), lambda j, i: (i, j + add.col // tn)))
        operands.append(add.arr)
    io_alias = {}
    if out is not None:
        assert out_col % tn == 0
        in_specs.append(pl.BlockSpec(memory_space=pl.ANY))
        operands.append(out)
        io_alias = {len(operands) - 1: 0}
        out_shape = jax.ShapeDtypeStruct(out.shape, out.dtype)
    else:
        out_shape = jax.ShapeDtypeStruct((m, n), out_dtype)
    signs = [p[3] for p in pairs]

    def body(*refs):
        o = refs[-1]
        acc = None
        for p in range(n_p):
            t = _bd(refs[2 * p][...], refs[2 * p + 1][...], mode)
            t = t if signs[p] > 0 else -t
            acc = t if acc is None else acc + t
        if add is not None:
            acc = acc + refs[2 * n_p][...].astype(f32)
        o[...] = acc.astype(o.dtype)

    return pl.pallas_call(
        body, name=name, grid=(n // tn, m // tm),
        in_specs=in_specs,
        out_specs=pl.BlockSpec((tm, tn), lambda j, i: (i, j + out_col // tn)),
        out_shape=out_shape, input_output_aliases=io_alias,
        compiler_params=_cparams(2),
    )(*operands)


SCAN_LANES = 512
SUBL = 8


def _cmul(p, q):
    return (p[0] * q[0] - p[1] * q[1], p[0] * q[1] + p[1] * q[0])


def _powers(a):
    a2 = _cmul(a, a)
    a4 = _cmul(a2, a2)
    a6 = _cmul(a4, a2)
    return [a, a2, _cmul(a2, a), a4, _cmul(a4, a), a6, _cmul(a6, a), _cmul(a4, a4)]


def _table(pw, order, w):
    row = _rows((SUBL, w))
    re = sum(jnp.where(row == t, pw[k][0], 0.0) for t, k in enumerate(order))
    im = sum(jnp.where(row == t, pw[k][1], 0.0) for t, k in enumerate(order))
    return re, im


def _pick_row(x, t):
    return jnp.sum(jnp.where(_rows(x.shape) == t, x, 0.0), axis=0, keepdims=True)


def s5_scan_fwd(bu_re, bu_im, a_re, a_im):
    seq, ns = bu_re.shape
    w, nb = SCAN_LANES, ns // SCAN_LANES

    def body(b_re, b_im, ar, ai, s_re, s_im):
        a = (ar[...], ai[...])
        pw = _powers(a)
        tab = _table(pw, list(range(SUBL)), w)
        row = _rows((SUBL, w))

        def step(i, carry):
            t0 = pl.multiple_of(i * SUBL, SUBL)
            x = (b_re[pl.ds(t0, SUBL), :], b_im[pl.ds(t0, SUBL), :])
            for d, k in ((1, 0), (2, 1), (4, 3)):
                sh = (jnp.where(row < d, 0.0, pltpu.roll(x[0], d, 0)), jnp.where(row < d, 0.0, pltpu.roll(x[1], d, 0)))
                t = _cmul(pw[k], sh)
                x = (x[0] + t[0], x[1] + t[1])
            t = _cmul(tab, carry)
            x = (x[0] + t[0], x[1] + t[1])
            s_re[pl.ds(t0, SUBL), :] = x[0]
            s_im[pl.ds(t0, SUBL), :] = x[1]
            return (x[0][SUBL - 1:, :], x[1][SUBL - 1:, :])

        z = jnp.zeros((1, w), f32)
        lax.fori_loop(0, seq // SUBL, step, (z, z), unroll=2)

    strip = pl.BlockSpec((seq, w), lambda j: (0, j))
    lane = pl.BlockSpec((1, w), lambda j: (0, j))
    return pl.pallas_call(
        body, name="s5_scan_fwd", grid=(nb,),
        in_specs=[strip, strip, lane, lane],
        out_specs=[strip, strip],
        out_shape=[jax.ShapeDtypeStruct((seq, ns), f32)] * 2,
        compiler_params=_cparams(1),
    )(bu_re, bu_im, a_re, a_im)


def s5_scan_bwd(ds_re, ds_im, s_re, s_im, a_re, a_im):
    seq, ns = ds_re.shape
    w, nb = SCAN_LANES, ns // SCAN_LANES
    nblk = seq // SUBL

    def body(g_re, g_im, sr, si, ar, ai, l_re, l_im, da_re, da_im):
        a = (ar[...], -ai[...])
        pw = _powers(a)
        tab = _table(pw, [SUBL - 1 - t for t in range(SUBL)], w)
        row = _rows((SUBL, w))

        def step(kk, carry):
            c_re, c_im, acc_re, acc_im = carry
            i = nblk - 1 - kk
            t0 = pl.multiple_of(i * SUBL, SUBL)
            x = (g_re[pl.ds(t0, SUBL), :], g_im[pl.ds(t0, SUBL), :])
            for d, k in ((1, 0), (2, 1), (4, 3)):
                sh = (jnp.where(row >= SUBL - d, 0.0, pltpu.roll(x[0], SUBL - d, 0)),
                      jnp.where(row >= SUBL - d, 0.0, pltpu.roll(x[1], SUBL - d, 0)))
                t = _cmul(pw[k], sh)
                x = (x[0] + t[0], x[1] + t[1])
            t = _cmul(tab, (c_re, c_im))
            x = (x[0] + t[0], x[1] + t[1])
            l_re[pl.ds(t0, SUBL), :] = x[0]
            l_im[pl.ds(t0, SUBL), :] = x[1]
            tp = pl.multiple_of(jnp.maximum(i - 1, 0) * SUBL, SUBL)
            live = (i > 0).astype(f32)
            p_re = _pick_row(sr[pl.ds(tp, SUBL), :], SUBL - 1) * live
            p_im = _pick_row(si[pl.ds(tp, SUBL), :], SUBL - 1) * live
            sp_re = jnp.where(row == 0, p_re, pltpu.roll(sr[pl.ds(t0, SUBL), :], 1, 0))
            sp_im = jnp.where(row == 0, p_im, pltpu.roll(si[pl.ds(t0, SUBL), :], 1, 0))
            acc_re = acc_re + x[0] * sp_re + x[1] * sp_im
            acc_im = acc_im + x[1] * sp_re - x[0] * sp_im
            return (x[0][:1, :], x[1][:1, :], acc_re, acc_im)

        z1 = jnp.zeros((1, w), f32)
        z8 = jnp.zeros((SUBL, w), f32)
        _, _, acc_re, acc_im = lax.fori_loop(0, nblk, step, (z1, z1, z8, z8), unroll=2)
        da_re[...] = jnp.sum(acc_re, axis=0, keepdims=True)
        da_im[...] = jnp.sum(acc_im, axis=0, keepdims=True)

    strip = pl.BlockSpec((seq, w), lambda j: (0, j))
    lane = pl.BlockSpec((1, w), lambda j: (0, j))
    return pl.pallas_call(
        body, name="s5_scan_bwd", grid=(nb,),
        in_specs=[strip, strip, strip, strip, lane, lane],
        out_specs=[strip, strip, lane, lane],
        out_shape=[jax.ShapeDtypeStruct((seq, ns), f32)] * 2 + [jax.ShapeDtypeStruct((1, ns), f32)] * 2,
        compiler_params=_cparams(1),
    )(ds_re, ds_im, s_re, s_im, a_re, a_im)


def _rms(x, w):
    return x * lax.rsqrt(jnp.mean(x * x, axis=-1, keepdims=True) + EPS) * w


def f_rms(x, w):
    return (_rms(x, w),)


def f_rms_res(x, w):
    return (_rms(x, w), x)


def f_s5_prep(lam_re, lam_im, log_step, b_re, b_im):
    e = _expand(log_step.shape[1], S5_NS, S5_N)
    step = hdot(jnp.exp(log_step), e)
    mag = jnp.exp(lam_re * step)
    ab_re, ab_im = mag * jnp.cos(lam_im * step), mag * jnp.sin(lam_im * step)
    den = lam_re * lam_re + lam_im * lam_im
    nr = ab_re - 1.0
    coef_re = (nr * lam_re + ab_im * lam_im) / den
    coef_im = (ab_im * lam_re - nr * lam_im) / den
    bb_re, bb_im = coef_re * b_re - coef_im * b_im, coef_re * b_im + coef_im * b_re
    sel = (_rows((BW, S5_P)) % S5_P == _cols((BW, S5_P))).astype(f32)
    blk = _rows((BW, S5_NS)) // S5_P == _cols((BW, S5_NS)) // S5_N
    rows_bd = lambda t: jnp.where(blk, hdot(sel, t), 0.0)
    return (ab_re, ab_im, rows_bd(bb_re), rows_bd(bb_im))


def f_s5_c(c_re, c_im):
    sel_t = (_cols((S5_P, BW)) % S5_P == _rows((S5_P, BW))).astype(f32)
    blk_t = _rows((S5_NS, BW)) // S5_N == _cols((S5_NS, BW)) // S5_P
    cols_bd = lambda t: jnp.where(blk_t, hdot_tn(t, sel_t), 0.0)
    return (cols_bd(c_re), cols_bd(c_im))


def f_s5_act(y_lin, u, gate, d, w_glu):
    y = jax.nn.gelu(y_lin + d * u)
    y = y * jax.nn.sigmoid(bdot_w(y, *w_glu))
    return (y * jax.nn.silu(gate),)


def f_sgu(u, v, gate, ln_w, ln_b, *rest):
    w_s, b_pad = rest[:8], rest[8]
    t = u.shape[0]
    u32, v32 = jax.nn.gelu(u), jax.nn.gelu(v)
    mu = jnp.mean(v32, axis=-1, keepdims=True)
    var = jnp.mean(jnp.square(v32 - mu), axis=-1, keepdims=True)
    vn = (v32 - mu) * lax.rsqrt(var + EPS) * ln_w + ln_b
    tri = _rows((t, t)) >= _cols((t, t))
    s = hdot_tn(b_pad, _expand(LANES, BW, BW // 8))
    for h in range(8):
        s = s + bdot(jnp.where(tri, w_s[h], 0.0), vn) * _lane_mask(BW, 64 * h, 64 * h + 64)
    return (u32 * s * jax.nn.silu(gate),)


def f_m2_conv(x, w, b):
    return (sum(_row_of(w, k) * shift(x, 3 - k) for k in range(4)) + b,)


def f_sc(bg, cg, h, gate, w):
    z = cg * h
    conv = sum(_row_of(w, k) * shift(z, 2 - k) for k in range(3))
    return (bg * conv * jax.nn.silu(gate),)


def f_m2(z, xc, b0, b1, c0, c1, dt_raw, dt_bias, a_log, d_par, norm_w, st):
    q = z.shape[0]
    x = jax.nn.silu(xc)
    bm, cm = (jax.nn.silu(b0), jax.nn.silu(b1)), (jax.nn.silu(c0), jax.nn.silu(c1))
    dt = jax.nn.softplus(dt_raw + dt_bias)
    da = dt * (-jnp.exp(a_log))
    tri = _rows((q, q)) >= _cols((q, q))
    acs = hdot(tri.astype(f32), da)
    e = _expand(LANES, BW, M2_HEAD_DIM)
    dt_f, acs_f = hdot(dt, e), hdot(acs, e)
    last = _rows((q, BW)) == q - 1
    alast_f = jnp.sum(jnp.where(last, acs_f, 0.0), axis=0, keepdims=True)
    xdt = x * dt_f
    xdec = xdt * jnp.exp(alast_f - acs_f)
    acs_t = acs.T
    st_new = st * jnp.exp(alast_f)
    y_diag, y_off = 0.0, 0.0
    for g in range(M2_GROUPS):
        gm = _lane_mask(BW, 256 * g, 256 * g + 256)
        cb = bdot_nt(cm[g], bm[g])
        st_new = st_new + bdot_tn(bm[g], xdec * gm)
        y_off = y_off + bdot(cm[g], st) * gm
        for hh in range(M2_HEADS // M2_GROUPS):
            h = g * (M2_HEADS // M2_GROUPS) + hh
            col = jnp.sum(jnp.where(_cols((q, LANES)) == h, acs, 0.0), axis=1, keepdims=True)
            row = jnp.sum(jnp.where(_rows((LANES, q)) == h, acs_t, 0.0), axis=0, keepdims=True)
            decay = jnp.exp(jnp.where(tri, col - row, -1e30))
            y_diag = y_diag + bdot(cb * decay, xdt) * _lane_mask(BW, 64 * h, 64 * h + 64)
    d_f = sum(jnp.sum(jnp.where(_cols((1, LANES)) == h, d_par, 0.0), axis=1, keepdims=True)
              * _lane_mask(BW, 64 * h, 64 * h + 64) for h in range(M2_HEADS))
    y = y_diag + y_off * jnp.exp(acs_f) + d_f * x
    y = y * jax.nn.silu(z)
    return (_rms(y, norm_w), st_new)


def f_gate_mix(*v):
    bo, lg, mb = v[0:4], v[4:8], v[8:12]
    return (sum(jax.nn.sigmoid(lg[k] + mb[k]) * bo[k] for k in range(N_BRANCH)),)


def _full(shape):
    nd = len(shape)
    return dict(block=tuple(shape), imap=lambda i: (0,) * nd)


def _param(arr, shadow=False):
    return A(arr, tuple(arr.shape), lambda i, nd=arr.ndim: (0,) * nd, shadow)


def _tb(arr, t, width, colblk):
    return A(arr, (t, width), lambda i: (i, colblk))


def _strip(arr, seq, colblk0, stride=1):
    return A(arr, (seq, LANES), lambda j: (0, colblk0 + stride * j))


def _s5_prep_args(p):
    lam_re = p["s5_lambda_re"].reshape(1, S5_NS)
    lam_im = p["s5_lambda_im"].reshape(1, S5_NS)
    log_step = jnp.pad(p["s5_log_step"].reshape(1, S5_GROUPS), ((0, 0), (0, LANES - S5_GROUPS)))
    b_lanes = lambda b: jnp.transpose(b, (2, 0, 1)).reshape(S5_P, S5_NS)
    return [_param(v) for v in (lam_re, lam_im, log_step, b_lanes(p["s5_b_re"]), b_lanes(p["s5_b_im"]))]


def _s5_c_args(p):
    c_lanes = lambda c: jnp.transpose(c, (1, 0, 2)).reshape(S5_P, S5_NS)
    return [_param(c_lanes(p["s5_c_re"])), _param(c_lanes(p["s5_c_im"]))]


def layer_fwd(x, p):
    seq = x.shape[0]
    nt = seq // CHUNK
    t2 = 256
    sv = {}
    hb = block_fwd("rms_in", f_rms, seq // t2, [_tb(x, t2, D_MODEL, 0), _param(p["norm_w"].reshape(1, D_MODEL))],
                   [O((seq, D_MODEL), bf16, (t2, D_MODEL), lambda i: (i, 0))])[0]
    proj = mm("mm_in", NN, [(Op(hb), Op(p["w_in"]), D_MODEL, 1)], seq, PW, 512, 1152)
    whole = lambda shape, dt: O(shape, dt, shape, lambda i: (0, 0))
    ab_re, ab_im, bb_re, bb_im = block_fwd("s5_prep", f_s5_prep, 1, _s5_prep_args(p),
                                           [whole((1, S5_NS), f32)] * 2 + [whole((BW, S5_NS), bf16)] * 2)
    cc_re, cc_im = block_fwd("s5_c", f_s5_c, 1, _s5_c_args(p), [whole((S5_NS, BW), bf16)] * 2)
    bu_re = mm("mm_bu_re", NN, [(Op(proj, 0, C_S5U), Op(bb_re), BW, 1)], seq, S5_NS, 512, 1024)
    bu_im = mm("mm_bu_im", NN, [(Op(proj, 0, C_S5U), Op(bb_im), BW, 1)], seq, S5_NS, 512, 1024)
    s_re, s_im = s5_scan_fwd(bu_re, bu_im, ab_re, ab_im)
    y_lin = mm("mm_s5y", NN, [(Op(s_re), Op(cc_re), S5_NS, 1), (Op(s_im), Op(cc_im), S5_NS, -1)], seq, BW, 256, BW)
    s5_act_args = [_tb(y_lin, CHUNK, BW, 0), _tb(proj, CHUNK, BW, C_S5U // BW), _tb(proj, CHUNK, BW, C_S5G // BW),
                   _param(p["s5_d"].reshape(1, BW)), _param(p["s5_w_glu"], True)]
    out_bw = O((seq, BW), f32, (CHUNK, BW), lambda i: (i, 0))
    y_a = block_fwd("s5_act", f_s5_act, nt, s5_act_args, [out_bw])[0]
    y_b = block_fwd("sgu", f_sgu, nt, _sgu_args(proj, p), [out_bw])[0]
    xc = block_fwd("m2_conv", f_m2_conv, 8, _m2_conv_args(proj, p, seq),
                   [O((seq, 2 * BW), f32, (seq, LANES), lambda j: (0, j))])[0]
    y_c, st_saved = block_fwd("m2_ssd", f_m2, nt, _m2_args(proj, xc, p), [out_bw], carries=[(M2_STATE, BW)])
    y_d = block_fwd("sc", f_sc, 4, _sc_args(proj, p, seq), [O((seq, BW), f32, (seq, LANES), lambda j: (0, j))])[0]
    ys = [y_a, y_b, y_c, y_d]
    bo = [mm(f"mm_branch{k}", NN, [(Op(ys[k]), Op(p["w_branch"], k * BW, 0), BW, 1)], seq, D_MODEL, 512, D_MODEL)
          for k in range(N_BRANCH)]
    merged = block_fwd("gate_mix", f_gate_mix, seq // t2, _mix_args(bo, proj, p, t2),
                       [O((seq, D_MODEL), bf16, (t2, D_MODEL), lambda i: (i, 0))])[0]
    x_new = mm("mm_out", NN, [(Op(merged), Op(p["w_out"]), D_MODEL, 1)], seq, D_MODEL, 512, D_MODEL, add=Op(x))
    sv.update(x=x, hb=hb, proj=proj, ab=(ab_re, ab_im), bb=(bb_re, bb_im), cc=(cc_re, cc_im), s=(s_re, s_im), y_lin=y_lin,
              xc=xc, st=st_saved, ys=ys, bo=bo, merged=merged)
    return x_new, sv


def _sgu_args(proj, p):
    c0 = C_SGU // BW
    args = [_tb(proj, CHUNK, BW, c0), _tb(proj, CHUNK, BW, c0 + 1), _tb(proj, CHUNK, BW, c0 + 2),
            _param(p["sgu_ln_w"].reshape(1, BW)), _param(p["sgu_ln_b"].reshape(1, BW))]
    args += [A(p["sgu_w"], (None, CHUNK, CHUNK), lambda i, h=h: (h, 0, 0)) for h in range(8)]
    args.append(_param(jnp.pad(p["sgu_b"], ((0, LANES - 8), (0, 0)))))
    return args


def _m2_conv_args(proj, p, seq):
    return [_strip(proj, seq, C_XBC // LANES), A(p["m2_conv_w"], (4, LANES), lambda j: (0, j)),
            A(p["m2_conv_b"].reshape(1, 2 * BW), (1, LANES), lambda j: (0, j))]


def _pad_lanes(v):
    return jnp.pad(v.reshape(1, -1), ((0, 0), (0, LANES - v.size)))


def _m2_args(proj, xc, p):
    args = [_tb(proj, CHUNK, BW, C_M2Z // BW), _tb(xc, CHUNK, BW, 0)]
    args += [_tb(xc, CHUNK, LANES, 4 + k) for k in range(4)]
    args.append(_tb(proj, CHUNK, LANES, C_DT // LANES))
    args += [_param(_pad_lanes(p["m2_dt_bias"])), _param(_pad_lanes(p["m2_a_log"])), _param(_pad_lanes(p["m2_d"])),
             _param(p["m2_norm_w"].reshape(1, BW))]
    return args


def _sc_args(proj, p, seq):
    c0 = C_SC // LANES
    return [_strip(proj, seq, c0 + k, 4) for k in range(4)] + [A(p["sc_conv_w"], (3, LANES), lambda j: (0, j))]


def _mix_args(bo, proj, p, t):
    args = [_tb(bo[k], t, D_MODEL, 0) for k in range(N_BRANCH)]
    args += [_tb(proj, t, D_MODEL, k) for k in range(N_BRANCH)]
    mb = p["merge_b"].reshape(N_BRANCH, 1, D_MODEL)
    args += [A(mb, (None, 1, D_MODEL), lambda i, k=k: (k, 0, 0)) for k in range(N_BRANCH)]
    return args


def layer_bwd(d_out, p, sv):
    seq = d_out.shape[0]
    nt = seq // CHUNK
    t2 = 256
    proj, ys, bo = sv["proj"], sv["ys"], sv["bo"]
    g = {}
    acc = lambda shape: O(tuple(shape), f32, tuple(shape), lambda i, nd=len(shape): (0,) * nd)
    d_merged = mm("mm_out_dx", NT, [(Op(d_out), Op(p["w_out"]), D_MODEL, 1)], seq, D_MODEL, 512, D_MODEL, out_dtype=f32)
    g["w_out"] = mm("mm_out_dw", TN, [(Op(sv["merged"]), Op(d_out), seq, 1)], D_MODEL, D_MODEL, 256, D_MODEL, out_dtype=bf16)
    gouts = [O((seq, D_MODEL), bf16, (t2, D_MODEL), lambda i: (i, 0)) for _ in range(N_BRANCH)]
    gouts.append(O((seq, PW_MAIN), bf16, (t2, N_BRANCH * D_MODEL), lambda i: (i, 0)))
    gouts += [acc((1, D_MODEL)) for _ in range(N_BRANCH)]
    routes = [R(k, k) for k in range(N_BRANCH)] + [R(4 + k, 4, k * D_MODEL) for k in range(N_BRANCH)]
    routes += [R(8 + k, 5 + k, acc=True) for k in range(N_BRANCH)]
    res = block_bwd("gate_mix_bwd", f_gate_mix, seq // t2, _mix_args(bo, proj, p, t2),
                    [_tb(d_merged, t2, D_MODEL, 0)], gouts, routes)
    dbo, dproj = res[:4], res[4]
    g["merge_b"] = jnp.concatenate(res[5:9], axis=0)
    dys = [mm(f"mm_branch{k}_dx", NT, [(Op(dbo[k]), Op(p["w_branch"], k * BW, 0), D_MODEL, 1)], seq, BW, 512, BW)
           for k in range(N_BRANCH)]
    g["w_branch"] = jnp.stack([mm(f"mm_branch{k}_dw", TN, [(Op(ys[k]), Op(dbo[k]), seq, 1)], BW, D_MODEL, 256, D_MODEL, out_dtype=bf16)
                               for k in range(N_BRANCH)])
    res = block_bwd("sc_bwd", f_sc, 4, _sc_args(proj, p, seq), [_strip(dys[3], seq, 0)],
                    [O((seq, PW_MAIN), bf16, (seq, 4 * LANES), lambda j: (0, C_SC // (4 * LANES) + j), alias=dproj),
                     O((3, BW), f32, (3, LANES), lambda j: (0, j))],
                    [R(k, 0, k * LANES) for k in range(4)] + [R(4, 1)])
    dproj, g["sc_conv_w"] = res
    res = block_bwd("m2_ssd_bwd", f_m2, nt, _m2_args(proj, sv["xc"], p), [_tb(dys[2], CHUNK, BW, 0)],
                    [O((seq, PW_MAIN), bf16, (CHUNK, BW), lambda i: (i, C_M2Z // BW), alias=dproj),
                     O((seq, 2 * BW), f32, (CHUNK, 2 * BW), lambda i: (i, 0)),
                     O((seq, LANES), bf16, (CHUNK, LANES), lambda i: (i, 0)),
                     acc((1, LANES)), acc((1, LANES)), acc((1, LANES)), acc((1, BW))],
                    [R(0, 0), R(1, 1, 0)] + [R(2 + k, 1, BW + k * LANES) for k in range(4)] + [R(6, 2)]
                    + [R(7, 3, acc=True), R(8, 4, acc=True), R(9, 5, acc=True), R(10, 6, acc=True)],
                    saved=[sv["st"]], rev=True)
    dproj, dxc, d_dt = res[0], res[1], res[2]
    g["m2_dt_bias"], g["m2_a_log"], g["m2_d"] = (r[0, :M2_HEADS] for r in res[3:6])
    g["m2_norm_w"] = res[6].reshape(BW)
    res = block_bwd("m2_conv_bwd", f_m2_conv, 8, _m2_conv_args(proj, p, seq), [_strip(dxc, seq, 0)],
                    [O((seq, PW_MAIN), bf16, (seq, LANES), lambda j: (0, C_XBC // LANES + j), alias=dproj),
                     O((4, 2 * BW), f32, (4, LANES), lambda j: (0, j)), O((1, 2 * BW), f32, (1, LANES), lambda j: (0, j))],
                    [R(0, 0), R(1, 1), R(2, 2)])
    dproj, g["m2_conv_w"], cb = res
    g["m2_conv_b"] = cb.reshape(2 * BW)
    res = block_bwd("sgu_bwd", f_sgu, nt, _sgu_args(proj, p), [_tb(dys[1], CHUNK, BW, 0)],
                    [O((seq, PW_MAIN), bf16, (CHUNK, 3 * BW), lambda i: (i, C_SGU // (3 * BW)), alias=dproj),
                     acc((1, BW)), acc((1, BW))] + [acc((CHUNK, CHUNK)) for _ in range(8)] + [acc((LANES, CHUNK))],
                    [R(0, 0, 0), R(1, 0, BW), R(2, 0, 2 * BW), R(3, 1, acc=True), R(4, 2, acc=True)]
                    + [R(5 + h, 3 + h, acc=True) for h in range(8)] + [R(13, 11, acc=True)])
    dproj = res[0]
    g["sgu_ln_w"], g["sgu_ln_b"] = res[1].reshape(BW), res[2].reshape(BW)
    g["sgu_w"] = jnp.stack(res[3:11])
    g["sgu_b"] = res[11][:8]
    y_lin, (s_re, s_im), (ab_re, ab_im) = sv["y_lin"], sv["s"], sv["ab"]
    s5_act_args = [_tb(y_lin, CHUNK, BW, 0), _tb(proj, CHUNK, BW, C_S5U // BW), _tb(proj, CHUNK, BW, C_S5G // BW),
                   _param(p["s5_d"].reshape(1, BW)), _param(p["s5_w_glu"], True)]
    res = block_bwd("s5_act_bwd", f_s5_act, nt, s5_act_args, [_tb(dys[0], CHUNK, BW, 0)],
                    [O((seq, BW), bf16, (CHUNK, BW), lambda i: (i, 0)), O((seq, BW), f32, (CHUNK, BW), lambda i: (i, 0)),
                     O((seq, PW_MAIN), bf16, (CHUNK, BW), lambda i: (i, C_S5G // BW), alias=dproj),
                     acc((1, BW)), acc((BW, BW))],
                    [R(0, 0), R(1, 1), R(2, 2), R(3, 3, acc=True), R(4, 4, acc=True)])
    dy_lin, du1, dproj = res[0], res[1], res[2]
    g["s5_d"] = res[3].reshape(S5_GROUPS, S5_P)
    g["s5_w_glu"] = res[4]
    (bb_re, bb_im), (cc_re, cc_im) = sv["bb"], sv["cc"]
    ds_re = mm("mm_s5y_dre", NT, [(Op(dy_lin), Op(cc_re), BW, 1)], seq, S5_NS, 512, 1024)
    ds_im = mm("mm_s5y_dim", NT, [(Op(dy_lin), Op(cc_im), BW, -1)], seq, S5_NS, 512, 1024)
    dc_re = mm("mm_s5y_dcre", TN, [(Op(s_re), Op(dy_lin), seq, 1)], S5_NS, BW, 512, BW)
    dc_im = mm("mm_s5y_dcim", TN, [(Op(s_im), Op(dy_lin), seq, -1)], S5_NS, BW, 512, BW)
    l_re, l_im, da_re, da_im = s5_scan_bwd(ds_re, ds_im, s_re, s_im, ab_re, ab_im)
    dproj = mm("mm_bu_dx", NT, [(Op(l_re), Op(bb_re), S5_NS, 1), (Op(l_im), Op(bb_im), S5_NS, 1)],
               seq, BW, 512, BW, add=Op(du1), out=dproj, out_col=C_S5U)
    dbb = [mm(f"mm_bu_dw{n}", TN, [(Op(proj, 0, C_S5U), Op(l), seq, 1)], BW, S5_NS, 256, 1024)
           for n, l in (("re", l_re), ("im", l_im))]
    gouts = [acc((1, S5_NS)), acc((1, S5_NS)), acc((1, LANES))] + [acc((S5_P, S5_NS))] * 2
    res = block_bwd("s5_prep_bwd", f_s5_prep, 1, _s5_prep_args(p),
                    [_param(v) for v in (da_re, da_im, dbb[0], dbb[1])], gouts, [R(k, k, acc=True) for k in range(5)])
    g["s5_lambda_re"], g["s5_lambda_im"] = res[0].reshape(S5_GROUPS, S5_N), res[1].reshape(S5_GROUPS, S5_N)
    g["s5_log_step"] = res[2][0, :S5_GROUPS]
    b_natural = lambda b: jnp.transpose(b.reshape(S5_P, S5_GROUPS, S5_N), (1, 2, 0))
    c_natural = lambda c: jnp.transpose(c.reshape(S5_P, S5_GROUPS, S5_N), (1, 0, 2))
    g["s5_b_re"], g["s5_b_im"] = b_natural(res[3]), b_natural(res[4])
    res = block_bwd("s5_c_bwd", f_s5_c, 1, _s5_c_args(p), [_param(dc_re), _param(dc_im)],
                    [acc((S5_P, S5_NS))] * 2, [R(0, 0, acc=True), R(1, 1, acc=True)])
    g["s5_c_re"], g["s5_c_im"] = c_natural(res[0]), c_natural(res[1])
    hb, w_in = sv["hb"], p["w_in"]
    dh = mm("mm_in_dx", NT, [(Op(dproj), Op(w_in), PW_MAIN, 1), (Op(d_dt), Op(w_in, 0, PW_MAIN), LANES, 1)],
            seq, D_MODEL, 256, 256)
    g["w_in"] = (mm("mm_in_dw", TN, [(Op(hb), Op(dproj), seq, 1)], D_MODEL, PW_MAIN, 256, 1024, out_dtype=bf16),
                 mm("mm_in_dwdt", TN, [(Op(hb), Op(d_dt), seq, 1)], D_MODEL, LANES, 256, LANES, out_dtype=bf16))
    dx, dnw = block_bwd("rms_in_bwd", f_rms_res, seq // t2, [_tb(sv["x"], t2, D_MODEL, 0), _param(p["norm_w"].reshape(1, D_MODEL))],
                        [_tb(dh, t2, D_MODEL, 0), _tb(d_out, t2, D_MODEL, 0)],
                        [O((seq, D_MODEL), f32, (t2, D_MODEL), lambda i: (i, 0)), acc((1, D_MODEL))],
                        [R(0, 0), R(1, 1, acc=True)])
    g["norm_w"] = dnw.reshape(D_MODEL)
    return dx, g


def loss_head(x, w, target):
    seq = x.shape[0]
    t = 256

    def body(x_ref, w_ref, t_ref, loss_ref, dx_ref, dw_ref):
        step = pl.program_id(0)

        def f(xv, wv):
            err = _rms(xv, wv) - t_ref[...]
            return 0.5 * jnp.sum(jnp.mean(err * err, axis=-1, keepdims=True), axis=0, keepdims=True)

        val, vjp_fn = jax.vjp(f, x_ref[...], w_ref[...])
        dx, dw = vjp_fn(jnp.ones((1, 1), f32))
        dx_ref[...] = dx

        @pl.when(step == 0)
        def _():
            loss_ref[...] = jnp.broadcast_to(val, loss_ref.shape)
            dw_ref[...] = dw

        @pl.when(step > 0)
        def _():
            loss_ref[...] += jnp.broadcast_to(val, loss_ref.shape)
            dw_ref[...] += dw

    blk = pl.BlockSpec((t, D_MODEL), lambda i: (i, 0))
    row = pl.BlockSpec((1, D_MODEL), lambda i: (0, 0))
    return pl.pallas_call(
        body, name="loss_head", grid=(seq // t,),
        in_specs=[blk, row, blk],
        out_specs=[pl.BlockSpec((1, LANES), lambda i: (0, 0)), blk, row],
        out_shape=[jax.ShapeDtypeStruct((1, LANES), f32), jax.ShapeDtypeStruct((seq, D_MODEL), f32),
                   jax.ShapeDtypeStruct((1, D_MODEL), f32)],
        compiler_params=_cparams(1),
    )(x, w.reshape(1, D_MODEL), target)


LAYER_KEYS = ("norm_w", "w_in", "s5_lambda_re", "s5_lambda_im", "s5_b_re", "s5_b_im", "s5_c_re", "s5_c_im", "s5_d",
              "s5_log_step", "s5_w_glu", "sgu_ln_w", "sgu_ln_b", "sgu_w", "sgu_b", "m2_conv_w", "m2_conv_b",
              "m2_dt_bias", "m2_a_log", "m2_d", "m2_norm_w", "sc_conv_w", "merge_b", "w_branch", "w_out")


def local_step(x, target, layers, final_norm_w):
    saved = []
    for p in layers:
        x, sv = layer_fwd(x, p)
        saved.append(sv)
    loss, dx, dfw = loss_head(x, final_norm_w, target)
    grads = []
    for p, sv in zip(reversed(layers), reversed(saved)):
        dx, g = layer_bwd(dx, p, sv)
        grads.append(g)
    return loss[0, 0], dx, grads[::-1], dfw.reshape(D_MODEL)


MESH = pl.DeviceIdType.MESH
ANY = pl.BlockSpec(memory_space=pl.ANY)


def _me():
    return lax.axis_index("x"), lax.axis_index("y"), lax.axis_index("c")


def _other_chips(x, y):
    return [(1 - x, y), (x, 1 - y), (1 - x, 1 - y)]


def _rcopy(src, dst, send, recv, dev):
    return pltpu.make_async_remote_copy(src_ref=src, dst_ref=dst, send_sem=send, recv_sem=recv,
                                        device_id=dev, device_id_type=MESH)


def _route_cut(rows, dtype):
    tile = 2 * SUBL * (4 // jnp.dtype(dtype).itemsize)
    return rows // 2 if rows % tile == 0 else rows


def _comm_call(name, body, arrs, out_shape, n_remote, aliases=None):
    n = len(arrs)
    return pl.pallas_call(
        body, name=name, in_specs=[ANY] * n, out_specs=[ANY] * len(out_shape), out_shape=out_shape,
        scratch_shapes=[pltpu.SemaphoreType.DMA((n, n_remote)), pltpu.SemaphoreType.DMA((n, n_remote))],
        input_output_aliases=aliases or {},
        compiler_params=pltpu.CompilerParams(has_side_effects=True),
    )(*arrs)


def gather_chips(name, arrs):
    n = len(arrs)
    cut = [_route_cut(a.shape[1], a.dtype) for a in arrs]

    def body(*refs):
        ins, outs = refs[:n], refs[n:2 * n]
        send, recv = refs[2 * n:]
        x, y, c = _me()
        jme, jx, jy, jd = 2 * x + y, 2 * (1 - x) + y, 2 * x + 1 - y, 2 * (1 - x) + 1 - y
        to_x, to_y, sib = (1 - x, y, c), (x, 1 - y, c), (x, y, 1 - c)

        def part(ref, a, hi):
            return ref.at[pl.ds(cut[a], ref.shape[0] - cut[a])] if hi else ref.at[pl.ds(0, cut[a])]

        def cp(a, k, ref, dev):
            return _rcopy(ref, ref, send.at[a, k], recv.at[a, k], dev)

        split = [a for a in range(n) if cut[a] < arrs[a].shape[1]]
        sent = [_rcopy(ins[a].at[c], outs[a].at[jme, c], send.at[a, k], recv.at[a, k], dev)
                for a in range(n) for k, dev in ((0, to_x), (1, to_y))]
        for s in sent:
            s.start()
        for a in range(n):
            blk = outs[a].at[jx, c]
            cp(a, 0, blk, to_x).wait_recv()
            sent += [cp(a, 2, part(blk, a, False), to_y), cp(a, 4, blk, sib)]
            sent[-2].start()
            sent[-1].start()
        for a in range(n):
            blk = outs[a].at[jy, c]
            cp(a, 1, blk, to_y).wait_recv()
            sent.append(cp(a, 5, blk, sib))
            sent[-1].start()
            if a in split:
                sent.append(cp(a, 3, part(blk, a, True), to_x))
                sent[-1].start()
        for a in range(n):
            lo = part(outs[a].at[jd, c], a, False)
            cp(a, 2, lo, to_y).wait_recv()
            sent.append(cp(a, 6, lo, sib))
            sent[-1].start()
        for a in split:
            hi = part(outs[a].at[jd, c], a, True)
            cp(a, 3, hi, to_x).wait_recv()
            sent.append(cp(a, 7, hi, sib))
            sent[-1].start()
        for a in range(n):
            cp(a, 4, outs[a].at[jx, 1 - c], sib).wait_recv()
            cp(a, 5, outs[a].at[jy, 1 - c], sib).wait_recv()
            cp(a, 6, part(outs[a].at[jd, 1 - c], a, False), sib).wait_recv()
        for a in split:
            cp(a, 7, part(outs[a].at[jd, 1 - c], a, True), sib).wait_recv()
        for s in sent:
            s.wait_send()

    out_shape = [jax.ShapeDtypeStruct((4,) + a.shape, a.dtype) for a in arrs]
    got = _comm_call(name, body, arrs, out_shape, 8)
    jme = 2 * lax.axis_index("x") + lax.axis_index("y")
    return [lax.dynamic_update_index_in_dim(g, a, jme, 0) for g, a in zip(got, arrs)]


def swap_halves(name, arrs):
    n = len(arrs)

    def body(*refs):
        ins, outs = refs[:n], refs[n:2 * n]
        send, recv = refs[2 * n:]
        x, y, c = _me()
        remote = [_rcopy(ins[a].at[1 - c], outs[a], send.at[a, 0], recv.at[a, 0], (x, y, 1 - c)) for a in range(n)]
        for cp in remote:
            cp.start()
        for cp in remote:
            cp.wait()

    return _comm_call(name, body, arrs, [jax.ShapeDtypeStruct(a.shape[1:], a.dtype) for a in arrs], 1)


def exchange_chips(name, arrs):
    n = len(arrs)

    def body(*refs):
        ins, outs = refs[:n], refs[n:2 * n]
        send, recv = refs[2 * n:]
        x, y, c = _me()
        remote = [_rcopy(ins[a].at[2 * cx + cy], outs[a].at[k], send.at[a, k], recv.at[a, k], (cx, cy, c))
                  for a in range(n) for k, (cx, cy) in enumerate(_other_chips(x, y))]
        for cp in remote:
            cp.start()
        for cp in remote:
            cp.wait()

    return _comm_call(name, body, arrs, [jax.ShapeDtypeStruct((3,) + a.shape[1:], a.dtype) for a in arrs], 3)


def gather_cores(name, arrs):
    n = len(arrs)

    def body(*refs):
        bufs = refs[n:2 * n]
        send, recv = refs[2 * n:]
        x, y, c = _me()
        remote = [_rcopy(bufs[a].at[c], bufs[a].at[c], send.at[a, 0], recv.at[a, 0], (x, y, 1 - c)) for a in range(n)]
        for cp in remote:
            cp.start()
        for a in range(n):
            _rcopy(bufs[a].at[1 - c], bufs[a].at[1 - c], send.at[a, 0], recv.at[a, 0], (x, y, 1 - c)).wait_recv()
        for cp in remote:
            cp.wait_send()

    return _comm_call(name, body, arrs, [jax.ShapeDtypeStruct(a.shape, a.dtype) for a in arrs], 1,
                      aliases={a: a for a in range(n)})


ROW_BLOCK = 256


def esum(name, terms, rows, width, out_dtype, out_slots=None):
    tr = next((t for t in range(min(rows, ROW_BLOCK), 0, -SUBL) if rows % t == 0 and t % SUBL == 0), rows)
    where =jnp.stack([lax.axis_index("c"), 2 * lax.axis_index("x") + lax.axis_index("y")]).astype(jnp.int32)
    pick = {"c": 0, "j": 1}

    def body(s_ref, *refs):
        acc = refs[0][...].astype(f32)
        for r in refs[1:-1]:
            acc = acc + r[...].astype(f32)
        refs[-1][...] = acc.astype(out_dtype)

    specs = []
    for arr, lead in terms:
        if lead is None:
            specs.append(pl.BlockSpec((tr, width), lambda i, s: (i, 0)))
        elif isinstance(lead, str):
            specs.append(pl.BlockSpec((None, tr, width), lambda i, s, lead=lead: (s[pick[lead]], i, 0)))
        else:
            specs.append(pl.BlockSpec((None, tr, width), lambda i, s, lead=lead: (lead, i, 0)))
    if out_slots is None:
        out_spec = pl.BlockSpec((tr, width), lambda i, s: (i, 0))
        out_shape = jax.ShapeDtypeStruct((rows, width), out_dtype)
    else:
        out_spec = pl.BlockSpec((None, tr, width), lambda i, s: (s[0], i, 0))
        out_shape = jax.ShapeDtypeStruct((out_slots, rows, width), out_dtype)
    return pl.pallas_call(
        body, name=name,
        grid_spec=pltpu.PrefetchScalarGridSpec(num_scalar_prefetch=1, grid=(rows // tr,), in_specs=specs, out_specs=out_spec),
        out_shape=out_shape, compiler_params=_cparams(1),
    )(where, *[t[0] for t in terms])


def reduce_to_shards(parts):
    tags = [str(k) for k in range(len(parts))]
    theirs = swap_halves("rs_swap", parts)
    t1 = []
    for tag, p, th in zip(tags, parts, theirs):
        _, _, h, w = p.shape
        t1.append(esum("rs_add_cores" + tag, [(p.reshape(2, 4 * h, w), "c"), (th.reshape(4 * h, w), None)],
                       4 * h, w, p.dtype).reshape(4, h, w))
    landed = exchange_chips("rs_exchange", t1)
    red = []
    for tag, p, t, got in zip(tags, parts, t1, landed):
        _, _, h, w = p.shape
        red.append(esum("rs_add_chips" + tag, [(t, "j"), (got, 0), (got, 1), (got, 2)], h, w, f32, out_slots=2))
    return gather_cores("rs_gather", red)


def _adamw_step(w_ref, g_ref, m_ref, v_ref, d_ref, nm_ref, nv_ref):
    gv = g_ref[...]
    nm = ADAM_B1 * m_ref[...] + (1.0 - ADAM_B1) * gv
    nv = ADAM_B2 * v_ref[...] + (1.0 - ADAM_B2) * jnp.square(gv)
    m_hat = nm / (1.0 - ADAM_B1 ** ADAM_STEP)
    v_hat = nv / (1.0 - ADAM_B2 ** ADAM_STEP)
    d_ref[...] = -ADAM_LR * (m_hat / (jnp.sqrt(v_hat) + ADAM_EPS) + ADAM_WD * w_ref[...])
    nm_ref[...] = nm
    nv_ref[...] = nv


def adamw(name, w, g, m, v, tr=None):
    rows, rest = w.shape[0], w.shape[1:]
    if tr is None:
        tr = ROW_BLOCK if rows % ROW_BLOCK == 0 else rows
    assert rows % tr == 0

    def body(*refs):
        _adamw_step(*refs)

    spec = pl.BlockSpec((tr,) + rest, lambda i: (i,) + (0,) * len(rest))
    return pl.pallas_call(
        body, name=name, grid=(rows // tr,), in_specs=[spec] * 4, out_specs=[spec] * 3,
        out_shape=[jax.ShapeDtypeStruct(w.shape, f32)] * 3, compiler_params=_cparams(1),
    )(w, g, m, v)


def adamw_many(name, ws, gs, ms, vs):
    n = len(ws)

    def body(*refs):
        ins, outs = refs[:4 * n], refs[4 * n:]
        for k in range(n):
            _adamw_step(ins[k], ins[n + k], ins[2 * n + k], ins[3 * n + k], outs[k], outs[n + k], outs[2 * n + k])

    vmem = pl.BlockSpec(memory_space=pltpu.VMEM)
    res = pl.pallas_call(
        body, name=name, in_specs=[vmem] * (4 * n), out_specs=[vmem] * (3 * n),
        out_shape=[jax.ShapeDtypeStruct(w.shape, f32) for w in ws] * 3,
        compiler_params=pltpu.CompilerParams(vmem_limit_bytes=VMEM_LIMIT),
    )(*ws, *gs, *ms, *vs)
    return res[:n], res[n:2 * n], res[2 * n:]


PACK_W = 1024


def _pack(parts, halves, row_mult):
    flat = jnp.concatenate([p.reshape(-1) for p in parts])
    per = halves * row_mult * PACK_W
    total = -(-flat.size // per) * per
    flat = jnp.pad(flat, (0, total - flat.size))
    return flat.reshape(halves, total // (halves * PACK_W), PACK_W)


def _unpack(flat, shapes):
    out, pos = [], 0
    for s in shapes:
        n = int(np.prod(s))
        out.append(flat[pos:pos + n].reshape(s))
        pos += n
    return out


SHARDED_BIG = ("w_in", "w_branch", "w_out", "s5_w_glu")
SHARDED_SMALL = ("m2_conv_w", "sc_conv_w", "merge_b")
SHARD_AXIS = {"w_in": 2, "w_branch": 3, "w_out": 1, "s5_w_glu": 1, "m2_conv_w": 2, "sc_conv_w": 2, "merge_b": 2}
REPLICATED = ("norm_w", "s5_lambda_re", "s5_lambda_im", "s5_b_re", "s5_b_im", "s5_c_re", "s5_c_im", "s5_d", "s5_log_step",
              "sgu_ln_w", "sgu_ln_b", "sgu_w", "sgu_b", "m2_conv_b", "m2_dt_bias", "m2_a_log", "m2_d", "m2_norm_w")
WEIGHTS = ("norm_w", "w_in", "s5_lambda_re", "s5_lambda_im", "s5_b_re", "s5_b_im", "s5_c_re", "s5_c_im", "s5_d",
           "s5_log_step", "s5_w_glu", "sgu_ln_w", "sgu_ln_b", "sgu_w", "sgu_b", "m2_conv_w", "m2_conv_b", "m2_dt_bias",
           "m2_a_log", "m2_d", "m2_norm_w", "sc_conv_w", "merge_b", "w_branch", "w_out", "final_norm_w")
N_LAYERS = 2


SHARD_W = IN_DIM // 4


def _kernel_pieces():
    out, pos = [], 0
    for s, n in _col_segments():
        while n:
            take = min(n, SHARD_W - s % SHARD_W)
            out.append((pos, s, take))
            pos, s, n = pos + take, s + take, n - take
    return out


def _kernel_cols_from_shards(blocks):
    parts = [blocks[s // SHARD_W][:, s % SHARD_W:s % SHARD_W + n] for _, s, n in _kernel_pieces()]
    parts.append(jnp.zeros((blocks.shape[1], PW - IN_DIM), blocks.dtype))
    return jnp.concatenate(parts, axis=1)


def _shards_from_kernel_cols(main, dt, dtype):
    blocks = []
    for j in range(4):
        mine = sorted((s, pos, n) for pos, s, n in _kernel_pieces() if s // SHARD_W == j)
        parts = [(main[:, pos:pos + n] if pos < PW_MAIN else dt[:, pos - PW_MAIN:pos - PW_MAIN + n]).astype(dtype)
                 for _, pos, n in mine]
        blocks.append(jnp.concatenate(parts, axis=1))
    return jnp.stack(blocks)


def _gather_weights(w):
    arrs = [w["w_in"].astype(bf16), w["w_branch"].reshape(N_LAYERS, N_BRANCH * BW, -1).astype(bf16),
            w["w_out"].astype(bf16), w["s5_w_glu"].astype(bf16), w["m2_conv_w"], w["sc_conv_w"], w["merge_b"]]
    got = gather_chips("ag_weights", arrs)
    cols = lambda t: jnp.transpose(t, (1, 0, 2)).reshape(t.shape[1], -1)
    layers = []
    for i in range(N_LAYERS):
        p = {k: w[k][i] for k in REPLICATED}
        p["w_in"] = _kernel_cols_from_shards(got[0][:, i])
        p["w_branch"] = cols(got[1][:, i])
        p["w_out"] = got[2][:, i].reshape(D_MODEL, D_MODEL)
        p["s5_w_glu"] = got[3][:, i].reshape(BW, BW)
        p["m2_conv_w"], p["sc_conv_w"], p["merge_b"] = cols(got[4][:, i]), cols(got[5][:, i]), cols(got[6][:, i])
        layers.append(p)
    return layers


def _reduce_grads(grads, d_final, w):
    to_chips = lambda t: jnp.transpose(t.reshape(t.shape[0], 4, -1), (1, 0, 2))
    stack = lambda f: jnp.stack([f(g) for g in grads])
    parts = [stack(lambda g: _shards_from_kernel_cols(g["w_in"][0], g["w_in"][1], bf16)),
             stack(lambda g: to_chips(g["w_branch"].reshape(N_BRANCH * BW, D_MODEL)).astype(bf16)),
             stack(lambda g: g["w_out"].reshape(4, D_MODEL // 4, D_MODEL).astype(bf16)),
             stack(lambda g: g["s5_w_glu"].reshape(4, BW // 4, BW).astype(bf16))]
    rep = jnp.concatenate([stack(lambda g: g[k]).reshape(-1) for k in REPLICATED] + [d_final.reshape(-1)])
    quarter = -(-rep.size // (4 * 2 * SUBL * PACK_W)) * (2 * SUBL * PACK_W)
    rep = jnp.pad(rep, (0, 4 * quarter - rep.size))
    small = []
    for j in range(4):
        sharded = [stack(lambda g: to_chips(g[k])[j]) for k in SHARDED_SMALL]
        small.append(_pack(sharded + [rep[j * quarter:(j + 1) * quarter]], 2, SUBL))
    parts.append(jnp.stack(small, axis=1))
    red = reduce_to_shards(parts)
    out = {"w_in": red[0], "w_branch": red[1].reshape(w["w_branch"].shape), "w_out": red[2], "s5_w_glu": red[3]}
    small_flat = red[4].reshape(-1)
    n_small = sum(int(np.prod(w[k].shape)) for k in SHARDED_SMALL)
    out.update(zip(SHARDED_SMALL, _unpack(small_flat, [w[k].shape for k in SHARDED_SMALL])))
    mine = small_flat[n_small:n_small + quarter].reshape(2, quarter // (2 * PACK_W), PACK_W)
    rep_all = gather_chips("ag_small_grads", [mine])[0].reshape(-1)
    names = REPLICATED + ("final_norm_w",)
    out.update(zip(names, _unpack(rep_all, [w[k].shape for k in names])))
    return out


def _update(w, g, m, v):
    d, nm, nv = {}, {}, {}
    flat2 = lambda a: a.reshape(-1, a.shape[-1])
    cols_major = lambda a: jnp.transpose(a, (2, 0, 1))
    res = adamw("adamw_w_in", *[cols_major(t["w_in"]) for t in (w, g, m, v)], tr=SHARD_W // 14)
    d["w_in"], nm["w_in"], nv["w_in"] = (jnp.transpose(r, (1, 2, 0)) for r in res)
    for k in SHARDED_BIG[1:]:
        res = adamw("adamw_" + k, *[flat2(t[k]) for t in (w, g, m, v)])
        d[k], nm[k], nv[k] = (r.reshape(w[k].shape) for r in res)
    for k in ("s5_b_re", "s5_b_im"):
        res = adamw("adamw_" + k, *[flat2(t[k]) for t in (w, g, m, v)])
        d[k], nm[k], nv[k] = (r.reshape(w[k].shape) for r in res)
    rest = [k for k in WEIGHTS if k not in SHARDED_BIG + ("s5_b_re", "s5_b_im")]
    two_d = lambda a: a.reshape(1, -1) if a.ndim == 1 else a
    res = adamw_many("adamw_rest", *[[two_d(t[k]) for k in rest] for t in (w, g, m, v)])
    for tgt, rs in zip((d, nm, nv), res):
        tgt.update({k: r.reshape(w[k].shape) for k, r in zip(rest, rs)})
    return d, nm, nv


def kernel(x, norm_w, w_in, s5_lambda_re, s5_lambda_im, s5_b_re, s5_b_im, s5_c_re, s5_c_im, s5_d, s5_log_step, s5_w_glu, sgu_ln_w, sgu_ln_b, sgu_w, sgu_b, m2_conv_w, m2_conv_b, m2_dt_bias, m2_a_log, m2_d, m2_norm_w, sc_conv_w, merge_b, w_branch, w_out, final_norm_w, loss_target, m_norm_w, m_w_in, m_s5_lambda_re, m_s5_lambda_im, m_s5_b_re, m_s5_b_im, m_s5_c_re, m_s5_c_im, m_s5_d, m_s5_log_step, m_s5_w_glu, m_sgu_ln_w, m_sgu_ln_b, m_sgu_w, m_sgu_b, m_m2_conv_w, m_m2_conv_b, m_m2_dt_bias, m_m2_a_log, m_m2_d, m_m2_norm_w, m_sc_conv_w, m_merge_b, m_w_branch, m_w_out, m_final_norm_w, v_norm_w, v_w_in, v_s5_lambda_re, v_s5_lambda_im, v_s5_b_re, v_s5_b_im, v_s5_c_re, v_s5_c_im, v_s5_d, v_s5_log_step, v_s5_w_glu, v_sgu_ln_w, v_sgu_ln_b, v_sgu_w, v_sgu_b, v_m2_conv_w, v_m2_conv_b, v_m2_dt_bias, v_m2_a_log, v_m2_d, v_m2_norm_w, v_sc_conv_w, v_merge_b, v_w_branch, v_w_out, v_final_norm_w):
    given = dict(locals())
    w = {k: given[k] for k in WEIGHTS}
    m = {k: given["m_" + k] for k in WEIGHTS}
    v = {k: given["v_" + k] for k in WEIGHTS}
    layers = _gather_weights(w)
    loss, dx, grads, d_final = local_step(x[0], loss_target[0], layers, final_norm_w)
    loss = lax.psum(loss, ("x", "y", "c"))
    g = _reduce_grads(grads, d_final, w)
    d, nm, nv = _update(w, g, m, v)
    return (loss, dx[None], *[g[k] for k in WEIGHTS], *[d[k] for k in WEIGHTS],
            *[nm[k] for k in WEIGHTS], *[nv[k] for k in WEIGHTS])
```

```python
import functools
from typing import Any, Callable, NamedTuple

import numpy as np
import jax
import jax.numpy as jnp
from jax import lax
from jax.experimental import pallas as pl
from jax.experimental.pallas import tpu as pltpu

f32 = jnp.float32
bf16 = jnp.bfloat16

D_MODEL = 1024
BW = 512
N_BRANCH = 4
EPS = 1e-6
S5_GROUPS, S5_P, S5_N = 32, 16, 64
S5_NS = S5_GROUPS * S5_N
CHUNK = 128
M2_HEADS, M2_HEAD_DIM, M2_GROUPS, M2_STATE = 8, 64, 2, 128
IN_DIM = 10248
PW = 10368
PW_MAIN = 10240
LANES = 128
VMEM_LIMIT = 60 * 1024 * 1024

ADAM_LR, ADAM_B1, ADAM_B2, ADAM_EPS, ADAM_WD, ADAM_STEP = 0.001, 0.9, 0.999, 1e-08, 0.01, 10

C_MERGE = 0
C_SC = 4096
C_SGU = 6144
C_S5G = 7680
C_S5U = 8192
C_M2Z = 8704
C_XBC = 9216
C_DT = 10240


def _col_segments():
    segs = [(6152, 4096)]
    for j in range(4):
        segs += [(4104 + 128 * j, 128), (4616 + 128 * j, 128), (5128 + 128 * j, 128), (5640 + 128 * j, 128)]
    segs += [(1024, 1536), (512, 512), (0, 512), (2560, 512), (3072, 1024), (4096, 8)]
    return segs


def _to_kernel_cols(w):
    parts = [w[:, s:s + n] for s, n in _col_segments()]
    parts.append(jnp.zeros((w.shape[0], PW - IN_DIM), w.dtype))
    return jnp.concatenate(parts, axis=1)


def _from_kernel_cols(wp):
    out, pos = {}, 0
    for s, n in _col_segments():
        out[s] = wp[:, pos:pos + n]
        pos += n
    return jnp.concatenate([out[s] for s in sorted(out)], axis=1)


NN = ((1,), (0,))
NT = ((1,), (1,))
TN = ((0,), (0,))


def _bd(a, b, dims):
    return lax.dot_general(a.astype(bf16), b.astype(bf16), (dims, ((), ())), preferred_element_type=f32)


def _hd(a, b, dims):
    return lax.dot_general(a, b, (dims, ((), ())), precision=lax.Precision.HIGHEST, preferred_element_type=f32)


def _make_dots(raw):
    @jax.custom_vjp
    def nn(a, b):
        return raw(a, b, NN)
    nn.defvjp(lambda a, b: (raw(a, b, NN), (a, b)), lambda r, g: (raw(g, r[1], NT), raw(r[0], g, TN)))

    @jax.custom_vjp
    def nt(a, b):
        return raw(a, b, NT)
    nt.defvjp(lambda a, b: (raw(a, b, NT), (a, b)), lambda r, g: (raw(g, r[1], NN), raw(g, r[0], TN)))

    @jax.custom_vjp
    def tn(a, b):
        return raw(a, b, TN)
    tn.defvjp(lambda a, b: (raw(a, b, TN), (a, b)), lambda r, g: (raw(r[1], g, NT), raw(r[0], g, NN)))
    return nn, nt, tn


bdot, bdot_nt, bdot_tn = _make_dots(_bd)
hdot, hdot_nt, hdot_tn = _make_dots(_hd)


@jax.custom_vjp
def bdot_w(a, w, shadow):
    return _bd(a, w, NN)


bdot_w.defvjp(lambda a, w, s: (_bd(a, w, NN), (a, w)),
              lambda r, g: (_bd(g, r[1], NT), jnp.zeros_like(r[1]), _bd(r[0], g, TN)))


def _rows(shape):
    return lax.broadcasted_iota(jnp.int32, shape, 0)


def _cols(shape):
    return lax.broadcasted_iota(jnp.int32, shape, 1)


def _shift_down(x, s):
    return jnp.where(_rows(x.shape) < s, 0.0, pltpu.roll(x, s, 0))


def _shift_up(x, s):
    n = x.shape[0]
    return jnp.where(_rows(x.shape) >= n - s, 0.0, pltpu.roll(x, n - s, 0))


@functools.partial(jax.custom_vjp, nondiff_argnums=(1,))
def shift(x, s):
    return _shift_down(x, s) if s else x


shift.defvjp(lambda x, s: (shift(x, s), None), lambda s, _, g: (_shift_up(g, s) if s else g,))


def _row_of(w, k):
    return jnp.sum(jnp.where(_rows(w.shape) == k, w, 0.0), axis=0, keepdims=True)


def _lane_mask(width, lo, hi):
    c = _cols((1, width))
    return ((c >= lo) & (c < hi)).astype(f32)


def _expand(rows, width, per):
    return (_cols((rows, width)) // per == _rows((rows, width))).astype(f32)


class A(NamedTuple):
    arr: Any
    block: tuple
    imap: Callable
    shadow: bool = False


class O(NamedTuple):
    shape: tuple
    dtype: Any
    block: tuple
    imap: Callable
    alias: Any = None


class R(NamedTuple):
    arg: int
    out: int
    off: Any = None
    acc: bool = False


def _cparams(n_grid):
    return pltpu.CompilerParams(dimension_semantics=("arbitrary",) * n_grid, vmem_limit_bytes=VMEM_LIMIT)


def _ispec(block, imap, n, rev):
    if rev:
        return pl.BlockSpec(block, lambda i: imap(n - 1 - i))
    return pl.BlockSpec(block, imap)


def _load(ref, a):
    v = ref[...]
    if a.shadow:
        return (v, jnp.zeros(v.shape, f32))
    return v.astype(f32)


def _save_spec(shape, n, rev):
    nd = len(shape)
    return _ispec((None,) + tuple(shape), lambda i: (i,) + (0,) * nd, n, rev)


def block_fwd(name, f, n, args, outs, carries=()):
    n_in, n_out, n_c = len(args), len(outs), len(carries)

    def body(*refs):
        ins, out_r = refs[:n_in], refs[n_in:n_in + n_out]
        saves, cs = refs[n_in + n_out:n_in + n_out + n_c], refs[n_in + n_out + n_c:]
        if n_c:
            @pl.when(pl.program_id(0) == 0)
            def _():
                for c in cs:
                    c[...] = jnp.zeros(c.shape, f32)
        vals = [_load(r, a) for r, a in zip(ins, args)]
        cv = [c[...] for c in cs]
        for s, v in zip(saves, cv):
            s[...] = v
        res = f(*vals, *cv)
        for r, v in zip(out_r, res[:n_out]):
            r[...] = v.astype(r.dtype)
        for c, v in zip(cs, res[n_out:]):
            c[...] = v

    out_shape = [jax.ShapeDtypeStruct(o.shape, o.dtype) for o in outs]
    out_specs = [pl.BlockSpec(o.block, o.imap) for o in outs]
    for shp in carries:
        out_shape.append(jax.ShapeDtypeStruct((n,) + tuple(shp), f32))
        out_specs.append(_save_spec(shp, n, False))
    return pl.pallas_call(
        body, name=name, grid=(n,),
        in_specs=[pl.BlockSpec(a.block, a.imap) for a in args],
        out_specs=out_specs, out_shape=out_shape,
        scratch_shapes=[pltpu.VMEM(tuple(shp), f32) for shp in carries],
        compiler_params=_cparams(1),
    )(*[a.arr for a in args])


def block_bwd(name, f, n, args, cots, gouts, routes, saved=(), rev=False):
    n_in, n_cot, n_c, n_go = len(args), len(cots), len(saved), len(gouts)
    diff = []
    for r in routes:
        if r.arg not in diff:
            diff.append(r.arg)
    aliases = [(k, o.alias) for k, o in enumerate(gouts) if o.alias is not None]

    def body(*refs):
        ins = refs[:n_in]
        cot_r = refs[n_in:n_in + n_cot]
        sav_r = refs[n_in + n_cot:n_in + n_cot + n_c]
        base = n_in + n_cot + n_c + len(aliases)
        go_r = refs[base:base + n_go]
        dcs = refs[base + n_go:]
        step = pl.program_id(0)
        if n_c:
            @pl.when(step == 0)
            def _():
                for d in dcs:
                    d[...] = jnp.zeros(d.shape, f32)
        vals = [_load(r, a) for r, a in zip(ins, args)]
        cv = [s[...] for s in sav_r]
        nd = len(diff)

        def g(*dv):
            full = list(vals)
            for idx, v in zip(diff, dv[:nd]):
                full[idx] = (vals[idx][0], v) if args[idx].shadow else v
            return tuple(f(*full, *dv[nd:]))

        primals = [vals[i][1] if args[i].shadow else vals[i] for i in diff] + cv
        _, vjp_fn = jax.vjp(g, *primals)
        ct = tuple([r[...].astype(f32) for r in cot_r] + [d[...] for d in dcs])
        grads = vjp_fn(ct)
        for r in routes:
            gr = grads[diff.index(r.arg)]
            ref = go_r[r.out]
            if r.acc:
                @pl.when(step == 0)
                def _(ref=ref, gr=gr):
                    ref[...] = gr.astype(ref.dtype)

                @pl.when(step > 0)
                def _(ref=ref, gr=gr):
                    ref[...] += gr.astype(ref.dtype)
            elif r.off is None:
                ref[...] = gr.astype(ref.dtype)
            else:
                ref[:, r.off:r.off + gr.shape[1]] = gr.astype(ref.dtype)
        for d, gr in zip(dcs, grads[nd:]):
            d[...] = gr

    in_specs = [_ispec(a.block, a.imap, n, rev) for a in list(args) + list(cots)]
    in_specs += [_save_spec(s.shape[1:], n, rev) for s in saved]
    in_specs += [pl.BlockSpec(memory_space=pl.ANY) for _ in aliases]
    operands = [a.arr for a in list(args) + list(cots)] + list(saved) + [arr for _, arr in aliases]
    io_alias = {n_in + n_cot + n_c + j: k for j, (k, _) in enumerate(aliases)}
    return pl.pallas_call(
        body, name=name, grid=(n,),
        in_specs=in_specs,
        out_specs=[_ispec(o.block, o.imap, n, rev) for o in gouts],
        out_shape=[jax.ShapeDtypeStruct(o.shape, o.dtype) for o in gouts],
        scratch_shapes=[pltpu.VMEM(tuple(s.shape[1:]), f32) for s in saved],
        input_output_aliases=io_alias,
        compiler_params=_cparams(1),
    )(*operands)


class Op(NamedTuple):
    arr: Any
    row: int = 0
    col: int = 0


def mm(name, mode, pairs, m, n, tm, tn, out_dtype=f32, add=None, out=None, out_col=0, out_width=None):
    tm, tn = min(tm, m), min(tn, n)
    assert m % tm == 0 and n % tn == 0
    in_specs, operands = [], []
    for a, b, k, _ in pairs:
        if mode == TN:
            assert a.row % k == 0 and a.col % tm == 0
            in_specs.append(pl.BlockSpec((k, tm), lambda j, i, a=a, k=k: (a.row // k, i + a.col // tm)))
        else:
            assert a.col % k == 0 and a.row % tm == 0
            in_specs.append(pl.BlockSpec((tm, k), lambda j, i, a=a, k=k: (i + a.row // tm, a.col // k)))
        if mode == NT:
            assert b.col % k == 0 and b.row % tn == 0
            in_specs.append(pl.BlockSpec((tn, k), lambda j, i, b=b, k=k: (j + b.row // tn, b.col // k)))
        else:
            assert b.row % k == 0 and b.col % tn == 0
            in_specs.append(pl.BlockSpec((k, tn), lambda j, i, b=b, k=k: (b.row // k, j + b.col // tn)))
        operands += [a.arr, b.arr]
    n_p = len(pairs)
    if add is not None:
        assert add.col % tn == 0
        in_specs.append(pl.BlockSpec((tm, tn), lambda j, i: (i, j + add.col // tn)))
        operands.append(add.arr)
    io_alias = {}
    if out is not None:
        assert out_col % tn == 0
        in_specs.append(pl.BlockSpec(memory_space=pl.ANY))
        operands.append(out)
        io_alias = {len(operands) - 1: 0}
        out_shape = jax.ShapeDtypeStruct(out.shape, out.dtype)
    else:
        out_shape = jax.ShapeDtypeStruct((m, out_width or n), out_dtype)
    signs = [p[3] for p in pairs]

    def body(*refs):
        o = refs[-1]
        acc = None
        for p in range(n_p):
            t = _bd(refs[2 * p][...], refs[2 * p + 1][...], mode)
            t = t if signs[p] > 0 else -t
            acc = t if acc is None else acc + t
        if add is not None:
            acc = acc + refs[2 * n_p][...].astype(f32)
        o[...] = acc.astype(o.dtype)

    return pl.pallas_call(
        body, name=name, grid=(n // tn, m // tm),
        in_specs=in_specs,
        out_specs=pl.BlockSpec((tm, tn), lambda j, i: (i, j + out_col // tn)),
        out_shape=out_shape, input_output_aliases=io_alias,
        compiler_params=_cparams(2),
    )(*operands)


SCAN_LANES = 512
SUBL = 8


def _cmul(p, q):
    return (p[0] * q[0] - p[1] * q[1], p[0] * q[1] + p[1] * q[0])


def _powers(a):
    a2 = _cmul(a, a)
    a4 = _cmul(a2, a2)
    a6 = _cmul(a4, a2)
    return [a, a2, _cmul(a2, a), a4, _cmul(a4, a), a6, _cmul(a6, a), _cmul(a4, a4)]


def _table(pw, order, w):
    row = _rows((SUBL, w))
    re = sum(jnp.where(row == t, pw[k][0], 0.0) for t, k in enumerate(order))
    im = sum(jnp.where(row == t, pw[k][1], 0.0) for t, k in enumerate(order))
    return re, im


def _pick_row(x, t):
    return jnp.sum(jnp.where(_rows(x.shape) == t, x, 0.0), axis=0, keepdims=True)


def s5_scan_fwd(bu_re, bu_im, a_re, a_im):
    seq, ns = bu_re.shape
    w, nb = SCAN_LANES, ns // SCAN_LANES

    def body(b_re, b_im, ar, ai, s_re, s_im):
        a = (ar[...], ai[...])
        pw = _powers(a)
        tab = _table(pw, list(range(SUBL)), w)
        row = _rows((SUBL, w))

        def step(i, carry):
            t0 = pl.multiple_of(i * SUBL, SUBL)
            x = (b_re[pl.ds(t0, SUBL), :], b_im[pl.ds(t0, SUBL), :])
            for d, k in ((1, 0), (2, 1), (4, 3)):
                sh = (jnp.where(row < d, 0.0, pltpu.roll(x[0], d, 0)), jnp.where(row < d, 0.0, pltpu.roll(x[1], d, 0)))
                t = _cmul(pw[k], sh)
                x = (x[0] + t[0], x[1] + t[1])
            t = _cmul(tab, carry)
            x = (x[0] + t[0], x[1] + t[1])
            s_re[pl.ds(t0, SUBL), :] = x[0]
            s_im[pl.ds(t0, SUBL), :] = x[1]
            return (x[0][SUBL - 1:, :], x[1][SUBL - 1:, :])

        z = jnp.zeros((1, w), f32)
        lax.fori_loop(0, seq // SUBL, step, (z, z), unroll=2)

    strip = pl.BlockSpec((seq, w), lambda j: (0, j))
    lane = pl.BlockSpec((1, w), lambda j: (0, j))
    return pl.pallas_call(
        body, name="s5_scan_fwd", grid=(nb,),
        in_specs=[strip, strip, lane, lane],
        out_specs=[strip, strip],
        out_shape=[jax.ShapeDtypeStruct((seq, ns), f32)] * 2,
        compiler_params=_cparams(1),
    )(bu_re, bu_im, a_re, a_im)


def s5_scan_bwd(ds_re, ds_im, s_re, s_im, a_re, a_im):
    seq, ns = ds_re.shape
    w, nb = SCAN_LANES, ns // SCAN_LANES
    nblk = seq // SUBL

    def body(g_re, g_im, sr, si, ar, ai, l_re, l_im, da_re, da_im):
        a = (ar[...], -ai[...])
        pw = _powers(a)
        tab = _table(pw, [SUBL - 1 - t for t in range(SUBL)], w)
        row = _rows((SUBL, w))

        def step(kk, carry):
            c_re, c_im, acc_re, acc_im = carry
            i = nblk - 1 - kk
            t0 = pl.multiple_of(i * SUBL, SUBL)
            x = (g_re[pl.ds(t0, SUBL), :], g_im[pl.ds(t0, SUBL), :])
            for d, k in ((1, 0), (2, 1), (4, 3)):
                sh = (jnp.where(row >= SUBL - d, 0.0, pltpu.roll(x[0], SUBL - d, 0)),
                      jnp.where(row >= SUBL - d, 0.0, pltpu.roll(x[1], SUBL - d, 0)))
                t = _cmul(pw[k], sh)
                x = (x[0] + t[0], x[1] + t[1])
            t = _cmul(tab, (c_re, c_im))
            x = (x[0] + t[0], x[1] + t[1])
            l_re[pl.ds(t0, SUBL), :] = x[0]
            l_im[pl.ds(t0, SUBL), :] = x[1]
            tp = pl.multiple_of(jnp.maximum(i - 1, 0) * SUBL, SUBL)
            live = (i > 0).astype(f32)
            p_re = _pick_row(sr[pl.ds(tp, SUBL), :], SUBL - 1) * live
            p_im = _pick_row(si[pl.ds(tp, SUBL), :], SUBL - 1) * live
            sp_re = jnp.where(row == 0, p_re, pltpu.roll(sr[pl.ds(t0, SUBL), :], 1, 0))
            sp_im = jnp.where(row == 0, p_im, pltpu.roll(si[pl.ds(t0, SUBL), :], 1, 0))
            acc_re = acc_re + x[0] * sp_re + x[1] * sp_im
            acc_im = acc_im + x[1] * sp_re - x[0] * sp_im
            return (x[0][:1, :], x[1][:1, :], acc_re, acc_im)

        z1 = jnp.zeros((1, w), f32)
        z8 = jnp.zeros((SUBL, w), f32)
        _, _, acc_re, acc_im = lax.fori_loop(0, nblk, step, (z1, z1, z8, z8), unroll=2)
        da_re[...] = jnp.sum(acc_re, axis=0, keepdims=True)
        da_im[...] = jnp.sum(acc_im, axis=0, keepdims=True)

    strip = pl.BlockSpec((seq, w), lambda j: (0, j))
    lane = pl.BlockSpec((1, w), lambda j: (0, j))
    return pl.pallas_call(
        body, name="s5_scan_bwd", grid=(nb,),
        in_specs=[strip, strip, strip, strip, lane, lane],
        out_specs=[strip, strip, lane, lane],
        out_shape=[jax.ShapeDtypeStruct((seq, ns), f32)] * 2 + [jax.ShapeDtypeStruct((1, ns), f32)] * 2,
        compiler_params=_cparams(1),
    )(ds_re, ds_im, s_re, s_im, a_re, a_im)


def _rms(x, w):
    return x * lax.rsqrt(jnp.mean(x * x, axis=-1, keepdims=True) + EPS) * w


def f_rms(x, w):
    return (_rms(x, w),)


def f_rms_res(x, w):
    return (_rms(x, w), x)


def f_s5_prep(lam_re, lam_im, log_step, b_re, b_im):
    e = _expand(log_step.shape[1], S5_NS, S5_N)
    step = hdot(jnp.exp(log_step), e)
    mag = jnp.exp(lam_re * step)
    ab_re, ab_im = mag * jnp.cos(lam_im * step), mag * jnp.sin(lam_im * step)
    den = lam_re * lam_re + lam_im * lam_im
    nr = ab_re - 1.0
    coef_re = (nr * lam_re + ab_im * lam_im) / den
    coef_im = (ab_im * lam_re - nr * lam_im) / den
    bb_re, bb_im = coef_re * b_re - coef_im * b_im, coef_re * b_im + coef_im * b_re
    sel = (_rows((BW, S5_P)) % S5_P == _cols((BW, S5_P))).astype(f32)
    blk = _rows((BW, S5_NS)) // S5_P == _cols((BW, S5_NS)) // S5_N
    rows_bd = lambda t: jnp.where(blk, hdot(sel, t), 0.0)
    return (ab_re, ab_im, rows_bd(bb_re), rows_bd(bb_im))


def f_s5_c(c_re, c_im):
    sel_t = (_cols((S5_P, BW)) % S5_P == _rows((S5_P, BW))).astype(f32)
    blk_t = _rows((S5_NS, BW)) // S5_N == _cols((S5_NS, BW)) // S5_P
    cols_bd = lambda t: jnp.where(blk_t, hdot_tn(t, sel_t), 0.0)
    return (cols_bd(c_re), cols_bd(c_im))


def f_s5_act(y_lin, u, gate, d, w_glu):
    y = jax.nn.gelu(y_lin + d * u)
    y = y * jax.nn.sigmoid(bdot_w(y, *w_glu))
    return (y * jax.nn.silu(gate),)


def f_sgu(u, v, gate, ln_w, ln_b, *rest):
    w_s, b_pad = rest[:8], rest[8]
    t = u.shape[0]
    u32, v32 = jax.nn.gelu(u), jax.nn.gelu(v)
    mu = jnp.mean(v32, axis=-1, keepdims=True)
    var = jnp.mean(jnp.square(v32 - mu), axis=-1, keepdims=True)
    vn = (v32 - mu) * lax.rsqrt(var + EPS) * ln_w + ln_b
    tri = _rows((t, t)) >= _cols((t, t))
    s = hdot_tn(b_pad, _expand(LANES, BW, BW // 8))
    for h in range(8):
        s = s + bdot(jnp.where(tri, w_s[h], 0.0), vn) * _lane_mask(BW, 64 * h, 64 * h + 64)
    return (u32 * s * jax.nn.silu(gate),)


def f_m2_conv(x, w, b):
    return (sum(_row_of(w, k) * shift(x, 3 - k) for k in range(4)) + b,)


def f_sc(bg, cg, h, gate, w):
    z = cg * h
    conv = sum(_row_of(w, k) * shift(z, 2 - k) for k in range(3))
    return (bg * conv * jax.nn.silu(gate),)


def f_m2(z, xc, b0, b1, c0, c1, dt_raw, dt_bias, a_log, d_par, norm_w, st):
    q = z.shape[0]
    x = jax.nn.silu(xc)
    bm, cm = (jax.nn.silu(b0), jax.nn.silu(b1)), (jax.nn.silu(c0), jax.nn.silu(c1))
    dt = jax.nn.softplus(dt_raw + dt_bias)
    da = dt * (-jnp.exp(a_log))
    tri = _rows((q, q)) >= _cols((q, q))
    acs = hdot(tri.astype(f32), da)
    e = _expand(LANES, BW, M2_HEAD_DIM)
    dt_f, acs_f = hdot(dt, e), hdot(acs, e)
    last = _rows((q, BW)) == q - 1
    alast_f = jnp.sum(jnp.where(last, acs_f, 0.0), axis=0, keepdims=True)
    xdt = x * dt_f
    xdec = xdt * jnp.exp(alast_f - acs_f)
    acs_t = acs.T
    st_new = st * jnp.exp(alast_f)
    y_diag, y_off = 0.0, 0.0
    for g in range(M2_GROUPS):
        gm = _lane_mask(BW, 256 * g, 256 * g + 256)
        cb = bdot_nt(cm[g], bm[g])
        st_new = st_new + bdot_tn(bm[g], xdec * gm)
        y_off = y_off + bdot(cm[g], st) * gm
        for hh in range(M2_HEADS // M2_GROUPS):
            h = g * (M2_HEADS // M2_GROUPS) + hh
            col = jnp.sum(jnp.where(_cols((q, LANES)) == h, acs, 0.0), axis=1, keepdims=True)
            row = jnp.sum(jnp.where(_rows((LANES, q)) == h, acs_t, 0.0), axis=0, keepdims=True)
            decay = jnp.exp(jnp.where(tri, col - row, -1e30))
            y_diag = y_diag + bdot(cb * decay, xdt) * _lane_mask(BW, 64 * h, 64 * h + 64)
    d_f = sum(jnp.sum(jnp.where(_cols((1, LANES)) == h, d_par, 0.0), axis=1, keepdims=True)
              * _lane_mask(BW, 64 * h, 64 * h + 64) for h in range(M2_HEADS))
    y = y_diag + y_off * jnp.exp(acs_f) + d_f * x
    y = y * jax.nn.silu(z)
    return (_rms(y, norm_w), st_new)


def f_gate_mix(*v):
    bo, lg, mb = v[0:4], v[4:8], v[8:12]
    return (sum(jax.nn.sigmoid(lg[k] + mb[k]) * bo[k] for k in range(N_BRANCH)),)


def _full(shape):
    nd = len(shape)
    return dict(block=tuple(shape), imap=lambda i: (0,) * nd)


def _param(arr, shadow=False):
    return A(arr, tuple(arr.shape), lambda i, nd=arr.ndim: (0,) * nd, shadow)


def _tb(arr, t, width, colblk):
    return A(arr, (t, width), lambda i: (i, colblk))


def _strip(arr, seq, colblk0, stride=1):
    return A(arr, (seq, LANES), lambda j: (0, colblk0 + stride * j))


def _s5_prep_args(p):
    lam_re = p["s5_lambda_re"].reshape(1, S5_NS)
    lam_im = p["s5_lambda_im"].reshape(1, S5_NS)
    log_step = jnp.pad(p["s5_log_step"].reshape(1, S5_GROUPS), ((0, 0), (0, LANES - S5_GROUPS)))
    b_lanes = lambda b: jnp.transpose(b, (2, 0, 1)).reshape(S5_P, S5_NS)
    return [_param(v) for v in (lam_re, lam_im, log_step, b_lanes(p["s5_b_re"]), b_lanes(p["s5_b_im"]))]


def _s5_c_args(p):
    c_lanes = lambda c: jnp.transpose(c, (1, 0, 2)).reshape(S5_P, S5_NS)
    return [_param(c_lanes(p["s5_c_re"])), _param(c_lanes(p["s5_c_im"]))]


def layer_fwd(x, p):
    seq = x.shape[0]
    nt = seq // CHUNK
    t2 = 256
    sv = {}
    hb = block_fwd("rms_in", f_rms, seq // t2, [_tb(x, t2, D_MODEL, 0), _param(p["norm_w"].reshape(1, D_MODEL))],
                   [O((seq, D_MODEL), bf16, (t2, D_MODEL), lambda i: (i, 0))])[0]
    proj = mm("mm_in", NN, [(Op(hb), Op(p["w_in"]), D_MODEL, 1)], seq, PW, 512, 1152)
    whole = lambda shape, dt: O(shape, dt, shape, lambda i: (0, 0))
    ab_re, ab_im, bb_re, bb_im = block_fwd("s5_prep", f_s5_prep, 1, _s5_prep_args(p),
                                           [whole((1, S5_NS), f32)] * 2 + [whole((BW, S5_NS), bf16)] * 2)
    cc_re, cc_im = block_fwd("s5_c", f_s5_c, 1, _s5_c_args(p), [whole((S5_NS, BW), bf16)] * 2)
    bu_re = mm("mm_bu_re", NN, [(Op(proj, 0, C_S5U), Op(bb_re), BW, 1)], seq, S5_NS, 512, 1024)
    bu_im = mm("mm_bu_im", NN, [(Op(proj, 0, C_S5U), Op(bb_im), BW, 1)], seq, S5_NS, 512, 1024)
    s_re, s_im = s5_scan_fwd(bu_re, bu_im, ab_re, ab_im)
    y_lin = mm("mm_s5y", NN, [(Op(s_re), Op(cc_re), S5_NS, 1), (Op(s_im), Op(cc_im), S5_NS, -1)], seq, BW, 256, BW)
    s5_act_args = [_tb(y_lin, CHUNK, BW, 0), _tb(proj, CHUNK, BW, C_S5U // BW), _tb(proj, CHUNK, BW, C_S5G // BW),
                   _param(p["s5_d"].reshape(1, BW)), _param(p["s5_w_glu"], True)]
    out_bw = O((seq, BW), f32, (CHUNK, BW), lambda i: (i, 0))
    y_a = block_fwd("s5_act", f_s5_act, nt, s5_act_args, [out_bw])[0]
    y_b = block_fwd("sgu", f_sgu, nt, _sgu_args(proj, p), [out_bw])[0]
    xc = block_fwd("m2_conv", f_m2_conv, 8, _m2_conv_args(proj, p, seq),
                   [O((seq, 2 * BW), f32, (seq, LANES), lambda j: (0, j))])[0]
    y_c, st_saved = block_fwd("m2_ssd", f_m2, nt, _m2_args(proj, xc, p), [out_bw], carries=[(M2_STATE, BW)])
    y_d = block_fwd("sc", f_sc, 4, _sc_args(proj, p, seq), [O((seq, BW), f32, (seq, LANES), lambda j: (0, j))])[0]
    ys = [y_a, y_b, y_c, y_d]
    bo = [mm(f"mm_branch{k}", NN, [(Op(ys[k]), Op(p["w_branch"], k * BW, 0), BW, 1)], seq, D_MODEL, 512, D_MODEL)
          for k in range(N_BRANCH)]
    merged = block_fwd("gate_mix", f_gate_mix, seq // t2, _mix_args(bo, proj, p, t2),
                       [O((seq, D_MODEL), bf16, (t2, D_MODEL), lambda i: (i, 0))])[0]
    x_new = mm("mm_out", NN, [(Op(merged), Op(p["w_out"]), D_MODEL, 1)], seq, D_MODEL, 512, D_MODEL, add=Op(x))
    sv.update(x=x, hb=hb, proj=proj, ab=(ab_re, ab_im), bb=(bb_re, bb_im), cc=(cc_re, cc_im), s=(s_re, s_im), y_lin=y_lin,
              xc=xc, st=st_saved, ys=ys, bo=bo, merged=merged)
    return x_new, sv


def _sgu_args(proj, p):
    c0 = C_SGU // BW
    args = [_tb(proj, CHUNK, BW, c0), _tb(proj, CHUNK, BW, c0 + 1), _tb(proj, CHUNK, BW, c0 + 2),
            _param(p["sgu_ln_w"].reshape(1, BW)), _param(p["sgu_ln_b"].reshape(1, BW))]
    args += [A(p["sgu_w"], (None, CHUNK, CHUNK), lambda i, h=h: (h, 0, 0)) for h in range(8)]
    args.append(_param(jnp.pad(p["sgu_b"], ((0, LANES - 8), (0, 0)))))
    return args


def _m2_conv_args(proj, p, seq):
    return [_strip(proj, seq, C_XBC // LANES), A(p["m2_conv_w"], (4, LANES), lambda j: (0, j)),
            A(p["m2_conv_b"].reshape(1, 2 * BW), (1, LANES), lambda j: (0, j))]


def _pad_lanes(v):
    return jnp.pad(v.reshape(1, -1), ((0, 0), (0, LANES - v.size)))


def _m2_args(proj, xc, p):
    args = [_tb(proj, CHUNK, BW, C_M2Z // BW), _tb(xc, CHUNK, BW, 0)]
    args += [_tb(xc, CHUNK, LANES, 4 + k) for k in range(4)]
    args.append(_tb(proj, CHUNK, LANES, C_DT // LANES))
    args += [_param(_pad_lanes(p["m2_dt_bias"])), _param(_pad_lanes(p["m2_a_log"])), _param(_pad_lanes(p["m2_d"])),
             _param(p["m2_norm_w"].reshape(1, BW))]
    return args


def _sc_args(proj, p, seq):
    c0 = C_SC // LANES
    return [_strip(proj, seq, c0 + k, 4) for k in range(4)] + [A(p["sc_conv_w"], (3, LANES), lambda j: (0, j))]


def _mix_args(bo, proj, p, t):
    args = [_tb(bo[k], t, D_MODEL, 0) for k in range(N_BRANCH)]
    args += [_tb(proj, t, D_MODEL, k) for k in range(N_BRANCH)]
    mb = p["merge_b"].reshape(N_BRANCH, 1, D_MODEL)
    args += [A(mb, (None, 1, D_MODEL), lambda i, k=k: (k, 0, 0)) for k in range(N_BRANCH)]
    return args


def layer_bwd(d_out, p, sv):
    seq = d_out.shape[0]
    nt = seq // CHUNK
    t2 = 256
    proj, ys, bo = sv["proj"], sv["ys"], sv["bo"]
    g = {}
    acc = lambda shape: O(tuple(shape), f32, tuple(shape), lambda i, nd=len(shape): (0,) * nd)
    d_merged = mm("mm_out_dx", NT, [(Op(d_out), Op(p["w_out"]), D_MODEL, 1)], seq, D_MODEL, 512, D_MODEL, out_dtype=f32)
    g["w_out"] = mm("mm_out_dw", TN, [(Op(sv["merged"]), Op(d_out), seq, 1)], D_MODEL, D_MODEL, 256, D_MODEL, out_dtype=bf16)
    gouts = [O((seq, D_MODEL), bf16, (t2, D_MODEL), lambda i: (i, 0)) for _ in range(N_BRANCH)]
    gouts.append(O((seq, PW_MAIN), bf16, (t2, N_BRANCH * D_MODEL), lambda i: (i, 0)))
    gouts += [acc((1, D_MODEL)) for _ in range(N_BRANCH)]
    routes = [R(k, k) for k in range(N_BRANCH)] + [R(4 + k, 4, k * D_MODEL) for k in range(N_BRANCH)]
    routes += [R(8 + k, 5 + k, acc=True) for k in range(N_BRANCH)]
    res = block_bwd("gate_mix_bwd", f_gate_mix, seq // t2, _mix_args(bo, proj, p, t2),
                    [_tb(d_merged, t2, D_MODEL, 0)], gouts, routes)
    dbo, dproj = res[:4], res[4]
    g["merge_b"] = jnp.concatenate(res[5:9], axis=0)
    dys = [mm(f"mm_branch{k}_dx", NT, [(Op(dbo[k]), Op(p["w_branch"], k * BW, 0), D_MODEL, 1)], seq, BW, 512, BW)
           for k in range(N_BRANCH)]
    g["w_branch"] = jnp.stack([mm(f"mm_branch{k}_dw", TN, [(Op(ys[k]), Op(dbo[k]), seq, 1)], BW, D_MODEL, 256, D_MODEL, out_dtype=bf16)
                               for k in range(N_BRANCH)])
    res = block_bwd("sc_bwd", f_sc, 4, _sc_args(proj, p, seq), [_strip(dys[3], seq, 0)],
                    [O((seq, PW_MAIN), bf16, (seq, 4 * LANES), lambda j: (0, C_SC // (4 * LANES) + j), alias=dproj),
                     O((3, BW), f32, (3, LANES), lambda j: (0, j))],
                    [R(k, 0, k * LANES) for k in range(4)] + [R(4, 1)])
    dproj, g["sc_conv_w"] = res
    res = block_bwd("m2_ssd_bwd", f_m2, nt, _m2_args(proj, sv["xc"], p), [_tb(dys[2], CHUNK, BW, 0)],
                    [O((seq, PW_MAIN), bf16, (CHUNK, BW), lambda i: (i, C_M2Z // BW), alias=dproj),
                     O((seq, 2 * BW), f32, (CHUNK, 2 * BW), lambda i: (i, 0)),
                     O((seq, LANES), bf16, (CHUNK, LANES), lambda i: (i, 0)),
                     acc((1, LANES)), acc((1, LANES)), acc((1, LANES)), acc((1, BW))],
                    [R(0, 0), R(1, 1, 0)] + [R(2 + k, 1, BW + k * LANES) for k in range(4)] + [R(6, 2)]
                    + [R(7, 3, acc=True), R(8, 4, acc=True), R(9, 5, acc=True), R(10, 6, acc=True)],
                    saved=[sv["st"]], rev=True)
    dproj, dxc, d_dt = res[0], res[1], res[2]
    g["m2_dt_bias"], g["m2_a_log"], g["m2_d"] = (r[0, :M2_HEADS] for r in res[3:6])
    g["m2_norm_w"] = res[6].reshape(BW)
    res = block_bwd("m2_conv_bwd", f_m2_conv, 8, _m2_conv_args(proj, p, seq), [_strip(dxc, seq, 0)],
                    [O((seq, PW_MAIN), bf16, (seq, LANES), lambda j: (0, C_XBC // LANES + j), alias=dproj),
                     O((4, 2 * BW), f32, (4, LANES), lambda j: (0, j)), O((1, 2 * BW), f32, (1, LANES), lambda j: (0, j))],
                    [R(0, 0), R(1, 1), R(2, 2)])
    dproj, g["m2_conv_w"], cb = res
    g["m2_conv_b"] = cb.reshape(2 * BW)
    res = block_bwd("sgu_bwd", f_sgu, nt, _sgu_args(proj, p), [_tb(dys[1], CHUNK, BW, 0)],
                    [O((seq, PW_MAIN), bf16, (CHUNK, 3 * BW), lambda i: (i, C_SGU // (3 * BW)), alias=dproj),
                     acc((1, BW)), acc((1, BW))] + [acc((CHUNK, CHUNK)) for _ in range(8)] + [acc((LANES, CHUNK))],
                    [R(0, 0, 0), R(1, 0, BW), R(2, 0, 2 * BW), R(3, 1, acc=True), R(4, 2, acc=True)]
                    + [R(5 + h, 3 + h, acc=True) for h in range(8)] + [R(13, 11, acc=True)])
    dproj = res[0]
    g["sgu_ln_w"], g["sgu_ln_b"] = res[1].reshape(BW), res[2].reshape(BW)
    g["sgu_w"] = jnp.stack(res[3:11])
    g["sgu_b"] = res[11][:8]
    y_lin, (s_re, s_im), (ab_re, ab_im) = sv["y_lin"], sv["s"], sv["ab"]
    s5_act_args = [_tb(y_lin, CHUNK, BW, 0), _tb(proj, CHUNK, BW, C_S5U // BW), _tb(proj, CHUNK, BW, C_S5G // BW),
                   _param(p["s5_d"].reshape(1, BW)), _param(p["s5_w_glu"], True)]
    res = block_bwd("s5_act_bwd", f_s5_act, nt, s5_act_args, [_tb(dys[0], CHUNK, BW, 0)],
                    [O((seq, BW), bf16, (CHUNK, BW), lambda i: (i, 0)), O((seq, BW), f32, (CHUNK, BW), lambda i: (i, 0)),
                     O((seq, PW_MAIN), bf16, (CHUNK, BW), lambda i: (i, C_S5G // BW), alias=dproj),
                     acc((1, BW)), acc((BW, BW))],
                    [R(0, 0), R(1, 1), R(2, 2), R(3, 3, acc=True), R(4, 4, acc=True)])
    dy_lin, du1, dproj = res[0], res[1], res[2]
    g["s5_d"] = res[3].reshape(S5_GROUPS, S5_P)
    g["s5_w_glu"] = res[4]
    (bb_re, bb_im), (cc_re, cc_im) = sv["bb"], sv["cc"]
    ds_re = mm("mm_s5y_dre", NT, [(Op(dy_lin), Op(cc_re), BW, 1)], seq, S5_NS, 512, 1024)
    ds_im = mm("mm_s5y_dim", NT, [(Op(dy_lin), Op(cc_im), BW, -1)], seq, S5_NS, 512, 1024)
    dc_re = mm("mm_s5y_dcre", TN, [(Op(s_re), Op(dy_lin), seq, 1)], S5_NS, BW, 512, BW)
    dc_im = mm("mm_s5y_dcim", TN, [(Op(s_im), Op(dy_lin), seq, -1)], S5_NS, BW, 512, BW)
    l_re, l_im, da_re, da_im = s5_scan_bwd(ds_re, ds_im, s_re, s_im, ab_re, ab_im)
    dproj = mm("mm_bu_dx", NT, [(Op(l_re), Op(bb_re), S5_NS, 1), (Op(l_im), Op(bb_im), S5_NS, 1)],
               seq, BW, 512, BW, add=Op(du1), out=dproj, out_col=C_S5U)
    dbb = [mm(f"mm_bu_dw{n}", TN, [(Op(proj, 0, C_S5U), Op(l), seq, 1)], BW, S5_NS, 256, 1024)
           for n, l in (("re", l_re), ("im", l_im))]
    gouts = [acc((1, S5_NS)), acc((1, S5_NS)), acc((1, LANES))] + [acc((S5_P, S5_NS))] * 2
    res = block_bwd("s5_prep_bwd", f_s5_prep, 1, _s5_prep_args(p),
                    [_param(v) for v in (da_re, da_im, dbb[0], dbb[1])], gouts, [R(k, k, acc=True) for k in range(5)])
    g["s5_lambda_re"], g["s5_lambda_im"] = res[0].reshape(S5_GROUPS, S5_N), res[1].reshape(S5_GROUPS, S5_N)
    g["s5_log_step"] = res[2][0, :S5_GROUPS]
    b_natural = lambda b: jnp.transpose(b.reshape(S5_P, S5_GROUPS, S5_N), (1, 2, 0))
    c_natural = lambda c: jnp.transpose(c.reshape(S5_P, S5_GROUPS, S5_N), (1, 0, 2))
    g["s5_b_re"], g["s5_b_im"] = b_natural(res[3]), b_natural(res[4])
    res = block_bwd("s5_c_bwd", f_s5_c, 1, _s5_c_args(p), [_param(dc_re), _param(dc_im)],
                    [acc((S5_P, S5_NS))] * 2, [R(0, 0, acc=True), R(1, 1, acc=True)])
    g["s5_c_re"], g["s5_c_im"] = c_natural(res[0]), c_natural(res[1])
    hb, w_in = sv["hb"], p["w_in"]
    dh = mm("mm_in_dx", NT, [(Op(dproj), Op(w_in), PW_MAIN, 1), (Op(d_dt), Op(w_in, 0, PW_MAIN), LANES, 1)],
            seq, D_MODEL, 256, 256)
    dw = mm("mm_in_dw", TN, [(Op(hb), Op(dproj), seq, 1)], D_MODEL, PW_MAIN, 256, 1024, out_dtype=bf16, out_width=PW)
    g["w_in"] = mm("mm_in_dwdt", TN, [(Op(hb), Op(d_dt), seq, 1)], D_MODEL, LANES, 256, LANES, out=dw, out_col=C_DT)
    dx, dnw = block_bwd("rms_in_bwd", f_rms_res, seq // t2, [_tb(sv["x"], t2, D_MODEL, 0), _param(p["norm_w"].reshape(1, D_MODEL))],
                        [_tb(dh, t2, D_MODEL, 0), _tb(d_out, t2, D_MODEL, 0)],
                        [O((seq, D_MODEL), f32, (t2, D_MODEL), lambda i: (i, 0)), acc((1, D_MODEL))],
                        [R(0, 0), R(1, 1, acc=True)])
    g["norm_w"] = dnw.reshape(D_MODEL)
    return dx, g


def loss_head(x, w, target):
    seq = x.shape[0]
    t = 256

    def body(x_ref, w_ref, t_ref, loss_ref, dx_ref, dw_ref):
        step = pl.program_id(0)

        def f(xv, wv):
            err = _rms(xv, wv) - t_ref[...]
            return 0.5 * jnp.sum(jnp.mean(err * err, axis=-1, keepdims=True), axis=0, keepdims=True)

        val, vjp_fn = jax.vjp(f, x_ref[...], w_ref[...])
        dx, dw = vjp_fn(jnp.ones((1, 1), f32))
        dx_ref[...] = dx

        @pl.when(step == 0)
        def _():
            loss_ref[...] = jnp.broadcast_to(val, loss_ref.shape)
            dw_ref[...] = dw

        @pl.when(step > 0)
        def _():
            loss_ref[...] += jnp.broadcast_to(val, loss_ref.shape)
            dw_ref[...] += dw

    blk = pl.BlockSpec((t, D_MODEL), lambda i: (i, 0))
    row = pl.BlockSpec((1, D_MODEL), lambda i: (0, 0))
    return pl.pallas_call(
        body, name="loss_head", grid=(seq // t,),
        in_specs=[blk, row, blk],
        out_specs=[pl.BlockSpec((1, LANES), lambda i: (0, 0)), blk, row],
        out_shape=[jax.ShapeDtypeStruct((1, LANES), f32), jax.ShapeDtypeStruct((seq, D_MODEL), f32),
                   jax.ShapeDtypeStruct((1, D_MODEL), f32)],
        compiler_params=_cparams(1),
    )(x, w.reshape(1, D_MODEL), target)


LAYER_KEYS = ("norm_w", "w_in", "s5_lambda_re", "s5_lambda_im", "s5_b_re", "s5_b_im", "s5_c_re", "s5_c_im", "s5_d",
              "s5_log_step", "s5_w_glu", "sgu_ln_w", "sgu_ln_b", "sgu_w", "sgu_b", "m2_conv_w", "m2_conv_b",
              "m2_dt_bias", "m2_a_log", "m2_d", "m2_norm_w", "sc_conv_w", "merge_b", "w_branch", "w_out")


def local_step(x, target, layers, final_norm_w):
    saved = []
    for p in layers:
        x, sv = layer_fwd(x, p)
        saved.append(sv)
    loss, dx, dfw = loss_head(x, final_norm_w, target)
    grads = []
    for p, sv in zip(reversed(layers), reversed(saved)):
        dx, g = layer_bwd(dx, p, sv)
        grads.append(g)
    return loss[0, 0], dx, grads[::-1], dfw.reshape(D_MODEL)


MESH = pl.DeviceIdType.MESH
ANY = pl.BlockSpec(memory_space=pl.ANY)


def _me():
    return lax.axis_index("x"), lax.axis_index("y"), lax.axis_index("c")


def _other_chips(x, y):
    return [(1 - x, y), (x, 1 - y), (1 - x, 1 - y)]


def _rcopy(src, dst, send, recv, dev):
    return pltpu.make_async_remote_copy(src_ref=src, dst_ref=dst, send_sem=send, recv_sem=recv,
                                        device_id=dev, device_id_type=MESH)


def _route_cut(rows, dtype):
    tile = 2 * SUBL * (4 // jnp.dtype(dtype).itemsize)
    return rows // 2 if rows % tile == 0 else rows


def _comm_call(name, body, arrs, out_shape, n_remote, aliases=None):
    n = len(arrs)
    return pl.pallas_call(
        body, name=name, in_specs=[ANY] * n, out_specs=[ANY] * len(out_shape), out_shape=out_shape,
        scratch_shapes=[pltpu.SemaphoreType.DMA((n, n_remote)), pltpu.SemaphoreType.DMA((n, n_remote))],
        input_output_aliases=aliases or {},
        compiler_params=pltpu.CompilerParams(has_side_effects=True),
    )(*arrs)


def gather_chips(name, arrs):
    n = len(arrs)
    cut = [_route_cut(a.shape[1], a.dtype) for a in arrs]

    def body(*refs):
        ins, outs = refs[:n], refs[n:2 * n]
        send, recv = refs[2 * n:]
        x, y, c = _me()
        jme, jx, jy, jd = 2 * x + y, 2 * (1 - x) + y, 2 * x + 1 - y, 2 * (1 - x) + 1 - y
        to_x, to_y, sib = (1 - x, y, c), (x, 1 - y, c), (x, y, 1 - c)

        def part(ref, a, hi):
            return ref.at[pl.ds(cut[a], ref.shape[0] - cut[a])] if hi else ref.at[pl.ds(0, cut[a])]

        def cp(a, k, ref, dev):
            return _rcopy(ref, ref, send.at[a, k], recv.at[a, k], dev)

        split = [a for a in range(n) if cut[a] < arrs[a].shape[1]]
        sent = [_rcopy(ins[a].at[c], outs[a].at[jme, c], send.at[a, k], recv.at[a, k], dev)
                for a in range(n) for k, dev in ((0, to_x), (1, to_y))]
        for s in sent:
            s.start()
        for a in range(n):
            blk = outs[a].at[jx, c]
            cp(a, 0, blk, to_x).wait_recv()
            sent += [cp(a, 2, part(blk, a, False), to_y), cp(a, 4, blk, sib)]
            sent[-2].start()
            sent[-1].start()
        for a in range(n):
            blk = outs[a].at[jy, c]
            cp(a, 1, blk, to_y).wait_recv()
            sent.append(cp(a, 5, blk, sib))
            sent[-1].start()
            if a in split:
                sent.append(cp(a, 3, part(blk, a, True), to_x))
                sent[-1].start()
        for a in range(n):
            lo = part(outs[a].at[jd, c], a, False)
            cp(a, 2, lo, to_y).wait_recv()
            sent.append(cp(a, 6, lo, sib))
            sent[-1].start()
        for a in split:
            hi = part(outs[a].at[jd, c], a, True)
            cp(a, 3, hi, to_x).wait_recv()
            sent.append(cp(a, 7, hi, sib))
            sent[-1].start()
        for a in range(n):
            cp(a, 4, outs[a].at[jx, 1 - c], sib).wait_recv()
            cp(a, 5, outs[a].at[jy, 1 - c], sib).wait_recv()
            cp(a, 6, part(outs[a].at[jd, 1 - c], a, False), sib).wait_recv()
        for a in split:
            cp(a, 7, part(outs[a].at[jd, 1 - c], a, True), sib).wait_recv()
        for s in sent:
            s.wait_send()

    out_shape = [jax.ShapeDtypeStruct((4,) + a.shape, a.dtype) for a in arrs]
    got = _comm_call(name, body, arrs, out_shape, 8)
    jme = 2 * lax.axis_index("x") + lax.axis_index("y")
    return [lax.dynamic_update_index_in_dim(g, a, jme, 0) for g, a in zip(got, arrs)]


def swap_halves(name, arrs):
    n = len(arrs)

    def body(*refs):
        ins, outs = refs[:n], refs[n:2 * n]
        send, recv = refs[2 * n:]
        x, y, c = _me()
        remote = [_rcopy(ins[a].at[1 - c], outs[a], send.at[a, 0], recv.at[a, 0], (x, y, 1 - c)) for a in range(n)]
        for cp in remote:
            cp.start()
        for cp in remote:
            cp.wait()

    return _comm_call(name, body, arrs, [jax.ShapeDtypeStruct(a.shape[1:], a.dtype) for a in arrs], 1)


def exchange_chips(name, arrs):
    n = len(arrs)

    def body(*refs):
        ins, outs = refs[:n], refs[n:2 * n]
        send, recv = refs[2 * n:]
        x, y, c = _me()
        remote = [_rcopy(ins[a].at[2 * cx + cy], outs[a].at[k], send.at[a, k], recv.at[a, k], (cx, cy, c))
                  for a in range(n) for k, (cx, cy) in enumerate(_other_chips(x, y))]
        for cp in remote:
            cp.start()
        for cp in remote:
            cp.wait()

    return _comm_call(name, body, arrs, [jax.ShapeDtypeStruct((3,) + a.shape[1:], a.dtype) for a in arrs], 3)


def gather_cores(name, arrs):
    n = len(arrs)

    def body(*refs):
        bufs = refs[n:2 * n]
        send, recv = refs[2 * n:]
        x, y, c = _me()
        remote = [_rcopy(bufs[a].at[c], bufs[a].at[c], send.at[a, 0], recv.at[a, 0], (x, y, 1 - c)) for a in range(n)]
        for cp in remote:
            cp.start()
        for a in range(n):
            _rcopy(bufs[a].at[1 - c], bufs[a].at[1 - c], send.at[a, 0], recv.at[a, 0], (x, y, 1 - c)).wait_recv()
        for cp in remote:
            cp.wait_send()

    return _comm_call(name, body, arrs, [jax.ShapeDtypeStruct(a.shape, a.dtype) for a in arrs], 1,
                      aliases={a: a for a in range(n)})


ROW_BLOCK = 256


def esum(name, terms, rows, width, out_dtype, out_slots=None):
    tr = next((t for t in range(min(rows, ROW_BLOCK), 0, -SUBL) if rows % t == 0 and t % SUBL == 0), rows)
    where =jnp.stack([lax.axis_index("c"), 2 * lax.axis_index("x") + lax.axis_index("y")]).astype(jnp.int32)
    pick = {"c": 0, "j": 1}

    def body(s_ref, *refs):
        acc = refs[0][...].astype(f32)
        for r in refs[1:-1]:
            acc = acc + r[...].astype(f32)
        refs[-1][...] = acc.astype(out_dtype)

    specs = []
    for arr, lead in terms:
        if lead is None:
            specs.append(pl.BlockSpec((tr, width), lambda i, s: (i, 0)))
        elif isinstance(lead, str):
            specs.append(pl.BlockSpec((None, tr, width), lambda i, s, lead=lead: (s[pick[lead]], i, 0)))
        else:
            specs.append(pl.BlockSpec((None, tr, width), lambda i, s, lead=lead: (lead, i, 0)))
    if out_slots is None:
        out_spec = pl.BlockSpec((tr, width), lambda i, s: (i, 0))
        out_shape = jax.ShapeDtypeStruct((rows, width), out_dtype)
    else:
        out_spec = pl.BlockSpec((None, tr, width), lambda i, s: (s[0], i, 0))
        out_shape = jax.ShapeDtypeStruct((out_slots, rows, width), out_dtype)
    return pl.pallas_call(
        body, name=name,
        grid_spec=pltpu.PrefetchScalarGridSpec(num_scalar_prefetch=1, grid=(rows // tr,), in_specs=specs, out_specs=out_spec),
        out_shape=out_shape, compiler_params=_cparams(1),
    )(where, *[t[0] for t in terms])


def reduce_to_shards(parts):
    tags = [str(k) for k in range(len(parts))]
    theirs = swap_halves("rs_swap", parts)
    t1 = []
    for tag, p, th in zip(tags, parts, theirs):
        _, _, h, w = p.shape
        t1.append(esum("rs_add_cores" + tag, [(p.reshape(2, 4 * h, w), "c"), (th.reshape(4 * h, w), None)],
                       4 * h, w, p.dtype).reshape(4, h, w))
    landed = exchange_chips("rs_exchange", t1)
    red = []
    for tag, p, t, got in zip(tags, parts, t1, landed):
        _, _, h, w = p.shape
        red.append(esum("rs_add_chips" + tag, [(t, "j"), (got, 0), (got, 1), (got, 2)], h, w, f32, out_slots=2))
    return gather_cores("rs_gather", red)


def _adamw_step(w_ref, g_ref, m_ref, v_ref, d_ref, nm_ref, nv_ref):
    gv = g_ref[...]
    nm = ADAM_B1 * m_ref[...] + (1.0 - ADAM_B1) * gv
    nv = ADAM_B2 * v_ref[...] + (1.0 - ADAM_B2) * jnp.square(gv)
    m_hat = nm / (1.0 - ADAM_B1 ** ADAM_STEP)
    v_hat = nv / (1.0 - ADAM_B2 ** ADAM_STEP)
    d_ref[...] = -ADAM_LR * (m_hat / (jnp.sqrt(v_hat) + ADAM_EPS) + ADAM_WD * w_ref[...])
    nm_ref[...] = nm
    nv_ref[...] = nv


def adamw(name, w, g, m, v, tr=None):
    rows, rest = w.shape[0], w.shape[1:]
    if tr is None:
        tr = ROW_BLOCK if rows % ROW_BLOCK == 0 else rows
    assert rows % tr == 0

    def body(*refs):
        _adamw_step(*refs)

    spec = pl.BlockSpec((tr,) + rest, lambda i: (i,) + (0,) * len(rest))
    return pl.pallas_call(
        body, name=name, grid=(rows // tr,), in_specs=[spec] * 4, out_specs=[spec] * 3,
        out_shape=[jax.ShapeDtypeStruct(w.shape, f32)] * 3, compiler_params=_cparams(1),
    )(w, g, m, v)


def adamw_many(name, ws, gs, ms, vs):
    n = len(ws)

    def body(*refs):
        ins, outs = refs[:4 * n], refs[4 * n:]
        for k in range(n):
            _adamw_step(ins[k], ins[n + k], ins[2 * n + k], ins[3 * n + k], outs[k], outs[n + k], outs[2 * n + k])

    vmem = pl.BlockSpec(memory_space=pltpu.VMEM)
    res = pl.pallas_call(
        body, name=name, in_specs=[vmem] * (4 * n), out_specs=[vmem] * (3 * n),
        out_shape=[jax.ShapeDtypeStruct(w.shape, f32) for w in ws] * 3,
        compiler_params=pltpu.CompilerParams(vmem_limit_bytes=VMEM_LIMIT),
    )(*ws, *gs, *ms, *vs)
    return res[:n], res[n:2 * n], res[2 * n:]


PACK_W = 1024


def _pack(parts, halves, row_mult):
    flat = jnp.concatenate([p.reshape(-1) for p in parts])
    per = halves * row_mult * PACK_W
    total = -(-flat.size // per) * per
    flat = jnp.pad(flat, (0, total - flat.size))
    return flat.reshape(halves, total // (halves * PACK_W), PACK_W)


def _unpack(flat, shapes):
    out, pos = [], 0
    for s in shapes:
        n = int(np.prod(s))
        out.append(flat[pos:pos + n].reshape(s))
        pos += n
    return out


SHARDED_BIG = ("w_in", "w_branch", "w_out", "s5_w_glu")
SHARDED_SMALL = ("m2_conv_w", "sc_conv_w", "merge_b")
SHARD_AXIS = {"w_in": 2, "w_branch": 3, "w_out": 1, "s5_w_glu": 1, "m2_conv_w": 2, "sc_conv_w": 2, "merge_b": 2}
REPLICATED = ("norm_w", "s5_lambda_re", "s5_lambda_im", "s5_b_re", "s5_b_im", "s5_c_re", "s5_c_im", "s5_d", "s5_log_step",
              "sgu_ln_w", "sgu_ln_b", "sgu_w", "sgu_b", "m2_conv_b", "m2_dt_bias", "m2_a_log", "m2_d", "m2_norm_w")
WEIGHTS = ("norm_w", "w_in", "s5_lambda_re", "s5_lambda_im", "s5_b_re", "s5_b_im", "s5_c_re", "s5_c_im", "s5_d",
           "s5_log_step", "s5_w_glu", "sgu_ln_w", "sgu_ln_b", "sgu_w", "sgu_b", "m2_conv_w", "m2_conv_b", "m2_dt_bias",
           "m2_a_log", "m2_d", "m2_norm_w", "sc_conv_w", "merge_b", "w_branch", "w_out", "final_norm_w")
N_LAYERS = 2


SHARD_W = IN_DIM // 4
SHARD_PAD = -(-SHARD_W // LANES) * LANES


def _kernel_pieces():
    out, pos = [], 0
    for s, n in _col_segments():
        while n:
            take = min(n, SHARD_W - s % SHARD_W)
            out.append((pos, s, take))
            pos, s, n = pos + take, s + take, n - take
    return out


def regroup_cols(name, src, steps, out_shape, out=None):
    n_src, rows, width = src.shape
    k_max = max(len(p) for _, _, p in steps)
    stride = 2 + 5 * k_max
    tab = np.zeros((len(steps), stride), np.int32)
    for o, (t, b, pieces) in enumerate(steps):
        tab[o, :2] = (t, b)
        for p, (s, col, lo, hi) in enumerate(pieces):
            tab[o, 2 + 5 * p:7 + 5 * p] = (s, col // LANES, col % LANES - lo, lo, hi)
    last = (width - 1) // LANES

    def body(tab_ref, *refs):
        o_ref = refs[-1]
        base = pl.program_id(0) * stride
        r, c = _rows((LANES, LANES)), _cols((LANES, LANES))
        lane = _cols((1, LANES))
        acc = jnp.zeros((rows, LANES), f32)
        for p in range(k_max):
            w, shift = tab_ref[base + 3 + 5 * p], tab_ref[base + 4 + 5 * p]
            keep = (c >= tab_ref[base + 5 + 5 * p]) & (c < tab_ref[base + 6 + 5 * p])
            for half in range(2):
                a = refs[2 * p + half][...]
                a = jnp.where((w + half) * LANES + lane < width, a, jnp.zeros((), a.dtype))
                sel = ((r + half * LANES == c + shift) & keep).astype(bf16)
                acc = acc + jnp.dot(a, sel, preferred_element_type=f32)
        o_ref[...] = acc.astype(o_ref.dtype)

    in_specs = []
    for p in range(k_max):
        for half in range(2):
            in_specs.append(pl.BlockSpec(
                (None, rows, LANES),
                lambda o, t, p=p, half=half: (t[o * stride + 2 + 5 * p], 0, jnp.minimum(t[o * stride + 3 + 5 * p] + half, last))))
    operands = [jnp.asarray(tab.reshape(-1))] + [src] * (2 * k_max)
    io_alias = {}
    if out is not None:
        in_specs.append(pl.BlockSpec(memory_space=pl.ANY))
        operands.append(out)
        io_alias = {len(operands) - 1: 0}
    return pl.pallas_call(
        body, name=name,
        grid_spec=pltpu.PrefetchScalarGridSpec(
            num_scalar_prefetch=1, grid=(len(steps),), in_specs=in_specs,
            out_specs=pl.BlockSpec((None, rows, LANES), lambda o, t: (t[o * stride], 0, t[o * stride + 1]))),
        out_shape=jax.ShapeDtypeStruct(out_shape, bf16), input_output_aliases=io_alias,
        compiler_params=_cparams(1),
    )(*operands)


def _steps_to_kernel_cols(layer):
    steps = []
    for o in range(PW // LANES):
        pieces = []
        for pos, s, n in _kernel_pieces():
            lo, hi = max(pos, o * LANES), min(pos + n, (o + 1) * LANES)
            if lo < hi:
                ref_col = s + lo - pos
                pieces.append(((ref_col // SHARD_W) * N_LAYERS + layer, ref_col % SHARD_W, lo - o * LANES, hi - o * LANES))
        steps.append((0, o, pieces))
    return steps


def _steps_to_shards(layer):
    steps = []
    for j in range(4):
        for b in range(-(-SHARD_W // LANES)):
            start, stop = j * SHARD_W + b * LANES, min(j * SHARD_W + (b + 1) * LANES, (j + 1) * SHARD_W)
            pieces = []
            for pos, s, n in _kernel_pieces():
                lo, hi = max(s, start), min(s + n, stop)
                if lo < hi:
                    pieces.append((0, pos + lo - s, lo - start, hi - start))
            steps.append((layer * 4 + j, b, pieces))
    return steps


def _gather_weights(w):
    w_in = jnp.pad(w["w_in"].astype(bf16), ((0, 0), (0, 0), (0, SHARD_PAD - SHARD_W)))
    arrs = [w_in, w["w_branch"].reshape(N_LAYERS, N_BRANCH * BW, -1).astype(bf16),
            w["w_out"].astype(bf16), w["s5_w_glu"].astype(bf16), w["m2_conv_w"], w["sc_conv_w"], w["merge_b"]]
    got = gather_chips("ag_weights", arrs)
    cols = lambda t: jnp.transpose(t, (1, 0, 2)).reshape(t.shape[1], -1)
    layers = []
    for i in range(N_LAYERS):
        p = {k: w[k][i] for k in REPLICATED}
        p["w_in"] = regroup_cols(f"w_in_cols{i}", got[0].reshape(4 * N_LAYERS, D_MODEL, SHARD_PAD), _steps_to_kernel_cols(i),
                                 (1, D_MODEL, PW))[0]
        p["w_branch"] = cols(got[1][:, i])
        p["w_out"] = got[2][:, i].reshape(D_MODEL, D_MODEL)
        p["s5_w_glu"] = got[3][:, i].reshape(BW, BW)
        p["m2_conv_w"], p["sc_conv_w"], p["merge_b"] = cols(got[4][:, i]), cols(got[5][:, i]), cols(got[6][:, i])
        layers.append(p)
    return layers


def _reduce_grads(grads, d_final, w):
    to_chips = lambda t: jnp.transpose(t.reshape(t.shape[0], 4, -1), (1, 0, 2))
    stack = lambda f: jnp.stack([f(g) for g in grads])
    dw_in = None
    for i, g in enumerate(grads):
        dw_in = regroup_cols(f"w_in_shards{i}", g["w_in"][None], _steps_to_shards(i), (4 * N_LAYERS, D_MODEL, SHARD_PAD), out=dw_in)
    parts = [dw_in.reshape(N_LAYERS, 4, D_MODEL, SHARD_PAD),
             stack(lambda g: to_chips(g["w_branch"].reshape(N_BRANCH * BW, D_MODEL)).astype(bf16)),
             stack(lambda g: g["w_out"].reshape(4, D_MODEL // 4, D_MODEL).astype(bf16)),
             stack(lambda g: g["s5_w_glu"].reshape(4, BW // 4, BW).astype(bf16))]
    rep = jnp.concatenate([stack(lambda g: g[k]).reshape(-1) for k in REPLICATED] + [d_final.reshape(-1)])
    quarter = -(-rep.size // (4 * 2 * SUBL * PACK_W)) * (2 * SUBL * PACK_W)
    rep = jnp.pad(rep, (0, 4 * quarter - rep.size))
    small = []
    for j in range(4):
        sharded = [stack(lambda g: to_chips(g[k])[j]) for k in SHARDED_SMALL]
        small.append(_pack(sharded + [rep[j * quarter:(j + 1) * quarter]], 2, SUBL))
    parts.append(jnp.stack(small, axis=1))
    red = reduce_to_shards(parts)
    out = {"w_in": red[0][:, :, :SHARD_W], "w_branch": red[1].reshape(w["w_branch"].shape), "w_out": red[2], "s5_w_glu": red[3]}
    small_flat = red[4].reshape(-1)
    n_small = sum(int(np.prod(w[k].shape)) for k in SHARDED_SMALL)
    out.update(zip(SHARDED_SMALL, _unpack(small_flat, [w[k].shape for k in SHARDED_SMALL])))
    mine = small_flat[n_small:n_small + quarter].reshape(2, quarter // (2 * PACK_W), PACK_W)
    rep_all = gather_chips("ag_small_grads", [mine])[0].reshape(-1)
    names = REPLICATED + ("final_norm_w",)
    out.update(zip(names, _unpack(rep_all, [w[k].shape for k in names])))
    return out


def _update(w, g, m, v):
    d, nm, nv = {}, {}, {}
    flat2 = lambda a: a.reshape(-1, a.shape[-1])
    cols_major = lambda a: jnp.transpose(a, (2, 0, 1))
    res = adamw("adamw_w_in", *[cols_major(t["w_in"]) for t in (w, g, m, v)], tr=SHARD_W // 14)
    d["w_in"], nm["w_in"], nv["w_in"] = (jnp.transpose(r, (1, 2, 0)) for r in res)
    for k in SHARDED_BIG[1:]:
        res = adamw("adamw_" + k, *[flat2(t[k]) for t in (w, g, m, v)])
        d[k], nm[k], nv[k] = (r.reshape(w[k].shape) for r in res)
    for k in ("s5_b_re", "s5_b_im"):
        res = adamw("adamw_" + k, *[flat2(t[k]) for t in (w, g, m, v)])
        d[k], nm[k], nv[k] = (r.reshape(w[k].shape) for r in res)
    rest = [k for k in WEIGHTS if k not in SHARDED_BIG + ("s5_b_re", "s5_b_im")]
    two_d = lambda a: a.reshape(1, -1) if a.ndim == 1 else a
    res = adamw_many("adamw_rest", *[[two_d(t[k]) for k in rest] for t in (w, g, m, v)])
    for tgt, rs in zip((d, nm, nv), res):
        tgt.update({k: r.reshape(w[k].shape) for k, r in zip(rest, rs)})
    return d, nm, nv


def kernel(x, norm_w, w_in, s5_lambda_re, s5_lambda_im, s5_b_re, s5_b_im, s5_c_re, s5_c_im, s5_d, s5_log_step, s5_w_glu, sgu_ln_w, sgu_ln_b, sgu_w, sgu_b, m2_conv_w, m2_conv_b, m2_dt_bias, m2_a_log, m2_d, m2_norm_w, sc_conv_w, merge_b, w_branch, w_out, final_norm_w, loss_target, m_norm_w, m_w_in, m_s5_lambda_re, m_s5_lambda_im, m_s5_b_re, m_s5_b_im, m_s5_c_re, m_s5_c_im, m_s5_d, m_s5_log_step, m_s5_w_glu, m_sgu_ln_w, m_sgu_ln_b, m_sgu_w, m_sgu_b, m_m2_conv_w, m_m2_conv_b, m_m2_dt_bias, m_m2_a_log, m_m2_d, m_m2_norm_w, m_sc_conv_w, m_merge_b, m_w_branch, m_w_out, m_final_norm_w, v_norm_w, v_w_in, v_s5_lambda_re, v_s5_lambda_im, v_s5_b_re, v_s5_b_im, v_s5_c_re, v_s5_c_im, v_s5_d, v_s5_log_step, v_s5_w_glu, v_sgu_ln_w, v_sgu_ln_b, v_sgu_w, v_sgu_b, v_m2_conv_w, v_m2_conv_b, v_m2_dt_bias, v_m2_a_log, v_m2_d, v_m2_norm_w, v_sc_conv_w, v_merge_b, v_w_branch, v_w_out, v_final_norm_w):
    given = dict(locals())
    w = {k: given[k] for k in WEIGHTS}
    m = {k: given["m_" + k] for k in WEIGHTS}
    v = {k: given["v_" + k] for k in WEIGHTS}
    layers = _gather_weights(w)
    loss, dx, grads, d_final = local_step(x[0], loss_target[0], layers, final_norm_w)
    loss = lax.psum(loss, ("x", "y", "c"))
    g = _reduce_grads(grads, d_final, w)
    d, nm, nv = _update(w, g, m, v)
    return (loss, dx[None], *[g[k] for k in WEIGHTS], *[d[k] for k in WEIGHTS],
            *[nm[k] for k in WEIGHTS], *[nv[k] for k in WEIGHTS])
```

```python
import functools
from typing import Any, Callable, NamedTuple

import numpy as np
import jax
import jax.numpy as jnp
from jax import lax
from jax.experimental import pallas as pl
from jax.experimental.pallas import tpu as pltpu

f32 = jnp.float32
bf16 = jnp.bfloat16

D_MODEL = 1024
BW = 512
N_BRANCH = 4
EPS = 1e-6
S5_GROUPS, S5_P, S5_N = 32, 16, 64
S5_NS = S5_GROUPS * S5_N
CHUNK = 128
M2_HEADS, M2_HEAD_DIM, M2_GROUPS, M2_STATE = 8, 64, 2, 128
IN_DIM = 10248
PW = 10368
PW_MAIN = 10240
LANES = 128
VMEM_LIMIT = 60 * 1024 * 1024

ADAM_LR, ADAM_B1, ADAM_B2, ADAM_EPS, ADAM_WD, ADAM_STEP = 0.001, 0.9, 0.999, 1e-08, 0.01, 10

C_MERGE = 0
C_SC = 4096
C_SGU = 6144
C_S5G = 7680
C_S5U = 8192
C_M2Z = 8704
C_XBC = 9216
C_DT = 10240


def _col_segments():
    segs = [(6152, 4096)]
    for j in range(4):
        segs += [(4104 + 128 * j, 128), (4616 + 128 * j, 128), (5128 + 128 * j, 128), (5640 + 128 * j, 128)]
    segs += [(1024, 1536), (512, 512), (0, 512), (2560, 512), (3072, 1024), (4096, 8)]
    return segs


def _to_kernel_cols(w):
    parts = [w[:, s:s + n] for s, n in _col_segments()]
    parts.append(jnp.zeros((w.shape[0], PW - IN_DIM), w.dtype))
    return jnp.concatenate(parts, axis=1)


def _from_kernel_cols(wp):
    out, pos = {}, 0
    for s, n in _col_segments():
        out[s] = wp[:, pos:pos + n]
        pos += n
    return jnp.concatenate([out[s] for s in sorted(out)], axis=1)


NN = ((1,), (0,))
NT = ((1,), (1,))
TN = ((0,), (0,))


def _bd(a, b, dims):
    return lax.dot_general(a.astype(bf16), b.astype(bf16), (dims, ((), ())), preferred_element_type=f32)


def _hd(a, b, dims):
    return lax.dot_general(a, b, (dims, ((), ())), precision=lax.Precision.HIGHEST, preferred_element_type=f32)


def _make_dots(raw):
    @jax.custom_vjp
    def nn(a, b):
        return raw(a, b, NN)
    nn.defvjp(lambda a, b: (raw(a, b, NN), (a, b)), lambda r, g: (raw(g, r[1], NT), raw(r[0], g, TN)))

    @jax.custom_vjp
    def nt(a, b):
        return raw(a, b, NT)
    nt.defvjp(lambda a, b: (raw(a, b, NT), (a, b)), lambda r, g: (raw(g, r[1], NN), raw(g, r[0], TN)))

    @jax.custom_vjp
    def tn(a, b):
        return raw(a, b, TN)
    tn.defvjp(lambda a, b: (raw(a, b, TN), (a, b)), lambda r, g: (raw(r[1], g, NT), raw(r[0], g, NN)))
    return nn, nt, tn


bdot, bdot_nt, bdot_tn = _make_dots(_bd)
hdot, hdot_nt, hdot_tn = _make_dots(_hd)


@jax.custom_vjp
def bdot_w(a, w, shadow):
    return _bd(a, w, NN)


bdot_w.defvjp(lambda a, w, s: (_bd(a, w, NN), (a, w)),
              lambda r, g: (_bd(g, r[1], NT), jnp.zeros_like(r[1]), _bd(r[0], g, TN)))


def _rows(shape):
    return lax.broadcasted_iota(jnp.int32, shape, 0)


def _cols(shape):
    return lax.broadcasted_iota(jnp.int32, shape, 1)


def _shift_down(x, s):
    return jnp.where(_rows(x.shape) < s, 0.0, pltpu.roll(x, s, 0))


def _shift_up(x, s):
    n = x.shape[0]
    return jnp.where(_rows(x.shape) >= n - s, 0.0, pltpu.roll(x, n - s, 0))


@functools.partial(jax.custom_vjp, nondiff_argnums=(1,))
def shift(x, s):
    return _shift_down(x, s) if s else x


shift.defvjp(lambda x, s: (shift(x, s), None), lambda s, _, g: (_shift_up(g, s) if s else g,))


def _row_of(w, k):
    return jnp.sum(jnp.where(_rows(w.shape) == k, w, 0.0), axis=0, keepdims=True)


def _lane_mask(width, lo, hi):
    c = _cols((1, width))
    return ((c >= lo) & (c < hi)).astype(f32)


def _expand(rows, width, per):
    return (_cols((rows, width)) // per == _rows((rows, width))).astype(f32)


class A(NamedTuple):
    arr: Any
    block: tuple
    imap: Callable
    shadow: bool = False


class O(NamedTuple):
    shape: tuple
    dtype: Any
    block: tuple
    imap: Callable
    alias: Any = None


class R(NamedTuple):
    arg: int
    out: int
    off: Any = None
    acc: bool = False


def _cparams(n_grid):
    return pltpu.CompilerParams(dimension_semantics=("arbitrary",) * n_grid, vmem_limit_bytes=VMEM_LIMIT)


def _ispec(block, imap, n, rev):
    if rev:
        return pl.BlockSpec(block, lambda i: imap(n - 1 - i))
    return pl.BlockSpec(block, imap)


def _load(ref, a):
    v = ref[...]
    if a.shadow:
        return (v, jnp.zeros(v.shape, f32))
    return v.astype(f32)


def _save_spec(shape, n, rev):
    nd = len(shape)
    return _ispec((None,) + tuple(shape), lambda i: (i,) + (0,) * nd, n, rev)


def block_fwd(name, f, n, args, outs, carries=()):
    n_in, n_out, n_c = len(args), len(outs), len(carries)

    def body(*refs):
        ins, out_r = refs[:n_in], refs[n_in:n_in + n_out]
        saves, cs = refs[n_in + n_out:n_in + n_out + n_c], refs[n_in + n_out + n_c:]
        if n_c:
            @pl.when(pl.program_id(0) == 0)
            def _():
                for c in cs:
                    c[...] = jnp.zeros(c.shape, f32)
        vals = [_load(r, a) for r, a in zip(ins, args)]
        cv = [c[...] for c in cs]
        for s, v in zip(saves, cv):
            s[...] = v
        res = f(*vals, *cv)
        for r, v in zip(out_r, res[:n_out]):
            r[...] = v.astype(r.dtype)
        for c, v in zip(cs, res[n_out:]):
            c[...] = v

    out_shape = [jax.ShapeDtypeStruct(o.shape, o.dtype) for o in outs]
    out_specs = [pl.BlockSpec(o.block, o.imap) for o in outs]
    for shp in carries:
        out_shape.append(jax.ShapeDtypeStruct((n,) + tuple(shp), f32))
        out_specs.append(_save_spec(shp, n, False))
    return pl.pallas_call(
        body, name=name, grid=(n,),
        in_specs=[pl.BlockSpec(a.block, a.imap) for a in args],
        out_specs=out_specs, out_shape=out_shape,
        scratch_shapes=[pltpu.VMEM(tuple(shp), f32) for shp in carries],
        compiler_params=_cparams(1),
    )(*[a.arr for a in args])


def block_bwd(name, f, n, args, cots, gouts, routes, saved=(), rev=False):
    n_in, n_cot, n_c, n_go = len(args), len(cots), len(saved), len(gouts)
    diff = []
    for r in routes:
        if r.arg not in diff:
            diff.append(r.arg)
    aliases = [(k, o.alias) for k, o in enumerate(gouts) if o.alias is not None]

    def body(*refs):
        ins = refs[:n_in]
        cot_r = refs[n_in:n_in + n_cot]
        sav_r = refs[n_in + n_cot:n_in + n_cot + n_c]
        base = n_in + n_cot + n_c + len(aliases)
        go_r = refs[base:base + n_go]
        dcs = refs[base + n_go:]
        step = pl.program_id(0)
        if n_c:
            @pl.when(step == 0)
            def _():
                for d in dcs:
                    d[...] = jnp.zeros(d.shape, f32)
        vals = [_load(r, a) for r, a in zip(ins, args)]
        cv = [s[...] for s in sav_r]
        nd = len(diff)

        def g(*dv):
            full = list(vals)
            for idx, v in zip(diff, dv[:nd]):
                full[idx] = (vals[idx][0], v) if args[idx].shadow else v
            return tuple(f(*full, *dv[nd:]))

        primals = [vals[i][1] if args[i].shadow else vals[i] for i in diff] + cv
        _, vjp_fn = jax.vjp(g, *primals)
        ct = tuple([r[...].astype(f32) for r in cot_r] + [d[...] for d in dcs])
        grads = vjp_fn(ct)
        for r in routes:
            gr = grads[diff.index(r.arg)]
            ref = go_r[r.out]
            if r.acc:
                @pl.when(step == 0)
                def _(ref=ref, gr=gr):
                    ref[...] = gr.astype(ref.dtype)

                @pl.when(step > 0)
                def _(ref=ref, gr=gr):
                    ref[...] += gr.astype(ref.dtype)
            elif r.off is None:
                ref[...] = gr.astype(ref.dtype)
            else:
                ref[:, r.off:r.off + gr.shape[1]] = gr.astype(ref.dtype)
        for d, gr in zip(dcs, grads[nd:]):
            d[...] = gr

    in_specs = [_ispec(a.block, a.imap, n, rev) for a in list(args) + list(cots)]
    in_specs += [_save_spec(s.shape[1:], n, rev) for s in saved]
    in_specs += [pl.BlockSpec(memory_space=pl.ANY) for _ in aliases]
    operands = [a.arr for a in list(args) + list(cots)] + list(saved) + [arr for _, arr in aliases]
    io_alias = {n_in + n_cot + n_c + j: k for j, (k, _) in enumerate(aliases)}
    return pl.pallas_call(
        body, name=name, grid=(n,),
        in_specs=in_specs,
        out_specs=[_ispec(o.block, o.imap, n, rev) for o in gouts],
        out_shape=[jax.ShapeDtypeStruct(o.shape, o.dtype) for o in gouts],
        scratch_shapes=[pltpu.VMEM(tuple(s.shape[1:]), f32) for s in saved],
        input_output_aliases=io_alias,
        compiler_params=_cparams(1),
    )(*operands)


class Op(NamedTuple):
    arr: Any
    row: int = 0
    col: int = 0


def mm(name, mode, pairs, m, n, tm, tn, out_dtype=f32, add=None, out=None, out_col=0, out_width=None):
    tm, tn = min(tm, m), min(tn, n)
    assert m % tm == 0 and n % tn == 0
    in_specs, operands = [], []
    for a, b, k, _ in pairs:
        if mode == TN:
            assert a.row % k == 0 and a.col % tm == 0
            in_specs.append(pl.BlockSpec((k, tm), lambda j, i, a=a, k=k: (a.row // k, i + a.col // tm)))
        else:
            assert a.col % k == 0 and a.row % tm == 0
            in_specs.append(pl.BlockSpec((tm, k), lambda j, i, a=a, k=k: (i + a.row // tm, a.col // k)))
        if mode == NT:
            assert b.col % k == 0 and b.row % tn == 0
            in_specs.append(pl.BlockSpec((tn, k), lambda j, i, b=b, k=k: (j + b.row // tn, b.col // k)))
        else:
            assert b.row % k == 0 and b.col % tn == 0
            in_specs.append(pl.BlockSpec((k, tn), lambda j, i, b=b, k=k: (b.row // k, j + b.col // tn)))
        operands += [a.arr, b.arr]
    n_p = len(pairs)
    if add is not None:
        assert add.col % tn == 0
        in_specs.append(pl.BlockSpec((tm, tn), lambda j, i: (i, j + add.col // tn)))
        operands.append(add.arr)
    io_alias = {}
    if out is not None:
        assert out_col % tn == 0
        in_specs.append(pl.BlockSpec(memory_space=pl.ANY))
        operands.append(out)
        io_alias = {len(operands) - 1: 0}
        out_shape = jax.ShapeDtypeStruct(out.shape, out.dtype)
    else:
        out_shape = jax.ShapeDtypeStruct((m, out_width or n), out_dtype)
    signs = [p[3] for p in pairs]

    def body(*refs):
        o = refs[-1]
        acc = None
        for p in range(n_p):
            t = _bd(refs[2 * p][...], refs[2 * p + 1][...], mode)
            t = t if signs[p] > 0 else -t
            acc = t if acc is None else acc + t
        if add is not None:
            acc = acc + refs[2 * n_p][...].astype(f32)
        o[...] = acc.astype(o.dtype)

    return pl.pallas_call(
        body, name=name, grid=(n // tn, m // tm),
        in_specs=in_specs,
        out_specs=pl.BlockSpec((tm, tn), lambda j, i: (i, j + out_col // tn)),
        out_shape=out_shape, input_output_aliases=io_alias,
        compiler_params=_cparams(2),
    )(*operands)


SCAN_LANES = 512
SUBL = 8


def _cmul(p, q):
    return (p[0] * q[0] - p[1] * q[1], p[0] * q[1] + p[1] * q[0])


def _powers(a):
    a2 = _cmul(a, a)
    a4 = _cmul(a2, a2)
    a6 = _cmul(a4, a2)
    return [a, a2, _cmul(a2, a), a4, _cmul(a4, a), a6, _cmul(a6, a), _cmul(a4, a4)]


def _table(pw, order, w):
    row = _rows((SUBL, w))
    re = sum(jnp.where(row == t, pw[k][0], 0.0) for t, k in enumerate(order))
    im = sum(jnp.where(row == t, pw[k][1], 0.0) for t, k in enumerate(order))
    return re, im


def _pick_row(x, t):
    return jnp.sum(jnp.where(_rows(x.shape) == t, x, 0.0), axis=0, keepdims=True)


def s5_scan_fwd(bu_re, bu_im, a_re, a_im):
    seq, ns = bu_re.shape
    w, nb = SCAN_LANES, ns // SCAN_LANES

    def body(b_re, b_im, ar, ai, s_re, s_im):
        a = (ar[...], ai[...])
        pw = _powers(a)
        tab = _table(pw, list(range(SUBL)), w)
        row = _rows((SUBL, w))

        def step(i, carry):
            t0 = pl.multiple_of(i * SUBL, SUBL)
            x = (b_re[pl.ds(t0, SUBL), :], b_im[pl.ds(t0, SUBL), :])
            for d, k in ((1, 0), (2, 1), (4, 3)):
                sh = (jnp.where(row < d, 0.0, pltpu.roll(x[0], d, 0)), jnp.where(row < d, 0.0, pltpu.roll(x[1], d, 0)))
                t = _cmul(pw[k], sh)
                x = (x[0] + t[0], x[1] + t[1])
            t = _cmul(tab, carry)
            x = (x[0] + t[0], x[1] + t[1])
            s_re[pl.ds(t0, SUBL), :] = x[0]
            s_im[pl.ds(t0, SUBL), :] = x[1]
            return (x[0][SUBL - 1:, :], x[1][SUBL - 1:, :])

        z = jnp.zeros((1, w), f32)
        lax.fori_loop(0, seq // SUBL, step, (z, z), unroll=2)

    strip = pl.BlockSpec((seq, w), lambda j: (0, j))
    lane = pl.BlockSpec((1, w), lambda j: (0, j))
    return pl.pallas_call(
        body, name="s5_scan_fwd", grid=(nb,),
        in_specs=[strip, strip, lane, lane],
        out_specs=[strip, strip],
        out_shape=[jax.ShapeDtypeStruct((seq, ns), f32)] * 2,
        compiler_params=_cparams(1),
    )(bu_re, bu_im, a_re, a_im)


def s5_scan_bwd(ds_re, ds_im, s_re, s_im, a_re, a_im):
    seq, ns = ds_re.shape
    w, nb = SCAN_LANES, ns // SCAN_LANES
    nblk = seq // SUBL

    def body(g_re, g_im, sr, si, ar, ai, l_re, l_im, da_re, da_im):
        a = (ar[...], -ai[...])
        pw = _powers(a)
        tab = _table(pw, [SUBL - 1 - t for t in range(SUBL)], w)
        row = _rows((SUBL, w))

        def step(kk, carry):
            c_re, c_im, acc_re, acc_im = carry
            i = nblk - 1 - kk
            t0 = pl.multiple_of(i * SUBL, SUBL)
            x = (g_re[pl.ds(t0, SUBL), :], g_im[pl.ds(t0, SUBL), :])
            for d, k in ((1, 0), (2, 1), (4, 3)):
                sh = (jnp.where(row >= SUBL - d, 0.0, pltpu.roll(x[0], SUBL - d, 0)),
                      jnp.where(row >= SUBL - d, 0.0, pltpu.roll(x[1], SUBL - d, 0)))
                t = _cmul(pw[k], sh)
                x = (x[0] + t[0], x[1] + t[1])
            t = _cmul(tab, (c_re, c_im))
            x = (x[0] + t[0], x[1] + t[1])
            l_re[pl.ds(t0, SUBL), :] = x[0]
            l_im[pl.ds(t0, SUBL), :] = x[1]
            tp = pl.multiple_of(jnp.maximum(i - 1, 0) * SUBL, SUBL)
            live = (i > 0).astype(f32)
            p_re = _pick_row(sr[pl.ds(tp, SUBL), :], SUBL - 1) * live
            p_im = _pick_row(si[pl.ds(tp, SUBL), :], SUBL - 1) * live
            sp_re = jnp.where(row == 0, p_re, pltpu.roll(sr[pl.ds(t0, SUBL), :], 1, 0))
            sp_im = jnp.where(row == 0, p_im, pltpu.roll(si[pl.ds(t0, SUBL), :], 1, 0))
            acc_re = acc_re + x[0] * sp_re + x[1] * sp_im
            acc_im = acc_im + x[1] * sp_re - x[0] * sp_im
            return (x[0][:1, :], x[1][:1, :], acc_re, acc_im)

        z1 = jnp.zeros((1, w), f32)
        z8 = jnp.zeros((SUBL, w), f32)
        _, _, acc_re, acc_im = lax.fori_loop(0, nblk, step, (z1, z1, z8, z8), unroll=2)
        da_re[...] = jnp.sum(acc_re, axis=0, keepdims=True)
        da_im[...] = jnp.sum(acc_im, axis=0, keepdims=True)

    strip = pl.BlockSpec((seq, w), lambda j: (0, j))
    lane = pl.BlockSpec((1, w), lambda j: (0, j))
    return pl.pallas_call(
        body, name="s5_scan_bwd", grid=(nb,),
        in_specs=[strip, strip, strip, strip, lane, lane],
        out_specs=[strip, strip, lane, lane],
        out_shape=[jax.ShapeDtypeStruct((seq, ns), f32)] * 2 + [jax.ShapeDtypeStruct((1, ns), f32)] * 2,
        compiler_params=_cparams(1),
    )(ds_re, ds_im, s_re, s_im, a_re, a_im)


def _rms(x, w):
    return x * lax.rsqrt(jnp.mean(x * x, axis=-1, keepdims=True) + EPS) * w


def f_rms(x, w):
    return (_rms(x, w),)


def f_rms_res(x, w):
    return (_rms(x, w), x)


def f_s5_prep(lam_re, lam_im, log_step, b_re, b_im):
    e = _expand(log_step.shape[1], S5_NS, S5_N)
    step = hdot(jnp.exp(log_step), e)
    mag = jnp.exp(lam_re * step)
    ab_re, ab_im = mag * jnp.cos(lam_im * step), mag * jnp.sin(lam_im * step)
    den = lam_re * lam_re + lam_im * lam_im
    nr = ab_re - 1.0
    coef_re = (nr * lam_re + ab_im * lam_im) / den
    coef_im = (ab_im * lam_re - nr * lam_im) / den
    bb_re, bb_im = coef_re * b_re - coef_im * b_im, coef_re * b_im + coef_im * b_re
    sel = (_rows((BW, S5_P)) % S5_P == _cols((BW, S5_P))).astype(f32)
    blk = _rows((BW, S5_NS)) // S5_P == _cols((BW, S5_NS)) // S5_N
    rows_bd = lambda t: jnp.where(blk, hdot(sel, t), 0.0)
    return (ab_re, ab_im, rows_bd(bb_re), rows_bd(bb_im))


def f_s5_c(c_re, c_im):
    sel_t = (_cols((S5_P, BW)) % S5_P == _rows((S5_P, BW))).astype(f32)
    blk_t = _rows((S5_NS, BW)) // S5_N == _cols((S5_NS, BW)) // S5_P
    cols_bd = lambda t: jnp.where(blk_t, hdot_tn(t, sel_t), 0.0)
    return (cols_bd(c_re), cols_bd(c_im))


def f_s5_act(y_lin, u, gate, d, w_glu):
    y = jax.nn.gelu(y_lin + d * u)
    y = y * jax.nn.sigmoid(bdot_w(y, *w_glu))
    return (y * jax.nn.silu(gate),)


def f_sgu(u, v, gate, ln_w, ln_b, *rest):
    w_s, b_pad = rest[:8], rest[8]
    t = u.shape[0]
    u32, v32 = jax.nn.gelu(u), jax.nn.gelu(v)
    mu = jnp.mean(v32, axis=-1, keepdims=True)
    var = jnp.mean(jnp.square(v32 - mu), axis=-1, keepdims=True)
    vn = (v32 - mu) * lax.rsqrt(var + EPS) * ln_w + ln_b
    tri = _rows((t, t)) >= _cols((t, t))
    s = hdot_tn(b_pad, _expand(LANES, BW, BW // 8))
    for h in range(8):
        s = s + bdot(jnp.where(tri, w_s[h], 0.0), vn) * _lane_mask(BW, 64 * h, 64 * h + 64)
    return (u32 * s * jax.nn.silu(gate),)


def f_m2_conv(x, w, b):
    return (sum(_row_of(w, k) * shift(x, 3 - k) for k in range(4)) + b,)


def f_sc(bg, cg, h, gate, w):
    z = cg * h
    conv = sum(_row_of(w, k) * shift(z, 2 - k) for k in range(3))
    return (bg * conv * jax.nn.silu(gate),)


def f_m2(z, xc, b0, b1, c0, c1, dt_raw, dt_bias, a_log, d_par, norm_w, st):
    q = z.shape[0]
    x = jax.nn.silu(xc)
    bm, cm = (jax.nn.silu(b0), jax.nn.silu(b1)), (jax.nn.silu(c0), jax.nn.silu(c1))
    dt = jax.nn.softplus(dt_raw + dt_bias)
    da = dt * (-jnp.exp(a_log))
    tri = _rows((q, q)) >= _cols((q, q))
    acs = hdot(tri.astype(f32), da)
    e = _expand(LANES, BW, M2_HEAD_DIM)
    dt_f, acs_f = hdot(dt, e), hdot(acs, e)
    last = _rows((q, BW)) == q - 1
    alast_f = jnp.sum(jnp.where(last, acs_f, 0.0), axis=0, keepdims=True)
    xdt = x * dt_f
    xdec = xdt * jnp.exp(alast_f - acs_f)
    acs_t = acs.T
    st_new = st * jnp.exp(alast_f)
    y_diag, y_off = 0.0, 0.0
    for g in range(M2_GROUPS):
        gm = _lane_mask(BW, 256 * g, 256 * g + 256)
        cb = bdot_nt(cm[g], bm[g])
        st_new = st_new + bdot_tn(bm[g], xdec * gm)
        y_off = y_off + bdot(cm[g], st) * gm
        for hh in range(M2_HEADS // M2_GROUPS):
            h = g * (M2_HEADS // M2_GROUPS) + hh
            col = jnp.sum(jnp.where(_cols((q, LANES)) == h, acs, 0.0), axis=1, keepdims=True)
            row = jnp.sum(jnp.where(_rows((LANES, q)) == h, acs_t, 0.0), axis=0, keepdims=True)
            decay = jnp.exp(jnp.where(tri, col - row, -1e30))
            y_diag = y_diag + bdot(cb * decay, xdt) * _lane_mask(BW, 64 * h, 64 * h + 64)
    d_f = sum(jnp.sum(jnp.where(_cols((1, LANES)) == h, d_par, 0.0), axis=1, keepdims=True)
              * _lane_mask(BW, 64 * h, 64 * h + 64) for h in range(M2_HEADS))
    y = y_diag + y_off * jnp.exp(acs_f) + d_f * x
    y = y * jax.nn.silu(z)
    return (_rms(y, norm_w), st_new)


def f_gate_mix(*v):
    bo, lg, mb = v[0:4], v[4:8], v[8:12]
    return (sum(jax.nn.sigmoid(lg[k] + mb[k]) * bo[k] for k in range(N_BRANCH)),)


def _full(shape):
    nd = len(shape)
    return dict(block=tuple(shape), imap=lambda i: (0,) * nd)


def _param(arr, shadow=False):
    return A(arr, tuple(arr.shape), lambda i, nd=arr.ndim: (0,) * nd, shadow)


def _tb(arr, t, width, colblk):
    return A(arr, (t, width), lambda i: (i, colblk))


def _strip(arr, seq, colblk0, stride=1):
    return A(arr, (seq, LANES), lambda j: (0, colblk0 + stride * j))


def _s5_prep_args(p):
    lam_re = p["s5_lambda_re"].reshape(1, S5_NS)
    lam_im = p["s5_lambda_im"].reshape(1, S5_NS)
    log_step = jnp.pad(p["s5_log_step"].reshape(1, S5_GROUPS), ((0, 0), (0, LANES - S5_GROUPS)))
    b_lanes = lambda b: jnp.transpose(b, (2, 0, 1)).reshape(S5_P, S5_NS)
    return [_param(v) for v in (lam_re, lam_im, log_step, b_lanes(p["s5_b_re"]), b_lanes(p["s5_b_im"]))]


def _s5_c_args(p):
    c_lanes = lambda c: jnp.transpose(c, (1, 0, 2)).reshape(S5_P, S5_NS)
    return [_param(c_lanes(p["s5_c_re"])), _param(c_lanes(p["s5_c_im"]))]


def layer_fwd(x, p):
    seq = x.shape[0]
    nt = seq // CHUNK
    t2 = 256
    sv = {}
    hb = block_fwd("rms_in", f_rms, seq // t2, [_tb(x, t2, D_MODEL, 0), _param(p["norm_w"].reshape(1, D_MODEL))],
                   [O((seq, D_MODEL), bf16, (t2, D_MODEL), lambda i: (i, 0))])[0]
    proj = mm("mm_in", NN, [(Op(hb), Op(p["w_in"]), D_MODEL, 1)], seq, PW, 512, 1152)
    whole = lambda shape, dt: O(shape, dt, shape, lambda i: (0, 0))
    ab_re, ab_im, bb_re, bb_im = block_fwd("s5_prep", f_s5_prep, 1, _s5_prep_args(p),
                                           [whole((1, S5_NS), f32)] * 2 + [whole((BW, S5_NS), bf16)] * 2)
    cc_re, cc_im = block_fwd("s5_c", f_s5_c, 1, _s5_c_args(p), [whole((S5_NS, BW), bf16)] * 2)
    bu_re = mm("mm_bu_re", NN, [(Op(proj, 0, C_S5U), Op(bb_re), BW, 1)], seq, S5_NS, 512, 1024)
    bu_im = mm("mm_bu_im", NN, [(Op(proj, 0, C_S5U), Op(bb_im), BW, 1)], seq, S5_NS, 512, 1024)
    s_re, s_im = s5_scan_fwd(bu_re, bu_im, ab_re, ab_im)
    y_lin = mm("mm_s5y", NN, [(Op(s_re), Op(cc_re), S5_NS, 1), (Op(s_im), Op(cc_im), S5_NS, -1)], seq, BW, 256, BW)
    s5_act_args = [_tb(y_lin, CHUNK, BW, 0), _tb(proj, CHUNK, BW, C_S5U // BW), _tb(proj, CHUNK, BW, C_S5G // BW),
                   _param(p["s5_d"].reshape(1, BW)), _param(p["s5_w_glu"], True)]
    out_bw = O((seq, BW), f32, (CHUNK, BW), lambda i: (i, 0))
    y_a = block_fwd("s5_act", f_s5_act, nt, s5_act_args, [out_bw])[0]
    y_b = block_fwd("sgu", f_sgu, nt, _sgu_args(proj, p), [out_bw])[0]
    xc = block_fwd("m2_conv", f_m2_conv, 8, _m2_conv_args(proj, p, seq),
                   [O((seq, 2 * BW), f32, (seq, LANES), lambda j: (0, j))])[0]
    y_c, st_saved = block_fwd("m2_ssd", f_m2, nt, _m2_args(proj, xc, p), [out_bw], carries=[(M2_STATE, BW)])
    y_d = block_fwd("sc", f_sc, 4, _sc_args(proj, p, seq), [O((seq, BW), f32, (seq, LANES), lambda j: (0, j))])[0]
    ys = [y_a, y_b, y_c, y_d]
    bo = [mm(f"mm_branch{k}", NN, [(Op(ys[k]), Op(p["w_branch"], k * BW, 0), BW, 1)], seq, D_MODEL, 512, D_MODEL)
          for k in range(N_BRANCH)]
    merged = block_fwd("gate_mix", f_gate_mix, seq // t2, _mix_args(bo, proj, p, t2),
                       [O((seq, D_MODEL), bf16, (t2, D_MODEL), lambda i: (i, 0))])[0]
    x_new = mm("mm_out", NN, [(Op(merged), Op(p["w_out"]), D_MODEL, 1)], seq, D_MODEL, 512, D_MODEL, add=Op(x))
    sv.update(x=x, hb=hb, proj=proj, ab=(ab_re, ab_im), bb=(bb_re, bb_im), cc=(cc_re, cc_im), s=(s_re, s_im), y_lin=y_lin,
              xc=xc, st=st_saved, ys=ys, bo=bo, merged=merged)
    return x_new, sv


def _sgu_args(proj, p):
    c0 = C_SGU // BW
    args = [_tb(proj, CHUNK, BW, c0), _tb(proj, CHUNK, BW, c0 + 1), _tb(proj, CHUNK, BW, c0 + 2),
            _param(p["sgu_ln_w"].reshape(1, BW)), _param(p["sgu_ln_b"].reshape(1, BW))]
    args += [A(p["sgu_w"], (None, CHUNK, CHUNK), lambda i, h=h: (h, 0, 0)) for h in range(8)]
    args.append(_param(jnp.pad(p["sgu_b"], ((0, LANES - 8), (0, 0)))))
    return args


def _m2_conv_args(proj, p, seq):
    return [_strip(proj, seq, C_XBC // LANES), A(p["m2_conv_w"], (4, LANES), lambda j: (0, j)),
            A(p["m2_conv_b"].reshape(1, 2 * BW), (1, LANES), lambda j: (0, j))]


def _pad_lanes(v):
    return jnp.pad(v.reshape(1, -1), ((0, 0), (0, LANES - v.size)))


def _m2_args(proj, xc, p):
    args = [_tb(proj, CHUNK, BW, C_M2Z // BW), _tb(xc, CHUNK, BW, 0)]
    args += [_tb(xc, CHUNK, LANES, 4 + k) for k in range(4)]
    args.append(_tb(proj, CHUNK, LANES, C_DT // LANES))
    args += [_param(_pad_lanes(p["m2_dt_bias"])), _param(_pad_lanes(p["m2_a_log"])), _param(_pad_lanes(p["m2_d"])),
             _param(p["m2_norm_w"].reshape(1, BW))]
    return args


def _sc_args(proj, p, seq):
    c0 = C_SC // LANES
    return [_strip(proj, seq, c0 + k, 4) for k in range(4)] + [A(p["sc_conv_w"], (3, LANES), lambda j: (0, j))]


def _mix_args(bo, proj, p, t):
    args = [_tb(bo[k], t, D_MODEL, 0) for k in range(N_BRANCH)]
    args += [_tb(proj, t, D_MODEL, k) for k in range(N_BRANCH)]
    mb = p["merge_b"].reshape(N_BRANCH, 1, D_MODEL)
    args += [A(mb, (None, 1, D_MODEL), lambda i, k=k: (k, 0, 0)) for k in range(N_BRANCH)]
    return args


def layer_bwd(d_out, p, sv):
    seq = d_out.shape[0]
    nt = seq // CHUNK
    t2 = 256
    proj, ys, bo = sv["proj"], sv["ys"], sv["bo"]
    g = {}
    acc = lambda shape: O(tuple(shape), f32, tuple(shape), lambda i, nd=len(shape): (0,) * nd)
    d_merged = mm("mm_out_dx", NT, [(Op(d_out), Op(p["w_out"]), D_MODEL, 1)], seq, D_MODEL, 512, D_MODEL, out_dtype=f32)
    g["w_out"] = mm("mm_out_dw", TN, [(Op(sv["merged"]), Op(d_out), seq, 1)], D_MODEL, D_MODEL, 256, D_MODEL, out_dtype=bf16)
    gouts = [O((seq, D_MODEL), bf16, (t2, D_MODEL), lambda i: (i, 0)) for _ in range(N_BRANCH)]
    gouts.append(O((seq, PW_MAIN), bf16, (t2, N_BRANCH * D_MODEL), lambda i: (i, 0)))
    gouts += [acc((1, D_MODEL)) for _ in range(N_BRANCH)]
    routes = [R(k, k) for k in range(N_BRANCH)] + [R(4 + k, 4, k * D_MODEL) for k in range(N_BRANCH)]
    routes += [R(8 + k, 5 + k, acc=True) for k in range(N_BRANCH)]
    res = block_bwd("gate_mix_bwd", f_gate_mix, seq // t2, _mix_args(bo, proj, p, t2),
                    [_tb(d_merged, t2, D_MODEL, 0)], gouts, routes)
    dbo, dproj = res[:4], res[4]
    g["merge_b"] = jnp.concatenate(res[5:9], axis=0)
    dys = [mm(f"mm_branch{k}_dx", NT, [(Op(dbo[k]), Op(p["w_branch"], k * BW, 0), D_MODEL, 1)], seq, BW, 512, BW)
           for k in range(N_BRANCH)]
    g["w_branch"] = jnp.stack([mm(f"mm_branch{k}_dw", TN, [(Op(ys[k]), Op(dbo[k]), seq, 1)], BW, D_MODEL, 256, D_MODEL, out_dtype=bf16)
                               for k in range(N_BRANCH)])
    res = block_bwd("sc_bwd", f_sc, 4, _sc_args(proj, p, seq), [_strip(dys[3], seq, 0)],
                    [O((seq, PW_MAIN), bf16, (seq, 4 * LANES), lambda j: (0, C_SC // (4 * LANES) + j), alias=dproj),
                     O((3, BW), f32, (3, LANES), lambda j: (0, j))],
                    [R(k, 0, k * LANES) for k in range(4)] + [R(4, 1)])
    dproj, g["sc_conv_w"] = res
    res = block_bwd("m2_ssd_bwd", f_m2, nt, _m2_args(proj, sv["xc"], p), [_tb(dys[2], CHUNK, BW, 0)],
                    [O((seq, PW_MAIN), bf16, (CHUNK, BW), lambda i: (i, C_M2Z // BW), alias=dproj),
                     O((seq, 2 * BW), f32, (CHUNK, 2 * BW), lambda i: (i, 0)),
                     O((seq, LANES), bf16, (CHUNK, LANES), lambda i: (i, 0)),
                     acc((1, LANES)), acc((1, LANES)), acc((1, LANES)), acc((1, BW))],
                    [R(0, 0), R(1, 1, 0)] + [R(2 + k, 1, BW + k * LANES) for k in range(4)] + [R(6, 2)]
                    + [R(7, 3, acc=True), R(8, 4, acc=True), R(9, 5, acc=True), R(10, 6, acc=True)],
                    saved=[sv["st"]], rev=True)
    dproj, dxc, d_dt = res[0], res[1], res[2]
    g["m2_dt_bias"], g["m2_a_log"], g["m2_d"] = (r[0, :M2_HEADS] for r in res[3:6])
    g["m2_norm_w"] = res[6].reshape(BW)
    res = block_bwd("m2_conv_bwd", f_m2_conv, 8, _m2_conv_args(proj, p, seq), [_strip(dxc, seq, 0)],
                    [O((seq, PW_MAIN), bf16, (seq, LANES), lambda j: (0, C_XBC // LANES + j), alias=dproj),
                     O((4, 2 * BW), f32, (4, LANES), lambda j: (0, j)), O((1, 2 * BW), f32, (1, LANES), lambda j: (0, j))],
                    [R(0, 0), R(1, 1), R(2, 2)])
    dproj, g["m2_conv_w"], cb = res
    g["m2_conv_b"] = cb.reshape(2 * BW)
    res = block_bwd("sgu_bwd", f_sgu, nt, _sgu_args(proj, p), [_tb(dys[1], CHUNK, BW, 0)],
                    [O((seq, PW_MAIN), bf16, (CHUNK, 3 * BW), lambda i: (i, C_SGU // (3 * BW)), alias=dproj),
                     acc((1, BW)), acc((1, BW))] + [acc((CHUNK, CHUNK)) for _ in range(8)] + [acc((LANES, CHUNK))],
                    [R(0, 0, 0), R(1, 0, BW), R(2, 0, 2 * BW), R(3, 1, acc=True), R(4, 2, acc=True)]
                    + [R(5 + h, 3 + h, acc=True) for h in range(8)] + [R(13, 11, acc=True)])
    dproj = res[0]
    g["sgu_ln_w"], g["sgu_ln_b"] = res[1].reshape(BW), res[2].reshape(BW)
    g["sgu_w"] = jnp.stack(res[3:11])
    g["sgu_b"] = res[11][:8]
    y_lin, (s_re, s_im), (ab_re, ab_im) = sv["y_lin"], sv["s"], sv["ab"]
    s5_act_args = [_tb(y_lin, CHUNK, BW, 0), _tb(proj, CHUNK, BW, C_S5U // BW), _tb(proj, CHUNK, BW, C_S5G // BW),
                   _param(p["s5_d"].reshape(1, BW)), _param(p["s5_w_glu"], True)]
    res = block_bwd("s5_act_bwd", f_s5_act, nt, s5_act_args, [_tb(dys[0], CHUNK, BW, 0)],
                    [O((seq, BW), bf16, (CHUNK, BW), lambda i: (i, 0)), O((seq, BW), f32, (CHUNK, BW), lambda i: (i, 0)),
                     O((seq, PW_MAIN), bf16, (CHUNK, BW), lambda i: (i, C_S5G // BW), alias=dproj),
                     acc((1, BW)), acc((BW, BW))],
                    [R(0, 0), R(1, 1), R(2, 2), R(3, 3, acc=True), R(4, 4, acc=True)])
    dy_lin, du1, dproj = res[0], res[1], res[2]
    g["s5_d"] = res[3].reshape(S5_GROUPS, S5_P)
    g["s5_w_glu"] = res[4]
    (bb_re, bb_im), (cc_re, cc_im) = sv["bb"], sv["cc"]
    ds_re = mm("mm_s5y_dre", NT, [(Op(dy_lin), Op(cc_re), BW, 1)], seq, S5_NS, 512, 1024)
    ds_im = mm("mm_s5y_dim", NT, [(Op(dy_lin), Op(cc_im), BW, -1)], seq, S5_NS, 512, 1024)
    dc_re = mm("mm_s5y_dcre", TN, [(Op(s_re), Op(dy_lin), seq, 1)], S5_NS, BW, 512, BW)
    dc_im = mm("mm_s5y_dcim", TN, [(Op(s_im), Op(dy_lin), seq, -1)], S5_NS, BW, 512, BW)
    l_re, l_im, da_re, da_im = s5_scan_bwd(ds_re, ds_im, s_re, s_im, ab_re, ab_im)
    dproj = mm("mm_bu_dx", NT, [(Op(l_re), Op(bb_re), S5_NS, 1), (Op(l_im), Op(bb_im), S5_NS, 1)],
               seq, BW, 512, BW, add=Op(du1), out=dproj, out_col=C_S5U)
    dbb = [mm(f"mm_bu_dw{n}", TN, [(Op(proj, 0, C_S5U), Op(l), seq, 1)], BW, S5_NS, 256, 1024)
           for n, l in (("re", l_re), ("im", l_im))]
    gouts = [acc((1, S5_NS)), acc((1, S5_NS)), acc((1, LANES))] + [acc((S5_P, S5_NS))] * 2
    res = block_bwd("s5_prep_bwd", f_s5_prep, 1, _s5_prep_args(p),
                    [_param(v) for v in (da_re, da_im, dbb[0], dbb[1])], gouts, [R(k, k, acc=True) for k in range(5)])
    g["s5_lambda_re"], g["s5_lambda_im"] = res[0].reshape(S5_GROUPS, S5_N), res[1].reshape(S5_GROUPS, S5_N)
    g["s5_log_step"] = res[2][0, :S5_GROUPS]
    b_natural = lambda b: jnp.transpose(b.reshape(S5_P, S5_GROUPS, S5_N), (1, 2, 0))
    c_natural = lambda c: jnp.transpose(c.reshape(S5_P, S5_GROUPS, S5_N), (1, 0, 2))
    g["s5_b_re"], g["s5_b_im"] = b_natural(res[3]), b_natural(res[4])
    res = block_bwd("s5_c_bwd", f_s5_c, 1, _s5_c_args(p), [_param(dc_re), _param(dc_im)],
                    [acc((S5_P, S5_NS))] * 2, [R(0, 0, acc=True), R(1, 1, acc=True)])
    g["s5_c_re"], g["s5_c_im"] = c_natural(res[0]), c_natural(res[1])
    hb, w_in = sv["hb"], p["w_in"]
    dh = mm("mm_in_dx", NT, [(Op(dproj), Op(w_in), PW_MAIN, 1), (Op(d_dt), Op(w_in, 0, PW_MAIN), LANES, 1)],
            seq, D_MODEL, 256, 256)
    dw = mm("mm_in_dw", TN, [(Op(hb), Op(dproj), seq, 1)], D_MODEL, PW_MAIN, 256, 1024, out_dtype=bf16, out_width=PW)
    g["w_in"] = mm("mm_in_dwdt", TN, [(Op(hb), Op(d_dt), seq, 1)], D_MODEL, LANES, 256, LANES, out=dw, out_col=C_DT)
    dx, dnw = block_bwd("rms_in_bwd", f_rms_res, seq // t2, [_tb(sv["x"], t2, D_MODEL, 0), _param(p["norm_w"].reshape(1, D_MODEL))],
                        [_tb(dh, t2, D_MODEL, 0), _tb(d_out, t2, D_MODEL, 0)],
                        [O((seq, D_MODEL), f32, (t2, D_MODEL), lambda i: (i, 0)), acc((1, D_MODEL))],
                        [R(0, 0), R(1, 1, acc=True)])
    g["norm_w"] = dnw.reshape(D_MODEL)
    return dx, g


def loss_head(x, w, target):
    seq = x.shape[0]
    t = 256

    def body(x_ref, w_ref, t_ref, loss_ref, dx_ref, dw_ref):
        step = pl.program_id(0)

        def f(xv, wv):
            err = _rms(xv, wv) - t_ref[...]
            return 0.5 * jnp.sum(jnp.mean(err * err, axis=-1, keepdims=True), axis=0, keepdims=True)

        val, vjp_fn = jax.vjp(f, x_ref[...], w_ref[...])
        dx, dw = vjp_fn(jnp.ones((1, 1), f32))
        dx_ref[...] = dx

        @pl.when(step == 0)
        def _():
            loss_ref[...] = jnp.broadcast_to(val, loss_ref.shape)
            dw_ref[...] = dw

        @pl.when(step > 0)
        def _():
            loss_ref[...] += jnp.broadcast_to(val, loss_ref.shape)
            dw_ref[...] += dw

    blk = pl.BlockSpec((t, D_MODEL), lambda i: (i, 0))
    row = pl.BlockSpec((1, D_MODEL), lambda i: (0, 0))
    return pl.pallas_call(
        body, name="loss_head", grid=(seq // t,),
        in_specs=[blk, row, blk],
        out_specs=[pl.BlockSpec((1, LANES), lambda i: (0, 0)), blk, row],
        out_shape=[jax.ShapeDtypeStruct((1, LANES), f32), jax.ShapeDtypeStruct((seq, D_MODEL), f32),
                   jax.ShapeDtypeStruct((1, D_MODEL), f32)],
        compiler_params=_cparams(1),
    )(x, w.reshape(1, D_MODEL), target)


LAYER_KEYS = ("norm_w", "w_in", "s5_lambda_re", "s5_lambda_im", "s5_b_re", "s5_b_im", "s5_c_re", "s5_c_im", "s5_d",
              "s5_log_step", "s5_w_glu", "sgu_ln_w", "sgu_ln_b", "sgu_w", "sgu_b", "m2_conv_w", "m2_conv_b",
              "m2_dt_bias", "m2_a_log", "m2_d", "m2_norm_w", "sc_conv_w", "merge_b", "w_branch", "w_out")


def local_step(x, target, layers, final_norm_w):
    saved = []
    for p in layers:
        x, sv = layer_fwd(x, p)
        saved.append(sv)
    loss, dx, dfw = loss_head(x, final_norm_w, target)
    grads = []
    for p, sv in zip(reversed(layers), reversed(saved)):
        dx, g = layer_bwd(dx, p, sv)
        grads.append(g)
    return loss[0, 0], dx, grads[::-1], dfw.reshape(D_MODEL)


MESH = pl.DeviceIdType.MESH
ANY = pl.BlockSpec(memory_space=pl.ANY)


def _me():
    return lax.axis_index("x"), lax.axis_index("y"), lax.axis_index("c")


def _other_chips(x, y):
    return [(1 - x, y), (x, 1 - y), (1 - x, 1 - y)]


def _rcopy(src, dst, send, recv, dev):
    return pltpu.make_async_remote_copy(src_ref=src, dst_ref=dst, send_sem=send, recv_sem=recv,
                                        device_id=dev, device_id_type=MESH)


def _route_cut(rows, dtype):
    tile = 2 * SUBL * (4 // jnp.dtype(dtype).itemsize)
    return rows // 2 if rows % tile == 0 else rows


def _comm_call(name, body, arrs, out_shape, n_remote, aliases=None):
    n = len(arrs)
    return pl.pallas_call(
        body, name=name, in_specs=[ANY] * n, out_specs=[ANY] * len(out_shape), out_shape=out_shape,
        scratch_shapes=[pltpu.SemaphoreType.DMA((n, n_remote)), pltpu.SemaphoreType.DMA((n, n_remote))],
        input_output_aliases=aliases or {},
        compiler_params=pltpu.CompilerParams(has_side_effects=True),
    )(*arrs)


def gather_chips(name, arrs):
    n = len(arrs)
    cut = [_route_cut(a.shape[1], a.dtype) for a in arrs]

    def body(*refs):
        ins, outs = refs[:n], refs[n:2 * n]
        send, recv = refs[2 * n:]
        x, y, c = _me()
        jme, jx, jy, jd = 2 * x + y, 2 * (1 - x) + y, 2 * x + 1 - y, 2 * (1 - x) + 1 - y
        to_x, to_y, sib = (1 - x, y, c), (x, 1 - y, c), (x, y, 1 - c)

        def part(ref, a, hi):
            return ref.at[pl.ds(cut[a], ref.shape[0] - cut[a])] if hi else ref.at[pl.ds(0, cut[a])]

        def cp(a, k, ref, dev):
            return _rcopy(ref, ref, send.at[a, k], recv.at[a, k], dev)

        split = [a for a in range(n) if cut[a] < arrs[a].shape[1]]
        sent = [_rcopy(ins[a].at[c], outs[a].at[jme, c], send.at[a, k], recv.at[a, k], dev)
                for a in range(n) for k, dev in ((0, to_x), (1, to_y))]
        for s in sent:
            s.start()
        for a in range(n):
            blk = outs[a].at[jx, c]
            cp(a, 0, blk, to_x).wait_recv()
            sent += [cp(a, 2, part(blk, a, False), to_y), cp(a, 4, blk, sib)]
            sent[-2].start()
            sent[-1].start()
        for a in range(n):
            blk = outs[a].at[jy, c]
            cp(a, 1, blk, to_y).wait_recv()
            sent.append(cp(a, 5, blk, sib))
            sent[-1].start()
            if a in split:
                sent.append(cp(a, 3, part(blk, a, True), to_x))
                sent[-1].start()
        for a in range(n):
            lo = part(outs[a].at[jd, c], a, False)
            cp(a, 2, lo, to_y).wait_recv()
            sent.append(cp(a, 6, lo, sib))
            sent[-1].start()
        for a in split:
            hi = part(outs[a].at[jd, c], a, True)
            cp(a, 3, hi, to_x).wait_recv()
            sent.append(cp(a, 7, hi, sib))
            sent[-1].start()
        for a in range(n):
            cp(a, 4, outs[a].at[jx, 1 - c], sib).wait_recv()
            cp(a, 5, outs[a].at[jy, 1 - c], sib).wait_recv()
            cp(a, 6, part(outs[a].at[jd, 1 - c], a, False), sib).wait_recv()
        for a in split:
            cp(a, 7, part(outs[a].at[jd, 1 - c], a, True), sib).wait_recv()
        for s in sent:
            s.wait_send()

    out_shape = [jax.ShapeDtypeStruct((4,) + a.shape, a.dtype) for a in arrs]
    got = _comm_call(name, body, arrs, out_shape, 8)
    jme = 2 * lax.axis_index("x") + lax.axis_index("y")
    return [lax.dynamic_update_index_in_dim(g, a, jme, 0) for g, a in zip(got, arrs)]


def swap_halves(name, arrs):
    n = len(arrs)

    def body(*refs):
        ins, outs = refs[:n], refs[n:2 * n]
        send, recv = refs[2 * n:]
        x, y, c = _me()
        remote = [_rcopy(ins[a].at[1 - c], outs[a], send.at[a, 0], recv.at[a, 0], (x, y, 1 - c)) for a in range(n)]
        for cp in remote:
            cp.start()
        for cp in remote:
            cp.wait()

    return _comm_call(name, body, arrs, [jax.ShapeDtypeStruct(a.shape[1:], a.dtype) for a in arrs], 1)


def exchange_chips(name, arrs):
    n = len(arrs)

    def body(*refs):
        ins, outs = refs[:n], refs[n:2 * n]
        send, recv = refs[2 * n:]
        x, y, c = _me()
        remote = [_rcopy(ins[a].at[2 * cx + cy], outs[a].at[k], send.at[a, k], recv.at[a, k], (cx, cy, c))
                  for a in range(n) for k, (cx, cy) in enumerate(_other_chips(x, y))]
        for cp in remote:
            cp.start()
        for cp in remote:
            cp.wait()

    return _comm_call(name, body, arrs, [jax.ShapeDtypeStruct((3,) + a.shape[1:], a.dtype) for a in arrs], 3)


def gather_cores(name, arrs):
    n = len(arrs)

    def body(*refs):
        bufs = refs[n:2 * n]
        send, recv = refs[2 * n:]
        x, y, c = _me()
        remote = [_rcopy(bufs[a].at[c], bufs[a].at[c], send.at[a, 0], recv.at[a, 0], (x, y, 1 - c)) for a in range(n)]
        for cp in remote:
            cp.start()
        for a in range(n):
            _rcopy(bufs[a].at[1 - c], bufs[a].at[1 - c], send.at[a, 0], recv.at[a, 0], (x, y, 1 - c)).wait_recv()
        for cp in remote:
            cp.wait_send()

    return _comm_call(name, body, arrs, [jax.ShapeDtypeStruct(a.shape, a.dtype) for a in arrs], 1,
                      aliases={a: a for a in range(n)})


ROW_BLOCK = 256


def esum(name, terms, rows, width, out_dtype, out_slots=None):
    tr = next((t for t in range(min(rows, ROW_BLOCK), 0, -SUBL) if rows % t == 0 and t % SUBL == 0), rows)
    where =jnp.stack([lax.axis_index("c"), 2 * lax.axis_index("x") + lax.axis_index("y")]).astype(jnp.int32)
    pick = {"c": 0, "j": 1}

    def body(s_ref, *refs):
        acc = refs[0][...].astype(f32)
        for r in refs[1:-1]:
            acc = acc + r[...].astype(f32)
        refs[-1][...] = acc.astype(out_dtype)

    specs = []
    for arr, lead in terms:
        if lead is None:
            specs.append(pl.BlockSpec((tr, width), lambda i, s: (i, 0)))
        elif isinstance(lead, str):
            specs.append(pl.BlockSpec((None, tr, width), lambda i, s, lead=lead: (s[pick[lead]], i, 0)))
        else:
            specs.append(pl.BlockSpec((None, tr, width), lambda i, s, lead=lead: (lead, i, 0)))
    if out_slots is None:
        out_spec = pl.BlockSpec((tr, width), lambda i, s: (i, 0))
        out_shape = jax.ShapeDtypeStruct((rows, width), out_dtype)
    else:
        out_spec = pl.BlockSpec((None, tr, width), lambda i, s: (s[0], i, 0))
        out_shape = jax.ShapeDtypeStruct((out_slots, rows, width), out_dtype)
    return pl.pallas_call(
        body, name=name,
        grid_spec=pltpu.PrefetchScalarGridSpec(num_scalar_prefetch=1, grid=(rows // tr,), in_specs=specs, out_specs=out_spec),
        out_shape=out_shape, compiler_params=_cparams(1),
    )(where, *[t[0] for t in terms])


def reduce_to_shards(parts):
    tags = [str(k) for k in range(len(parts))]
    theirs = swap_halves("rs_swap", parts)
    t1 = []
    for tag, p, th in zip(tags, parts, theirs):
        _, _, h, w = p.shape
        t1.append(esum("rs_add_cores" + tag, [(p.reshape(2, 4 * h, w), "c"), (th.reshape(4 * h, w), None)],
                       4 * h, w, p.dtype).reshape(4, h, w))
    landed = exchange_chips("rs_exchange", t1)
    red = []
    for tag, p, t, got in zip(tags, parts, t1, landed):
        _, _, h, w = p.shape
        red.append(esum("rs_add_chips" + tag, [(t, "j"), (got, 0), (got, 1), (got, 2)], h, w, f32, out_slots=2))
    return gather_cores("rs_gather", red)


def _adamw_step(w_ref, g_ref, m_ref, v_ref, d_ref, nm_ref, nv_ref):
    gv = g_ref[...]
    nm = ADAM_B1 * m_ref[...] + (1.0 - ADAM_B1) * gv
    nv = ADAM_B2 * v_ref[...] + (1.0 - ADAM_B2) * jnp.square(gv)
    m_hat = nm / (1.0 - ADAM_B1 ** ADAM_STEP)
    v_hat = nv / (1.0 - ADAM_B2 ** ADAM_STEP)
    d_ref[...] = -ADAM_LR * (m_hat / (jnp.sqrt(v_hat) + ADAM_EPS) + ADAM_WD * w_ref[...])
    nm_ref[...] = nm
    nv_ref[...] = nv


def adamw(name, w, g, m, v, tr=None):
    rows, rest = w.shape[0], w.shape[1:]
    if tr is None:
        tr = ROW_BLOCK if rows % ROW_BLOCK == 0 else rows
    assert rows % tr == 0

    def body(*refs):
        _adamw_step(*refs)

    spec = pl.BlockSpec((tr,) + rest, lambda i: (i,) + (0,) * len(rest))
    return pl.pallas_call(
        body, name=name, grid=(rows // tr,), in_specs=[spec] * 4, out_specs=[spec] * 3,
        out_shape=[jax.ShapeDtypeStruct(w.shape, f32)] * 3, compiler_params=_cparams(1),
    )(w, g, m, v)


def adamw_many(name, ws, gs, ms, vs):
    n = len(ws)

    def body(*refs):
        ins, outs = refs[:4 * n], refs[4 * n:]
        for k in range(n):
            _adamw_step(ins[k], ins[n + k], ins[2 * n + k], ins[3 * n + k], outs[k], outs[n + k], outs[2 * n + k])

    vmem = pl.BlockSpec(memory_space=pltpu.VMEM)
    res = pl.pallas_call(
        body, name=name, in_specs=[vmem] * (4 * n), out_specs=[vmem] * (3 * n),
        out_shape=[jax.ShapeDtypeStruct(w.shape, f32) for w in ws] * 3,
        compiler_params=pltpu.CompilerParams(vmem_limit_bytes=VMEM_LIMIT),
    )(*ws, *gs, *ms, *vs)
    return res[:n], res[n:2 * n], res[2 * n:]


PACK_W = 1024


def _pack(parts, halves, row_mult):
    flat = jnp.concatenate([p.reshape(-1) for p in parts])
    per = halves * row_mult * PACK_W
    total = -(-flat.size // per) * per
    flat = jnp.pad(flat, (0, total - flat.size))
    return flat.reshape(halves, total // (halves * PACK_W), PACK_W)


def _unpack(flat, shapes):
    out, pos = [], 0
    for s in shapes:
        n = int(np.prod(s))
        out.append(flat[pos:pos + n].reshape(s))
        pos += n
    return out


SHARDED_BIG = ("w_in", "w_branch", "w_out", "s5_w_glu")
SHARDED_SMALL = ("m2_conv_w", "sc_conv_w", "merge_b")
SHARD_AXIS = {"w_in": 2, "w_branch": 3, "w_out": 1, "s5_w_glu": 1, "m2_conv_w": 2, "sc_conv_w": 2, "merge_b": 2}
REPLICATED = ("norm_w", "s5_lambda_re", "s5_lambda_im", "s5_b_re", "s5_b_im", "s5_c_re", "s5_c_im", "s5_d", "s5_log_step",
              "sgu_ln_w", "sgu_ln_b", "sgu_w", "sgu_b", "m2_conv_b", "m2_dt_bias", "m2_a_log", "m2_d", "m2_norm_w")
WEIGHTS = ("norm_w", "w_in", "s5_lambda_re", "s5_lambda_im", "s5_b_re", "s5_b_im", "s5_c_re", "s5_c_im", "s5_d",
           "s5_log_step", "s5_w_glu", "sgu_ln_w", "sgu_ln_b", "sgu_w", "sgu_b", "m2_conv_w", "m2_conv_b", "m2_dt_bias",
           "m2_a_log", "m2_d", "m2_norm_w", "sc_conv_w", "merge_b", "w_branch", "w_out", "final_norm_w")
N_LAYERS = 2


SHARD_W = IN_DIM // 4
SHARD_PAD = -(-SHARD_W // LANES) * LANES
REGROUP_W = 3 * LANES


def _kernel_pieces():
    out, pos = [], 0
    for s, n in _col_segments():
        while n:
            take = min(n, SHARD_W - s % SHARD_W)
            out.append((pos, s, take))
            pos, s, n = pos + take, s + take, n - take
    return out


def regroup_cols(name, src, steps, out_shape, out=None):
    n_src, rows, width = src.shape
    ow, win = REGROUP_W, REGROUP_W + LANES
    k_max = max(len(p) for _, _, p in steps)
    assert width % LANES == 0

    def body(*refs):
        src_ref, out_ref = refs[0], refs[-5]
        wbuf, obuf, sem_in, sem_out = refs[-4:]

        def fetch(q, slot):
            started = []
            for p, (s, col, lo, hi) in enumerate(steps[q][2]):
                w0 = col // LANES * LANES
                wlen = min(win, width - w0)
                cp = pltpu.make_async_copy(src_ref.at[s, :, pl.ds(w0, wlen)], wbuf.at[slot, p, :, pl.ds(0, wlen)], sem_in.at[slot, p])
                cp.start()
                started.append((cp, wlen, col - w0 - lo, lo, hi))
            return started

        pend, writes = fetch(0, 0), [None, None]
        for q, (t, b, _) in enumerate(steps):
            slot = q % 2
            nxt = fetch(q + 1, 1 - slot) if q + 1 < len(steps) else []
            acc = [jnp.zeros((rows, LANES), f32) for _ in range(ow // LANES)]
            for p, (cp, wlen, shift, lo, hi) in enumerate(pend):
                cp.wait()
                for k in range(ow // LANES):
                    c_lo, c_hi = max(lo, k * LANES), min(hi, (k + 1) * LANES)
                    if c_lo >= c_hi:
                        continue
                    wb = (c_lo + shift) // LANES
                    alen = min(2 * LANES, wlen - wb * LANES)
                    a = wbuf[slot, p, :, wb * LANES:wb * LANES + alen]
                    r, c = _rows((alen, LANES)), _cols((alen, LANES))
                    sh = shift + (k - wb) * LANES
                    sel = (r == c + sh) & (c >= c_lo - k * LANES) & (c < c_hi - k * LANES)
                    acc[k] = acc[k] + jnp.dot(a, sel.astype(bf16), preferred_element_type=f32)
            if writes[slot] is not None:
                writes[slot].wait()
            for k in range(ow // LANES):
                obuf[slot, :, k * LANES:(k + 1) * LANES] = acc[k].astype(bf16)
            writes[slot] = pltpu.make_async_copy(obuf.at[slot], out_ref.at[t, :, pl.ds(b * ow, ow)], sem_out.at[slot])
            writes[slot].start()
            pend = nxt
        for wr in writes:
            if wr is not None:
                wr.wait()

    operands, io_alias = [src], {}
    if out is not None:
        operands.append(out)
        io_alias = {1: 0}
    return pl.pallas_call(
        body, name=name, in_specs=[ANY] * len(operands), out_specs=ANY,
        out_shape=jax.ShapeDtypeStruct(out_shape, bf16), input_output_aliases=io_alias,
        scratch_shapes=[pltpu.VMEM((2, k_max, rows, win), bf16), pltpu.VMEM((2, rows, ow), bf16),
                        pltpu.SemaphoreType.DMA((2, k_max)), pltpu.SemaphoreType.DMA((2,))],
        compiler_params=pltpu.CompilerParams(vmem_limit_bytes=VMEM_LIMIT),
    )(*operands)


def _steps_to_kernel_cols(layer):
    steps = []
    for o in range(PW // REGROUP_W):
        pieces = []
        for pos, s, n in _kernel_pieces():
            lo, hi = max(pos, o * REGROUP_W), min(pos + n, (o + 1) * REGROUP_W)
            if lo < hi:
                ref_col = s + lo - pos
                pieces.append(((ref_col // SHARD_W) * N_LAYERS + layer, ref_col % SHARD_W, lo - o * REGROUP_W, hi - o * REGROUP_W))
        steps.append((0, o, pieces))
    return steps


def _steps_to_shards(layer):
    steps = []
    for j in range(4):
        for b in range(SHARD_PAD // REGROUP_W):
            start, stop = j * SHARD_W + b * REGROUP_W, min(j * SHARD_W + (b + 1) * REGROUP_W, (j + 1) * SHARD_W)
            pieces = []
            for pos, s, n in _kernel_pieces():
                lo, hi = max(s, start), min(s + n, stop)
                if lo < hi:
                    pieces.append((0, pos + lo - s, lo - start, hi - start))
            steps.append((layer * 4 + j, b, pieces))
    return steps


def _gather_weights(w):
    w_in = jnp.pad(w["w_in"].astype(bf16), ((0, 0), (0, 0), (0, SHARD_PAD - SHARD_W)))
    arrs = [w_in, w["w_branch"].reshape(N_LAYERS, N_BRANCH * BW, -1).astype(bf16),
            w["w_out"].astype(bf16), w["s5_w_glu"].astype(bf16), w["m2_conv_w"], w["sc_conv_w"], w["merge_b"]]
    got = gather_chips("ag_weights", arrs)
    cols = lambda t: jnp.transpose(t, (1, 0, 2)).reshape(t.shape[1], -1)
    layers = []
    for i in range(N_LAYERS):
        p = {k: w[k][i] for k in REPLICATED}
        p["w_in"] = regroup_cols(f"w_in_cols{i}", got[0].reshape(4 * N_LAYERS, D_MODEL, SHARD_PAD), _steps_to_kernel_cols(i),
                                 (1, D_MODEL, PW))[0]
        p["w_branch"] = cols(got[1][:, i])
        p["w_out"] = got[2][:, i].reshape(D_MODEL, D_MODEL)
        p["s5_w_glu"] = got[3][:, i].reshape(BW, BW)
        p["m2_conv_w"], p["sc_conv_w"], p["merge_b"] = cols(got[4][:, i]), cols(got[5][:, i]), cols(got[6][:, i])
        layers.append(p)
    return layers


def _reduce_grads(grads, d_final, w):
    to_chips = lambda t: jnp.transpose(t.reshape(t.shape[0], 4, -1), (1, 0, 2))
    stack = lambda f: jnp.stack([f(g) for g in grads])
    dw_in = None
    for i, g in enumerate(grads):
        dw_in = regroup_cols(f"w_in_shards{i}", g["w_in"][None], _steps_to_shards(i), (4 * N_LAYERS, D_MODEL, SHARD_PAD), out=dw_in)
    parts = [dw_in.reshape(N_LAYERS, 4, D_MODEL, SHARD_PAD),
             stack(lambda g: to_chips(g["w_branch"].reshape(N_BRANCH * BW, D_MODEL)).astype(bf16)),
             stack(lambda g: g["w_out"].reshape(4, D_MODEL // 4, D_MODEL).astype(bf16)),
             stack(lambda g: g["s5_w_glu"].reshape(4, BW // 4, BW).astype(bf16))]
    rep = jnp.concatenate([stack(lambda g: g[k]).reshape(-1) for k in REPLICATED] + [d_final.reshape(-1)])
    quarter = -(-rep.size // (4 * 2 * SUBL * PACK_W)) * (2 * SUBL * PACK_W)
    rep = jnp.pad(rep, (0, 4 * quarter - rep.size))
    small = []
    for j in range(4):
        sharded = [stack(lambda g: to_chips(g[k])[j]) for k in SHARDED_SMALL]
        small.append(_pack(sharded + [rep[j * quarter:(j + 1) * quarter]], 2, SUBL))
    parts.append(jnp.stack(small, axis=1))
    red = reduce_to_shards(parts)
    out = {"w_in": red[0][:, :, :SHARD_W], "w_branch": red[1].reshape(w["w_branch"].shape), "w_out": red[2], "s5_w_glu": red[3]}
    small_flat = red[4].reshape(-1)
    n_small = sum(int(np.prod(w[k].shape)) for k in SHARDED_SMALL)
    out.update(zip(SHARDED_SMALL, _unpack(small_flat, [w[k].shape for k in SHARDED_SMALL])))
    mine = small_flat[n_small:n_small + quarter].reshape(2, quarter // (2 * PACK_W), PACK_W)
    rep_all = gather_chips("ag_small_grads", [mine])[0].reshape(-1)
    names = REPLICATED + ("final_norm_w",)
    out.update(zip(names, _unpack(rep_all, [w[k].shape for k in names])))
    return out


def _update(w, g, m, v):
    d, nm, nv = {}, {}, {}
    flat2 = lambda a: a.reshape(-1, a.shape[-1])
    cols_major = lambda a: jnp.transpose(a, (2, 0, 1))
    res = adamw("adamw_w_in", *[cols_major(t["w_in"]) for t in (w, g, m, v)], tr=SHARD_W // 14)
    d["w_in"], nm["w_in"], nv["w_in"] = (jnp.transpose(r, (1, 2, 0)) for r in res)
    for k in SHARDED_BIG[1:]:
        res = adamw("adamw_" + k, *[flat2(t[k]) for t in (w, g, m, v)])
        d[k], nm[k], nv[k] = (r.reshape(w[k].shape) for r in res)
    for k in ("s5_b_re", "s5_b_im"):
        res = adamw("adamw_" + k, *[flat2(t[k]) for t in (w, g, m, v)])
        d[k], nm[k], nv[k] = (r.reshape(w[k].shape) for r in res)
    rest = [k for k in WEIGHTS if k not in SHARDED_BIG + ("s5_b_re", "s5_b_im")]
    two_d = lambda a: a.reshape(1, -1) if a.ndim == 1 else a
    res = adamw_many("adamw_rest", *[[two_d(t[k]) for k in rest] for t in (w, g, m, v)])
    for tgt, rs in zip((d, nm, nv), res):
        tgt.update({k: r.reshape(w[k].shape) for k, r in zip(rest, rs)})
    return d, nm, nv


def kernel(x, norm_w, w_in, s5_lambda_re, s5_lambda_im, s5_b_re, s5_b_im, s5_c_re, s5_c_im, s5_d, s5_log_step, s5_w_glu, sgu_ln_w, sgu_ln_b, sgu_w, sgu_b, m2_conv_w, m2_conv_b, m2_dt_bias, m2_a_log, m2_d, m2_norm_w, sc_conv_w, merge_b, w_branch, w_out, final_norm_w, loss_target, m_norm_w, m_w_in, m_s5_lambda_re, m_s5_lambda_im, m_s5_b_re, m_s5_b_im, m_s5_c_re, m_s5_c_im, m_s5_d, m_s5_log_step, m_s5_w_glu, m_sgu_ln_w, m_sgu_ln_b, m_sgu_w, m_sgu_b, m_m2_conv_w, m_m2_conv_b, m_m2_dt_bias, m_m2_a_log, m_m2_d, m_m2_norm_w, m_sc_conv_w, m_merge_b, m_w_branch, m_w_out, m_final_norm_w, v_norm_w, v_w_in, v_s5_lambda_re, v_s5_lambda_im, v_s5_b_re, v_s5_b_im, v_s5_c_re, v_s5_c_im, v_s5_d, v_s5_log_step, v_s5_w_glu, v_sgu_ln_w, v_sgu_ln_b, v_sgu_w, v_sgu_b, v_m2_conv_w, v_m2_conv_b, v_m2_dt_bias, v_m2_a_log, v_m2_d, v_m2_norm_w, v_sc_conv_w, v_merge_b, v_w_branch, v_w_out, v_final_norm_w):
    given = dict(locals())
    w = {k: given[k] for k in WEIGHTS}
    m = {k: given["m_" + k] for k in WEIGHTS}
    v = {k: given["v_" + k] for k in WEIGHTS}
    layers = _gather_weights(w)
    loss, dx, grads, d_final = local_step(x[0], loss_target[0], layers, final_norm_w)
    loss = lax.psum(loss, ("x", "y", "c"))
    g = _reduce_grads(grads, d_final, w)
    d, nm, nv = _update(w, g, m, v)
    return (loss, dx[None], *[g[k] for k in WEIGHTS], *[d[k] for k in WEIGHTS],
            *[nm[k] for k in WEIGHTS], *[nv[k] for k in WEIGHTS])
```

```python
import functools
from typing import Any, Callable, NamedTuple

import numpy as np
import jax
import jax.numpy as jnp
from jax import lax
from jax.experimental import pallas as pl
from jax.experimental.pallas import tpu as pltpu

f32 = jnp.float32
bf16 = jnp.bfloat16

D_MODEL = 1024
BW = 512
N_BRANCH = 4
EPS = 1e-6
S5_GROUPS, S5_P, S5_N = 32, 16, 64
S5_NS = S5_GROUPS * S5_N
CHUNK = 128
M2_HEADS, M2_HEAD_DIM, M2_GROUPS, M2_STATE = 8, 64, 2, 128
IN_DIM = 10248
PW = 10368
PW_MAIN = 10240
LANES = 128
VMEM_LIMIT = 60 * 1024 * 1024

ADAM_LR, ADAM_B1, ADAM_B2, ADAM_EPS, ADAM_WD, ADAM_STEP = 0.001, 0.9, 0.999, 1e-08, 0.01, 10

C_MERGE = 0
C_SC = 4096
C_SGU = 6144
C_S5G = 7680
C_S5U = 8192
C_M2Z = 8704
C_XBC = 9216
C_DT = 10240


def _col_segments():
    segs = [(6152, 4096)]
    for j in range(4):
        segs += [(4104 + 128 * j, 128), (4616 + 128 * j, 128), (5128 + 128 * j, 128), (5640 + 128 * j, 128)]
    segs += [(1024, 1536), (512, 512), (0, 512), (2560, 512), (3072, 1024), (4096, 8)]
    return segs


def _to_kernel_cols(w):
    parts = [w[:, s:s + n] for s, n in _col_segments()]
    parts.append(jnp.zeros((w.shape[0], PW - IN_DIM), w.dtype))
    return jnp.concatenate(parts, axis=1)


def _from_kernel_cols(wp):
    out, pos = {}, 0
    for s, n in _col_segments():
        out[s] = wp[:, pos:pos + n]
        pos += n
    return jnp.concatenate([out[s] for s in sorted(out)], axis=1)


NN = ((1,), (0,))
NT = ((1,), (1,))
TN = ((0,), (0,))


def _bd(a, b, dims):
    return lax.dot_general(a.astype(bf16), b.astype(bf16), (dims, ((), ())), preferred_element_type=f32)


def _hd(a, b, dims):
    return lax.dot_general(a, b, (dims, ((), ())), precision=lax.Precision.HIGHEST, preferred_element_type=f32)


def _make_dots(raw):
    @jax.custom_vjp
    def nn(a, b):
        return raw(a, b, NN)
    nn.defvjp(lambda a, b: (raw(a, b, NN), (a, b)), lambda r, g: (raw(g, r[1], NT), raw(r[0], g, TN)))

    @jax.custom_vjp
    def nt(a, b):
        return raw(a, b, NT)
    nt.defvjp(lambda a, b: (raw(a, b, NT), (a, b)), lambda r, g: (raw(g, r[1], NN), raw(g, r[0], TN)))

    @jax.custom_vjp
    def tn(a, b):
        return raw(a, b, TN)
    tn.defvjp(lambda a, b: (raw(a, b, TN), (a, b)), lambda r, g: (raw(r[1], g, NT), raw(r[0], g, NN)))
    return nn, nt, tn


bdot, bdot_nt, bdot_tn = _make_dots(_bd)
hdot, hdot_nt, hdot_tn = _make_dots(_hd)


@jax.custom_vjp
def bdot_w(a, w, shadow):
    return _bd(a, w, NN)


bdot_w.defvjp(lambda a, w, s: (_bd(a, w, NN), (a, w)),
              lambda r, g: (_bd(g, r[1], NT), jnp.zeros_like(r[1]), _bd(r[0], g, TN)))


def _rows(shape):
    return lax.broadcasted_iota(jnp.int32, shape, 0)


def _cols(shape):
    return lax.broadcasted_iota(jnp.int32, shape, 1)


def _shift_down(x, s):
    return jnp.where(_rows(x.shape) < s, 0.0, pltpu.roll(x, s, 0))


def _shift_up(x, s):
    n = x.shape[0]
    return jnp.where(_rows(x.shape) >= n - s, 0.0, pltpu.roll(x, n - s, 0))


@functools.partial(jax.custom_vjp, nondiff_argnums=(1,))
def shift(x, s):
    return _shift_down(x, s) if s else x


shift.defvjp(lambda x, s: (shift(x, s), None), lambda s, _, g: (_shift_up(g, s) if s else g,))


def _row_of(w, k):
    return jnp.sum(jnp.where(_rows(w.shape) == k, w, 0.0), axis=0, keepdims=True)


def _lane_mask(width, lo, hi):
    c = _cols((1, width))
    return ((c >= lo) & (c < hi)).astype(f32)


def _expand(rows, width, per):
    return (_cols((rows, width)) // per == _rows((rows, width))).astype(f32)


class A(NamedTuple):
    arr: Any
    block: tuple
    imap: Callable
    shadow: bool = False


class O(NamedTuple):
    shape: tuple
    dtype: Any
    block: tuple
    imap: Callable
    alias: Any = None


class R(NamedTuple):
    arg: int
    out: int
    off: Any = None
    acc: bool = False


def _cparams(n_grid):
    return pltpu.CompilerParams(dimension_semantics=("arbitrary",) * n_grid, vmem_limit_bytes=VMEM_LIMIT)


def _ispec(block, imap, n, rev):
    if rev:
        return pl.BlockSpec(block, lambda i: imap(n - 1 - i))
    return pl.BlockSpec(block, imap)


def _load(ref, a):
    v = ref[...]
    if a.shadow:
        return (v, jnp.zeros(v.shape, f32))
    return v.astype(f32)


def _save_spec(shape, n, rev):
    nd = len(shape)
    return _ispec((None,) + tuple(shape), lambda i: (i,) + (0,) * nd, n, rev)


def block_fwd(name, f, n, args, outs, carries=()):
    n_in, n_out, n_c = len(args), len(outs), len(carries)

    def body(*refs):
        ins, out_r = refs[:n_in], refs[n_in:n_in + n_out]
        saves, cs = refs[n_in + n_out:n_in + n_out + n_c], refs[n_in + n_out + n_c:]
        if n_c:
            @pl.when(pl.program_id(0) == 0)
            def _():
                for c in cs:
                    c[...] = jnp.zeros(c.shape, f32)
        vals = [_load(r, a) for r, a in zip(ins, args)]
        cv = [c[...] for c in cs]
        for s, v in zip(saves, cv):
            s[...] = v
        res = f(*vals, *cv)
        for r, v in zip(out_r, res[:n_out]):
            r[...] = v.astype(r.dtype)
        for c, v in zip(cs, res[n_out:]):
            c[...] = v

    out_shape = [jax.ShapeDtypeStruct(o.shape, o.dtype) for o in outs]
    out_specs = [pl.BlockSpec(o.block, o.imap) for o in outs]
    for shp in carries:
        out_shape.append(jax.ShapeDtypeStruct((n,) + tuple(shp), f32))
        out_specs.append(_save_spec(shp, n, False))
    return pl.pallas_call(
        body, name=name, grid=(n,),
        in_specs=[pl.BlockSpec(a.block, a.imap) for a in args],
        out_specs=out_specs, out_shape=out_shape,
        scratch_shapes=[pltpu.VMEM(tuple(shp), f32) for shp in carries],
        compiler_params=_cparams(1),
    )(*[a.arr for a in args])


def block_bwd(name, f, n, args, cots, gouts, routes, saved=(), rev=False):
    n_in, n_cot, n_c, n_go = len(args), len(cots), len(saved), len(gouts)
    diff = []
    for r in routes:
        if r.arg not in diff:
            diff.append(r.arg)
    aliases = [(k, o.alias) for k, o in enumerate(gouts) if o.alias is not None]

    def body(*refs):
        ins = refs[:n_in]
        cot_r = refs[n_in:n_in + n_cot]
        sav_r = refs[n_in + n_cot:n_in + n_cot + n_c]
        base = n_in + n_cot + n_c + len(aliases)
        go_r = refs[base:base + n_go]
        dcs = refs[base + n_go:]
        step = pl.program_id(0)
        if n_c:
            @pl.when(step == 0)
            def _():
                for d in dcs:
                    d[...] = jnp.zeros(d.shape, f32)
        vals = [_load(r, a) for r, a in zip(ins, args)]
        cv = [s[...] for s in sav_r]
        nd = len(diff)

        def g(*dv):
            full = list(vals)
            for idx, v in zip(diff, dv[:nd]):
                full[idx] = (vals[idx][0], v) if args[idx].shadow else v
            return tuple(f(*full, *dv[nd:]))

        primals = [vals[i][1] if args[i].shadow else vals[i] for i in diff] + cv
        _, vjp_fn = jax.vjp(g, *primals)
        ct = tuple([r[...].astype(f32) for r in cot_r] + [d[...] for d in dcs])
        grads = vjp_fn(ct)
        for r in routes:
            gr = grads[diff.index(r.arg)]
            ref = go_r[r.out]
            if r.acc:
                @pl.when(step == 0)
                def _(ref=ref, gr=gr):
                    ref[...] = gr.astype(ref.dtype)

                @pl.when(step > 0)
                def _(ref=ref, gr=gr):
                    ref[...] += gr.astype(ref.dtype)
            elif r.off is None:
                ref[...] = gr.astype(ref.dtype)
            else:
                ref[:, r.off:r.off + gr.shape[1]] = gr.astype(ref.dtype)
        for d, gr in zip(dcs, grads[nd:]):
            d[...] = gr

    in_specs = [_ispec(a.block, a.imap, n, rev) for a in list(args) + list(cots)]
    in_specs += [_save_spec(s.shape[1:], n, rev) for s in saved]
    in_specs += [pl.BlockSpec(memory_space=pl.ANY) for _ in aliases]
    operands = [a.arr for a in list(args) + list(cots)] + list(saved) + [arr for _, arr in aliases]
    io_alias = {n_in + n_cot + n_c + j: k for j, (k, _) in enumerate(aliases)}
    return pl.pallas_call(
        body, name=name, grid=(n,),
        in_specs=in_specs,
        out_specs=[_ispec(o.block, o.imap, n, rev) for o in gouts],
        out_shape=[jax.ShapeDtypeStruct(o.shape, o.dtype) for o in gouts],
        scratch_shapes=[pltpu.VMEM(tuple(s.shape[1:]), f32) for s in saved],
        input_output_aliases=io_alias,
        compiler_params=_cparams(1),
    )(*operands)


class Op(NamedTuple):
    arr: Any
    row: int = 0
    col: int = 0


def mm(name, mode, pairs, m, n, tm, tn, out_dtype=f32, add=None, out=None, out_col=0, out_width=None):
    tm, tn = min(tm, m), min(tn, n)
    assert m % tm == 0 and n % tn == 0
    in_specs, operands = [], []
    for a, b, k, _ in pairs:
        if mode == TN:
            assert a.row % k == 0 and a.col % tm == 0
            in_specs.append(pl.BlockSpec((k, tm), lambda j, i, a=a, k=k: (a.row // k, i + a.col // tm)))
        else:
            assert a.col % k == 0 and a.row % tm == 0
            in_specs.append(pl.BlockSpec((tm, k), lambda j, i, a=a, k=k: (i + a.row // tm, a.col // k)))
        if mode == NT:
            assert b.col % k == 0 and b.row % tn == 0
            in_specs.append(pl.BlockSpec((tn, k), lambda j, i, b=b, k=k: (j + b.row // tn, b.col // k)))
        else:
            assert b.row % k == 0 and b.col % tn == 0
            in_specs.append(pl.BlockSpec((k, tn), lambda j, i, b=b, k=k: (b.row // k, j + b.col // tn)))
        operands += [a.arr, b.arr]
    n_p = len(pairs)
    if add is not None:
        assert add.col % tn == 0
        in_specs.append(pl.BlockSpec((tm, tn), lambda j, i: (i, j + add.col // tn)))
        operands.append(add.arr)
    io_alias = {}
    if out is not None:
        assert out_col % tn == 0
        in_specs.append(pl.BlockSpec(memory_space=pl.ANY))
        operands.append(out)
        io_alias = {len(operands) - 1: 0}
        out_shape = jax.ShapeDtypeStruct(out.shape, out.dtype)
    else:
        out_shape = jax.ShapeDtypeStruct((m, out_width or n), out_dtype)
    signs = [p[3] for p in pairs]

    def body(*refs):
        o = refs[-1]
        acc = None
        for p in range(n_p):
            t = _bd(refs[2 * p][...], refs[2 * p + 1][...], mode)
            t = t if signs[p] > 0 else -t
            acc = t if acc is None else acc + t
        if add is not None:
            acc = acc + refs[2 * n_p][...].astype(f32)
        o[...] = acc.astype(o.dtype)

    return pl.pallas_call(
        body, name=name, grid=(n // tn, m // tm),
        in_specs=in_specs,
        out_specs=pl.BlockSpec((tm, tn), lambda j, i: (i, j + out_col // tn)),
        out_shape=out_shape, input_output_aliases=io_alias,
        compiler_params=_cparams(2),
    )(*operands)


SCAN_LANES = 512
SUBL = 8


def _cmul(p, q):
    return (p[0] * q[0] - p[1] * q[1], p[0] * q[1] + p[1] * q[0])


def _powers(a):
    a2 = _cmul(a, a)
    a4 = _cmul(a2, a2)
    a6 = _cmul(a4, a2)
    return [a, a2, _cmul(a2, a), a4, _cmul(a4, a), a6, _cmul(a6, a), _cmul(a4, a4)]


def _table(pw, order, w):
    row = _rows((SUBL, w))
    re = sum(jnp.where(row == t, pw[k][0], 0.0) for t, k in enumerate(order))
    im = sum(jnp.where(row == t, pw[k][1], 0.0) for t, k in enumerate(order))
    return re, im


def _pick_row(x, t):
    return jnp.sum(jnp.where(_rows(x.shape) == t, x, 0.0), axis=0, keepdims=True)


def s5_scan_fwd(bu_re, bu_im, a_re, a_im):
    seq, ns = bu_re.shape
    w, nb = SCAN_LANES, ns // SCAN_LANES

    def body(b_re, b_im, ar, ai, s_re, s_im):
        a = (ar[...], ai[...])
        pw = _powers(a)
        tab = _table(pw, list(range(SUBL)), w)
        row = _rows((SUBL, w))

        def step(i, carry):
            t0 = pl.multiple_of(i * SUBL, SUBL)
            x = (b_re[pl.ds(t0, SUBL), :], b_im[pl.ds(t0, SUBL), :])
            for d, k in ((1, 0), (2, 1), (4, 3)):
                sh = (jnp.where(row < d, 0.0, pltpu.roll(x[0], d, 0)), jnp.where(row < d, 0.0, pltpu.roll(x[1], d, 0)))
                t = _cmul(pw[k], sh)
                x = (x[0] + t[0], x[1] + t[1])
            t = _cmul(tab, carry)
            x = (x[0] + t[0], x[1] + t[1])
            s_re[pl.ds(t0, SUBL), :] = x[0]
            s_im[pl.ds(t0, SUBL), :] = x[1]
            return (x[0][SUBL - 1:, :], x[1][SUBL - 1:, :])

        z = jnp.zeros((1, w), f32)
        lax.fori_loop(0, seq // SUBL, step, (z, z), unroll=2)

    strip = pl.BlockSpec((seq, w), lambda j: (0, j))
    lane = pl.BlockSpec((1, w), lambda j: (0, j))
    return pl.pallas_call(
        body, name="s5_scan_fwd", grid=(nb,),
        in_specs=[strip, strip, lane, lane],
        out_specs=[strip, strip],
        out_shape=[jax.ShapeDtypeStruct((seq, ns), f32)] * 2,
        compiler_params=_cparams(1),
    )(bu_re, bu_im, a_re, a_im)


def s5_scan_bwd(ds_re, ds_im, s_re, s_im, a_re, a_im):
    seq, ns = ds_re.shape
    w, nb = SCAN_LANES, ns // SCAN_LANES
    nblk = seq // SUBL

    def body(g_re, g_im, sr, si, ar, ai, l_re, l_im, da_re, da_im):
        a = (ar[...], -ai[...])
        pw = _powers(a)
        tab = _table(pw, [SUBL - 1 - t for t in range(SUBL)], w)
        row = _rows((SUBL, w))

        def step(kk, carry):
            c_re, c_im, acc_re, acc_im = carry
            i = nblk - 1 - kk
            t0 = pl.multiple_of(i * SUBL, SUBL)
            x = (g_re[pl.ds(t0, SUBL), :], g_im[pl.ds(t0, SUBL), :])
            for d, k in ((1, 0), (2, 1), (4, 3)):
                sh = (jnp.where(row >= SUBL - d, 0.0, pltpu.roll(x[0], SUBL - d, 0)),
                      jnp.where(row >= SUBL - d, 0.0, pltpu.roll(x[1], SUBL - d, 0)))
                t = _cmul(pw[k], sh)
                x = (x[0] + t[0], x[1] + t[1])
            t = _cmul(tab, (c_re, c_im))
            x = (x[0] + t[0], x[1] + t[1])
            l_re[pl.ds(t0, SUBL), :] = x[0]
            l_im[pl.ds(t0, SUBL), :] = x[1]
            tp = jnp.maximum(t0 - 1, 0)
            live = (i > 0).astype(f32)
            p_re = sr[pl.ds(tp, 1), :] * live
            p_im = si[pl.ds(tp, 1), :] * live
            sp_re = jnp.where(row == 0, p_re, pltpu.roll(sr[pl.ds(t0, SUBL), :], 1, 0))
            sp_im = jnp.where(row == 0, p_im, pltpu.roll(si[pl.ds(t0, SUBL), :], 1, 0))
            acc_re = acc_re + x[0] * sp_re + x[1] * sp_im
            acc_im = acc_im + x[1] * sp_re - x[0] * sp_im
            return (x[0][:1, :], x[1][:1, :], acc_re, acc_im)

        z1 = jnp.zeros((1, w), f32)
        z8 = jnp.zeros((SUBL, w), f32)
        _, _, acc_re, acc_im = lax.fori_loop(0, nblk, step, (z1, z1, z8, z8), unroll=2)
        da_re[...] = jnp.sum(acc_re, axis=0, keepdims=True)
        da_im[...] = jnp.sum(acc_im, axis=0, keepdims=True)

    strip = pl.BlockSpec((seq, w), lambda j: (0, j))
    lane = pl.BlockSpec((1, w), lambda j: (0, j))
    return pl.pallas_call(
        body, name="s5_scan_bwd", grid=(nb,),
        in_specs=[strip, strip, strip, strip, lane, lane],
        out_specs=[strip, strip, lane, lane],
        out_shape=[jax.ShapeDtypeStruct((seq, ns), f32)] * 2 + [jax.ShapeDtypeStruct((1, ns), f32)] * 2,
        compiler_params=_cparams(1),
    )(ds_re, ds_im, s_re, s_im, a_re, a_im)


def _rms(x, w):
    return x * lax.rsqrt(jnp.mean(x * x, axis=-1, keepdims=True) + EPS) * w


def f_rms(x, w):
    return (_rms(x, w),)


def f_rms_res(x, w):
    return (_rms(x, w), x)


def f_s5_prep(lam_re, lam_im, log_step, b_re, b_im):
    e = _expand(log_step.shape[1], S5_NS, S5_N)
    step = hdot(jnp.exp(log_step), e)
    mag = jnp.exp(lam_re * step)
    ab_re, ab_im = mag * jnp.cos(lam_im * step), mag * jnp.sin(lam_im * step)
    den = lam_re * lam_re + lam_im * lam_im
    nr = ab_re - 1.0
    coef_re = (nr * lam_re + ab_im * lam_im) / den
    coef_im = (ab_im * lam_re - nr * lam_im) / den
    bb_re, bb_im = coef_re * b_re - coef_im * b_im, coef_re * b_im + coef_im * b_re
    sel = (_rows((BW, S5_P)) % S5_P == _cols((BW, S5_P))).astype(f32)
    blk = _rows((BW, S5_NS)) // S5_P == _cols((BW, S5_NS)) // S5_N
    rows_bd = lambda t: jnp.where(blk, hdot(sel, t), 0.0)
    return (ab_re, ab_im, rows_bd(bb_re), rows_bd(bb_im))


def f_s5_c(c_re, c_im):
    sel_t = (_cols((S5_P, BW)) % S5_P == _rows((S5_P, BW))).astype(f32)
    blk_t = _rows((S5_NS, BW)) // S5_N == _cols((S5_NS, BW)) // S5_P
    cols_bd = lambda t: jnp.where(blk_t, hdot_tn(t, sel_t), 0.0)
    return (cols_bd(c_re), cols_bd(c_im))


def f_s5_act(y_lin, u, gate, d, w_glu):
    y = jax.nn.gelu(y_lin + d * u)
    y = y * jax.nn.sigmoid(bdot_w(y, *w_glu))
    return (y * jax.nn.silu(gate),)


def f_sgu(u, v, gate, ln_w, ln_b, *rest):
    w_s, b_pad = rest[:8], rest[8]
    t = u.shape[0]
    u32, v32 = jax.nn.gelu(u), jax.nn.gelu(v)
    mu = jnp.mean(v32, axis=-1, keepdims=True)
    var = jnp.mean(jnp.square(v32 - mu), axis=-1, keepdims=True)
    vn = (v32 - mu) * lax.rsqrt(var + EPS) * ln_w + ln_b
    tri = _rows((t, t)) >= _cols((t, t))
    s = hdot_tn(b_pad, _expand(LANES, BW, BW // 8))
    for h in range(8):
        s = s + bdot(jnp.where(tri, w_s[h], 0.0), vn) * _lane_mask(BW, 64 * h, 64 * h + 64)
    return (u32 * s * jax.nn.silu(gate),)


def f_m2_conv(x, w, b):
    return (sum(_row_of(w, k) * shift(x, 3 - k) for k in range(4)) + b,)


def f_sc(bg, cg, h, gate, w):
    z = cg * h
    conv = sum(_row_of(w, k) * shift(z, 2 - k) for k in range(3))
    return (bg * conv * jax.nn.silu(gate),)


def f_m2(z, xc, b0, b1, c0, c1, dt_raw, dt_bias, a_log, d_par, norm_w, st):
    q = z.shape[0]
    x = jax.nn.silu(xc)
    bm, cm = (jax.nn.silu(b0), jax.nn.silu(b1)), (jax.nn.silu(c0), jax.nn.silu(c1))
    dt = jax.nn.softplus(dt_raw + dt_bias)
    da = dt * (-jnp.exp(a_log))
    tri = _rows((q, q)) >= _cols((q, q))
    acs = hdot(tri.astype(f32), da)
    e = _expand(LANES, BW, M2_HEAD_DIM)
    dt_f, acs_f = hdot(dt, e), hdot(acs, e)
    last = _rows((q, BW)) == q - 1
    alast_f = jnp.sum(jnp.where(last, acs_f, 0.0), axis=0, keepdims=True)
    xdt = x * dt_f
    xdec = xdt * jnp.exp(alast_f - acs_f)
    acs_t = acs.T
    st_new = st * jnp.exp(alast_f)
    y_diag, y_off = 0.0, 0.0
    for g in range(M2_GROUPS):
        gm = _lane_mask(BW, 256 * g, 256 * g + 256)
        cb = bdot_nt(cm[g], bm[g])
        st_new = st_new + bdot_tn(bm[g], xdec * gm)
        y_off = y_off + bdot(cm[g], st) * gm
        for hh in range(M2_HEADS // M2_GROUPS):
            h = g * (M2_HEADS // M2_GROUPS) + hh
            col = jnp.sum(jnp.where(_cols((q, LANES)) == h, acs, 0.0), axis=1, keepdims=True)
            row = jnp.sum(jnp.where(_rows((LANES, q)) == h, acs_t, 0.0), axis=0, keepdims=True)
            decay = jnp.exp(jnp.where(tri, col - row, -1e30))
            y_diag = y_diag + bdot(cb * decay, xdt) * _lane_mask(BW, 64 * h, 64 * h + 64)
    d_f = sum(jnp.sum(jnp.where(_cols((1, LANES)) == h, d_par, 0.0), axis=1, keepdims=True)
              * _lane_mask(BW, 64 * h, 64 * h + 64) for h in range(M2_HEADS))
    y = y_diag + y_off * jnp.exp(acs_f) + d_f * x
    y = y * jax.nn.silu(z)
    return (_rms(y, norm_w), st_new)


def f_branch_mix(*v):
    ys, lg, wb, mb = v[0:4], v[4:8], v[8:12], v[12:16]
    return (sum(jax.nn.sigmoid(lg[k] + mb[k]) * bdot_w(ys[k], *wb[k]) for k in range(N_BRANCH)),)


def _full(shape):
    nd = len(shape)
    return dict(block=tuple(shape), imap=lambda i: (0,) * nd)


def _param(arr, shadow=False):
    return A(arr, tuple(arr.shape), lambda i, nd=arr.ndim: (0,) * nd, shadow)


def _tb(arr, t, width, colblk):
    return A(arr, (t, width), lambda i: (i, colblk))


def _strip(arr, seq, colblk0, stride=1):
    return A(arr, (seq, LANES), lambda j: (0, colblk0 + stride * j))


def _s5_prep_args(p):
    lam_re = p["s5_lambda_re"].reshape(1, S5_NS)
    lam_im = p["s5_lambda_im"].reshape(1, S5_NS)
    log_step = jnp.pad(p["s5_log_step"].reshape(1, S5_GROUPS), ((0, 0), (0, LANES - S5_GROUPS)))
    b_lanes = lambda b: jnp.transpose(b, (2, 0, 1)).reshape(S5_P, S5_NS)
    return [_param(v) for v in (lam_re, lam_im, log_step, b_lanes(p["s5_b_re"]), b_lanes(p["s5_b_im"]))]


def _s5_c_args(p):
    c_lanes = lambda c: jnp.transpose(c, (1, 0, 2)).reshape(S5_P, S5_NS)
    return [_param(c_lanes(p["s5_c_re"])), _param(c_lanes(p["s5_c_im"]))]


def layer_fwd(x, p):
    seq = x.shape[0]
    nt = seq // CHUNK
    t2 = 256
    sv = {}
    hb = block_fwd("rms_in", f_rms, seq // t2, [_tb(x, t2, D_MODEL, 0), _param(p["norm_w"].reshape(1, D_MODEL))],
                   [O((seq, D_MODEL), bf16, (t2, D_MODEL), lambda i: (i, 0))])[0]
    proj = mm("mm_in", NN, [(Op(hb), Op(p["w_in"]), D_MODEL, 1)], seq, PW, 1024, 1152)
    whole = lambda shape, dt: O(shape, dt, shape, lambda i: (0, 0))
    ab_re, ab_im, bb_re, bb_im = block_fwd("s5_prep", f_s5_prep, 1, _s5_prep_args(p),
                                           [whole((1, S5_NS), f32)] * 2 + [whole((BW, S5_NS), bf16)] * 2)
    cc_re, cc_im = block_fwd("s5_c", f_s5_c, 1, _s5_c_args(p), [whole((S5_NS, BW), bf16)] * 2)
    bu_re = mm("mm_bu_re", NN, [(Op(proj, 0, C_S5U), Op(bb_re), BW, 1)], seq, S5_NS, 512, 1024)
    bu_im = mm("mm_bu_im", NN, [(Op(proj, 0, C_S5U), Op(bb_im), BW, 1)], seq, S5_NS, 512, 1024)
    s_re, s_im = s5_scan_fwd(bu_re, bu_im, ab_re, ab_im)
    y_lin = mm("mm_s5y", NN, [(Op(s_re), Op(cc_re), S5_NS, 1), (Op(s_im), Op(cc_im), S5_NS, -1)], seq, BW, 256, BW)
    s5_act_args = [_tb(y_lin, CHUNK, BW, 0), _tb(proj, CHUNK, BW, C_S5U // BW), _tb(proj, CHUNK, BW, C_S5G // BW),
                   _param(p["s5_d"].reshape(1, BW)), _param(p["s5_w_glu"], True)]
    out_bw = O((seq, BW), f32, (CHUNK, BW), lambda i: (i, 0))
    y_a = block_fwd("s5_act", f_s5_act, nt, s5_act_args, [out_bw])[0]
    y_b = block_fwd("sgu", f_sgu, nt, _sgu_args(proj, p), [out_bw])[0]
    xc = block_fwd("m2_conv", f_m2_conv, 8, _m2_conv_args(proj, p, seq),
                   [O((seq, 2 * BW), f32, (seq, LANES), lambda j: (0, j))])[0]
    y_c, st_saved = block_fwd("m2_ssd", f_m2, nt, _m2_args(proj, xc, p), [out_bw], carries=[(M2_STATE, BW)])
    y_d = block_fwd("sc", f_sc, 4, _sc_args(proj, p, seq), [O((seq, BW), f32, (seq, LANES), lambda j: (0, j))])[0]
    ys = [y_a, y_b, y_c, y_d]
    merged = block_fwd("branch_mix", f_branch_mix, seq // t2, _mix_args(ys, proj, p, t2),
                       [O((seq, D_MODEL), bf16, (t2, D_MODEL), lambda i: (i, 0))])[0]
    x_new = mm("mm_out", NN, [(Op(merged), Op(p["w_out"]), D_MODEL, 1)], seq, D_MODEL, 512, D_MODEL, add=Op(x))
    sv.update(x=x, hb=hb, proj=proj, ab=(ab_re, ab_im), bb=(bb_re, bb_im), cc=(cc_re, cc_im), s=(s_re, s_im), y_lin=y_lin,
              xc=xc, st=st_saved, ys=ys, merged=merged)
    return x_new, sv


def _sgu_args(proj, p):
    c0 = C_SGU // BW
    args = [_tb(proj, CHUNK, BW, c0), _tb(proj, CHUNK, BW, c0 + 1), _tb(proj, CHUNK, BW, c0 + 2),
            _param(p["sgu_ln_w"].reshape(1, BW)), _param(p["sgu_ln_b"].reshape(1, BW))]
    args += [A(p["sgu_w"], (None, CHUNK, CHUNK), lambda i, h=h: (h, 0, 0)) for h in range(8)]
    args.append(_param(jnp.pad(p["sgu_b"], ((0, LANES - 8), (0, 0)))))
    return args


def _m2_conv_args(proj, p, seq):
    return [_strip(proj, seq, C_XBC // LANES), A(p["m2_conv_w"], (4, LANES), lambda j: (0, j)),
            A(p["m2_conv_b"].reshape(1, 2 * BW), (1, LANES), lambda j: (0, j))]


def _pad_lanes(v):
    return jnp.pad(v.reshape(1, -1), ((0, 0), (0, LANES - v.size)))


def _m2_args(proj, xc, p):
    args = [_tb(proj, CHUNK, BW, C_M2Z // BW), _tb(xc, CHUNK, BW, 0)]
    args += [_tb(xc, CHUNK, LANES, 4 + k) for k in range(4)]
    args.append(_tb(proj, CHUNK, LANES, C_DT // LANES))
    args += [_param(_pad_lanes(p["m2_dt_bias"])), _param(_pad_lanes(p["m2_a_log"])), _param(_pad_lanes(p["m2_d"])),
             _param(p["m2_norm_w"].reshape(1, BW))]
    return args


def _sc_args(proj, p, seq):
    c0 = C_SC // LANES
    return [_strip(proj, seq, c0 + k, 4) for k in range(4)] + [A(p["sc_conv_w"], (3, LANES), lambda j: (0, j))]


def _mix_args(ys, proj, p, t):
    args = [_tb(ys[k], t, BW, 0) for k in range(N_BRANCH)]
    args += [_tb(proj, t, D_MODEL, k) for k in range(N_BRANCH)]
    args += [A(p["w_branch"], (BW, D_MODEL), lambda i, k=k: (k, 0), True) for k in range(N_BRANCH)]
    mb = p["merge_b"].reshape(N_BRANCH, 1, D_MODEL)
    args += [A(mb, (None, 1, D_MODEL), lambda i, k=k: (k, 0, 0)) for k in range(N_BRANCH)]
    return args


def layer_bwd(d_out, p, sv):
    seq = d_out.shape[0]
    nt = seq // CHUNK
    t2 = 256
    proj, ys = sv["proj"], sv["ys"]
    g = {}
    acc = lambda shape: O(tuple(shape), f32, tuple(shape), lambda i, nd=len(shape): (0,) * nd)
    d_merged = mm("mm_out_dx", NT, [(Op(d_out), Op(p["w_out"]), D_MODEL, 1)], seq, D_MODEL, 512, D_MODEL, out_dtype=f32)
    g["w_out"] = mm("mm_out_dw", TN, [(Op(sv["merged"]), Op(d_out), seq, 1)], D_MODEL, D_MODEL, 256, D_MODEL, out_dtype=bf16)
    gouts = [O((seq, BW), f32, (CHUNK, BW), lambda i: (i, 0)) for _ in range(N_BRANCH)]
    gouts.append(O((seq, PW_MAIN), bf16, (CHUNK, N_BRANCH * D_MODEL), lambda i: (i, 0)))
    gouts += [acc((BW, D_MODEL)) for _ in range(N_BRANCH)] + [acc((1, D_MODEL)) for _ in range(N_BRANCH)]
    routes = [R(k, k) for k in range(N_BRANCH)] + [R(4 + k, 4, k * D_MODEL) for k in range(N_BRANCH)]
    routes += [R(8 + k, 5 + k, acc=True) for k in range(N_BRANCH)] + [R(12 + k, 9 + k, acc=True) for k in range(N_BRANCH)]
    res = block_bwd("branch_mix_bwd", f_branch_mix, nt, _mix_args(ys, proj, p, CHUNK),
                    [_tb(d_merged, CHUNK, D_MODEL, 0)], gouts, routes)
    dys, dproj = res[:4], res[4]
    g["w_branch"] = jnp.stack(res[5:9])
    g["merge_b"] = jnp.concatenate(res[9:13], axis=0)
    res = block_bwd("sc_bwd", f_sc, 4, _sc_args(proj, p, seq), [_strip(dys[3], seq, 0)],
                    [O((seq, PW_MAIN), bf16, (seq, 4 * LANES), lambda j: (0, C_SC // (4 * LANES) + j), alias=dproj),
                     O((3, BW), f32, (3, LANES), lambda j: (0, j))],
                    [R(k, 0, k * LANES) for k in range(4)] + [R(4, 1)])
    dproj, g["sc_conv_w"] = res
    res = block_bwd("m2_ssd_bwd", f_m2, nt, _m2_args(proj, sv["xc"], p), [_tb(dys[2], CHUNK, BW, 0)],
                    [O((seq, PW_MAIN), bf16, (CHUNK, BW), lambda i: (i, C_M2Z // BW), alias=dproj),
                     O((seq, 2 * BW), f32, (CHUNK, 2 * BW), lambda i: (i, 0)),
                     O((seq, LANES), bf16, (CHUNK, LANES), lambda i: (i, 0)),
                     acc((1, LANES)), acc((1, LANES)), acc((1, LANES)), acc((1, BW))],
                    [R(0, 0), R(1, 1, 0)] + [R(2 + k, 1, BW + k * LANES) for k in range(4)] + [R(6, 2)]
                    + [R(7, 3, acc=True), R(8, 4, acc=True), R(9, 5, acc=True), R(10, 6, acc=True)],
                    saved=[sv["st"]], rev=True)
    dproj, dxc, d_dt = res[0], res[1], res[2]
    g["m2_dt_bias"], g["m2_a_log"], g["m2_d"] = (r[0, :M2_HEADS] for r in res[3:6])
    g["m2_norm_w"] = res[6].reshape(BW)
    res = block_bwd("m2_conv_bwd", f_m2_conv, 8, _m2_conv_args(proj, p, seq), [_strip(dxc, seq, 0)],
                    [O((seq, PW_MAIN), bf16, (seq, LANES), lambda j: (0, C_XBC // LANES + j), alias=dproj),
                     O((4, 2 * BW), f32, (4, LANES), lambda j: (0, j)), O((1, 2 * BW), f32, (1, LANES), lambda j: (0, j))],
                    [R(0, 0), R(1, 1), R(2, 2)])
    dproj, g["m2_conv_w"], cb = res
    g["m2_conv_b"] = cb.reshape(2 * BW)
    res = block_bwd("sgu_bwd", f_sgu, nt, _sgu_args(proj, p), [_tb(dys[1], CHUNK, BW, 0)],
                    [O((seq, PW_MAIN), bf16, (CHUNK, 3 * BW), lambda i: (i, C_SGU // (3 * BW)), alias=dproj),
                     acc((1, BW)), acc((1, BW))] + [acc((CHUNK, CHUNK)) for _ in range(8)] + [acc((LANES, CHUNK))],
                    [R(0, 0, 0), R(1, 0, BW), R(2, 0, 2 * BW), R(3, 1, acc=True), R(4, 2, acc=True)]
                    + [R(5 + h, 3 + h, acc=True) for h in range(8)] + [R(13, 11, acc=True)])
    dproj = res[0]
    g["sgu_ln_w"], g["sgu_ln_b"] = res[1].reshape(BW), res[2].reshape(BW)
    g["sgu_w"] = jnp.stack(res[3:11])
    g["sgu_b"] = res[11][:8]
    y_lin, (s_re, s_im), (ab_re, ab_im) = sv["y_lin"], sv["s"], sv["ab"]
    s5_act_args = [_tb(y_lin, CHUNK, BW, 0), _tb(proj, CHUNK, BW, C_S5U // BW), _tb(proj, CHUNK, BW, C_S5G // BW),
                   _param(p["s5_d"].reshape(1, BW)), _param(p["s5_w_glu"], True)]
    res = block_bwd("s5_act_bwd", f_s5_act, nt, s5_act_args, [_tb(dys[0], CHUNK, BW, 0)],
                    [O((seq, BW), bf16, (CHUNK, BW), lambda i: (i, 0)), O((seq, BW), f32, (CHUNK, BW), lambda i: (i, 0)),
                     O((seq, PW_MAIN), bf16, (CHUNK, BW), lambda i: (i, C_S5G // BW), alias=dproj),
                     acc((1, BW)), acc((BW, BW))],
                    [R(0, 0), R(1, 1), R(2, 2), R(3, 3, acc=True), R(4, 4, acc=True)])
    dy_lin, du1, dproj = res[0], res[1], res[2]
    g["s5_d"] = res[3].reshape(S5_GROUPS, S5_P)
    g["s5_w_glu"] = res[4]
    (bb_re, bb_im), (cc_re, cc_im) = sv["bb"], sv["cc"]
    ds_re = mm("mm_s5y_dre", NT, [(Op(dy_lin), Op(cc_re), BW, 1)], seq, S5_NS, 512, 1024)
    ds_im = mm("mm_s5y_dim", NT, [(Op(dy_lin), Op(cc_im), BW, -1)], seq, S5_NS, 512, 1024)
    dc_re = mm("mm_s5y_dcre", TN, [(Op(s_re), Op(dy_lin), seq, 1)], S5_NS, BW, 512, BW)
    dc_im = mm("mm_s5y_dcim", TN, [(Op(s_im), Op(dy_lin), seq, -1)], S5_NS, BW, 512, BW)
    l_re, l_im, da_re, da_im = s5_scan_bwd(ds_re, ds_im, s_re, s_im, ab_re, ab_im)
    dproj = mm("mm_bu_dx", NT, [(Op(l_re), Op(bb_re), S5_NS, 1), (Op(l_im), Op(bb_im), S5_NS, 1)],
               seq, BW, 512, BW, add=Op(du1), out=dproj, out_col=C_S5U)
    dbb = [mm(f"mm_bu_dw{n}", TN, [(Op(proj, 0, C_S5U), Op(l), seq, 1)], BW, S5_NS, 256, 1024)
           for n, l in (("re", l_re), ("im", l_im))]
    gouts = [acc((1, S5_NS)), acc((1, S5_NS)), acc((1, LANES))] + [acc((S5_P, S5_NS))] * 2
    res = block_bwd("s5_prep_bwd", f_s5_prep, 1, _s5_prep_args(p),
                    [_param(v) for v in (da_re, da_im, dbb[0], dbb[1])], gouts, [R(k, k, acc=True) for k in range(5)])
    g["s5_lambda_re"], g["s5_lambda_im"] = res[0].reshape(S5_GROUPS, S5_N), res[1].reshape(S5_GROUPS, S5_N)
    g["s5_log_step"] = res[2][0, :S5_GROUPS]
    b_natural = lambda b: jnp.transpose(b.reshape(S5_P, S5_GROUPS, S5_N), (1, 2, 0))
    c_natural = lambda c: jnp.transpose(c.reshape(S5_P, S5_GROUPS, S5_N), (1, 0, 2))
    g["s5_b_re"], g["s5_b_im"] = b_natural(res[3]), b_natural(res[4])
    res = block_bwd("s5_c_bwd", f_s5_c, 1, _s5_c_args(p), [_param(dc_re), _param(dc_im)],
                    [acc((S5_P, S5_NS))] * 2, [R(0, 0, acc=True), R(1, 1, acc=True)])
    g["s5_c_re"], g["s5_c_im"] = c_natural(res[0]), c_natural(res[1])
    hb, w_in = sv["hb"], p["w_in"]
    dh = mm("mm_in_dx", NT, [(Op(dproj), Op(w_in), PW_MAIN, 1), (Op(d_dt), Op(w_in, 0, PW_MAIN), LANES, 1)],
            seq, D_MODEL, 256, 512)
    dw = mm("mm_in_dw", TN, [(Op(hb), Op(dproj), seq, 1)], D_MODEL, PW_MAIN, 256, 1024, out_dtype=bf16, out_width=PW)
    g["w_in"] = mm("mm_in_dwdt", TN, [(Op(hb), Op(d_dt), seq, 1)], D_MODEL, LANES, 256, LANES, out=dw, out_col=C_DT)
    dx, dnw = block_bwd("rms_in_bwd", f_rms_res, seq // t2, [_tb(sv["x"], t2, D_MODEL, 0), _param(p["norm_w"].reshape(1, D_MODEL))],
                        [_tb(dh, t2, D_MODEL, 0), _tb(d_out, t2, D_MODEL, 0)],
                        [O((seq, D_MODEL), f32, (t2, D_MODEL), lambda i: (i, 0)), acc((1, D_MODEL))],
                        [R(0, 0), R(1, 1, acc=True)])
    g["norm_w"] = dnw.reshape(D_MODEL)
    return dx, g


def loss_head(x, w, target):
    seq = x.shape[0]
    t = 256

    def body(x_ref, w_ref, t_ref, loss_ref, dx_ref, dw_ref):
        step = pl.program_id(0)

        def f(xv, wv):
            err = _rms(xv, wv) - t_ref[...]
            return 0.5 * jnp.sum(jnp.mean(err * err, axis=-1, keepdims=True), axis=0, keepdims=True)

        val, vjp_fn = jax.vjp(f, x_ref[...], w_ref[...])
        dx, dw = vjp_fn(jnp.ones((1, 1), f32))
        dx_ref[...] = dx

        @pl.when(step == 0)
        def _():
            loss_ref[...] = jnp.broadcast_to(val, loss_ref.shape)
            dw_ref[...] = dw

        @pl.when(step > 0)
        def _():
            loss_ref[...] += jnp.broadcast_to(val, loss_ref.shape)
            dw_ref[...] += dw

    blk = pl.BlockSpec((t, D_MODEL), lambda i: (i, 0))
    row = pl.BlockSpec((1, D_MODEL), lambda i: (0, 0))
    return pl.pallas_call(
        body, name="loss_head", grid=(seq // t,),
        in_specs=[blk, row, blk],
        out_specs=[pl.BlockSpec((1, LANES), lambda i: (0, 0)), blk, row],
        out_shape=[jax.ShapeDtypeStruct((1, LANES), f32), jax.ShapeDtypeStruct((seq, D_MODEL), f32),
                   jax.ShapeDtypeStruct((1, D_MODEL), f32)],
        compiler_params=_cparams(1),
    )(x, w.reshape(1, D_MODEL), target)


LAYER_KEYS = ("norm_w", "w_in", "s5_lambda_re", "s5_lambda_im", "s5_b_re", "s5_b_im", "s5_c_re", "s5_c_im", "s5_d",
              "s5_log_step", "s5_w_glu", "sgu_ln_w", "sgu_ln_b", "sgu_w", "sgu_b", "m2_conv_w", "m2_conv_b",
              "m2_dt_bias", "m2_a_log", "m2_d", "m2_norm_w", "sc_conv_w", "merge_b", "w_branch", "w_out")


def local_step(x, target, layers, final_norm_w):
    saved = []
    for p in layers:
        x, sv = layer_fwd(x, p)
        saved.append(sv)
    loss, dx, dfw = loss_head(x, final_norm_w, target)
    grads = []
    for p, sv in zip(reversed(layers), reversed(saved)):
        dx, g = layer_bwd(dx, p, sv)
        grads.append(g)
    return loss[0, 0], dx, grads[::-1], dfw.reshape(D_MODEL)


MESH = pl.DeviceIdType.MESH
ANY = pl.BlockSpec(memory_space=pl.ANY)


def _me():
    return lax.axis_index("x"), lax.axis_index("y"), lax.axis_index("c")


def _other_chips(x, y):
    return [(1 - x, y), (x, 1 - y), (1 - x, 1 - y)]


def _rcopy(src, dst, send, recv, dev):
    return pltpu.make_async_remote_copy(src_ref=src, dst_ref=dst, send_sem=send, recv_sem=recv,
                                        device_id=dev, device_id_type=MESH)


def _route_cut(rows, dtype):
    tile = 2 * SUBL * (4 // jnp.dtype(dtype).itemsize)
    return rows // 2 if rows % tile == 0 else rows


def _comm_call(name, body, arrs, out_shape, n_remote, aliases=None):
    n = len(arrs)
    return pl.pallas_call(
        body, name=name, in_specs=[ANY] * n, out_specs=[ANY] * len(out_shape), out_shape=out_shape,
        scratch_shapes=[pltpu.SemaphoreType.DMA((n, n_remote)), pltpu.SemaphoreType.DMA((n, n_remote))],
        input_output_aliases=aliases or {},
        compiler_params=pltpu.CompilerParams(has_side_effects=True),
    )(*arrs)


def gather_chips(name, arrs):
    n = len(arrs)
    cut = [_route_cut(a.shape[1], a.dtype) for a in arrs]

    def body(*refs):
        ins, outs = refs[:n], refs[n:2 * n]
        send, recv = refs[2 * n:]
        x, y, c = _me()
        jme, jx, jy, jd = 2 * x + y, 2 * (1 - x) + y, 2 * x + 1 - y, 2 * (1 - x) + 1 - y
        to_x, to_y, sib = (1 - x, y, c), (x, 1 - y, c), (x, y, 1 - c)

        def part(ref, a, hi):
            return ref.at[pl.ds(cut[a], ref.shape[0] - cut[a])] if hi else ref.at[pl.ds(0, cut[a])]

        def cp(a, k, ref, dev):
            return _rcopy(ref, ref, send.at[a, k], recv.at[a, k], dev)

        split = [a for a in range(n) if cut[a] < arrs[a].shape[1]]
        sent = [_rcopy(ins[a].at[c], outs[a].at[jme, c], send.at[a, k], recv.at[a, k], dev)
                for a in range(n) for k, dev in ((0, to_x), (1, to_y))]
        for s in sent:
            s.start()
        for a in range(n):
            blk = outs[a].at[jx, c]
            cp(a, 0, blk, to_x).wait_recv()
            sent += [cp(a, 2, part(blk, a, False), to_y), cp(a, 4, blk, sib)]
            sent[-2].start()
            sent[-1].start()
        for a in range(n):
            blk = outs[a].at[jy, c]
            cp(a, 1, blk, to_y).wait_recv()
            sent.append(cp(a, 5, blk, sib))
            sent[-1].start()
            if a in split:
                sent.append(cp(a, 3, part(blk, a, True), to_x))
                sent[-1].start()
        for a in range(n):
            lo = part(outs[a].at[jd, c], a, False)
            cp(a, 2, lo, to_y).wait_recv()
            sent.append(cp(a, 6, lo, sib))
            sent[-1].start()
        for a in split:
            hi = part(outs[a].at[jd, c], a, True)
            cp(a, 3, hi, to_x).wait_recv()
            sent.append(cp(a, 7, hi, sib))
            sent[-1].start()
        for a in range(n):
            cp(a, 4, outs[a].at[jx, 1 - c], sib).wait_recv()
            cp(a, 5, outs[a].at[jy, 1 - c], sib).wait_recv()
            cp(a, 6, part(outs[a].at[jd, 1 - c], a, False), sib).wait_recv()
        for a in split:
            cp(a, 7, part(outs[a].at[jd, 1 - c], a, True), sib).wait_recv()
        for s in sent:
            s.wait_send()

    out_shape = [jax.ShapeDtypeStruct((4,) + a.shape, a.dtype) for a in arrs]
    got = _comm_call(name, body, arrs, out_shape, 8)
    jme = 2 * lax.axis_index("x") + lax.axis_index("y")
    return [lax.dynamic_update_index_in_dim(g, a, jme, 0) for g, a in zip(got, arrs)]


def swap_halves(name, arrs):
    n = len(arrs)

    def body(*refs):
        ins, outs = refs[:n], refs[n:2 * n]
        send, recv = refs[2 * n:]
        x, y, c = _me()
        remote = [_rcopy(ins[a].at[1 - c], outs[a], send.at[a, 0], recv.at[a, 0], (x, y, 1 - c)) for a in range(n)]
        for cp in remote:
            cp.start()
        for cp in remote:
            cp.wait()

    return _comm_call(name, body, arrs, [jax.ShapeDtypeStruct(a.shape[1:], a.dtype) for a in arrs], 1)


def exchange_chips(name, arrs):
    n = len(arrs)

    def body(*refs):
        ins, outs = refs[:n], refs[n:2 * n]
        send, recv = refs[2 * n:]
        x, y, c = _me()
        remote = [_rcopy(ins[a].at[2 * cx + cy], outs[a].at[k], send.at[a, k], recv.at[a, k], (cx, cy, c))
                  for a in range(n) for k, (cx, cy) in enumerate(_other_chips(x, y))]
        for cp in remote:
            cp.start()
        for cp in remote:
            cp.wait()

    return _comm_call(name, body, arrs, [jax.ShapeDtypeStruct((3,) + a.shape[1:], a.dtype) for a in arrs], 3)


def gather_cores(name, arrs):
    n = len(arrs)

    def body(*refs):
        bufs = refs[n:2 * n]
        send, recv = refs[2 * n:]
        x, y, c = _me()
        remote = [_rcopy(bufs[a].at[c], bufs[a].at[c], send.at[a, 0], recv.at[a, 0], (x, y, 1 - c)) for a in range(n)]
        for cp in remote:
            cp.start()
        for a in range(n):
            _rcopy(bufs[a].at[1 - c], bufs[a].at[1 - c], send.at[a, 0], recv.at[a, 0], (x, y, 1 - c)).wait_recv()
        for cp in remote:
            cp.wait_send()

    return _comm_call(name, body, arrs, [jax.ShapeDtypeStruct(a.shape, a.dtype) for a in arrs], 1,
                      aliases={a: a for a in range(n)})


ROW_BLOCK = 256


def esum(name, terms, rows, width, out_dtype, out_slots=None):
    tr = next((t for t in range(min(rows, ROW_BLOCK), 0, -SUBL) if rows % t == 0 and t % SUBL == 0), rows)
    where =jnp.stack([lax.axis_index("c"), 2 * lax.axis_index("x") + lax.axis_index("y")]).astype(jnp.int32)
    pick = {"c": 0, "j": 1}

    def body(s_ref, *refs):
        acc = refs[0][...].astype(f32)
        for r in refs[1:-1]:
            acc = acc + r[...].astype(f32)
        refs[-1][...] = acc.astype(out_dtype)

    specs = []
    for arr, lead in terms:
        if lead is None:
            specs.append(pl.BlockSpec((tr, width), lambda i, s: (i, 0)))
        elif isinstance(lead, str):
            specs.append(pl.BlockSpec((None, tr, width), lambda i, s, lead=lead: (s[pick[lead]], i, 0)))
        else:
            specs.append(pl.BlockSpec((None, tr, width), lambda i, s, lead=lead: (lead, i, 0)))
    if out_slots is None:
        out_spec = pl.BlockSpec((tr, width), lambda i, s: (i, 0))
        out_shape = jax.ShapeDtypeStruct((rows, width), out_dtype)
    else:
        out_spec = pl.BlockSpec((None, tr, width), lambda i, s: (s[0], i, 0))
        out_shape = jax.ShapeDtypeStruct((out_slots, rows, width), out_dtype)
    return pl.pallas_call(
        body, name=name,
        grid_spec=pltpu.PrefetchScalarGridSpec(num_scalar_prefetch=1, grid=(rows // tr,), in_specs=specs, out_specs=out_spec),
        out_shape=out_shape, compiler_params=_cparams(1),
    )(where, *[t[0] for t in terms])


def reduce_to_shards(parts):
    tags = [str(k) for k in range(len(parts))]
    theirs = swap_halves("rs_swap", parts)
    t1 = []
    for tag, p, th in zip(tags, parts, theirs):
        _, _, h, w = p.shape
        t1.append(esum("rs_add_cores" + tag, [(p.reshape(2, 4 * h, w), "c"), (th.reshape(4 * h, w), None)],
                       4 * h, w, p.dtype).reshape(4, h, w))
    landed = exchange_chips("rs_exchange", t1)
    red = []
    for tag, p, t, got in zip(tags, parts, t1, landed):
        _, _, h, w = p.shape
        red.append(esum("rs_add_chips" + tag, [(t, "j"), (got, 0), (got, 1), (got, 2)], h, w, f32, out_slots=2))
    return gather_cores("rs_gather", red)


def _adamw_step(w_ref, g_ref, m_ref, v_ref, d_ref, nm_ref, nv_ref):
    gv = g_ref[...]
    nm = ADAM_B1 * m_ref[...] + (1.0 - ADAM_B1) * gv
    nv = ADAM_B2 * v_ref[...] + (1.0 - ADAM_B2) * jnp.square(gv)
    m_hat = nm / (1.0 - ADAM_B1 ** ADAM_STEP)
    v_hat = nv / (1.0 - ADAM_B2 ** ADAM_STEP)
    d_ref[...] = -ADAM_LR * (m_hat / (jnp.sqrt(v_hat) + ADAM_EPS) + ADAM_WD * w_ref[...])
    nm_ref[...] = nm
    nv_ref[...] = nv


def adamw(name, w, g, m, v, tr=None):
    rows, rest = w.shape[0], w.shape[1:]
    if tr is None:
        tr = ROW_BLOCK if rows % ROW_BLOCK == 0 else rows
    assert rows % tr == 0

    def body(*refs):
        _adamw_step(*refs)

    spec = pl.BlockSpec((tr,) + rest, lambda i: (i,) + (0,) * len(rest))
    return pl.pallas_call(
        body, name=name, grid=(rows // tr,), in_specs=[spec] * 4, out_specs=[spec] * 3,
        out_shape=[jax.ShapeDtypeStruct(w.shape, f32)] * 3, compiler_params=_cparams(1),
    )(w, g, m, v)


def adamw_many(name, ws, gs, ms, vs):
    n = len(ws)

    def body(*refs):
        ins, outs = refs[:4 * n], refs[4 * n:]
        for k in range(n):
            _adamw_step(ins[k], ins[n + k], ins[2 * n + k], ins[3 * n + k], outs[k], outs[n + k], outs[2 * n + k])

    vmem = pl.BlockSpec(memory_space=pltpu.VMEM)
    res = pl.pallas_call(
        body, name=name, in_specs=[vmem] * (4 * n), out_specs=[vmem] * (3 * n),
        out_shape=[jax.ShapeDtypeStruct(w.shape, f32) for w in ws] * 3,
        compiler_params=pltpu.CompilerParams(vmem_limit_bytes=VMEM_LIMIT),
    )(*ws, *gs, *ms, *vs)
    return res[:n], res[n:2 * n], res[2 * n:]


PACK_W = 1024


def _pack(parts, halves, row_mult):
    flat = jnp.concatenate([p.reshape(-1) for p in parts])
    per = halves * row_mult * PACK_W
    total = -(-flat.size // per) * per
    flat = jnp.pad(flat, (0, total - flat.size))
    return flat.reshape(halves, total // (halves * PACK_W), PACK_W)


def _unpack(flat, shapes):
    out, pos = [], 0
    for s in shapes:
        n = int(np.prod(s))
        out.append(flat[pos:pos + n].reshape(s))
        pos += n
    return out


SHARDED_BIG = ("w_in", "w_branch", "w_out", "s5_w_glu")
SHARDED_SMALL = ("m2_conv_w", "sc_conv_w", "merge_b")
SHARD_AXIS = {"w_in": 2, "w_branch": 3, "w_out": 1, "s5_w_glu": 1, "m2_conv_w": 2, "sc_conv_w": 2, "merge_b": 2}
REPLICATED = ("norm_w", "s5_lambda_re", "s5_lambda_im", "s5_b_re", "s5_b_im", "s5_c_re", "s5_c_im", "s5_d", "s5_log_step",
              "sgu_ln_w", "sgu_ln_b", "sgu_w", "sgu_b", "m2_conv_b", "m2_dt_bias", "m2_a_log", "m2_d", "m2_norm_w")
WEIGHTS = ("norm_w", "w_in", "s5_lambda_re", "s5_lambda_im", "s5_b_re", "s5_b_im", "s5_c_re", "s5_c_im", "s5_d",
           "s5_log_step", "s5_w_glu", "sgu_ln_w", "sgu_ln_b", "sgu_w", "sgu_b", "m2_conv_w", "m2_conv_b", "m2_dt_bias",
           "m2_a_log", "m2_d", "m2_norm_w", "sc_conv_w", "merge_b", "w_branch", "w_out", "final_norm_w")
N_LAYERS = 2


SHARD_W = IN_DIM // 4
SHARD_PAD = -(-SHARD_W // LANES) * LANES
REGROUP_W = 3 * LANES


def _kernel_pieces():
    out, pos = [], 0
    for s, n in _col_segments():
        while n:
            take = min(n, SHARD_W - s % SHARD_W)
            out.append((pos, s, take))
            pos, s, n = pos + take, s + take, n - take
    return out


def regroup_cols(name, src, steps, out_shape, out=None):
    n_src, rows, width = src.shape
    ow, win = REGROUP_W, REGROUP_W + LANES
    k_max = max(len(p) for _, _, p in steps)
    assert width % LANES == 0

    def body(*refs):
        src_ref, out_ref = refs[0], refs[-5]
        wbuf, obuf, sem_in, sem_out = refs[-4:]

        def fetch(q, slot):
            started = []
            for p, (s, col, lo, hi) in enumerate(steps[q][2]):
                w0 = col // LANES * LANES
                wlen = min(win, width - w0)
                cp = pltpu.make_async_copy(src_ref.at[s, :, pl.ds(w0, wlen)], wbuf.at[slot, p, :, pl.ds(0, wlen)], sem_in.at[slot, p])
                cp.start()
                started.append((cp, wlen, col - w0 - lo, lo, hi))
            return started

        pend, writes = fetch(0, 0), [None, None]
        for q, (t, b, _) in enumerate(steps):
            slot = q % 2
            nxt = fetch(q + 1, 1 - slot) if q + 1 < len(steps) else []
            acc = [jnp.zeros((rows, LANES), f32) for _ in range(ow // LANES)]
            for p, (cp, wlen, shift, lo, hi) in enumerate(pend):
                cp.wait()
                for k in range(ow // LANES):
                    c_lo, c_hi = max(lo, k * LANES), min(hi, (k + 1) * LANES)
                    if c_lo >= c_hi:
                        continue
                    wb = (c_lo + shift) // LANES
                    alen = min(2 * LANES, wlen - wb * LANES)
                    a = wbuf[slot, p, :, wb * LANES:wb * LANES + alen]
                    r, c = _rows((alen, LANES)), _cols((alen, LANES))
                    sh = shift + (k - wb) * LANES
                    sel = (r == c + sh) & (c >= c_lo - k * LANES) & (c < c_hi - k * LANES)
                    acc[k] = acc[k] + jnp.dot(a, sel.astype(bf16), preferred_element_type=f32)
            if writes[slot] is not None:
                writes[slot].wait()
            for k in range(ow // LANES):
                obuf[slot, :, k * LANES:(k + 1) * LANES] = acc[k].astype(bf16)
            writes[slot] = pltpu.make_async_copy(obuf.at[slot], out_ref.at[t, :, pl.ds(b * ow, ow)], sem_out.at[slot])
            writes[slot].start()
            pend = nxt
        for wr in writes:
            if wr is not None:
                wr.wait()

    operands, io_alias = [src], {}
    if out is not None:
        operands.append(out)
        io_alias = {1: 0}
    return pl.pallas_call(
        body, name=name, in_specs=[ANY] * len(operands), out_specs=ANY,
        out_shape=jax.ShapeDtypeStruct(out_shape, bf16), input_output_aliases=io_alias,
        scratch_shapes=[pltpu.VMEM((2, k_max, rows, win), bf16), pltpu.VMEM((2, rows, ow), bf16),
                        pltpu.SemaphoreType.DMA((2, k_max)), pltpu.SemaphoreType.DMA((2,))],
        compiler_params=pltpu.CompilerParams(vmem_limit_bytes=VMEM_LIMIT),
    )(*operands)


def _steps_to_kernel_cols(layer):
    steps = []
    for o in range(PW // REGROUP_W):
        pieces = []
        for pos, s, n in _kernel_pieces():
            lo, hi = max(pos, o * REGROUP_W), min(pos + n, (o + 1) * REGROUP_W)
            if lo < hi:
                ref_col = s + lo - pos
                pieces.append(((ref_col // SHARD_W) * N_LAYERS + layer, ref_col % SHARD_W, lo - o * REGROUP_W, hi - o * REGROUP_W))
        steps.append((0, o, pieces))
    return steps


def _steps_to_shards(layer):
    steps = []
    for j in range(4):
        for b in range(SHARD_PAD // REGROUP_W):
            start, stop = j * SHARD_W + b * REGROUP_W, min(j * SHARD_W + (b + 1) * REGROUP_W, (j + 1) * SHARD_W)
            pieces = []
            for pos, s, n in _kernel_pieces():
                lo, hi = max(s, start), min(s + n, stop)
                if lo < hi:
                    pieces.append((0, pos + lo - s, lo - start, hi - start))
            steps.append((layer * 4 + j, b, pieces))
    return steps


def _gather_weights(w):
    w_in = jnp.pad(w["w_in"].astype(bf16), ((0, 0), (0, 0), (0, SHARD_PAD - SHARD_W)))
    arrs = [w_in, w["w_branch"].reshape(N_LAYERS, N_BRANCH * BW, -1).astype(bf16),
            w["w_out"].astype(bf16), w["s5_w_glu"].astype(bf16), w["m2_conv_w"], w["sc_conv_w"], w["merge_b"]]
    got = gather_chips("ag_weights", arrs)
    cols = lambda t: jnp.transpose(t, (1, 0, 2)).reshape(t.shape[1], -1)
    layers = []
    for i in range(N_LAYERS):
        p = {k: w[k][i] for k in REPLICATED}
        p["w_in"] = regroup_cols(f"w_in_cols{i}", got[0].reshape(4 * N_LAYERS, D_MODEL, SHARD_PAD), _steps_to_kernel_cols(i),
                                 (1, D_MODEL, PW))[0]
        p["w_branch"] = cols(got[1][:, i])
        p["w_out"] = got[2][:, i].reshape(D_MODEL, D_MODEL)
        p["s5_w_glu"] = got[3][:, i].reshape(BW, BW)
        p["m2_conv_w"], p["sc_conv_w"], p["merge_b"] = cols(got[4][:, i]), cols(got[5][:, i]), cols(got[6][:, i])
        layers.append(p)
    return layers


def _reduce_grads(grads, d_final, w):
    to_chips = lambda t: jnp.transpose(t.reshape(t.shape[0], 4, -1), (1, 0, 2))
    stack = lambda f: jnp.stack([f(g) for g in grads])
    dw_in = None
    for i, g in enumerate(grads):
        dw_in = regroup_cols(f"w_in_shards{i}", g["w_in"][None], _steps_to_shards(i), (4 * N_LAYERS, D_MODEL, SHARD_PAD), out=dw_in)
    parts = [dw_in.reshape(N_LAYERS, 4, D_MODEL, SHARD_PAD),
             stack(lambda g: to_chips(g["w_branch"].reshape(N_BRANCH * BW, D_MODEL)).astype(bf16)),
             stack(lambda g: g["w_out"].reshape(4, D_MODEL // 4, D_MODEL).astype(bf16)),
             stack(lambda g: g["s5_w_glu"].reshape(4, BW // 4, BW).astype(bf16))]
    rep = jnp.concatenate([stack(lambda g: g[k]).reshape(-1) for k in REPLICATED] + [d_final.reshape(-1)])
    quarter = -(-rep.size // (4 * 2 * SUBL * PACK_W)) * (2 * SUBL * PACK_W)
    rep = jnp.pad(rep, (0, 4 * quarter - rep.size))
    small = []
    for j in range(4):
        sharded = [stack(lambda g: to_chips(g[k])[j]) for k in SHARDED_SMALL]
        small.append(_pack(sharded + [rep[j * quarter:(j + 1) * quarter]], 2, SUBL))
    parts.append(jnp.stack(small, axis=1))
    red = reduce_to_shards(parts)
    out = {"w_in": red[0][:, :, :SHARD_W], "w_branch": red[1].reshape(w["w_branch"].shape), "w_out": red[2], "s5_w_glu": red[3]}
    small_flat = red[4].reshape(-1)
    n_small = sum(int(np.prod(w[k].shape)) for k in SHARDED_SMALL)
    out.update(zip(SHARDED_SMALL, _unpack(small_flat, [w[k].shape for k in SHARDED_SMALL])))
    mine = small_flat[n_small:n_small + quarter].reshape(2, quarter // (2 * PACK_W), PACK_W)
    rep_all = gather_chips("ag_small_grads", [mine])[0].reshape(-1)
    names = REPLICATED + ("final_norm_w",)
    out.update(zip(names, _unpack(rep_all, [w[k].shape for k in names])))
    return out


def _update(w, g, m, v):
    d, nm, nv = {}, {}, {}
    flat2 = lambda a: a.reshape(-1, a.shape[-1])
    cols_major = lambda a: jnp.transpose(a, (2, 0, 1))
    res = adamw("adamw_w_in", *[cols_major(t["w_in"]) for t in (w, g, m, v)], tr=SHARD_W // 14)
    d["w_in"], nm["w_in"], nv["w_in"] = (jnp.transpose(r, (1, 2, 0)) for r in res)
    for k in SHARDED_BIG[1:]:
        res = adamw("adamw_" + k, *[flat2(t[k]) for t in (w, g, m, v)])
        d[k], nm[k], nv[k] = (r.reshape(w[k].shape) for r in res)
    for k in ("s5_b_re", "s5_b_im"):
        res = adamw("adamw_" + k, *[flat2(t[k]) for t in (w, g, m, v)])
        d[k], nm[k], nv[k] = (r.reshape(w[k].shape) for r in res)
    rest = [k for k in WEIGHTS if k not in SHARDED_BIG + ("s5_b_re", "s5_b_im")]
    two_d = lambda a: a.reshape(1, -1) if a.ndim == 1 else a
    res = adamw_many("adamw_rest", *[[two_d(t[k]) for k in rest] for t in (w, g, m, v)])
    for tgt, rs in zip((d, nm, nv), res):
        tgt.update({k: r.reshape(w[k].shape) for k, r in zip(rest, rs)})
    return d, nm, nv


def kernel(x, norm_w, w_in, s5_lambda_re, s5_lambda_im, s5_b_re, s5_b_im, s5_c_re, s5_c_im, s5_d, s5_log_step, s5_w_glu, sgu_ln_w, sgu_ln_b, sgu_w, sgu_b, m2_conv_w, m2_conv_b, m2_dt_bias, m2_a_log, m2_d, m2_norm_w, sc_conv_w, merge_b, w_branch, w_out, final_norm_w, loss_target, m_norm_w, m_w_in, m_s5_lambda_re, m_s5_lambda_im, m_s5_b_re, m_s5_b_im, m_s5_c_re, m_s5_c_im, m_s5_d, m_s5_log_step, m_s5_w_glu, m_sgu_ln_w, m_sgu_ln_b, m_sgu_w, m_sgu_b, m_m2_conv_w, m_m2_conv_b, m_m2_dt_bias, m_m2_a_log, m_m2_d, m_m2_norm_w, m_sc_conv_w, m_merge_b, m_w_branch, m_w_out, m_final_norm_w, v_norm_w, v_w_in, v_s5_lambda_re, v_s5_lambda_im, v_s5_b_re, v_s5_b_im, v_s5_c_re, v_s5_c_im, v_s5_d, v_s5_log_step, v_s5_w_glu, v_sgu_ln_w, v_sgu_ln_b, v_sgu_w, v_sgu_b, v_m2_conv_w, v_m2_conv_b, v_m2_dt_bias, v_m2_a_log, v_m2_d, v_m2_norm_w, v_sc_conv_w, v_merge_b, v_w_branch, v_w_out, v_final_norm_w):
    given = dict(locals())
    w = {k: given[k] for k in WEIGHTS}
    m = {k: given["m_" + k] for k in WEIGHTS}
    v = {k: given["v_" + k] for k in WEIGHTS}
    layers = _gather_weights(w)
    loss, dx, grads, d_final = local_step(x[0], loss_target[0], layers, final_norm_w)
    loss = lax.psum(loss, ("x", "y", "c"))
    g = _reduce_grads(grads, d_final, w)
    d, nm, nv = _update(w, g, m, v)
    return (loss, dx[None], *[g[k] for k in WEIGHTS], *[d[k] for k in WEIGHTS],
            *[nm[k] for k in WEIGHTS], *[nv[k] for k in WEIGHTS])
```

```python
import functools
from typing import Any, Callable, NamedTuple

import numpy as np
import jax
import jax.numpy as jnp
from jax import lax
from jax.experimental import pallas as pl
from jax.experimental.pallas import tpu as pltpu

f32 = jnp.float32
bf16 = jnp.bfloat16

D_MODEL = 1024
BW = 512
N_BRANCH = 4
EPS = 1e-6
S5_GROUPS, S5_P, S5_N = 32, 16, 64
S5_NS = S5_GROUPS * S5_N
CHUNK = 128
M2_HEADS, M2_HEAD_DIM, M2_GROUPS, M2_STATE = 8, 64, 2, 128
IN_DIM = 10248
PW = 10368
PW_MAIN = 10240
LANES = 128
VMEM_LIMIT = 60 * 1024 * 1024

ADAM_LR, ADAM_B1, ADAM_B2, ADAM_EPS, ADAM_WD, ADAM_STEP = 0.001, 0.9, 0.999, 1e-08, 0.01, 10

C_MERGE = 0
C_SC = 4096
C_SGU = 6144
C_S5G = 7680
C_S5U = 8192
C_M2Z = 8704
C_XBC = 9216
C_DT = 10240


def _col_segments():
    segs = [(6152, 4096)]
    for j in range(4):
        segs += [(4104 + 128 * j, 128), (4616 + 128 * j, 128), (5128 + 128 * j, 128), (5640 + 128 * j, 128)]
    segs += [(1024, 1536), (512, 512), (0, 512), (2560, 512), (3072, 1024), (4096, 8)]
    return segs


def _to_kernel_cols(w):
    parts = [w[:, s:s + n] for s, n in _col_segments()]
    parts.append(jnp.zeros((w.shape[0], PW - IN_DIM), w.dtype))
    return jnp.concatenate(parts, axis=1)


def _from_kernel_cols(wp):
    out, pos = {}, 0
    for s, n in _col_segments():
        out[s] = wp[:, pos:pos + n]
        pos += n
    return jnp.concatenate([out[s] for s in sorted(out)], axis=1)


NN = ((1,), (0,))
NT = ((1,), (1,))
TN = ((0,), (0,))


def _bd(a, b, dims):
    return lax.dot_general(a.astype(bf16), b.astype(bf16), (dims, ((), ())), preferred_element_type=f32)


def _hd(a, b, dims):
    return lax.dot_general(a, b, (dims, ((), ())), precision=lax.Precision.HIGHEST, preferred_element_type=f32)


def _make_dots(raw):
    @jax.custom_vjp
    def nn(a, b):
        return raw(a, b, NN)
    nn.defvjp(lambda a, b: (raw(a, b, NN), (a, b)), lambda r, g: (raw(g, r[1], NT), raw(r[0], g, TN)))

    @jax.custom_vjp
    def nt(a, b):
        return raw(a, b, NT)
    nt.defvjp(lambda a, b: (raw(a, b, NT), (a, b)), lambda r, g: (raw(g, r[1], NN), raw(g, r[0], TN)))

    @jax.custom_vjp
    def tn(a, b):
        return raw(a, b, TN)
    tn.defvjp(lambda a, b: (raw(a, b, TN), (a, b)), lambda r, g: (raw(r[1], g, NT), raw(r[0], g, NN)))
    return nn, nt, tn


bdot, bdot_nt, bdot_tn = _make_dots(_bd)
hdot, hdot_nt, hdot_tn = _make_dots(_hd)


@jax.custom_vjp
def bdot_w(a, w, shadow):
    return _bd(a, w, NN)


bdot_w.defvjp(lambda a, w, s: (_bd(a, w, NN), (a, w)),
              lambda r, g: (_bd(g, r[1], NT), jnp.zeros_like(r[1]), _bd(r[0], g, TN)))


def _rows(shape):
    return lax.broadcasted_iota(jnp.int32, shape, 0)


def _cols(shape):
    return lax.broadcasted_iota(jnp.int32, shape, 1)


def _shift_down(x, s):
    return jnp.where(_rows(x.shape) < s, 0.0, pltpu.roll(x, s, 0))


def _shift_up(x, s):
    n = x.shape[0]
    return jnp.where(_rows(x.shape) >= n - s, 0.0, pltpu.roll(x, n - s, 0))


@functools.partial(jax.custom_vjp, nondiff_argnums=(1,))
def shift(x, s):
    return _shift_down(x, s) if s else x


shift.defvjp(lambda x, s: (shift(x, s), None), lambda s, _, g: (_shift_up(g, s) if s else g,))


def _row_of(w, k):
    return jnp.sum(jnp.where(_rows(w.shape) == k, w, 0.0), axis=0, keepdims=True)


def _lane_mask(width, lo, hi):
    c = _cols((1, width))
    return ((c >= lo) & (c < hi)).astype(f32)


def _expand(rows, width, per):
    return (_cols((rows, width)) // per == _rows((rows, width))).astype(f32)


class A(NamedTuple):
    arr: Any
    block: tuple
    imap: Callable
    shadow: bool = False


class O(NamedTuple):
    shape: tuple
    dtype: Any
    block: tuple
    imap: Callable
    alias: Any = None


class R(NamedTuple):
    arg: int
    out: int
    off: Any = None
    acc: bool = False


def _cparams(n_grid):
    return pltpu.CompilerParams(dimension_semantics=("arbitrary",) * n_grid, vmem_limit_bytes=VMEM_LIMIT)


def _ispec(block, imap, n, rev):
    if rev:
        return pl.BlockSpec(block, lambda i: imap(n - 1 - i))
    return pl.BlockSpec(block, imap)


def _load(ref, a):
    v = ref[...]
    if a.shadow:
        return (v, jnp.zeros(v.shape, f32))
    return v.astype(f32)


def _save_spec(shape, n, rev):
    nd = len(shape)
    return _ispec((None,) + tuple(shape), lambda i: (i,) + (0,) * nd, n, rev)


def block_fwd(name, f, n, args, outs, carries=()):
    n_in, n_out, n_c = len(args), len(outs), len(carries)

    def body(*refs):
        ins, out_r = refs[:n_in], refs[n_in:n_in + n_out]
        saves, cs = refs[n_in + n_out:n_in + n_out + n_c], refs[n_in + n_out + n_c:]
        if n_c:
            @pl.when(pl.program_id(0) == 0)
            def _():
                for c in cs:
                    c[...] = jnp.zeros(c.shape, f32)
        vals = [_load(r, a) for r, a in zip(ins, args)]
        cv = [c[...] for c in cs]
        for s, v in zip(saves, cv):
            s[...] = v
        res = f(*vals, *cv)
        for r, v in zip(out_r, res[:n_out]):
            r[...] = v.astype(r.dtype)
        for c, v in zip(cs, res[n_out:]):
            c[...] = v

    out_shape = [jax.ShapeDtypeStruct(o.shape, o.dtype) for o in outs]
    out_specs = [pl.BlockSpec(o.block, o.imap) for o in outs]
    for shp in carries:
        out_shape.append(jax.ShapeDtypeStruct((n,) + tuple(shp), f32))
        out_specs.append(_save_spec(shp, n, False))
    return pl.pallas_call(
        body, name=name, grid=(n,),
        in_specs=[pl.BlockSpec(a.block, a.imap) for a in args],
        out_specs=out_specs, out_shape=out_shape,
        scratch_shapes=[pltpu.VMEM(tuple(shp), f32) for shp in carries],
        compiler_params=_cparams(1),
    )(*[a.arr for a in args])


def block_bwd(name, f, n, args, cots, gouts, routes, saved=(), rev=False):
    n_in, n_cot, n_c, n_go = len(args), len(cots), len(saved), len(gouts)
    diff = []
    for r in routes:
        if r.arg not in diff:
            diff.append(r.arg)
    aliases = [(k, o.alias) for k, o in enumerate(gouts) if o.alias is not None]

    def body(*refs):
        ins = refs[:n_in]
        cot_r = refs[n_in:n_in + n_cot]
        sav_r = refs[n_in + n_cot:n_in + n_cot + n_c]
        base = n_in + n_cot + n_c + len(aliases)
        go_r = refs[base:base + n_go]
        dcs = refs[base + n_go:]
        step = pl.program_id(0)
        if n_c:
            @pl.when(step == 0)
            def _():
                for d in dcs:
                    d[...] = jnp.zeros(d.shape, f32)
        vals = [_load(r, a) for r, a in zip(ins, args)]
        cv = [s[...] for s in sav_r]
        nd = len(diff)

        def g(*dv):
            full = list(vals)
            for idx, v in zip(diff, dv[:nd]):
                full[idx] = (vals[idx][0], v) if args[idx].shadow else v
            return tuple(f(*full, *dv[nd:]))

        primals = [vals[i][1] if args[i].shadow else vals[i] for i in diff] + cv
        _, vjp_fn = jax.vjp(g, *primals)
        ct = tuple([r[...].astype(f32) for r in cot_r] + [d[...] for d in dcs])
        grads = vjp_fn(ct)
        for r in routes:
            gr = grads[diff.index(r.arg)]
            ref = go_r[r.out]
            if r.acc:
                @pl.when(step == 0)
                def _(ref=ref, gr=gr):
                    ref[...] = gr.astype(ref.dtype)

                @pl.when(step > 0)
                def _(ref=ref, gr=gr):
                    ref[...] += gr.astype(ref.dtype)
            elif r.off is None:
                ref[...] = gr.astype(ref.dtype)
            else:
                ref[:, r.off:r.off + gr.shape[1]] = gr.astype(ref.dtype)
        for d, gr in zip(dcs, grads[nd:]):
            d[...] = gr

    in_specs = [_ispec(a.block, a.imap, n, rev) for a in list(args) + list(cots)]
    in_specs += [_save_spec(s.shape[1:], n, rev) for s in saved]
    in_specs += [pl.BlockSpec(memory_space=pl.ANY) for _ in aliases]
    operands = [a.arr for a in list(args) + list(cots)] + list(saved) + [arr for _, arr in aliases]
    io_alias = {n_in + n_cot + n_c + j: k for j, (k, _) in enumerate(aliases)}
    return pl.pallas_call(
        body, name=name, grid=(n,),
        in_specs=in_specs,
        out_specs=[_ispec(o.block, o.imap, n, rev) for o in gouts],
        out_shape=[jax.ShapeDtypeStruct(o.shape, o.dtype) for o in gouts],
        scratch_shapes=[pltpu.VMEM(tuple(s.shape[1:]), f32) for s in saved],
        input_output_aliases=io_alias,
        compiler_params=_cparams(1),
    )(*operands)


class Op(NamedTuple):
    arr: Any
    row: int = 0
    col: int = 0


def mm(name, mode, pairs, m, n, tm, tn, out_dtype=f32, add=None, out=None, out_col=0, out_width=None, banded=False):
    tm, tn = min(tm, m), min(tn, n)
    assert m % tm == 0 and n % tn == 0
    step = 1 if banded and mode != TN else 0
    in_specs, operands = [], []
    for a, b, k, _ in pairs:
        if mode == TN:
            assert a.row % k == 0 and a.col % tm == 0
            in_specs.append(pl.BlockSpec((k, tm), lambda j, i, a=a, k=k: (a.row // k, i + a.col // tm)))
        else:
            assert a.col % k == 0 and a.row % tm == 0
            in_specs.append(pl.BlockSpec((tm, k), lambda j, i, a=a, k=k: (i + a.row // tm, a.col // k + step * j)))
        if mode == NT:
            assert b.col % k == 0 and b.row % tn == 0
            in_specs.append(pl.BlockSpec((tn, k), lambda j, i, b=b, k=k: (j + b.row // tn, b.col // k + step * j)))
        else:
            assert b.row % k == 0 and b.col % tn == 0
            in_specs.append(pl.BlockSpec((k, tn), lambda j, i, b=b, k=k: (b.row // k + step * j, j + b.col // tn)))
        operands += [a.arr, b.arr]
    n_p = len(pairs)
    if add is not None:
        assert add.col % tn == 0
        in_specs.append(pl.BlockSpec((tm, tn), lambda j, i: (i, j + add.col // tn)))
        operands.append(add.arr)
    io_alias = {}
    if out is not None:
        assert out_col % tn == 0
        in_specs.append(pl.BlockSpec(memory_space=pl.ANY))
        operands.append(out)
        io_alias = {len(operands) - 1: 0}
        out_shape = jax.ShapeDtypeStruct(out.shape, out.dtype)
    else:
        out_shape = jax.ShapeDtypeStruct((m, out_width or n), out_dtype)
    signs = [p[3] for p in pairs]

    def body(*refs):
        o = refs[-1]

        def product():
            acc = None
            for p in range(n_p):
                t = _bd(refs[2 * p][...], refs[2 * p + 1][...], mode)
                t = t if signs[p] > 0 else -t
                acc = t if acc is None else acc + t
            if add is not None:
                acc = acc + refs[2 * n_p][...].astype(f32)
            o[...] = acc.astype(o.dtype)

        if banded and mode == TN:
            on_diagonal = pl.program_id(0) == pl.program_id(1)
            pl.when(on_diagonal)(product)

            @pl.when(jnp.logical_not(on_diagonal))
            def _():
                o[...] = jnp.zeros(o.shape, o.dtype)
        else:
            product()

    return pl.pallas_call(
        body, name=name, grid=(n // tn, m // tm),
        in_specs=in_specs,
        out_specs=pl.BlockSpec((tm, tn), lambda j, i: (i, j + out_col // tn)),
        out_shape=out_shape, input_output_aliases=io_alias,
        compiler_params=_cparams(2),
    )(*operands)


SCAN_LANES = 512
SUBL = 8


def _cmul(p, q):
    return (p[0] * q[0] - p[1] * q[1], p[0] * q[1] + p[1] * q[0])


def _powers(a):
    a2 = _cmul(a, a)
    a4 = _cmul(a2, a2)
    a6 = _cmul(a4, a2)
    return [a, a2, _cmul(a2, a), a4, _cmul(a4, a), a6, _cmul(a6, a), _cmul(a4, a4)]


def _table(pw, order, w):
    row = _rows((SUBL, w))
    re = sum(jnp.where(row == t, pw[k][0], 0.0) for t, k in enumerate(order))
    im = sum(jnp.where(row == t, pw[k][1], 0.0) for t, k in enumerate(order))
    return re, im


def _pick_row(x, t):
    return jnp.sum(jnp.where(_rows(x.shape) == t, x, 0.0), axis=0, keepdims=True)


def s5_scan_fwd(bu_re, bu_im, a_re, a_im):
    seq, ns = bu_re.shape
    w, nb = SCAN_LANES, ns // SCAN_LANES

    def body(b_re, b_im, ar, ai, s_re, s_im):
        a = (ar[...], ai[...])
        pw = _powers(a)
        tab = _table(pw, list(range(SUBL)), w)
        row = _rows((SUBL, w))

        def step(i, carry):
            t0 = pl.multiple_of(i * SUBL, SUBL)
            x = (b_re[pl.ds(t0, SUBL), :], b_im[pl.ds(t0, SUBL), :])
            for d, k in ((1, 0), (2, 1), (4, 3)):
                sh = (jnp.where(row < d, 0.0, pltpu.roll(x[0], d, 0)), jnp.where(row < d, 0.0, pltpu.roll(x[1], d, 0)))
                t = _cmul(pw[k], sh)
                x = (x[0] + t[0], x[1] + t[1])
            t = _cmul(tab, carry)
            x = (x[0] + t[0], x[1] + t[1])
            s_re[pl.ds(t0, SUBL), :] = x[0]
            s_im[pl.ds(t0, SUBL), :] = x[1]
            return (x[0][SUBL - 1:, :], x[1][SUBL - 1:, :])

        z = jnp.zeros((1, w), f32)
        lax.fori_loop(0, seq // SUBL, step, (z, z), unroll=2)

    strip = pl.BlockSpec((seq, w), lambda j: (0, j))
    lane = pl.BlockSpec((1, w), lambda j: (0, j))
    return pl.pallas_call(
        body, name="s5_scan_fwd", grid=(nb,),
        in_specs=[strip, strip, lane, lane],
        out_specs=[strip, strip],
        out_shape=[jax.ShapeDtypeStruct((seq, ns), f32)] * 2,
        compiler_params=_cparams(1),
    )(bu_re, bu_im, a_re, a_im)


def s5_scan_bwd(ds_re, ds_im, s_re, s_im, a_re, a_im):
    seq, ns = ds_re.shape
    w, nb = SCAN_LANES, ns // SCAN_LANES
    nblk = seq // SUBL

    def body(g_re, g_im, sr, si, ar, ai, l_re, l_im, da_re, da_im):
        a = (ar[...], -ai[...])
        pw = _powers(a)
        tab = _table(pw, [SUBL - 1 - t for t in range(SUBL)], w)
        row = _rows((SUBL, w))

        def step(kk, carry):
            c_re, c_im, acc_re, acc_im = carry
            i = nblk - 1 - kk
            t0 = pl.multiple_of(i * SUBL, SUBL)
            x = (g_re[pl.ds(t0, SUBL), :], g_im[pl.ds(t0, SUBL), :])
            for d, k in ((1, 0), (2, 1), (4, 3)):
                sh = (jnp.where(row >= SUBL - d, 0.0, pltpu.roll(x[0], SUBL - d, 0)),
                      jnp.where(row >= SUBL - d, 0.0, pltpu.roll(x[1], SUBL - d, 0)))
                t = _cmul(pw[k], sh)
                x = (x[0] + t[0], x[1] + t[1])
            t = _cmul(tab, (c_re, c_im))
            x = (x[0] + t[0], x[1] + t[1])
            l_re[pl.ds(t0, SUBL), :] = x[0]
            l_im[pl.ds(t0, SUBL), :] = x[1]
            tp = jnp.maximum(t0 - 1, 0)
            live = (i > 0).astype(f32)
            p_re = sr[pl.ds(tp, 1), :] * live
            p_im = si[pl.ds(tp, 1), :] * live
            sp_re = jnp.where(row == 0, p_re, pltpu.roll(sr[pl.ds(t0, SUBL), :], 1, 0))
            sp_im = jnp.where(row == 0, p_im, pltpu.roll(si[pl.ds(t0, SUBL), :], 1, 0))
            acc_re = acc_re + x[0] * sp_re + x[1] * sp_im
            acc_im = acc_im + x[1] * sp_re - x[0] * sp_im
            return (x[0][:1, :], x[1][:1, :], acc_re, acc_im)

        z1 = jnp.zeros((1, w), f32)
        z8 = jnp.zeros((SUBL, w), f32)
        _, _, acc_re, acc_im = lax.fori_loop(0, nblk, step, (z1, z1, z8, z8), unroll=2)
        da_re[...] = jnp.sum(acc_re, axis=0, keepdims=True)
        da_im[...] = jnp.sum(acc_im, axis=0, keepdims=True)

    strip = pl.BlockSpec((seq, w), lambda j: (0, j))
    lane = pl.BlockSpec((1, w), lambda j: (0, j))
    return pl.pallas_call(
        body, name="s5_scan_bwd", grid=(nb,),
        in_specs=[strip, strip, strip, strip, lane, lane],
        out_specs=[strip, strip, lane, lane],
        out_shape=[jax.ShapeDtypeStruct((seq, ns), f32)] * 2 + [jax.ShapeDtypeStruct((1, ns), f32)] * 2,
        compiler_params=_cparams(1),
    )(ds_re, ds_im, s_re, s_im, a_re, a_im)


def _rms(x, w):
    return x * lax.rsqrt(jnp.mean(x * x, axis=-1, keepdims=True) + EPS) * w


def f_rms(x, w):
    return (_rms(x, w),)


def f_rms_res(x, w):
    return (_rms(x, w), x)


def f_s5_prep(lam_re, lam_im, log_step, b_re, b_im):
    e = _expand(log_step.shape[1], S5_NS, S5_N)
    step = hdot(jnp.exp(log_step), e)
    mag = jnp.exp(lam_re * step)
    ab_re, ab_im = mag * jnp.cos(lam_im * step), mag * jnp.sin(lam_im * step)
    den = lam_re * lam_re + lam_im * lam_im
    nr = ab_re - 1.0
    coef_re = (nr * lam_re + ab_im * lam_im) / den
    coef_im = (ab_im * lam_re - nr * lam_im) / den
    bb_re, bb_im = coef_re * b_re - coef_im * b_im, coef_re * b_im + coef_im * b_re
    sel = (_rows((BW, S5_P)) % S5_P == _cols((BW, S5_P))).astype(f32)
    blk = _rows((BW, S5_NS)) // S5_P == _cols((BW, S5_NS)) // S5_N
    rows_bd = lambda t: jnp.where(blk, hdot(sel, t), 0.0)
    return (ab_re, ab_im, rows_bd(bb_re), rows_bd(bb_im))


def f_s5_c(c_re, c_im):
    sel_t = (_cols((S5_P, BW)) % S5_P == _rows((S5_P, BW))).astype(f32)
    blk_t = _rows((S5_NS, BW)) // S5_N == _cols((S5_NS, BW)) // S5_P
    cols_bd = lambda t: jnp.where(blk_t, hdot_tn(t, sel_t), 0.0)
    return (cols_bd(c_re), cols_bd(c_im))


def f_s5_act(y_lin, u, gate, d, w_glu):
    y = jax.nn.gelu(y_lin + d * u)
    y = y * jax.nn.sigmoid(bdot_w(y, *w_glu))
    return (y * jax.nn.silu(gate),)


def f_sgu(u, v, gate, ln_w, ln_b, *rest):
    w_s, b_pad = rest[:8], rest[8]
    t = u.shape[0]
    u32, v32 = jax.nn.gelu(u), jax.nn.gelu(v)
    mu = jnp.mean(v32, axis=-1, keepdims=True)
    var = jnp.mean(jnp.square(v32 - mu), axis=-1, keepdims=True)
    vn = (v32 - mu) * lax.rsqrt(var + EPS) * ln_w + ln_b
    tri = _rows((t, t)) >= _cols((t, t))
    s = hdot_tn(b_pad, _expand(LANES, BW, BW // 8))
    for h in range(8):
        s = s + bdot(jnp.where(tri, w_s[h], 0.0), vn) * _lane_mask(BW, 64 * h, 64 * h + 64)
    return (u32 * s * jax.nn.silu(gate),)


def f_m2_conv(x, w, b):
    return (sum(_row_of(w, k) * shift(x, 3 - k) for k in range(4)) + b,)


def f_sc(bg, cg, h, gate, w):
    z = cg * h
    conv = sum(_row_of(w, k) * shift(z, 2 - k) for k in range(3))
    return (bg * conv * jax.nn.silu(gate),)


def f_m2(z, xc, b0, b1, c0, c1, dt_raw, dt_bias, a_log, d_par, norm_w, st):
    q = z.shape[0]
    x = jax.nn.silu(xc)
    bm, cm = (jax.nn.silu(b0), jax.nn.silu(b1)), (jax.nn.silu(c0), jax.nn.silu(c1))
    dt = jax.nn.softplus(dt_raw + dt_bias)
    da = dt * (-jnp.exp(a_log))
    tri = _rows((q, q)) >= _cols((q, q))
    acs = hdot(tri.astype(f32), da)
    e = _expand(LANES, BW, M2_HEAD_DIM)
    dt_f, acs_f = hdot(dt, e), hdot(acs, e)
    last = _rows((q, BW)) == q - 1
    alast_f = jnp.sum(jnp.where(last, acs_f, 0.0), axis=0, keepdims=True)
    xdt = x * dt_f
    xdec = xdt * jnp.exp(alast_f - acs_f)
    acs_t = acs.T
    st_new = st * jnp.exp(alast_f)
    y_diag, y_off = 0.0, 0.0
    for g in range(M2_GROUPS):
        gm = _lane_mask(BW, 256 * g, 256 * g + 256)
        cb = bdot_nt(cm[g], bm[g])
        st_new = st_new + bdot_tn(bm[g], xdec * gm)
        y_off = y_off + bdot(cm[g], st) * gm
        for hh in range(M2_HEADS // M2_GROUPS):
            h = g * (M2_HEADS // M2_GROUPS) + hh
            col = jnp.sum(jnp.where(_cols((q, LANES)) == h, acs, 0.0), axis=1, keepdims=True)
            row = jnp.sum(jnp.where(_rows((LANES, q)) == h, acs_t, 0.0), axis=0, keepdims=True)
            decay = jnp.exp(jnp.where(tri, col - row, -1e30))
            y_diag = y_diag + bdot(cb * decay, xdt) * _lane_mask(BW, 64 * h, 64 * h + 64)
    d_f = sum(jnp.sum(jnp.where(_cols((1, LANES)) == h, d_par, 0.0), axis=1, keepdims=True)
              * _lane_mask(BW, 64 * h, 64 * h + 64) for h in range(M2_HEADS))
    y = y_diag + y_off * jnp.exp(acs_f) + d_f * x
    y = y * jax.nn.silu(z)
    return (_rms(y, norm_w), st_new)


def f_branch_mix(*v):
    ys, lg, wb, mb = v[0:4], v[4:8], v[8:12], v[12:16]
    return (sum(jax.nn.sigmoid(lg[k] + mb[k]) * bdot_w(ys[k], *wb[k]) for k in range(N_BRANCH)),)


def _full(shape):
    nd = len(shape)
    return dict(block=tuple(shape), imap=lambda i: (0,) * nd)


def _param(arr, shadow=False):
    return A(arr, tuple(arr.shape), lambda i, nd=arr.ndim: (0,) * nd, shadow)


def _tb(arr, t, width, colblk):
    return A(arr, (t, width), lambda i: (i, colblk))


def _strip(arr, seq, colblk0, stride=1):
    return A(arr, (seq, LANES), lambda j: (0, colblk0 + stride * j))


def _s5_prep_args(p):
    lam_re = p["s5_lambda_re"].reshape(1, S5_NS)
    lam_im = p["s5_lambda_im"].reshape(1, S5_NS)
    log_step = jnp.pad(p["s5_log_step"].reshape(1, S5_GROUPS), ((0, 0), (0, LANES - S5_GROUPS)))
    b_lanes = lambda b: jnp.transpose(b, (2, 0, 1)).reshape(S5_P, S5_NS)
    return [_param(v) for v in (lam_re, lam_im, log_step, b_lanes(p["s5_b_re"]), b_lanes(p["s5_b_im"]))]


def _s5_c_args(p):
    c_lanes = lambda c: jnp.transpose(c, (1, 0, 2)).reshape(S5_P, S5_NS)
    return [_param(c_lanes(p["s5_c_re"])), _param(c_lanes(p["s5_c_im"]))]


def layer_fwd(x, p):
    seq = x.shape[0]
    nt = seq // CHUNK
    t2 = 256
    sv = {}
    hb = block_fwd("rms_in", f_rms, seq // t2, [_tb(x, t2, D_MODEL, 0), _param(p["norm_w"].reshape(1, D_MODEL))],
                   [O((seq, D_MODEL), bf16, (t2, D_MODEL), lambda i: (i, 0))])[0]
    proj = mm("mm_in", NN, [(Op(hb), Op(p["w_in"]), D_MODEL, 1)], seq, PW, 1024, 1152)
    whole = lambda shape, dt: O(shape, dt, shape, lambda i: (0, 0))
    ab_re, ab_im, bb_re, bb_im = block_fwd("s5_prep", f_s5_prep, 1, _s5_prep_args(p),
                                           [whole((1, S5_NS), f32)] * 2 + [whole((BW, S5_NS), bf16)] * 2)
    cc_re, cc_im = block_fwd("s5_c", f_s5_c, 1, _s5_c_args(p), [whole((S5_NS, BW), bf16)] * 2)
    bu_re = mm("mm_bu_re", NN, [(Op(proj, 0, C_S5U), Op(bb_re), LANES, 1)], seq, S5_NS, 512, 512, banded=True)
    bu_im = mm("mm_bu_im", NN, [(Op(proj, 0, C_S5U), Op(bb_im), LANES, 1)], seq, S5_NS, 512, 512, banded=True)
    s_re, s_im = s5_scan_fwd(bu_re, bu_im, ab_re, ab_im)
    y_lin = mm("mm_s5y", NN, [(Op(s_re), Op(cc_re), 512, 1), (Op(s_im), Op(cc_im), 512, -1)], seq, BW, 512, LANES, banded=True)
    s5_act_args = [_tb(y_lin, CHUNK, BW, 0), _tb(proj, CHUNK, BW, C_S5U // BW), _tb(proj, CHUNK, BW, C_S5G // BW),
                   _param(p["s5_d"].reshape(1, BW)), _param(p["s5_w_glu"], True)]
    out_bw = O((seq, BW), f32, (CHUNK, BW), lambda i: (i, 0))
    y_a = block_fwd("s5_act", f_s5_act, nt, s5_act_args, [out_bw])[0]
    y_b = block_fwd("sgu", f_sgu, nt, _sgu_args(proj, p), [out_bw])[0]
    xc = block_fwd("m2_conv", f_m2_conv, 8, _m2_conv_args(proj, p, seq),
                   [O((seq, 2 * BW), f32, (seq, LANES), lambda j: (0, j))])[0]
    y_c, st_saved = block_fwd("m2_ssd", f_m2, nt, _m2_args(proj, xc, p), [out_bw], carries=[(M2_STATE, BW)])
    y_d = block_fwd("sc", f_sc, 4, _sc_args(proj, p, seq), [O((seq, BW), f32, (seq, LANES), lambda j: (0, j))])[0]
    ys = [y_a, y_b, y_c, y_d]
    merged = block_fwd("branch_mix", f_branch_mix, seq // t2, _mix_args(ys, proj, p, t2),
                       [O((seq, D_MODEL), bf16, (t2, D_MODEL), lambda i: (i, 0))])[0]
    x_new = mm("mm_out", NN, [(Op(merged), Op(p["w_out"]), D_MODEL, 1)], seq, D_MODEL, 512, D_MODEL, add=Op(x))
    sv.update(x=x, hb=hb, proj=proj, ab=(ab_re, ab_im), bb=(bb_re, bb_im), cc=(cc_re, cc_im), s=(s_re, s_im), y_lin=y_lin,
              xc=xc, st=st_saved, ys=ys, merged=merged)
    return x_new, sv


def _sgu_args(proj, p):
    c0 = C_SGU // BW
    args = [_tb(proj, CHUNK, BW, c0), _tb(proj, CHUNK, BW, c0 + 1), _tb(proj, CHUNK, BW, c0 + 2),
            _param(p["sgu_ln_w"].reshape(1, BW)), _param(p["sgu_ln_b"].reshape(1, BW))]
    args += [A(p["sgu_w"], (None, CHUNK, CHUNK), lambda i, h=h: (h, 0, 0)) for h in range(8)]
    args.append(_param(jnp.pad(p["sgu_b"], ((0, LANES - 8), (0, 0)))))
    return args


def _m2_conv_args(proj, p, seq):
    return [_strip(proj, seq, C_XBC // LANES), A(p["m2_conv_w"], (4, LANES), lambda j: (0, j)),
            A(p["m2_conv_b"].reshape(1, 2 * BW), (1, LANES), lambda j: (0, j))]


def _pad_lanes(v):
    return jnp.pad(v.reshape(1, -1), ((0, 0), (0, LANES - v.size)))


def _m2_args(proj, xc, p):
    args = [_tb(proj, CHUNK, BW, C_M2Z // BW), _tb(xc, CHUNK, BW, 0)]
    args += [_tb(xc, CHUNK, LANES, 4 + k) for k in range(4)]
    args.append(_tb(proj, CHUNK, LANES, C_DT // LANES))
    args += [_param(_pad_lanes(p["m2_dt_bias"])), _param(_pad_lanes(p["m2_a_log"])), _param(_pad_lanes(p["m2_d"])),
             _param(p["m2_norm_w"].reshape(1, BW))]
    return args


def _sc_args(proj, p, seq):
    c0 = C_SC // LANES
    return [_strip(proj, seq, c0 + k, 4) for k in range(4)] + [A(p["sc_conv_w"], (3, LANES), lambda j: (0, j))]


def _mix_args(ys, proj, p, t):
    args = [_tb(ys[k], t, BW, 0) for k in range(N_BRANCH)]
    args += [_tb(proj, t, D_MODEL, k) for k in range(N_BRANCH)]
    args += [A(p["w_branch"], (BW, D_MODEL), lambda i, k=k: (k, 0), True) for k in range(N_BRANCH)]
    mb = p["merge_b"].reshape(N_BRANCH, 1, D_MODEL)
    args += [A(mb, (None, 1, D_MODEL), lambda i, k=k: (k, 0, 0)) for k in range(N_BRANCH)]
    return args


def layer_bwd(d_out, p, sv):
    seq = d_out.shape[0]
    nt = seq // CHUNK
    t2 = 256
    proj, ys = sv["proj"], sv["ys"]
    g = {}
    acc = lambda shape: O(tuple(shape), f32, tuple(shape), lambda i, nd=len(shape): (0,) * nd)
    d_merged = mm("mm_out_dx", NT, [(Op(d_out), Op(p["w_out"]), D_MODEL, 1)], seq, D_MODEL, 512, D_MODEL, out_dtype=f32)
    g["w_out"] = mm("mm_out_dw", TN, [(Op(sv["merged"]), Op(d_out), seq, 1)], D_MODEL, D_MODEL, 256, D_MODEL, out_dtype=bf16)
    gouts = [O((seq, BW), f32, (CHUNK, BW), lambda i: (i, 0)) for _ in range(N_BRANCH)]
    gouts.append(O((seq, PW_MAIN), bf16, (CHUNK, N_BRANCH * D_MODEL), lambda i: (i, 0)))
    gouts += [acc((BW, D_MODEL)) for _ in range(N_BRANCH)] + [acc((1, D_MODEL)) for _ in range(N_BRANCH)]
    routes = [R(k, k) for k in range(N_BRANCH)] + [R(4 + k, 4, k * D_MODEL) for k in range(N_BRANCH)]
    routes += [R(8 + k, 5 + k, acc=True) for k in range(N_BRANCH)] + [R(12 + k, 9 + k, acc=True) for k in range(N_BRANCH)]
    res = block_bwd("branch_mix_bwd", f_branch_mix, nt, _mix_args(ys, proj, p, CHUNK),
                    [_tb(d_merged, CHUNK, D_MODEL, 0)], gouts, routes)
    dys, dproj = res[:4], res[4]
    g["w_branch"] = jnp.stack(res[5:9])
    g["merge_b"] = jnp.concatenate(res[9:13], axis=0)
    res = block_bwd("sc_bwd", f_sc, 4, _sc_args(proj, p, seq), [_strip(dys[3], seq, 0)],
                    [O((seq, PW_MAIN), bf16, (seq, 4 * LANES), lambda j: (0, C_SC // (4 * LANES) + j), alias=dproj),
                     O((3, BW), f32, (3, LANES), lambda j: (0, j))],
                    [R(k, 0, k * LANES) for k in range(4)] + [R(4, 1)])
    dproj, g["sc_conv_w"] = res
    res = block_bwd("m2_ssd_bwd", f_m2, nt, _m2_args(proj, sv["xc"], p), [_tb(dys[2], CHUNK, BW, 0)],
                    [O((seq, PW_MAIN), bf16, (CHUNK, BW), lambda i: (i, C_M2Z // BW), alias=dproj),
                     O((seq, 2 * BW), f32, (CHUNK, 2 * BW), lambda i: (i, 0)),
                     O((seq, LANES), bf16, (CHUNK, LANES), lambda i: (i, 0)),
                     acc((1, LANES)), acc((1, LANES)), acc((1, LANES)), acc((1, BW))],
                    [R(0, 0), R(1, 1, 0)] + [R(2 + k, 1, BW + k * LANES) for k in range(4)] + [R(6, 2)]
                    + [R(7, 3, acc=True), R(8, 4, acc=True), R(9, 5, acc=True), R(10, 6, acc=True)],
                    saved=[sv["st"]], rev=True)
    dproj, dxc, d_dt = res[0], res[1], res[2]
    g["m2_dt_bias"], g["m2_a_log"], g["m2_d"] = (r[0, :M2_HEADS] for r in res[3:6])
    g["m2_norm_w"] = res[6].reshape(BW)
    res = block_bwd("m2_conv_bwd", f_m2_conv, 8, _m2_conv_args(proj, p, seq), [_strip(dxc, seq, 0)],
                    [O((seq, PW_MAIN), bf16, (seq, LANES), lambda j: (0, C_XBC // LANES + j), alias=dproj),
                     O((4, 2 * BW), f32, (4, LANES), lambda j: (0, j)), O((1, 2 * BW), f32, (1, LANES), lambda j: (0, j))],
                    [R(0, 0), R(1, 1), R(2, 2)])
    dproj, g["m2_conv_w"], cb = res
    g["m2_conv_b"] = cb.reshape(2 * BW)
    res = block_bwd("sgu_bwd", f_sgu, nt, _sgu_args(proj, p), [_tb(dys[1], CHUNK, BW, 0)],
                    [O((seq, PW_MAIN), bf16, (CHUNK, 3 * BW), lambda i: (i, C_SGU // (3 * BW)), alias=dproj),
                     acc((1, BW)), acc((1, BW))] + [acc((CHUNK, CHUNK)) for _ in range(8)] + [acc((LANES, CHUNK))],
                    [R(0, 0, 0), R(1, 0, BW), R(2, 0, 2 * BW), R(3, 1, acc=True), R(4, 2, acc=True)]
                    + [R(5 + h, 3 + h, acc=True) for h in range(8)] + [R(13, 11, acc=True)])
    dproj = res[0]
    g["sgu_ln_w"], g["sgu_ln_b"] = res[1].reshape(BW), res[2].reshape(BW)
    g["sgu_w"] = jnp.stack(res[3:11])
    g["sgu_b"] = res[11][:8]
    y_lin, (s_re, s_im), (ab_re, ab_im) = sv["y_lin"], sv["s"], sv["ab"]
    s5_act_args = [_tb(y_lin, CHUNK, BW, 0), _tb(proj, CHUNK, BW, C_S5U // BW), _tb(proj, CHUNK, BW, C_S5G // BW),
                   _param(p["s5_d"].reshape(1, BW)), _param(p["s5_w_glu"], True)]
    res = block_bwd("s5_act_bwd", f_s5_act, nt, s5_act_args, [_tb(dys[0], CHUNK, BW, 0)],
                    [O((seq, BW), bf16, (CHUNK, BW), lambda i: (i, 0)), O((seq, BW), f32, (CHUNK, BW), lambda i: (i, 0)),
                     O((seq, PW_MAIN), bf16, (CHUNK, BW), lambda i: (i, C_S5G // BW), alias=dproj),
                     acc((1, BW)), acc((BW, BW))],
                    [R(0, 0), R(1, 1), R(2, 2), R(3, 3, acc=True), R(4, 4, acc=True)])
    dy_lin, du1, dproj = res[0], res[1], res[2]
    g["s5_d"] = res[3].reshape(S5_GROUPS, S5_P)
    g["s5_w_glu"] = res[4]
    (bb_re, bb_im), (cc_re, cc_im) = sv["bb"], sv["cc"]
    ds_re = mm("mm_s5y_dre", NT, [(Op(dy_lin), Op(cc_re), LANES, 1)], seq, S5_NS, 512, 512, banded=True)
    ds_im = mm("mm_s5y_dim", NT, [(Op(dy_lin), Op(cc_im), LANES, -1)], seq, S5_NS, 512, 512, banded=True)
    dc_re = mm("mm_s5y_dcre", TN, [(Op(s_re), Op(dy_lin), seq, 1)], S5_NS, BW, 512, LANES, banded=True)
    dc_im = mm("mm_s5y_dcim", TN, [(Op(s_im), Op(dy_lin), seq, -1)], S5_NS, BW, 512, LANES, banded=True)
    l_re, l_im, da_re, da_im = s5_scan_bwd(ds_re, ds_im, s_re, s_im, ab_re, ab_im)
    dproj = mm("mm_bu_dx", NT, [(Op(l_re), Op(bb_re), 512, 1), (Op(l_im), Op(bb_im), 512, 1)],
               seq, BW, 512, LANES, add=Op(du1), out=dproj, out_col=C_S5U, banded=True)
    dbb = [mm(f"mm_bu_dw{n}", TN, [(Op(proj, 0, C_S5U), Op(l), seq, 1)], BW, S5_NS, LANES, 512, banded=True)
           for n, l in (("re", l_re), ("im", l_im))]
    gouts = [acc((1, S5_NS)), acc((1, S5_NS)), acc((1, LANES))] + [acc((S5_P, S5_NS))] * 2
    res = block_bwd("s5_prep_bwd", f_s5_prep, 1, _s5_prep_args(p),
                    [_param(v) for v in (da_re, da_im, dbb[0], dbb[1])], gouts, [R(k, k, acc=True) for k in range(5)])
    g["s5_lambda_re"], g["s5_lambda_im"] = res[0].reshape(S5_GROUPS, S5_N), res[1].reshape(S5_GROUPS, S5_N)
    g["s5_log_step"] = res[2][0, :S5_GROUPS]
    b_natural = lambda b: jnp.transpose(b.reshape(S5_P, S5_GROUPS, S5_N), (1, 2, 0))
    c_natural = lambda c: jnp.transpose(c.reshape(S5_P, S5_GROUPS, S5_N), (1, 0, 2))
    g["s5_b_re"], g["s5_b_im"] = b_natural(res[3]), b_natural(res[4])
    res = block_bwd("s5_c_bwd", f_s5_c, 1, _s5_c_args(p), [_param(dc_re), _param(dc_im)],
                    [acc((S5_P, S5_NS))] * 2, [R(0, 0, acc=True), R(1, 1, acc=True)])
    g["s5_c_re"], g["s5_c_im"] = c_natural(res[0]), c_natural(res[1])
    hb, w_in = sv["hb"], p["w_in"]
    dh = mm("mm_in_dx", NT, [(Op(dproj), Op(w_in), PW_MAIN, 1), (Op(d_dt), Op(w_in, 0, PW_MAIN), LANES, 1)],
            seq, D_MODEL, 256, 512)
    dw = mm("mm_in_dw", TN, [(Op(hb), Op(dproj), seq, 1)], D_MODEL, PW_MAIN, 512, 1024, out_dtype=bf16, out_width=PW)
    g["w_in"] = mm("mm_in_dwdt", TN, [(Op(hb), Op(d_dt), seq, 1)], D_MODEL, LANES, 256, LANES, out=dw, out_col=C_DT)
    dx, dnw = block_bwd("rms_in_bwd", f_rms_res, seq // t2, [_tb(sv["x"], t2, D_MODEL, 0), _param(p["norm_w"].reshape(1, D_MODEL))],
                        [_tb(dh, t2, D_MODEL, 0), _tb(d_out, t2, D_MODEL, 0)],
                        [O((seq, D_MODEL), f32, (t2, D_MODEL), lambda i: (i, 0)), acc((1, D_MODEL))],
                        [R(0, 0), R(1, 1, acc=True)])
    g["norm_w"] = dnw.reshape(D_MODEL)
    return dx, g


def loss_head(x, w, target):
    seq = x.shape[0]
    t = 256

    def body(x_ref, w_ref, t_ref, loss_ref, dx_ref, dw_ref):
        step = pl.program_id(0)

        def f(xv, wv):
            err = _rms(xv, wv) - t_ref[...]
            return 0.5 * jnp.sum(jnp.mean(err * err, axis=-1, keepdims=True), axis=0, keepdims=True)

        val, vjp_fn = jax.vjp(f, x_ref[...], w_ref[...])
        dx, dw = vjp_fn(jnp.ones((1, 1), f32))
        dx_ref[...] = dx

        @pl.when(step == 0)
        def _():
            loss_ref[...] = jnp.broadcast_to(val, loss_ref.shape)
            dw_ref[...] = dw

        @pl.when(step > 0)
        def _():
            loss_ref[...] += jnp.broadcast_to(val, loss_ref.shape)
            dw_ref[...] += dw

    blk = pl.BlockSpec((t, D_MODEL), lambda i: (i, 0))
    row = pl.BlockSpec((1, D_MODEL), lambda i: (0, 0))
    return pl.pallas_call(
        body, name="loss_head", grid=(seq // t,),
        in_specs=[blk, row, blk],
        out_specs=[pl.BlockSpec((1, LANES), lambda i: (0, 0)), blk, row],
        out_shape=[jax.ShapeDtypeStruct((1, LANES), f32), jax.ShapeDtypeStruct((seq, D_MODEL), f32),
                   jax.ShapeDtypeStruct((1, D_MODEL), f32)],
        compiler_params=_cparams(1),
    )(x, w.reshape(1, D_MODEL), target)


LAYER_KEYS = ("norm_w", "w_in", "s5_lambda_re", "s5_lambda_im", "s5_b_re", "s5_b_im", "s5_c_re", "s5_c_im", "s5_d",
              "s5_log_step", "s5_w_glu", "sgu_ln_w", "sgu_ln_b", "sgu_w", "sgu_b", "m2_conv_w", "m2_conv_b",
              "m2_dt_bias", "m2_a_log", "m2_d", "m2_norm_w", "sc_conv_w", "merge_b", "w_branch", "w_out")


def local_step(x, target, layers, final_norm_w):
    saved = []
    for p in layers:
        x, sv = layer_fwd(x, p)
        saved.append(sv)
    loss, dx, dfw = loss_head(x, final_norm_w, target)
    grads = []
    for p, sv in zip(reversed(layers), reversed(saved)):
        dx, g = layer_bwd(dx, p, sv)
        grads.append(g)
    return loss[0, 0], dx, grads[::-1], dfw.reshape(D_MODEL)


MESH = pl.DeviceIdType.MESH
ANY = pl.BlockSpec(memory_space=pl.ANY)


def _me():
    return lax.axis_index("x"), lax.axis_index("y"), lax.axis_index("c")


def _other_chips(x, y):
    return [(1 - x, y), (x, 1 - y), (1 - x, 1 - y)]


def _rcopy(src, dst, send, recv, dev):
    return pltpu.make_async_remote_copy(src_ref=src, dst_ref=dst, send_sem=send, recv_sem=recv,
                                        device_id=dev, device_id_type=MESH)


def _route_cut(rows, dtype):
    tile = 2 * SUBL * (4 // jnp.dtype(dtype).itemsize)
    return rows // 2 if rows % tile == 0 else rows


def _comm_call(name, body, arrs, out_shape, n_remote, aliases=None):
    n = len(arrs)
    return pl.pallas_call(
        body, name=name, in_specs=[ANY] * n, out_specs=[ANY] * len(out_shape), out_shape=out_shape,
        scratch_shapes=[pltpu.SemaphoreType.DMA((n, n_remote)), pltpu.SemaphoreType.DMA((n, n_remote))],
        input_output_aliases=aliases or {},
        compiler_params=pltpu.CompilerParams(has_side_effects=True),
    )(*arrs)


def gather_chips(name, arrs):
    n = len(arrs)
    cut = [_route_cut(a.shape[1], a.dtype) for a in arrs]

    def body(*refs):
        ins, outs = refs[:n], refs[n:2 * n]
        send, recv = refs[2 * n:]
        x, y, c = _me()
        jme, jx, jy, jd = 2 * x + y, 2 * (1 - x) + y, 2 * x + 1 - y, 2 * (1 - x) + 1 - y
        to_x, to_y, sib = (1 - x, y, c), (x, 1 - y, c), (x, y, 1 - c)

        def part(ref, a, hi):
            return ref.at[pl.ds(cut[a], ref.shape[0] - cut[a])] if hi else ref.at[pl.ds(0, cut[a])]

        def cp(a, k, ref, dev):
            return _rcopy(ref, ref, send.at[a, k], recv.at[a, k], dev)

        split = [a for a in range(n) if cut[a] < arrs[a].shape[1]]
        sent = [_rcopy(ins[a].at[c], outs[a].at[jme, c], send.at[a, k], recv.at[a, k], dev)
                for a in range(n) for k, dev in ((0, to_x), (1, to_y))]
        for s in sent:
            s.start()
        for a in range(n):
            blk = outs[a].at[jx, c]
            cp(a, 0, blk, to_x).wait_recv()
            sent += [cp(a, 2, part(blk, a, False), to_y), cp(a, 4, blk, sib)]
            sent[-2].start()
            sent[-1].start()
        for a in range(n):
            blk = outs[a].at[jy, c]
            cp(a, 1, blk, to_y).wait_recv()
            sent.append(cp(a, 5, blk, sib))
            sent[-1].start()
            if a in split:
                sent.append(cp(a, 3, part(blk, a, True), to_x))
                sent[-1].start()
        for a in range(n):
            lo = part(outs[a].at[jd, c], a, False)
            cp(a, 2, lo, to_y).wait_recv()
            sent.append(cp(a, 6, lo, sib))
            sent[-1].start()
        for a in split:
            hi = part(outs[a].at[jd, c], a, True)
            cp(a, 3, hi, to_x).wait_recv()
            sent.append(cp(a, 7, hi, sib))
            sent[-1].start()
        for a in range(n):
            cp(a, 4, outs[a].at[jx, 1 - c], sib).wait_recv()
            cp(a, 5, outs[a].at[jy, 1 - c], sib).wait_recv()
            cp(a, 6, part(outs[a].at[jd, 1 - c], a, False), sib).wait_recv()
        for a in split:
            cp(a, 7, part(outs[a].at[jd, 1 - c], a, True), sib).wait_recv()
        for s in sent:
            s.wait_send()

    out_shape = [jax.ShapeDtypeStruct((4,) + a.shape, a.dtype) for a in arrs]
    got = _comm_call(name, body, arrs, out_shape, 8)
    jme = 2 * lax.axis_index("x") + lax.axis_index("y")
    return [lax.dynamic_update_index_in_dim(g, a, jme, 0) for g, a in zip(got, arrs)]


def swap_halves(name, arrs):
    n = len(arrs)

    def body(*refs):
        ins, outs = refs[:n], refs[n:2 * n]
        send, recv = refs[2 * n:]
        x, y, c = _me()
        remote = [_rcopy(ins[a].at[1 - c], outs[a], send.at[a, 0], recv.at[a, 0], (x, y, 1 - c)) for a in range(n)]
        for cp in remote:
            cp.start()
        for cp in remote:
            cp.wait()

    return _comm_call(name, body, arrs, [jax.ShapeDtypeStruct(a.shape[1:], a.dtype) for a in arrs], 1)


def exchange_chips(name, arrs):
    n = len(arrs)

    def body(*refs):
        ins, outs = refs[:n], refs[n:2 * n]
        send, recv = refs[2 * n:]
        x, y, c = _me()
        remote = [_rcopy(ins[a].at[2 * cx + cy], outs[a].at[k], send.at[a, k], recv.at[a, k], (cx, cy, c))
                  for a in range(n) for k, (cx, cy) in enumerate(_other_chips(x, y))]
        for cp in remote:
            cp.start()
        for cp in remote:
            cp.wait()

    return _comm_call(name, body, arrs, [jax.ShapeDtypeStruct((3,) + a.shape[1:], a.dtype) for a in arrs], 3)


def gather_cores(name, arrs):
    n = len(arrs)

    def body(*refs):
        bufs = refs[n:2 * n]
        send, recv = refs[2 * n:]
        x, y, c = _me()
        remote = [_rcopy(bufs[a].at[c], bufs[a].at[c], send.at[a, 0], recv.at[a, 0], (x, y, 1 - c)) for a in range(n)]
        for cp in remote:
            cp.start()
        for a in range(n):
            _rcopy(bufs[a].at[1 - c], bufs[a].at[1 - c], send.at[a, 0], recv.at[a, 0], (x, y, 1 - c)).wait_recv()
        for cp in remote:
            cp.wait_send()

    return _comm_call(name, body, arrs, [jax.ShapeDtypeStruct(a.shape, a.dtype) for a in arrs], 1,
                      aliases={a: a for a in range(n)})


ROW_BLOCK = 256


def esum(name, terms, rows, width, out_dtype, out_slots=None):
    tr = next((t for t in range(min(rows, ROW_BLOCK), 0, -SUBL) if rows % t == 0 and t % SUBL == 0), rows)
    where =jnp.stack([lax.axis_index("c"), 2 * lax.axis_index("x") + lax.axis_index("y")]).astype(jnp.int32)
    pick = {"c": 0, "j": 1}

    def body(s_ref, *refs):
        acc = refs[0][...].astype(f32)
        for r in refs[1:-1]:
            acc = acc + r[...].astype(f32)
        refs[-1][...] = acc.astype(out_dtype)

    specs = []
    for arr, lead in terms:
        if lead is None:
            specs.append(pl.BlockSpec((tr, width), lambda i, s: (i, 0)))
        elif isinstance(lead, str):
            specs.append(pl.BlockSpec((None, tr, width), lambda i, s, lead=lead: (s[pick[lead]], i, 0)))
        else:
            specs.append(pl.BlockSpec((None, tr, width), lambda i, s, lead=lead: (lead, i, 0)))
    if out_slots is None:
        out_spec = pl.BlockSpec((tr, width), lambda i, s: (i, 0))
        out_shape = jax.ShapeDtypeStruct((rows, width), out_dtype)
    else:
        out_spec = pl.BlockSpec((None, tr, width), lambda i, s: (s[0], i, 0))
        out_shape = jax.ShapeDtypeStruct((out_slots, rows, width), out_dtype)
    return pl.pallas_call(
        body, name=name,
        grid_spec=pltpu.PrefetchScalarGridSpec(num_scalar_prefetch=1, grid=(rows // tr,), in_specs=specs, out_specs=out_spec),
        out_shape=out_shape, compiler_params=_cparams(1),
    )(where, *[t[0] for t in terms])


def reduce_to_shards(parts):
    tags = [str(k) for k in range(len(parts))]
    theirs = swap_halves("rs_swap", parts)
    t1 = []
    for tag, p, th in zip(tags, parts, theirs):
        _, _, h, w = p.shape
        t1.append(esum("rs_add_cores" + tag, [(p.reshape(2, 4 * h, w), "c"), (th.reshape(4 * h, w), None)],
                       4 * h, w, p.dtype).reshape(4, h, w))
    landed = exchange_chips("rs_exchange", t1)
    red = []
    for tag, p, t, got in zip(tags, parts, t1, landed):
        _, _, h, w = p.shape
        red.append(esum("rs_add_chips" + tag, [(t, "j"), (got, 0), (got, 1), (got, 2)], h, w, f32, out_slots=2))
    return gather_cores("rs_gather", red)


def _adamw_step(w_ref, g_ref, m_ref, v_ref, d_ref, nm_ref, nv_ref):
    gv = g_ref[...]
    nm = ADAM_B1 * m_ref[...] + (1.0 - ADAM_B1) * gv
    nv = ADAM_B2 * v_ref[...] + (1.0 - ADAM_B2) * jnp.square(gv)
    m_hat = nm / (1.0 - ADAM_B1 ** ADAM_STEP)
    v_hat = nv / (1.0 - ADAM_B2 ** ADAM_STEP)
    d_ref[...] = -ADAM_LR * (m_hat / (jnp.sqrt(v_hat) + ADAM_EPS) + ADAM_WD * w_ref[...])
    nm_ref[...] = nm
    nv_ref[...] = nv


def adamw(name, w, g, m, v, tr=None):
    rows, rest = w.shape[0], w.shape[1:]
    if tr is None:
        tr = ROW_BLOCK if rows % ROW_BLOCK == 0 else rows
    assert rows % tr == 0

    def body(*refs):
        _adamw_step(*refs)

    spec = pl.BlockSpec((tr,) + rest, lambda i: (i,) + (0,) * len(rest))
    return pl.pallas_call(
        body, name=name, grid=(rows // tr,), in_specs=[spec] * 4, out_specs=[spec] * 3,
        out_shape=[jax.ShapeDtypeStruct(w.shape, f32)] * 3, compiler_params=_cparams(1),
    )(w, g, m, v)


def adamw_many(name, ws, gs, ms, vs):
    n = len(ws)

    def body(*refs):
        ins, outs = refs[:4 * n], refs[4 * n:]
        for k in range(n):
            _adamw_step(ins[k], ins[n + k], ins[2 * n + k], ins[3 * n + k], outs[k], outs[n + k], outs[2 * n + k])

    vmem = pl.BlockSpec(memory_space=pltpu.VMEM)
    res = pl.pallas_call(
        body, name=name, in_specs=[vmem] * (4 * n), out_specs=[vmem] * (3 * n),
        out_shape=[jax.ShapeDtypeStruct(w.shape, f32) for w in ws] * 3,
        compiler_params=pltpu.CompilerParams(vmem_limit_bytes=VMEM_LIMIT),
    )(*ws, *gs, *ms, *vs)
    return res[:n], res[n:2 * n], res[2 * n:]


PACK_W = 1024


def _pack(parts, halves, row_mult):
    flat = jnp.concatenate([p.reshape(-1) for p in parts])
    per = halves * row_mult * PACK_W
    total = -(-flat.size // per) * per
    flat = jnp.pad(flat, (0, total - flat.size))
    return flat.reshape(halves, total // (halves * PACK_W), PACK_W)


def _unpack(flat, shapes):
    out, pos = [], 0
    for s in shapes:
        n = int(np.prod(s))
        out.append(flat[pos:pos + n].reshape(s))
        pos += n
    return out


SHARDED_BIG = ("w_in", "w_branch", "w_out", "s5_w_glu")
SHARDED_SMALL = ("m2_conv_w", "sc_conv_w", "merge_b")
SHARD_AXIS = {"w_in": 2, "w_branch": 3, "w_out": 1, "s5_w_glu": 1, "m2_conv_w": 2, "sc_conv_w": 2, "merge_b": 2}
REPLICATED = ("norm_w", "s5_lambda_re", "s5_lambda_im", "s5_b_re", "s5_b_im", "s5_c_re", "s5_c_im", "s5_d", "s5_log_step",
              "sgu_ln_w", "sgu_ln_b", "sgu_w", "sgu_b", "m2_conv_b", "m2_dt_bias", "m2_a_log", "m2_d", "m2_norm_w")
WEIGHTS = ("norm_w", "w_in", "s5_lambda_re", "s5_lambda_im", "s5_b_re", "s5_b_im", "s5_c_re", "s5_c_im", "s5_d",
           "s5_log_step", "s5_w_glu", "sgu_ln_w", "sgu_ln_b", "sgu_w", "sgu_b", "m2_conv_w", "m2_conv_b", "m2_dt_bias",
           "m2_a_log", "m2_d", "m2_norm_w", "sc_conv_w", "merge_b", "w_branch", "w_out", "final_norm_w")
N_LAYERS = 2


SHARD_W = IN_DIM // 4
SHARD_PAD = -(-SHARD_W // LANES) * LANES
REGROUP_W = 3 * LANES


def _kernel_pieces():
    out, pos = [], 0
    for s, n in _col_segments():
        while n:
            take = min(n, SHARD_W - s % SHARD_W)
            out.append((pos, s, take))
            pos, s, n = pos + take, s + take, n - take
    return out


def regroup_cols(name, src, steps, out_shape, out=None):
    n_src, rows, width = src.shape
    ow, win = REGROUP_W, REGROUP_W + LANES
    k_max = max(len(p) for _, _, p in steps)
    assert width % LANES == 0

    def body(*refs):
        src_ref, out_ref = refs[0], refs[-5]
        wbuf, obuf, sem_in, sem_out = refs[-4:]

        def fetch(q, slot):
            started = []
            for p, (s, col, lo, hi) in enumerate(steps[q][2]):
                w0 = col // LANES * LANES
                wlen = min(win, width - w0)
                cp = pltpu.make_async_copy(src_ref.at[s, :, pl.ds(w0, wlen)], wbuf.at[slot, p, :, pl.ds(0, wlen)], sem_in.at[slot, p])
                cp.start()
                started.append((cp, wlen, col - w0 - lo, lo, hi))
            return started

        pend, writes = fetch(0, 0), [None, None]
        for q, (t, b, _) in enumerate(steps):
            slot = q % 2
            nxt = fetch(q + 1, 1 - slot) if q + 1 < len(steps) else []
            acc = [jnp.zeros((rows, LANES), f32) for _ in range(ow // LANES)]
            for p, (cp, wlen, shift, lo, hi) in enumerate(pend):
                cp.wait()
                for k in range(ow // LANES):
                    c_lo, c_hi = max(lo, k * LANES), min(hi, (k + 1) * LANES)
                    if c_lo >= c_hi:
                        continue
                    wb = (c_lo + shift) // LANES
                    alen = min(2 * LANES, wlen - wb * LANES)
                    a = wbuf[slot, p, :, wb * LANES:wb * LANES + alen]
                    r, c = _rows((alen, LANES)), _cols((alen, LANES))
                    sh = shift + (k - wb) * LANES
                    sel = (r == c + sh) & (c >= c_lo - k * LANES) & (c < c_hi - k * LANES)
                    acc[k] = acc[k] + jnp.dot(a, sel.astype(bf16), preferred_element_type=f32)
            if writes[slot] is not None:
                writes[slot].wait()
            for k in range(ow // LANES):
                obuf[slot, :, k * LANES:(k + 1) * LANES] = acc[k].astype(bf16)
            writes[slot] = pltpu.make_async_copy(obuf.at[slot], out_ref.at[t, :, pl.ds(b * ow, ow)], sem_out.at[slot])
            writes[slot].start()
            pend = nxt
        for wr in writes:
            if wr is not None:
                wr.wait()

    operands, io_alias = [src], {}
    if out is not None:
        operands.append(out)
        io_alias = {1: 0}
    return pl.pallas_call(
        body, name=name, in_specs=[ANY] * len(operands), out_specs=ANY,
        out_shape=jax.ShapeDtypeStruct(out_shape, bf16), input_output_aliases=io_alias,
        scratch_shapes=[pltpu.VMEM((2, k_max, rows, win), bf16), pltpu.VMEM((2, rows, ow), bf16),
                        pltpu.SemaphoreType.DMA((2, k_max)), pltpu.SemaphoreType.DMA((2,))],
        compiler_params=pltpu.CompilerParams(vmem_limit_bytes=VMEM_LIMIT),
    )(*operands)


def _steps_to_kernel_cols(layer):
    steps = []
    for o in range(PW // REGROUP_W):
        pieces = []
        for pos, s, n in _kernel_pieces():
            lo, hi = max(pos, o * REGROUP_W), min(pos + n, (o + 1) * REGROUP_W)
            if lo < hi:
                ref_col = s + lo - pos
                pieces.append(((ref_col // SHARD_W) * N_LAYERS + layer, ref_col % SHARD_W, lo - o * REGROUP_W, hi - o * REGROUP_W))
        steps.append((0, o, pieces))
    return steps


def _steps_to_shards(layer):
    steps = []
    for j in range(4):
        for b in range(SHARD_PAD // REGROUP_W):
            start, stop = j * SHARD_W + b * REGROUP_W, min(j * SHARD_W + (b + 1) * REGROUP_W, (j + 1) * SHARD_W)
            pieces = []
            for pos, s, n in _kernel_pieces():
                lo, hi = max(s, start), min(s + n, stop)
                if lo < hi:
                    pieces.append((0, pos + lo - s, lo - start, hi - start))
            steps.append((layer * 4 + j, b, pieces))
    return steps


def _gather_weights(w):
    w_in = jnp.pad(w["w_in"].astype(bf16), ((0, 0), (0, 0), (0, SHARD_PAD - SHARD_W)))
    arrs = [w_in, w["w_branch"].reshape(N_LAYERS, N_BRANCH * BW, -1).astype(bf16),
            w["w_out"].astype(bf16), w["s5_w_glu"].astype(bf16), w["m2_conv_w"], w["sc_conv_w"], w["merge_b"]]
    got = gather_chips("ag_weights", arrs)
    cols = lambda t: jnp.transpose(t, (1, 0, 2)).reshape(t.shape[1], -1)
    layers = []
    for i in range(N_LAYERS):
        p = {k: w[k][i] for k in REPLICATED}
        p["w_in"] = regroup_cols(f"w_in_cols{i}", got[0].reshape(4 * N_LAYERS, D_MODEL, SHARD_PAD), _steps_to_kernel_cols(i),
                                 (1, D_MODEL, PW))[0]
        p["w_branch"] = cols(got[1][:, i])
        p["w_out"] = got[2][:, i].reshape(D_MODEL, D_MODEL)
        p["s5_w_glu"] = got[3][:, i].reshape(BW, BW)
        p["m2_conv_w"], p["sc_conv_w"], p["merge_b"] = cols(got[4][:, i]), cols(got[5][:, i]), cols(got[6][:, i])
        layers.append(p)
    return layers


def _reduce_grads(grads, d_final, w):
    to_chips = lambda t: jnp.transpose(t.reshape(t.shape[0], 4, -1), (1, 0, 2))
    stack = lambda f: jnp.stack([f(g) for g in grads])
    dw_in = None
    for i, g in enumerate(grads):
        dw_in = regroup_cols(f"w_in_shards{i}", g["w_in"][None], _steps_to_shards(i), (4 * N_LAYERS, D_MODEL, SHARD_PAD), out=dw_in)
    parts = [dw_in.reshape(N_LAYERS, 4, D_MODEL, SHARD_PAD),
             stack(lambda g: to_chips(g["w_branch"].reshape(N_BRANCH * BW, D_MODEL)).astype(bf16)),
             stack(lambda g: g["w_out"].reshape(4, D_MODEL // 4, D_MODEL).astype(bf16)),
             stack(lambda g: g["s5_w_glu"].reshape(4, BW // 4, BW).astype(bf16))]
    rep = jnp.concatenate([stack(lambda g: g[k]).reshape(-1) for k in REPLICATED] + [d_final.reshape(-1)])
    quarter = -(-rep.size // (4 * 2 * SUBL * PACK_W)) * (2 * SUBL * PACK_W)
    rep = jnp.pad(rep, (0, 4 * quarter - rep.size))
    small = []
    for j in range(4):
        sharded = [stack(lambda g: to_chips(g[k])[j]) for k in SHARDED_SMALL]
        small.append(_pack(sharded + [rep[j * quarter:(j + 1) * quarter]], 2, SUBL))
    parts.append(jnp.stack(small, axis=1))
    red = reduce_to_shards(parts)
    out = {"w_in": red[0][:, :, :SHARD_W], "w_branch": red[1].reshape(w["w_branch"].shape), "w_out": red[2], "s5_w_glu": red[3]}
    small_flat = red[4].reshape(-1)
    n_small = sum(int(np.prod(w[k].shape)) for k in SHARDED_SMALL)
    out.update(zip(SHARDED_SMALL, _unpack(small_flat, [w[k].shape for k in SHARDED_SMALL])))
    mine = small_flat[n_small:n_small + quarter].reshape(2, quarter // (2 * PACK_W), PACK_W)
    rep_all = gather_chips("ag_small_grads", [mine])[0].reshape(-1)
    names = REPLICATED + ("final_norm_w",)
    out.update(zip(names, _unpack(rep_all, [w[k].shape for k in names])))
    return out


def _update(w, g, m, v):
    d, nm, nv = {}, {}, {}
    flat2 = lambda a: a.reshape(-1, a.shape[-1])
    cols_major = lambda a: jnp.transpose(a, (2, 0, 1))
    res = adamw("adamw_w_in", *[cols_major(t["w_in"]) for t in (w, g, m, v)], tr=SHARD_W // 14)
    d["w_in"], nm["w_in"], nv["w_in"] = (jnp.transpose(r, (1, 2, 0)) for r in res)
    for k in SHARDED_BIG[1:]:
        res = adamw("adamw_" + k, *[flat2(t[k]) for t in (w, g, m, v)])
        d[k], nm[k], nv[k] = (r.reshape(w[k].shape) for r in res)
    for k in ("s5_b_re", "s5_b_im"):
        res = adamw("adamw_" + k, *[flat2(t[k]) for t in (w, g, m, v)])
        d[k], nm[k], nv[k] = (r.reshape(w[k].shape) for r in res)
    rest = [k for k in WEIGHTS if k not in SHARDED_BIG + ("s5_b_re", "s5_b_im")]
    two_d = lambda a: a.reshape(1, -1) if a.ndim == 1 else a
    res = adamw_many("adamw_rest", *[[two_d(t[k]) for k in rest] for t in (w, g, m, v)])
    for tgt, rs in zip((d, nm, nv), res):
        tgt.update({k: r.reshape(w[k].shape) for k, r in zip(rest, rs)})
    return d, nm, nv


def kernel(x, norm_w, w_in, s5_lambda_re, s5_lambda_im, s5_b_re, s5_b_im, s5_c_re, s5_c_im, s5_d, s5_log_step, s5_w_glu, sgu_ln_w, sgu_ln_b, sgu_w, sgu_b, m2_conv_w, m2_conv_b, m2_dt_bias, m2_a_log, m2_d, m2_norm_w, sc_conv_w, merge_b, w_branch, w_out, final_norm_w, loss_target, m_norm_w, m_w_in, m_s5_lambda_re, m_s5_lambda_im, m_s5_b_re, m_s5_b_im, m_s5_c_re, m_s5_c_im, m_s5_d, m_s5_log_step, m_s5_w_glu, m_sgu_ln_w, m_sgu_ln_b, m_sgu_w, m_sgu_b, m_m2_conv_w, m_m2_conv_b, m_m2_dt_bias, m_m2_a_log, m_m2_d, m_m2_norm_w, m_sc_conv_w, m_merge_b, m_w_branch, m_w_out, m_final_norm_w, v_norm_w, v_w_in, v_s5_lambda_re, v_s5_lambda_im, v_s5_b_re, v_s5_b_im, v_s5_c_re, v_s5_c_im, v_s5_d, v_s5_log_step, v_s5_w_glu, v_sgu_ln_w, v_sgu_ln_b, v_sgu_w, v_sgu_b, v_m2_conv_w, v_m2_conv_b, v_m2_dt_bias, v_m2_a_log, v_m2_d, v_m2_norm_w, v_sc_conv_w, v_merge_b, v_w_branch, v_w_out, v_final_norm_w):
    given = dict(locals())
    w = {k: given[k] for k in WEIGHTS}
    m = {k: given["m_" + k] for k in WEIGHTS}
    v = {k: given["v_" + k] for k in WEIGHTS}
    layers = _gather_weights(w)
    loss, dx, grads, d_final = local_step(x[0], loss_target[0], layers, final_norm_w)
    loss = lax.psum(loss, ("x", "y", "c"))
    g = _reduce_grads(grads, d_final, w)
    d, nm, nv = _update(w, g, m, v)
    return (loss, dx[None], *[g[k] for k in WEIGHTS], *[d[k] for k in WEIGHTS],
            *[nm[k] for k in WEIGHTS], *[nv[k] for k in WEIGHTS])
```

```python
import functools
from typing import Any, Callable, NamedTuple

import numpy as np
import jax
import jax.numpy as jnp
from jax import lax
from jax.experimental import pallas as pl
from jax.experimental.pallas import tpu as pltpu

f32 = jnp.float32
bf16 = jnp.bfloat16

D_MODEL = 1024
BW = 512
N_BRANCH = 4
EPS = 1e-6
S5_GROUPS, S5_P, S5_N = 32, 16, 64
S5_NS = S5_GROUPS * S5_N
CHUNK = 128
M2_HEADS, M2_HEAD_DIM, M2_GROUPS, M2_STATE = 8, 64, 2, 128
IN_DIM = 10248
PW = 10368
PW_MAIN = 10240
LANES = 128
VMEM_LIMIT = 60 * 1024 * 1024

ADAM_LR, ADAM_B1, ADAM_B2, ADAM_EPS, ADAM_WD, ADAM_STEP = 0.001, 0.9, 0.999, 1e-08, 0.01, 10

C_MERGE = 0
C_SC = 4096
C_SGU = 6144
C_S5G = 7680
C_S5U = 8192
C_M2Z = 8704
C_XBC = 9216
C_DT = 10240


def _col_segments():
    segs = [(6152, 4096)]
    for j in range(4):
        segs += [(4104 + 128 * j, 128), (4616 + 128 * j, 128), (5128 + 128 * j, 128), (5640 + 128 * j, 128)]
    segs += [(1024, 1536), (512, 512), (0, 512), (2560, 512), (3072, 1024), (4096, 8)]
    return segs


NN = ((1,), (0,))
NT = ((1,), (1,))
TN = ((0,), (0,))


def _bd(a, b, dims):
    return lax.dot_general(a.astype(bf16), b.astype(bf16), (dims, ((), ())), preferred_element_type=f32)


def _hd(a, b, dims):
    return lax.dot_general(a, b, (dims, ((), ())), precision=lax.Precision.HIGHEST, preferred_element_type=f32)


def _make_dots(raw):
    @jax.custom_vjp
    def nn(a, b):
        return raw(a, b, NN)
    nn.defvjp(lambda a, b: (raw(a, b, NN), (a, b)), lambda r, g: (raw(g, r[1], NT), raw(r[0], g, TN)))

    @jax.custom_vjp
    def nt(a, b):
        return raw(a, b, NT)
    nt.defvjp(lambda a, b: (raw(a, b, NT), (a, b)), lambda r, g: (raw(g, r[1], NN), raw(g, r[0], TN)))

    @jax.custom_vjp
    def tn(a, b):
        return raw(a, b, TN)
    tn.defvjp(lambda a, b: (raw(a, b, TN), (a, b)), lambda r, g: (raw(r[1], g, NT), raw(r[0], g, NN)))
    return nn, nt, tn


bdot, bdot_nt, bdot_tn = _make_dots(_bd)
hdot, hdot_nt, hdot_tn = _make_dots(_hd)


@jax.custom_vjp
def bdot_w(a, w, shadow):
    return _bd(a, w, NN)


bdot_w.defvjp(lambda a, w, s: (_bd(a, w, NN), (a, w)),
              lambda r, g: (_bd(g, r[1], NT), jnp.zeros_like(r[1]), _bd(r[0], g, TN)))


def _rows(shape):
    return lax.broadcasted_iota(jnp.int32, shape, 0)


def _cols(shape):
    return lax.broadcasted_iota(jnp.int32, shape, 1)


def _shift_down(x, s):
    return jnp.where(_rows(x.shape) < s, 0.0, pltpu.roll(x, s, 0))


def _shift_up(x, s):
    n = x.shape[0]
    return jnp.where(_rows(x.shape) >= n - s, 0.0, pltpu.roll(x, n - s, 0))


@functools.partial(jax.custom_vjp, nondiff_argnums=(1,))
def shift(x, s):
    return _shift_down(x, s) if s else x


shift.defvjp(lambda x, s: (shift(x, s), None), lambda s, _, g: (_shift_up(g, s) if s else g,))


def _row_of(w, k):
    return jnp.sum(jnp.where(_rows(w.shape) == k, w, 0.0), axis=0, keepdims=True)


def _lane_mask(width, lo, hi):
    c = _cols((1, width))
    return ((c >= lo) & (c < hi)).astype(f32)


def _expand(rows, width, per):
    return (_cols((rows, width)) // per == _rows((rows, width))).astype(f32)


class A(NamedTuple):
    arr: Any
    block: tuple
    imap: Callable
    shadow: bool = False


class O(NamedTuple):
    shape: tuple
    dtype: Any
    block: tuple
    imap: Callable
    alias: Any = None


class R(NamedTuple):
    arg: int
    out: int
    off: Any = None
    acc: bool = False


def _cparams(n_grid):
    return pltpu.CompilerParams(dimension_semantics=("arbitrary",) * n_grid, vmem_limit_bytes=VMEM_LIMIT)


def _ispec(block, imap, n, rev):
    if rev:
        return pl.BlockSpec(block, lambda i: imap(n - 1 - i))
    return pl.BlockSpec(block, imap)


def _load(ref, a):
    v = ref[...]
    if a.shadow:
        return (v, jnp.zeros(v.shape, f32))
    return v.astype(f32)


def _save_spec(shape, n, rev):
    nd = len(shape)
    return _ispec((None,) + tuple(shape), lambda i: (i,) + (0,) * nd, n, rev)


def block_fwd(name, f, n, args, outs, carries=()):
    n_in, n_out, n_c = len(args), len(outs), len(carries)

    def body(*refs):
        ins, out_r = refs[:n_in], refs[n_in:n_in + n_out]
        saves, cs = refs[n_in + n_out:n_in + n_out + n_c], refs[n_in + n_out + n_c:]
        if n_c:
            @pl.when(pl.program_id(0) == 0)
            def _():
                for c in cs:
                    c[...] = jnp.zeros(c.shape, f32)
        vals = [_load(r, a) for r, a in zip(ins, args)]
        cv = [c[...] for c in cs]
        for s, v in zip(saves, cv):
            s[...] = v
        res = f(*vals, *cv)
        for r, v in zip(out_r, res[:n_out]):
            r[...] = v.astype(r.dtype)
        for c, v in zip(cs, res[n_out:]):
            c[...] = v

    out_shape = [jax.ShapeDtypeStruct(o.shape, o.dtype) for o in outs]
    out_specs = [pl.BlockSpec(o.block, o.imap) for o in outs]
    for shp in carries:
        out_shape.append(jax.ShapeDtypeStruct((n,) + tuple(shp), f32))
        out_specs.append(_save_spec(shp, n, False))
    return pl.pallas_call(
        body, name=name, grid=(n,),
        in_specs=[pl.BlockSpec(a.block, a.imap) for a in args],
        out_specs=out_specs, out_shape=out_shape,
        scratch_shapes=[pltpu.VMEM(tuple(shp), f32) for shp in carries],
        compiler_params=_cparams(1),
    )(*[a.arr for a in args])


def block_bwd(name, f, n, args, cots, gouts, routes, saved=(), rev=False):
    n_in, n_cot, n_c, n_go = len(args), len(cots), len(saved), len(gouts)
    diff = []
    for r in routes:
        if r.arg not in diff:
            diff.append(r.arg)
    aliases = [(k, o.alias) for k, o in enumerate(gouts) if o.alias is not None]

    def body(*refs):
        ins = refs[:n_in]
        cot_r = refs[n_in:n_in + n_cot]
        sav_r = refs[n_in + n_cot:n_in + n_cot + n_c]
        base = n_in + n_cot + n_c + len(aliases)
        go_r = refs[base:base + n_go]
        dcs = refs[base + n_go:]
        step = pl.program_id(0)
        if n_c:
            @pl.when(step == 0)
            def _():
                for d in dcs:
                    d[...] = jnp.zeros(d.shape, f32)
        vals = [_load(r, a) for r, a in zip(ins, args)]
        cv = [s[...] for s in sav_r]
        nd = len(diff)

        def g(*dv):
            full = list(vals)
            for idx, v in zip(diff, dv[:nd]):
                full[idx] = (vals[idx][0], v) if args[idx].shadow else v
            return tuple(f(*full, *dv[nd:]))

        primals = [vals[i][1] if args[i].shadow else vals[i] for i in diff] + cv
        _, vjp_fn = jax.vjp(g, *primals)
        ct = tuple([r[...].astype(f32) for r in cot_r] + [d[...] for d in dcs])
        grads = vjp_fn(ct)
        for r in routes:
            gr = grads[diff.index(r.arg)]
            ref = go_r[r.out]
            if r.acc:
                @pl.when(step == 0)
                def _(ref=ref, gr=gr):
                    ref[...] = gr.astype(ref.dtype)

                @pl.when(step > 0)
                def _(ref=ref, gr=gr):
                    ref[...] += gr.astype(ref.dtype)
            elif r.off is None:
                ref[...] = gr.astype(ref.dtype)
            else:
                ref[:, r.off:r.off + gr.shape[1]] = gr.astype(ref.dtype)
        for d, gr in zip(dcs, grads[nd:]):
            d[...] = gr

    in_specs = [_ispec(a.block, a.imap, n, rev) for a in list(args) + list(cots)]
    in_specs += [_save_spec(s.shape[1:], n, rev) for s in saved]
    in_specs += [pl.BlockSpec(memory_space=pl.ANY) for _ in aliases]
    operands = [a.arr for a in list(args) + list(cots)] + list(saved) + [arr for _, arr in aliases]
    io_alias = {n_in + n_cot + n_c + j: k for j, (k, _) in enumerate(aliases)}
    return pl.pallas_call(
        body, name=name, grid=(n,),
        in_specs=in_specs,
        out_specs=[_ispec(o.block, o.imap, n, rev) for o in gouts],
        out_shape=[jax.ShapeDtypeStruct(o.shape, o.dtype) for o in gouts],
        scratch_shapes=[pltpu.VMEM(tuple(s.shape[1:]), f32) for s in saved],
        input_output_aliases=io_alias,
        compiler_params=_cparams(1),
    )(*operands)


class Op(NamedTuple):
    arr: Any
    row: int = 0
    col: int = 0


def mm(name, mode, pairs, m, n, tm, tn, out_dtype=f32, add=None, out=None, out_col=0, out_width=None):
    tm, tn = min(tm, m), min(tn, n)
    assert m % tm == 0 and n % tn == 0
    in_specs, operands = [], []
    for a, b, k, _ in pairs:
        if mode == TN:
            assert a.row % k == 0 and a.col % tm == 0
            in_specs.append(pl.BlockSpec((k, tm), lambda j, i, a=a, k=k: (a.row // k, i + a.col // tm)))
        else:
            assert a.col % k == 0 and a.row % tm == 0
            in_specs.append(pl.BlockSpec((tm, k), lambda j, i, a=a, k=k: (i + a.row // tm, a.col // k)))
        if mode == NT:
            assert b.col % k == 0 and b.row % tn == 0
            in_specs.append(pl.BlockSpec((tn, k), lambda j, i, b=b, k=k: (j + b.row // tn, b.col // k)))
        else:
            assert b.row % k == 0 and b.col % tn == 0
            in_specs.append(pl.BlockSpec((k, tn), lambda j, i, b=b, k=k: (b.row // k, j + b.col // tn)))
        operands += [a.arr, b.arr]
    n_p = len(pairs)
    if add is not None:
        assert add.col % tn == 0
        in_specs.append(pl.BlockSpec((tm, tn), lambda j, i: (i, j + add.col // tn)))
        operands.append(add.arr)
    io_alias = {}
    if out is not None:
        assert out_col % tn == 0
        in_specs.append(pl.BlockSpec(memory_space=pl.ANY))
        operands.append(out)
        io_alias = {len(operands) - 1: 0}
        out_shape = jax.ShapeDtypeStruct(out.shape, out.dtype)
    else:
        out_shape = jax.ShapeDtypeStruct((m, out_width or n), out_dtype)
    signs = [p[3] for p in pairs]

    def body(*refs):
        o = refs[-1]
        acc = None
        for p in range(n_p):
            t = _bd(refs[2 * p][...], refs[2 * p + 1][...], mode)
            t = t if signs[p] > 0 else -t
            acc = t if acc is None else acc + t
        if add is not None:
            acc = acc + refs[2 * n_p][...].astype(f32)
        o[...] = acc.astype(o.dtype)

    return pl.pallas_call(
        body, name=name, grid=(n // tn, m // tm),
        in_specs=in_specs,
        out_specs=pl.BlockSpec((tm, tn), lambda j, i: (i, j + out_col // tn)),
        out_shape=out_shape, input_output_aliases=io_alias,
        compiler_params=_cparams(2),
    )(*operands)


SCAN_LANES = 512
SUBL = 8


def _cmul(p, q):
    return (p[0] * q[0] - p[1] * q[1], p[0] * q[1] + p[1] * q[0])


def _powers(a):
    a2 = _cmul(a, a)
    a4 = _cmul(a2, a2)
    a6 = _cmul(a4, a2)
    return [a, a2, _cmul(a2, a), a4, _cmul(a4, a), a6, _cmul(a6, a), _cmul(a4, a4)]


def _table(pw, order, w):
    row = _rows((SUBL, w))
    re = sum(jnp.where(row == t, pw[k][0], 0.0) for t, k in enumerate(order))
    im = sum(jnp.where(row == t, pw[k][1], 0.0) for t, k in enumerate(order))
    return re, im


def s5_scan_fwd(bu_re, bu_im, a_re, a_im):
    seq, ns = bu_re.shape
    w, nb = SCAN_LANES, ns // SCAN_LANES

    def body(b_re, b_im, ar, ai, s_re, s_im):
        a = (ar[...], ai[...])
        pw = _powers(a)
        tab = _table(pw, list(range(SUBL)), w)
        row = _rows((SUBL, w))

        def step(i, carry):
            t0 = pl.multiple_of(i * SUBL, SUBL)
            x = (b_re[pl.ds(t0, SUBL), :], b_im[pl.ds(t0, SUBL), :])
            for d, k in ((1, 0), (2, 1), (4, 3)):
                sh = (jnp.where(row < d, 0.0, pltpu.roll(x[0], d, 0)), jnp.where(row < d, 0.0, pltpu.roll(x[1], d, 0)))
                t = _cmul(pw[k], sh)
                x = (x[0] + t[0], x[1] + t[1])
            t = _cmul(tab, carry)
            x = (x[0] + t[0], x[1] + t[1])
            s_re[pl.ds(t0, SUBL), :] = x[0]
            s_im[pl.ds(t0, SUBL), :] = x[1]
            return (x[0][SUBL - 1:, :], x[1][SUBL - 1:, :])

        z = jnp.zeros((1, w), f32)
        lax.fori_loop(0, seq // SUBL, step, (z, z), unroll=2)

    strip = pl.BlockSpec((seq, w), lambda j: (0, j))
    lane = pl.BlockSpec((1, w), lambda j: (0, j))
    return pl.pallas_call(
        body, name="s5_scan_fwd", grid=(nb,),
        in_specs=[strip, strip, lane, lane],
        out_specs=[strip, strip],
        out_shape=[jax.ShapeDtypeStruct((seq, ns), f32)] * 2,
        compiler_params=_cparams(1),
    )(bu_re, bu_im, a_re, a_im)


def s5_scan_bwd(ds_re, ds_im, s_re, s_im, a_re, a_im):
    seq, ns = ds_re.shape
    w, nb = SCAN_LANES, ns // SCAN_LANES
    nblk = seq // SUBL

    def body(g_re, g_im, sr, si, ar, ai, l_re, l_im, da_re, da_im):
        a = (ar[...], -ai[...])
        pw = _powers(a)
        tab = _table(pw, [SUBL - 1 - t for t in range(SUBL)], w)
        row = _rows((SUBL, w))

        def step(kk, carry):
            c_re, c_im, acc_re, acc_im = carry
            i = nblk - 1 - kk
            t0 = pl.multiple_of(i * SUBL, SUBL)
            x = (g_re[pl.ds(t0, SUBL), :], g_im[pl.ds(t0, SUBL), :])
            for d, k in ((1, 0), (2, 1), (4, 3)):
                sh = (jnp.where(row >= SUBL - d, 0.0, pltpu.roll(x[0], SUBL - d, 0)),
                      jnp.where(row >= SUBL - d, 0.0, pltpu.roll(x[1], SUBL - d, 0)))
                t = _cmul(pw[k], sh)
                x = (x[0] + t[0], x[1] + t[1])
            t = _cmul(tab, (c_re, c_im))
            x = (x[0] + t[0], x[1] + t[1])
            l_re[pl.ds(t0, SUBL), :] = x[0]
            l_im[pl.ds(t0, SUBL), :] = x[1]
            tp = jnp.maximum(t0 - 1, 0)
            live = (i > 0).astype(f32)
            p_re = sr[pl.ds(tp, 1), :] * live
            p_im = si[pl.ds(tp, 1), :] * live
            sp_re = jnp.where(row == 0, p_re, pltpu.roll(sr[pl.ds(t0, SUBL), :], 1, 0))
            sp_im = jnp.where(row == 0, p_im, pltpu.roll(si[pl.ds(t0, SUBL), :], 1, 0))
            acc_re = acc_re + x[0] * sp_re + x[1] * sp_im
            acc_im = acc_im + x[1] * sp_re - x[0] * sp_im
            return (x[0][:1, :], x[1][:1, :], acc_re, acc_im)

        z1 = jnp.zeros((1, w), f32)
        z8 = jnp.zeros((SUBL, w), f32)
        _, _, acc_re, acc_im = lax.fori_loop(0, nblk, step, (z1, z1, z8, z8), unroll=2)
        da_re[...] = jnp.sum(acc_re, axis=0, keepdims=True)
        da_im[...] = jnp.sum(acc_im, axis=0, keepdims=True)

    strip = pl.BlockSpec((seq, w), lambda j: (0, j))
    lane = pl.BlockSpec((1, w), lambda j: (0, j))
    return pl.pallas_call(
        body, name="s5_scan_bwd", grid=(nb,),
        in_specs=[strip, strip, strip, strip, lane, lane],
        out_specs=[strip, strip, lane, lane],
        out_shape=[jax.ShapeDtypeStruct((seq, ns), f32)] * 2 + [jax.ShapeDtypeStruct((1, ns), f32)] * 2,
        compiler_params=_cparams(1),
    )(ds_re, ds_im, s_re, s_im, a_re, a_im)


def _rms(x, w):
    return x * lax.rsqrt(jnp.mean(x * x, axis=-1, keepdims=True) + EPS) * w


def f_rms(x, w):
    return (_rms(x, w),)


def f_rms_res(x, w):
    return (_rms(x, w), x)


def f_s5_prep(lam_re, lam_im, log_step, b_re, b_im):
    e = _expand(log_step.shape[1], S5_NS, S5_N)
    step = hdot(jnp.exp(log_step), e)
    mag = jnp.exp(lam_re * step)
    ab_re, ab_im = mag * jnp.cos(lam_im * step), mag * jnp.sin(lam_im * step)
    den = lam_re * lam_re + lam_im * lam_im
    nr = ab_re - 1.0
    coef_re = (nr * lam_re + ab_im * lam_im) / den
    coef_im = (ab_im * lam_re - nr * lam_im) / den
    bb_re, bb_im = coef_re * b_re - coef_im * b_im, coef_re * b_im + coef_im * b_re
    sel = (_rows((BW, S5_P)) % S5_P == _cols((BW, S5_P))).astype(f32)
    blk = _rows((BW, S5_NS)) // S5_P == _cols((BW, S5_NS)) // S5_N
    rows_bd = lambda t: jnp.where(blk, hdot(sel, t), 0.0)
    return (ab_re, ab_im, rows_bd(bb_re), rows_bd(bb_im))


def f_s5_c(c_re, c_im):
    sel_t = (_cols((S5_P, BW)) % S5_P == _rows((S5_P, BW))).astype(f32)
    blk_t = _rows((S5_NS, BW)) // S5_N == _cols((S5_NS, BW)) // S5_P
    cols_bd = lambda t: jnp.where(blk_t, hdot_tn(t, sel_t), 0.0)
    return (cols_bd(c_re), cols_bd(c_im))


def f_s5_act(y_lin, u, gate, d, w_glu):
    y = jax.nn.gelu(y_lin + d * u)
    y = y * jax.nn.sigmoid(bdot_w(y, *w_glu))
    return (y * jax.nn.silu(gate),)


def f_sgu(u, v, gate, ln_w, ln_b, *rest):
    w_s, b_pad = rest[:8], rest[8]
    t = u.shape[0]
    u32, v32 = jax.nn.gelu(u), jax.nn.gelu(v)
    mu = jnp.mean(v32, axis=-1, keepdims=True)
    var = jnp.mean(jnp.square(v32 - mu), axis=-1, keepdims=True)
    vn = (v32 - mu) * lax.rsqrt(var + EPS) * ln_w + ln_b
    tri = _rows((t, t)) >= _cols((t, t))
    s = hdot_tn(b_pad, _expand(LANES, BW, BW // 8))
    for h in range(8):
        s = s + bdot(jnp.where(tri, w_s[h], 0.0), vn) * _lane_mask(BW, 64 * h, 64 * h + 64)
    return (u32 * s * jax.nn.silu(gate),)


def f_m2_conv(x, w, b):
    return (sum(_row_of(w, k) * shift(x, 3 - k) for k in range(4)) + b,)


def f_sc(bg, cg, h, gate, w):
    z = cg * h
    conv = sum(_row_of(w, k) * shift(z, 2 - k) for k in range(3))
    return (bg * conv * jax.nn.silu(gate),)


def f_m2(z, xc, b0, b1, c0, c1, dt_raw, dt_bias, a_log, d_par, norm_w, st):
    q = z.shape[0]
    x = jax.nn.silu(xc)
    bm, cm = (jax.nn.silu(b0), jax.nn.silu(b1)), (jax.nn.silu(c0), jax.nn.silu(c1))
    dt = jax.nn.softplus(dt_raw + dt_bias)
    da = dt * (-jnp.exp(a_log))
    tri = _rows((q, q)) >= _cols((q, q))
    acs = hdot(tri.astype(f32), da)
    e = _expand(LANES, BW, M2_HEAD_DIM)
    dt_f, acs_f = hdot(dt, e), hdot(acs, e)
    last = _rows((q, BW)) == q - 1
    alast_f = jnp.sum(jnp.where(last, acs_f, 0.0), axis=0, keepdims=True)
    xdt = x * dt_f
    xdec = xdt * jnp.exp(alast_f - acs_f)
    acs_t = acs.T
    st_new = st * jnp.exp(alast_f)
    y_diag, y_off = 0.0, 0.0
    for g in range(M2_GROUPS):
        gm = _lane_mask(BW, 256 * g, 256 * g + 256)
        cb = bdot_nt(cm[g], bm[g])
        st_new = st_new + bdot_tn(bm[g], xdec * gm)
        y_off = y_off + bdot(cm[g], st) * gm
        for hh in range(M2_HEADS // M2_GROUPS):
            h = g * (M2_HEADS // M2_GROUPS) + hh
            col = jnp.sum(jnp.where(_cols((q, LANES)) == h, acs, 0.0), axis=1, keepdims=True)
            row = jnp.sum(jnp.where(_rows((LANES, q)) == h, acs_t, 0.0), axis=0, keepdims=True)
            decay = jnp.exp(jnp.where(tri, col - row, -1e30))
            y_diag = y_diag + bdot(cb * decay, xdt) * _lane_mask(BW, 64 * h, 64 * h + 64)
    d_f = sum(jnp.sum(jnp.where(_cols((1, LANES)) == h, d_par, 0.0), axis=1, keepdims=True)
              * _lane_mask(BW, 64 * h, 64 * h + 64) for h in range(M2_HEADS))
    y = y_diag + y_off * jnp.exp(acs_f) + d_f * x
    y = y * jax.nn.silu(z)
    return (_rms(y, norm_w), st_new)


def f_branch_mix(*v):
    ys, lg, wb, mb = v[0:4], v[4:8], v[8:12], v[12:16]
    return (sum(jax.nn.sigmoid(lg[k] + mb[k]) * bdot_w(ys[k], *wb[k]) for k in range(N_BRANCH)),)


def _param(arr, shadow=False):
    return A(arr, tuple(arr.shape), lambda i, nd=arr.ndim: (0,) * nd, shadow)


def _tb(arr, t, width, colblk):
    return A(arr, (t, width), lambda i: (i, colblk))


def _strip(arr, seq, colblk0, stride=1):
    return A(arr, (seq, LANES), lambda j: (0, colblk0 + stride * j))


def _s5_prep_args(p):
    lam_re = p["s5_lambda_re"].reshape(1, S5_NS)
    lam_im = p["s5_lambda_im"].reshape(1, S5_NS)
    log_step = jnp.pad(p["s5_log_step"].reshape(1, S5_GROUPS), ((0, 0), (0, LANES - S5_GROUPS)))
    b_lanes = lambda b: jnp.transpose(b, (2, 0, 1)).reshape(S5_P, S5_NS)
    return [_param(v) for v in (lam_re, lam_im, log_step, b_lanes(p["s5_b_re"]), b_lanes(p["s5_b_im"]))]


def _s5_c_args(p):
    c_lanes = lambda c: jnp.transpose(c, (1, 0, 2)).reshape(S5_P, S5_NS)
    return [_param(c_lanes(p["s5_c_re"])), _param(c_lanes(p["s5_c_im"]))]


def layer_fwd(x, p):
    seq = x.shape[0]
    nt = seq // CHUNK
    t2 = 256
    sv = {}
    hb = block_fwd("rms_in", f_rms, seq // t2, [_tb(x, t2, D_MODEL, 0), _param(p["norm_w"].reshape(1, D_MODEL))],
                   [O((seq, D_MODEL), bf16, (t2, D_MODEL), lambda i: (i, 0))])[0]
    proj = mm("mm_in", NN, [(Op(hb), Op(p["w_in"]), D_MODEL, 1)], seq, PW, 1024, 1152)
    whole = lambda shape, dt: O(shape, dt, shape, lambda i: (0, 0))
    ab_re, ab_im, bb_re, bb_im = block_fwd("s5_prep", f_s5_prep, 1, _s5_prep_args(p),
                                           [whole((1, S5_NS), f32)] * 2 + [whole((BW, S5_NS), bf16)] * 2)
    cc_re, cc_im = block_fwd("s5_c", f_s5_c, 1, _s5_c_args(p), [whole((S5_NS, BW), bf16)] * 2)
    bu_re = mm("mm_bu_re", NN, [(Op(proj, 0, C_S5U), Op(bb_re), BW, 1)], seq, S5_NS, 512, 1024)
    bu_im = mm("mm_bu_im", NN, [(Op(proj, 0, C_S5U), Op(bb_im), BW, 1)], seq, S5_NS, 512, 1024)
    s_re, s_im = s5_scan_fwd(bu_re, bu_im, ab_re, ab_im)
    y_lin = mm("mm_s5y", NN, [(Op(s_re), Op(cc_re), S5_NS, 1), (Op(s_im), Op(cc_im), S5_NS, -1)], seq, BW, 256, BW)
    s5_act_args = [_tb(y_lin, CHUNK, BW, 0), _tb(proj, CHUNK, BW, C_S5U // BW), _tb(proj, CHUNK, BW, C_S5G // BW),
                   _param(p["s5_d"].reshape(1, BW)), _param(p["s5_w_glu"], True)]
    out_bw = O((seq, BW), f32, (CHUNK, BW), lambda i: (i, 0))
    y_a = block_fwd("s5_act", f_s5_act, nt, s5_act_args, [out_bw])[0]
    y_b = block_fwd("sgu", f_sgu, nt, _sgu_args(proj, p), [out_bw])[0]
    xc = block_fwd("m2_conv", f_m2_conv, 8, _m2_conv_args(proj, p, seq),
                   [O((seq, 2 * BW), f32, (seq, LANES), lambda j: (0, j))])[0]
    y_c, st_saved = block_fwd("m2_ssd", f_m2, nt, _m2_args(proj, xc, p), [out_bw], carries=[(M2_STATE, BW)])
    y_d = block_fwd("sc", f_sc, 4, _sc_args(proj, p, seq), [O((seq, BW), f32, (seq, LANES), lambda j: (0, j))])[0]
    ys = [y_a, y_b, y_c, y_d]
    merged = block_fwd("branch_mix", f_branch_mix, seq // t2, _mix_args(ys, proj, p, t2),
                       [O((seq, D_MODEL), bf16, (t2, D_MODEL), lambda i: (i, 0))])[0]
    x_new = mm("mm_out", NN, [(Op(merged), Op(p["w_out"]), D_MODEL, 1)], seq, D_MODEL, 512, D_MODEL, add=Op(x))
    sv.update(x=x, hb=hb, proj=proj, ab=(ab_re, ab_im), bb=(bb_re, bb_im), cc=(cc_re, cc_im), s=(s_re, s_im), y_lin=y_lin,
              xc=xc, st=st_saved, ys=ys, merged=merged)
    return x_new, sv


def _sgu_args(proj, p):
    c0 = C_SGU // BW
    args = [_tb(proj, CHUNK, BW, c0), _tb(proj, CHUNK, BW, c0 + 1), _tb(proj, CHUNK, BW, c0 + 2),
            _param(p["sgu_ln_w"].reshape(1, BW)), _param(p["sgu_ln_b"].reshape(1, BW))]
    args += [A(p["sgu_w"], (None, CHUNK, CHUNK), lambda i, h=h: (h, 0, 0)) for h in range(8)]
    args.append(_param(jnp.pad(p["sgu_b"], ((0, LANES - 8), (0, 0)))))
    return args


def _m2_conv_args(proj, p, seq):
    return [_strip(proj, seq, C_XBC // LANES), A(p["m2_conv_w"], (4, LANES), lambda j: (0, j)),
            A(p["m2_conv_b"].reshape(1, 2 * BW), (1, LANES), lambda j: (0, j))]


def _pad_lanes(v):
    return jnp.pad(v.reshape(1, -1), ((0, 0), (0, LANES - v.size)))


def _m2_args(proj, xc, p):
    args = [_tb(proj, CHUNK, BW, C_M2Z // BW), _tb(xc, CHUNK, BW, 0)]
    args += [_tb(xc, CHUNK, LANES, 4 + k) for k in range(4)]
    args.append(_tb(proj, CHUNK, LANES, C_DT // LANES))
    args += [_param(_pad_lanes(p["m2_dt_bias"])), _param(_pad_lanes(p["m2_a_log"])), _param(_pad_lanes(p["m2_d"])),
             _param(p["m2_norm_w"].reshape(1, BW))]
    return args


def _sc_args(proj, p, seq):
    c0 = C_SC // LANES
    return [_strip(proj, seq, c0 + k, 4) for k in range(4)] + [A(p["sc_conv_w"], (3, LANES), lambda j: (0, j))]


def _mix_args(ys, proj, p, t):
    args = [_tb(ys[k], t, BW, 0) for k in range(N_BRANCH)]
    args += [_tb(proj, t, D_MODEL, k) for k in range(N_BRANCH)]
    args += [A(p["w_branch"], (BW, D_MODEL), lambda i, k=k: (k, 0), True) for k in range(N_BRANCH)]
    mb = p["merge_b"].reshape(N_BRANCH, 1, D_MODEL)
    args += [A(mb, (None, 1, D_MODEL), lambda i, k=k: (k, 0, 0)) for k in range(N_BRANCH)]
    return args


def layer_bwd(d_out, p, sv):
    seq = d_out.shape[0]
    nt = seq // CHUNK
    t2 = 256
    proj, ys = sv["proj"], sv["ys"]
    g = {}
    acc = lambda shape: O(tuple(shape), f32, tuple(shape), lambda i, nd=len(shape): (0,) * nd)
    d_merged = mm("mm_out_dx", NT, [(Op(d_out), Op(p["w_out"]), D_MODEL, 1)], seq, D_MODEL, 512, D_MODEL, out_dtype=f32)
    g["w_out"] = mm("mm_out_dw", TN, [(Op(sv["merged"]), Op(d_out), seq, 1)], D_MODEL, D_MODEL, 256, D_MODEL, out_dtype=bf16)
    gouts = [O((seq, BW), f32, (CHUNK, BW), lambda i: (i, 0)) for _ in range(N_BRANCH)]
    gouts.append(O((seq, PW_MAIN), bf16, (CHUNK, N_BRANCH * D_MODEL), lambda i: (i, 0)))
    gouts += [acc((BW, D_MODEL)) for _ in range(N_BRANCH)] + [acc((1, D_MODEL)) for _ in range(N_BRANCH)]
    routes = [R(k, k) for k in range(N_BRANCH)] + [R(4 + k, 4, k * D_MODEL) for k in range(N_BRANCH)]
    routes += [R(8 + k, 5 + k, acc=True) for k in range(N_BRANCH)] + [R(12 + k, 9 + k, acc=True) for k in range(N_BRANCH)]
    res = block_bwd("branch_mix_bwd", f_branch_mix, nt, _mix_args(ys, proj, p, CHUNK),
                    [_tb(d_merged, CHUNK, D_MODEL, 0)], gouts, routes)
    dys, dproj = res[:4], res[4]
    g["w_branch"] = jnp.stack(res[5:9])
    g["merge_b"] = jnp.concatenate(res[9:13], axis=0)
    res = block_bwd("sc_bwd", f_sc, 4, _sc_args(proj, p, seq), [_strip(dys[3], seq, 0)],
                    [O((seq, PW_MAIN), bf16, (seq, 4 * LANES), lambda j: (0, C_SC // (4 * LANES) + j), alias=dproj),
                     O((3, BW), f32, (3, LANES), lambda j: (0, j))],
                    [R(k, 0, k * LANES) for k in range(4)] + [R(4, 1)])
    dproj, g["sc_conv_w"] = res
    res = block_bwd("m2_ssd_bwd", f_m2, nt, _m2_args(proj, sv["xc"], p), [_tb(dys[2], CHUNK, BW, 0)],
                    [O((seq, PW_MAIN), bf16, (CHUNK, BW), lambda i: (i, C_M2Z // BW), alias=dproj),
                     O((seq, 2 * BW), f32, (CHUNK, 2 * BW), lambda i: (i, 0)),
                     O((seq, LANES), bf16, (CHUNK, LANES), lambda i: (i, 0)),
                     acc((1, LANES)), acc((1, LANES)), acc((1, LANES)), acc((1, BW))],
                    [R(0, 0), R(1, 1, 0)] + [R(2 + k, 1, BW + k * LANES) for k in range(4)] + [R(6, 2)]
                    + [R(7, 3, acc=True), R(8, 4, acc=True), R(9, 5, acc=True), R(10, 6, acc=True)],
                    saved=[sv["st"]], rev=True)
    dproj, dxc, d_dt = res[0], res[1], res[2]
    g["m2_dt_bias"], g["m2_a_log"], g["m2_d"] = (r[0, :M2_HEADS] for r in res[3:6])
    g["m2_norm_w"] = res[6].reshape(BW)
    res = block_bwd("m2_conv_bwd", f_m2_conv, 8, _m2_conv_args(proj, p, seq), [_strip(dxc, seq, 0)],
                    [O((seq, PW_MAIN), bf16, (seq, LANES), lambda j: (0, C_XBC // LANES + j), alias=dproj),
                     O((4, 2 * BW), f32, (4, LANES), lambda j: (0, j)), O((1, 2 * BW), f32, (1, LANES), lambda j: (0, j))],
                    [R(0, 0), R(1, 1), R(2, 2)])
    dproj, g["m2_conv_w"], cb = res
    g["m2_conv_b"] = cb.reshape(2 * BW)
    res = block_bwd("sgu_bwd", f_sgu, nt, _sgu_args(proj, p), [_tb(dys[1], CHUNK, BW, 0)],
                    [O((seq, PW_MAIN), bf16, (CHUNK, 3 * BW), lambda i: (i, C_SGU // (3 * BW)), alias=dproj),
                     acc((1, BW)), acc((1, BW))] + [acc((CHUNK, CHUNK)) for _ in range(8)] + [acc((LANES, CHUNK))],
                    [R(0, 0, 0), R(1, 0, BW), R(2, 0, 2 * BW), R(3, 1, acc=True), R(4, 2, acc=True)]
                    + [R(5 + h, 3 + h, acc=True) for h in range(8)] + [R(13, 11, acc=True)])
    dproj = res[0]
    g["sgu_ln_w"], g["sgu_ln_b"] = res[1].reshape(BW), res[2].reshape(BW)
    g["sgu_w"] = jnp.stack(res[3:11])
    g["sgu_b"] = res[11][:8]
    y_lin, (s_re, s_im), (ab_re, ab_im) = sv["y_lin"], sv["s"], sv["ab"]
    s5_act_args = [_tb(y_lin, CHUNK, BW, 0), _tb(proj, CHUNK, BW, C_S5U // BW), _tb(proj, CHUNK, BW, C_S5G // BW),
                   _param(p["s5_d"].reshape(1, BW)), _param(p["s5_w_glu"], True)]
    res = block_bwd("s5_act_bwd", f_s5_act, nt, s5_act_args, [_tb(dys[0], CHUNK, BW, 0)],
                    [O((seq, BW), bf16, (CHUNK, BW), lambda i: (i, 0)), O((seq, BW), f32, (CHUNK, BW), lambda i: (i, 0)),
                     O((seq, PW_MAIN), bf16, (CHUNK, BW), lambda i: (i, C_S5G // BW), alias=dproj),
                     acc((1, BW)), acc((BW, BW))],
                    [R(0, 0), R(1, 1), R(2, 2), R(3, 3, acc=True), R(4, 4, acc=True)])
    dy_lin, du1, dproj = res[0], res[1], res[2]
    g["s5_d"] = res[3].reshape(S5_GROUPS, S5_P)
    g["s5_w_glu"] = res[4]
    (bb_re, bb_im), (cc_re, cc_im) = sv["bb"], sv["cc"]
    ds_re = mm("mm_s5y_dre", NT, [(Op(dy_lin), Op(cc_re), BW, 1)], seq, S5_NS, 512, 1024)
    ds_im = mm("mm_s5y_dim", NT, [(Op(dy_lin), Op(cc_im), BW, -1)], seq, S5_NS, 512, 1024)
    dc_re = mm("mm_s5y_dcre", TN, [(Op(s_re), Op(dy_lin), seq, 1)], S5_NS, BW, 512, BW)
    dc_im = mm("mm_s5y_dcim", TN, [(Op(s_im), Op(dy_lin), seq, -1)], S5_NS, BW, 512, BW)
    l_re, l_im, da_re, da_im = s5_scan_bwd(ds_re, ds_im, s_re, s_im, ab_re, ab_im)
    dproj = mm("mm_bu_dx", NT, [(Op(l_re), Op(bb_re), S5_NS, 1), (Op(l_im), Op(bb_im), S5_NS, 1)],
               seq, BW, 512, BW, add=Op(du1), out=dproj, out_col=C_S5U)
    dbb = [mm(f"mm_bu_dw{n}", TN, [(Op(proj, 0, C_S5U), Op(l), seq, 1)], BW, S5_NS, 256, 1024)
           for n, l in (("re", l_re), ("im", l_im))]
    gouts = [acc((1, S5_NS)), acc((1, S5_NS)), acc((1, LANES))] + [acc((S5_P, S5_NS))] * 2
    res = block_bwd("s5_prep_bwd", f_s5_prep, 1, _s5_prep_args(p),
                    [_param(v) for v in (da_re, da_im, dbb[0], dbb[1])], gouts, [R(k, k, acc=True) for k in range(5)])
    g["s5_lambda_re"], g["s5_lambda_im"] = res[0].reshape(S5_GROUPS, S5_N), res[1].reshape(S5_GROUPS, S5_N)
    g["s5_log_step"] = res[2][0, :S5_GROUPS]
    b_natural = lambda b: jnp.transpose(b.reshape(S5_P, S5_GROUPS, S5_N), (1, 2, 0))
    c_natural = lambda c: jnp.transpose(c.reshape(S5_P, S5_GROUPS, S5_N), (1, 0, 2))
    g["s5_b_re"], g["s5_b_im"] = b_natural(res[3]), b_natural(res[4])
    res = block_bwd("s5_c_bwd", f_s5_c, 1, _s5_c_args(p), [_param(dc_re), _param(dc_im)],
                    [acc((S5_P, S5_NS))] * 2, [R(0, 0, acc=True), R(1, 1, acc=True)])
    g["s5_c_re"], g["s5_c_im"] = c_natural(res[0]), c_natural(res[1])
    hb, w_in = sv["hb"], p["w_in"]
    dh = mm("mm_in_dx", NT, [(Op(dproj), Op(w_in), PW_MAIN, 1), (Op(d_dt), Op(w_in, 0, PW_MAIN), LANES, 1)],
            seq, D_MODEL, 256, 512)
    dw = mm("mm_in_dw", TN, [(Op(hb), Op(dproj), seq, 1)], D_MODEL, PW_MAIN, 512, 1024, out_dtype=bf16, out_width=PW)
    g["w_in"] = mm("mm_in_dwdt", TN, [(Op(hb), Op(d_dt), seq, 1)], D_MODEL, LANES, 256, LANES, out=dw, out_col=C_DT)
    dx, dnw = block_bwd("rms_in_bwd", f_rms_res, seq // t2, [_tb(sv["x"], t2, D_MODEL, 0), _param(p["norm_w"].reshape(1, D_MODEL))],
                        [_tb(dh, t2, D_MODEL, 0), _tb(d_out, t2, D_MODEL, 0)],
                        [O((seq, D_MODEL), f32, (t2, D_MODEL), lambda i: (i, 0)), acc((1, D_MODEL))],
                        [R(0, 0), R(1, 1, acc=True)])
    g["norm_w"] = dnw.reshape(D_MODEL)
    return dx, g


def loss_head(x, w, target):
    seq = x.shape[0]
    t = 256

    def body(x_ref, w_ref, t_ref, loss_ref, dx_ref, dw_ref):
        step = pl.program_id(0)

        def f(xv, wv):
            err = _rms(xv, wv) - t_ref[...]
            return 0.5 * jnp.sum(jnp.mean(err * err, axis=-1, keepdims=True), axis=0, keepdims=True)

        val, vjp_fn = jax.vjp(f, x_ref[...], w_ref[...])
        dx, dw = vjp_fn(jnp.ones((1, 1), f32))
        dx_ref[...] = dx

        @pl.when(step == 0)
        def _():
            loss_ref[...] = jnp.broadcast_to(val, loss_ref.shape)
            dw_ref[...] = dw

        @pl.when(step > 0)
        def _():
            loss_ref[...] += jnp.broadcast_to(val, loss_ref.shape)
            dw_ref[...] += dw

    blk = pl.BlockSpec((t, D_MODEL), lambda i: (i, 0))
    row = pl.BlockSpec((1, D_MODEL), lambda i: (0, 0))
    return pl.pallas_call(
        body, name="loss_head", grid=(seq // t,),
        in_specs=[blk, row, blk],
        out_specs=[pl.BlockSpec((1, LANES), lambda i: (0, 0)), blk, row],
        out_shape=[jax.ShapeDtypeStruct((1, LANES), f32), jax.ShapeDtypeStruct((seq, D_MODEL), f32),
                   jax.ShapeDtypeStruct((1, D_MODEL), f32)],
        compiler_params=_cparams(1),
    )(x, w.reshape(1, D_MODEL), target)


LAYER_KEYS = ("norm_w", "w_in", "s5_lambda_re", "s5_lambda_im", "s5_b_re", "s5_b_im", "s5_c_re", "s5_c_im", "s5_d",
              "s5_log_step", "s5_w_glu", "sgu_ln_w", "sgu_ln_b", "sgu_w", "sgu_b", "m2_conv_w", "m2_conv_b",
              "m2_dt_bias", "m2_a_log", "m2_d", "m2_norm_w", "sc_conv_w", "merge_b", "w_branch", "w_out")


def local_step(x, target, layers, final_norm_w):
    saved = []
    for p in layers:
        x, sv = layer_fwd(x, p)
        saved.append(sv)
    loss, dx, dfw = loss_head(x, final_norm_w, target)
    grads = []
    for p, sv in zip(reversed(layers), reversed(saved)):
        dx, g = layer_bwd(dx, p, sv)
        grads.append(g)
    return loss[0, 0], dx, grads[::-1], dfw.reshape(D_MODEL)


MESH = pl.DeviceIdType.MESH
ANY = pl.BlockSpec(memory_space=pl.ANY)


def _me():
    return lax.axis_index("x"), lax.axis_index("y"), lax.axis_index("c")


def _other_chips(x, y):
    return [(1 - x, y), (x, 1 - y), (1 - x, 1 - y)]


def _rcopy(src, dst, send, recv, dev):
    return pltpu.make_async_remote_copy(src_ref=src, dst_ref=dst, send_sem=send, recv_sem=recv,
                                        device_id=dev, device_id_type=MESH)


def _route_cut(rows, dtype):
    tile = 2 * SUBL * (4 // jnp.dtype(dtype).itemsize)
    return rows // 2 if rows % tile == 0 else rows


def _comm_call(name, body, arrs, out_shape, n_remote, aliases=None):
    n = len(arrs)
    return pl.pallas_call(
        body, name=name, in_specs=[ANY] * n, out_specs=[ANY] * len(out_shape), out_shape=out_shape,
        scratch_shapes=[pltpu.SemaphoreType.DMA((n, n_remote)), pltpu.SemaphoreType.DMA((n, n_remote))],
        input_output_aliases=aliases or {},
        compiler_params=pltpu.CompilerParams(has_side_effects=True),
    )(*arrs)


def gather_chips(name, arrs):
    n = len(arrs)
    cut = [_route_cut(a.shape[1], a.dtype) for a in arrs]

    def body(*refs):
        ins, outs = refs[:n], refs[n:2 * n]
        send, recv = refs[2 * n:]
        x, y, c = _me()
        jme, jx, jy, jd = 2 * x + y, 2 * (1 - x) + y, 2 * x + 1 - y, 2 * (1 - x) + 1 - y
        to_x, to_y, sib = (1 - x, y, c), (x, 1 - y, c), (x, y, 1 - c)

        def part(ref, a, hi):
            return ref.at[pl.ds(cut[a], ref.shape[0] - cut[a])] if hi else ref.at[pl.ds(0, cut[a])]

        def cp(a, k, ref, dev):
            return _rcopy(ref, ref, send.at[a, k], recv.at[a, k], dev)

        split = [a for a in range(n) if cut[a] < arrs[a].shape[1]]
        sent = [_rcopy(ins[a].at[c], outs[a].at[jme, c], send.at[a, k], recv.at[a, k], dev)
                for a in range(n) for k, dev in ((0, to_x), (1, to_y))]
        for s in sent:
            s.start()
        for a in range(n):
            blk = outs[a].at[jx, c]
            cp(a, 0, blk, to_x).wait_recv()
            sent += [cp(a, 2, part(blk, a, False), to_y), cp(a, 4, blk, sib)]
            sent[-2].start()
            sent[-1].start()
        for a in range(n):
            blk = outs[a].at[jy, c]
            cp(a, 1, blk, to_y).wait_recv()
            sent.append(cp(a, 5, blk, sib))
            sent[-1].start()
            if a in split:
                sent.append(cp(a, 3, part(blk, a, True), to_x))
                sent[-1].start()
        for a in range(n):
            lo = part(outs[a].at[jd, c], a, False)
            cp(a, 2, lo, to_y).wait_recv()
            sent.append(cp(a, 6, lo, sib))
            sent[-1].start()
        for a in split:
            hi = part(outs[a].at[jd, c], a, True)
            cp(a, 3, hi, to_x).wait_recv()
            sent.append(cp(a, 7, hi, sib))
            sent[-1].start()
        for a in range(n):
            cp(a, 4, outs[a].at[jx, 1 - c], sib).wait_recv()
            cp(a, 5, outs[a].at[jy, 1 - c], sib).wait_recv()
            cp(a, 6, part(outs[a].at[jd, 1 - c], a, False), sib).wait_recv()
        for a in split:
            cp(a, 7, part(outs[a].at[jd, 1 - c], a, True), sib).wait_recv()
        for s in sent:
            s.wait_send()

    out_shape = [jax.ShapeDtypeStruct((4,) + a.shape, a.dtype) for a in arrs]
    got = _comm_call(name, body, arrs, out_shape, 8)
    jme = 2 * lax.axis_index("x") + lax.axis_index("y")
    return [lax.dynamic_update_index_in_dim(g, a, jme, 0) for g, a in zip(got, arrs)]


def swap_halves(name, arrs):
    n = len(arrs)

    def body(*refs):
        ins, outs = refs[:n], refs[n:2 * n]
        send, recv = refs[2 * n:]
        x, y, c = _me()
        remote = [_rcopy(ins[a].at[1 - c], outs[a], send.at[a, 0], recv.at[a, 0], (x, y, 1 - c)) for a in range(n)]
        for cp in remote:
            cp.start()
        for cp in remote:
            cp.wait()

    return _comm_call(name, body, arrs, [jax.ShapeDtypeStruct(a.shape[1:], a.dtype) for a in arrs], 1)


def exchange_chips(name, arrs):
    n = len(arrs)

    def body(*refs):
        ins, outs = refs[:n], refs[n:2 * n]
        send, recv = refs[2 * n:]
        x, y, c = _me()
        remote = [_rcopy(ins[a].at[2 * cx + cy], outs[a].at[k], send.at[a, k], recv.at[a, k], (cx, cy, c))
                  for a in range(n) for k, (cx, cy) in enumerate(_other_chips(x, y))]
        for cp in remote:
            cp.start()
        for cp in remote:
            cp.wait()

    return _comm_call(name, body, arrs, [jax.ShapeDtypeStruct((3,) + a.shape[1:], a.dtype) for a in arrs], 3)


def gather_cores(name, arrs):
    n = len(arrs)

    def body(*refs):
        bufs = refs[n:2 * n]
        send, recv = refs[2 * n:]
        x, y, c = _me()
        remote = [_rcopy(bufs[a].at[c], bufs[a].at[c], send.at[a, 0], recv.at[a, 0], (x, y, 1 - c)) for a in range(n)]
        for cp in remote:
            cp.start()
        for a in range(n):
            _rcopy(bufs[a].at[1 - c], bufs[a].at[1 - c], send.at[a, 0], recv.at[a, 0], (x, y, 1 - c)).wait_recv()
        for cp in remote:
            cp.wait_send()

    return _comm_call(name, body, arrs, [jax.ShapeDtypeStruct(a.shape, a.dtype) for a in arrs], 1,
                      aliases={a: a for a in range(n)})


ROW_BLOCK = 256


def esum(name, terms, rows, width, out_dtype, out_slots=None):
    tr = next((t for t in range(min(rows, ROW_BLOCK), 0, -SUBL) if rows % t == 0 and t % SUBL == 0), rows)
    where =jnp.stack([lax.axis_index("c"), 2 * lax.axis_index("x") + lax.axis_index("y")]).astype(jnp.int32)
    pick = {"c": 0, "j": 1}

    def body(s_ref, *refs):
        acc = refs[0][...].astype(f32)
        for r in refs[1:-1]:
            acc = acc + r[...].astype(f32)
        refs[-1][...] = acc.astype(out_dtype)

    specs = []
    for arr, lead in terms:
        if lead is None:
            specs.append(pl.BlockSpec((tr, width), lambda i, s: (i, 0)))
        elif isinstance(lead, str):
            specs.append(pl.BlockSpec((None, tr, width), lambda i, s, lead=lead: (s[pick[lead]], i, 0)))
        else:
            specs.append(pl.BlockSpec((None, tr, width), lambda i, s, lead=lead: (lead, i, 0)))
    if out_slots is None:
        out_spec = pl.BlockSpec((tr, width), lambda i, s: (i, 0))
        out_shape = jax.ShapeDtypeStruct((rows, width), out_dtype)
    else:
        out_spec = pl.BlockSpec((None, tr, width), lambda i, s: (s[0], i, 0))
        out_shape = jax.ShapeDtypeStruct((out_slots, rows, width), out_dtype)
    return pl.pallas_call(
        body, name=name,
        grid_spec=pltpu.PrefetchScalarGridSpec(num_scalar_prefetch=1, grid=(rows // tr,), in_specs=specs, out_specs=out_spec),
        out_shape=out_shape, compiler_params=_cparams(1),
    )(where, *[t[0] for t in terms])


def reduce_to_shards(parts):
    tags = [str(k) for k in range(len(parts))]
    theirs = swap_halves("rs_swap", parts)
    t1 = []
    for tag, p, th in zip(tags, parts, theirs):
        _, _, h, w = p.shape
        t1.append(esum("rs_add_cores" + tag, [(p.reshape(2, 4 * h, w), "c"), (th.reshape(4 * h, w), None)],
                       4 * h, w, p.dtype).reshape(4, h, w))
    landed = exchange_chips("rs_exchange", t1)
    red = []
    for tag, p, t, got in zip(tags, parts, t1, landed):
        _, _, h, w = p.shape
        red.append(esum("rs_add_chips" + tag, [(t, "j"), (got, 0), (got, 1), (got, 2)], h, w, f32, out_slots=2))
    return gather_cores("rs_gather", red)


def _adamw_step(w_ref, g_ref, m_ref, v_ref, d_ref, nm_ref, nv_ref):
    gv = g_ref[...]
    nm = ADAM_B1 * m_ref[...] + (1.0 - ADAM_B1) * gv
    nv = ADAM_B2 * v_ref[...] + (1.0 - ADAM_B2) * jnp.square(gv)
    m_hat = nm / (1.0 - ADAM_B1 ** ADAM_STEP)
    v_hat = nv / (1.0 - ADAM_B2 ** ADAM_STEP)
    d_ref[...] = -ADAM_LR * (m_hat / (jnp.sqrt(v_hat) + ADAM_EPS) + ADAM_WD * w_ref[...])
    nm_ref[...] = nm
    nv_ref[...] = nv


def adamw(name, w, g, m, v, tr=None):
    rows, rest = w.shape[0], w.shape[1:]
    if tr is None:
        tr = ROW_BLOCK if rows % ROW_BLOCK == 0 else rows
    assert rows % tr == 0

    def body(*refs):
        _adamw_step(*refs)

    spec = pl.BlockSpec((tr,) + rest, lambda i: (i,) + (0,) * len(rest))
    return pl.pallas_call(
        body, name=name, grid=(rows // tr,), in_specs=[spec] * 4, out_specs=[spec] * 3,
        out_shape=[jax.ShapeDtypeStruct(w.shape, f32)] * 3, compiler_params=_cparams(1),
    )(w, g, m, v)


def adamw_many(name, ws, gs, ms, vs):
    n = len(ws)

    def body(*refs):
        ins, outs = refs[:4 * n], refs[4 * n:]
        for k in range(n):
            _adamw_step(ins[k], ins[n + k], ins[2 * n + k], ins[3 * n + k], outs[k], outs[n + k], outs[2 * n + k])

    vmem = pl.BlockSpec(memory_space=pltpu.VMEM)
    res = pl.pallas_call(
        body, name=name, in_specs=[vmem] * (4 * n), out_specs=[vmem] * (3 * n),
        out_shape=[jax.ShapeDtypeStruct(w.shape, f32) for w in ws] * 3,
        compiler_params=pltpu.CompilerParams(vmem_limit_bytes=VMEM_LIMIT),
    )(*ws, *gs, *ms, *vs)
    return res[:n], res[n:2 * n], res[2 * n:]


PACK_W = 1024


def _pack(parts, halves, row_mult):
    flat = jnp.concatenate([p.reshape(-1) for p in parts])
    per = halves * row_mult * PACK_W
    total = -(-flat.size // per) * per
    flat = jnp.pad(flat, (0, total - flat.size))
    return flat.reshape(halves, total // (halves * PACK_W), PACK_W)


def _unpack(flat, shapes):
    out, pos = [], 0
    for s in shapes:
        n = int(np.prod(s))
        out.append(flat[pos:pos + n].reshape(s))
        pos += n
    return out


SHARDED_BIG = ("w_in", "w_branch", "w_out", "s5_w_glu")
SHARDED_SMALL = ("m2_conv_w", "sc_conv_w", "merge_b")
SHARD_AXIS = {"w_in": 2, "w_branch": 3, "w_out": 1, "s5_w_glu": 1, "m2_conv_w": 2, "sc_conv_w": 2, "merge_b": 2}
REPLICATED = ("norm_w", "s5_lambda_re", "s5_lambda_im", "s5_b_re", "s5_b_im", "s5_c_re", "s5_c_im", "s5_d", "s5_log_step",
              "sgu_ln_w", "sgu_ln_b", "sgu_w", "sgu_b", "m2_conv_b", "m2_dt_bias", "m2_a_log", "m2_d", "m2_norm_w")
WEIGHTS = ("norm_w", "w_in", "s5_lambda_re", "s5_lambda_im", "s5_b_re", "s5_b_im", "s5_c_re", "s5_c_im", "s5_d",
           "s5_log_step", "s5_w_glu", "sgu_ln_w", "sgu_ln_b", "sgu_w", "sgu_b", "m2_conv_w", "m2_conv_b", "m2_dt_bias",
           "m2_a_log", "m2_d", "m2_norm_w", "sc_conv_w", "merge_b", "w_branch", "w_out", "final_norm_w")
N_LAYERS = 2


SHARD_W = IN_DIM // 4
SHARD_PAD = -(-SHARD_W // LANES) * LANES
REGROUP_W = 3 * LANES


def _kernel_pieces():
    out, pos = [], 0
    for s, n in _col_segments():
        while n:
            take = min(n, SHARD_W - s % SHARD_W)
            out.append((pos, s, take))
            pos, s, n = pos + take, s + take, n - take
    return out


def regroup_cols(name, src, steps, out_shape, out=None):
    n_src, rows, width = src.shape
    ow, win = REGROUP_W, REGROUP_W + LANES
    k_max = max(len(p) for _, _, p in steps)
    assert width % LANES == 0

    def body(*refs):
        src_ref, out_ref = refs[0], refs[-5]
        wbuf, obuf, sem_in, sem_out = refs[-4:]

        def fetch(q, slot):
            started = []
            for p, (s, col, lo, hi) in enumerate(steps[q][2]):
                w0 = col // LANES * LANES
                wlen = min(win, width - w0)
                cp = pltpu.make_async_copy(src_ref.at[s, :, pl.ds(w0, wlen)], wbuf.at[slot, p, :, pl.ds(0, wlen)], sem_in.at[slot, p])
                cp.start()
                started.append((cp, wlen, col - w0 - lo, lo, hi))
            return started

        pend, writes = fetch(0, 0), [None, None]
        for q, (t, b, _) in enumerate(steps):
            slot = q % 2
            nxt = fetch(q + 1, 1 - slot) if q + 1 < len(steps) else []
            acc = [jnp.zeros((rows, LANES), f32) for _ in range(ow // LANES)]
            for p, (cp, wlen, shift, lo, hi) in enumerate(pend):
                cp.wait()
                for k in range(ow // LANES):
                    c_lo, c_hi = max(lo, k * LANES), min(hi, (k + 1) * LANES)
                    if c_lo >= c_hi:
                        continue
                    wb = (c_lo + shift) // LANES
                    alen = min(2 * LANES, wlen - wb * LANES)
                    a = wbuf[slot, p, :, wb * LANES:wb * LANES + alen]
                    r, c = _rows((alen, LANES)), _cols((alen, LANES))
                    sh = shift + (k - wb) * LANES
                    sel = (r == c + sh) & (c >= c_lo - k * LANES) & (c < c_hi - k * LANES)
                    acc[k] = acc[k] + jnp.dot(a, sel.astype(bf16), preferred_element_type=f32)
            if writes[slot] is not None:
                writes[slot].wait()
            for k in range(ow // LANES):
                obuf[slot, :, k * LANES:(k + 1) * LANES] = acc[k].astype(bf16)
            writes[slot] = pltpu.make_async_copy(obuf.at[slot], out_ref.at[t, :, pl.ds(b * ow, ow)], sem_out.at[slot])
            writes[slot].start()
            pend = nxt
        for wr in writes:
            if wr is not None:
                wr.wait()

    operands, io_alias = [src], {}
    if out is not None:
        operands.append(out)
        io_alias = {1: 0}
    return pl.pallas_call(
        body, name=name, in_specs=[ANY] * len(operands), out_specs=ANY,
        out_shape=jax.ShapeDtypeStruct(out_shape, bf16), input_output_aliases=io_alias,
        scratch_shapes=[pltpu.VMEM((2, k_max, rows, win), bf16), pltpu.VMEM((2, rows, ow), bf16),
                        pltpu.SemaphoreType.DMA((2, k_max)), pltpu.SemaphoreType.DMA((2,))],
        compiler_params=pltpu.CompilerParams(vmem_limit_bytes=VMEM_LIMIT),
    )(*operands)


def _steps_to_kernel_cols(layer):
    steps = []
    for o in range(PW // REGROUP_W):
        pieces = []
        for pos, s, n in _kernel_pieces():
            lo, hi = max(pos, o * REGROUP_W), min(pos + n, (o + 1) * REGROUP_W)
            if lo < hi:
                ref_col = s + lo - pos
                pieces.append(((ref_col // SHARD_W) * N_LAYERS + layer, ref_col % SHARD_W, lo - o * REGROUP_W, hi - o * REGROUP_W))
        steps.append((0, o, pieces))
    return steps


def _steps_to_shards(layer):
    steps = []
    for j in range(4):
        for b in range(SHARD_PAD // REGROUP_W):
            start, stop = j * SHARD_W + b * REGROUP_W, min(j * SHARD_W + (b + 1) * REGROUP_W, (j + 1) * SHARD_W)
            pieces = []
            for pos, s, n in _kernel_pieces():
                lo, hi = max(s, start), min(s + n, stop)
                if lo < hi:
                    pieces.append((0, pos + lo - s, lo - start, hi - start))
            steps.append((layer * 4 + j, b, pieces))
    return steps


def _gather_weights(w):
    w_in = jnp.pad(w["w_in"].astype(bf16), ((0, 0), (0, 0), (0, SHARD_PAD - SHARD_W)))
    arrs = [w_in, w["w_branch"].reshape(N_LAYERS, N_BRANCH * BW, -1).astype(bf16),
            w["w_out"].astype(bf16), w["s5_w_glu"].astype(bf16), w["m2_conv_w"], w["sc_conv_w"], w["merge_b"]]
    got = gather_chips("ag_weights", arrs)
    cols = lambda t: jnp.transpose(t, (1, 0, 2)).reshape(t.shape[1], -1)
    layers = []
    for i in range(N_LAYERS):
        p = {k: w[k][i] for k in REPLICATED}
        p["w_in"] = regroup_cols(f"w_in_cols{i}", got[0].reshape(4 * N_LAYERS, D_MODEL, SHARD_PAD), _steps_to_kernel_cols(i),
                                 (1, D_MODEL, PW))[0]
        p["w_branch"] = cols(got[1][:, i])
        p["w_out"] = got[2][:, i].reshape(D_MODEL, D_MODEL)
        p["s5_w_glu"] = got[3][:, i].reshape(BW, BW)
        p["m2_conv_w"], p["sc_conv_w"], p["merge_b"] = cols(got[4][:, i]), cols(got[5][:, i]), cols(got[6][:, i])
        layers.append(p)
    return layers


def _reduce_grads(grads, d_final, w):
    to_chips = lambda t: jnp.transpose(t.reshape(t.shape[0], 4, -1), (1, 0, 2))
    stack = lambda f: jnp.stack([f(g) for g in grads])
    dw_in = None
    for i, g in enumerate(grads):
        dw_in = regroup_cols(f"w_in_shards{i}", g["w_in"][None], _steps_to_shards(i), (4 * N_LAYERS, D_MODEL, SHARD_PAD), out=dw_in)
    parts = [dw_in.reshape(N_LAYERS, 4, D_MODEL, SHARD_PAD),
             stack(lambda g: to_chips(g["w_branch"].reshape(N_BRANCH * BW, D_MODEL)).astype(bf16)),
             stack(lambda g: g["w_out"].reshape(4, D_MODEL // 4, D_MODEL).astype(bf16)),
             stack(lambda g: g["s5_w_glu"].reshape(4, BW // 4, BW).astype(bf16))]
    rep = jnp.concatenate([stack(lambda g: g[k]).reshape(-1) for k in REPLICATED] + [d_final.reshape(-1)])
    quarter = -(-rep.size // (4 * 2 * SUBL * PACK_W)) * (2 * SUBL * PACK_W)
    rep = jnp.pad(rep, (0, 4 * quarter - rep.size))
    small = []
    for j in range(4):
        sharded = [stack(lambda g: to_chips(g[k])[j]) for k in SHARDED_SMALL]
        small.append(_pack(sharded + [rep[j * quarter:(j + 1) * quarter]], 2, SUBL))
    parts.append(jnp.stack(small, axis=1))
    red = reduce_to_shards(parts)
    out = {"w_in": red[0][:, :, :SHARD_W], "w_branch": red[1].reshape(w["w_branch"].shape), "w_out": red[2], "s5_w_glu": red[3]}
    small_flat = red[4].reshape(-1)
    n_small = sum(int(np.prod(w[k].shape)) for k in SHARDED_SMALL)
    out.update(zip(SHARDED_SMALL, _unpack(small_flat, [w[k].shape for k in SHARDED_SMALL])))
    mine = small_flat[n_small:n_small + quarter].reshape(2, quarter // (2 * PACK_W), PACK_W)
    rep_all = gather_chips("ag_small_grads", [mine])[0].reshape(-1)
    names = REPLICATED + ("final_norm_w",)
    out.update(zip(names, _unpack(rep_all, [w[k].shape for k in names])))
    return out


def _update(w, g, m, v):
    d, nm, nv = {}, {}, {}
    flat2 = lambda a: a.reshape(-1, a.shape[-1])
    cols_major = lambda a: jnp.transpose(a, (2, 0, 1))
    res = adamw("adamw_w_in", *[cols_major(t["w_in"]) for t in (w, g, m, v)], tr=SHARD_W // 14)
    d["w_in"], nm["w_in"], nv["w_in"] = (jnp.transpose(r, (1, 2, 0)) for r in res)
    for k in SHARDED_BIG[1:]:
        res = adamw("adamw_" + k, *[flat2(t[k]) for t in (w, g, m, v)])
        d[k], nm[k], nv[k] = (r.reshape(w[k].shape) for r in res)
    for k in ("s5_b_re", "s5_b_im"):
        res = adamw("adamw_" + k, *[flat2(t[k]) for t in (w, g, m, v)])
        d[k], nm[k], nv[k] = (r.reshape(w[k].shape) for r in res)
    rest = [k for k in WEIGHTS if k not in SHARDED_BIG + ("s5_b_re", "s5_b_im")]
    two_d = lambda a: a.reshape(1, -1) if a.ndim == 1 else a
    res = adamw_many("adamw_rest", *[[two_d(t[k]) for k in rest] for t in (w, g, m, v)])
    for tgt, rs in zip((d, nm, nv), res):
        tgt.update({k: r.reshape(w[k].shape) for k, r in zip(rest, rs)})
    return d, nm, nv


def kernel(x, norm_w, w_in, s5_lambda_re, s5_lambda_im, s5_b_re, s5_b_im, s5_c_re, s5_c_im, s5_d, s5_log_step, s5_w_glu, sgu_ln_w, sgu_ln_b, sgu_w, sgu_b, m2_conv_w, m2_conv_b, m2_dt_bias, m2_a_log, m2_d, m2_norm_w, sc_conv_w, merge_b, w_branch, w_out, final_norm_w, loss_target, m_norm_w, m_w_in, m_s5_lambda_re, m_s5_lambda_im, m_s5_b_re, m_s5_b_im, m_s5_c_re, m_s5_c_im, m_s5_d, m_s5_log_step, m_s5_w_glu, m_sgu_ln_w, m_sgu_ln_b, m_sgu_w, m_sgu_b, m_m2_conv_w, m_m2_conv_b, m_m2_dt_bias, m_m2_a_log, m_m2_d, m_m2_norm_w, m_sc_conv_w, m_merge_b, m_w_branch, m_w_out, m_final_norm_w, v_norm_w, v_w_in, v_s5_lambda_re, v_s5_lambda_im, v_s5_b_re, v_s5_b_im, v_s5_c_re, v_s5_c_im, v_s5_d, v_s5_log_step, v_s5_w_glu, v_sgu_ln_w, v_sgu_ln_b, v_sgu_w, v_sgu_b, v_m2_conv_w, v_m2_conv_b, v_m2_dt_bias, v_m2_a_log, v_m2_d, v_m2_norm_w, v_sc_conv_w, v_merge_b, v_w_branch, v_w_out, v_final_norm_w):
    given = dict(locals())
    w = {k: given[k] for k in WEIGHTS}
    m = {k: given["m_" + k] for k in WEIGHTS}
    v = {k: given["v_" + k] for k in WEIGHTS}
    layers = _gather_weights(w)
    loss, dx, grads, d_final = local_step(x[0], loss_target[0], layers, final_norm_w)
    loss = lax.psum(loss, ("x", "y", "c"))
    g = _reduce_grads(grads, d_final, w)
    d, nm, nv = _update(w, g, m, v)
    return (loss, dx[None], *[g[k] for k in WEIGHTS], *[d[k] for k in WEIGHTS],
            *[nm[k] for k in WEIGHTS], *[nv[k] for k in WEIGHTS])
```

```python
import functools
from typing import Any, Callable, NamedTuple

import numpy as np
import jax
import jax.numpy as jnp
from jax import lax
from jax.experimental import pallas as pl
from jax.experimental.pallas import tpu as pltpu

f32 = jnp.float32
bf16 = jnp.bfloat16

D_MODEL = 1024
BW = 512
N_BRANCH = 4
EPS = 1e-6
S5_GROUPS, S5_P, S5_N = 32, 16, 64
S5_NS = S5_GROUPS * S5_N
CHUNK = 128
M2_HEADS, M2_HEAD_DIM, M2_GROUPS, M2_STATE = 8, 64, 2, 128
IN_DIM = 10248
PW = 10368
PW_MAIN = 10240
LANES = 128
VMEM_LIMIT = 60 * 1024 * 1024

ADAM_LR, ADAM_B1, ADAM_B2, ADAM_EPS, ADAM_WD, ADAM_STEP = 0.001, 0.9, 0.999, 1e-08, 0.01, 10

C_MERGE = 0
C_SC = 4096
C_SGU = 6144
C_S5G = 7680
C_S5U = 8192
C_M2Z = 8704
C_XBC = 9216
C_DT = 10240


def _col_segments():
    segs = [(6152, 4096)]
    for j in range(4):
        segs += [(4104 + 128 * j, 128), (4616 + 128 * j, 128), (5128 + 128 * j, 128), (5640 + 128 * j, 128)]
    segs += [(1024, 1536), (512, 512), (0, 512), (2560, 512), (3072, 1024), (4096, 8)]
    return segs


NN = ((1,), (0,))
NT = ((1,), (1,))
TN = ((0,), (0,))


def _bd(a, b, dims):
    return lax.dot_general(a.astype(bf16), b.astype(bf16), (dims, ((), ())), preferred_element_type=f32)


def _hd(a, b, dims):
    return lax.dot_general(a, b, (dims, ((), ())), precision=lax.Precision.HIGHEST, preferred_element_type=f32)


def _make_dots(raw):
    @jax.custom_vjp
    def nn(a, b):
        return raw(a, b, NN)
    nn.defvjp(lambda a, b: (raw(a, b, NN), (a, b)), lambda r, g: (raw(g, r[1], NT), raw(r[0], g, TN)))

    @jax.custom_vjp
    def nt(a, b):
        return raw(a, b, NT)
    nt.defvjp(lambda a, b: (raw(a, b, NT), (a, b)), lambda r, g: (raw(g, r[1], NN), raw(g, r[0], TN)))

    @jax.custom_vjp
    def tn(a, b):
        return raw(a, b, TN)
    tn.defvjp(lambda a, b: (raw(a, b, TN), (a, b)), lambda r, g: (raw(r[1], g, NT), raw(r[0], g, NN)))
    return nn, nt, tn


bdot, bdot_nt, bdot_tn = _make_dots(_bd)
hdot, hdot_nt, hdot_tn = _make_dots(_hd)


@jax.custom_vjp
def bdot_w(a, w, shadow):
    return _bd(a, w, NN)


bdot_w.defvjp(lambda a, w, s: (_bd(a, w, NN), (a, w)),
              lambda r, g: (_bd(g, r[1], NT), jnp.zeros_like(r[1]), _bd(r[0], g, TN)))


def _rows(shape):
    return lax.broadcasted_iota(jnp.int32, shape, 0)


def _cols(shape):
    return lax.broadcasted_iota(jnp.int32, shape, 1)


def _shift_down(x, s):
    return jnp.where(_rows(x.shape) < s, 0.0, pltpu.roll(x, s, 0))


def _shift_up(x, s):
    n = x.shape[0]
    return jnp.where(_rows(x.shape) >= n - s, 0.0, pltpu.roll(x, n - s, 0))


@functools.partial(jax.custom_vjp, nondiff_argnums=(1,))
def shift(x, s):
    return _shift_down(x, s) if s else x


shift.defvjp(lambda x, s: (shift(x, s), None), lambda s, _, g: (_shift_up(g, s) if s else g,))


def _row_of(w, k):
    return jnp.sum(jnp.where(_rows(w.shape) == k, w, 0.0), axis=0, keepdims=True)


def _lane_mask(width, lo, hi):
    c = _cols((1, width))
    return ((c >= lo) & (c < hi)).astype(f32)


def _expand(rows, width, per):
    return (_cols((rows, width)) // per == _rows((rows, width))).astype(f32)


class A(NamedTuple):
    arr: Any
    block: tuple
    imap: Callable
    shadow: bool = False


class O(NamedTuple):
    shape: tuple
    dtype: Any
    block: tuple
    imap: Callable
    alias: Any = None


class R(NamedTuple):
    arg: int
    out: int
    off: Any = None
    acc: bool = False


def _cparams(n_grid):
    return pltpu.CompilerParams(dimension_semantics=("arbitrary",) * n_grid, vmem_limit_bytes=VMEM_LIMIT)


def _ispec(block, imap, n, rev):
    if rev:
        return pl.BlockSpec(block, lambda i: imap(n - 1 - i))
    return pl.BlockSpec(block, imap)


def _load(ref, a):
    v = ref[...]
    if a.shadow:
        return (v, jnp.zeros(v.shape, f32))
    return v.astype(f32)


def _save_spec(shape, n, rev):
    nd = len(shape)
    return _ispec((None,) + tuple(shape), lambda i: (i,) + (0,) * nd, n, rev)


def block_fwd(name, f, n, args, outs, carries=()):
    n_in, n_out, n_c = len(args), len(outs), len(carries)

    def body(*refs):
        ins, out_r = refs[:n_in], refs[n_in:n_in + n_out]
        saves, cs = refs[n_in + n_out:n_in + n_out + n_c], refs[n_in + n_out + n_c:]
        if n_c:
            @pl.when(pl.program_id(0) == 0)
            def _():
                for c in cs:
                    c[...] = jnp.zeros(c.shape, f32)
        vals = [_load(r, a) for r, a in zip(ins, args)]
        cv = [c[...] for c in cs]
        for s, v in zip(saves, cv):
            s[...] = v
        res = f(*vals, *cv)
        for r, v in zip(out_r, res[:n_out]):
            r[...] = v.astype(r.dtype)
        for c, v in zip(cs, res[n_out:]):
            c[...] = v

    out_shape = [jax.ShapeDtypeStruct(o.shape, o.dtype) for o in outs]
    out_specs = [pl.BlockSpec(o.block, o.imap) for o in outs]
    for shp in carries:
        out_shape.append(jax.ShapeDtypeStruct((n,) + tuple(shp), f32))
        out_specs.append(_save_spec(shp, n, False))
    return pl.pallas_call(
        body, name=name, grid=(n,),
        in_specs=[pl.BlockSpec(a.block, a.imap) for a in args],
        out_specs=out_specs, out_shape=out_shape,
        scratch_shapes=[pltpu.VMEM(tuple(shp), f32) for shp in carries],
        compiler_params=_cparams(1),
    )(*[a.arr for a in args])


def block_bwd(name, f, n, args, cots, gouts, routes, saved=(), rev=False):
    n_in, n_cot, n_c, n_go = len(args), len(cots), len(saved), len(gouts)
    diff = []
    for r in routes:
        if r.arg not in diff:
            diff.append(r.arg)
    aliases = [(k, o.alias) for k, o in enumerate(gouts) if o.alias is not None]

    def body(*refs):
        ins = refs[:n_in]
        cot_r = refs[n_in:n_in + n_cot]
        sav_r = refs[n_in + n_cot:n_in + n_cot + n_c]
        base = n_in + n_cot + n_c + len(aliases)
        go_r = refs[base:base + n_go]
        dcs = refs[base + n_go:]
        step = pl.program_id(0)
        if n_c:
            @pl.when(step == 0)
            def _():
                for d in dcs:
                    d[...] = jnp.zeros(d.shape, f32)
        vals = [_load(r, a) for r, a in zip(ins, args)]
        cv = [s[...] for s in sav_r]
        nd = len(diff)

        def g(*dv):
            full = list(vals)
            for idx, v in zip(diff, dv[:nd]):
                full[idx] = (vals[idx][0], v) if args[idx].shadow else v
            return tuple(f(*full, *dv[nd:]))

        primals = [vals[i][1] if args[i].shadow else vals[i] for i in diff] + cv
        _, vjp_fn = jax.vjp(g, *primals)
        ct = tuple([r[...].astype(f32) for r in cot_r] + [d[...] for d in dcs])
        grads = vjp_fn(ct)
        for r in routes:
            gr = grads[diff.index(r.arg)]
            ref = go_r[r.out]
            if r.acc:
                @pl.when(step == 0)
                def _(ref=ref, gr=gr):
                    ref[...] = gr.astype(ref.dtype)

                @pl.when(step > 0)
                def _(ref=ref, gr=gr):
                    ref[...] += gr.astype(ref.dtype)
            elif r.off is None:
                ref[...] = gr.astype(ref.dtype)
            else:
                ref[:, r.off:r.off + gr.shape[1]] = gr.astype(ref.dtype)
        for d, gr in zip(dcs, grads[nd:]):
            d[...] = gr

    in_specs = [_ispec(a.block, a.imap, n, rev) for a in list(args) + list(cots)]
    in_specs += [_save_spec(s.shape[1:], n, rev) for s in saved]
    in_specs += [pl.BlockSpec(memory_space=pl.ANY) for _ in aliases]
    operands = [a.arr for a in list(args) + list(cots)] + list(saved) + [arr for _, arr in aliases]
    io_alias = {n_in + n_cot + n_c + j: k for j, (k, _) in enumerate(aliases)}
    return pl.pallas_call(
        body, name=name, grid=(n,),
        in_specs=in_specs,
        out_specs=[_ispec(o.block, o.imap, n, rev) for o in gouts],
        out_shape=[jax.ShapeDtypeStruct(o.shape, o.dtype) for o in gouts],
        scratch_shapes=[pltpu.VMEM(tuple(s.shape[1:]), f32) for s in saved],
        input_output_aliases=io_alias,
        compiler_params=_cparams(1),
    )(*operands)


class Op(NamedTuple):
    arr: Any
    row: int = 0
    col: int = 0


def mm(name, mode, pairs, m, n, tm, tn, out_dtype=f32, add=None, out=None, out_col=0, out_width=None):
    tm, tn = min(tm, m), min(tn, n)
    assert m % tm == 0 and n % tn == 0
    in_specs, operands = [], []
    for a, b, k, _ in pairs:
        if mode == TN:
            assert a.row % k == 0 and a.col % tm == 0
            in_specs.append(pl.BlockSpec((k, tm), lambda j, i, a=a, k=k: (a.row // k, i + a.col // tm)))
        else:
            assert a.col % k == 0 and a.row % tm == 0
            in_specs.append(pl.BlockSpec((tm, k), lambda j, i, a=a, k=k: (i + a.row // tm, a.col // k)))
        if mode == NT:
            assert b.col % k == 0 and b.row % tn == 0
            in_specs.append(pl.BlockSpec((tn, k), lambda j, i, b=b, k=k: (j + b.row // tn, b.col // k)))
        else:
            assert b.row % k == 0 and b.col % tn == 0
            in_specs.append(pl.BlockSpec((k, tn), lambda j, i, b=b, k=k: (b.row // k, j + b.col // tn)))
        operands += [a.arr, b.arr]
    n_p = len(pairs)
    if add is not None:
        assert add.col % tn == 0
        in_specs.append(pl.BlockSpec((tm, tn), lambda j, i: (i, j + add.col // tn)))
        operands.append(add.arr)
    io_alias = {}
    if out is not None:
        assert out_col % tn == 0
        in_specs.append(pl.BlockSpec(memory_space=pl.ANY))
        operands.append(out)
        io_alias = {len(operands) - 1: 0}
        out_shape = jax.ShapeDtypeStruct(out.shape, out.dtype)
    else:
        out_shape = jax.ShapeDtypeStruct((m, out_width or n), out_dtype)
    signs = [p[3] for p in pairs]

    def body(*refs):
        o = refs[-1]
        acc = None
        for p in range(n_p):
            t = _bd(refs[2 * p][...], refs[2 * p + 1][...], mode)
            t = t if signs[p] > 0 else -t
            acc = t if acc is None else acc + t
        if add is not None:
            acc = acc + refs[2 * n_p][...].astype(f32)
        o[...] = acc.astype(o.dtype)

    return pl.pallas_call(
        body, name=name, grid=(n // tn, m // tm),
        in_specs=in_specs,
        out_specs=pl.BlockSpec((tm, tn), lambda j, i: (i, j + out_col // tn)),
        out_shape=out_shape, input_output_aliases=io_alias,
        compiler_params=_cparams(2),
    )(*operands)


SCAN_LANES = 512
SUBL = 8


def _cmul(p, q):
    return (p[0] * q[0] - p[1] * q[1], p[0] * q[1] + p[1] * q[0])


def _powers(a):
    a2 = _cmul(a, a)
    a4 = _cmul(a2, a2)
    a6 = _cmul(a4, a2)
    return [a, a2, _cmul(a2, a), a4, _cmul(a4, a), a6, _cmul(a6, a), _cmul(a4, a4)]


def _table(pw, order, w):
    row = _rows((SUBL, w))
    re = sum(jnp.where(row == t, pw[k][0], 0.0) for t, k in enumerate(order))
    im = sum(jnp.where(row == t, pw[k][1], 0.0) for t, k in enumerate(order))
    return re, im


def s5_scan_fwd(bu_re, bu_im, a_re, a_im):
    seq, ns = bu_re.shape
    w, nb = SCAN_LANES, ns // SCAN_LANES

    def body(b_re, b_im, ar, ai, s_re, s_im):
        a = (ar[...], ai[...])
        pw = _powers(a)
        tab = _table(pw, list(range(SUBL)), w)
        row = _rows((SUBL, w))

        def step(i, carry):
            t0 = pl.multiple_of(i * SUBL, SUBL)
            x = (b_re[pl.ds(t0, SUBL), :], b_im[pl.ds(t0, SUBL), :])
            for d, k in ((1, 0), (2, 1), (4, 3)):
                sh = (jnp.where(row < d, 0.0, pltpu.roll(x[0], d, 0)), jnp.where(row < d, 0.0, pltpu.roll(x[1], d, 0)))
                t = _cmul(pw[k], sh)
                x = (x[0] + t[0], x[1] + t[1])
            t = _cmul(tab, carry)
            x = (x[0] + t[0], x[1] + t[1])
            s_re[pl.ds(t0, SUBL), :] = x[0]
            s_im[pl.ds(t0, SUBL), :] = x[1]
            return (x[0][SUBL - 1:, :], x[1][SUBL - 1:, :])

        z = jnp.zeros((1, w), f32)
        lax.fori_loop(0, seq // SUBL, step, (z, z), unroll=2)

    strip = pl.BlockSpec((seq, w), lambda j: (0, j))
    lane = pl.BlockSpec((1, w), lambda j: (0, j))
    return pl.pallas_call(
        body, name="s5_scan_fwd", grid=(nb,),
        in_specs=[strip, strip, lane, lane],
        out_specs=[strip, strip],
        out_shape=[jax.ShapeDtypeStruct((seq, ns), f32)] * 2,
        compiler_params=_cparams(1),
    )(bu_re, bu_im, a_re, a_im)


def s5_scan_bwd(ds_re, ds_im, s_re, s_im, a_re, a_im):
    seq, ns = ds_re.shape
    w, nb = SCAN_LANES, ns // SCAN_LANES
    nblk = seq // SUBL

    def body(g_re, g_im, sr, si, ar, ai, l_re, l_im, da_re, da_im):
        a = (ar[...], -ai[...])
        pw = _powers(a)
        tab = _table(pw, [SUBL - 1 - t for t in range(SUBL)], w)
        row = _rows((SUBL, w))

        def step(kk, carry):
            c_re, c_im, acc_re, acc_im = carry
            i = nblk - 1 - kk
            t0 = pl.multiple_of(i * SUBL, SUBL)
            x = (g_re[pl.ds(t0, SUBL), :], g_im[pl.ds(t0, SUBL), :])
            for d, k in ((1, 0), (2, 1), (4, 3)):
                sh = (jnp.where(row >= SUBL - d, 0.0, pltpu.roll(x[0], SUBL - d, 0)),
                      jnp.where(row >= SUBL - d, 0.0, pltpu.roll(x[1], SUBL - d, 0)))
                t = _cmul(pw[k], sh)
                x = (x[0] + t[0], x[1] + t[1])
            t = _cmul(tab, (c_re, c_im))
            x = (x[0] + t[0], x[1] + t[1])
            l_re[pl.ds(t0, SUBL), :] = x[0]
            l_im[pl.ds(t0, SUBL), :] = x[1]
            tp = jnp.maximum(t0 - 1, 0)
            live = (i > 0).astype(f32)
            p_re = sr[pl.ds(tp, 1), :] * live
            p_im = si[pl.ds(tp, 1), :] * live
            sp_re = jnp.where(row == 0, p_re, pltpu.roll(sr[pl.ds(t0, SUBL), :], 1, 0))
            sp_im = jnp.where(row == 0, p_im, pltpu.roll(si[pl.ds(t0, SUBL), :], 1, 0))
            acc_re = acc_re + x[0] * sp_re + x[1] * sp_im
            acc_im = acc_im + x[1] * sp_re - x[0] * sp_im
            return (x[0][:1, :], x[1][:1, :], acc_re, acc_im)

        z1 = jnp.zeros((1, w), f32)
        z8 = jnp.zeros((SUBL, w), f32)
        _, _, acc_re, acc_im = lax.fori_loop(0, nblk, step, (z1, z1, z8, z8), unroll=2)
        da_re[...] = jnp.sum(acc_re, axis=0, keepdims=True)
        da_im[...] = jnp.sum(acc_im, axis=0, keepdims=True)

    strip = pl.BlockSpec((seq, w), lambda j: (0, j))
    lane = pl.BlockSpec((1, w), lambda j: (0, j))
    return pl.pallas_call(
        body, name="s5_scan_bwd", grid=(nb,),
        in_specs=[strip, strip, strip, strip, lane, lane],
        out_specs=[strip, strip, lane, lane],
        out_shape=[jax.ShapeDtypeStruct((seq, ns), f32)] * 2 + [jax.ShapeDtypeStruct((1, ns), f32)] * 2,
        compiler_params=_cparams(1),
    )(ds_re, ds_im, s_re, s_im, a_re, a_im)


def _rms(x, w):
    return x * lax.rsqrt(jnp.mean(x * x, axis=-1, keepdims=True) + EPS) * w


def f_rms(x, w):
    return (_rms(x, w),)


def f_rms_res(x, w):
    return (_rms(x, w), x)


def f_s5_prep(lam_re, lam_im, log_step, b_re, b_im):
    e = _expand(log_step.shape[1], S5_NS, S5_N)
    step = hdot(jnp.exp(log_step), e)
    mag = jnp.exp(lam_re * step)
    ab_re, ab_im = mag * jnp.cos(lam_im * step), mag * jnp.sin(lam_im * step)
    den = lam_re * lam_re + lam_im * lam_im
    nr = ab_re - 1.0
    coef_re = (nr * lam_re + ab_im * lam_im) / den
    coef_im = (ab_im * lam_re - nr * lam_im) / den
    bb_re, bb_im = coef_re * b_re - coef_im * b_im, coef_re * b_im + coef_im * b_re
    sel = (_rows((BW, S5_P)) % S5_P == _cols((BW, S5_P))).astype(f32)
    blk = _rows((BW, S5_NS)) // S5_P == _cols((BW, S5_NS)) // S5_N
    rows_bd = lambda t: jnp.where(blk, hdot(sel, t), 0.0)
    return (ab_re, ab_im, rows_bd(bb_re), rows_bd(bb_im))


def f_s5_c(c_re, c_im):
    sel_t = (_cols((S5_P, BW)) % S5_P == _rows((S5_P, BW))).astype(f32)
    blk_t = _rows((S5_NS, BW)) // S5_N == _cols((S5_NS, BW)) // S5_P
    cols_bd = lambda t: jnp.where(blk_t, hdot_tn(t, sel_t), 0.0)
    return (cols_bd(c_re), cols_bd(c_im))


def f_s5_act(y_lin, u, gate, d, w_glu):
    y = jax.nn.gelu(y_lin + d * u)
    y = y * jax.nn.sigmoid(bdot_w(y, *w_glu))
    return (y * jax.nn.silu(gate),)


def f_sgu(u, v, gate, ln_w, ln_b, *rest):
    w_s, b_pad = rest[:8], rest[8]
    t = u.shape[0]
    u32, v32 = jax.nn.gelu(u), jax.nn.gelu(v)
    mu = jnp.mean(v32, axis=-1, keepdims=True)
    var = jnp.mean(jnp.square(v32 - mu), axis=-1, keepdims=True)
    vn = (v32 - mu) * lax.rsqrt(var + EPS) * ln_w + ln_b
    tri = _rows((t, t)) >= _cols((t, t))
    s = hdot_tn(b_pad, _expand(LANES, BW, BW // 8))
    for h in range(8):
        s = s + bdot(jnp.where(tri, w_s[h], 0.0), vn) * _lane_mask(BW, 64 * h, 64 * h + 64)
    return (u32 * s * jax.nn.silu(gate),)


def f_m2_conv(x, w, b):
    return (sum(_row_of(w, k) * shift(x, 3 - k) for k in range(4)) + b,)


def f_sc(bg, cg, h, gate, w):
    z = cg * h
    conv = sum(_row_of(w, k) * shift(z, 2 - k) for k in range(3))
    return (bg * conv * jax.nn.silu(gate),)


def f_m2(z, xc, b0, b1, c0, c1, dt_raw, dt_bias, a_log, d_par, norm_w, st):
    q = z.shape[0]
    x = jax.nn.silu(xc)
    bm, cm = (jax.nn.silu(b0), jax.nn.silu(b1)), (jax.nn.silu(c0), jax.nn.silu(c1))
    dt = jax.nn.softplus(dt_raw + dt_bias)
    da = dt * (-jnp.exp(a_log))
    tri = _rows((q, q)) >= _cols((q, q))
    acs = hdot(tri.astype(f32), da)
    e = _expand(LANES, BW, M2_HEAD_DIM)
    dt_f, acs_f = hdot(dt, e), hdot(acs, e)
    last = _rows((q, BW)) == q - 1
    alast_f = jnp.sum(jnp.where(last, acs_f, 0.0), axis=0, keepdims=True)
    xdt = x * dt_f
    xdec = xdt * jnp.exp(alast_f - acs_f)
    acs_t = acs.T
    st_new = st * jnp.exp(alast_f)
    y_diag, y_off = 0.0, 0.0
    for g in range(M2_GROUPS):
        gm = _lane_mask(BW, 256 * g, 256 * g + 256)
        cb = bdot_nt(cm[g], bm[g])
        st_new = st_new + bdot_tn(bm[g], xdec * gm)
        y_off = y_off + bdot(cm[g], st) * gm
        for hh in range(M2_HEADS // M2_GROUPS):
            h = g * (M2_HEADS // M2_GROUPS) + hh
            col = jnp.sum(jnp.where(_cols((q, LANES)) == h, acs, 0.0), axis=1, keepdims=True)
            row = jnp.sum(jnp.where(_rows((LANES, q)) == h, acs_t, 0.0), axis=0, keepdims=True)
            decay = jnp.exp(jnp.where(tri, col - row, -1e30))
            y_diag = y_diag + bdot(cb * decay, xdt) * _lane_mask(BW, 64 * h, 64 * h + 64)
    d_f = sum(jnp.sum(jnp.where(_cols((1, LANES)) == h, d_par, 0.0), axis=1, keepdims=True)
              * _lane_mask(BW, 64 * h, 64 * h + 64) for h in range(M2_HEADS))
    y = y_diag + y_off * jnp.exp(acs_f) + d_f * x
    y = y * jax.nn.silu(z)
    return (_rms(y, norm_w), st_new)


def f_branch_mix(*v):
    ys, lg, wb, mb = v[0:4], v[4:8], v[8:12], v[12:16]
    return (sum(jax.nn.sigmoid(lg[k] + mb[k]) * bdot_w(ys[k], *wb[k]) for k in range(N_BRANCH)),)


def _param(arr, shadow=False):
    return A(arr, tuple(arr.shape), lambda i, nd=arr.ndim: (0,) * nd, shadow)


def _tb(arr, t, width, colblk):
    return A(arr, (t, width), lambda i: (i, colblk))


def _strip(arr, seq, colblk0, stride=1):
    return A(arr, (seq, LANES), lambda j: (0, colblk0 + stride * j))


def _s5_prep_args(p):
    lam_re = p["s5_lambda_re"].reshape(1, S5_NS)
    lam_im = p["s5_lambda_im"].reshape(1, S5_NS)
    log_step = jnp.pad(p["s5_log_step"].reshape(1, S5_GROUPS), ((0, 0), (0, LANES - S5_GROUPS)))
    b_lanes = lambda b: jnp.transpose(b, (2, 0, 1)).reshape(S5_P, S5_NS)
    return [_param(v) for v in (lam_re, lam_im, log_step, b_lanes(p["s5_b_re"]), b_lanes(p["s5_b_im"]))]


def _s5_c_args(p):
    c_lanes = lambda c: jnp.transpose(c, (1, 0, 2)).reshape(S5_P, S5_NS)
    return [_param(c_lanes(p["s5_c_re"])), _param(c_lanes(p["s5_c_im"]))]


def layer_fwd(x, p):
    seq = x.shape[0]
    nt = seq // CHUNK
    t2 = 256
    sv = {}
    t4 = min(512, seq)
    hb = block_fwd("rms_in", f_rms, seq // t4, [_tb(x, t4, D_MODEL, 0), _param(p["norm_w"].reshape(1, D_MODEL))],
                   [O((seq, D_MODEL), bf16, (t4, D_MODEL), lambda i: (i, 0))])[0]
    proj = mm("mm_in", NN, [(Op(hb), Op(p["w_in"]), D_MODEL, 1)], seq, PW, 1024, 1152)
    whole = lambda shape, dt: O(shape, dt, shape, lambda i: (0, 0))
    ab_re, ab_im, bb_re, bb_im = block_fwd("s5_prep", f_s5_prep, 1, _s5_prep_args(p),
                                           [whole((1, S5_NS), f32)] * 2 + [whole((BW, S5_NS), bf16)] * 2)
    cc_re, cc_im = block_fwd("s5_c", f_s5_c, 1, _s5_c_args(p), [whole((S5_NS, BW), bf16)] * 2)
    bu_re = mm("mm_bu_re", NN, [(Op(proj, 0, C_S5U), Op(bb_re), BW, 1)], seq, S5_NS, 512, 1024)
    bu_im = mm("mm_bu_im", NN, [(Op(proj, 0, C_S5U), Op(bb_im), BW, 1)], seq, S5_NS, 512, 1024)
    s_re, s_im = s5_scan_fwd(bu_re, bu_im, ab_re, ab_im)
    y_lin = mm("mm_s5y", NN, [(Op(s_re), Op(cc_re), S5_NS, 1), (Op(s_im), Op(cc_im), S5_NS, -1)], seq, BW, 256, BW)
    s5_act_args = [_tb(y_lin, t4, BW, 0), _tb(proj, t4, BW, C_S5U // BW), _tb(proj, t4, BW, C_S5G // BW),
                   _param(p["s5_d"].reshape(1, BW)), _param(p["s5_w_glu"], True)]
    out_bw =O((seq, BW), f32, (CHUNK, BW), lambda i: (i, 0))
    y_a = block_fwd("s5_act", f_s5_act, seq // t4, s5_act_args, [O((seq, BW), f32, (t4, BW), lambda i: (i, 0))])[0]
    y_b = block_fwd("sgu", f_sgu, nt, _sgu_args(proj, p), [out_bw])[0]
    xc = block_fwd("m2_conv", f_m2_conv, 8, _m2_conv_args(proj, p, seq),
                   [O((seq, 2 * BW), f32, (seq, LANES), lambda j: (0, j))])[0]
    y_c, st_saved = block_fwd("m2_ssd", f_m2, nt, _m2_args(proj, xc, p), [out_bw], carries=[(M2_STATE, BW)])
    y_d = block_fwd("sc", f_sc, 4, _sc_args(proj, p, seq), [O((seq, BW), f32, (seq, LANES), lambda j: (0, j))])[0]
    ys = [y_a, y_b, y_c, y_d]
    merged = block_fwd("branch_mix", f_branch_mix, seq // t2, _mix_args(ys, proj, p, t2),
                       [O((seq, D_MODEL), bf16, (t2, D_MODEL), lambda i: (i, 0))])[0]
    x_new = mm("mm_out", NN, [(Op(merged), Op(p["w_out"]), D_MODEL, 1)], seq, D_MODEL, 512, D_MODEL, add=Op(x))
    sv.update(x=x, hb=hb, proj=proj, ab=(ab_re, ab_im), bb=(bb_re, bb_im), cc=(cc_re, cc_im), s=(s_re, s_im), y_lin=y_lin,
              xc=xc, st=st_saved, ys=ys, merged=merged)
    return x_new, sv


def _sgu_args(proj, p):
    c0 = C_SGU // BW
    args = [_tb(proj, CHUNK, BW, c0), _tb(proj, CHUNK, BW, c0 + 1), _tb(proj, CHUNK, BW, c0 + 2),
            _param(p["sgu_ln_w"].reshape(1, BW)), _param(p["sgu_ln_b"].reshape(1, BW))]
    args += [A(p["sgu_w"], (None, CHUNK, CHUNK), lambda i, h=h: (h, 0, 0)) for h in range(8)]
    args.append(_param(jnp.pad(p["sgu_b"], ((0, LANES - 8), (0, 0)))))
    return args


def _m2_conv_args(proj, p, seq):
    return [_strip(proj, seq, C_XBC // LANES), A(p["m2_conv_w"], (4, LANES), lambda j: (0, j)),
            A(p["m2_conv_b"].reshape(1, 2 * BW), (1, LANES), lambda j: (0, j))]


def _pad_lanes(v):
    return jnp.pad(v.reshape(1, -1), ((0, 0), (0, LANES - v.size)))


def _m2_args(proj, xc, p):
    args = [_tb(proj, CHUNK, BW, C_M2Z // BW), _tb(xc, CHUNK, BW, 0)]
    args += [_tb(xc, CHUNK, LANES, 4 + k) for k in range(4)]
    args.append(_tb(proj, CHUNK, LANES, C_DT // LANES))
    args += [_param(_pad_lanes(p["m2_dt_bias"])), _param(_pad_lanes(p["m2_a_log"])), _param(_pad_lanes(p["m2_d"])),
             _param(p["m2_norm_w"].reshape(1, BW))]
    return args


def _sc_args(proj, p, seq):
    c0 = C_SC // LANES
    return [_strip(proj, seq, c0 + k, 4) for k in range(4)] + [A(p["sc_conv_w"], (3, LANES), lambda j: (0, j))]


def _mix_args(ys, proj, p, t):
    args = [_tb(ys[k], t, BW, 0) for k in range(N_BRANCH)]
    args += [_tb(proj, t, D_MODEL, k) for k in range(N_BRANCH)]
    args += [A(p["w_branch"], (BW, D_MODEL), lambda i, k=k: (k, 0), True) for k in range(N_BRANCH)]
    mb = p["merge_b"].reshape(N_BRANCH, 1, D_MODEL)
    args += [A(mb, (None, 1, D_MODEL), lambda i, k=k: (k, 0, 0)) for k in range(N_BRANCH)]
    return args


def layer_bwd(d_out, p, sv):
    seq = d_out.shape[0]
    nt = seq // CHUNK
    t4 = min(512, seq)
    proj, ys = sv["proj"], sv["ys"]
    g = {}
    acc = lambda shape: O(tuple(shape), f32, tuple(shape), lambda i, nd=len(shape): (0,) * nd)
    d_merged = mm("mm_out_dx", NT, [(Op(d_out), Op(p["w_out"]), D_MODEL, 1)], seq, D_MODEL, 512, D_MODEL, out_dtype=f32)
    g["w_out"] = mm("mm_out_dw", TN, [(Op(sv["merged"]), Op(d_out), seq, 1)], D_MODEL, D_MODEL, 256, D_MODEL, out_dtype=bf16)
    gouts = [O((seq, BW), f32, (CHUNK, BW), lambda i: (i, 0)) for _ in range(N_BRANCH)]
    gouts.append(O((seq, PW_MAIN), bf16, (CHUNK, N_BRANCH * D_MODEL), lambda i: (i, 0)))
    gouts += [acc((BW, D_MODEL)) for _ in range(N_BRANCH)] + [acc((1, D_MODEL)) for _ in range(N_BRANCH)]
    routes = [R(k, k) for k in range(N_BRANCH)] + [R(4 + k, 4, k * D_MODEL) for k in range(N_BRANCH)]
    routes += [R(8 + k, 5 + k, acc=True) for k in range(N_BRANCH)] + [R(12 + k, 9 + k, acc=True) for k in range(N_BRANCH)]
    res = block_bwd("branch_mix_bwd", f_branch_mix, nt, _mix_args(ys, proj, p, CHUNK),
                    [_tb(d_merged, CHUNK, D_MODEL, 0)], gouts, routes)
    dys, dproj = res[:4], res[4]
    g["w_branch"] = jnp.stack(res[5:9])
    g["merge_b"] = jnp.concatenate(res[9:13], axis=0)
    res = block_bwd("sc_bwd", f_sc, 4, _sc_args(proj, p, seq), [_strip(dys[3], seq, 0)],
                    [O((seq, PW_MAIN), bf16, (seq, 4 * LANES), lambda j: (0, C_SC // (4 * LANES) + j), alias=dproj),
                     O((3, BW), f32, (3, LANES), lambda j: (0, j))],
                    [R(k, 0, k * LANES) for k in range(4)] + [R(4, 1)])
    dproj, g["sc_conv_w"] = res
    res = block_bwd("m2_ssd_bwd", f_m2, nt, _m2_args(proj, sv["xc"], p), [_tb(dys[2], CHUNK, BW, 0)],
                    [O((seq, PW_MAIN), bf16, (CHUNK, BW), lambda i: (i, C_M2Z // BW), alias=dproj),
                     O((seq, 2 * BW), f32, (CHUNK, 2 * BW), lambda i: (i, 0)),
                     O((seq, LANES), bf16, (CHUNK, LANES), lambda i: (i, 0)),
                     acc((1, LANES)), acc((1, LANES)), acc((1, LANES)), acc((1, BW))],
                    [R(0, 0), R(1, 1, 0)] + [R(2 + k, 1, BW + k * LANES) for k in range(4)] + [R(6, 2)]
                    + [R(7, 3, acc=True), R(8, 4, acc=True), R(9, 5, acc=True), R(10, 6, acc=True)],
                    saved=[sv["st"]], rev=True)
    dproj, dxc, d_dt = res[0], res[1], res[2]
    g["m2_dt_bias"], g["m2_a_log"], g["m2_d"] = (r[0, :M2_HEADS] for r in res[3:6])
    g["m2_norm_w"] = res[6].reshape(BW)
    res = block_bwd("m2_conv_bwd", f_m2_conv, 8, _m2_conv_args(proj, p, seq), [_strip(dxc, seq, 0)],
                    [O((seq, PW_MAIN), bf16, (seq, LANES), lambda j: (0, C_XBC // LANES + j), alias=dproj),
                     O((4, 2 * BW), f32, (4, LANES), lambda j: (0, j)), O((1, 2 * BW), f32, (1, LANES), lambda j: (0, j))],
                    [R(0, 0), R(1, 1), R(2, 2)])
    dproj, g["m2_conv_w"], cb = res
    g["m2_conv_b"] = cb.reshape(2 * BW)
    res = block_bwd("sgu_bwd", f_sgu, nt, _sgu_args(proj, p), [_tb(dys[1], CHUNK, BW, 0)],
                    [O((seq, PW_MAIN), bf16, (CHUNK, 3 * BW), lambda i: (i, C_SGU // (3 * BW)), alias=dproj),
                     acc((1, BW)), acc((1, BW))] + [acc((CHUNK, CHUNK)) for _ in range(8)] + [acc((LANES, CHUNK))],
                    [R(0, 0, 0), R(1, 0, BW), R(2, 0, 2 * BW), R(3, 1, acc=True), R(4, 2, acc=True)]
                    + [R(5 + h, 3 + h, acc=True) for h in range(8)] + [R(13, 11, acc=True)])
    dproj = res[0]
    g["sgu_ln_w"], g["sgu_ln_b"] = res[1].reshape(BW), res[2].reshape(BW)
    g["sgu_w"] = jnp.stack(res[3:11])
    g["sgu_b"] = res[11][:8]
    y_lin, (s_re, s_im), (ab_re, ab_im) = sv["y_lin"], sv["s"], sv["ab"]
    s5_act_args = [_tb(y_lin, t4, BW, 0), _tb(proj, t4, BW, C_S5U // BW), _tb(proj, t4, BW, C_S5G // BW),
                   _param(p["s5_d"].reshape(1, BW)), _param(p["s5_w_glu"], True)]
    res = block_bwd("s5_act_bwd", f_s5_act, seq // t4, s5_act_args, [_tb(dys[0], t4, BW, 0)],
                    [O((seq, BW), bf16, (t4, BW), lambda i: (i, 0)), O((seq, BW), f32, (t4, BW), lambda i: (i, 0)),
                     O((seq, PW_MAIN), bf16, (t4, BW), lambda i: (i, C_S5G // BW), alias=dproj),
                     acc((1, BW)), acc((BW, BW))],
                    [R(0, 0), R(1, 1), R(2, 2), R(3, 3, acc=True), R(4, 4, acc=True)])
    dy_lin, du1, dproj = res[0], res[1], res[2]
    g["s5_d"] = res[3].reshape(S5_GROUPS, S5_P)
    g["s5_w_glu"] = res[4]
    (bb_re, bb_im), (cc_re, cc_im) = sv["bb"], sv["cc"]
    ds_re = mm("mm_s5y_dre", NT, [(Op(dy_lin), Op(cc_re), BW, 1)], seq, S5_NS, 512, 1024)
    ds_im = mm("mm_s5y_dim", NT, [(Op(dy_lin), Op(cc_im), BW, -1)], seq, S5_NS, 512, 1024)
    dc_re = mm("mm_s5y_dcre", TN, [(Op(s_re), Op(dy_lin), seq, 1)], S5_NS, BW, 512, BW)
    dc_im = mm("mm_s5y_dcim", TN, [(Op(s_im), Op(dy_lin), seq, -1)], S5_NS, BW, 512, BW)
    l_re, l_im, da_re, da_im = s5_scan_bwd(ds_re, ds_im, s_re, s_im, ab_re, ab_im)
    dproj = mm("mm_bu_dx", NT, [(Op(l_re), Op(bb_re), S5_NS, 1), (Op(l_im), Op(bb_im), S5_NS, 1)],
               seq, BW, 512, BW, add=Op(du1), out=dproj, out_col=C_S5U)
    dbb = [mm(f"mm_bu_dw{n}", TN, [(Op(proj, 0, C_S5U), Op(l), seq, 1)], BW, S5_NS, 256, 1024)
           for n, l in (("re", l_re), ("im", l_im))]
    gouts = [acc((1, S5_NS)), acc((1, S5_NS)), acc((1, LANES))] + [acc((S5_P, S5_NS))] * 2
    res = block_bwd("s5_prep_bwd", f_s5_prep, 1, _s5_prep_args(p),
                    [_param(v) for v in (da_re, da_im, dbb[0], dbb[1])], gouts, [R(k, k, acc=True) for k in range(5)])
    g["s5_lambda_re"], g["s5_lambda_im"] = res[0].reshape(S5_GROUPS, S5_N), res[1].reshape(S5_GROUPS, S5_N)
    g["s5_log_step"] = res[2][0, :S5_GROUPS]
    b_natural = lambda b: jnp.transpose(b.reshape(S5_P, S5_GROUPS, S5_N), (1, 2, 0))
    c_natural = lambda c: jnp.transpose(c.reshape(S5_P, S5_GROUPS, S5_N), (1, 0, 2))
    g["s5_b_re"], g["s5_b_im"] = b_natural(res[3]), b_natural(res[4])
    res = block_bwd("s5_c_bwd", f_s5_c, 1, _s5_c_args(p), [_param(dc_re), _param(dc_im)],
                    [acc((S5_P, S5_NS))] * 2, [R(0, 0, acc=True), R(1, 1, acc=True)])
    g["s5_c_re"], g["s5_c_im"] = c_natural(res[0]), c_natural(res[1])
    hb, w_in = sv["hb"], p["w_in"]
    dh = mm("mm_in_dx", NT, [(Op(dproj), Op(w_in), PW_MAIN, 1), (Op(d_dt), Op(w_in, 0, PW_MAIN), LANES, 1)],
            seq, D_MODEL, 256, 512)
    dw = mm("mm_in_dw", TN, [(Op(hb), Op(dproj), seq, 1)], D_MODEL, PW_MAIN, 512, 1024, out_dtype=bf16, out_width=PW)
    g["w_in"] = mm("mm_in_dwdt", TN, [(Op(hb), Op(d_dt), seq, 1)], D_MODEL, LANES, 256, LANES, out=dw, out_col=C_DT)
    dx, dnw = block_bwd("rms_in_bwd", f_rms_res, seq // t4, [_tb(sv["x"], t4, D_MODEL, 0), _param(p["norm_w"].reshape(1, D_MODEL))],
                        [_tb(dh, t4, D_MODEL, 0), _tb(d_out, t4, D_MODEL, 0)],
                        [O((seq, D_MODEL), f32, (t4, D_MODEL), lambda i: (i, 0)), acc((1, D_MODEL))],
                        [R(0, 0), R(1, 1, acc=True)])
    g["norm_w"] = dnw.reshape(D_MODEL)
    return dx, g


def loss_head(x, w, target):
    seq = x.shape[0]
    t = 256

    def body(x_ref, w_ref, t_ref, loss_ref, dx_ref, dw_ref):
        step = pl.program_id(0)

        def f(xv, wv):
            err = _rms(xv, wv) - t_ref[...]
            return 0.5 * jnp.sum(jnp.mean(err * err, axis=-1, keepdims=True), axis=0, keepdims=True)

        val, vjp_fn = jax.vjp(f, x_ref[...], w_ref[...])
        dx, dw = vjp_fn(jnp.ones((1, 1), f32))
        dx_ref[...] = dx

        @pl.when(step == 0)
        def _():
            loss_ref[...] = jnp.broadcast_to(val, loss_ref.shape)
            dw_ref[...] = dw

        @pl.when(step > 0)
        def _():
            loss_ref[...] += jnp.broadcast_to(val, loss_ref.shape)
            dw_ref[...] += dw

    blk = pl.BlockSpec((t, D_MODEL), lambda i: (i, 0))
    row = pl.BlockSpec((1, D_MODEL), lambda i: (0, 0))
    return pl.pallas_call(
        body, name="loss_head", grid=(seq // t,),
        in_specs=[blk, row, blk],
        out_specs=[pl.BlockSpec((1, LANES), lambda i: (0, 0)), blk, row],
        out_shape=[jax.ShapeDtypeStruct((1, LANES), f32), jax.ShapeDtypeStruct((seq, D_MODEL), f32),
                   jax.ShapeDtypeStruct((1, D_MODEL), f32)],
        compiler_params=_cparams(1),
    )(x, w.reshape(1, D_MODEL), target)


LAYER_KEYS = ("norm_w", "w_in", "s5_lambda_re", "s5_lambda_im", "s5_b_re", "s5_b_im", "s5_c_re", "s5_c_im", "s5_d",
              "s5_log_step", "s5_w_glu", "sgu_ln_w", "sgu_ln_b", "sgu_w", "sgu_b", "m2_conv_w", "m2_conv_b",
              "m2_dt_bias", "m2_a_log", "m2_d", "m2_norm_w", "sc_conv_w", "merge_b", "w_branch", "w_out")


def local_step(x, target, layers, final_norm_w):
    saved = []
    for p in layers:
        x, sv = layer_fwd(x, p)
        saved.append(sv)
    loss, dx, dfw = loss_head(x, final_norm_w, target)
    grads = []
    for p, sv in zip(reversed(layers), reversed(saved)):
        dx, g = layer_bwd(dx, p, sv)
        grads.append(g)
    return loss[0, 0], dx, grads[::-1], dfw.reshape(D_MODEL)


MESH = pl.DeviceIdType.MESH
ANY = pl.BlockSpec(memory_space=pl.ANY)


def _me():
    return lax.axis_index("x"), lax.axis_index("y"), lax.axis_index("c")


def _other_chips(x, y):
    return [(1 - x, y), (x, 1 - y), (1 - x, 1 - y)]


def _rcopy(src, dst, send, recv, dev):
    return pltpu.make_async_remote_copy(src_ref=src, dst_ref=dst, send_sem=send, recv_sem=recv,
                                        device_id=dev, device_id_type=MESH)


def _route_cut(rows, dtype):
    tile = 2 * SUBL * (4 // jnp.dtype(dtype).itemsize)
    return rows // 2 if rows % tile == 0 else rows


def _comm_call(name, body, arrs, out_shape, n_remote, aliases=None):
    n = len(arrs)
    return pl.pallas_call(
        body, name=name, in_specs=[ANY] * n, out_specs=[ANY] * len(out_shape), out_shape=out_shape,
        scratch_shapes=[pltpu.SemaphoreType.DMA((n, n_remote)), pltpu.SemaphoreType.DMA((n, n_remote))],
        input_output_aliases=aliases or {},
        compiler_params=pltpu.CompilerParams(has_side_effects=True),
    )(*arrs)


def gather_chips(name, arrs):
    n = len(arrs)
    cut = [_route_cut(a.shape[1], a.dtype) for a in arrs]

    def body(*refs):
        ins, outs = refs[:n], refs[n:2 * n]
        send, recv = refs[2 * n:]
        x, y, c = _me()
        jme, jx, jy, jd = 2 * x + y, 2 * (1 - x) + y, 2 * x + 1 - y, 2 * (1 - x) + 1 - y
        to_x, to_y, sib = (1 - x, y, c), (x, 1 - y, c), (x, y, 1 - c)

        def part(ref, a, hi):
            return ref.at[pl.ds(cut[a], ref.shape[0] - cut[a])] if hi else ref.at[pl.ds(0, cut[a])]

        def cp(a, k, ref, dev):
            return _rcopy(ref, ref, send.at[a, k], recv.at[a, k], dev)

        split = [a for a in range(n) if cut[a] < arrs[a].shape[1]]
        sent = [_rcopy(ins[a].at[c], outs[a].at[jme, c], send.at[a, k], recv.at[a, k], dev)
                for a in range(n) for k, dev in ((0, to_x), (1, to_y))]
        for s in sent:
            s.start()
        for a in range(n):
            blk = outs[a].at[jx, c]
            cp(a, 0, blk, to_x).wait_recv()
            sent += [cp(a, 2, part(blk, a, False), to_y), cp(a, 4, blk, sib)]
            sent[-2].start()
            sent[-1].start()
        for a in range(n):
            blk = outs[a].at[jy, c]
            cp(a, 1, blk, to_y).wait_recv()
            sent.append(cp(a, 5, blk, sib))
            sent[-1].start()
            if a in split:
                sent.append(cp(a, 3, part(blk, a, True), to_x))
                sent[-1].start()
        for a in range(n):
            lo = part(outs[a].at[jd, c], a, False)
            cp(a, 2, lo, to_y).wait_recv()
            sent.append(cp(a, 6, lo, sib))
            sent[-1].start()
        for a in split:
            hi = part(outs[a].at[jd, c], a, True)
            cp(a, 3, hi, to_x).wait_recv()
            sent.append(cp(a, 7, hi, sib))
            sent[-1].start()
        for a in range(n):
            cp(a, 4, outs[a].at[jx, 1 - c], sib).wait_recv()
            cp(a, 5, outs[a].at[jy, 1 - c], sib).wait_recv()
            cp(a, 6, part(outs[a].at[jd, 1 - c], a, False), sib).wait_recv()
        for a in split:
            cp(a, 7, part(outs[a].at[jd, 1 - c], a, True), sib).wait_recv()
        for s in sent:
            s.wait_send()

    out_shape = [jax.ShapeDtypeStruct((4,) + a.shape, a.dtype) for a in arrs]
    got = _comm_call(name, body, arrs, out_shape, 8)
    jme = 2 * lax.axis_index("x") + lax.axis_index("y")
    return [lax.dynamic_update_index_in_dim(g, a, jme, 0) for g, a in zip(got, arrs)]


def swap_halves(name, arrs):
    n = len(arrs)

    def body(*refs):
        ins, outs = refs[:n], refs[n:2 * n]
        send, recv = refs[2 * n:]
        x, y, c = _me()
        remote = [_rcopy(ins[a].at[1 - c], outs[a], send.at[a, 0], recv.at[a, 0], (x, y, 1 - c)) for a in range(n)]
        for cp in remote:
            cp.start()
        for cp in remote:
            cp.wait()

    return _comm_call(name, body, arrs, [jax.ShapeDtypeStruct(a.shape[1:], a.dtype) for a in arrs], 1)


def exchange_chips(name, arrs):
    n = len(arrs)

    def body(*refs):
        ins, outs = refs[:n], refs[n:2 * n]
        send, recv = refs[2 * n:]
        x, y, c = _me()
        remote = [_rcopy(ins[a].at[2 * cx + cy], outs[a].at[k], send.at[a, k], recv.at[a, k], (cx, cy, c))
                  for a in range(n) for k, (cx, cy) in enumerate(_other_chips(x, y))]
        for cp in remote:
            cp.start()
        for cp in remote:
            cp.wait()

    return _comm_call(name, body, arrs, [jax.ShapeDtypeStruct((3,) + a.shape[1:], a.dtype) for a in arrs], 3)


def gather_cores(name, arrs):
    n = len(arrs)

    def body(*refs):
        bufs = refs[n:2 * n]
        send, recv = refs[2 * n:]
        x, y, c = _me()
        remote = [_rcopy(bufs[a].at[c], bufs[a].at[c], send.at[a, 0], recv.at[a, 0], (x, y, 1 - c)) for a in range(n)]
        for cp in remote:
            cp.start()
        for a in range(n):
            _rcopy(bufs[a].at[1 - c], bufs[a].at[1 - c], send.at[a, 0], recv.at[a, 0], (x, y, 1 - c)).wait_recv()
        for cp in remote:
            cp.wait_send()

    return _comm_call(name, body, arrs, [jax.ShapeDtypeStruct(a.shape, a.dtype) for a in arrs], 1,
                      aliases={a: a for a in range(n)})


ROW_BLOCK = 512


def esum(name, terms, rows, width, out_dtype, out_slots=None):
    tr = next((t for t in range(min(rows, ROW_BLOCK), 0, -SUBL) if rows % t == 0 and t % SUBL == 0), rows)
    where =jnp.stack([lax.axis_index("c"), 2 * lax.axis_index("x") + lax.axis_index("y")]).astype(jnp.int32)
    pick = {"c": 0, "j": 1}

    def body(s_ref, *refs):
        acc = refs[0][...].astype(f32)
        for r in refs[1:-1]:
            acc = acc + r[...].astype(f32)
        refs[-1][...] = acc.astype(out_dtype)

    specs = []
    for arr, lead in terms:
        if lead is None:
            specs.append(pl.BlockSpec((tr, width), lambda i, s: (i, 0)))
        elif isinstance(lead, str):
            specs.append(pl.BlockSpec((None, tr, width), lambda i, s, lead=lead: (s[pick[lead]], i, 0)))
        else:
            specs.append(pl.BlockSpec((None, tr, width), lambda i, s, lead=lead: (lead, i, 0)))
    if out_slots is None:
        out_spec = pl.BlockSpec((tr, width), lambda i, s: (i, 0))
        out_shape = jax.ShapeDtypeStruct((rows, width), out_dtype)
    else:
        out_spec = pl.BlockSpec((None, tr, width), lambda i, s: (s[0], i, 0))
        out_shape = jax.ShapeDtypeStruct((out_slots, rows, width), out_dtype)
    return pl.pallas_call(
        body, name=name,
        grid_spec=pltpu.PrefetchScalarGridSpec(num_scalar_prefetch=1, grid=(rows // tr,), in_specs=specs, out_specs=out_spec),
        out_shape=out_shape, compiler_params=_cparams(1),
    )(where, *[t[0] for t in terms])


def reduce_to_shards(parts):
    tags = [str(k) for k in range(len(parts))]
    theirs = swap_halves("rs_swap", parts)
    t1 = []
    for tag, p, th in zip(tags, parts, theirs):
        _, _, h, w = p.shape
        t1.append(esum("rs_add_cores" + tag, [(p.reshape(2, 4 * h, w), "c"), (th.reshape(4 * h, w), None)],
                       4 * h, w, p.dtype).reshape(4, h, w))
    landed = exchange_chips("rs_exchange", t1)
    red = []
    for tag, p, t, got in zip(tags, parts, t1, landed):
        _, _, h, w = p.shape
        red.append(esum("rs_add_chips" + tag, [(t, "j"), (got, 0), (got, 1), (got, 2)], h, w, f32, out_slots=2))
    return gather_cores("rs_gather", red)


def _adamw_step(w_ref, g_ref, m_ref, v_ref, d_ref, nm_ref, nv_ref):
    gv = g_ref[...]
    nm = ADAM_B1 * m_ref[...] + (1.0 - ADAM_B1) * gv
    nv = ADAM_B2 * v_ref[...] + (1.0 - ADAM_B2) * jnp.square(gv)
    m_hat = nm / (1.0 - ADAM_B1 ** ADAM_STEP)
    v_hat = nv / (1.0 - ADAM_B2 ** ADAM_STEP)
    d_ref[...] = -ADAM_LR * (m_hat / (jnp.sqrt(v_hat) + ADAM_EPS) + ADAM_WD * w_ref[...])
    nm_ref[...] = nm
    nv_ref[...] = nv


def adamw(name, w, g, m, v, tr=None):
    rows, rest = w.shape[0], w.shape[1:]
    if tr is None:
        tr = ROW_BLOCK if rows % ROW_BLOCK == 0 else rows
    assert rows % tr == 0

    def body(*refs):
        _adamw_step(*refs)

    spec = pl.BlockSpec((tr,) + rest, lambda i: (i,) + (0,) * len(rest))
    return pl.pallas_call(
        body, name=name, grid=(rows // tr,), in_specs=[spec] * 4, out_specs=[spec] * 3,
        out_shape=[jax.ShapeDtypeStruct(w.shape, f32)] * 3, compiler_params=_cparams(1),
    )(w, g, m, v)


def adamw_many(name, ws, gs, ms, vs):
    n = len(ws)

    def body(*refs):
        ins, outs = refs[:4 * n], refs[4 * n:]
        for k in range(n):
            _adamw_step(ins[k], ins[n + k], ins[2 * n + k], ins[3 * n + k], outs[k], outs[n + k], outs[2 * n + k])

    vmem = pl.BlockSpec(memory_space=pltpu.VMEM)
    res = pl.pallas_call(
        body, name=name, in_specs=[vmem] * (4 * n), out_specs=[vmem] * (3 * n),
        out_shape=[jax.ShapeDtypeStruct(w.shape, f32) for w in ws] * 3,
        compiler_params=pltpu.CompilerParams(vmem_limit_bytes=VMEM_LIMIT),
    )(*ws, *gs, *ms, *vs)
    return res[:n], res[n:2 * n], res[2 * n:]


PACK_W = 1024


def _pack(parts, halves, row_mult):
    flat = jnp.concatenate([p.reshape(-1) for p in parts])
    per = halves * row_mult * PACK_W
    total = -(-flat.size // per) * per
    flat = jnp.pad(flat, (0, total - flat.size))
    return flat.reshape(halves, total // (halves * PACK_W), PACK_W)


def _unpack(flat, shapes):
    out, pos = [], 0
    for s in shapes:
        n = int(np.prod(s))
        out.append(flat[pos:pos + n].reshape(s))
        pos += n
    return out


SHARDED_BIG = ("w_in", "w_branch", "w_out", "s5_w_glu")
SHARDED_SMALL = ("m2_conv_w", "sc_conv_w", "merge_b")
SHARD_AXIS = {"w_in": 2, "w_branch": 3, "w_out": 1, "s5_w_glu": 1, "m2_conv_w": 2, "sc_conv_w": 2, "merge_b": 2}
REPLICATED = ("norm_w", "s5_lambda_re", "s5_lambda_im", "s5_b_re", "s5_b_im", "s5_c_re", "s5_c_im", "s5_d", "s5_log_step",
              "sgu_ln_w", "sgu_ln_b", "sgu_w", "sgu_b", "m2_conv_b", "m2_dt_bias", "m2_a_log", "m2_d", "m2_norm_w")
WEIGHTS = ("norm_w", "w_in", "s5_lambda_re", "s5_lambda_im", "s5_b_re", "s5_b_im", "s5_c_re", "s5_c_im", "s5_d",
           "s5_log_step", "s5_w_glu", "sgu_ln_w", "sgu_ln_b", "sgu_w", "sgu_b", "m2_conv_w", "m2_conv_b", "m2_dt_bias",
           "m2_a_log", "m2_d", "m2_norm_w", "sc_conv_w", "merge_b", "w_branch", "w_out", "final_norm_w")
N_LAYERS = 2


SHARD_W = IN_DIM // 4
SHARD_PAD = -(-SHARD_W // LANES) * LANES
REGROUP_W = 3 * LANES


def _kernel_pieces():
    out, pos = [], 0
    for s, n in _col_segments():
        while n:
            take = min(n, SHARD_W - s % SHARD_W)
            out.append((pos, s, take))
            pos, s, n = pos + take, s + take, n - take
    return out


def regroup_cols(name, src, steps, out_shape, out=None):
    n_src, rows, width = src.shape
    ow, win = REGROUP_W, REGROUP_W + LANES
    k_max = max(len(p) for _, _, p in steps)
    assert width % LANES == 0

    def body(*refs):
        src_ref, out_ref = refs[0], refs[-5]
        wbuf, obuf, sem_in, sem_out = refs[-4:]

        def fetch(q, slot):
            started = []
            for p, (s, col, lo, hi) in enumerate(steps[q][2]):
                w0 = col // LANES * LANES
                wlen = min(win, width - w0)
                cp = pltpu.make_async_copy(src_ref.at[s, :, pl.ds(w0, wlen)], wbuf.at[slot, p, :, pl.ds(0, wlen)], sem_in.at[slot, p])
                cp.start()
                started.append((cp, wlen, col - w0 - lo, lo, hi))
            return started

        pend, writes = fetch(0, 0), [None, None]
        for q, (t, b, _) in enumerate(steps):
            slot = q % 2
            nxt = fetch(q + 1, 1 - slot) if q + 1 < len(steps) else []
            acc = [jnp.zeros((rows, LANES), f32) for _ in range(ow // LANES)]
            for p, (cp, wlen, shift, lo, hi) in enumerate(pend):
                cp.wait()
                for k in range(ow // LANES):
                    c_lo, c_hi = max(lo, k * LANES), min(hi, (k + 1) * LANES)
                    if c_lo >= c_hi:
                        continue
                    wb = (c_lo + shift) // LANES
                    alen = min(2 * LANES, wlen - wb * LANES)
                    a = wbuf[slot, p, :, wb * LANES:wb * LANES + alen]
                    r, c = _rows((alen, LANES)), _cols((alen, LANES))
                    sh = shift + (k - wb) * LANES
                    sel = (r == c + sh) & (c >= c_lo - k * LANES) & (c < c_hi - k * LANES)
                    acc[k] = acc[k] + jnp.dot(a, sel.astype(bf16), preferred_element_type=f32)
            if writes[slot] is not None:
                writes[slot].wait()
            for k in range(ow // LANES):
                obuf[slot, :, k * LANES:(k + 1) * LANES] = acc[k].astype(bf16)
            writes[slot] = pltpu.make_async_copy(obuf.at[slot], out_ref.at[t, :, pl.ds(b * ow, ow)], sem_out.at[slot])
            writes[slot].start()
            pend = nxt
        for wr in writes:
            if wr is not None:
                wr.wait()

    operands, io_alias = [src], {}
    if out is not None:
        operands.append(out)
        io_alias = {1: 0}
    return pl.pallas_call(
        body, name=name, in_specs=[ANY] * len(operands), out_specs=ANY,
        out_shape=jax.ShapeDtypeStruct(out_shape, bf16), input_output_aliases=io_alias,
        scratch_shapes=[pltpu.VMEM((2, k_max, rows, win), bf16), pltpu.VMEM((2, rows, ow), bf16),
                        pltpu.SemaphoreType.DMA((2, k_max)), pltpu.SemaphoreType.DMA((2,))],
        compiler_params=pltpu.CompilerParams(vmem_limit_bytes=VMEM_LIMIT),
    )(*operands)


def _steps_to_kernel_cols(layer):
    steps = []
    for o in range(PW // REGROUP_W):
        pieces = []
        for pos, s, n in _kernel_pieces():
            lo, hi = max(pos, o * REGROUP_W), min(pos + n, (o + 1) * REGROUP_W)
            if lo < hi:
                ref_col = s + lo - pos
                pieces.append(((ref_col // SHARD_W) * N_LAYERS + layer, ref_col % SHARD_W, lo - o * REGROUP_W, hi - o * REGROUP_W))
        steps.append((0, o, pieces))
    return steps


def _steps_to_shards(layer):
    steps = []
    for j in range(4):
        for b in range(SHARD_PAD // REGROUP_W):
            start, stop = j * SHARD_W + b * REGROUP_W, min(j * SHARD_W + (b + 1) * REGROUP_W, (j + 1) * SHARD_W)
            pieces = []
            for pos, s, n in _kernel_pieces():
                lo, hi = max(s, start), min(s + n, stop)
                if lo < hi:
                    pieces.append((0, pos + lo - s, lo - start, hi - start))
            steps.append((layer * 4 + j, b, pieces))
    return steps


def _gather_weights(w):
    w_in = jnp.pad(w["w_in"].astype(bf16), ((0, 0), (0, 0), (0, SHARD_PAD - SHARD_W)))
    arrs = [w_in, w["w_branch"].reshape(N_LAYERS, N_BRANCH * BW, -1).astype(bf16),
            w["w_out"].astype(bf16), w["s5_w_glu"].astype(bf16), w["m2_conv_w"], w["sc_conv_w"], w["merge_b"]]
    got = gather_chips("ag_weights", arrs)
    cols = lambda t: jnp.transpose(t, (1, 0, 2)).reshape(t.shape[1], -1)
    layers = []
    for i in range(N_LAYERS):
        p = {k: w[k][i] for k in REPLICATED}
        p["w_in"] = regroup_cols(f"w_in_cols{i}", got[0].reshape(4 * N_LAYERS, D_MODEL, SHARD_PAD), _steps_to_kernel_cols(i),
                                 (1, D_MODEL, PW))[0]
        p["w_branch"] = cols(got[1][:, i])
        p["w_out"] = got[2][:, i].reshape(D_MODEL, D_MODEL)
        p["s5_w_glu"] = got[3][:, i].reshape(BW, BW)
        p["m2_conv_w"], p["sc_conv_w"], p["merge_b"] = cols(got[4][:, i]), cols(got[5][:, i]), cols(got[6][:, i])
        layers.append(p)
    return layers


def _reduce_grads(grads, d_final, w):
    to_chips = lambda t: jnp.transpose(t.reshape(t.shape[0], 4, -1), (1, 0, 2))
    stack = lambda f: jnp.stack([f(g) for g in grads])
    dw_in = None
    for i, g in enumerate(grads):
        dw_in = regroup_cols(f"w_in_shards{i}", g["w_in"][None], _steps_to_shards(i), (4 * N_LAYERS, D_MODEL, SHARD_PAD), out=dw_in)
    parts = [dw_in.reshape(N_LAYERS, 4, D_MODEL, SHARD_PAD),
             stack(lambda g: to_chips(g["w_branch"].reshape(N_BRANCH * BW, D_MODEL)).astype(bf16)),
             stack(lambda g: g["w_out"].reshape(4, D_MODEL // 4, D_MODEL).astype(bf16)),
             stack(lambda g: g["s5_w_glu"].reshape(4, BW // 4, BW).astype(bf16))]
    rep = jnp.concatenate([stack(lambda g: g[k]).reshape(-1) for k in REPLICATED] + [d_final.reshape(-1)])
    quarter = -(-rep.size // (4 * 2 * SUBL * PACK_W)) * (2 * SUBL * PACK_W)
    rep = jnp.pad(rep, (0, 4 * quarter - rep.size))
    small = []
    for j in range(4):
        sharded = [stack(lambda g: to_chips(g[k])[j]) for k in SHARDED_SMALL]
        small.append(_pack(sharded + [rep[j * quarter:(j + 1) * quarter]], 2, SUBL))
    parts.append(jnp.stack(small, axis=1))
    red = reduce_to_shards(parts)
    out = {"w_in": red[0][:, :, :SHARD_W], "w_branch": red[1].reshape(w["w_branch"].shape), "w_out": red[2], "s5_w_glu": red[3]}
    small_flat = red[4].reshape(-1)
    n_small = sum(int(np.prod(w[k].shape)) for k in SHARDED_SMALL)
    out.update(zip(SHARDED_SMALL, _unpack(small_flat, [w[k].shape for k in SHARDED_SMALL])))
    mine = small_flat[n_small:n_small + quarter].reshape(2, quarter // (2 * PACK_W), PACK_W)
    rep_all = gather_chips("ag_small_grads", [mine])[0].reshape(-1)
    names = REPLICATED + ("final_norm_w",)
    out.update(zip(names, _unpack(rep_all, [w[k].shape for k in names])))
    return out


def _update(w, g, m, v):
    d, nm, nv = {}, {}, {}
    flat2 = lambda a: a.reshape(-1, a.shape[-1])
    cols_major = lambda a: jnp.transpose(a, (2, 0, 1))
    res = adamw("adamw_w_in", *[cols_major(t["w_in"]) for t in (w, g, m, v)], tr=SHARD_W // 14)
    d["w_in"], nm["w_in"], nv["w_in"] = (jnp.transpose(r, (1, 2, 0)) for r in res)
    for k in SHARDED_BIG[1:]:
        res = adamw("adamw_" + k, *[flat2(t[k]) for t in (w, g, m, v)])
        d[k], nm[k], nv[k] = (r.reshape(w[k].shape) for r in res)
    for k in ("s5_b_re", "s5_b_im"):
        res = adamw("adamw_" + k, *[flat2(t[k]) for t in (w, g, m, v)])
        d[k], nm[k], nv[k] = (r.reshape(w[k].shape) for r in res)
    rest = [k for k in WEIGHTS if k not in SHARDED_BIG + ("s5_b_re", "s5_b_im")]
    two_d = lambda a: a.reshape(1, -1) if a.ndim == 1 else a
    res = adamw_many("adamw_rest", *[[two_d(t[k]) for k in rest] for t in (w, g, m, v)])
    for tgt, rs in zip((d, nm, nv), res):
        tgt.update({k: r.reshape(w[k].shape) for k, r in zip(rest, rs)})
    return d, nm, nv


def kernel(x, norm_w, w_in, s5_lambda_re, s5_lambda_im, s5_b_re, s5_b_im, s5_c_re, s5_c_im, s5_d, s5_log_step, s5_w_glu, sgu_ln_w, sgu_ln_b, sgu_w, sgu_b, m2_conv_w, m2_conv_b, m2_dt_bias, m2_a_log, m2_d, m2_norm_w, sc_conv_w, merge_b, w_branch, w_out, final_norm_w, loss_target, m_norm_w, m_w_in, m_s5_lambda_re, m_s5_lambda_im, m_s5_b_re, m_s5_b_im, m_s5_c_re, m_s5_c_im, m_s5_d, m_s5_log_step, m_s5_w_glu, m_sgu_ln_w, m_sgu_ln_b, m_sgu_w, m_sgu_b, m_m2_conv_w, m_m2_conv_b, m_m2_dt_bias, m_m2_a_log, m_m2_d, m_m2_norm_w, m_sc_conv_w, m_merge_b, m_w_branch, m_w_out, m_final_norm_w, v_norm_w, v_w_in, v_s5_lambda_re, v_s5_lambda_im, v_s5_b_re, v_s5_b_im, v_s5_c_re, v_s5_c_im, v_s5_d, v_s5_log_step, v_s5_w_glu, v_sgu_ln_w, v_sgu_ln_b, v_sgu_w, v_sgu_b, v_m2_conv_w, v_m2_conv_b, v_m2_dt_bias, v_m2_a_log, v_m2_d, v_m2_norm_w, v_sc_conv_w, v_merge_b, v_w_branch, v_w_out, v_final_norm_w):
    given = dict(locals())
    w = {k: given[k] for k in WEIGHTS}
    m = {k: given["m_" + k] for k in WEIGHTS}
    v = {k: given["v_" + k] for k in WEIGHTS}
    layers = _gather_weights(w)
    loss, dx, grads, d_final = local_step(x[0], loss_target[0], layers, final_norm_w)
    loss = lax.psum(loss, ("x", "y", "c"))
    g = _reduce_grads(grads, d_final, w)
    d, nm, nv = _update(w, g, m, v)
    return (loss, dx[None], *[g[k] for k in WEIGHTS], *[d[k] for k in WEIGHTS],
            *[nm[k] for k in WEIGHTS], *[nv[k] for k in WEIGHTS])
```

```python
import functools
from typing import Any, Callable, NamedTuple

import numpy as np
import jax
import jax.numpy as jnp
from jax import lax
from jax.experimental import pallas as pl
from jax.experimental.pallas import tpu as pltpu

f32 = jnp.float32
bf16 = jnp.bfloat16

D_MODEL = 1024
BW = 512
N_BRANCH = 4
EPS = 1e-6
S5_GROUPS, S5_P, S5_N = 32, 16, 64
S5_NS = S5_GROUPS * S5_N
CHUNK = 128
M2_HEADS, M2_HEAD_DIM, M2_GROUPS, M2_STATE = 8, 64, 2, 128
IN_DIM = 10248
PW = 10368
PW_MAIN = 10240
LANES = 128
VMEM_LIMIT = 60 * 1024 * 1024

ADAM_LR, ADAM_B1, ADAM_B2, ADAM_EPS, ADAM_WD, ADAM_STEP = 0.001, 0.9, 0.999, 1e-08, 0.01, 10

C_MERGE = 0
C_SC = 4096
C_SGU = 6144
C_S5G = 7680
C_S5U = 8192
C_M2Z = 8704
C_XBC = 9216
C_DT = 10240


def _col_segments():
    segs = [(6152, 4096)]
    for j in range(4):
        segs += [(4104 + 128 * j, 128), (4616 + 128 * j, 128), (5128 + 128 * j, 128), (5640 + 128 * j, 128)]
    segs += [(1024, 1536), (512, 512), (0, 512), (2560, 512), (3072, 1024), (4096, 8)]
    return segs


NN = ((1,), (0,))
NT = ((1,), (1,))
TN = ((0,), (0,))


def _bd(a, b, dims):
    return lax.dot_general(a.astype(bf16), b.astype(bf16), (dims, ((), ())), preferred_element_type=f32)


def _hd(a, b, dims):
    return lax.dot_general(a, b, (dims, ((), ())), precision=lax.Precision.HIGHEST, preferred_element_type=f32)


def _make_dots(raw):
    @jax.custom_vjp
    def nn(a, b):
        return raw(a, b, NN)
    nn.defvjp(lambda a, b: (raw(a, b, NN), (a, b)), lambda r, g: (raw(g, r[1], NT), raw(r[0], g, TN)))

    @jax.custom_vjp
    def nt(a, b):
        return raw(a, b, NT)
    nt.defvjp(lambda a, b: (raw(a, b, NT), (a, b)), lambda r, g: (raw(g, r[1], NN), raw(g, r[0], TN)))

    @jax.custom_vjp
    def tn(a, b):
        return raw(a, b, TN)
    tn.defvjp(lambda a, b: (raw(a, b, TN), (a, b)), lambda r, g: (raw(r[1], g, NT), raw(r[0], g, NN)))
    return nn, nt, tn


bdot, bdot_nt, bdot_tn = _make_dots(_bd)
hdot, hdot_nt, hdot_tn = _make_dots(_hd)


@jax.custom_vjp
def bdot_w(a, w, shadow):
    return _bd(a, w, NN)


bdot_w.defvjp(lambda a, w, s: (_bd(a, w, NN), (a, w)),
              lambda r, g: (_bd(g, r[1], NT), jnp.zeros_like(r[1]), _bd(r[0], g, TN)))


def _rows(shape):
    return lax.broadcasted_iota(jnp.int32, shape, 0)


def _cols(shape):
    return lax.broadcasted_iota(jnp.int32, shape, 1)


def _shift_down(x, s):
    return jnp.where(_rows(x.shape) < s, 0.0, pltpu.roll(x, s, 0))


def _shift_up(x, s):
    n = x.shape[0]
    return jnp.where(_rows(x.shape) >= n - s, 0.0, pltpu.roll(x, n - s, 0))


@functools.partial(jax.custom_vjp, nondiff_argnums=(1,))
def shift(x, s):
    return _shift_down(x, s) if s else x


shift.defvjp(lambda x, s: (shift(x, s), None), lambda s, _, g: (_shift_up(g, s) if s else g,))


def _row_of(w, k):
    return jnp.sum(jnp.where(_rows(w.shape) == k, w, 0.0), axis=0, keepdims=True)


def _lane_mask(width, lo, hi):
    c = _cols((1, width))
    return ((c >= lo) & (c < hi)).astype(f32)


def _expand(rows, width, per):
    return (_cols((rows, width)) // per == _rows((rows, width))).astype(f32)


class A(NamedTuple):
    arr: Any
    block: tuple
    imap: Callable
    shadow: bool = False


class O(NamedTuple):
    shape: tuple
    dtype: Any
    block: tuple
    imap: Callable
    alias: Any = None


class R(NamedTuple):
    arg: int
    out: int
    off: Any = None
    acc: bool = False


def _cparams(n_grid):
    return pltpu.CompilerParams(dimension_semantics=("arbitrary",) * n_grid, vmem_limit_bytes=VMEM_LIMIT)


def _ispec(block, imap, n, rev):
    if rev:
        return pl.BlockSpec(block, lambda i: imap(n - 1 - i))
    return pl.BlockSpec(block, imap)


def _load(ref, a):
    v = ref[...]
    if a.shadow:
        return (v, jnp.zeros(v.shape, f32))
    return v.astype(f32)


def _save_spec(shape, n, rev):
    nd = len(shape)
    return _ispec((None,) + tuple(shape), lambda i: (i,) + (0,) * nd, n, rev)


def block_fwd(name, f, n, args, outs, carries=()):
    n_in, n_out, n_c = len(args), len(outs), len(carries)

    def body(*refs):
        ins, out_r = refs[:n_in], refs[n_in:n_in + n_out]
        saves, cs = refs[n_in + n_out:n_in + n_out + n_c], refs[n_in + n_out + n_c:]
        if n_c:
            @pl.when(pl.program_id(0) == 0)
            def _():
                for c in cs:
                    c[...] = jnp.zeros(c.shape, f32)
        vals = [_load(r, a) for r, a in zip(ins, args)]
        cv = [c[...] for c in cs]
        for s, v in zip(saves, cv):
            s[...] = v
        res = f(*vals, *cv)
        for r, v in zip(out_r, res[:n_out]):
            r[...] = v.astype(r.dtype)
        for c, v in zip(cs, res[n_out:]):
            c[...] = v

    out_shape = [jax.ShapeDtypeStruct(o.shape, o.dtype) for o in outs]
    out_specs = [pl.BlockSpec(o.block, o.imap) for o in outs]
    for shp in carries:
        out_shape.append(jax.ShapeDtypeStruct((n,) + tuple(shp), f32))
        out_specs.append(_save_spec(shp, n, False))
    return pl.pallas_call(
        body, name=name, grid=(n,),
        in_specs=[pl.BlockSpec(a.block, a.imap) for a in args],
        out_specs=out_specs, out_shape=out_shape,
        scratch_shapes=[pltpu.VMEM(tuple(shp), f32) for shp in carries],
        compiler_params=_cparams(1),
    )(*[a.arr for a in args])


def block_bwd(name, f, n, args, cots, gouts, routes, saved=(), rev=False):
    n_in, n_cot, n_c, n_go = len(args), len(cots), len(saved), len(gouts)
    diff = []
    for r in routes:
        if r.arg not in diff:
            diff.append(r.arg)
    aliases = [(k, o.alias) for k, o in enumerate(gouts) if o.alias is not None]

    def body(*refs):
        ins = refs[:n_in]
        cot_r = refs[n_in:n_in + n_cot]
        sav_r = refs[n_in + n_cot:n_in + n_cot + n_c]
        base = n_in + n_cot + n_c + len(aliases)
        go_r = refs[base:base + n_go]
        dcs = refs[base + n_go:]
        step = pl.program_id(0)
        if n_c:
            @pl.when(step == 0)
            def _():
                for d in dcs:
                    d[...] = jnp.zeros(d.shape, f32)
        vals = [_load(r, a) for r, a in zip(ins, args)]
        cv = [s[...] for s in sav_r]
        nd = len(diff)

        def g(*dv):
            full = list(vals)
            for idx, v in zip(diff, dv[:nd]):
                full[idx] = (vals[idx][0], v) if args[idx].shadow else v
            return tuple(f(*full, *dv[nd:]))

        primals = [vals[i][1] if args[i].shadow else vals[i] for i in diff] + cv
        _, vjp_fn = jax.vjp(g, *primals)
        ct = tuple([r[...].astype(f32) for r in cot_r] + [d[...] for d in dcs])
        grads = vjp_fn(ct)
        for r in routes:
            gr = grads[diff.index(r.arg)]
            ref = go_r[r.out]
            if r.acc:
                @pl.when(step == 0)
                def _(ref=ref, gr=gr):
                    ref[...] = gr.astype(ref.dtype)

                @pl.when(step > 0)
                def _(ref=ref, gr=gr):
                    ref[...] += gr.astype(ref.dtype)
            elif r.off is None:
                ref[...] = gr.astype(ref.dtype)
            else:
                ref[:, r.off:r.off + gr.shape[1]] = gr.astype(ref.dtype)
        for d, gr in zip(dcs, grads[nd:]):
            d[...] = gr

    in_specs = [_ispec(a.block, a.imap, n, rev) for a in list(args) + list(cots)]
    in_specs += [_save_spec(s.shape[1:], n, rev) for s in saved]
    in_specs += [pl.BlockSpec(memory_space=pl.ANY) for _ in aliases]
    operands = [a.arr for a in list(args) + list(cots)] + list(saved) + [arr for _, arr in aliases]
    io_alias = {n_in + n_cot + n_c + j: k for j, (k, _) in enumerate(aliases)}
    return pl.pallas_call(
        body, name=name, grid=(n,),
        in_specs=in_specs,
        out_specs=[_ispec(o.block, o.imap, n, rev) for o in gouts],
        out_shape=[jax.ShapeDtypeStruct(o.shape, o.dtype) for o in gouts],
        scratch_shapes=[pltpu.VMEM(tuple(s.shape[1:]), f32) for s in saved],
        input_output_aliases=io_alias,
        compiler_params=_cparams(1),
    )(*operands)


class Op(NamedTuple):
    arr: Any
    row: int = 0
    col: int = 0


def mm(name, mode, pairs, m, n, tm, tn, out_dtype=f32, add=None, out=None, out_col=0, out_width=None):
    tm, tn = min(tm, m), min(tn, n)
    assert m % tm == 0 and n % tn == 0
    in_specs, operands = [], []
    for a, b, k, _ in pairs:
        if mode == TN:
            assert a.row % k == 0 and a.col % tm == 0
            in_specs.append(pl.BlockSpec((k, tm), lambda j, i, a=a, k=k: (a.row // k, i + a.col // tm)))
        else:
            assert a.col % k == 0 and a.row % tm == 0
            in_specs.append(pl.BlockSpec((tm, k), lambda j, i, a=a, k=k: (i + a.row // tm, a.col // k)))
        if mode == NT:
            assert b.col % k == 0 and b.row % tn == 0
            in_specs.append(pl.BlockSpec((tn, k), lambda j, i, b=b, k=k: (j + b.row // tn, b.col // k)))
        else:
            assert b.row % k == 0 and b.col % tn == 0
            in_specs.append(pl.BlockSpec((k, tn), lambda j, i, b=b, k=k: (b.row // k, j + b.col // tn)))
        operands += [a.arr, b.arr]
    n_p = len(pairs)
    if add is not None:
        assert add.col % tn == 0
        in_specs.append(pl.BlockSpec((tm, tn), lambda j, i: (i, j + add.col // tn)))
        operands.append(add.arr)
    io_alias = {}
    if out is not None:
        assert out_col % tn == 0
        in_specs.append(pl.BlockSpec(memory_space=pl.ANY))
        operands.append(out)
        io_alias = {len(operands) - 1: 0}
        out_shape = jax.ShapeDtypeStruct(out.shape, out.dtype)
    else:
        out_shape = jax.ShapeDtypeStruct((m, out_width or n), out_dtype)
    signs = [p[3] for p in pairs]

    def body(*refs):
        o = refs[-1]
        acc = None
        for p in range(n_p):
            t = _bd(refs[2 * p][...], refs[2 * p + 1][...], mode)
            t = t if signs[p] > 0 else -t
            acc = t if acc is None else acc + t
        if add is not None:
            acc = acc + refs[2 * n_p][...].astype(f32)
        o[...] = acc.astype(o.dtype)

    return pl.pallas_call(
        body, name=name, grid=(n // tn, m // tm),
        in_specs=in_specs,
        out_specs=pl.BlockSpec((tm, tn), lambda j, i: (i, j + out_col // tn)),
        out_shape=out_shape, input_output_aliases=io_alias,
        compiler_params=_cparams(2),
    )(*operands)


SCAN_LANES = 512
SUBL = 8


def _cmul(p, q):
    return (p[0] * q[0] - p[1] * q[1], p[0] * q[1] + p[1] * q[0])


def _powers(a):
    a2 = _cmul(a, a)
    a4 = _cmul(a2, a2)
    a6 = _cmul(a4, a2)
    return [a, a2, _cmul(a2, a), a4, _cmul(a4, a), a6, _cmul(a6, a), _cmul(a4, a4)]


def _table(pw, order, w):
    row = _rows((SUBL, w))
    re = sum(jnp.where(row == t, pw[k][0], 0.0) for t, k in enumerate(order))
    im = sum(jnp.where(row == t, pw[k][1], 0.0) for t, k in enumerate(order))
    return re, im


def s5_scan_fwd(bu_re, bu_im, a_re, a_im):
    seq, ns = bu_re.shape
    w, nb = SCAN_LANES, ns // SCAN_LANES

    def body(b_re, b_im, ar, ai, s_re, s_im):
        a = (ar[...], ai[...])
        pw = _powers(a)
        tab = _table(pw, list(range(SUBL)), w)
        row = _rows((SUBL, w))

        def step(i, carry):
            t0 = pl.multiple_of(i * SUBL, SUBL)
            x = (b_re[pl.ds(t0, SUBL), :], b_im[pl.ds(t0, SUBL), :])
            for d, k in ((1, 0), (2, 1), (4, 3)):
                sh = (jnp.where(row < d, 0.0, pltpu.roll(x[0], d, 0)), jnp.where(row < d, 0.0, pltpu.roll(x[1], d, 0)))
                t = _cmul(pw[k], sh)
                x = (x[0] + t[0], x[1] + t[1])
            t = _cmul(tab, carry)
            x = (x[0] + t[0], x[1] + t[1])
            s_re[pl.ds(t0, SUBL), :] = x[0]
            s_im[pl.ds(t0, SUBL), :] = x[1]
            return (x[0][SUBL - 1:, :], x[1][SUBL - 1:, :])

        z = jnp.zeros((1, w), f32)
        lax.fori_loop(0, seq // SUBL, step, (z, z), unroll=2)

    strip = pl.BlockSpec((seq, w), lambda j: (0, j))
    lane = pl.BlockSpec((1, w), lambda j: (0, j))
    return pl.pallas_call(
        body, name="s5_scan_fwd", grid=(nb,),
        in_specs=[strip, strip, lane, lane],
        out_specs=[strip, strip],
        out_shape=[jax.ShapeDtypeStruct((seq, ns), f32)] * 2,
        compiler_params=_cparams(1),
    )(bu_re, bu_im, a_re, a_im)


def s5_scan_bwd(ds_re, ds_im, s_re, s_im, a_re, a_im):
    seq, ns = ds_re.shape
    w, nb = SCAN_LANES, ns // SCAN_LANES
    nblk = seq // SUBL

    def body(g_re, g_im, sr, si, ar, ai, l_re, l_im, da_re, da_im):
        a = (ar[...], -ai[...])
        pw = _powers(a)
        tab = _table(pw, [SUBL - 1 - t for t in range(SUBL)], w)
        row = _rows((SUBL, w))

        def step(kk, carry):
            c_re, c_im, acc_re, acc_im = carry
            i = nblk - 1 - kk
            t0 = pl.multiple_of(i * SUBL, SUBL)
            x = (g_re[pl.ds(t0, SUBL), :], g_im[pl.ds(t0, SUBL), :])
            for d, k in ((1, 0), (2, 1), (4, 3)):
                sh = (jnp.where(row >= SUBL - d, 0.0, pltpu.roll(x[0], SUBL - d, 0)),
                      jnp.where(row >= SUBL - d, 0.0, pltpu.roll(x[1], SUBL - d, 0)))
                t = _cmul(pw[k], sh)
                x = (x[0] + t[0], x[1] + t[1])
            t = _cmul(tab, (c_re, c_im))
            x = (x[0] + t[0], x[1] + t[1])
            l_re[pl.ds(t0, SUBL), :] = x[0]
            l_im[pl.ds(t0, SUBL), :] = x[1]
            tp = jnp.maximum(t0 - 1, 0)
            live = (i > 0).astype(f32)
            p_re = sr[pl.ds(tp, 1), :] * live
            p_im = si[pl.ds(tp, 1), :] * live
            sp_re = jnp.where(row == 0, p_re, pltpu.roll(sr[pl.ds(t0, SUBL), :], 1, 0))
            sp_im = jnp.where(row == 0, p_im, pltpu.roll(si[pl.ds(t0, SUBL), :], 1, 0))
            acc_re = acc_re + x[0] * sp_re + x[1] * sp_im
            acc_im = acc_im + x[1] * sp_re - x[0] * sp_im
            return (x[0][:1, :], x[1][:1, :], acc_re, acc_im)

        z1 = jnp.zeros((1, w), f32)
        z8 = jnp.zeros((SUBL, w), f32)
        _, _, acc_re, acc_im = lax.fori_loop(0, nblk, step, (z1, z1, z8, z8), unroll=2)
        da_re[...] = jnp.sum(acc_re, axis=0, keepdims=True)
        da_im[...] = jnp.sum(acc_im, axis=0, keepdims=True)

    strip = pl.BlockSpec((seq, w), lambda j: (0, j))
    lane = pl.BlockSpec((1, w), lambda j: (0, j))
    return pl.pallas_call(
        body, name="s5_scan_bwd", grid=(nb,),
        in_specs=[strip, strip, strip, strip, lane, lane],
        out_specs=[strip, strip, lane, lane],
        out_shape=[jax.ShapeDtypeStruct((seq, ns), f32)] * 2 + [jax.ShapeDtypeStruct((1, ns), f32)] * 2,
        compiler_params=_cparams(1),
    )(ds_re, ds_im, s_re, s_im, a_re, a_im)


def _rms(x, w):
    return x * lax.rsqrt(jnp.mean(x * x, axis=-1, keepdims=True) + EPS) * w


def f_rms(x, w):
    return (_rms(x, w),)


def f_rms_res(x, w):
    return (_rms(x, w), x)


def f_s5_prep(lam_re, lam_im, log_step, b_re, b_im):
    e = _expand(log_step.shape[1], S5_NS, S5_N)
    step = hdot(jnp.exp(log_step), e)
    mag = jnp.exp(lam_re * step)
    ab_re, ab_im = mag * jnp.cos(lam_im * step), mag * jnp.sin(lam_im * step)
    den = lam_re * lam_re + lam_im * lam_im
    nr = ab_re - 1.0
    coef_re = (nr * lam_re + ab_im * lam_im) / den
    coef_im = (ab_im * lam_re - nr * lam_im) / den
    bb_re, bb_im = coef_re * b_re - coef_im * b_im, coef_re * b_im + coef_im * b_re
    sel = (_rows((BW, S5_P)) % S5_P == _cols((BW, S5_P))).astype(f32)
    blk = _rows((BW, S5_NS)) // S5_P == _cols((BW, S5_NS)) // S5_N
    rows_bd = lambda t: jnp.where(blk, hdot(sel, t), 0.0)
    return (ab_re, ab_im, rows_bd(bb_re), rows_bd(bb_im))


def f_s5_c(c_re, c_im):
    sel_t = (_cols((S5_P, BW)) % S5_P == _rows((S5_P, BW))).astype(f32)
    blk_t = _rows((S5_NS, BW)) // S5_N == _cols((S5_NS, BW)) // S5_P
    cols_bd = lambda t: jnp.where(blk_t, hdot_tn(t, sel_t), 0.0)
    return (cols_bd(c_re), cols_bd(c_im))


def f_s5_act(y_lin, u, gate, d, w_glu):
    y = jax.nn.gelu(y_lin + d * u)
    y = y * jax.nn.sigmoid(bdot_w(y, *w_glu))
    return (y * jax.nn.silu(gate),)


def f_sgu(u, v, gate, ln_w, ln_b, *rest):
    w_s, b_pad = rest[:8], rest[8]
    t = u.shape[0]
    u32, v32 = jax.nn.gelu(u), jax.nn.gelu(v)
    mu = jnp.mean(v32, axis=-1, keepdims=True)
    var = jnp.mean(jnp.square(v32 - mu), axis=-1, keepdims=True)
    vn = (v32 - mu) * lax.rsqrt(var + EPS) * ln_w + ln_b
    tri = _rows((t, t)) >= _cols((t, t))
    s = hdot_tn(b_pad, _expand(LANES, BW, BW // 8))
    for h in range(8):
        s = s + bdot(jnp.where(tri, w_s[h], 0.0), vn) * _lane_mask(BW, 64 * h, 64 * h + 64)
    return (u32 * s * jax.nn.silu(gate),)


def f_m2_conv(x, w, b):
    return (sum(_row_of(w, k) * shift(x, 3 - k) for k in range(4)) + b,)


def f_sc(bg, cg, h, gate, w):
    z = cg * h
    conv = sum(_row_of(w, k) * shift(z, 2 - k) for k in range(3))
    return (bg * conv * jax.nn.silu(gate),)


def f_m2(z, xc, b0, b1, c0, c1, dt_raw, dt_bias, a_log, d_par, norm_w, st):
    q = z.shape[0]
    x = jax.nn.silu(xc)
    bm, cm = (jax.nn.silu(b0), jax.nn.silu(b1)), (jax.nn.silu(c0), jax.nn.silu(c1))
    dt = jax.nn.softplus(dt_raw + dt_bias)
    da = dt * (-jnp.exp(a_log))
    tri = _rows((q, q)) >= _cols((q, q))
    acs = hdot(tri.astype(f32), da)
    e = _expand(LANES, BW, M2_HEAD_DIM)
    dt_f, acs_f = hdot(dt, e), hdot(acs, e)
    last = _rows((q, BW)) == q - 1
    alast_f = jnp.sum(jnp.where(last, acs_f, 0.0), axis=0, keepdims=True)
    xdt = x * dt_f
    xdec = xdt * jnp.exp(alast_f - acs_f)
    acs_t = acs.T
    st_new = st * jnp.exp(alast_f)
    y_diag, y_off = 0.0, 0.0
    for g in range(M2_GROUPS):
        gm = _lane_mask(BW, 256 * g, 256 * g + 256)
        cb = bdot_nt(cm[g], bm[g])
        st_new = st_new + bdot_tn(bm[g], xdec * gm)
        y_off = y_off + bdot(cm[g], st) * gm
        for hh in range(M2_HEADS // M2_GROUPS):
            h = g * (M2_HEADS // M2_GROUPS) + hh
            col = jnp.sum(jnp.where(_cols((q, LANES)) == h, acs, 0.0), axis=1, keepdims=True)
            row = jnp.sum(jnp.where(_rows((LANES, q)) == h, acs_t, 0.0), axis=0, keepdims=True)
            decay = jnp.exp(jnp.where(tri, col - row, -1e30))
            y_diag = y_diag + bdot(cb * decay, xdt) * _lane_mask(BW, 64 * h, 64 * h + 64)
    d_f = sum(jnp.sum(jnp.where(_cols((1, LANES)) == h, d_par, 0.0), axis=1, keepdims=True)
              * _lane_mask(BW, 64 * h, 64 * h + 64) for h in range(M2_HEADS))
    y = y_diag + y_off * jnp.exp(acs_f) + d_f * x
    y = y * jax.nn.silu(z)
    return (_rms(y, norm_w), st_new)


def f_branch_mix(*v):
    nb = len(v) // 4
    ys, lg, wb, mb = v[0:nb], v[nb:2 * nb], v[2 * nb:3 * nb], v[3 * nb:]
    return (sum(jax.nn.sigmoid(lg[k] + mb[k]) * bdot_w(ys[k], *wb[k]) for k in range(nb)),)


def _param(arr, shadow=False):
    return A(arr, tuple(arr.shape), lambda i, nd=arr.ndim: (0,) * nd, shadow)


def _tb(arr, t, width, colblk):
    return A(arr, (t, width), lambda i: (i, colblk))


def _strip(arr, seq, colblk0, stride=1):
    return A(arr, (seq, LANES), lambda j: (0, colblk0 + stride * j))


def _s5_prep_args(p):
    lam_re = p["s5_lambda_re"].reshape(1, S5_NS)
    lam_im = p["s5_lambda_im"].reshape(1, S5_NS)
    log_step = jnp.pad(p["s5_log_step"].reshape(1, S5_GROUPS), ((0, 0), (0, LANES - S5_GROUPS)))
    b_lanes = lambda b: jnp.transpose(b, (2, 0, 1)).reshape(S5_P, S5_NS)
    return [_param(v) for v in (lam_re, lam_im, log_step, b_lanes(p["s5_b_re"]), b_lanes(p["s5_b_im"]))]


def _s5_c_args(p):
    c_lanes = lambda c: jnp.transpose(c, (1, 0, 2)).reshape(S5_P, S5_NS)
    return [_param(c_lanes(p["s5_c_re"])), _param(c_lanes(p["s5_c_im"]))]


def layer_fwd(x, p):
    seq = x.shape[0]
    nt = seq // CHUNK
    t2 = 256
    sv = {}
    t4 = min(512, seq)
    hb = block_fwd("rms_in", f_rms, seq // t4, [_tb(x, t4, D_MODEL, 0), _param(p["norm_w"].reshape(1, D_MODEL))],
                   [O((seq, D_MODEL), bf16, (t4, D_MODEL), lambda i: (i, 0))])[0]
    proj = mm("mm_in", NN, [(Op(hb), Op(p["w_in"]), D_MODEL, 1)], seq, PW, 1024, 1152)
    whole = lambda shape, dt: O(shape, dt, shape, lambda i: (0, 0))
    ab_re, ab_im, bb_re, bb_im = block_fwd("s5_prep", f_s5_prep, 1, _s5_prep_args(p),
                                           [whole((1, S5_NS), f32)] * 2 + [whole((BW, S5_NS), bf16)] * 2)
    cc_re, cc_im = block_fwd("s5_c", f_s5_c, 1, _s5_c_args(p), [whole((S5_NS, BW), bf16)] * 2)
    bu_re = mm("mm_bu_re", NN, [(Op(proj, 0, C_S5U), Op(bb_re), BW, 1)], seq, S5_NS, 512, 1024)
    bu_im = mm("mm_bu_im", NN, [(Op(proj, 0, C_S5U), Op(bb_im), BW, 1)], seq, S5_NS, 512, 1024)
    s_re, s_im = s5_scan_fwd(bu_re, bu_im, ab_re, ab_im)
    y_lin = mm("mm_s5y", NN, [(Op(s_re), Op(cc_re), S5_NS, 1), (Op(s_im), Op(cc_im), S5_NS, -1)], seq, BW, 256, BW)
    s5_act_args = [_tb(y_lin, t4, BW, 0), _tb(proj, t4, BW, C_S5U // BW), _tb(proj, t4, BW, C_S5G // BW),
                   _param(p["s5_d"].reshape(1, BW)), _param(p["s5_w_glu"], True)]
    out_bw =O((seq, BW), f32, (CHUNK, BW), lambda i: (i, 0))
    y_a = block_fwd("s5_act", f_s5_act, seq // t4, s5_act_args, [O((seq, BW), f32, (t4, BW), lambda i: (i, 0))])[0]
    y_b = block_fwd("sgu", f_sgu, nt, _sgu_args(proj, p), [out_bw])[0]
    xc = block_fwd("m2_conv", f_m2_conv, 8, _m2_conv_args(proj, p, seq),
                   [O((seq, 2 * BW), f32, (seq, LANES), lambda j: (0, j))])[0]
    y_c, st_saved = block_fwd("m2_ssd", f_m2, nt, _m2_args(proj, xc, p), [out_bw], carries=[(M2_STATE, BW)])
    y_d = block_fwd("sc", f_sc, 4, _sc_args(proj, p, seq), [O((seq, BW), f32, (seq, LANES), lambda j: (0, j))])[0]
    ys = [y_a, y_b, y_c, y_d]
    merged = block_fwd("branch_mix", f_branch_mix, seq // t2, _mix_args(ys, proj, p, t2),
                       [O((seq, D_MODEL), bf16, (t2, D_MODEL), lambda i: (i, 0))])[0]
    x_new = mm("mm_out", NN, [(Op(merged), Op(p["w_out"]), D_MODEL, 1)], seq, D_MODEL, 512, D_MODEL, add=Op(x))
    sv.update(x=x, hb=hb, proj=proj, ab=(ab_re, ab_im), bb=(bb_re, bb_im), cc=(cc_re, cc_im), s=(s_re, s_im), y_lin=y_lin,
              xc=xc, st=st_saved, ys=ys, merged=merged)
    return x_new, sv


def _sgu_args(proj, p):
    c0 = C_SGU // BW
    args = [_tb(proj, CHUNK, BW, c0), _tb(proj, CHUNK, BW, c0 + 1), _tb(proj, CHUNK, BW, c0 + 2),
            _param(p["sgu_ln_w"].reshape(1, BW)), _param(p["sgu_ln_b"].reshape(1, BW))]
    args += [A(p["sgu_w"], (None, CHUNK, CHUNK), lambda i, h=h: (h, 0, 0)) for h in range(8)]
    args.append(_param(jnp.pad(p["sgu_b"], ((0, LANES - 8), (0, 0)))))
    return args


def _m2_conv_args(proj, p, seq):
    return [_strip(proj, seq, C_XBC // LANES), A(p["m2_conv_w"], (4, LANES), lambda j: (0, j)),
            A(p["m2_conv_b"].reshape(1, 2 * BW), (1, LANES), lambda j: (0, j))]


def _pad_lanes(v):
    return jnp.pad(v.reshape(1, -1), ((0, 0), (0, LANES - v.size)))


def _m2_args(proj, xc, p):
    args = [_tb(proj, CHUNK, BW, C_M2Z // BW), _tb(xc, CHUNK, BW, 0)]
    args += [_tb(xc, CHUNK, LANES, 4 + k) for k in range(4)]
    args.append(_tb(proj, CHUNK, LANES, C_DT // LANES))
    args += [_param(_pad_lanes(p["m2_dt_bias"])), _param(_pad_lanes(p["m2_a_log"])), _param(_pad_lanes(p["m2_d"])),
             _param(p["m2_norm_w"].reshape(1, BW))]
    return args


def _sc_args(proj, p, seq):
    c0 = C_SC // LANES
    return [_strip(proj, seq, c0 + k, 4) for k in range(4)] + [A(p["sc_conv_w"], (3, LANES), lambda j: (0, j))]


def _mix_args(ys, proj, p, t, ks=range(N_BRANCH)):
    args = [_tb(ys[k], t, BW, 0) for k in ks]
    args += [_tb(proj, t, D_MODEL, k) for k in ks]
    args += [A(p["w_branch"], (BW, D_MODEL), lambda i, k=k: (k, 0), True) for k in ks]
    mb = p["merge_b"].reshape(N_BRANCH, 1, D_MODEL)
    args += [A(mb, (None, 1, D_MODEL), lambda i, k=k: (k, 0, 0)) for k in ks]
    return args


def layer_bwd(d_out, p, sv):
    seq = d_out.shape[0]
    nt = seq // CHUNK
    t4 = min(512, seq)
    proj, ys = sv["proj"], sv["ys"]
    g = {}
    acc = lambda shape: O(tuple(shape), f32, tuple(shape), lambda i, nd=len(shape): (0,) * nd)
    d_merged = mm("mm_out_dx", NT, [(Op(d_out), Op(p["w_out"]), D_MODEL, 1)], seq, D_MODEL, 512, D_MODEL, out_dtype=f32)
    g["w_out"] = mm("mm_out_dw", TN, [(Op(sv["merged"]), Op(d_out), seq, 1)], D_MODEL, D_MODEL, 256, D_MODEL, out_dtype=bf16)
    dys, dwb, dmb, dproj, tb = [], [], [], None, min(256, seq)
    for half in range(2):
        gouts = [O((seq, BW), f32, (tb, BW), lambda i: (i, 0)) for _ in range(2)]
        gouts.append(O((seq, PW_MAIN), bf16, (tb, 2 * D_MODEL), lambda i, half=half: (i, half), alias=dproj))
        gouts += [acc((BW, D_MODEL)) for _ in range(2)] + [acc((1, D_MODEL)) for _ in range(2)]
        routes = [R(k, k) for k in range(2)] + [R(2 + k, 2, k * D_MODEL) for k in range(2)]
        routes += [R(4 + k, 3 + k, acc=True) for k in range(2)] + [R(6 + k, 5 + k, acc=True) for k in range(2)]
        res = block_bwd(f"branch_mix_bwd{half}", f_branch_mix, seq // tb, _mix_args(ys, proj, p, tb, (2 * half, 2 * half + 1)),
                        [_tb(d_merged, tb, D_MODEL, 0)], gouts, routes)
        dys, dproj, dwb, dmb = dys + list(res[:2]), res[2], dwb + list(res[3:5]), dmb + list(res[5:7])
    g["w_branch"] = jnp.stack(dwb)
    g["merge_b"] = jnp.concatenate(dmb, axis=0)
    res = block_bwd("sc_bwd", f_sc, 4, _sc_args(proj, p, seq), [_strip(dys[3], seq, 0)],
                    [O((seq, PW_MAIN), bf16, (seq, 4 * LANES), lambda j: (0, C_SC // (4 * LANES) + j), alias=dproj),
                     O((3, BW), f32, (3, LANES), lambda j: (0, j))],
                    [R(k, 0, k * LANES) for k in range(4)] + [R(4, 1)])
    dproj, g["sc_conv_w"] = res
    res = block_bwd("m2_ssd_bwd", f_m2, nt, _m2_args(proj, sv["xc"], p), [_tb(dys[2], CHUNK, BW, 0)],
                    [O((seq, PW_MAIN), bf16, (CHUNK, BW), lambda i: (i, C_M2Z // BW), alias=dproj),
                     O((seq, 2 * BW), f32, (CHUNK, 2 * BW), lambda i: (i, 0)),
                     O((seq, LANES), bf16, (CHUNK, LANES), lambda i: (i, 0)),
                     acc((1, LANES)), acc((1, LANES)), acc((1, LANES)), acc((1, BW))],
                    [R(0, 0), R(1, 1, 0)] + [R(2 + k, 1, BW + k * LANES) for k in range(4)] + [R(6, 2)]
                    + [R(7, 3, acc=True), R(8, 4, acc=True), R(9, 5, acc=True), R(10, 6, acc=True)],
                    saved=[sv["st"]], rev=True)
    dproj, dxc, d_dt = res[0], res[1], res[2]
    g["m2_dt_bias"], g["m2_a_log"], g["m2_d"] = (r[0, :M2_HEADS] for r in res[3:6])
    g["m2_norm_w"] = res[6].reshape(BW)
    res = block_bwd("m2_conv_bwd", f_m2_conv, 8, _m2_conv_args(proj, p, seq), [_strip(dxc, seq, 0)],
                    [O((seq, PW_MAIN), bf16, (seq, LANES), lambda j: (0, C_XBC // LANES + j), alias=dproj),
                     O((4, 2 * BW), f32, (4, LANES), lambda j: (0, j)), O((1, 2 * BW), f32, (1, LANES), lambda j: (0, j))],
                    [R(0, 0), R(1, 1), R(2, 2)])
    dproj, g["m2_conv_w"], cb = res
    g["m2_conv_b"] = cb.reshape(2 * BW)
    res = block_bwd("sgu_bwd", f_sgu, nt, _sgu_args(proj, p), [_tb(dys[1], CHUNK, BW, 0)],
                    [O((seq, PW_MAIN), bf16, (CHUNK, 3 * BW), lambda i: (i, C_SGU // (3 * BW)), alias=dproj),
                     acc((1, BW)), acc((1, BW))] + [acc((CHUNK, CHUNK)) for _ in range(8)] + [acc((LANES, CHUNK))],
                    [R(0, 0, 0), R(1, 0, BW), R(2, 0, 2 * BW), R(3, 1, acc=True), R(4, 2, acc=True)]
                    + [R(5 + h, 3 + h, acc=True) for h in range(8)] + [R(13, 11, acc=True)])
    dproj = res[0]
    g["sgu_ln_w"], g["sgu_ln_b"] = res[1].reshape(BW), res[2].reshape(BW)
    g["sgu_w"] = jnp.stack(res[3:11])
    g["sgu_b"] = res[11][:8]
    y_lin, (s_re, s_im), (ab_re, ab_im) = sv["y_lin"], sv["s"], sv["ab"]
    s5_act_args = [_tb(y_lin, t4, BW, 0), _tb(proj, t4, BW, C_S5U // BW), _tb(proj, t4, BW, C_S5G // BW),
                   _param(p["s5_d"].reshape(1, BW)), _param(p["s5_w_glu"], True)]
    res = block_bwd("s5_act_bwd", f_s5_act, seq // t4, s5_act_args, [_tb(dys[0], t4, BW, 0)],
                    [O((seq, BW), bf16, (t4, BW), lambda i: (i, 0)), O((seq, BW), f32, (t4, BW), lambda i: (i, 0)),
                     O((seq, PW_MAIN), bf16, (t4, BW), lambda i: (i, C_S5G // BW), alias=dproj),
                     acc((1, BW)), acc((BW, BW))],
                    [R(0, 0), R(1, 1), R(2, 2), R(3, 3, acc=True), R(4, 4, acc=True)])
    dy_lin, du1, dproj = res[0], res[1], res[2]
    g["s5_d"] = res[3].reshape(S5_GROUPS, S5_P)
    g["s5_w_glu"] = res[4]
    (bb_re, bb_im), (cc_re, cc_im) = sv["bb"], sv["cc"]
    ds_re = mm("mm_s5y_dre", NT, [(Op(dy_lin), Op(cc_re), BW, 1)], seq, S5_NS, 512, 1024)
    ds_im = mm("mm_s5y_dim", NT, [(Op(dy_lin), Op(cc_im), BW, -1)], seq, S5_NS, 512, 1024)
    dc_re = mm("mm_s5y_dcre", TN, [(Op(s_re), Op(dy_lin), seq, 1)], S5_NS, BW, 512, BW)
    dc_im = mm("mm_s5y_dcim", TN, [(Op(s_im), Op(dy_lin), seq, -1)], S5_NS, BW, 512, BW)
    l_re, l_im, da_re, da_im = s5_scan_bwd(ds_re, ds_im, s_re, s_im, ab_re, ab_im)
    dproj = mm("mm_bu_dx", NT, [(Op(l_re), Op(bb_re), S5_NS, 1), (Op(l_im), Op(bb_im), S5_NS, 1)],
               seq, BW, 512, BW, add=Op(du1), out=dproj, out_col=C_S5U)
    dbb = [mm(f"mm_bu_dw{n}", TN, [(Op(proj, 0, C_S5U), Op(l), seq, 1)], BW, S5_NS, 256, 1024)
           for n, l in (("re", l_re), ("im", l_im))]
    gouts = [acc((1, S5_NS)), acc((1, S5_NS)), acc((1, LANES))] + [acc((S5_P, S5_NS))] * 2
    res = block_bwd("s5_prep_bwd", f_s5_prep, 1, _s5_prep_args(p),
                    [_param(v) for v in (da_re, da_im, dbb[0], dbb[1])], gouts, [R(k, k, acc=True) for k in range(5)])
    g["s5_lambda_re"], g["s5_lambda_im"] = res[0].reshape(S5_GROUPS, S5_N), res[1].reshape(S5_GROUPS, S5_N)
    g["s5_log_step"] = res[2][0, :S5_GROUPS]
    b_natural = lambda b: jnp.transpose(b.reshape(S5_P, S5_GROUPS, S5_N), (1, 2, 0))
    c_natural = lambda c: jnp.transpose(c.reshape(S5_P, S5_GROUPS, S5_N), (1, 0, 2))
    g["s5_b_re"], g["s5_b_im"] = b_natural(res[3]), b_natural(res[4])
    res = block_bwd("s5_c_bwd", f_s5_c, 1, _s5_c_args(p), [_param(dc_re), _param(dc_im)],
                    [acc((S5_P, S5_NS))] * 2, [R(0, 0, acc=True), R(1, 1, acc=True)])
    g["s5_c_re"], g["s5_c_im"] = c_natural(res[0]), c_natural(res[1])
    hb, w_in = sv["hb"], p["w_in"]
    dh = mm("mm_in_dx", NT, [(Op(dproj), Op(w_in), PW_MAIN, 1), (Op(d_dt), Op(w_in, 0, PW_MAIN), LANES, 1)],
            seq, D_MODEL, 256, 512)
    dw = mm("mm_in_dw", TN, [(Op(hb), Op(dproj), seq, 1)], D_MODEL, PW_MAIN, 512, 1024, out_dtype=bf16, out_width=PW)
    g["w_in"] = mm("mm_in_dwdt", TN, [(Op(hb), Op(d_dt), seq, 1)], D_MODEL, LANES, 256, LANES, out=dw, out_col=C_DT)
    dx, dnw = block_bwd("rms_in_bwd", f_rms_res, seq // t4, [_tb(sv["x"], t4, D_MODEL, 0), _param(p["norm_w"].reshape(1, D_MODEL))],
                        [_tb(dh, t4, D_MODEL, 0), _tb(d_out, t4, D_MODEL, 0)],
                        [O((seq, D_MODEL), f32, (t4, D_MODEL), lambda i: (i, 0)), acc((1, D_MODEL))],
                        [R(0, 0), R(1, 1, acc=True)])
    g["norm_w"] = dnw.reshape(D_MODEL)
    return dx, g


def loss_head(x, w, target):
    seq = x.shape[0]
    t = 256

    def body(x_ref, w_ref, t_ref, loss_ref, dx_ref, dw_ref):
        step = pl.program_id(0)

        def f(xv, wv):
            err = _rms(xv, wv) - t_ref[...]
            return 0.5 * jnp.sum(jnp.mean(err * err, axis=-1, keepdims=True), axis=0, keepdims=True)

        val, vjp_fn = jax.vjp(f, x_ref[...], w_ref[...])
        dx, dw = vjp_fn(jnp.ones((1, 1), f32))
        dx_ref[...] = dx

        @pl.when(step == 0)
        def _():
            loss_ref[...] = jnp.broadcast_to(val, loss_ref.shape)
            dw_ref[...] = dw

        @pl.when(step > 0)
        def _():
            loss_ref[...] += jnp.broadcast_to(val, loss_ref.shape)
            dw_ref[...] += dw

    blk = pl.BlockSpec((t, D_MODEL), lambda i: (i, 0))
    row = pl.BlockSpec((1, D_MODEL), lambda i: (0, 0))
    return pl.pallas_call(
        body, name="loss_head", grid=(seq // t,),
        in_specs=[blk, row, blk],
        out_specs=[pl.BlockSpec((1, LANES), lambda i: (0, 0)), blk, row],
        out_shape=[jax.ShapeDtypeStruct((1, LANES), f32), jax.ShapeDtypeStruct((seq, D_MODEL), f32),
                   jax.ShapeDtypeStruct((1, D_MODEL), f32)],
        compiler_params=_cparams(1),
    )(x, w.reshape(1, D_MODEL), target)


LAYER_KEYS = ("norm_w", "w_in", "s5_lambda_re", "s5_lambda_im", "s5_b_re", "s5_b_im", "s5_c_re", "s5_c_im", "s5_d",
              "s5_log_step", "s5_w_glu", "sgu_ln_w", "sgu_ln_b", "sgu_w", "sgu_b", "m2_conv_w", "m2_conv_b",
              "m2_dt_bias", "m2_a_log", "m2_d", "m2_norm_w", "sc_conv_w", "merge_b", "w_branch", "w_out")


def local_step(x, target, layers, final_norm_w):
    saved = []
    for p in layers:
        x, sv = layer_fwd(x, p)
        saved.append(sv)
    loss, dx, dfw = loss_head(x, final_norm_w, target)
    grads = []
    for p, sv in zip(reversed(layers), reversed(saved)):
        dx, g = layer_bwd(dx, p, sv)
        grads.append(g)
    return loss[0, 0], dx, grads[::-1], dfw.reshape(D_MODEL)


MESH = pl.DeviceIdType.MESH
ANY = pl.BlockSpec(memory_space=pl.ANY)


def _me():
    return lax.axis_index("x"), lax.axis_index("y"), lax.axis_index("c")


def _other_chips(x, y):
    return [(1 - x, y), (x, 1 - y), (1 - x, 1 - y)]


def _rcopy(src, dst, send, recv, dev):
    return pltpu.make_async_remote_copy(src_ref=src, dst_ref=dst, send_sem=send, recv_sem=recv,
                                        device_id=dev, device_id_type=MESH)


def _route_cut(rows, dtype):
    tile = 2 * SUBL * (4 // jnp.dtype(dtype).itemsize)
    return rows // 2 if rows % tile == 0 else rows


def _comm_call(name, body, arrs, out_shape, n_remote, aliases=None):
    n = len(arrs)
    return pl.pallas_call(
        body, name=name, in_specs=[ANY] * n, out_specs=[ANY] * len(out_shape), out_shape=out_shape,
        scratch_shapes=[pltpu.SemaphoreType.DMA((n, n_remote)), pltpu.SemaphoreType.DMA((n, n_remote))],
        input_output_aliases=aliases or {},
        compiler_params=pltpu.CompilerParams(has_side_effects=True),
    )(*arrs)


def gather_chips(name, arrs):
    n = len(arrs)
    cut = [_route_cut(a.shape[1], a.dtype) for a in arrs]

    def body(*refs):
        ins, outs = refs[:n], refs[n:2 * n]
        send, recv = refs[2 * n:]
        x, y, c = _me()
        jme, jx, jy, jd = 2 * x + y, 2 * (1 - x) + y, 2 * x + 1 - y, 2 * (1 - x) + 1 - y
        to_x, to_y, sib = (1 - x, y, c), (x, 1 - y, c), (x, y, 1 - c)

        def part(ref, a, hi):
            return ref.at[pl.ds(cut[a], ref.shape[0] - cut[a])] if hi else ref.at[pl.ds(0, cut[a])]

        def cp(a, k, ref, dev):
            return _rcopy(ref, ref, send.at[a, k], recv.at[a, k], dev)

        split = [a for a in range(n) if cut[a] < arrs[a].shape[1]]
        sent = [_rcopy(ins[a].at[c], outs[a].at[jme, c], send.at[a, k], recv.at[a, k], dev)
                for a in range(n) for k, dev in ((0, to_x), (1, to_y))]
        for s in sent:
            s.start()
        for a in range(n):
            blk = outs[a].at[jx, c]
            cp(a, 0, blk, to_x).wait_recv()
            sent += [cp(a, 2, part(blk, a, False), to_y), cp(a, 4, blk, sib)]
            sent[-2].start()
            sent[-1].start()
        for a in range(n):
            blk = outs[a].at[jy, c]
            cp(a, 1, blk, to_y).wait_recv()
            sent.append(cp(a, 5, blk, sib))
            sent[-1].start()
            if a in split:
                sent.append(cp(a, 3, part(blk, a, True), to_x))
                sent[-1].start()
        for a in range(n):
            lo = part(outs[a].at[jd, c], a, False)
            cp(a, 2, lo, to_y).wait_recv()
            sent.append(cp(a, 6, lo, sib))
            sent[-1].start()
        for a in split:
            hi = part(outs[a].at[jd, c], a, True)
            cp(a, 3, hi, to_x).wait_recv()
            sent.append(cp(a, 7, hi, sib))
            sent[-1].start()
        for a in range(n):
            cp(a, 4, outs[a].at[jx, 1 - c], sib).wait_recv()
            cp(a, 5, outs[a].at[jy, 1 - c], sib).wait_recv()
            cp(a, 6, part(outs[a].at[jd, 1 - c], a, False), sib).wait_recv()
        for a in split:
            cp(a, 7, part(outs[a].at[jd, 1 - c], a, True), sib).wait_recv()
        for s in sent:
            s.wait_send()

    out_shape = [jax.ShapeDtypeStruct((4,) + a.shape, a.dtype) for a in arrs]
    got = _comm_call(name, body, arrs, out_shape, 8)
    jme = 2 * lax.axis_index("x") + lax.axis_index("y")
    return [lax.dynamic_update_index_in_dim(g, a, jme, 0) for g, a in zip(got, arrs)]


def swap_halves(name, arrs):
    n = len(arrs)

    def body(*refs):
        ins, outs = refs[:n], refs[n:2 * n]
        send, recv = refs[2 * n:]
        x, y, c = _me()
        remote = [_rcopy(ins[a].at[1 - c], outs[a], send.at[a, 0], recv.at[a, 0], (x, y, 1 - c)) for a in range(n)]
        for cp in remote:
            cp.start()
        for cp in remote:
            cp.wait()

    return _comm_call(name, body, arrs, [jax.ShapeDtypeStruct(a.shape[1:], a.dtype) for a in arrs], 1)


def exchange_chips(name, arrs):
    n = len(arrs)

    def body(*refs):
        ins, outs = refs[:n], refs[n:2 * n]
        send, recv = refs[2 * n:]
        x, y, c = _me()
        remote = [_rcopy(ins[a].at[2 * cx + cy], outs[a].at[k], send.at[a, k], recv.at[a, k], (cx, cy, c))
                  for a in range(n) for k, (cx, cy) in enumerate(_other_chips(x, y))]
        for cp in remote:
            cp.start()
        for cp in remote:
            cp.wait()

    return _comm_call(name, body, arrs, [jax.ShapeDtypeStruct((3,) + a.shape[1:], a.dtype) for a in arrs], 3)


def gather_cores(name, arrs):
    n = len(arrs)

    def body(*refs):
        bufs = refs[n:2 * n]
        send, recv = refs[2 * n:]
        x, y, c = _me()
        remote = [_rcopy(bufs[a].at[c], bufs[a].at[c], send.at[a, 0], recv.at[a, 0], (x, y, 1 - c)) for a in range(n)]
        for cp in remote:
            cp.start()
        for a in range(n):
            _rcopy(bufs[a].at[1 - c], bufs[a].at[1 - c], send.at[a, 0], recv.at[a, 0], (x, y, 1 - c)).wait_recv()
        for cp in remote:
            cp.wait_send()

    return _comm_call(name, body, arrs, [jax.ShapeDtypeStruct(a.shape, a.dtype) for a in arrs], 1,
                      aliases={a: a for a in range(n)})


ROW_BLOCK = 512


def esum(name, terms, rows, width, out_dtype, out_slots=None):
    tr = next((t for t in range(min(rows, ROW_BLOCK), 0, -SUBL) if rows % t == 0 and t % SUBL == 0), rows)
    where =jnp.stack([lax.axis_index("c"), 2 * lax.axis_index("x") + lax.axis_index("y")]).astype(jnp.int32)
    pick = {"c": 0, "j": 1}

    def body(s_ref, *refs):
        acc = refs[0][...].astype(f32)
        for r in refs[1:-1]:
            acc = acc + r[...].astype(f32)
        refs[-1][...] = acc.astype(out_dtype)

    specs = []
    for arr, lead in terms:
        if lead is None:
            specs.append(pl.BlockSpec((tr, width), lambda i, s: (i, 0)))
        elif isinstance(lead, str):
            specs.append(pl.BlockSpec((None, tr, width), lambda i, s, lead=lead: (s[pick[lead]], i, 0)))
        else:
            specs.append(pl.BlockSpec((None, tr, width), lambda i, s, lead=lead: (lead, i, 0)))
    if out_slots is None:
        out_spec = pl.BlockSpec((tr, width), lambda i, s: (i, 0))
        out_shape = jax.ShapeDtypeStruct((rows, width), out_dtype)
    else:
        out_spec = pl.BlockSpec((None, tr, width), lambda i, s: (s[0], i, 0))
        out_shape = jax.ShapeDtypeStruct((out_slots, rows, width), out_dtype)
    return pl.pallas_call(
        body, name=name,
        grid_spec=pltpu.PrefetchScalarGridSpec(num_scalar_prefetch=1, grid=(rows // tr,), in_specs=specs, out_specs=out_spec),
        out_shape=out_shape, compiler_params=_cparams(1),
    )(where, *[t[0] for t in terms])


def reduce_to_shards(parts):
    tags = [str(k) for k in range(len(parts))]
    theirs = swap_halves("rs_swap", parts)
    t1 = []
    for tag, p, th in zip(tags, parts, theirs):
        _, _, h, w = p.shape
        t1.append(esum("rs_add_cores" + tag, [(p.reshape(2, 4 * h, w), "c"), (th.reshape(4 * h, w), None)],
                       4 * h, w, p.dtype).reshape(4, h, w))
    landed = exchange_chips("rs_exchange", t1)
    red = []
    for tag, p, t, got in zip(tags, parts, t1, landed):
        _, _, h, w = p.shape
        red.append(esum("rs_add_chips" + tag, [(t, "j"), (got, 0), (got, 1), (got, 2)], h, w, f32, out_slots=2))
    return gather_cores("rs_gather", red)


def _adamw_step(w_ref, g_ref, m_ref, v_ref, d_ref, nm_ref, nv_ref):
    gv = g_ref[...]
    nm = ADAM_B1 * m_ref[...] + (1.0 - ADAM_B1) * gv
    nv = ADAM_B2 * v_ref[...] + (1.0 - ADAM_B2) * jnp.square(gv)
    m_hat = nm / (1.0 - ADAM_B1 ** ADAM_STEP)
    v_hat = nv / (1.0 - ADAM_B2 ** ADAM_STEP)
    d_ref[...] = -ADAM_LR * (m_hat / (jnp.sqrt(v_hat) + ADAM_EPS) + ADAM_WD * w_ref[...])
    nm_ref[...] = nm
    nv_ref[...] = nv


def adamw(name, w, g, m, v, tr=None):
    rows, rest = w.shape[0], w.shape[1:]
    if tr is None:
        tr = ROW_BLOCK if rows % ROW_BLOCK == 0 else rows
    assert rows % tr == 0

    def body(*refs):
        _adamw_step(*refs)

    spec = pl.BlockSpec((tr,) + rest, lambda i: (i,) + (0,) * len(rest))
    return pl.pallas_call(
        body, name=name, grid=(rows // tr,), in_specs=[spec] * 4, out_specs=[spec] * 3,
        out_shape=[jax.ShapeDtypeStruct(w.shape, f32)] * 3, compiler_params=_cparams(1),
    )(w, g, m, v)


def adamw_many(name, ws, gs, ms, vs):
    n = len(ws)

    def body(*refs):
        ins, outs = refs[:4 * n], refs[4 * n:]
        for k in range(n):
            _adamw_step(ins[k], ins[n + k], ins[2 * n + k], ins[3 * n + k], outs[k], outs[n + k], outs[2 * n + k])

    vmem = pl.BlockSpec(memory_space=pltpu.VMEM)
    res = pl.pallas_call(
        body, name=name, in_specs=[vmem] * (4 * n), out_specs=[vmem] * (3 * n),
        out_shape=[jax.ShapeDtypeStruct(w.shape, f32) for w in ws] * 3,
        compiler_params=pltpu.CompilerParams(vmem_limit_bytes=VMEM_LIMIT),
    )(*ws, *gs, *ms, *vs)
    return res[:n], res[n:2 * n], res[2 * n:]


PACK_W = 1024


def _pack(parts, halves, row_mult):
    flat = jnp.concatenate([p.reshape(-1) for p in parts])
    per = halves * row_mult * PACK_W
    total = -(-flat.size // per) * per
    flat = jnp.pad(flat, (0, total - flat.size))
    return flat.reshape(halves, total // (halves * PACK_W), PACK_W)


def _unpack(flat, shapes):
    out, pos = [], 0
    for s in shapes:
        n = int(np.prod(s))
        out.append(flat[pos:pos + n].reshape(s))
        pos += n
    return out


SHARDED_BIG = ("w_in", "w_branch", "w_out", "s5_w_glu")
SHARDED_SMALL = ("m2_conv_w", "sc_conv_w", "merge_b")
SHARD_AXIS = {"w_in": 2, "w_branch": 3, "w_out": 1, "s5_w_glu": 1, "m2_conv_w": 2, "sc_conv_w": 2, "merge_b": 2}
REPLICATED = ("norm_w", "s5_lambda_re", "s5_lambda_im", "s5_b_re", "s5_b_im", "s5_c_re", "s5_c_im", "s5_d", "s5_log_step",
              "sgu_ln_w", "sgu_ln_b", "sgu_w", "sgu_b", "m2_conv_b", "m2_dt_bias", "m2_a_log", "m2_d", "m2_norm_w")
WEIGHTS = ("norm_w", "w_in", "s5_lambda_re", "s5_lambda_im", "s5_b_re", "s5_b_im", "s5_c_re", "s5_c_im", "s5_d",
           "s5_log_step", "s5_w_glu", "sgu_ln_w", "sgu_ln_b", "sgu_w", "sgu_b", "m2_conv_w", "m2_conv_b", "m2_dt_bias",
           "m2_a_log", "m2_d", "m2_norm_w", "sc_conv_w", "merge_b", "w_branch", "w_out", "final_norm_w")
N_LAYERS = 2


SHARD_W = IN_DIM // 4
SHARD_PAD = -(-SHARD_W // LANES) * LANES
REGROUP_W = 3 * LANES


def _kernel_pieces():
    out, pos = [], 0
    for s, n in _col_segments():
        while n:
            take = min(n, SHARD_W - s % SHARD_W)
            out.append((pos, s, take))
            pos, s, n = pos + take, s + take, n - take
    return out


def regroup_cols(name, src, steps, out_shape, out=None):
    n_src, rows, width = src.shape
    ow, win = REGROUP_W, REGROUP_W + LANES
    k_max = max(len(p) for _, _, p in steps)
    assert width % LANES == 0

    def body(*refs):
        src_ref, out_ref = refs[0], refs[-5]
        wbuf, obuf, sem_in, sem_out = refs[-4:]

        def fetch(q, slot):
            started = []
            for p, (s, col, lo, hi) in enumerate(steps[q][2]):
                w0 = col // LANES * LANES
                wlen = min(win, width - w0)
                cp = pltpu.make_async_copy(src_ref.at[s, :, pl.ds(w0, wlen)], wbuf.at[slot, p, :, pl.ds(0, wlen)], sem_in.at[slot, p])
                cp.start()
                started.append((cp, wlen, col - w0 - lo, lo, hi))
            return started

        pend, writes = fetch(0, 0), [None, None]
        for q, (t, b, _) in enumerate(steps):
            slot = q % 2
            nxt = fetch(q + 1, 1 - slot) if q + 1 < len(steps) else []
            acc = [jnp.zeros((rows, LANES), f32) for _ in range(ow // LANES)]
            for p, (cp, wlen, shift, lo, hi) in enumerate(pend):
                cp.wait()
                for k in range(ow // LANES):
                    c_lo, c_hi = max(lo, k * LANES), min(hi, (k + 1) * LANES)
                    if c_lo >= c_hi:
                        continue
                    wb = (c_lo + shift) // LANES
                    alen = min(2 * LANES, wlen - wb * LANES)
                    a = wbuf[slot, p, :, wb * LANES:wb * LANES + alen]
                    r, c = _rows((alen, LANES)), _cols((alen, LANES))
                    sh = shift + (k - wb) * LANES
                    sel = (r == c + sh) & (c >= c_lo - k * LANES) & (c < c_hi - k * LANES)
                    acc[k] = acc[k] + jnp.dot(a, sel.astype(bf16), preferred_element_type=f32)
            if writes[slot] is not None:
                writes[slot].wait()
            for k in range(ow // LANES):
                obuf[slot, :, k * LANES:(k + 1) * LANES] = acc[k].astype(bf16)
            writes[slot] = pltpu.make_async_copy(obuf.at[slot], out_ref.at[t, :, pl.ds(b * ow, ow)], sem_out.at[slot])
            writes[slot].start()
            pend = nxt
        for wr in writes:
            if wr is not None:
                wr.wait()

    operands, io_alias = [src], {}
    if out is not None:
        operands.append(out)
        io_alias = {1: 0}
    return pl.pallas_call(
        body, name=name, in_specs=[ANY] * len(operands), out_specs=ANY,
        out_shape=jax.ShapeDtypeStruct(out_shape, bf16), input_output_aliases=io_alias,
        scratch_shapes=[pltpu.VMEM((2, k_max, rows, win), bf16), pltpu.VMEM((2, rows, ow), bf16),
                        pltpu.SemaphoreType.DMA((2, k_max)), pltpu.SemaphoreType.DMA((2,))],
        compiler_params=pltpu.CompilerParams(vmem_limit_bytes=VMEM_LIMIT),
    )(*operands)


def _steps_to_kernel_cols(layer):
    steps = []
    for o in range(PW // REGROUP_W):
        pieces = []
        for pos, s, n in _kernel_pieces():
            lo, hi = max(pos, o * REGROUP_W), min(pos + n, (o + 1) * REGROUP_W)
            if lo < hi:
                ref_col = s + lo - pos
                pieces.append(((ref_col // SHARD_W) * N_LAYERS + layer, ref_col % SHARD_W, lo - o * REGROUP_W, hi - o * REGROUP_W))
        steps.append((0, o, pieces))
    return steps


def _steps_to_shards(layer):
    steps = []
    for j in range(4):
        for b in range(SHARD_PAD // REGROUP_W):
            start, stop = j * SHARD_W + b * REGROUP_W, min(j * SHARD_W + (b + 1) * REGROUP_W, (j + 1) * SHARD_W)
            pieces = []
            for pos, s, n in _kernel_pieces():
                lo, hi = max(s, start), min(s + n, stop)
                if lo < hi:
                    pieces.append((0, pos + lo - s, lo - start, hi - start))
            steps.append((layer * 4 + j, b, pieces))
    return steps


def _gather_weights(w):
    w_in = jnp.pad(w["w_in"].astype(bf16), ((0, 0), (0, 0), (0, SHARD_PAD - SHARD_W)))
    arrs = [w_in, w["w_branch"].reshape(N_LAYERS, N_BRANCH * BW, -1).astype(bf16),
            w["w_out"].astype(bf16), w["s5_w_glu"].astype(bf16), w["m2_conv_w"], w["sc_conv_w"], w["merge_b"]]
    got = gather_chips("ag_weights", arrs)
    cols = lambda t: jnp.transpose(t, (1, 0, 2)).reshape(t.shape[1], -1)
    layers = []
    for i in range(N_LAYERS):
        p = {k: w[k][i] for k in REPLICATED}
        p["w_in"] = regroup_cols(f"w_in_cols{i}", got[0].reshape(4 * N_LAYERS, D_MODEL, SHARD_PAD), _steps_to_kernel_cols(i),
                                 (1, D_MODEL, PW))[0]
        p["w_branch"] = cols(got[1][:, i])
        p["w_out"] = got[2][:, i].reshape(D_MODEL, D_MODEL)
        p["s5_w_glu"] = got[3][:, i].reshape(BW, BW)
        p["m2_conv_w"], p["sc_conv_w"], p["merge_b"] = cols(got[4][:, i]), cols(got[5][:, i]), cols(got[6][:, i])
        layers.append(p)
    return layers


def _reduce_grads(grads, d_final, w):
    to_chips = lambda t: jnp.transpose(t.reshape(t.shape[0], 4, -1), (1, 0, 2))
    stack = lambda f: jnp.stack([f(g) for g in grads])
    dw_in = None
    for i, g in enumerate(grads):
        dw_in = regroup_cols(f"w_in_shards{i}", g["w_in"][None], _steps_to_shards(i), (4 * N_LAYERS, D_MODEL, SHARD_PAD), out=dw_in)
    parts = [dw_in.reshape(N_LAYERS, 4, D_MODEL, SHARD_PAD),
             stack(lambda g: to_chips(g["w_branch"].reshape(N_BRANCH * BW, D_MODEL)).astype(bf16)),
             stack(lambda g: g["w_out"].reshape(4, D_MODEL // 4, D_MODEL).astype(bf16)),
             stack(lambda g: g["s5_w_glu"].reshape(4, BW // 4, BW).astype(bf16))]
    rep = jnp.concatenate([stack(lambda g: g[k]).reshape(-1) for k in REPLICATED] + [d_final.reshape(-1)])
    quarter = -(-rep.size // (4 * 2 * SUBL * PACK_W)) * (2 * SUBL * PACK_W)
    rep = jnp.pad(rep, (0, 4 * quarter - rep.size))
    small = []
    for j in range(4):
        sharded = [stack(lambda g: to_chips(g[k])[j]) for k in SHARDED_SMALL]
        small.append(_pack(sharded + [rep[j * quarter:(j + 1) * quarter]], 2, SUBL))
    parts.append(jnp.stack(small, axis=1))
    red = reduce_to_shards(parts)
    out = {"w_in": red[0][:, :, :SHARD_W], "w_branch": red[1].reshape(w["w_branch"].shape), "w_out": red[2], "s5_w_glu": red[3]}
    small_flat = red[4].reshape(-1)
    n_small = sum(int(np.prod(w[k].shape)) for k in SHARDED_SMALL)
    out.update(zip(SHARDED_SMALL, _unpack(small_flat, [w[k].shape for k in SHARDED_SMALL])))
    mine = small_flat[n_small:n_small + quarter].reshape(2, quarter // (2 * PACK_W), PACK_W)
    rep_all = gather_chips("ag_small_grads", [mine])[0].reshape(-1)
    names = REPLICATED + ("final_norm_w",)
    out.update(zip(names, _unpack(rep_all, [w[k].shape for k in names])))
    return out


def _update(w, g, m, v):
    d, nm, nv = {}, {}, {}
    flat2 = lambda a: a.reshape(-1, a.shape[-1])
    cols_major = lambda a: jnp.transpose(a, (2, 0, 1))
    res = adamw("adamw_w_in", *[cols_major(t["w_in"]) for t in (w, g, m, v)], tr=SHARD_W // 14)
    d["w_in"], nm["w_in"], nv["w_in"] = (jnp.transpose(r, (1, 2, 0)) for r in res)
    for k in SHARDED_BIG[1:]:
        res = adamw("adamw_" + k, *[flat2(t[k]) for t in (w, g, m, v)])
        d[k], nm[k], nv[k] = (r.reshape(w[k].shape) for r in res)
    for k in ("s5_b_re", "s5_b_im"):
        res = adamw("adamw_" + k, *[flat2(t[k]) for t in (w, g, m, v)])
        d[k], nm[k], nv[k] = (r.reshape(w[k].shape) for r in res)
    rest = [k for k in WEIGHTS if k not in SHARDED_BIG + ("s5_b_re", "s5_b_im")]
    two_d = lambda a: a.reshape(1, -1) if a.ndim == 1 else a
    res = adamw_many("adamw_rest", *[[two_d(t[k]) for k in rest] for t in (w, g, m, v)])
    for tgt, rs in zip((d, nm, nv), res):
        tgt.update({k: r.reshape(w[k].shape) for k, r in zip(rest, rs)})
    return d, nm, nv


def kernel(x, norm_w, w_in, s5_lambda_re, s5_lambda_im, s5_b_re, s5_b_im, s5_c_re, s5_c_im, s5_d, s5_log_step, s5_w_glu, sgu_ln_w, sgu_ln_b, sgu_w, sgu_b, m2_conv_w, m2_conv_b, m2_dt_bias, m2_a_log, m2_d, m2_norm_w, sc_conv_w, merge_b, w_branch, w_out, final_norm_w, loss_target, m_norm_w, m_w_in, m_s5_lambda_re, m_s5_lambda_im, m_s5_b_re, m_s5_b_im, m_s5_c_re, m_s5_c_im, m_s5_d, m_s5_log_step, m_s5_w_glu, m_sgu_ln_w, m_sgu_ln_b, m_sgu_w, m_sgu_b, m_m2_conv_w, m_m2_conv_b, m_m2_dt_bias, m_m2_a_log, m_m2_d, m_m2_norm_w, m_sc_conv_w, m_merge_b, m_w_branch, m_w_out, m_final_norm_w, v_norm_w, v_w_in, v_s5_lambda_re, v_s5_lambda_im, v_s5_b_re, v_s5_b_im, v_s5_c_re, v_s5_c_im, v_s5_d, v_s5_log_step, v_s5_w_glu, v_sgu_ln_w, v_sgu_ln_b, v_sgu_w, v_sgu_b, v_m2_conv_w, v_m2_conv_b, v_m2_dt_bias, v_m2_a_log, v_m2_d, v_m2_norm_w, v_sc_conv_w, v_merge_b, v_w_branch, v_w_out, v_final_norm_w):
    given = dict(locals())
    w = {k: given[k] for k in WEIGHTS}
    m = {k: given["m_" + k] for k in WEIGHTS}
    v = {k: given["v_" + k] for k in WEIGHTS}
    layers = _gather_weights(w)
    loss, dx, grads, d_final = local_step(x[0], loss_target[0], layers, final_norm_w)
    loss = lax.psum(loss, ("x", "y", "c"))
    g = _reduce_grads(grads, d_final, w)
    d, nm, nv = _update(w, g, m, v)
    return (loss, dx[None], *[g[k] for k in WEIGHTS], *[d[k] for k in WEIGHTS],
            *[nm[k] for k in WEIGHTS], *[nv[k] for k in WEIGHTS])
```

```python
import functools
from typing import Any, Callable, NamedTuple

import numpy as np
import jax
import jax.numpy as jnp
from jax import lax
from jax.experimental import pallas as pl
from jax.experimental.pallas import tpu as pltpu

f32 = jnp.float32
bf16 = jnp.bfloat16

D_MODEL = 1024
BW = 512
N_BRANCH = 4
EPS = 1e-6
S5_GROUPS, S5_P, S5_N = 32, 16, 64
S5_NS = S5_GROUPS * S5_N
CHUNK = 128
M2_HEADS, M2_HEAD_DIM, M2_GROUPS, M2_STATE = 8, 64, 2, 128
IN_DIM = 10248
PW = 10368
PW_MAIN = 10240
LANES = 128
VMEM_LIMIT = 60 * 1024 * 1024

ADAM_LR, ADAM_B1, ADAM_B2, ADAM_EPS, ADAM_WD, ADAM_STEP = 0.001, 0.9, 0.999, 1e-08, 0.01, 10

C_MERGE = 0
C_SC = 4096
C_SGU = 6144
C_S5G = 7680
C_S5U = 8192
C_M2Z = 8704
C_XBC = 9216
C_DT = 10240


def _col_segments():
    segs = [(6152, 4096)]
    for j in range(4):
        segs += [(4104 + 128 * j, 128), (4616 + 128 * j, 128), (5128 + 128 * j, 128), (5640 + 128 * j, 128)]
    segs += [(1024, 1536), (512, 512), (0, 512), (2560, 512), (3072, 1024), (4096, 8)]
    return segs


NN = ((1,), (0,))
NT = ((1,), (1,))
TN = ((0,), (0,))


def _bd(a, b, dims):
    return lax.dot_general(a.astype(bf16), b.astype(bf16), (dims, ((), ())), preferred_element_type=f32)


def _hd(a, b, dims):
    return lax.dot_general(a, b, (dims, ((), ())), precision=lax.Precision.HIGHEST, preferred_element_type=f32)


def _make_dots(raw):
    @jax.custom_vjp
    def nn(a, b):
        return raw(a, b, NN)
    nn.defvjp(lambda a, b: (raw(a, b, NN), (a, b)), lambda r, g: (raw(g, r[1], NT), raw(r[0], g, TN)))

    @jax.custom_vjp
    def nt(a, b):
        return raw(a, b, NT)
    nt.defvjp(lambda a, b: (raw(a, b, NT), (a, b)), lambda r, g: (raw(g, r[1], NN), raw(g, r[0], TN)))

    @jax.custom_vjp
    def tn(a, b):
        return raw(a, b, TN)
    tn.defvjp(lambda a, b: (raw(a, b, TN), (a, b)), lambda r, g: (raw(r[1], g, NT), raw(r[0], g, NN)))
    return nn, nt, tn


bdot, bdot_nt, bdot_tn = _make_dots(_bd)
hdot, hdot_nt, hdot_tn = _make_dots(_hd)


@jax.custom_vjp
def bdot_w(a, w, shadow):
    return _bd(a, w, NN)


bdot_w.defvjp(lambda a, w, s: (_bd(a, w, NN), (a, w)),
              lambda r, g: (_bd(g, r[1], NT), jnp.zeros_like(r[1]), _bd(r[0], g, TN)))


def _rows(shape):
    return lax.broadcasted_iota(jnp.int32, shape, 0)


def _cols(shape):
    return lax.broadcasted_iota(jnp.int32, shape, 1)


def _shift_down(x, s):
    return jnp.where(_rows(x.shape) < s, 0.0, pltpu.roll(x, s, 0))


def _shift_up(x, s):
    n = x.shape[0]
    return jnp.where(_rows(x.shape) >= n - s, 0.0, pltpu.roll(x, n - s, 0))


@functools.partial(jax.custom_vjp, nondiff_argnums=(1,))
def shift(x, s):
    return _shift_down(x, s) if s else x


shift.defvjp(lambda x, s: (shift(x, s), None), lambda s, _, g: (_shift_up(g, s) if s else g,))


def _row_of(w, k):
    return jnp.sum(jnp.where(_rows(w.shape) == k, w, 0.0), axis=0, keepdims=True)


def _lane_mask(width, lo, hi):
    c = _cols((1, width))
    return ((c >= lo) & (c < hi)).astype(f32)


def _expand(rows, width, per):
    return (_cols((rows, width)) // per == _rows((rows, width))).astype(f32)


class A(NamedTuple):
    arr: Any
    block: tuple
    imap: Callable
    shadow: bool = False


class O(NamedTuple):
    shape: tuple
    dtype: Any
    block: tuple
    imap: Callable
    alias: Any = None


class R(NamedTuple):
    arg: int
    out: int
    off: Any = None
    acc: bool = False


def _cparams(n_grid):
    return pltpu.CompilerParams(dimension_semantics=("arbitrary",) * n_grid, vmem_limit_bytes=VMEM_LIMIT)


def _ispec(block, imap, n, rev):
    if rev:
        return pl.BlockSpec(block, lambda i: imap(n - 1 - i))
    return pl.BlockSpec(block, imap)


def _load(ref, a):
    v = ref[...]
    if a.shadow:
        return (v, jnp.zeros(v.shape, f32))
    return v.astype(f32)


def _save_spec(shape, n, rev):
    nd = len(shape)
    return _ispec((None,) + tuple(shape), lambda i: (i,) + (0,) * nd, n, rev)


def block_fwd(name, f, n, args, outs, carries=()):
    n_in, n_out, n_c = len(args), len(outs), len(carries)

    def body(*refs):
        ins, out_r = refs[:n_in], refs[n_in:n_in + n_out]
        saves, cs = refs[n_in + n_out:n_in + n_out + n_c], refs[n_in + n_out + n_c:]
        if n_c:
            @pl.when(pl.program_id(0) == 0)
            def _():
                for c in cs:
                    c[...] = jnp.zeros(c.shape, f32)
        vals = [_load(r, a) for r, a in zip(ins, args)]
        cv = [c[...] for c in cs]
        for s, v in zip(saves, cv):
            s[...] = v
        res = f(*vals, *cv)
        for r, v in zip(out_r, res[:n_out]):
            r[...] = v.astype(r.dtype)
        for c, v in zip(cs, res[n_out:]):
            c[...] = v

    out_shape = [jax.ShapeDtypeStruct(o.shape, o.dtype) for o in outs]
    out_specs = [pl.BlockSpec(o.block, o.imap) for o in outs]
    for shp in carries:
        out_shape.append(jax.ShapeDtypeStruct((n,) + tuple(shp), f32))
        out_specs.append(_save_spec(shp, n, False))
    return pl.pallas_call(
        body, name=name, grid=(n,),
        in_specs=[pl.BlockSpec(a.block, a.imap) for a in args],
        out_specs=out_specs, out_shape=out_shape,
        scratch_shapes=[pltpu.VMEM(tuple(shp), f32) for shp in carries],
        compiler_params=_cparams(1),
    )(*[a.arr for a in args])


def block_bwd(name, f, n, args, cots, gouts, routes, saved=(), rev=False):
    n_in, n_cot, n_c, n_go = len(args), len(cots), len(saved), len(gouts)
    diff = []
    for r in routes:
        if r.arg not in diff:
            diff.append(r.arg)
    aliases = [(k, o.alias) for k, o in enumerate(gouts) if o.alias is not None]

    def body(*refs):
        ins = refs[:n_in]
        cot_r = refs[n_in:n_in + n_cot]
        sav_r = refs[n_in + n_cot:n_in + n_cot + n_c]
        base = n_in + n_cot + n_c + len(aliases)
        go_r = refs[base:base + n_go]
        dcs = refs[base + n_go:]
        step = pl.program_id(0)
        if n_c:
            @pl.when(step == 0)
            def _():
                for d in dcs:
                    d[...] = jnp.zeros(d.shape, f32)
        vals = [_load(r, a) for r, a in zip(ins, args)]
        cv = [s[...] for s in sav_r]
        nd = len(diff)

        def g(*dv):
            full = list(vals)
            for idx, v in zip(diff, dv[:nd]):
                full[idx] = (vals[idx][0], v) if args[idx].shadow else v
            return tuple(f(*full, *dv[nd:]))

        primals = [vals[i][1] if args[i].shadow else vals[i] for i in diff] + cv
        _, vjp_fn = jax.vjp(g, *primals)
        ct = tuple([r[...].astype(f32) for r in cot_r] + [d[...] for d in dcs])
        grads = vjp_fn(ct)
        for r in routes:
            gr = grads[diff.index(r.arg)]
            ref = go_r[r.out]
            if r.acc:
                @pl.when(step == 0)
                def _(ref=ref, gr=gr):
                    ref[...] = gr.astype(ref.dtype)

                @pl.when(step > 0)
                def _(ref=ref, gr=gr):
                    ref[...] += gr.astype(ref.dtype)
            elif r.off is None:
                ref[...] = gr.astype(ref.dtype)
            else:
                ref[:, r.off:r.off + gr.shape[1]] = gr.astype(ref.dtype)
        for d, gr in zip(dcs, grads[nd:]):
            d[...] = gr

    in_specs = [_ispec(a.block, a.imap, n, rev) for a in list(args) + list(cots)]
    in_specs += [_save_spec(s.shape[1:], n, rev) for s in saved]
    in_specs += [pl.BlockSpec(memory_space=pl.ANY) for _ in aliases]
    operands = [a.arr for a in list(args) + list(cots)] + list(saved) + [arr for _, arr in aliases]
    io_alias = {n_in + n_cot + n_c + j: k for j, (k, _) in enumerate(aliases)}
    return pl.pallas_call(
        body, name=name, grid=(n,),
        in_specs=in_specs,
        out_specs=[_ispec(o.block, o.imap, n, rev) for o in gouts],
        out_shape=[jax.ShapeDtypeStruct(o.shape, o.dtype) for o in gouts],
        scratch_shapes=[pltpu.VMEM(tuple(s.shape[1:]), f32) for s in saved],
        input_output_aliases=io_alias,
        compiler_params=_cparams(1),
    )(*operands)


class Op(NamedTuple):
    arr: Any
    row: int = 0
    col: int = 0


def mm(name, mode, pairs, m, n, tm, tn, out_dtype=f32, add=None, out=None, out_col=0, out_width=None):
    tm, tn = min(tm, m), min(tn, n)
    assert m % tm == 0 and n % tn == 0
    in_specs, operands = [], []
    for a, b, k, _ in pairs:
        if mode == TN:
            assert a.row % k == 0 and a.col % tm == 0
            in_specs.append(pl.BlockSpec((k, tm), lambda j, i, a=a, k=k: (a.row // k, i + a.col // tm)))
        else:
            assert a.col % k == 0 and a.row % tm == 0
            in_specs.append(pl.BlockSpec((tm, k), lambda j, i, a=a, k=k: (i + a.row // tm, a.col // k)))
        if mode == NT:
            assert b.col % k == 0 and b.row % tn == 0
            in_specs.append(pl.BlockSpec((tn, k), lambda j, i, b=b, k=k: (j + b.row // tn, b.col // k)))
        else:
            assert b.row % k == 0 and b.col % tn == 0
            in_specs.append(pl.BlockSpec((k, tn), lambda j, i, b=b, k=k: (b.row // k, j + b.col // tn)))
        operands += [a.arr, b.arr]
    n_p = len(pairs)
    if add is not None:
        assert add.col % tn == 0
        in_specs.append(pl.BlockSpec((tm, tn), lambda j, i: (i, j + add.col // tn)))
        operands.append(add.arr)
    io_alias = {}
    if out is not None:
        assert out_col % tn == 0
        in_specs.append(pl.BlockSpec(memory_space=pl.ANY))
        operands.append(out)
        io_alias = {len(operands) - 1: 0}
        out_shape = jax.ShapeDtypeStruct(out.shape, out.dtype)
    else:
        out_shape = jax.ShapeDtypeStruct((m, out_width or n), out_dtype)
    signs = [p[3] for p in pairs]

    def body(*refs):
        o = refs[-1]
        acc = None
        for p in range(n_p):
            t = _bd(refs[2 * p][...], refs[2 * p + 1][...], mode)
            t = t if signs[p] > 0 else -t
            acc = t if acc is None else acc + t
        if add is not None:
            acc = acc + refs[2 * n_p][...].astype(f32)
        o[...] = acc.astype(o.dtype)

    return pl.pallas_call(
        body, name=name, grid=(n // tn, m // tm),
        in_specs=in_specs,
        out_specs=pl.BlockSpec((tm, tn), lambda j, i: (i, j + out_col // tn)),
        out_shape=out_shape, input_output_aliases=io_alias,
        compiler_params=_cparams(2),
    )(*operands)


SCAN_LANES = 512
SUBL = 8


def _cmul(p, q):
    return (p[0] * q[0] - p[1] * q[1], p[0] * q[1] + p[1] * q[0])


def _powers(a):
    a2 = _cmul(a, a)
    a4 = _cmul(a2, a2)
    a6 = _cmul(a4, a2)
    return [a, a2, _cmul(a2, a), a4, _cmul(a4, a), a6, _cmul(a6, a), _cmul(a4, a4)]


def _table(pw, order, w):
    row = _rows((SUBL, w))
    re = sum(jnp.where(row == t, pw[k][0], 0.0) for t, k in enumerate(order))
    im = sum(jnp.where(row == t, pw[k][1], 0.0) for t, k in enumerate(order))
    return re, im


def s5_scan_fwd(bu_re, bu_im, a_re, a_im):
    seq, ns = bu_re.shape
    w, nb = SCAN_LANES, ns // SCAN_LANES

    def body(b_re, b_im, ar, ai, s_re, s_im):
        a = (ar[...], ai[...])
        pw = _powers(a)
        tab = _table(pw, list(range(SUBL)), w)
        row = _rows((SUBL, w))

        def step(i, carry):
            t0 = pl.multiple_of(i * SUBL, SUBL)
            x = (b_re[pl.ds(t0, SUBL), :], b_im[pl.ds(t0, SUBL), :])
            for d, k in ((1, 0), (2, 1), (4, 3)):
                sh = (jnp.where(row < d, 0.0, pltpu.roll(x[0], d, 0)), jnp.where(row < d, 0.0, pltpu.roll(x[1], d, 0)))
                t = _cmul(pw[k], sh)
                x = (x[0] + t[0], x[1] + t[1])
            t = _cmul(tab, carry)
            x = (x[0] + t[0], x[1] + t[1])
            s_re[pl.ds(t0, SUBL), :] = x[0]
            s_im[pl.ds(t0, SUBL), :] = x[1]
            return (x[0][SUBL - 1:, :], x[1][SUBL - 1:, :])

        z = jnp.zeros((1, w), f32)
        lax.fori_loop(0, seq // SUBL, step, (z, z), unroll=2)

    strip = pl.BlockSpec((seq, w), lambda j: (0, j))
    lane = pl.BlockSpec((1, w), lambda j: (0, j))
    return pl.pallas_call(
        body, name="s5_scan_fwd", grid=(nb,),
        in_specs=[strip, strip, lane, lane],
        out_specs=[strip, strip],
        out_shape=[jax.ShapeDtypeStruct((seq, ns), f32)] * 2,
        compiler_params=_cparams(1),
    )(bu_re, bu_im, a_re, a_im)


def s5_scan_bwd(ds_re, ds_im, s_re, s_im, a_re, a_im):
    seq, ns = ds_re.shape
    w, nb = SCAN_LANES, ns // SCAN_LANES
    nblk = seq // SUBL

    def body(g_re, g_im, sr, si, ar, ai, l_re, l_im, da_re, da_im):
        a = (ar[...], -ai[...])
        pw = _powers(a)
        tab = _table(pw, [SUBL - 1 - t for t in range(SUBL)], w)
        row = _rows((SUBL, w))

        def step(kk, carry):
            c_re, c_im, acc_re, acc_im = carry
            i = nblk - 1 - kk
            t0 = pl.multiple_of(i * SUBL, SUBL)
            x = (g_re[pl.ds(t0, SUBL), :], g_im[pl.ds(t0, SUBL), :])
            for d, k in ((1, 0), (2, 1), (4, 3)):
                sh = (jnp.where(row >= SUBL - d, 0.0, pltpu.roll(x[0], SUBL - d, 0)),
                      jnp.where(row >= SUBL - d, 0.0, pltpu.roll(x[1], SUBL - d, 0)))
                t = _cmul(pw[k], sh)
                x = (x[0] + t[0], x[1] + t[1])
            t = _cmul(tab, (c_re, c_im))
            x = (x[0] + t[0], x[1] + t[1])
            l_re[pl.ds(t0, SUBL), :] = x[0]
            l_im[pl.ds(t0, SUBL), :] = x[1]
            tp = jnp.maximum(t0 - 1, 0)
            live = (i > 0).astype(f32)
            p_re = sr[pl.ds(tp, 1), :] * live
            p_im = si[pl.ds(tp, 1), :] * live
            sp_re = jnp.where(row == 0, p_re, pltpu.roll(sr[pl.ds(t0, SUBL), :], 1, 0))
            sp_im = jnp.where(row == 0, p_im, pltpu.roll(si[pl.ds(t0, SUBL), :], 1, 0))
            acc_re = acc_re + x[0] * sp_re + x[1] * sp_im
            acc_im = acc_im + x[1] * sp_re - x[0] * sp_im
            return (x[0][:1, :], x[1][:1, :], acc_re, acc_im)

        z1 = jnp.zeros((1, w), f32)
        z8 = jnp.zeros((SUBL, w), f32)
        _, _, acc_re, acc_im = lax.fori_loop(0, nblk, step, (z1, z1, z8, z8), unroll=2)
        da_re[...] = jnp.sum(acc_re, axis=0, keepdims=True)
        da_im[...] = jnp.sum(acc_im, axis=0, keepdims=True)

    strip = pl.BlockSpec((seq, w), lambda j: (0, j))
    lane = pl.BlockSpec((1, w), lambda j: (0, j))
    return pl.pallas_call(
        body, name="s5_scan_bwd", grid=(nb,),
        in_specs=[strip, strip, strip, strip, lane, lane],
        out_specs=[strip, strip, lane, lane],
        out_shape=[jax.ShapeDtypeStruct((seq, ns), f32)] * 2 + [jax.ShapeDtypeStruct((1, ns), f32)] * 2,
        compiler_params=_cparams(1),
    )(ds_re, ds_im, s_re, s_im, a_re, a_im)


def _rms(x, w):
    return x * lax.rsqrt(jnp.mean(x * x, axis=-1, keepdims=True) + EPS) * w


def f_rms(x, w):
    return (_rms(x, w),)


def f_rms_res(x, w):
    return (_rms(x, w), x)


def f_s5_prep(lam_re, lam_im, log_step, b_re, b_im):
    e = _expand(log_step.shape[1], S5_NS, S5_N)
    step = hdot(jnp.exp(log_step), e)
    mag = jnp.exp(lam_re * step)
    ab_re, ab_im = mag * jnp.cos(lam_im * step), mag * jnp.sin(lam_im * step)
    den = lam_re * lam_re + lam_im * lam_im
    nr = ab_re - 1.0
    coef_re = (nr * lam_re + ab_im * lam_im) / den
    coef_im = (ab_im * lam_re - nr * lam_im) / den
    bb_re, bb_im = coef_re * b_re - coef_im * b_im, coef_re * b_im + coef_im * b_re
    sel = (_rows((BW, S5_P)) % S5_P == _cols((BW, S5_P))).astype(f32)
    blk = _rows((BW, S5_NS)) // S5_P == _cols((BW, S5_NS)) // S5_N
    rows_bd = lambda t: jnp.where(blk, hdot(sel, t), 0.0)
    return (ab_re, ab_im, rows_bd(bb_re), rows_bd(bb_im))


def f_s5_c(c_re, c_im):
    sel_t = (_cols((S5_P, BW)) % S5_P == _rows((S5_P, BW))).astype(f32)
    blk_t = _rows((S5_NS, BW)) // S5_N == _cols((S5_NS, BW)) // S5_P
    cols_bd = lambda t: jnp.where(blk_t, hdot_tn(t, sel_t), 0.0)
    return (cols_bd(c_re), cols_bd(c_im))


def f_s5_act(y_lin, u, gate, d, w_glu):
    y = jax.nn.gelu(y_lin + d * u)
    y = y * jax.nn.sigmoid(bdot_w(y, *w_glu))
    return (y * jax.nn.silu(gate),)


def f_sgu(u, v, gate, ln_w, ln_b, *rest):
    w_s, b_pad = rest[:8], rest[8]
    t = u.shape[0]
    u32, v32 = jax.nn.gelu(u), jax.nn.gelu(v)
    mu = jnp.mean(v32, axis=-1, keepdims=True)
    var = jnp.mean(jnp.square(v32 - mu), axis=-1, keepdims=True)
    vn = (v32 - mu) * lax.rsqrt(var + EPS) * ln_w + ln_b
    tri = _rows((t, t)) >= _cols((t, t))
    s = hdot_tn(b_pad, _expand(LANES, BW, BW // 8))
    for h in range(8):
        s = s + bdot(jnp.where(tri, w_s[h], 0.0), vn) * _lane_mask(BW, 64 * h, 64 * h + 64)
    return (u32 * s * jax.nn.silu(gate),)


def f_m2_conv(x, w, b):
    return (sum(_row_of(w, k) * shift(x, 3 - k) for k in range(4)) + b,)


def f_sc(bg, cg, h, gate, w):
    z = cg * h
    conv = sum(_row_of(w, k) * shift(z, 2 - k) for k in range(3))
    return (bg * conv * jax.nn.silu(gate),)


def f_m2(z, xc, b0, b1, c0, c1, dt_raw, dt_bias, a_log, d_par, norm_w, st):
    q = z.shape[0]
    x = jax.nn.silu(xc)
    bm, cm = (jax.nn.silu(b0), jax.nn.silu(b1)), (jax.nn.silu(c0), jax.nn.silu(c1))
    dt = jax.nn.softplus(dt_raw + dt_bias)
    da = dt * (-jnp.exp(a_log))
    tri = _rows((q, q)) >= _cols((q, q))
    acs = hdot(tri.astype(f32), da)
    e = _expand(LANES, BW, M2_HEAD_DIM)
    dt_f, acs_f = hdot(dt, e), hdot(acs, e)
    last = _rows((q, BW)) == q - 1
    alast_f = jnp.sum(jnp.where(last, acs_f, 0.0), axis=0, keepdims=True)
    xdt = x * dt_f
    xdec = xdt * jnp.exp(alast_f - acs_f)
    acs_t = acs.T
    st_new = st * jnp.exp(alast_f)
    y_diag, y_off = 0.0, 0.0
    for g in range(M2_GROUPS):
        gm = _lane_mask(BW, 256 * g, 256 * g + 256)
        cb = bdot_nt(cm[g], bm[g])
        st_new = st_new + bdot_tn(bm[g], xdec * gm)
        y_off = y_off + bdot(cm[g], st) * gm
        for hh in range(M2_HEADS // M2_GROUPS):
            h = g * (M2_HEADS // M2_GROUPS) + hh
            col = jnp.sum(jnp.where(_cols((q, LANES)) == h, acs, 0.0), axis=1, keepdims=True)
            row = jnp.sum(jnp.where(_rows((LANES, q)) == h, acs_t, 0.0), axis=0, keepdims=True)
            decay = jnp.exp(jnp.where(tri, col - row, -1e30))
            y_diag = y_diag + bdot(cb * decay, xdt) * _lane_mask(BW, 64 * h, 64 * h + 64)
    d_f = sum(jnp.sum(jnp.where(_cols((1, LANES)) == h, d_par, 0.0), axis=1, keepdims=True)
              * _lane_mask(BW, 64 * h, 64 * h + 64) for h in range(M2_HEADS))
    y = y_diag + y_off * jnp.exp(acs_f) + d_f * x
    y = y * jax.nn.silu(z)
    return (_rms(y, norm_w), st_new)


def f_branch_mix(*v):
    nb = len(v) // 4
    ys, lg, wb, mb = v[0:nb], v[nb:2 * nb], v[2 * nb:3 * nb], v[3 * nb:]
    return (sum(jax.nn.sigmoid(lg[k] + mb[k]) * bdot_w(ys[k], *wb[k]) for k in range(nb)),)


def _param(arr, shadow=False):
    return A(arr, tuple(arr.shape), lambda i, nd=arr.ndim: (0,) * nd, shadow)


def _tb(arr, t, width, colblk):
    return A(arr, (t, width), lambda i: (i, colblk))


def _strip(arr, seq, colblk0, stride=1):
    return A(arr, (seq, LANES), lambda j: (0, colblk0 + stride * j))


def _s5_prep_args(p):
    lam_re = p["s5_lambda_re"].reshape(1, S5_NS)
    lam_im = p["s5_lambda_im"].reshape(1, S5_NS)
    log_step = jnp.pad(p["s5_log_step"].reshape(1, S5_GROUPS), ((0, 0), (0, LANES - S5_GROUPS)))
    b_lanes = lambda b: jnp.transpose(b, (2, 0, 1)).reshape(S5_P, S5_NS)
    return [_param(v) for v in (lam_re, lam_im, log_step, b_lanes(p["s5_b_re"]), b_lanes(p["s5_b_im"]))]


def _s5_c_args(p):
    c_lanes = lambda c: jnp.transpose(c, (1, 0, 2)).reshape(S5_P, S5_NS)
    return [_param(c_lanes(p["s5_c_re"])), _param(c_lanes(p["s5_c_im"]))]


def layer_fwd(x, p):
    seq = x.shape[0]
    nt = seq // CHUNK
    t2 = 256
    sv = {}
    t4 = min(512, seq)
    hb = block_fwd("rms_in", f_rms, seq // t4, [_tb(x, t4, D_MODEL, 0), _param(p["norm_w"].reshape(1, D_MODEL))],
                   [O((seq, D_MODEL), bf16, (t4, D_MODEL), lambda i: (i, 0))])[0]
    proj = mm("mm_in", NN, [(Op(hb), Op(p["w_in"]), D_MODEL, 1)], seq, PW, 1024, 1152)
    whole = lambda shape, dt: O(shape, dt, shape, lambda i: (0, 0))
    ab_re, ab_im, bb_re, bb_im = block_fwd("s5_prep", f_s5_prep, 1, _s5_prep_args(p),
                                           [whole((1, S5_NS), f32)] * 2 + [whole((BW, S5_NS), bf16)] * 2)
    cc_re, cc_im = block_fwd("s5_c", f_s5_c, 1, _s5_c_args(p), [whole((S5_NS, BW), bf16)] * 2)
    bu_re = mm("mm_bu_re", NN, [(Op(proj, 0, C_S5U), Op(bb_re), BW, 1)], seq, S5_NS, 512, 1024)
    bu_im = mm("mm_bu_im", NN, [(Op(proj, 0, C_S5U), Op(bb_im), BW, 1)], seq, S5_NS, 512, 1024)
    s_re, s_im = s5_scan_fwd(bu_re, bu_im, ab_re, ab_im)
    y_lin = mm("mm_s5y", NN, [(Op(s_re), Op(cc_re), S5_NS, 1), (Op(s_im), Op(cc_im), S5_NS, -1)], seq, BW, 256, BW)
    s5_act_args = [_tb(y_lin, t4, BW, 0), _tb(proj, t4, BW, C_S5U // BW), _tb(proj, t4, BW, C_S5G // BW),
                   _param(p["s5_d"].reshape(1, BW)), _param(p["s5_w_glu"], True)]
    out_bw =O((seq, BW), f32, (CHUNK, BW), lambda i: (i, 0))
    y_a = block_fwd("s5_act", f_s5_act, seq // t4, s5_act_args, [O((seq, BW), f32, (t4, BW), lambda i: (i, 0))])[0]
    y_b = block_fwd("sgu", f_sgu, nt, _sgu_args(proj, p), [out_bw])[0]
    xc = block_fwd("m2_conv", f_m2_conv, 8, _m2_conv_args(proj, p, seq),
                   [O((seq, 2 * BW), f32, (seq, LANES), lambda j: (0, j))])[0]
    y_c, st_saved = block_fwd("m2_ssd", f_m2, nt, _m2_args(proj, xc, p), [out_bw], carries=[(M2_STATE, BW)])
    y_d = block_fwd("sc", f_sc, 4, _sc_args(proj, p, seq), [O((seq, BW), f32, (seq, LANES), lambda j: (0, j))])[0]
    ys = [y_a, y_b, y_c, y_d]
    merged = block_fwd("branch_mix", f_branch_mix, seq // t2, _mix_args(ys, proj, p, t2),
                       [O((seq, D_MODEL), bf16, (t2, D_MODEL), lambda i: (i, 0))])[0]
    x_new = mm("mm_out", NN, [(Op(merged), Op(p["w_out"]), D_MODEL, 1)], seq, D_MODEL, 512, D_MODEL, add=Op(x))
    sv.update(x=x, hb=hb, proj=proj, ab=(ab_re, ab_im), bb=(bb_re, bb_im), cc=(cc_re, cc_im), s=(s_re, s_im), y_lin=y_lin,
              xc=xc, st=st_saved, ys=ys, merged=merged)
    return x_new, sv


def _sgu_args(proj, p):
    c0 = C_SGU // BW
    args = [_tb(proj, CHUNK, BW, c0), _tb(proj, CHUNK, BW, c0 + 1), _tb(proj, CHUNK, BW, c0 + 2),
            _param(p["sgu_ln_w"].reshape(1, BW)), _param(p["sgu_ln_b"].reshape(1, BW))]
    args += [A(p["sgu_w"], (None, CHUNK, CHUNK), lambda i, h=h: (h, 0, 0)) for h in range(8)]
    args.append(_param(jnp.pad(p["sgu_b"], ((0, LANES - 8), (0, 0)))))
    return args


def _m2_conv_args(proj, p, seq):
    return [_strip(proj, seq, C_XBC // LANES), A(p["m2_conv_w"], (4, LANES), lambda j: (0, j)),
            A(p["m2_conv_b"].reshape(1, 2 * BW), (1, LANES), lambda j: (0, j))]


def _pad_lanes(v):
    return jnp.pad(v.reshape(1, -1), ((0, 0), (0, LANES - v.size)))


def _m2_args(proj, xc, p):
    args = [_tb(proj, CHUNK, BW, C_M2Z // BW), _tb(xc, CHUNK, BW, 0)]
    args += [_tb(xc, CHUNK, LANES, 4 + k) for k in range(4)]
    args.append(_tb(proj, CHUNK, LANES, C_DT // LANES))
    args += [_param(_pad_lanes(p["m2_dt_bias"])), _param(_pad_lanes(p["m2_a_log"])), _param(_pad_lanes(p["m2_d"])),
             _param(p["m2_norm_w"].reshape(1, BW))]
    return args


def _sc_args(proj, p, seq):
    c0 = C_SC // LANES
    return [_strip(proj, seq, c0 + k, 4) for k in range(4)] + [A(p["sc_conv_w"], (3, LANES), lambda j: (0, j))]


def _mix_args(ys, proj, p, t, ks=range(N_BRANCH)):
    args = [_tb(ys[k], t, BW, 0) for k in ks]
    args += [_tb(proj, t, D_MODEL, k) for k in ks]
    args += [A(p["w_branch"], (BW, D_MODEL), lambda i, k=k: (k, 0), True) for k in ks]
    mb = p["merge_b"].reshape(N_BRANCH, 1, D_MODEL)
    args += [A(mb, (None, 1, D_MODEL), lambda i, k=k: (k, 0, 0)) for k in ks]
    return args


def layer_bwd(d_out, p, sv):
    seq = d_out.shape[0]
    nt = seq // CHUNK
    t4 = min(512, seq)
    proj, ys = sv["proj"], sv["ys"]
    g = {}
    acc = lambda shape: O(tuple(shape), f32, tuple(shape), lambda i, nd=len(shape): (0,) * nd)
    d_merged = mm("mm_out_dx", NT, [(Op(d_out), Op(p["w_out"]), D_MODEL, 1)], seq, D_MODEL, 512, D_MODEL, out_dtype=f32)
    g["w_out"] = mm("mm_out_dw", TN, [(Op(sv["merged"]), Op(d_out), seq, 1)], D_MODEL, D_MODEL, 256, D_MODEL, out_dtype=bf16)
    dys, dwb, dmb, dproj, tb = [], [], [], None, min(256, seq)
    for half in range(2):
        gouts = [O((seq, BW), f32, (tb, BW), lambda i: (i, 0)) for _ in range(2)]
        gouts.append(O((seq, PW_MAIN), bf16, (tb, 2 * D_MODEL), lambda i, half=half: (i, half), alias=dproj))
        gouts += [acc((BW, D_MODEL)) for _ in range(2)] + [acc((1, D_MODEL)) for _ in range(2)]
        routes = [R(k, k) for k in range(2)] + [R(2 + k, 2, k * D_MODEL) for k in range(2)]
        routes += [R(4 + k, 3 + k, acc=True) for k in range(2)] + [R(6 + k, 5 + k, acc=True) for k in range(2)]
        res = block_bwd(f"branch_mix_bwd{half}", f_branch_mix, seq // tb, _mix_args(ys, proj, p, tb, (2 * half, 2 * half + 1)),
                        [_tb(d_merged, tb, D_MODEL, 0)], gouts, routes)
        dys, dproj, dwb, dmb = dys + list(res[:2]), res[2], dwb + list(res[3:5]), dmb + list(res[5:7])
    g["w_branch"] = jnp.stack(dwb)
    g["merge_b"] = jnp.concatenate(dmb, axis=0)
    res = block_bwd("sc_bwd", f_sc, 4, _sc_args(proj, p, seq), [_strip(dys[3], seq, 0)],
                    [O((seq, PW_MAIN), bf16, (seq, 4 * LANES), lambda j: (0, C_SC // (4 * LANES) + j), alias=dproj),
                     O((3, BW), f32, (3, LANES), lambda j: (0, j))],
                    [R(k, 0, k * LANES) for k in range(4)] + [R(4, 1)])
    dproj, g["sc_conv_w"] = res
    res = block_bwd("m2_ssd_bwd", f_m2, nt, _m2_args(proj, sv["xc"], p), [_tb(dys[2], CHUNK, BW, 0)],
                    [O((seq, PW_MAIN), bf16, (CHUNK, BW), lambda i: (i, C_M2Z // BW), alias=dproj),
                     O((seq, 2 * BW), f32, (CHUNK, 2 * BW), lambda i: (i, 0)),
                     O((seq, LANES), bf16, (CHUNK, LANES), lambda i: (i, 0)),
                     acc((1, LANES)), acc((1, LANES)), acc((1, LANES)), acc((1, BW))],
                    [R(0, 0), R(1, 1, 0)] + [R(2 + k, 1, BW + k * LANES) for k in range(4)] + [R(6, 2)]
                    + [R(7, 3, acc=True), R(8, 4, acc=True), R(9, 5, acc=True), R(10, 6, acc=True)],
                    saved=[sv["st"]], rev=True)
    dproj, dxc, d_dt = res[0], res[1], res[2]
    g["m2_dt_bias"], g["m2_a_log"], g["m2_d"] = (r[0, :M2_HEADS] for r in res[3:6])
    g["m2_norm_w"] = res[6].reshape(BW)
    res = block_bwd("m2_conv_bwd", f_m2_conv, 8, _m2_conv_args(proj, p, seq), [_strip(dxc, seq, 0)],
                    [O((seq, PW_MAIN), bf16, (seq, LANES), lambda j: (0, C_XBC // LANES + j), alias=dproj),
                     O((4, 2 * BW), f32, (4, LANES), lambda j: (0, j)), O((1, 2 * BW), f32, (1, LANES), lambda j: (0, j))],
                    [R(0, 0), R(1, 1), R(2, 2)])
    dproj, g["m2_conv_w"], cb = res
    g["m2_conv_b"] = cb.reshape(2 * BW)
    res = block_bwd("sgu_bwd", f_sgu, nt, _sgu_args(proj, p), [_tb(dys[1], CHUNK, BW, 0)],
                    [O((seq, PW_MAIN), bf16, (CHUNK, 3 * BW), lambda i: (i, C_SGU // (3 * BW)), alias=dproj),
                     acc((1, BW)), acc((1, BW))] + [acc((CHUNK, CHUNK)) for _ in range(8)] + [acc((LANES, CHUNK))],
                    [R(0, 0, 0), R(1, 0, BW), R(2, 0, 2 * BW), R(3, 1, acc=True), R(4, 2, acc=True)]
                    + [R(5 + h, 3 + h, acc=True) for h in range(8)] + [R(13, 11, acc=True)])
    dproj = res[0]
    g["sgu_ln_w"], g["sgu_ln_b"] = res[1].reshape(BW), res[2].reshape(BW)
    g["sgu_w"] = jnp.stack(res[3:11])
    g["sgu_b"] = res[11][:8]
    y_lin, (s_re, s_im), (ab_re, ab_im) = sv["y_lin"], sv["s"], sv["ab"]
    s5_act_args = [_tb(y_lin, t4, BW, 0), _tb(proj, t4, BW, C_S5U // BW), _tb(proj, t4, BW, C_S5G // BW),
                   _param(p["s5_d"].reshape(1, BW)), _param(p["s5_w_glu"], True)]
    res = block_bwd("s5_act_bwd", f_s5_act, seq // t4, s5_act_args, [_tb(dys[0], t4, BW, 0)],
                    [O((seq, BW), bf16, (t4, BW), lambda i: (i, 0)), O((seq, BW), f32, (t4, BW), lambda i: (i, 0)),
                     O((seq, PW_MAIN), bf16, (t4, BW), lambda i: (i, C_S5G // BW), alias=dproj),
                     acc((1, BW)), acc((BW, BW))],
                    [R(0, 0), R(1, 1), R(2, 2), R(3, 3, acc=True), R(4, 4, acc=True)])
    dy_lin, du1, dproj = res[0], res[1], res[2]
    g["s5_d"] = res[3].reshape(S5_GROUPS, S5_P)
    g["s5_w_glu"] = res[4]
    (bb_re, bb_im), (cc_re, cc_im) = sv["bb"], sv["cc"]
    ds_re = mm("mm_s5y_dre", NT, [(Op(dy_lin), Op(cc_re), BW, 1)], seq, S5_NS, 512, 1024)
    ds_im = mm("mm_s5y_dim", NT, [(Op(dy_lin), Op(cc_im), BW, -1)], seq, S5_NS, 512, 1024)
    dc_re = mm("mm_s5y_dcre", TN, [(Op(s_re), Op(dy_lin), seq, 1)], S5_NS, BW, 512, BW)
    dc_im = mm("mm_s5y_dcim", TN, [(Op(s_im), Op(dy_lin), seq, -1)], S5_NS, BW, 512, BW)
    l_re, l_im, da_re, da_im = s5_scan_bwd(ds_re, ds_im, s_re, s_im, ab_re, ab_im)
    dproj = mm("mm_bu_dx", NT, [(Op(l_re), Op(bb_re), S5_NS, 1), (Op(l_im), Op(bb_im), S5_NS, 1)],
               seq, BW, 512, BW, add=Op(du1), out=dproj, out_col=C_S5U)
    dbb = [mm(f"mm_bu_dw{n}", TN, [(Op(proj, 0, C_S5U), Op(l), seq, 1)], BW, S5_NS, 256, 1024)
           for n, l in (("re", l_re), ("im", l_im))]
    gouts = [acc((1, S5_NS)), acc((1, S5_NS)), acc((1, LANES))] + [acc((S5_P, S5_NS))] * 2
    res = block_bwd("s5_prep_bwd", f_s5_prep, 1, _s5_prep_args(p),
                    [_param(v) for v in (da_re, da_im, dbb[0], dbb[1])], gouts, [R(k, k, acc=True) for k in range(5)])
    g["s5_lambda_re"], g["s5_lambda_im"] = res[0].reshape(S5_GROUPS, S5_N), res[1].reshape(S5_GROUPS, S5_N)
    g["s5_log_step"] = res[2][0, :S5_GROUPS]
    b_natural = lambda b: jnp.transpose(b.reshape(S5_P, S5_GROUPS, S5_N), (1, 2, 0))
    c_natural = lambda c: jnp.transpose(c.reshape(S5_P, S5_GROUPS, S5_N), (1, 0, 2))
    g["s5_b_re"], g["s5_b_im"] = b_natural(res[3]), b_natural(res[4])
    res = block_bwd("s5_c_bwd", f_s5_c, 1, _s5_c_args(p), [_param(dc_re), _param(dc_im)],
                    [acc((S5_P, S5_NS))] * 2, [R(0, 0, acc=True), R(1, 1, acc=True)])
    g["s5_c_re"], g["s5_c_im"] = c_natural(res[0]), c_natural(res[1])
    hb, w_in = sv["hb"], p["w_in"]
    dh = mm("mm_in_dx", NT, [(Op(dproj), Op(w_in), PW_MAIN, 1), (Op(d_dt), Op(w_in, 0, PW_MAIN), LANES, 1)],
            seq, D_MODEL, 512, 512)
    dw = mm("mm_in_dw", TN, [(Op(hb), Op(dproj), seq, 1)], D_MODEL, PW_MAIN, 512, 2048, out_dtype=bf16, out_width=PW)
    g["w_in"] = mm("mm_in_dwdt", TN, [(Op(hb), Op(d_dt), seq, 1)], D_MODEL, LANES, 256, LANES, out=dw, out_col=C_DT)
    dx, dnw = block_bwd("rms_in_bwd", f_rms_res, seq // t4, [_tb(sv["x"], t4, D_MODEL, 0), _param(p["norm_w"].reshape(1, D_MODEL))],
                        [_tb(dh, t4, D_MODEL, 0), _tb(d_out, t4, D_MODEL, 0)],
                        [O((seq, D_MODEL), f32, (t4, D_MODEL), lambda i: (i, 0)), acc((1, D_MODEL))],
                        [R(0, 0), R(1, 1, acc=True)])
    g["norm_w"] = dnw.reshape(D_MODEL)
    return dx, g


def loss_head(x, w, target):
    seq = x.shape[0]
    t = 256

    def body(x_ref, w_ref, t_ref, loss_ref, dx_ref, dw_ref):
        step = pl.program_id(0)

        def f(xv, wv):
            err = _rms(xv, wv) - t_ref[...]
            return 0.5 * jnp.sum(jnp.mean(err * err, axis=-1, keepdims=True), axis=0, keepdims=True)

        val, vjp_fn = jax.vjp(f, x_ref[...], w_ref[...])
        dx, dw = vjp_fn(jnp.ones((1, 1), f32))
        dx_ref[...] = dx

        @pl.when(step == 0)
        def _():
            loss_ref[...] = jnp.broadcast_to(val, loss_ref.shape)
            dw_ref[...] = dw

        @pl.when(step > 0)
        def _():
            loss_ref[...] += jnp.broadcast_to(val, loss_ref.shape)
            dw_ref[...] += dw

    blk = pl.BlockSpec((t, D_MODEL), lambda i: (i, 0))
    row = pl.BlockSpec((1, D_MODEL), lambda i: (0, 0))
    return pl.pallas_call(
        body, name="loss_head", grid=(seq // t,),
        in_specs=[blk, row, blk],
        out_specs=[pl.BlockSpec((1, LANES), lambda i: (0, 0)), blk, row],
        out_shape=[jax.ShapeDtypeStruct((1, LANES), f32), jax.ShapeDtypeStruct((seq, D_MODEL), f32),
                   jax.ShapeDtypeStruct((1, D_MODEL), f32)],
        compiler_params=_cparams(1),
    )(x, w.reshape(1, D_MODEL), target)


LAYER_KEYS = ("norm_w", "w_in", "s5_lambda_re", "s5_lambda_im", "s5_b_re", "s5_b_im", "s5_c_re", "s5_c_im", "s5_d",
              "s5_log_step", "s5_w_glu", "sgu_ln_w", "sgu_ln_b", "sgu_w", "sgu_b", "m2_conv_w", "m2_conv_b",
              "m2_dt_bias", "m2_a_log", "m2_d", "m2_norm_w", "sc_conv_w", "merge_b", "w_branch", "w_out")


def local_step(x, target, layers, final_norm_w):
    saved = []
    for p in layers:
        x, sv = layer_fwd(x, p)
        saved.append(sv)
    loss, dx, dfw = loss_head(x, final_norm_w, target)
    grads = []
    for p, sv in zip(reversed(layers), reversed(saved)):
        dx, g = layer_bwd(dx, p, sv)
        grads.append(g)
    return loss[0, 0], dx, grads[::-1], dfw.reshape(D_MODEL)


MESH = pl.DeviceIdType.MESH
ANY = pl.BlockSpec(memory_space=pl.ANY)


def _me():
    return lax.axis_index("x"), lax.axis_index("y"), lax.axis_index("c")


def _other_chips(x, y):
    return [(1 - x, y), (x, 1 - y), (1 - x, 1 - y)]


def _rcopy(src, dst, send, recv, dev):
    return pltpu.make_async_remote_copy(src_ref=src, dst_ref=dst, send_sem=send, recv_sem=recv,
                                        device_id=dev, device_id_type=MESH)


def _route_cut(rows, dtype):
    tile = 2 * SUBL * (4 // jnp.dtype(dtype).itemsize)
    return rows // 2 if rows % tile == 0 else rows


def _comm_call(name, body, arrs, out_shape, n_remote, aliases=None):
    n = len(arrs)
    return pl.pallas_call(
        body, name=name, in_specs=[ANY] * n, out_specs=[ANY] * len(out_shape), out_shape=out_shape,
        scratch_shapes=[pltpu.SemaphoreType.DMA((n, n_remote)), pltpu.SemaphoreType.DMA((n, n_remote))],
        input_output_aliases=aliases or {},
        compiler_params=pltpu.CompilerParams(has_side_effects=True),
    )(*arrs)


def gather_chips(name, arrs):
    n = len(arrs)
    cut = [_route_cut(a.shape[1], a.dtype) for a in arrs]

    def body(*refs):
        ins, outs = refs[:n], refs[n:2 * n]
        send, recv = refs[2 * n:]
        x, y, c = _me()
        jme, jx, jy, jd = 2 * x + y, 2 * (1 - x) + y, 2 * x + 1 - y, 2 * (1 - x) + 1 - y
        to_x, to_y, sib = (1 - x, y, c), (x, 1 - y, c), (x, y, 1 - c)

        def part(ref, a, hi):
            return ref.at[pl.ds(cut[a], ref.shape[0] - cut[a])] if hi else ref.at[pl.ds(0, cut[a])]

        def cp(a, k, ref, dev):
            return _rcopy(ref, ref, send.at[a, k], recv.at[a, k], dev)

        split = [a for a in range(n) if cut[a] < arrs[a].shape[1]]
        sent = [_rcopy(ins[a].at[c], outs[a].at[jme, c], send.at[a, k], recv.at[a, k], dev)
                for a in range(n) for k, dev in ((0, to_x), (1, to_y))]
        for s in sent:
            s.start()
        for a in range(n):
            blk = outs[a].at[jx, c]
            cp(a, 0, blk, to_x).wait_recv()
            sent += [cp(a, 2, part(blk, a, False), to_y), cp(a, 4, blk, sib)]
            sent[-2].start()
            sent[-1].start()
        for a in range(n):
            blk = outs[a].at[jy, c]
            cp(a, 1, blk, to_y).wait_recv()
            sent.append(cp(a, 5, blk, sib))
            sent[-1].start()
            if a in split:
                sent.append(cp(a, 3, part(blk, a, True), to_x))
                sent[-1].start()
        for a in range(n):
            lo = part(outs[a].at[jd, c], a, False)
            cp(a, 2, lo, to_y).wait_recv()
            sent.append(cp(a, 6, lo, sib))
            sent[-1].start()
        for a in split:
            hi = part(outs[a].at[jd, c], a, True)
            cp(a, 3, hi, to_x).wait_recv()
            sent.append(cp(a, 7, hi, sib))
            sent[-1].start()
        for a in range(n):
            cp(a, 4, outs[a].at[jx, 1 - c], sib).wait_recv()
            cp(a, 5, outs[a].at[jy, 1 - c], sib).wait_recv()
            cp(a, 6, part(outs[a].at[jd, 1 - c], a, False), sib).wait_recv()
        for a in split:
            cp(a, 7, part(outs[a].at[jd, 1 - c], a, True), sib).wait_recv()
        for s in sent:
            s.wait_send()

    out_shape = [jax.ShapeDtypeStruct((4,) + a.shape, a.dtype) for a in arrs]
    got = _comm_call(name, body, arrs, out_shape, 8)
    jme = 2 * lax.axis_index("x") + lax.axis_index("y")
    return [lax.dynamic_update_index_in_dim(g, a, jme, 0) for g, a in zip(got, arrs)]


def swap_halves(name, arrs):
    n = len(arrs)

    def body(*refs):
        ins, outs = refs[:n], refs[n:2 * n]
        send, recv = refs[2 * n:]
        x, y, c = _me()
        remote = [_rcopy(ins[a].at[1 - c], outs[a], send.at[a, 0], recv.at[a, 0], (x, y, 1 - c)) for a in range(n)]
        for cp in remote:
            cp.start()
        for cp in remote:
            cp.wait()

    return _comm_call(name, body, arrs, [jax.ShapeDtypeStruct(a.shape[1:], a.dtype) for a in arrs], 1)


def exchange_chips(name, arrs):
    n = len(arrs)

    def body(*refs):
        ins, outs = refs[:n], refs[n:2 * n]
        send, recv = refs[2 * n:]
        x, y, c = _me()
        remote = [_rcopy(ins[a].at[2 * cx + cy], outs[a].at[k], send.at[a, k], recv.at[a, k], (cx, cy, c))
                  for a in range(n) for k, (cx, cy) in enumerate(_other_chips(x, y))]
        for cp in remote:
            cp.start()
        for cp in remote:
            cp.wait()

    return _comm_call(name, body, arrs, [jax.ShapeDtypeStruct((3,) + a.shape[1:], a.dtype) for a in arrs], 3)


def gather_cores(name, arrs):
    n = len(arrs)

    def body(*refs):
        bufs = refs[n:2 * n]
        send, recv = refs[2 * n:]
        x, y, c = _me()
        remote = [_rcopy(bufs[a].at[c], bufs[a].at[c], send.at[a, 0], recv.at[a, 0], (x, y, 1 - c)) for a in range(n)]
        for cp in remote:
            cp.start()
        for a in range(n):
            _rcopy(bufs[a].at[1 - c], bufs[a].at[1 - c], send.at[a, 0], recv.at[a, 0], (x, y, 1 - c)).wait_recv()
        for cp in remote:
            cp.wait_send()

    return _comm_call(name, body, arrs, [jax.ShapeDtypeStruct(a.shape, a.dtype) for a in arrs], 1,
                      aliases={a: a for a in range(n)})


ROW_BLOCK = 512


def esum(name, terms, rows, width, out_dtype, out_slots=None):
    tr = next((t for t in range(min(rows, ROW_BLOCK), 0, -SUBL) if rows % t == 0 and t % SUBL == 0), rows)
    where =jnp.stack([lax.axis_index("c"), 2 * lax.axis_index("x") + lax.axis_index("y")]).astype(jnp.int32)
    pick = {"c": 0, "j": 1}

    def body(s_ref, *refs):
        acc = refs[0][...].astype(f32)
        for r in refs[1:-1]:
            acc = acc + r[...].astype(f32)
        refs[-1][...] = acc.astype(out_dtype)

    specs = []
    for arr, lead in terms:
        if lead is None:
            specs.append(pl.BlockSpec((tr, width), lambda i, s: (i, 0)))
        elif isinstance(lead, str):
            specs.append(pl.BlockSpec((None, tr, width), lambda i, s, lead=lead: (s[pick[lead]], i, 0)))
        else:
            specs.append(pl.BlockSpec((None, tr, width), lambda i, s, lead=lead: (lead, i, 0)))
    if out_slots is None:
        out_spec = pl.BlockSpec((tr, width), lambda i, s: (i, 0))
        out_shape = jax.ShapeDtypeStruct((rows, width), out_dtype)
    else:
        out_spec = pl.BlockSpec((None, tr, width), lambda i, s: (s[0], i, 0))
        out_shape = jax.ShapeDtypeStruct((out_slots, rows, width), out_dtype)
    return pl.pallas_call(
        body, name=name,
        grid_spec=pltpu.PrefetchScalarGridSpec(num_scalar_prefetch=1, grid=(rows // tr,), in_specs=specs, out_specs=out_spec),
        out_shape=out_shape, compiler_params=_cparams(1),
    )(where, *[t[0] for t in terms])


def reduce_to_shards(parts):
    tags = [str(k) for k in range(len(parts))]
    theirs = swap_halves("rs_swap", parts)
    t1 = []
    for tag, p, th in zip(tags, parts, theirs):
        _, _, h, w = p.shape
        t1.append(esum("rs_add_cores" + tag, [(p.reshape(2, 4 * h, w), "c"), (th.reshape(4 * h, w), None)],
                       4 * h, w, p.dtype).reshape(4, h, w))
    landed = exchange_chips("rs_exchange", t1)
    red = []
    for tag, p, t, got in zip(tags, parts, t1, landed):
        _, _, h, w = p.shape
        red.append(esum("rs_add_chips" + tag, [(t, "j"), (got, 0), (got, 1), (got, 2)], h, w, f32, out_slots=2))
    return gather_cores("rs_gather", red)


def _adamw_step(w_ref, g_ref, m_ref, v_ref, d_ref, nm_ref, nv_ref):
    gv = g_ref[...]
    nm = ADAM_B1 * m_ref[...] + (1.0 - ADAM_B1) * gv
    nv = ADAM_B2 * v_ref[...] + (1.0 - ADAM_B2) * jnp.square(gv)
    m_hat = nm / (1.0 - ADAM_B1 ** ADAM_STEP)
    v_hat = nv / (1.0 - ADAM_B2 ** ADAM_STEP)
    d_ref[...] = -ADAM_LR * (m_hat / (jnp.sqrt(v_hat) + ADAM_EPS) + ADAM_WD * w_ref[...])
    nm_ref[...] = nm
    nv_ref[...] = nv


def adamw(name, w, g, m, v, tr=None):
    rows, rest = w.shape[0], w.shape[1:]
    if tr is None:
        tr = ROW_BLOCK if rows % ROW_BLOCK == 0 else rows
    assert rows % tr == 0

    def body(*refs):
        _adamw_step(*refs)

    spec = pl.BlockSpec((tr,) + rest, lambda i: (i,) + (0,) * len(rest))
    return pl.pallas_call(
        body, name=name, grid=(rows // tr,), in_specs=[spec] * 4, out_specs=[spec] * 3,
        out_shape=[jax.ShapeDtypeStruct(w.shape, f32)] * 3, compiler_params=_cparams(1),
    )(w, g, m, v)


def adamw_many(name, ws, gs, ms, vs):
    n = len(ws)

    def body(*refs):
        ins, outs = refs[:4 * n], refs[4 * n:]
        for k in range(n):
            _adamw_step(ins[k], ins[n + k], ins[2 * n + k], ins[3 * n + k], outs[k], outs[n + k], outs[2 * n + k])

    vmem = pl.BlockSpec(memory_space=pltpu.VMEM)
    res = pl.pallas_call(
        body, name=name, in_specs=[vmem] * (4 * n), out_specs=[vmem] * (3 * n),
        out_shape=[jax.ShapeDtypeStruct(w.shape, f32) for w in ws] * 3,
        compiler_params=pltpu.CompilerParams(vmem_limit_bytes=VMEM_LIMIT),
    )(*ws, *gs, *ms, *vs)
    return res[:n], res[n:2 * n], res[2 * n:]


PACK_W = 1024


def _pack(parts, halves, row_mult):
    flat = jnp.concatenate([p.reshape(-1) for p in parts])
    per = halves * row_mult * PACK_W
    total = -(-flat.size // per) * per
    flat = jnp.pad(flat, (0, total - flat.size))
    return flat.reshape(halves, total // (halves * PACK_W), PACK_W)


def _unpack(flat, shapes):
    out, pos = [], 0
    for s in shapes:
        n = int(np.prod(s))
        out.append(flat[pos:pos + n].reshape(s))
        pos += n
    return out


SHARDED_BIG = ("w_in", "w_branch", "w_out", "s5_w_glu")
SHARDED_SMALL = ("m2_conv_w", "sc_conv_w", "merge_b")
SHARD_AXIS = {"w_in": 2, "w_branch": 3, "w_out": 1, "s5_w_glu": 1, "m2_conv_w": 2, "sc_conv_w": 2, "merge_b": 2}
REPLICATED = ("norm_w", "s5_lambda_re", "s5_lambda_im", "s5_b_re", "s5_b_im", "s5_c_re", "s5_c_im", "s5_d", "s5_log_step",
              "sgu_ln_w", "sgu_ln_b", "sgu_w", "sgu_b", "m2_conv_b", "m2_dt_bias", "m2_a_log", "m2_d", "m2_norm_w")
WEIGHTS = ("norm_w", "w_in", "s5_lambda_re", "s5_lambda_im", "s5_b_re", "s5_b_im", "s5_c_re", "s5_c_im", "s5_d",
           "s5_log_step", "s5_w_glu", "sgu_ln_w", "sgu_ln_b", "sgu_w", "sgu_b", "m2_conv_w", "m2_conv_b", "m2_dt_bias",
           "m2_a_log", "m2_d", "m2_norm_w", "sc_conv_w", "merge_b", "w_branch", "w_out", "final_norm_w")
N_LAYERS = 2


SHARD_W = IN_DIM // 4
SHARD_PAD = -(-SHARD_W // LANES) * LANES
REGROUP_W = 3 * LANES


def _kernel_pieces():
    out, pos = [], 0
    for s, n in _col_segments():
        while n:
            take = min(n, SHARD_W - s % SHARD_W)
            out.append((pos, s, take))
            pos, s, n = pos + take, s + take, n - take
    return out


def regroup_cols(name, src, steps, out_shape, out=None):
    n_src, rows, width = src.shape
    ow, win = REGROUP_W, REGROUP_W + LANES
    k_max = max(len(p) for _, _, p in steps)
    assert width % LANES == 0

    def body(*refs):
        src_ref, out_ref = refs[0], refs[-5]
        wbuf, obuf, sem_in, sem_out = refs[-4:]

        def fetch(q, slot):
            started = []
            for p, (s, col, lo, hi) in enumerate(steps[q][2]):
                w0 = col // LANES * LANES
                wlen = min(win, width - w0)
                cp = pltpu.make_async_copy(src_ref.at[s, :, pl.ds(w0, wlen)], wbuf.at[slot, p, :, pl.ds(0, wlen)], sem_in.at[slot, p])
                cp.start()
                started.append((cp, wlen, col - w0 - lo, lo, hi))
            return started

        pend, writes = fetch(0, 0), [None, None]
        for q, (t, b, _) in enumerate(steps):
            slot = q % 2
            nxt = fetch(q + 1, 1 - slot) if q + 1 < len(steps) else []
            acc = [jnp.zeros((rows, LANES), f32) for _ in range(ow // LANES)]
            for p, (cp, wlen, shift, lo, hi) in enumerate(pend):
                cp.wait()
                for k in range(ow // LANES):
                    c_lo, c_hi = max(lo, k * LANES), min(hi, (k + 1) * LANES)
                    if c_lo >= c_hi:
                        continue
                    wb = (c_lo + shift) // LANES
                    alen = min(2 * LANES, wlen - wb * LANES)
                    a = wbuf[slot, p, :, wb * LANES:wb * LANES + alen]
                    r, c = _rows((alen, LANES)), _cols((alen, LANES))
                    sh = shift + (k - wb) * LANES
                    sel = (r == c + sh) & (c >= c_lo - k * LANES) & (c < c_hi - k * LANES)
                    acc[k] = acc[k] + jnp.dot(a, sel.astype(bf16), preferred_element_type=f32)
            if writes[slot] is not None:
                writes[slot].wait()
            for k in range(ow // LANES):
                obuf[slot, :, k * LANES:(k + 1) * LANES] = acc[k].astype(bf16)
            writes[slot] = pltpu.make_async_copy(obuf.at[slot], out_ref.at[t, :, pl.ds(b * ow, ow)], sem_out.at[slot])
            writes[slot].start()
            pend = nxt
        for wr in writes:
            if wr is not None:
                wr.wait()

    operands, io_alias = [src], {}
    if out is not None:
        operands.append(out)
        io_alias = {1: 0}
    return pl.pallas_call(
        body, name=name, in_specs=[ANY] * len(operands), out_specs=ANY,
        out_shape=jax.ShapeDtypeStruct(out_shape, bf16), input_output_aliases=io_alias,
        scratch_shapes=[pltpu.VMEM((2, k_max, rows, win), bf16), pltpu.VMEM((2, rows, ow), bf16),
                        pltpu.SemaphoreType.DMA((2, k_max)), pltpu.SemaphoreType.DMA((2,))],
        compiler_params=pltpu.CompilerParams(vmem_limit_bytes=VMEM_LIMIT),
    )(*operands)


def _steps_to_kernel_cols(layer):
    steps = []
    for o in range(PW // REGROUP_W):
        pieces = []
        for pos, s, n in _kernel_pieces():
            lo, hi = max(pos, o * REGROUP_W), min(pos + n, (o + 1) * REGROUP_W)
            if lo < hi:
                ref_col = s + lo - pos
                pieces.append(((ref_col // SHARD_W) * N_LAYERS + layer, ref_col % SHARD_W, lo - o * REGROUP_W, hi - o * REGROUP_W))
        steps.append((0, o, pieces))
    return steps


def _steps_to_shards(layer):
    steps = []
    for j in range(4):
        for b in range(SHARD_PAD // REGROUP_W):
            start, stop = j * SHARD_W + b * REGROUP_W, min(j * SHARD_W + (b + 1) * REGROUP_W, (j + 1) * SHARD_W)
            pieces = []
            for pos, s, n in _kernel_pieces():
                lo, hi = max(s, start), min(s + n, stop)
                if lo < hi:
                    pieces.append((0, pos + lo - s, lo - start, hi - start))
            steps.append((layer * 4 + j, b, pieces))
    return steps


def _gather_weights(w):
    w_in = jnp.pad(w["w_in"].astype(bf16), ((0, 0), (0, 0), (0, SHARD_PAD - SHARD_W)))
    arrs = [w_in, w["w_branch"].reshape(N_LAYERS, N_BRANCH * BW, -1).astype(bf16),
            w["w_out"].astype(bf16), w["s5_w_glu"].astype(bf16), w["m2_conv_w"], w["sc_conv_w"], w["merge_b"]]
    got = gather_chips("ag_weights", arrs)
    cols = lambda t: jnp.transpose(t, (1, 0, 2)).reshape(t.shape[1], -1)
    layers = []
    for i in range(N_LAYERS):
        p = {k: w[k][i] for k in REPLICATED}
        p["w_in"] = regroup_cols(f"w_in_cols{i}", got[0].reshape(4 * N_LAYERS, D_MODEL, SHARD_PAD), _steps_to_kernel_cols(i),
                                 (1, D_MODEL, PW))[0]
        p["w_branch"] = cols(got[1][:, i])
        p["w_out"] = got[2][:, i].reshape(D_MODEL, D_MODEL)
        p["s5_w_glu"] = got[3][:, i].reshape(BW, BW)
        p["m2_conv_w"], p["sc_conv_w"], p["merge_b"] = cols(got[4][:, i]), cols(got[5][:, i]), cols(got[6][:, i])
        layers.append(p)
    return layers


def _reduce_grads(grads, d_final, w):
    to_chips = lambda t: jnp.transpose(t.reshape(t.shape[0], 4, -1), (1, 0, 2))
    stack = lambda f: jnp.stack([f(g) for g in grads])
    dw_in = None
    for i, g in enumerate(grads):
        dw_in = regroup_cols(f"w_in_shards{i}", g["w_in"][None], _steps_to_shards(i), (4 * N_LAYERS, D_MODEL, SHARD_PAD), out=dw_in)
    parts = [dw_in.reshape(N_LAYERS, 4, D_MODEL, SHARD_PAD),
             stack(lambda g: to_chips(g["w_branch"].reshape(N_BRANCH * BW, D_MODEL)).astype(bf16)),
             stack(lambda g: g["w_out"].reshape(4, D_MODEL // 4, D_MODEL).astype(bf16)),
             stack(lambda g: g["s5_w_glu"].reshape(4, BW // 4, BW).astype(bf16))]
    rep = jnp.concatenate([stack(lambda g: g[k]).reshape(-1) for k in REPLICATED] + [d_final.reshape(-1)])
    quarter = -(-rep.size // (4 * 2 * SUBL * PACK_W)) * (2 * SUBL * PACK_W)
    rep = jnp.pad(rep, (0, 4 * quarter - rep.size))
    small = []
    for j in range(4):
        sharded = [stack(lambda g: to_chips(g[k])[j]) for k in SHARDED_SMALL]
        small.append(_pack(sharded + [rep[j * quarter:(j + 1) * quarter]], 2, SUBL))
    parts.append(jnp.stack(small, axis=1))
    red = reduce_to_shards(parts)
    out = {"w_in": red[0][:, :, :SHARD_W], "w_branch": red[1].reshape(w["w_branch"].shape), "w_out": red[2], "s5_w_glu": red[3]}
    small_flat = red[4].reshape(-1)
    n_small = sum(int(np.prod(w[k].shape)) for k in SHARDED_SMALL)
    out.update(zip(SHARDED_SMALL, _unpack(small_flat, [w[k].shape for k in SHARDED_SMALL])))
    mine = small_flat[n_small:n_small + quarter].reshape(2, quarter // (2 * PACK_W), PACK_W)
    rep_all = gather_chips("ag_small_grads", [mine])[0].reshape(-1)
    names = REPLICATED + ("final_norm_w",)
    out.update(zip(names, _unpack(rep_all, [w[k].shape for k in names])))
    return out


def _update(w, g, m, v):
    d, nm, nv = {}, {}, {}
    flat2 = lambda a: a.reshape(-1, a.shape[-1])
    cols_major = lambda a: jnp.transpose(a, (2, 0, 1))
    res = adamw("adamw_w_in", *[cols_major(t["w_in"]) for t in (w, g, m, v)], tr=SHARD_W // 14)
    d["w_in"], nm["w_in"], nv["w_in"] = (jnp.transpose(r, (1, 2, 0)) for r in res)
    for k in SHARDED_BIG[1:]:
        res = adamw("adamw_" + k, *[flat2(t[k]) for t in (w, g, m, v)])
        d[k], nm[k], nv[k] = (r.reshape(w[k].shape) for r in res)
    for k in ("s5_b_re", "s5_b_im"):
        res = adamw("adamw_" + k, *[flat2(t[k]) for t in (w, g, m, v)])
        d[k], nm[k], nv[k] = (r.reshape(w[k].shape) for r in res)
    rest = [k for k in WEIGHTS if k not in SHARDED_BIG + ("s5_b_re", "s5_b_im")]
    two_d = lambda a: a.reshape(1, -1) if a.ndim == 1 else a
    res = adamw_many("adamw_rest", *[[two_d(t[k]) for k in rest] for t in (w, g, m, v)])
    for tgt, rs in zip((d, nm, nv), res):
        tgt.update({k: r.reshape(w[k].shape) for k, r in zip(rest, rs)})
    return d, nm, nv


def kernel(x, norm_w, w_in, s5_lambda_re, s5_lambda_im, s5_b_re, s5_b_im, s5_c_re, s5_c_im, s5_d, s5_log_step, s5_w_glu, sgu_ln_w, sgu_ln_b, sgu_w, sgu_b, m2_conv_w, m2_conv_b, m2_dt_bias, m2_a_log, m2_d, m2_norm_w, sc_conv_w, merge_b, w_branch, w_out, final_norm_w, loss_target, m_norm_w, m_w_in, m_s5_lambda_re, m_s5_lambda_im, m_s5_b_re, m_s5_b_im, m_s5_c_re, m_s5_c_im, m_s5_d, m_s5_log_step, m_s5_w_glu, m_sgu_ln_w, m_sgu_ln_b, m_sgu_w, m_sgu_b, m_m2_conv_w, m_m2_conv_b, m_m2_dt_bias, m_m2_a_log, m_m2_d, m_m2_norm_w, m_sc_conv_w, m_merge_b, m_w_branch, m_w_out, m_final_norm_w, v_norm_w, v_w_in, v_s5_lambda_re, v_s5_lambda_im, v_s5_b_re, v_s5_b_im, v_s5_c_re, v_s5_c_im, v_s5_d, v_s5_log_step, v_s5_w_glu, v_sgu_ln_w, v_sgu_ln_b, v_sgu_w, v_sgu_b, v_m2_conv_w, v_m2_conv_b, v_m2_dt_bias, v_m2_a_log, v_m2_d, v_m2_norm_w, v_sc_conv_w, v_merge_b, v_w_branch, v_w_out, v_final_norm_w):
    given = dict(locals())
    w = {k: given[k] for k in WEIGHTS}
    m = {k: given["m_" + k] for k in WEIGHTS}
    v = {k: given["v_" + k] for k in WEIGHTS}
    layers = _gather_weights(w)
    loss, dx, grads, d_final = local_step(x[0], loss_target[0], layers, final_norm_w)
    loss = lax.psum(loss, ("x", "y", "c"))
    g = _reduce_grads(grads, d_final, w)
    d, nm, nv = _update(w, g, m, v)
    return (loss, dx[None], *[g[k] for k in WEIGHTS], *[d[k] for k in WEIGHTS],
            *[nm[k] for k in WEIGHTS], *[nv[k] for k in WEIGHTS])
```

```python
import functools
from typing import Any, Callable, NamedTuple

import numpy as np
import jax
import jax.numpy as jnp
from jax import lax
from jax.experimental import pallas as pl
from jax.experimental.pallas import tpu as pltpu

f32 = jnp.float32
bf16 = jnp.bfloat16

D_MODEL = 1024
BW = 512
N_BRANCH = 4
EPS = 1e-6
S5_GROUPS, S5_P, S5_N = 32, 16, 64
S5_NS = S5_GROUPS * S5_N
CHUNK = 128
M2_HEADS, M2_HEAD_DIM, M2_GROUPS, M2_STATE = 8, 64, 2, 128
IN_DIM = 10248
PW = 10368
PW_MAIN = 10240
LANES = 128
VMEM_LIMIT = 60 * 1024 * 1024

ADAM_LR, ADAM_B1, ADAM_B2, ADAM_EPS, ADAM_WD, ADAM_STEP = 0.001, 0.9, 0.999, 1e-08, 0.01, 10

C_MERGE = 0
C_SC = 4096
C_SGU = 6144
C_S5G = 7680
C_S5U = 8192
C_M2Z = 8704
C_XBC = 9216
C_DT = 10240


def _col_segments():
    segs = [(6152, 4096)]
    for j in range(4):
        segs += [(4104 + 128 * j, 128), (4616 + 128 * j, 128), (5128 + 128 * j, 128), (5640 + 128 * j, 128)]
    segs += [(1024, 1536), (512, 512), (0, 512), (2560, 512), (3072, 1024), (4096, 8)]
    return segs


NN = ((1,), (0,))
NT = ((1,), (1,))
TN = ((0,), (0,))


def _bd(a, b, dims):
    return lax.dot_general(a.astype(bf16), b.astype(bf16), (dims, ((), ())), preferred_element_type=f32)


def _hd(a, b, dims):
    return lax.dot_general(a, b, (dims, ((), ())), precision=lax.Precision.HIGHEST, preferred_element_type=f32)


def _make_dots(raw):
    @jax.custom_vjp
    def nn(a, b):
        return raw(a, b, NN)
    nn.defvjp(lambda a, b: (raw(a, b, NN), (a, b)), lambda r, g: (raw(g, r[1], NT), raw(r[0], g, TN)))

    @jax.custom_vjp
    def nt(a, b):
        return raw(a, b, NT)
    nt.defvjp(lambda a, b: (raw(a, b, NT), (a, b)), lambda r, g: (raw(g, r[1], NN), raw(g, r[0], TN)))

    @jax.custom_vjp
    def tn(a, b):
        return raw(a, b, TN)
    tn.defvjp(lambda a, b: (raw(a, b, TN), (a, b)), lambda r, g: (raw(r[1], g, NT), raw(r[0], g, NN)))
    return nn, nt, tn


bdot, bdot_nt, bdot_tn = _make_dots(_bd)
hdot, hdot_nt, hdot_tn = _make_dots(_hd)


@jax.custom_vjp
def bdot_w(a, w, shadow):
    return _bd(a, w, NN)


bdot_w.defvjp(lambda a, w, s: (_bd(a, w, NN), (a, w)),
              lambda r, g: (_bd(g, r[1], NT), jnp.zeros_like(r[1]), _bd(r[0], g, TN)))


def _rows(shape):
    return lax.broadcasted_iota(jnp.int32, shape, 0)


def _cols(shape):
    return lax.broadcasted_iota(jnp.int32, shape, 1)


def _shift_down(x, s):
    return jnp.where(_rows(x.shape) < s, 0.0, pltpu.roll(x, s, 0))


def _shift_up(x, s):
    n = x.shape[0]
    return jnp.where(_rows(x.shape) >= n - s, 0.0, pltpu.roll(x, n - s, 0))


@functools.partial(jax.custom_vjp, nondiff_argnums=(1,))
def shift(x, s):
    return _shift_down(x, s) if s else x


shift.defvjp(lambda x, s: (shift(x, s), None), lambda s, _, g: (_shift_up(g, s) if s else g,))


def _row_of(w, k):
    return jnp.sum(jnp.where(_rows(w.shape) == k, w, 0.0), axis=0, keepdims=True)


def _lane_mask(width, lo, hi):
    c = _cols((1, width))
    return ((c >= lo) & (c < hi)).astype(f32)


def _expand(rows, width, per):
    return (_cols((rows, width)) // per == _rows((rows, width))).astype(f32)


class A(NamedTuple):
    arr: Any
    block: tuple
    imap: Callable
    shadow: bool = False


class O(NamedTuple):
    shape: tuple
    dtype: Any
    block: tuple
    imap: Callable
    alias: Any = None


class R(NamedTuple):
    arg: int
    out: int
    off: Any = None
    acc: bool = False


def _cparams(n_grid):
    return pltpu.CompilerParams(dimension_semantics=("arbitrary",) * n_grid, vmem_limit_bytes=VMEM_LIMIT)


def _ispec(block, imap, n, rev):
    if rev:
        return pl.BlockSpec(block, lambda i: imap(n - 1 - i))
    return pl.BlockSpec(block, imap)


def _load(ref, a):
    v = ref[...]
    if a.shadow:
        return (v, jnp.zeros(v.shape, f32))
    return v.astype(f32)


def _save_spec(shape, n, rev):
    nd = len(shape)
    return _ispec((None,) + tuple(shape), lambda i: (i,) + (0,) * nd, n, rev)


def block_fwd(name, f, n, args, outs, carries=()):
    n_in, n_out, n_c = len(args), len(outs), len(carries)

    def body(*refs):
        ins, out_r = refs[:n_in], refs[n_in:n_in + n_out]
        saves, cs = refs[n_in + n_out:n_in + n_out + n_c], refs[n_in + n_out + n_c:]
        if n_c:
            @pl.when(pl.program_id(0) == 0)
            def _():
                for c in cs:
                    c[...] = jnp.zeros(c.shape, f32)
        vals = [_load(r, a) for r, a in zip(ins, args)]
        cv = [c[...] for c in cs]
        for s, v in zip(saves, cv):
            s[...] = v
        res = f(*vals, *cv)
        for r, v in zip(out_r, res[:n_out]):
            r[...] = v.astype(r.dtype)
        for c, v in zip(cs, res[n_out:]):
            c[...] = v

    out_shape = [jax.ShapeDtypeStruct(o.shape, o.dtype) for o in outs]
    out_specs = [pl.BlockSpec(o.block, o.imap) for o in outs]
    for shp in carries:
        out_shape.append(jax.ShapeDtypeStruct((n,) + tuple(shp), f32))
        out_specs.append(_save_spec(shp, n, False))
    return pl.pallas_call(
        body, name=name, grid=(n,),
        in_specs=[pl.BlockSpec(a.block, a.imap) for a in args],
        out_specs=out_specs, out_shape=out_shape,
        scratch_shapes=[pltpu.VMEM(tuple(shp), f32) for shp in carries],
        compiler_params=_cparams(1),
    )(*[a.arr for a in args])


def block_bwd(name, f, n, args, cots, gouts, routes, saved=(), rev=False):
    n_in, n_cot, n_c, n_go = len(args), len(cots), len(saved), len(gouts)
    diff = []
    for r in routes:
        if r.arg not in diff:
            diff.append(r.arg)
    aliases = [(k, o.alias) for k, o in enumerate(gouts) if o.alias is not None]

    def body(*refs):
        ins = refs[:n_in]
        cot_r = refs[n_in:n_in + n_cot]
        sav_r = refs[n_in + n_cot:n_in + n_cot + n_c]
        base = n_in + n_cot + n_c + len(aliases)
        go_r = refs[base:base + n_go]
        dcs = refs[base + n_go:]
        step = pl.program_id(0)
        if n_c:
            @pl.when(step == 0)
            def _():
                for d in dcs:
                    d[...] = jnp.zeros(d.shape, f32)
        vals = [_load(r, a) for r, a in zip(ins, args)]
        cv = [s[...] for s in sav_r]
        nd = len(diff)

        def g(*dv):
            full = list(vals)
            for idx, v in zip(diff, dv[:nd]):
                full[idx] = (vals[idx][0], v) if args[idx].shadow else v
            return tuple(f(*full, *dv[nd:]))

        primals = [vals[i][1] if args[i].shadow else vals[i] for i in diff] + cv
        _, vjp_fn = jax.vjp(g, *primals)
        ct = tuple([r[...].astype(f32) for r in cot_r] + [d[...] for d in dcs])
        grads = vjp_fn(ct)
        for r in routes:
            gr = grads[diff.index(r.arg)]
            ref = go_r[r.out]
            if r.acc:
                @pl.when(step == 0)
                def _(ref=ref, gr=gr):
                    ref[...] = gr.astype(ref.dtype)

                @pl.when(step > 0)
                def _(ref=ref, gr=gr):
                    ref[...] += gr.astype(ref.dtype)
            elif r.off is None:
                ref[...] = gr.astype(ref.dtype)
            else:
                ref[:, r.off:r.off + gr.shape[1]] = gr.astype(ref.dtype)
        for d, gr in zip(dcs, grads[nd:]):
            d[...] = gr

    in_specs = [_ispec(a.block, a.imap, n, rev) for a in list(args) + list(cots)]
    in_specs += [_save_spec(s.shape[1:], n, rev) for s in saved]
    in_specs += [pl.BlockSpec(memory_space=pl.ANY) for _ in aliases]
    operands = [a.arr for a in list(args) + list(cots)] + list(saved) + [arr for _, arr in aliases]
    io_alias = {n_in + n_cot + n_c + j: k for j, (k, _) in enumerate(aliases)}
    return pl.pallas_call(
        body, name=name, grid=(n,),
        in_specs=in_specs,
        out_specs=[_ispec(o.block, o.imap, n, rev) for o in gouts],
        out_shape=[jax.ShapeDtypeStruct(o.shape, o.dtype) for o in gouts],
        scratch_shapes=[pltpu.VMEM(tuple(s.shape[1:]), f32) for s in saved],
        input_output_aliases=io_alias,
        compiler_params=_cparams(1),
    )(*operands)


class Op(NamedTuple):
    arr: Any
    row: int = 0
    col: int = 0


def mm(name, mode, pairs, m, n, tm, tn, out_dtype=f32, add=None, out=None, out_col=0, out_width=None):
    tm, tn = min(tm, m), min(tn, n)
    assert m % tm == 0 and n % tn == 0
    in_specs, operands = [], []
    for a, b, k, _ in pairs:
        if mode == TN:
            assert a.row % k == 0 and a.col % tm == 0
            in_specs.append(pl.BlockSpec((k, tm), lambda j, i, a=a, k=k: (a.row // k, i + a.col // tm)))
        else:
            assert a.col % k == 0 and a.row % tm == 0
            in_specs.append(pl.BlockSpec((tm, k), lambda j, i, a=a, k=k: (i + a.row // tm, a.col // k)))
        if mode == NT:
            assert b.col % k == 0 and b.row % tn == 0
            in_specs.append(pl.BlockSpec((tn, k), lambda j, i, b=b, k=k: (j + b.row // tn, b.col // k)))
        else:
            assert b.row % k == 0 and b.col % tn == 0
            in_specs.append(pl.BlockSpec((k, tn), lambda j, i, b=b, k=k: (b.row // k, j + b.col // tn)))
        operands += [a.arr, b.arr]
    n_p = len(pairs)
    if add is not None:
        assert add.col % tn == 0
        in_specs.append(pl.BlockSpec((tm, tn), lambda j, i: (i, j + add.col // tn)))
        operands.append(add.arr)
    io_alias = {}
    if out is not None:
        assert out_col % tn == 0
        in_specs.append(pl.BlockSpec(memory_space=pl.ANY))
        operands.append(out)
        io_alias = {len(operands) - 1: 0}
        out_shape = jax.ShapeDtypeStruct(out.shape, out.dtype)
    else:
        out_shape = jax.ShapeDtypeStruct((m, out_width or n), out_dtype)
    signs = [p[3] for p in pairs]

    def body(*refs):
        o = refs[-1]
        acc = None
        for p in range(n_p):
            t = _bd(refs[2 * p][...], refs[2 * p + 1][...], mode)
            t = t if signs[p] > 0 else -t
            acc = t if acc is None else acc + t
        if add is not None:
            acc = acc + refs[2 * n_p][...].astype(f32)
        o[...] = acc.astype(o.dtype)

    return pl.pallas_call(
        body, name=name, grid=(n // tn, m // tm),
        in_specs=in_specs,
        out_specs=pl.BlockSpec((tm, tn), lambda j, i: (i, j + out_col // tn)),
        out_shape=out_shape, input_output_aliases=io_alias,
        compiler_params=_cparams(2),
    )(*operands)


SCAN_LANES = 512
SUBL = 8


def _cmul(p, q):
    return (p[0] * q[0] - p[1] * q[1], p[0] * q[1] + p[1] * q[0])


def _powers(a):
    a2 = _cmul(a, a)
    a4 = _cmul(a2, a2)
    a6 = _cmul(a4, a2)
    return [a, a2, _cmul(a2, a), a4, _cmul(a4, a), a6, _cmul(a6, a), _cmul(a4, a4)]


def _table(pw, order, w):
    row = _rows((SUBL, w))
    re = sum(jnp.where(row == t, pw[k][0], 0.0) for t, k in enumerate(order))
    im = sum(jnp.where(row == t, pw[k][1], 0.0) for t, k in enumerate(order))
    return re, im


def s5_scan_fwd(bu_re, bu_im, a_re, a_im):
    seq, ns = bu_re.shape
    w, nb = SCAN_LANES, ns // SCAN_LANES

    def body(b_re, b_im, ar, ai, s_re, s_im):
        a = (ar[...], ai[...])
        pw = _powers(a)
        tab = _table(pw, list(range(SUBL)), w)
        row = _rows((SUBL, w))

        def step(i, carry):
            t0 = pl.multiple_of(i * SUBL, SUBL)
            x = (b_re[pl.ds(t0, SUBL), :], b_im[pl.ds(t0, SUBL), :])
            for d, k in ((1, 0), (2, 1), (4, 3)):
                sh = (jnp.where(row < d, 0.0, pltpu.roll(x[0], d, 0)), jnp.where(row < d, 0.0, pltpu.roll(x[1], d, 0)))
                t = _cmul(pw[k], sh)
                x = (x[0] + t[0], x[1] + t[1])
            t = _cmul(tab, carry)
            x = (x[0] + t[0], x[1] + t[1])
            s_re[pl.ds(t0, SUBL), :] = x[0]
            s_im[pl.ds(t0, SUBL), :] = x[1]
            return (x[0][SUBL - 1:, :], x[1][SUBL - 1:, :])

        z = jnp.zeros((1, w), f32)
        lax.fori_loop(0, seq // SUBL, step, (z, z), unroll=2)

    strip = pl.BlockSpec((seq, w), lambda j: (0, j))
    lane = pl.BlockSpec((1, w), lambda j: (0, j))
    return pl.pallas_call(
        body, name="s5_scan_fwd", grid=(nb,),
        in_specs=[strip, strip, lane, lane],
        out_specs=[strip, strip],
        out_shape=[jax.ShapeDtypeStruct((seq, ns), f32)] * 2,
        compiler_params=_cparams(1),
    )(bu_re, bu_im, a_re, a_im)


def s5_scan_bwd(ds_re, ds_im, s_re, s_im, a_re, a_im):
    seq, ns = ds_re.shape
    w, nb = SCAN_LANES, ns // SCAN_LANES
    nblk = seq // SUBL

    def body(g_re, g_im, sr, si, ar, ai, l_re, l_im, da_re, da_im):
        a = (ar[...], -ai[...])
        pw = _powers(a)
        tab = _table(pw, [SUBL - 1 - t for t in range(SUBL)], w)
        row = _rows((SUBL, w))

        def step(kk, carry):
            c_re, c_im, acc_re, acc_im = carry
            i = nblk - 1 - kk
            t0 = pl.multiple_of(i * SUBL, SUBL)
            x = (g_re[pl.ds(t0, SUBL), :], g_im[pl.ds(t0, SUBL), :])
            for d, k in ((1, 0), (2, 1), (4, 3)):
                sh = (jnp.where(row >= SUBL - d, 0.0, pltpu.roll(x[0], SUBL - d, 0)),
                      jnp.where(row >= SUBL - d, 0.0, pltpu.roll(x[1], SUBL - d, 0)))
                t = _cmul(pw[k], sh)
                x = (x[0] + t[0], x[1] + t[1])
            t = _cmul(tab, (c_re, c_im))
            x = (x[0] + t[0], x[1] + t[1])
            l_re[pl.ds(t0, SUBL), :] = x[0]
            l_im[pl.ds(t0, SUBL), :] = x[1]
            tp = jnp.maximum(t0 - 1, 0)
            live = (i > 0).astype(f32)
            p_re = sr[pl.ds(tp, 1), :] * live
            p_im = si[pl.ds(tp, 1), :] * live
            sp_re = jnp.where(row == 0, p_re, pltpu.roll(sr[pl.ds(t0, SUBL), :], 1, 0))
            sp_im = jnp.where(row == 0, p_im, pltpu.roll(si[pl.ds(t0, SUBL), :], 1, 0))
            acc_re = acc_re + x[0] * sp_re + x[1] * sp_im
            acc_im = acc_im + x[1] * sp_re - x[0] * sp_im
            return (x[0][:1, :], x[1][:1, :], acc_re, acc_im)

        z1 = jnp.zeros((1, w), f32)
        z8 = jnp.zeros((SUBL, w), f32)
        _, _, acc_re, acc_im = lax.fori_loop(0, nblk, step, (z1, z1, z8, z8), unroll=2)
        da_re[...] = jnp.sum(acc_re, axis=0, keepdims=True)
        da_im[...] = jnp.sum(acc_im, axis=0, keepdims=True)

    strip = pl.BlockSpec((seq, w), lambda j: (0, j))
    lane = pl.BlockSpec((1, w), lambda j: (0, j))
    return pl.pallas_call(
        body, name="s5_scan_bwd", grid=(nb,),
        in_specs=[strip, strip, strip, strip, lane, lane],
        out_specs=[strip, strip, lane, lane],
        out_shape=[jax.ShapeDtypeStruct((seq, ns), f32)] * 2 + [jax.ShapeDtypeStruct((1, ns), f32)] * 2,
        compiler_params=_cparams(1),
    )(ds_re, ds_im, s_re, s_im, a_re, a_im)


def _rms(x, w):
    return x * lax.rsqrt(jnp.mean(x * x, axis=-1, keepdims=True) + EPS) * w


def f_rms(x, w):
    return (_rms(x, w),)


def f_rms_res(x, w):
    return (_rms(x, w), x)


def f_s5_prep(lam_re, lam_im, log_step, b_re, b_im):
    e = _expand(log_step.shape[1], S5_NS, S5_N)
    step = hdot(jnp.exp(log_step), e)
    mag = jnp.exp(lam_re * step)
    ab_re, ab_im = mag * jnp.cos(lam_im * step), mag * jnp.sin(lam_im * step)
    den = lam_re * lam_re + lam_im * lam_im
    nr = ab_re - 1.0
    coef_re = (nr * lam_re + ab_im * lam_im) / den
    coef_im = (ab_im * lam_re - nr * lam_im) / den
    bb_re, bb_im = coef_re * b_re - coef_im * b_im, coef_re * b_im + coef_im * b_re
    sel = (_rows((BW, S5_P)) % S5_P == _cols((BW, S5_P))).astype(f32)
    blk = _rows((BW, S5_NS)) // S5_P == _cols((BW, S5_NS)) // S5_N
    rows_bd = lambda t: jnp.where(blk, hdot(sel, t), 0.0)
    return (ab_re, ab_im, rows_bd(bb_re), rows_bd(bb_im))


def f_s5_c(c_re, c_im):
    sel_t = (_cols((S5_P, BW)) % S5_P == _rows((S5_P, BW))).astype(f32)
    blk_t = _rows((S5_NS, BW)) // S5_N == _cols((S5_NS, BW)) // S5_P
    cols_bd = lambda t: jnp.where(blk_t, hdot_tn(t, sel_t), 0.0)
    return (cols_bd(c_re), cols_bd(c_im))


def f_s5_act(y_lin, u, gate, d, w_glu):
    y = jax.nn.gelu(y_lin + d * u)
    y = y * jax.nn.sigmoid(bdot_w(y, *w_glu))
    return (y * jax.nn.silu(gate),)


def f_sgu(u, v, gate, ln_w, ln_b, *rest):
    w_s, b_pad = rest[:8], rest[8]
    t = u.shape[0]
    u32, v32 = jax.nn.gelu(u), jax.nn.gelu(v)
    mu = jnp.mean(v32, axis=-1, keepdims=True)
    var = jnp.mean(jnp.square(v32 - mu), axis=-1, keepdims=True)
    vn = (v32 - mu) * lax.rsqrt(var + EPS) * ln_w + ln_b
    tri = _rows((t, t)) >= _cols((t, t))
    s = hdot_tn(b_pad, _expand(LANES, BW, BW // 8))
    for h in range(8):
        s = s + bdot(jnp.where(tri, w_s[h], 0.0), vn) * _lane_mask(BW, 64 * h, 64 * h + 64)
    return (u32 * s * jax.nn.silu(gate),)


def f_m2_conv(x, w, b):
    return (sum(_row_of(w, k) * shift(x, 3 - k) for k in range(4)) + b,)


def f_sc(bg, cg, h, gate, w):
    z = cg * h
    conv = sum(_row_of(w, k) * shift(z, 2 - k) for k in range(3))
    return (bg * conv * jax.nn.silu(gate),)


def f_m2(z, xc, b0, b1, c0, c1, dt_raw, dt_bias, a_log, d_par, norm_w, st):
    q = z.shape[0]
    x = jax.nn.silu(xc)
    bm, cm = (jax.nn.silu(b0), jax.nn.silu(b1)), (jax.nn.silu(c0), jax.nn.silu(c1))
    dt = jax.nn.softplus(dt_raw + dt_bias)
    da = dt * (-jnp.exp(a_log))
    tri = _rows((q, q)) >= _cols((q, q))
    acs = hdot(tri.astype(f32), da)
    e = _expand(LANES, BW, M2_HEAD_DIM)
    dt_f, acs_f = hdot(dt, e), hdot(acs, e)
    last = _rows((q, BW)) == q - 1
    alast_f = jnp.sum(jnp.where(last, acs_f, 0.0), axis=0, keepdims=True)
    xdt = x * dt_f
    xdec = xdt * jnp.exp(alast_f - acs_f)
    acs_t = acs.T
    st_new = st * jnp.exp(alast_f)
    y_diag, y_off = 0.0, 0.0
    for g in range(M2_GROUPS):
        gm = _lane_mask(BW, 256 * g, 256 * g + 256)
        cb = bdot_nt(cm[g], bm[g])
        st_new = st_new + bdot_tn(bm[g], xdec * gm)
        y_off = y_off + bdot(cm[g], st) * gm
        for hh in range(M2_HEADS // M2_GROUPS):
            h = g * (M2_HEADS // M2_GROUPS) + hh
            col = jnp.sum(jnp.where(_cols((q, LANES)) == h, acs, 0.0), axis=1, keepdims=True)
            row = jnp.sum(jnp.where(_rows((LANES, q)) == h, acs_t, 0.0), axis=0, keepdims=True)
            decay = jnp.exp(jnp.where(tri, col - row, -1e30))
            y_diag = y_diag + bdot(cb * decay, xdt) * _lane_mask(BW, 64 * h, 64 * h + 64)
    d_f = sum(jnp.sum(jnp.where(_cols((1, LANES)) == h, d_par, 0.0), axis=1, keepdims=True)
              * _lane_mask(BW, 64 * h, 64 * h + 64) for h in range(M2_HEADS))
    y = y_diag + y_off * jnp.exp(acs_f) + d_f * x
    y = y * jax.nn.silu(z)
    return (_rms(y, norm_w), st_new)


def f_branch_mix(*v):
    nb = len(v) // 4
    ys, lg, wb, mb = v[0:nb], v[nb:2 * nb], v[2 * nb:3 * nb], v[3 * nb:]
    return (sum(jax.nn.sigmoid(lg[k] + mb[k]) * bdot_w(ys[k], *wb[k]) for k in range(nb)),)


def _param(arr, shadow=False):
    return A(arr, tuple(arr.shape), lambda i, nd=arr.ndim: (0,) * nd, shadow)


def _tb(arr, t, width, colblk):
    return A(arr, (t, width), lambda i: (i, colblk))


def _strip(arr, seq, colblk0, stride=1):
    return A(arr, (seq, LANES), lambda j: (0, colblk0 + stride * j))


def _s5_prep_args(p):
    lam_re = p["s5_lambda_re"].reshape(1, S5_NS)
    lam_im = p["s5_lambda_im"].reshape(1, S5_NS)
    log_step = jnp.pad(p["s5_log_step"].reshape(1, S5_GROUPS), ((0, 0), (0, LANES - S5_GROUPS)))
    b_lanes = lambda b: jnp.transpose(b, (2, 0, 1)).reshape(S5_P, S5_NS)
    return [_param(v) for v in (lam_re, lam_im, log_step, b_lanes(p["s5_b_re"]), b_lanes(p["s5_b_im"]))]


def _s5_c_args(p):
    c_lanes = lambda c: jnp.transpose(c, (1, 0, 2)).reshape(S5_P, S5_NS)
    return [_param(c_lanes(p["s5_c_re"])), _param(c_lanes(p["s5_c_im"]))]


def layer_fwd(x, p):
    seq = x.shape[0]
    nt = seq // CHUNK
    t2 = 256
    sv = {}
    t4 = min(512, seq)
    hb = block_fwd("rms_in", f_rms, seq // t4, [_tb(x, t4, D_MODEL, 0), _param(p["norm_w"].reshape(1, D_MODEL))],
                   [O((seq, D_MODEL), bf16, (t4, D_MODEL), lambda i: (i, 0))])[0]
    proj = mm("mm_in", NN, [(Op(hb), Op(p["w_in"]), D_MODEL, 1)], seq, PW, 2048, 1152)
    whole = lambda shape, dt: O(shape, dt, shape, lambda i: (0, 0))
    ab_re, ab_im, bb_re, bb_im = block_fwd("s5_prep", f_s5_prep, 1, _s5_prep_args(p),
                                           [whole((1, S5_NS), f32)] * 2 + [whole((BW, S5_NS), bf16)] * 2)
    cc_re, cc_im = block_fwd("s5_c", f_s5_c, 1, _s5_c_args(p), [whole((S5_NS, BW), bf16)] * 2)
    bu_re = mm("mm_bu_re", NN, [(Op(proj, 0, C_S5U), Op(bb_re), BW, 1)], seq, S5_NS, 1024, 1024)
    bu_im = mm("mm_bu_im", NN, [(Op(proj, 0, C_S5U), Op(bb_im), BW, 1)], seq, S5_NS, 1024, 1024)
    s_re, s_im = s5_scan_fwd(bu_re, bu_im, ab_re, ab_im)
    y_lin = mm("mm_s5y", NN, [(Op(s_re), Op(cc_re), S5_NS, 1), (Op(s_im), Op(cc_im), S5_NS, -1)], seq, BW, 512, BW)
    s5_act_args = [_tb(y_lin, t4, BW, 0), _tb(proj, t4, BW, C_S5U // BW), _tb(proj, t4, BW, C_S5G // BW),
                   _param(p["s5_d"].reshape(1, BW)), _param(p["s5_w_glu"], True)]
    out_bw =O((seq, BW), f32, (CHUNK, BW), lambda i: (i, 0))
    y_a = block_fwd("s5_act", f_s5_act, seq // t4, s5_act_args, [O((seq, BW), f32, (t4, BW), lambda i: (i, 0))])[0]
    y_b = block_fwd("sgu", f_sgu, nt, _sgu_args(proj, p), [out_bw])[0]
    xc = block_fwd("m2_conv", f_m2_conv, 8, _m2_conv_args(proj, p, seq),
                   [O((seq, 2 * BW), f32, (seq, LANES), lambda j: (0, j))])[0]
    y_c, st_saved = block_fwd("m2_ssd", f_m2, nt, _m2_args(proj, xc, p), [out_bw], carries=[(M2_STATE, BW)])
    y_d = block_fwd("sc", f_sc, 4, _sc_args(proj, p, seq), [O((seq, BW), f32, (seq, LANES), lambda j: (0, j))])[0]
    ys = [y_a, y_b, y_c, y_d]
    merged = block_fwd("branch_mix", f_branch_mix, seq // t2, _mix_args(ys, proj, p, t2),
                       [O((seq, D_MODEL), bf16, (t2, D_MODEL), lambda i: (i, 0))])[0]
    x_new = mm("mm_out", NN, [(Op(merged), Op(p["w_out"]), D_MODEL, 1)], seq, D_MODEL, 512, D_MODEL, add=Op(x))
    sv.update(x=x, hb=hb, proj=proj, ab=(ab_re, ab_im), bb=(bb_re, bb_im), cc=(cc_re, cc_im), s=(s_re, s_im), y_lin=y_lin,
              xc=xc, st=st_saved, ys=ys, merged=merged)
    return x_new, sv


def _sgu_args(proj, p):
    c0 = C_SGU // BW
    args = [_tb(proj, CHUNK, BW, c0), _tb(proj, CHUNK, BW, c0 + 1), _tb(proj, CHUNK, BW, c0 + 2),
            _param(p["sgu_ln_w"].reshape(1, BW)), _param(p["sgu_ln_b"].reshape(1, BW))]
    args += [A(p["sgu_w"], (None, CHUNK, CHUNK), lambda i, h=h: (h, 0, 0)) for h in range(8)]
    args.append(_param(jnp.pad(p["sgu_b"], ((0, LANES - 8), (0, 0)))))
    return args


def _m2_conv_args(proj, p, seq):
    return [_strip(proj, seq, C_XBC // LANES), A(p["m2_conv_w"], (4, LANES), lambda j: (0, j)),
            A(p["m2_conv_b"].reshape(1, 2 * BW), (1, LANES), lambda j: (0, j))]


def _pad_lanes(v):
    return jnp.pad(v.reshape(1, -1), ((0, 0), (0, LANES - v.size)))


def _m2_args(proj, xc, p):
    args = [_tb(proj, CHUNK, BW, C_M2Z // BW), _tb(xc, CHUNK, BW, 0)]
    args += [_tb(xc, CHUNK, LANES, 4 + k) for k in range(4)]
    args.append(_tb(proj, CHUNK, LANES, C_DT // LANES))
    args += [_param(_pad_lanes(p["m2_dt_bias"])), _param(_pad_lanes(p["m2_a_log"])), _param(_pad_lanes(p["m2_d"])),
             _param(p["m2_norm_w"].reshape(1, BW))]
    return args


def _sc_args(proj, p, seq):
    c0 = C_SC // LANES
    return [_strip(proj, seq, c0 + k, 4) for k in range(4)] + [A(p["sc_conv_w"], (3, LANES), lambda j: (0, j))]


def _mix_args(ys, proj, p, t, ks=range(N_BRANCH)):
    args = [_tb(ys[k], t, BW, 0) for k in ks]
    args += [_tb(proj, t, D_MODEL, k) for k in ks]
    args += [A(p["w_branch"], (BW, D_MODEL), lambda i, k=k: (k, 0), True) for k in ks]
    mb = p["merge_b"].reshape(N_BRANCH, 1, D_MODEL)
    args += [A(mb, (None, 1, D_MODEL), lambda i, k=k: (k, 0, 0)) for k in ks]
    return args


def layer_bwd(d_out, p, sv):
    seq = d_out.shape[0]
    nt = seq // CHUNK
    t4 = min(512, seq)
    proj, ys = sv["proj"], sv["ys"]
    g = {}
    acc = lambda shape: O(tuple(shape), f32, tuple(shape), lambda i, nd=len(shape): (0,) * nd)
    d_merged = mm("mm_out_dx", NT, [(Op(d_out), Op(p["w_out"]), D_MODEL, 1)], seq, D_MODEL, 512, D_MODEL, out_dtype=f32)
    g["w_out"] = mm("mm_out_dw", TN, [(Op(sv["merged"]), Op(d_out), seq, 1)], D_MODEL, D_MODEL, 256, D_MODEL, out_dtype=bf16)
    dys, dwb, dmb, dproj, tb = [], [], [], None, min(256, seq)
    for half in range(2):
        gouts = [O((seq, BW), f32, (tb, BW), lambda i: (i, 0)) for _ in range(2)]
        gouts.append(O((seq, PW_MAIN), bf16, (tb, 2 * D_MODEL), lambda i, half=half: (i, half), alias=dproj))
        gouts += [acc((BW, D_MODEL)) for _ in range(2)] + [acc((1, D_MODEL)) for _ in range(2)]
        routes = [R(k, k) for k in range(2)] + [R(2 + k, 2, k * D_MODEL) for k in range(2)]
        routes += [R(4 + k, 3 + k, acc=True) for k in range(2)] + [R(6 + k, 5 + k, acc=True) for k in range(2)]
        res = block_bwd(f"branch_mix_bwd{half}", f_branch_mix, seq // tb, _mix_args(ys, proj, p, tb, (2 * half, 2 * half + 1)),
                        [_tb(d_merged, tb, D_MODEL, 0)], gouts, routes)
        dys, dproj, dwb, dmb = dys + list(res[:2]), res[2], dwb + list(res[3:5]), dmb + list(res[5:7])
    g["w_branch"] = jnp.stack(dwb)
    g["merge_b"] = jnp.concatenate(dmb, axis=0)
    res = block_bwd("sc_bwd", f_sc, 4, _sc_args(proj, p, seq), [_strip(dys[3], seq, 0)],
                    [O((seq, PW_MAIN), bf16, (seq, 4 * LANES), lambda j: (0, C_SC // (4 * LANES) + j), alias=dproj),
                     O((3, BW), f32, (3, LANES), lambda j: (0, j))],
                    [R(k, 0, k * LANES) for k in range(4)] + [R(4, 1)])
    dproj, g["sc_conv_w"] = res
    res = block_bwd("m2_ssd_bwd", f_m2, nt, _m2_args(proj, sv["xc"], p), [_tb(dys[2], CHUNK, BW, 0)],
                    [O((seq, PW_MAIN), bf16, (CHUNK, BW), lambda i: (i, C_M2Z // BW), alias=dproj),
                     O((seq, 2 * BW), f32, (CHUNK, 2 * BW), lambda i: (i, 0)),
                     O((seq, LANES), bf16, (CHUNK, LANES), lambda i: (i, 0)),
                     acc((1, LANES)), acc((1, LANES)), acc((1, LANES)), acc((1, BW))],
                    [R(0, 0), R(1, 1, 0)] + [R(2 + k, 1, BW + k * LANES) for k in range(4)] + [R(6, 2)]
                    + [R(7, 3, acc=True), R(8, 4, acc=True), R(9, 5, acc=True), R(10, 6, acc=True)],
                    saved=[sv["st"]], rev=True)
    dproj, dxc, d_dt = res[0], res[1], res[2]
    g["m2_dt_bias"], g["m2_a_log"], g["m2_d"] = (r[0, :M2_HEADS] for r in res[3:6])
    g["m2_norm_w"] = res[6].reshape(BW)
    res = block_bwd("m2_conv_bwd", f_m2_conv, 8, _m2_conv_args(proj, p, seq), [_strip(dxc, seq, 0)],
                    [O((seq, PW_MAIN), bf16, (seq, LANES), lambda j: (0, C_XBC // LANES + j), alias=dproj),
                     O((4, 2 * BW), f32, (4, LANES), lambda j: (0, j)), O((1, 2 * BW), f32, (1, LANES), lambda j: (0, j))],
                    [R(0, 0), R(1, 1), R(2, 2)])
    dproj, g["m2_conv_w"], cb = res
    g["m2_conv_b"] = cb.reshape(2 * BW)
    res = block_bwd("sgu_bwd", f_sgu, nt, _sgu_args(proj, p), [_tb(dys[1], CHUNK, BW, 0)],
                    [O((seq, PW_MAIN), bf16, (CHUNK, 3 * BW), lambda i: (i, C_SGU // (3 * BW)), alias=dproj),
                     acc((1, BW)), acc((1, BW))] + [acc((CHUNK, CHUNK)) for _ in range(8)] + [acc((LANES, CHUNK))],
                    [R(0, 0, 0), R(1, 0, BW), R(2, 0, 2 * BW), R(3, 1, acc=True), R(4, 2, acc=True)]
                    + [R(5 + h, 3 + h, acc=True) for h in range(8)] + [R(13, 11, acc=True)])
    dproj = res[0]
    g["sgu_ln_w"], g["sgu_ln_b"] = res[1].reshape(BW), res[2].reshape(BW)
    g["sgu_w"] = jnp.stack(res[3:11])
    g["sgu_b"] = res[11][:8]
    y_lin, (s_re, s_im), (ab_re, ab_im) = sv["y_lin"], sv["s"], sv["ab"]
    s5_act_args = [_tb(y_lin, t4, BW, 0), _tb(proj, t4, BW, C_S5U // BW), _tb(proj, t4, BW, C_S5G // BW),
                   _param(p["s5_d"].reshape(1, BW)), _param(p["s5_w_glu"], True)]
    res = block_bwd("s5_act_bwd", f_s5_act, seq // t4, s5_act_args, [_tb(dys[0], t4, BW, 0)],
                    [O((seq, BW), bf16, (t4, BW), lambda i: (i, 0)), O((seq, BW), f32, (t4, BW), lambda i: (i, 0)),
                     O((seq, PW_MAIN), bf16, (t4, BW), lambda i: (i, C_S5G // BW), alias=dproj),
                     acc((1, BW)), acc((BW, BW))],
                    [R(0, 0), R(1, 1), R(2, 2), R(3, 3, acc=True), R(4, 4, acc=True)])
    dy_lin, du1, dproj = res[0], res[1], res[2]
    g["s5_d"] = res[3].reshape(S5_GROUPS, S5_P)
    g["s5_w_glu"] = res[4]
    (bb_re, bb_im), (cc_re, cc_im) = sv["bb"], sv["cc"]
    ds_re = mm("mm_s5y_dre", NT, [(Op(dy_lin), Op(cc_re), BW, 1)], seq, S5_NS, 512, 1024)
    ds_im = mm("mm_s5y_dim", NT, [(Op(dy_lin), Op(cc_im), BW, -1)], seq, S5_NS, 512, 1024)
    dc_re = mm("mm_s5y_dcre", TN, [(Op(s_re), Op(dy_lin), seq, 1)], S5_NS, BW, 512, BW)
    dc_im = mm("mm_s5y_dcim", TN, [(Op(s_im), Op(dy_lin), seq, -1)], S5_NS, BW, 512, BW)
    l_re, l_im, da_re, da_im = s5_scan_bwd(ds_re, ds_im, s_re, s_im, ab_re, ab_im)
    dproj = mm("mm_bu_dx", NT, [(Op(l_re), Op(bb_re), S5_NS, 1), (Op(l_im), Op(bb_im), S5_NS, 1)],
               seq, BW, 512, BW, add=Op(du1), out=dproj, out_col=C_S5U)
    dbb = [mm(f"mm_bu_dw{n}", TN, [(Op(proj, 0, C_S5U), Op(l), seq, 1)], BW, S5_NS, 256, 1024)
           for n, l in (("re", l_re), ("im", l_im))]
    gouts = [acc((1, S5_NS)), acc((1, S5_NS)), acc((1, LANES))] + [acc((S5_P, S5_NS))] * 2
    res = block_bwd("s5_prep_bwd", f_s5_prep, 1, _s5_prep_args(p),
                    [_param(v) for v in (da_re, da_im, dbb[0], dbb[1])], gouts, [R(k, k, acc=True) for k in range(5)])
    g["s5_lambda_re"], g["s5_lambda_im"] = res[0].reshape(S5_GROUPS, S5_N), res[1].reshape(S5_GROUPS, S5_N)
    g["s5_log_step"] = res[2][0, :S5_GROUPS]
    b_natural = lambda b: jnp.transpose(b.reshape(S5_P, S5_GROUPS, S5_N), (1, 2, 0))
    c_natural = lambda c: jnp.transpose(c.reshape(S5_P, S5_GROUPS, S5_N), (1, 0, 2))
    g["s5_b_re"], g["s5_b_im"] = b_natural(res[3]), b_natural(res[4])
    res = block_bwd("s5_c_bwd", f_s5_c, 1, _s5_c_args(p), [_param(dc_re), _param(dc_im)],
                    [acc((S5_P, S5_NS))] * 2, [R(0, 0, acc=True), R(1, 1, acc=True)])
    g["s5_c_re"], g["s5_c_im"] = c_natural(res[0]), c_natural(res[1])
    hb, w_in = sv["hb"], p["w_in"]
    dh = mm("mm_in_dx", NT, [(Op(dproj), Op(w_in), PW_MAIN, 1), (Op(d_dt), Op(w_in, 0, PW_MAIN), LANES, 1)],
            seq, D_MODEL, 512, 512)
    dw = mm("mm_in_dw", TN, [(Op(hb), Op(dproj), seq, 1)], D_MODEL, PW_MAIN, 1024, 2048, out_dtype=bf16, out_width=PW)
    g["w_in"] = mm("mm_in_dwdt", TN, [(Op(hb), Op(d_dt), seq, 1)], D_MODEL, LANES, 256, LANES, out=dw, out_col=C_DT)
    dx, dnw = block_bwd("rms_in_bwd", f_rms_res, seq // t4, [_tb(sv["x"], t4, D_MODEL, 0), _param(p["norm_w"].reshape(1, D_MODEL))],
                        [_tb(dh, t4, D_MODEL, 0), _tb(d_out, t4, D_MODEL, 0)],
                        [O((seq, D_MODEL), f32, (t4, D_MODEL), lambda i: (i, 0)), acc((1, D_MODEL))],
                        [R(0, 0), R(1, 1, acc=True)])
    g["norm_w"] = dnw.reshape(D_MODEL)
    return dx, g


def loss_head(x, w, target):
    seq = x.shape[0]
    t = 256

    def body(x_ref, w_ref, t_ref, loss_ref, dx_ref, dw_ref):
        step = pl.program_id(0)

        def f(xv, wv):
            err = _rms(xv, wv) - t_ref[...]
            return 0.5 * jnp.sum(jnp.mean(err * err, axis=-1, keepdims=True), axis=0, keepdims=True)

        val, vjp_fn = jax.vjp(f, x_ref[...], w_ref[...])
        dx, dw = vjp_fn(jnp.ones((1, 1), f32))
        dx_ref[...] = dx

        @pl.when(step == 0)
        def _():
            loss_ref[...] = jnp.broadcast_to(val, loss_ref.shape)
            dw_ref[...] = dw

        @pl.when(step > 0)
        def _():
            loss_ref[...] += jnp.broadcast_to(val, loss_ref.shape)
            dw_ref[...] += dw

    blk = pl.BlockSpec((t, D_MODEL), lambda i: (i, 0))
    row = pl.BlockSpec((1, D_MODEL), lambda i: (0, 0))
    return pl.pallas_call(
        body, name="loss_head", grid=(seq // t,),
        in_specs=[blk, row, blk],
        out_specs=[pl.BlockSpec((1, LANES), lambda i: (0, 0)), blk, row],
        out_shape=[jax.ShapeDtypeStruct((1, LANES), f32), jax.ShapeDtypeStruct((seq, D_MODEL), f32),
                   jax.ShapeDtypeStruct((1, D_MODEL), f32)],
        compiler_params=_cparams(1),
    )(x, w.reshape(1, D_MODEL), target)


LAYER_KEYS = ("norm_w", "w_in", "s5_lambda_re", "s5_lambda_im", "s5_b_re", "s5_b_im", "s5_c_re", "s5_c_im", "s5_d",
              "s5_log_step", "s5_w_glu", "sgu_ln_w", "sgu_ln_b", "sgu_w", "sgu_b", "m2_conv_w", "m2_conv_b",
              "m2_dt_bias", "m2_a_log", "m2_d", "m2_norm_w", "sc_conv_w", "merge_b", "w_branch", "w_out")


def local_step(x, target, layers, final_norm_w):
    saved = []
    for p in layers:
        x, sv = layer_fwd(x, p)
        saved.append(sv)
    loss, dx, dfw = loss_head(x, final_norm_w, target)
    grads = []
    for p, sv in zip(reversed(layers), reversed(saved)):
        dx, g = layer_bwd(dx, p, sv)
        grads.append(g)
    return loss[0, 0], dx, grads[::-1], dfw.reshape(D_MODEL)


MESH = pl.DeviceIdType.MESH
ANY = pl.BlockSpec(memory_space=pl.ANY)


def _me():
    return lax.axis_index("x"), lax.axis_index("y"), lax.axis_index("c")


def _other_chips(x, y):
    return [(1 - x, y), (x, 1 - y), (1 - x, 1 - y)]


def _rcopy(src, dst, send, recv, dev):
    return pltpu.make_async_remote_copy(src_ref=src, dst_ref=dst, send_sem=send, recv_sem=recv,
                                        device_id=dev, device_id_type=MESH)


def _route_cut(rows, dtype):
    tile = 2 * SUBL * (4 // jnp.dtype(dtype).itemsize)
    return rows // 2 if rows % tile == 0 else rows


def _comm_call(name, body, arrs, out_shape, n_remote, aliases=None):
    n = len(arrs)
    return pl.pallas_call(
        body, name=name, in_specs=[ANY] * n, out_specs=[ANY] * len(out_shape), out_shape=out_shape,
        scratch_shapes=[pltpu.SemaphoreType.DMA((n, n_remote)), pltpu.SemaphoreType.DMA((n, n_remote))],
        input_output_aliases=aliases or {},
        compiler_params=pltpu.CompilerParams(has_side_effects=True),
    )(*arrs)


def gather_chips(name, arrs):
    n = len(arrs)
    cut = [_route_cut(a.shape[1], a.dtype) for a in arrs]

    def body(*refs):
        ins, outs = refs[:n], refs[n:2 * n]
        send, recv = refs[2 * n:]
        x, y, c = _me()
        jme, jx, jy, jd = 2 * x + y, 2 * (1 - x) + y, 2 * x + 1 - y, 2 * (1 - x) + 1 - y
        to_x, to_y, sib = (1 - x, y, c), (x, 1 - y, c), (x, y, 1 - c)

        def part(ref, a, hi):
            return ref.at[pl.ds(cut[a], ref.shape[0] - cut[a])] if hi else ref.at[pl.ds(0, cut[a])]

        def cp(a, k, ref, dev):
            return _rcopy(ref, ref, send.at[a, k], recv.at[a, k], dev)

        split = [a for a in range(n) if cut[a] < arrs[a].shape[1]]
        sent = [_rcopy(ins[a].at[c], outs[a].at[jme, c], send.at[a, k], recv.at[a, k], dev)
                for a in range(n) for k, dev in ((0, to_x), (1, to_y))]
        for s in sent:
            s.start()
        for a in range(n):
            blk = outs[a].at[jx, c]
            cp(a, 0, blk, to_x).wait_recv()
            sent += [cp(a, 2, part(blk, a, False), to_y), cp(a, 4, blk, sib)]
            sent[-2].start()
            sent[-1].start()
        for a in range(n):
            blk = outs[a].at[jy, c]
            cp(a, 1, blk, to_y).wait_recv()
            sent.append(cp(a, 5, blk, sib))
            sent[-1].start()
            if a in split:
                sent.append(cp(a, 3, part(blk, a, True), to_x))
                sent[-1].start()
        for a in range(n):
            lo = part(outs[a].at[jd, c], a, False)
            cp(a, 2, lo, to_y).wait_recv()
            sent.append(cp(a, 6, lo, sib))
            sent[-1].start()
        for a in split:
            hi = part(outs[a].at[jd, c], a, True)
            cp(a, 3, hi, to_x).wait_recv()
            sent.append(cp(a, 7, hi, sib))
            sent[-1].start()
        for a in range(n):
            cp(a, 4, outs[a].at[jx, 1 - c], sib).wait_recv()
            cp(a, 5, outs[a].at[jy, 1 - c], sib).wait_recv()
            cp(a, 6, part(outs[a].at[jd, 1 - c], a, False), sib).wait_recv()
        for a in split:
            cp(a, 7, part(outs[a].at[jd, 1 - c], a, True), sib).wait_recv()
        for s in sent:
            s.wait_send()

    out_shape = [jax.ShapeDtypeStruct((4,) + a.shape, a.dtype) for a in arrs]
    got = _comm_call(name, body, arrs, out_shape, 8)
    jme = 2 * lax.axis_index("x") + lax.axis_index("y")
    return [lax.dynamic_update_index_in_dim(g, a, jme, 0) for g, a in zip(got, arrs)]


def swap_halves(name, arrs):
    n = len(arrs)

    def body(*refs):
        ins, outs = refs[:n], refs[n:2 * n]
        send, recv = refs[2 * n:]
        x, y, c = _me()
        remote = [_rcopy(ins[a].at[1 - c], outs[a], send.at[a, 0], recv.at[a, 0], (x, y, 1 - c)) for a in range(n)]
        for cp in remote:
            cp.start()
        for cp in remote:
            cp.wait()

    return _comm_call(name, body, arrs, [jax.ShapeDtypeStruct(a.shape[1:], a.dtype) for a in arrs], 1)


def exchange_chips(name, arrs):
    n = len(arrs)

    def body(*refs):
        ins, outs = refs[:n], refs[n:2 * n]
        send, recv = refs[2 * n:]
        x, y, c = _me()
        remote = [_rcopy(ins[a].at[2 * cx + cy], outs[a].at[k], send.at[a, k], recv.at[a, k], (cx, cy, c))
                  for a in range(n) for k, (cx, cy) in enumerate(_other_chips(x, y))]
        for cp in remote:
            cp.start()
        for cp in remote:
            cp.wait()

    return _comm_call(name, body, arrs, [jax.ShapeDtypeStruct((3,) + a.shape[1:], a.dtype) for a in arrs], 3)


def gather_cores(name, arrs):
    n = len(arrs)

    def body(*refs):
        bufs = refs[n:2 * n]
        send, recv = refs[2 * n:]
        x, y, c = _me()
        remote = [_rcopy(bufs[a].at[c], bufs[a].at[c], send.at[a, 0], recv.at[a, 0], (x, y, 1 - c)) for a in range(n)]
        for cp in remote:
            cp.start()
        for a in range(n):
            _rcopy(bufs[a].at[1 - c], bufs[a].at[1 - c], send.at[a, 0], recv.at[a, 0], (x, y, 1 - c)).wait_recv()
        for cp in remote:
            cp.wait_send()

    return _comm_call(name, body, arrs, [jax.ShapeDtypeStruct(a.shape, a.dtype) for a in arrs], 1,
                      aliases={a: a for a in range(n)})


ROW_BLOCK = 512


def esum(name, terms, rows, width, out_dtype, out_slots=None):
    tr = next((t for t in range(min(rows, ROW_BLOCK), 0, -SUBL) if rows % t == 0 and t % SUBL == 0), rows)
    where =jnp.stack([lax.axis_index("c"), 2 * lax.axis_index("x") + lax.axis_index("y")]).astype(jnp.int32)
    pick = {"c": 0, "j": 1}

    def body(s_ref, *refs):
        acc = refs[0][...].astype(f32)
        for r in refs[1:-1]:
            acc = acc + r[...].astype(f32)
        refs[-1][...] = acc.astype(out_dtype)

    specs = []
    for arr, lead in terms:
        if lead is None:
            specs.append(pl.BlockSpec((tr, width), lambda i, s: (i, 0)))
        elif isinstance(lead, str):
            specs.append(pl.BlockSpec((None, tr, width), lambda i, s, lead=lead: (s[pick[lead]], i, 0)))
        else:
            specs.append(pl.BlockSpec((None, tr, width), lambda i, s, lead=lead: (lead, i, 0)))
    if out_slots is None:
        out_spec = pl.BlockSpec((tr, width), lambda i, s: (i, 0))
        out_shape = jax.ShapeDtypeStruct((rows, width), out_dtype)
    else:
        out_spec = pl.BlockSpec((None, tr, width), lambda i, s: (s[0], i, 0))
        out_shape = jax.ShapeDtypeStruct((out_slots, rows, width), out_dtype)
    return pl.pallas_call(
        body, name=name,
        grid_spec=pltpu.PrefetchScalarGridSpec(num_scalar_prefetch=1, grid=(rows // tr,), in_specs=specs, out_specs=out_spec),
        out_shape=out_shape, compiler_params=_cparams(1),
    )(where, *[t[0] for t in terms])


def reduce_to_shards(parts):
    tags = [str(k) for k in range(len(parts))]
    theirs = swap_halves("rs_swap", parts)
    t1 = []
    for tag, p, th in zip(tags, parts, theirs):
        _, _, h, w = p.shape
        t1.append(esum("rs_add_cores" + tag, [(p.reshape(2, 4 * h, w), "c"), (th.reshape(4 * h, w), None)],
                       4 * h, w, p.dtype).reshape(4, h, w))
    landed = exchange_chips("rs_exchange", t1)
    red = []
    for tag, p, t, got in zip(tags, parts, t1, landed):
        _, _, h, w = p.shape
        red.append(esum("rs_add_chips" + tag, [(t, "j"), (got, 0), (got, 1), (got, 2)], h, w, f32, out_slots=2))
    return gather_cores("rs_gather", red)


def _adamw_step(w_ref, g_ref, m_ref, v_ref, d_ref, nm_ref, nv_ref):
    gv = g_ref[...]
    nm = ADAM_B1 * m_ref[...] + (1.0 - ADAM_B1) * gv
    nv = ADAM_B2 * v_ref[...] + (1.0 - ADAM_B2) * jnp.square(gv)
    m_hat = nm / (1.0 - ADAM_B1 ** ADAM_STEP)
    v_hat = nv / (1.0 - ADAM_B2 ** ADAM_STEP)
    d_ref[...] = -ADAM_LR * (m_hat / (jnp.sqrt(v_hat) + ADAM_EPS) + ADAM_WD * w_ref[...])
    nm_ref[...] = nm
    nv_ref[...] = nv


def adamw(name, w, g, m, v, tr=None):
    rows, rest = w.shape[0], w.shape[1:]
    if tr is None:
        tr = ROW_BLOCK if rows % ROW_BLOCK == 0 else rows
    assert rows % tr == 0

    def body(*refs):
        _adamw_step(*refs)

    spec = pl.BlockSpec((tr,) + rest, lambda i: (i,) + (0,) * len(rest))
    return pl.pallas_call(
        body, name=name, grid=(rows // tr,), in_specs=[spec] * 4, out_specs=[spec] * 3,
        out_shape=[jax.ShapeDtypeStruct(w.shape, f32)] * 3, compiler_params=_cparams(1),
    )(w, g, m, v)


def adamw_many(name, ws, gs, ms, vs):
    n = len(ws)

    def body(*refs):
        ins, outs = refs[:4 * n], refs[4 * n:]
        for k in range(n):
            _adamw_step(ins[k], ins[n + k], ins[2 * n + k], ins[3 * n + k], outs[k], outs[n + k], outs[2 * n + k])

    vmem = pl.BlockSpec(memory_space=pltpu.VMEM)
    res = pl.pallas_call(
        body, name=name, in_specs=[vmem] * (4 * n), out_specs=[vmem] * (3 * n),
        out_shape=[jax.ShapeDtypeStruct(w.shape, f32) for w in ws] * 3,
        compiler_params=pltpu.CompilerParams(vmem_limit_bytes=VMEM_LIMIT),
    )(*ws, *gs, *ms, *vs)
    return res[:n], res[n:2 * n], res[2 * n:]


PACK_W = 1024


def _pack(parts, halves, row_mult):
    flat = jnp.concatenate([p.reshape(-1) for p in parts])
    per = halves * row_mult * PACK_W
    total = -(-flat.size // per) * per
    flat = jnp.pad(flat, (0, total - flat.size))
    return flat.reshape(halves, total // (halves * PACK_W), PACK_W)


def _unpack(flat, shapes):
    out, pos = [], 0
    for s in shapes:
        n = int(np.prod(s))
        out.append(flat[pos:pos + n].reshape(s))
        pos += n
    return out


SHARDED_BIG = ("w_in", "w_branch", "w_out", "s5_w_glu")
SHARDED_SMALL = ("m2_conv_w", "sc_conv_w", "merge_b")
SHARD_AXIS = {"w_in": 2, "w_branch": 3, "w_out": 1, "s5_w_glu": 1, "m2_conv_w": 2, "sc_conv_w": 2, "merge_b": 2}
REPLICATED = ("norm_w", "s5_lambda_re", "s5_lambda_im", "s5_b_re", "s5_b_im", "s5_c_re", "s5_c_im", "s5_d", "s5_log_step",
              "sgu_ln_w", "sgu_ln_b", "sgu_w", "sgu_b", "m2_conv_b", "m2_dt_bias", "m2_a_log", "m2_d", "m2_norm_w")
WEIGHTS = ("norm_w", "w_in", "s5_lambda_re", "s5_lambda_im", "s5_b_re", "s5_b_im", "s5_c_re", "s5_c_im", "s5_d",
           "s5_log_step", "s5_w_glu", "sgu_ln_w", "sgu_ln_b", "sgu_w", "sgu_b", "m2_conv_w", "m2_conv_b", "m2_dt_bias",
           "m2_a_log", "m2_d", "m2_norm_w", "sc_conv_w", "merge_b", "w_branch", "w_out", "final_norm_w")
N_LAYERS = 2


SHARD_W = IN_DIM // 4
SHARD_PAD = -(-SHARD_W // LANES) * LANES
REGROUP_W = 3 * LANES


def _kernel_pieces():
    out, pos = [], 0
    for s, n in _col_segments():
        while n:
            take = min(n, SHARD_W - s % SHARD_W)
            out.append((pos, s, take))
            pos, s, n = pos + take, s + take, n - take
    return out


def regroup_cols(name, src, steps, out_shape, out=None):
    n_src, rows, width = src.shape
    ow, win = REGROUP_W, REGROUP_W + LANES
    k_max = max(len(p) for _, _, p in steps)
    assert width % LANES == 0

    def body(*refs):
        src_ref, out_ref = refs[0], refs[-5]
        wbuf, obuf, sem_in, sem_out = refs[-4:]

        def fetch(q, slot):
            started = []
            for p, (s, col, lo, hi) in enumerate(steps[q][2]):
                w0 = col // LANES * LANES
                wlen = min(win, width - w0)
                cp = pltpu.make_async_copy(src_ref.at[s, :, pl.ds(w0, wlen)], wbuf.at[slot, p, :, pl.ds(0, wlen)], sem_in.at[slot, p])
                cp.start()
                started.append((cp, wlen, col - w0 - lo, lo, hi))
            return started

        pend, writes = fetch(0, 0), [None, None]
        for q, (t, b, _) in enumerate(steps):
            slot = q % 2
            nxt = fetch(q + 1, 1 - slot) if q + 1 < len(steps) else []
            acc = [jnp.zeros((rows, LANES), f32) for _ in range(ow // LANES)]
            for p, (cp, wlen, shift, lo, hi) in enumerate(pend):
                cp.wait()
                for k in range(ow // LANES):
                    c_lo, c_hi = max(lo, k * LANES), min(hi, (k + 1) * LANES)
                    if c_lo >= c_hi:
                        continue
                    wb = (c_lo + shift) // LANES
                    alen = min(2 * LANES, wlen - wb * LANES)
                    a = wbuf[slot, p, :, wb * LANES:wb * LANES + alen]
                    r, c = _rows((alen, LANES)), _cols((alen, LANES))
                    sh = shift + (k - wb) * LANES
                    sel = (r == c + sh) & (c >= c_lo - k * LANES) & (c < c_hi - k * LANES)
                    acc[k] = acc[k] + jnp.dot(a, sel.astype(bf16), preferred_element_type=f32)
            if writes[slot] is not None:
                writes[slot].wait()
            for k in range(ow // LANES):
                obuf[slot, :, k * LANES:(k + 1) * LANES] = acc[k].astype(bf16)
            writes[slot] = pltpu.make_async_copy(obuf.at[slot], out_ref.at[t, :, pl.ds(b * ow, ow)], sem_out.at[slot])
            writes[slot].start()
            pend = nxt
        for wr in writes:
            if wr is not None:
                wr.wait()

    operands, io_alias = [src], {}
    if out is not None:
        operands.append(out)
        io_alias = {1: 0}
    return pl.pallas_call(
        body, name=name, in_specs=[ANY] * len(operands), out_specs=ANY,
        out_shape=jax.ShapeDtypeStruct(out_shape, bf16), input_output_aliases=io_alias,
        scratch_shapes=[pltpu.VMEM((2, k_max, rows, win), bf16), pltpu.VMEM((2, rows, ow), bf16),
                        pltpu.SemaphoreType.DMA((2, k_max)), pltpu.SemaphoreType.DMA((2,))],
        compiler_params=pltpu.CompilerParams(vmem_limit_bytes=VMEM_LIMIT),
    )(*operands)


def _steps_to_kernel_cols(layer):
    steps = []
    for o in range(PW // REGROUP_W):
        pieces = []
        for pos, s, n in _kernel_pieces():
            lo, hi = max(pos, o * REGROUP_W), min(pos + n, (o + 1) * REGROUP_W)
            if lo < hi:
                ref_col = s + lo - pos
                pieces.append(((ref_col // SHARD_W) * N_LAYERS + layer, ref_col % SHARD_W, lo - o * REGROUP_W, hi - o * REGROUP_W))
        steps.append((0, o, pieces))
    return steps


def _steps_to_shards(layer):
    steps = []
    for j in range(4):
        for b in range(SHARD_PAD // REGROUP_W):
            start, stop = j * SHARD_W + b * REGROUP_W, min(j * SHARD_W + (b + 1) * REGROUP_W, (j + 1) * SHARD_W)
            pieces = []
            for pos, s, n in _kernel_pieces():
                lo, hi = max(s, start), min(s + n, stop)
                if lo < hi:
                    pieces.append((0, pos + lo - s, lo - start, hi - start))
            steps.append((layer * 4 + j, b, pieces))
    return steps


def _gather_weights(w):
    w_in = jnp.pad(w["w_in"].astype(bf16), ((0, 0), (0, 0), (0, SHARD_PAD - SHARD_W)))
    arrs = [w_in, w["w_branch"].reshape(N_LAYERS, N_BRANCH * BW, -1).astype(bf16),
            w["w_out"].astype(bf16), w["s5_w_glu"].astype(bf16), w["m2_conv_w"], w["sc_conv_w"], w["merge_b"]]
    got = gather_chips("ag_weights", arrs)
    cols = lambda t: jnp.transpose(t, (1, 0, 2)).reshape(t.shape[1], -1)
    layers = []
    for i in range(N_LAYERS):
        p = {k: w[k][i] for k in REPLICATED}
        p["w_in"] = regroup_cols(f"w_in_cols{i}", got[0].reshape(4 * N_LAYERS, D_MODEL, SHARD_PAD), _steps_to_kernel_cols(i),
                                 (1, D_MODEL, PW))[0]
        p["w_branch"] = cols(got[1][:, i])
        p["w_out"] = got[2][:, i].reshape(D_MODEL, D_MODEL)
        p["s5_w_glu"] = got[3][:, i].reshape(BW, BW)
        p["m2_conv_w"], p["sc_conv_w"], p["merge_b"] = cols(got[4][:, i]), cols(got[5][:, i]), cols(got[6][:, i])
        layers.append(p)
    return layers


def _reduce_grads(grads, d_final, w):
    to_chips = lambda t: jnp.transpose(t.reshape(t.shape[0], 4, -1), (1, 0, 2))
    stack = lambda f: jnp.stack([f(g) for g in grads])
    dw_in = None
    for i, g in enumerate(grads):
        dw_in = regroup_cols(f"w_in_shards{i}", g["w_in"][None], _steps_to_shards(i), (4 * N_LAYERS, D_MODEL, SHARD_PAD), out=dw_in)
    parts = [dw_in.reshape(N_LAYERS, 4, D_MODEL, SHARD_PAD),
             stack(lambda g: to_chips(g["w_branch"].reshape(N_BRANCH * BW, D_MODEL)).astype(bf16)),
             stack(lambda g: g["w_out"].reshape(4, D_MODEL // 4, D_MODEL).astype(bf16)),
             stack(lambda g: g["s5_w_glu"].reshape(4, BW // 4, BW).astype(bf16))]
    rep = jnp.concatenate([stack(lambda g: g[k]).reshape(-1) for k in REPLICATED] + [d_final.reshape(-1)])
    quarter = -(-rep.size // (4 * 2 * SUBL * PACK_W)) * (2 * SUBL * PACK_W)
    rep = jnp.pad(rep, (0, 4 * quarter - rep.size))
    small = []
    for j in range(4):
        sharded = [stack(lambda g: to_chips(g[k])[j]) for k in SHARDED_SMALL]
        small.append(_pack(sharded + [rep[j * quarter:(j + 1) * quarter]], 2, SUBL))
    parts.append(jnp.stack(small, axis=1))
    red = reduce_to_shards(parts)
    out = {"w_in": red[0][:, :, :SHARD_W], "w_branch": red[1].reshape(w["w_branch"].shape), "w_out": red[2], "s5_w_glu": red[3]}
    small_flat = red[4].reshape(-1)
    n_small = sum(int(np.prod(w[k].shape)) for k in SHARDED_SMALL)
    out.update(zip(SHARDED_SMALL, _unpack(small_flat, [w[k].shape for k in SHARDED_SMALL])))
    mine = small_flat[n_small:n_small + quarter].reshape(2, quarter // (2 * PACK_W), PACK_W)
    rep_all = gather_chips("ag_small_grads", [mine])[0].reshape(-1)
    names = REPLICATED + ("final_norm_w",)
    out.update(zip(names, _unpack(rep_all, [w[k].shape for k in names])))
    return out


def _update(w, g, m, v):
    d, nm, nv = {}, {}, {}
    flat2 = lambda a: a.reshape(-1, a.shape[-1])
    cols_major = lambda a: jnp.transpose(a, (2, 0, 1))
    res = adamw("adamw_w_in", *[cols_major(t["w_in"]) for t in (w, g, m, v)], tr=SHARD_W // 14)
    d["w_in"], nm["w_in"], nv["w_in"] = (jnp.transpose(r, (1, 2, 0)) for r in res)
    for k in SHARDED_BIG[1:]:
        res = adamw("adamw_" + k, *[flat2(t[k]) for t in (w, g, m, v)])
        d[k], nm[k], nv[k] = (r.reshape(w[k].shape) for r in res)
    for k in ("s5_b_re", "s5_b_im"):
        res = adamw("adamw_" + k, *[flat2(t[k]) for t in (w, g, m, v)])
        d[k], nm[k], nv[k] = (r.reshape(w[k].shape) for r in res)
    rest = [k for k in WEIGHTS if k not in SHARDED_BIG + ("s5_b_re", "s5_b_im")]
    two_d = lambda a: a.reshape(1, -1) if a.ndim == 1 else a
    res = adamw_many("adamw_rest", *[[two_d(t[k]) for k in rest] for t in (w, g, m, v)])
    for tgt, rs in zip((d, nm, nv), res):
        tgt.update({k: r.reshape(w[k].shape) for k, r in zip(rest, rs)})
    return d, nm, nv


def kernel(x, norm_w, w_in, s5_lambda_re, s5_lambda_im, s5_b_re, s5_b_im, s5_c_re, s5_c_im, s5_d, s5_log_step, s5_w_glu, sgu_ln_w, sgu_ln_b, sgu_w, sgu_b, m2_conv_w, m2_conv_b, m2_dt_bias, m2_a_log, m2_d, m2_norm_w, sc_conv_w, merge_b, w_branch, w_out, final_norm_w, loss_target, m_norm_w, m_w_in, m_s5_lambda_re, m_s5_lambda_im, m_s5_b_re, m_s5_b_im, m_s5_c_re, m_s5_c_im, m_s5_d, m_s5_log_step, m_s5_w_glu, m_sgu_ln_w, m_sgu_ln_b, m_sgu_w, m_sgu_b, m_m2_conv_w, m_m2_conv_b, m_m2_dt_bias, m_m2_a_log, m_m2_d, m_m2_norm_w, m_sc_conv_w, m_merge_b, m_w_branch, m_w_out, m_final_norm_w, v_norm_w, v_w_in, v_s5_lambda_re, v_s5_lambda_im, v_s5_b_re, v_s5_b_im, v_s5_c_re, v_s5_c_im, v_s5_d, v_s5_log_step, v_s5_w_glu, v_sgu_ln_w, v_sgu_ln_b, v_sgu_w, v_sgu_b, v_m2_conv_w, v_m2_conv_b, v_m2_dt_bias, v_m2_a_log, v_m2_d, v_m2_norm_w, v_sc_conv_w, v_merge_b, v_w_branch, v_w_out, v_final_norm_w):
    given = dict(locals())
    w = {k: given[k] for k in WEIGHTS}
    m = {k: given["m_" + k] for k in WEIGHTS}
    v = {k: given["v_" + k] for k in WEIGHTS}
    layers = _gather_weights(w)
    loss, dx, grads, d_final = local_step(x[0], loss_target[0], layers, final_norm_w)
    loss = lax.psum(loss, ("x", "y", "c"))
    g = _reduce_grads(grads, d_final, w)
    d, nm, nv = _update(w, g, m, v)
    return (loss, dx[None], *[g[k] for k in WEIGHTS], *[d[k] for k in WEIGHTS],
            *[nm[k] for k in WEIGHTS], *[nv[k] for k in WEIGHTS])
```

```python
import functools
from typing import Any, Callable, NamedTuple

import numpy as np
import jax
import jax.numpy as jnp
from jax import lax
from jax.experimental import pallas as pl
from jax.experimental.pallas import tpu as pltpu

f32 = jnp.float32
bf16 = jnp.bfloat16

D_MODEL = 1024
BW = 512
N_BRANCH = 4
EPS = 1e-6
S5_GROUPS, S5_P, S5_N = 32, 16, 64
S5_NS = S5_GROUPS * S5_N
CHUNK = 128
M2_HEADS, M2_HEAD_DIM, M2_GROUPS, M2_STATE = 8, 64, 2, 128
IN_DIM = 10248
PW = 10368
PW_MAIN = 10240
LANES = 128
VMEM_LIMIT = 60 * 1024 * 1024

ADAM_LR, ADAM_B1, ADAM_B2, ADAM_EPS, ADAM_WD, ADAM_STEP = 0.001, 0.9, 0.999, 1e-08, 0.01, 10

C_MERGE = 0
C_SC = 4096
C_SGU = 6144
C_S5G = 7680
C_S5U = 8192
C_M2Z = 8704
C_XBC = 9216
C_DT = 10240


def _col_segments():
    segs = [(6152, 4096)]
    for j in range(4):
        segs += [(4104 + 128 * j, 128), (4616 + 128 * j, 128), (5128 + 128 * j, 128), (5640 + 128 * j, 128)]
    segs += [(1024, 1536), (512, 512), (0, 512), (2560, 512), (3072, 1024), (4096, 8)]
    return segs


NN = ((1,), (0,))
NT = ((1,), (1,))
TN = ((0,), (0,))


def _bd(a, b, dims):
    return lax.dot_general(a.astype(bf16), b.astype(bf16), (dims, ((), ())), preferred_element_type=f32)


def _hd(a, b, dims):
    return lax.dot_general(a, b, (dims, ((), ())), precision=lax.Precision.HIGHEST, preferred_element_type=f32)


def _make_dots(raw):
    @jax.custom_vjp
    def nn(a, b):
        return raw(a, b, NN)
    nn.defvjp(lambda a, b: (raw(a, b, NN), (a, b)), lambda r, g: (raw(g, r[1], NT), raw(r[0], g, TN)))

    @jax.custom_vjp
    def nt(a, b):
        return raw(a, b, NT)
    nt.defvjp(lambda a, b: (raw(a, b, NT), (a, b)), lambda r, g: (raw(g, r[1], NN), raw(g, r[0], TN)))

    @jax.custom_vjp
    def tn(a, b):
        return raw(a, b, TN)
    tn.defvjp(lambda a, b: (raw(a, b, TN), (a, b)), lambda r, g: (raw(r[1], g, NT), raw(r[0], g, NN)))
    return nn, nt, tn


bdot, bdot_nt, bdot_tn = _make_dots(_bd)
hdot, hdot_nt, hdot_tn = _make_dots(_hd)


@jax.custom_vjp
def bdot_w(a, w, shadow):
    return _bd(a, w, NN)


bdot_w.defvjp(lambda a, w, s: (_bd(a, w, NN), (a, w)),
              lambda r, g: (_bd(g, r[1], NT), jnp.zeros_like(r[1]), _bd(r[0], g, TN)))


def _rows(shape):
    return lax.broadcasted_iota(jnp.int32, shape, 0)


def _cols(shape):
    return lax.broadcasted_iota(jnp.int32, shape, 1)


def _shift_down(x, s):
    return jnp.where(_rows(x.shape) < s, 0.0, pltpu.roll(x, s, 0))


def _shift_up(x, s):
    n = x.shape[0]
    return jnp.where(_rows(x.shape) >= n - s, 0.0, pltpu.roll(x, n - s, 0))


@functools.partial(jax.custom_vjp, nondiff_argnums=(1,))
def shift(x, s):
    return _shift_down(x, s) if s else x


shift.defvjp(lambda x, s: (shift(x, s), None), lambda s, _, g: (_shift_up(g, s) if s else g,))


def _row_of(w, k):
    return jnp.sum(jnp.where(_rows(w.shape) == k, w, 0.0), axis=0, keepdims=True)


def _lane_mask(width, lo, hi):
    c = _cols((1, width))
    return ((c >= lo) & (c < hi)).astype(f32)


def _expand(rows, width, per):
    return (_cols((rows, width)) // per == _rows((rows, width))).astype(f32)


class A(NamedTuple):
    arr: Any
    block: tuple
    imap: Callable
    shadow: bool = False


class O(NamedTuple):
    shape: tuple
    dtype: Any
    block: tuple
    imap: Callable
    alias: Any = None


class R(NamedTuple):
    arg: int
    out: int
    off: Any = None
    acc: bool = False


def _cparams(n_grid):
    return pltpu.CompilerParams(dimension_semantics=("arbitrary",) * n_grid, vmem_limit_bytes=VMEM_LIMIT)


def _ispec(block, imap, n, rev):
    if rev:
        return pl.BlockSpec(block, lambda i: imap(n - 1 - i))
    return pl.BlockSpec(block, imap)


def _load(ref, a):
    v = ref[...]
    if a.shadow:
        return (v, jnp.zeros(v.shape, f32))
    return v.astype(f32)


def _save_spec(shape, n, rev):
    nd = len(shape)
    return _ispec((None,) + tuple(shape), lambda i: (i,) + (0,) * nd, n, rev)


def block_fwd(name, f, n, args, outs, carries=()):
    n_in, n_out, n_c = len(args), len(outs), len(carries)

    def body(*refs):
        ins, out_r = refs[:n_in], refs[n_in:n_in + n_out]
        saves, cs = refs[n_in + n_out:n_in + n_out + n_c], refs[n_in + n_out + n_c:]
        if n_c:
            @pl.when(pl.program_id(0) == 0)
            def _():
                for c in cs:
                    c[...] = jnp.zeros(c.shape, f32)
        vals = [_load(r, a) for r, a in zip(ins, args)]
        cv = [c[...] for c in cs]
        for s, v in zip(saves, cv):
            s[...] = v
        res = f(*vals, *cv)
        for r, v in zip(out_r, res[:n_out]):
            r[...] = v.astype(r.dtype)
        for c, v in zip(cs, res[n_out:]):
            c[...] = v

    out_shape = [jax.ShapeDtypeStruct(o.shape, o.dtype) for o in outs]
    out_specs = [pl.BlockSpec(o.block, o.imap) for o in outs]
    for shp in carries:
        out_shape.append(jax.ShapeDtypeStruct((n,) + tuple(shp), f32))
        out_specs.append(_save_spec(shp, n, False))
    return pl.pallas_call(
        body, name=name, grid=(n,),
        in_specs=[pl.BlockSpec(a.block, a.imap) for a in args],
        out_specs=out_specs, out_shape=out_shape,
        scratch_shapes=[pltpu.VMEM(tuple(shp), f32) for shp in carries],
        compiler_params=_cparams(1),
    )(*[a.arr for a in args])


def block_bwd(name, f, n, args, cots, gouts, routes, saved=(), rev=False):
    n_in, n_cot, n_c, n_go = len(args), len(cots), len(saved), len(gouts)
    diff = []
    for r in routes:
        if r.arg not in diff:
            diff.append(r.arg)
    aliases = [(k, o.alias) for k, o in enumerate(gouts) if o.alias is not None]

    def body(*refs):
        ins = refs[:n_in]
        cot_r = refs[n_in:n_in + n_cot]
        sav_r = refs[n_in + n_cot:n_in + n_cot + n_c]
        base = n_in + n_cot + n_c + len(aliases)
        go_r = refs[base:base + n_go]
        dcs = refs[base + n_go:]
        step = pl.program_id(0)
        if n_c:
            @pl.when(step == 0)
            def _():
                for d in dcs:
                    d[...] = jnp.zeros(d.shape, f32)
        vals = [_load(r, a) for r, a in zip(ins, args)]
        cv = [s[...] for s in sav_r]
        nd = len(diff)

        def g(*dv):
            full = list(vals)
            for idx, v in zip(diff, dv[:nd]):
                full[idx] = (vals[idx][0], v) if args[idx].shadow else v
            return tuple(f(*full, *dv[nd:]))

        primals = [vals[i][1] if args[i].shadow else vals[i] for i in diff] + cv
        _, vjp_fn = jax.vjp(g, *primals)
        ct = tuple([r[...].astype(f32) for r in cot_r] + [d[...] for d in dcs])
        grads = vjp_fn(ct)
        for r in routes:
            gr = grads[diff.index(r.arg)]
            ref = go_r[r.out]
            if r.acc:
                @pl.when(step == 0)
                def _(ref=ref, gr=gr):
                    ref[...] = gr.astype(ref.dtype)

                @pl.when(step > 0)
                def _(ref=ref, gr=gr):
                    ref[...] += gr.astype(ref.dtype)
            elif r.off is None:
                ref[...] = gr.astype(ref.dtype)
            else:
                ref[:, r.off:r.off + gr.shape[1]] = gr.astype(ref.dtype)
        for d, gr in zip(dcs, grads[nd:]):
            d[...] = gr

    in_specs = [_ispec(a.block, a.imap, n, rev) for a in list(args) + list(cots)]
    in_specs += [_save_spec(s.shape[1:], n, rev) for s in saved]
    in_specs += [pl.BlockSpec(memory_space=pl.ANY) for _ in aliases]
    operands = [a.arr for a in list(args) + list(cots)] + list(saved) + [arr for _, arr in aliases]
    io_alias = {n_in + n_cot + n_c + j: k for j, (k, _) in enumerate(aliases)}
    return pl.pallas_call(
        body, name=name, grid=(n,),
        in_specs=in_specs,
        out_specs=[_ispec(o.block, o.imap, n, rev) for o in gouts],
        out_shape=[jax.ShapeDtypeStruct(o.shape, o.dtype) for o in gouts],
        scratch_shapes=[pltpu.VMEM(tuple(s.shape[1:]), f32) for s in saved],
        input_output_aliases=io_alias,
        compiler_params=_cparams(1),
    )(*operands)


class Op(NamedTuple):
    arr: Any
    row: int = 0
    col: int = 0


def mm(name, mode, pairs, m, n, tm, tn, out_dtype=f32, add=None, out=None, out_col=0, out_width=None):
    tm, tn = min(tm, m), min(tn, n)
    assert m % tm == 0 and n % tn == 0
    in_specs, operands = [], []
    for a, b, k, _ in pairs:
        if mode == TN:
            assert a.row % k == 0 and a.col % tm == 0
            in_specs.append(pl.BlockSpec((k, tm), lambda j, i, a=a, k=k: (a.row // k, i + a.col // tm)))
        else:
            assert a.col % k == 0 and a.row % tm == 0
            in_specs.append(pl.BlockSpec((tm, k), lambda j, i, a=a, k=k: (i + a.row // tm, a.col // k)))
        if mode == NT:
            assert b.col % k == 0 and b.row % tn == 0
            in_specs.append(pl.BlockSpec((tn, k), lambda j, i, b=b, k=k: (j + b.row // tn, b.col // k)))
        else:
            assert b.row % k == 0 and b.col % tn == 0
            in_specs.append(pl.BlockSpec((k, tn), lambda j, i, b=b, k=k: (b.row // k, j + b.col // tn)))
        operands += [a.arr, b.arr]
    n_p = len(pairs)
    if add is not None:
        assert add.col % tn == 0
        in_specs.append(pl.BlockSpec((tm, tn), lambda j, i: (i, j + add.col // tn)))
        operands.append(add.arr)
    io_alias = {}
    if out is not None:
        assert out_col % tn == 0
        in_specs.append(pl.BlockSpec(memory_space=pl.ANY))
        operands.append(out)
        io_alias = {len(operands) - 1: 0}
        out_shape = jax.ShapeDtypeStruct(out.shape, out.dtype)
    else:
        out_shape = jax.ShapeDtypeStruct((m, out_width or n), out_dtype)
    signs = [p[3] for p in pairs]

    def body(*refs):
        o = refs[-1]
        acc = None
        for p in range(n_p):
            t = _bd(refs[2 * p][...], refs[2 * p + 1][...], mode)
            t = t if signs[p] > 0 else -t
            acc = t if acc is None else acc + t
        if add is not None:
            acc = acc + refs[2 * n_p][...].astype(f32)
        o[...] = acc.astype(o.dtype)

    return pl.pallas_call(
        body, name=name, grid=(n // tn, m // tm),
        in_specs=in_specs,
        out_specs=pl.BlockSpec((tm, tn), lambda j, i: (i, j + out_col // tn)),
        out_shape=out_shape, input_output_aliases=io_alias,
        compiler_params=_cparams(2),
    )(*operands)


SCAN_LANES = 512
SUBL = 8


def _cmul(p, q):
    return (p[0] * q[0] - p[1] * q[1], p[0] * q[1] + p[1] * q[0])


def _powers(a):
    a2 = _cmul(a, a)
    a4 = _cmul(a2, a2)
    a6 = _cmul(a4, a2)
    return [a, a2, _cmul(a2, a), a4, _cmul(a4, a), a6, _cmul(a6, a), _cmul(a4, a4)]


def _table(pw, order, w):
    row = _rows((SUBL, w))
    re = sum(jnp.where(row == t, pw[k][0], 0.0) for t, k in enumerate(order))
    im = sum(jnp.where(row == t, pw[k][1], 0.0) for t, k in enumerate(order))
    return re, im


def s5_scan_fwd(bu_re, bu_im, a_re, a_im):
    seq, ns = bu_re.shape
    w, nb = SCAN_LANES, ns // SCAN_LANES

    def body(b_re, b_im, ar, ai, s_re, s_im):
        a = (ar[...], ai[...])
        pw = _powers(a)
        tab = _table(pw, list(range(SUBL)), w)
        row = _rows((SUBL, w))

        def step(i, carry):
            t0 = pl.multiple_of(i * SUBL, SUBL)
            x = (b_re[pl.ds(t0, SUBL), :], b_im[pl.ds(t0, SUBL), :])
            for d, k in ((1, 0), (2, 1), (4, 3)):
                sh = (jnp.where(row < d, 0.0, pltpu.roll(x[0], d, 0)), jnp.where(row < d, 0.0, pltpu.roll(x[1], d, 0)))
                t = _cmul(pw[k], sh)
                x = (x[0] + t[0], x[1] + t[1])
            t = _cmul(tab, carry)
            x = (x[0] + t[0], x[1] + t[1])
            s_re[pl.ds(t0, SUBL), :] = x[0]
            s_im[pl.ds(t0, SUBL), :] = x[1]
            return (x[0][SUBL - 1:, :], x[1][SUBL - 1:, :])

        z = jnp.zeros((1, w), f32)
        lax.fori_loop(0, seq // SUBL, step, (z, z), unroll=2)

    strip = pl.BlockSpec((seq, w), lambda j: (0, j))
    lane = pl.BlockSpec((1, w), lambda j: (0, j))
    return pl.pallas_call(
        body, name="s5_scan_fwd", grid=(nb,),
        in_specs=[strip, strip, lane, lane],
        out_specs=[strip, strip],
        out_shape=[jax.ShapeDtypeStruct((seq, ns), f32)] * 2,
        compiler_params=_cparams(1),
    )(bu_re, bu_im, a_re, a_im)


def s5_scan_bwd(ds_re, ds_im, s_re, s_im, a_re, a_im):
    seq, ns = ds_re.shape
    w, nb = SCAN_LANES, ns // SCAN_LANES
    nblk = seq // SUBL

    def body(g_re, g_im, sr, si, ar, ai, l_re, l_im, da_re, da_im):
        a = (ar[...], -ai[...])
        pw = _powers(a)
        tab = _table(pw, [SUBL - 1 - t for t in range(SUBL)], w)
        row = _rows((SUBL, w))

        def step(kk, carry):
            c_re, c_im, acc_re, acc_im = carry
            i = nblk - 1 - kk
            t0 = pl.multiple_of(i * SUBL, SUBL)
            x = (g_re[pl.ds(t0, SUBL), :], g_im[pl.ds(t0, SUBL), :])
            for d, k in ((1, 0), (2, 1), (4, 3)):
                sh = (jnp.where(row >= SUBL - d, 0.0, pltpu.roll(x[0], SUBL - d, 0)),
                      jnp.where(row >= SUBL - d, 0.0, pltpu.roll(x[1], SUBL - d, 0)))
                t = _cmul(pw[k], sh)
                x = (x[0] + t[0], x[1] + t[1])
            t = _cmul(tab, (c_re, c_im))
            x = (x[0] + t[0], x[1] + t[1])
            l_re[pl.ds(t0, SUBL), :] = x[0]
            l_im[pl.ds(t0, SUBL), :] = x[1]
            tp = jnp.maximum(t0 - 1, 0)
            live = (i > 0).astype(f32)
            p_re = sr[pl.ds(tp, 1), :] * live
            p_im = si[pl.ds(tp, 1), :] * live
            sp_re = jnp.where(row == 0, p_re, pltpu.roll(sr[pl.ds(t0, SUBL), :], 1, 0))
            sp_im = jnp.where(row == 0, p_im, pltpu.roll(si[pl.ds(t0, SUBL), :], 1, 0))
            acc_re = acc_re + x[0] * sp_re + x[1] * sp_im
            acc_im = acc_im + x[1] * sp_re - x[0] * sp_im
            return (x[0][:1, :], x[1][:1, :], acc_re, acc_im)

        z1 = jnp.zeros((1, w), f32)
        z8 = jnp.zeros((SUBL, w), f32)
        _, _, acc_re, acc_im = lax.fori_loop(0, nblk, step, (z1, z1, z8, z8), unroll=2)
        da_re[...] = jnp.sum(acc_re, axis=0, keepdims=True)
        da_im[...] = jnp.sum(acc_im, axis=0, keepdims=True)

    strip = pl.BlockSpec((seq, w), lambda j: (0, j))
    lane = pl.BlockSpec((1, w), lambda j: (0, j))
    return pl.pallas_call(
        body, name="s5_scan_bwd", grid=(nb,),
        in_specs=[strip, strip, strip, strip, lane, lane],
        out_specs=[strip, strip, lane, lane],
        out_shape=[jax.ShapeDtypeStruct((seq, ns), f32)] * 2 + [jax.ShapeDtypeStruct((1, ns), f32)] * 2,
        compiler_params=_cparams(1),
    )(ds_re, ds_im, s_re, s_im, a_re, a_im)


def _rms(x, w):
    return x * lax.rsqrt(jnp.mean(x * x, axis=-1, keepdims=True) + EPS) * w


def f_rms(x, w):
    return (_rms(x, w),)


def f_rms_res(x, w):
    return (_rms(x, w), x)


def f_s5_prep(lam_re, lam_im, log_step, b_re, b_im):
    e = _expand(log_step.shape[1], S5_NS, S5_N)
    step = hdot(jnp.exp(log_step), e)
    mag = jnp.exp(lam_re * step)
    ab_re, ab_im = mag * jnp.cos(lam_im * step), mag * jnp.sin(lam_im * step)
    den = lam_re * lam_re + lam_im * lam_im
    nr = ab_re - 1.0
    coef_re = (nr * lam_re + ab_im * lam_im) / den
    coef_im = (ab_im * lam_re - nr * lam_im) / den
    bb_re, bb_im = coef_re * b_re - coef_im * b_im, coef_re * b_im + coef_im * b_re
    sel = (_rows((BW, S5_P)) % S5_P == _cols((BW, S5_P))).astype(f32)
    blk = _rows((BW, S5_NS)) // S5_P == _cols((BW, S5_NS)) // S5_N
    rows_bd = lambda t: jnp.where(blk, hdot(sel, t), 0.0)
    return (ab_re, ab_im, rows_bd(bb_re), rows_bd(bb_im))


def f_s5_c(c_re, c_im):
    sel_t = (_cols((S5_P, BW)) % S5_P == _rows((S5_P, BW))).astype(f32)
    blk_t = _rows((S5_NS, BW)) // S5_N == _cols((S5_NS, BW)) // S5_P
    cols_bd = lambda t: jnp.where(blk_t, hdot_tn(t, sel_t), 0.0)
    return (cols_bd(c_re), cols_bd(c_im))


def f_s5_act(y_lin, u, gate, d, w_glu):
    y = jax.nn.gelu(y_lin + d * u)
    y = y * jax.nn.sigmoid(bdot_w(y, *w_glu))
    return (y * jax.nn.silu(gate),)


def f_sgu(u, v, gate, ln_w, ln_b, *rest):
    w_s, b_pad = rest[:8], rest[8]
    t = u.shape[0]
    u32, v32 = jax.nn.gelu(u), jax.nn.gelu(v)
    mu = jnp.mean(v32, axis=-1, keepdims=True)
    var = jnp.mean(jnp.square(v32 - mu), axis=-1, keepdims=True)
    vn = (v32 - mu) * lax.rsqrt(var + EPS) * ln_w + ln_b
    tri = _rows((t, t)) >= _cols((t, t))
    s = hdot_tn(b_pad, _expand(LANES, BW, BW // 8))
    for h in range(8):
        s = s + bdot(jnp.where(tri, w_s[h], 0.0), vn) * _lane_mask(BW, 64 * h, 64 * h + 64)
    return (u32 * s * jax.nn.silu(gate),)


def f_m2_conv(x, w, b):
    return (sum(_row_of(w, k) * shift(x, 3 - k) for k in range(4)) + b,)


def f_sc(bg, cg, h, gate, w):
    z = cg * h
    conv = sum(_row_of(w, k) * shift(z, 2 - k) for k in range(3))
    return (bg * conv * jax.nn.silu(gate),)


def f_m2(z, xc, b0, b1, c0, c1, dt_raw, dt_bias, a_log, d_par, norm_w, st):
    q = z.shape[0]
    x = jax.nn.silu(xc)
    bm, cm = (jax.nn.silu(b0), jax.nn.silu(b1)), (jax.nn.silu(c0), jax.nn.silu(c1))
    dt = jax.nn.softplus(dt_raw + dt_bias)
    da = dt * (-jnp.exp(a_log))
    tri = _rows((q, q)) >= _cols((q, q))
    acs = hdot(tri.astype(f32), da)
    e = _expand(LANES, BW, M2_HEAD_DIM)
    dt_f, acs_f = hdot(dt, e), hdot(acs, e)
    last = _rows((q, BW)) == q - 1
    alast_f = jnp.sum(jnp.where(last, acs_f, 0.0), axis=0, keepdims=True)
    xdt = x * dt_f
    xdec = xdt * jnp.exp(alast_f - acs_f)
    acs_t = acs.T
    st_new = st * jnp.exp(alast_f)
    y_diag, y_off = 0.0, 0.0
    for g in range(M2_GROUPS):
        gm = _lane_mask(BW, 256 * g, 256 * g + 256)
        cb = bdot_nt(cm[g], bm[g])
        st_new = st_new + bdot_tn(bm[g], xdec * gm)
        y_off = y_off + bdot(cm[g], st) * gm
        for hh in range(M2_HEADS // M2_GROUPS):
            h = g * (M2_HEADS // M2_GROUPS) + hh
            col = jnp.sum(jnp.where(_cols((q, LANES)) == h, acs, 0.0), axis=1, keepdims=True)
            row = jnp.sum(jnp.where(_rows((LANES, q)) == h, acs_t, 0.0), axis=0, keepdims=True)
            decay = jnp.exp(jnp.where(tri, col - row, -1e30))
            y_diag = y_diag + bdot(cb * decay, xdt) * _lane_mask(BW, 64 * h, 64 * h + 64)
    d_f = sum(jnp.sum(jnp.where(_cols((1, LANES)) == h, d_par, 0.0), axis=1, keepdims=True)
              * _lane_mask(BW, 64 * h, 64 * h + 64) for h in range(M2_HEADS))
    y = y_diag + y_off * jnp.exp(acs_f) + d_f * x
    y = y * jax.nn.silu(z)
    return (_rms(y, norm_w), st_new)


def f_branch_mix(*v):
    nb = len(v) // 4
    ys, lg, wb, mb = v[0:nb], v[nb:2 * nb], v[2 * nb:3 * nb], v[3 * nb:]
    return (sum(jax.nn.sigmoid(lg[k] + mb[k]) * bdot_w(ys[k], *wb[k]) for k in range(nb)),)


def _param(arr, shadow=False):
    return A(arr, tuple(arr.shape), lambda i, nd=arr.ndim: (0,) * nd, shadow)


def _tb(arr, t, width, colblk):
    return A(arr, (t, width), lambda i: (i, colblk))


def _strip(arr, seq, colblk0, stride=1):
    return A(arr, (seq, LANES), lambda j: (0, colblk0 + stride * j))


def _s5_prep_args(p):
    lam_re = p["s5_lambda_re"].reshape(1, S5_NS)
    lam_im = p["s5_lambda_im"].reshape(1, S5_NS)
    log_step = jnp.pad(p["s5_log_step"].reshape(1, S5_GROUPS), ((0, 0), (0, LANES - S5_GROUPS)))
    b_lanes = lambda b: jnp.transpose(b, (2, 0, 1)).reshape(S5_P, S5_NS)
    return [_param(v) for v in (lam_re, lam_im, log_step, b_lanes(p["s5_b_re"]), b_lanes(p["s5_b_im"]))]


def _s5_c_args(p):
    c_lanes = lambda c: jnp.transpose(c, (1, 0, 2)).reshape(S5_P, S5_NS)
    return [_param(c_lanes(p["s5_c_re"])), _param(c_lanes(p["s5_c_im"]))]


def layer_fwd(x, p):
    seq = x.shape[0]
    nt = seq // CHUNK
    t2 = 256
    sv = {}
    t4 = min(512, seq)
    hb = block_fwd("rms_in", f_rms, seq // t4, [_tb(x, t4, D_MODEL, 0), _param(p["norm_w"].reshape(1, D_MODEL))],
                   [O((seq, D_MODEL), bf16, (t4, D_MODEL), lambda i: (i, 0))])[0]
    proj = mm("mm_in", NN, [(Op(hb), Op(p["w_in"]), D_MODEL, 1)], seq, PW, 2048, 1152)
    whole = lambda shape, dt: O(shape, dt, shape, lambda i: (0, 0))
    ab_re, ab_im, bb_re, bb_im = block_fwd("s5_prep", f_s5_prep, 1, _s5_prep_args(p),
                                           [whole((1, S5_NS), f32)] * 2 + [whole((BW, S5_NS), bf16)] * 2)
    cc_re, cc_im = block_fwd("s5_c", f_s5_c, 1, _s5_c_args(p), [whole((S5_NS, BW), bf16)] * 2)
    bu_re = mm("mm_bu_re", NN, [(Op(proj, 0, C_S5U), Op(bb_re), BW, 1)], seq, S5_NS, 1024, 1024)
    bu_im = mm("mm_bu_im", NN, [(Op(proj, 0, C_S5U), Op(bb_im), BW, 1)], seq, S5_NS, 1024, 1024)
    s_re, s_im = s5_scan_fwd(bu_re, bu_im, ab_re, ab_im)
    y_lin = mm("mm_s5y", NN, [(Op(s_re), Op(cc_re), S5_NS, 1), (Op(s_im), Op(cc_im), S5_NS, -1)], seq, BW, 512, BW)
    s5_act_args = [_tb(y_lin, t4, BW, 0), _tb(proj, t4, BW, C_S5U // BW), _tb(proj, t4, BW, C_S5G // BW),
                   _param(p["s5_d"].reshape(1, BW)), _param(p["s5_w_glu"], True)]
    out_bw =O((seq, BW), f32, (CHUNK, BW), lambda i: (i, 0))
    y_a = block_fwd("s5_act", f_s5_act, seq // t4, s5_act_args, [O((seq, BW), f32, (t4, BW), lambda i: (i, 0))])[0]
    y_b = block_fwd("sgu", f_sgu, nt, _sgu_args(proj, p), [out_bw])[0]
    xc = block_fwd("m2_conv", f_m2_conv, 8, _m2_conv_args(proj, p, seq),
                   [O((seq, 2 * BW), f32, (seq, LANES), lambda j: (0, j))])[0]
    y_c, st_saved = block_fwd("m2_ssd", f_m2, nt, _m2_args(proj, xc, p), [out_bw], carries=[(M2_STATE, BW)])
    y_d = block_fwd("sc", f_sc, 4, _sc_args(proj, p, seq), [O((seq, BW), f32, (seq, LANES), lambda j: (0, j))])[0]
    ys = [y_a, y_b, y_c, y_d]
    merged = block_fwd("branch_mix", f_branch_mix, seq // t2, _mix_args(ys, proj, p, t2),
                       [O((seq, D_MODEL), bf16, (t2, D_MODEL), lambda i: (i, 0))])[0]
    x_new = mm("mm_out", NN, [(Op(merged), Op(p["w_out"]), D_MODEL, 1)], seq, D_MODEL, 1024, D_MODEL, add=Op(x))
    sv.update(x=x, hb=hb, proj=proj, ab=(ab_re, ab_im), bb=(bb_re, bb_im), cc=(cc_re, cc_im), s=(s_re, s_im), y_lin=y_lin,
              xc=xc, st=st_saved, ys=ys, merged=merged)
    return x_new, sv


def _sgu_args(proj, p):
    c0 = C_SGU // BW
    args = [_tb(proj, CHUNK, BW, c0), _tb(proj, CHUNK, BW, c0 + 1), _tb(proj, CHUNK, BW, c0 + 2),
            _param(p["sgu_ln_w"].reshape(1, BW)), _param(p["sgu_ln_b"].reshape(1, BW))]
    args += [A(p["sgu_w"], (None, CHUNK, CHUNK), lambda i, h=h: (h, 0, 0)) for h in range(8)]
    args.append(_param(jnp.pad(p["sgu_b"], ((0, LANES - 8), (0, 0)))))
    return args


def _m2_conv_args(proj, p, seq):
    return [_strip(proj, seq, C_XBC // LANES), A(p["m2_conv_w"], (4, LANES), lambda j: (0, j)),
            A(p["m2_conv_b"].reshape(1, 2 * BW), (1, LANES), lambda j: (0, j))]


def _pad_lanes(v):
    return jnp.pad(v.reshape(1, -1), ((0, 0), (0, LANES - v.size)))


def _m2_args(proj, xc, p):
    args = [_tb(proj, CHUNK, BW, C_M2Z // BW), _tb(xc, CHUNK, BW, 0)]
    args += [_tb(xc, CHUNK, LANES, 4 + k) for k in range(4)]
    args.append(_tb(proj, CHUNK, LANES, C_DT // LANES))
    args += [_param(_pad_lanes(p["m2_dt_bias"])), _param(_pad_lanes(p["m2_a_log"])), _param(_pad_lanes(p["m2_d"])),
             _param(p["m2_norm_w"].reshape(1, BW))]
    return args


def _sc_args(proj, p, seq):
    c0 = C_SC // LANES
    return [_strip(proj, seq, c0 + k, 4) for k in range(4)] + [A(p["sc_conv_w"], (3, LANES), lambda j: (0, j))]


def _mix_args(ys, proj, p, t, ks=range(N_BRANCH)):
    args = [_tb(ys[k], t, BW, 0) for k in ks]
    args += [_tb(proj, t, D_MODEL, k) for k in ks]
    args += [A(p["w_branch"], (BW, D_MODEL), lambda i, k=k: (k, 0), True) for k in ks]
    mb = p["merge_b"].reshape(N_BRANCH, 1, D_MODEL)
    args += [A(mb, (None, 1, D_MODEL), lambda i, k=k: (k, 0, 0)) for k in ks]
    return args


def layer_bwd(d_out, p, sv):
    seq = d_out.shape[0]
    nt = seq // CHUNK
    t4 = min(512, seq)
    proj, ys = sv["proj"], sv["ys"]
    g = {}
    acc = lambda shape: O(tuple(shape), f32, tuple(shape), lambda i, nd=len(shape): (0,) * nd)
    d_merged = mm("mm_out_dx", NT, [(Op(d_out), Op(p["w_out"]), D_MODEL, 1)], seq, D_MODEL, 1024, D_MODEL, out_dtype=f32)
    g["w_out"] = mm("mm_out_dw", TN, [(Op(sv["merged"]), Op(d_out), seq, 1)], D_MODEL, D_MODEL, 512, D_MODEL, out_dtype=bf16)
    dys, dwb, dmb, dproj, tb = [], [], [], None, min(256, seq)
    for half in range(2):
        gouts = [O((seq, BW), f32, (tb, BW), lambda i: (i, 0)) for _ in range(2)]
        gouts.append(O((seq, PW_MAIN), bf16, (tb, 2 * D_MODEL), lambda i, half=half: (i, half), alias=dproj))
        gouts += [acc((BW, D_MODEL)) for _ in range(2)] + [acc((1, D_MODEL)) for _ in range(2)]
        routes = [R(k, k) for k in range(2)] + [R(2 + k, 2, k * D_MODEL) for k in range(2)]
        routes += [R(4 + k, 3 + k, acc=True) for k in range(2)] + [R(6 + k, 5 + k, acc=True) for k in range(2)]
        res = block_bwd(f"branch_mix_bwd{half}", f_branch_mix, seq // tb, _mix_args(ys, proj, p, tb, (2 * half, 2 * half + 1)),
                        [_tb(d_merged, tb, D_MODEL, 0)], gouts, routes)
        dys, dproj, dwb, dmb = dys + list(res[:2]), res[2], dwb + list(res[3:5]), dmb + list(res[5:7])
    g["w_branch"] = jnp.stack(dwb)
    g["merge_b"] = jnp.concatenate(dmb, axis=0)
    res = block_bwd("sc_bwd", f_sc, 4, _sc_args(proj, p, seq), [_strip(dys[3], seq, 0)],
                    [O((seq, PW_MAIN), bf16, (seq, 4 * LANES), lambda j: (0, C_SC // (4 * LANES) + j), alias=dproj),
                     O((3, BW), f32, (3, LANES), lambda j: (0, j))],
                    [R(k, 0, k * LANES) for k in range(4)] + [R(4, 1)])
    dproj, g["sc_conv_w"] = res
    res = block_bwd("m2_ssd_bwd", f_m2, nt, _m2_args(proj, sv["xc"], p), [_tb(dys[2], CHUNK, BW, 0)],
                    [O((seq, PW_MAIN), bf16, (CHUNK, BW), lambda i: (i, C_M2Z // BW), alias=dproj),
                     O((seq, 2 * BW), f32, (CHUNK, 2 * BW), lambda i: (i, 0)),
                     O((seq, LANES), bf16, (CHUNK, LANES), lambda i: (i, 0)),
                     acc((1, LANES)), acc((1, LANES)), acc((1, LANES)), acc((1, BW))],
                    [R(0, 0), R(1, 1, 0)] + [R(2 + k, 1, BW + k * LANES) for k in range(4)] + [R(6, 2)]
                    + [R(7, 3, acc=True), R(8, 4, acc=True), R(9, 5, acc=True), R(10, 6, acc=True)],
                    saved=[sv["st"]], rev=True)
    dproj, dxc, d_dt = res[0], res[1], res[2]
    g["m2_dt_bias"], g["m2_a_log"], g["m2_d"] = (r[0, :M2_HEADS] for r in res[3:6])
    g["m2_norm_w"] = res[6].reshape(BW)
    res = block_bwd("m2_conv_bwd", f_m2_conv, 8, _m2_conv_args(proj, p, seq), [_strip(dxc, seq, 0)],
                    [O((seq, PW_MAIN), bf16, (seq, LANES), lambda j: (0, C_XBC // LANES + j), alias=dproj),
                     O((4, 2 * BW), f32, (4, LANES), lambda j: (0, j)), O((1, 2 * BW), f32, (1, LANES), lambda j: (0, j))],
                    [R(0, 0), R(1, 1), R(2, 2)])
    dproj, g["m2_conv_w"], cb = res
    g["m2_conv_b"] = cb.reshape(2 * BW)
    res = block_bwd("sgu_bwd", f_sgu, nt, _sgu_args(proj, p), [_tb(dys[1], CHUNK, BW, 0)],
                    [O((seq, PW_MAIN), bf16, (CHUNK, 3 * BW), lambda i: (i, C_SGU // (3 * BW)), alias=dproj),
                     acc((1, BW)), acc((1, BW))] + [acc((CHUNK, CHUNK)) for _ in range(8)] + [acc((LANES, CHUNK))],
                    [R(0, 0, 0), R(1, 0, BW), R(2, 0, 2 * BW), R(3, 1, acc=True), R(4, 2, acc=True)]
                    + [R(5 + h, 3 + h, acc=True) for h in range(8)] + [R(13, 11, acc=True)])
    dproj = res[0]
    g["sgu_ln_w"], g["sgu_ln_b"] = res[1].reshape(BW), res[2].reshape(BW)
    g["sgu_w"] = jnp.stack(res[3:11])
    g["sgu_b"] = res[11][:8]
    y_lin, (s_re, s_im), (ab_re, ab_im) = sv["y_lin"], sv["s"], sv["ab"]
    s5_act_args = [_tb(y_lin, t4, BW, 0), _tb(proj, t4, BW, C_S5U // BW), _tb(proj, t4, BW, C_S5G // BW),
                   _param(p["s5_d"].reshape(1, BW)), _param(p["s5_w_glu"], True)]
    res = block_bwd("s5_act_bwd", f_s5_act, seq // t4, s5_act_args, [_tb(dys[0], t4, BW, 0)],
                    [O((seq, BW), bf16, (t4, BW), lambda i: (i, 0)), O((seq, BW), f32, (t4, BW), lambda i: (i, 0)),
                     O((seq, PW_MAIN), bf16, (t4, BW), lambda i: (i, C_S5G // BW), alias=dproj),
                     acc((1, BW)), acc((BW, BW))],
                    [R(0, 0), R(1, 1), R(2, 2), R(3, 3, acc=True), R(4, 4, acc=True)])
    dy_lin, du1, dproj = res[0], res[1], res[2]
    g["s5_d"] = res[3].reshape(S5_GROUPS, S5_P)
    g["s5_w_glu"] = res[4]
    (bb_re, bb_im), (cc_re, cc_im) = sv["bb"], sv["cc"]
    ds_re = mm("mm_s5y_dre", NT, [(Op(dy_lin), Op(cc_re), BW, 1)], seq, S5_NS, 1024, 1024)
    ds_im = mm("mm_s5y_dim", NT, [(Op(dy_lin), Op(cc_im), BW, -1)], seq, S5_NS, 1024, 1024)
    dc_re = mm("mm_s5y_dcre", TN, [(Op(s_re), Op(dy_lin), seq, 1)], S5_NS, BW, 512, BW)
    dc_im = mm("mm_s5y_dcim", TN, [(Op(s_im), Op(dy_lin), seq, -1)], S5_NS, BW, 512, BW)
    l_re, l_im, da_re, da_im = s5_scan_bwd(ds_re, ds_im, s_re, s_im, ab_re, ab_im)
    dproj = mm("mm_bu_dx", NT, [(Op(l_re), Op(bb_re), S5_NS, 1), (Op(l_im), Op(bb_im), S5_NS, 1)],
               seq, BW, 512, BW, add=Op(du1), out=dproj, out_col=C_S5U)
    dbb = [mm(f"mm_bu_dw{n}", TN, [(Op(proj, 0, C_S5U), Op(l), seq, 1)], BW, S5_NS, 512, 1024)
           for n, l in (("re", l_re), ("im", l_im))]
    gouts = [acc((1, S5_NS)), acc((1, S5_NS)), acc((1, LANES))] + [acc((S5_P, S5_NS))] * 2
    res = block_bwd("s5_prep_bwd", f_s5_prep, 1, _s5_prep_args(p),
                    [_param(v) for v in (da_re, da_im, dbb[0], dbb[1])], gouts, [R(k, k, acc=True) for k in range(5)])
    g["s5_lambda_re"], g["s5_lambda_im"] = res[0].reshape(S5_GROUPS, S5_N), res[1].reshape(S5_GROUPS, S5_N)
    g["s5_log_step"] = res[2][0, :S5_GROUPS]
    b_natural = lambda b: jnp.transpose(b.reshape(S5_P, S5_GROUPS, S5_N), (1, 2, 0))
    c_natural = lambda c: jnp.transpose(c.reshape(S5_P, S5_GROUPS, S5_N), (1, 0, 2))
    g["s5_b_re"], g["s5_b_im"] = b_natural(res[3]), b_natural(res[4])
    res = block_bwd("s5_c_bwd", f_s5_c, 1, _s5_c_args(p), [_param(dc_re), _param(dc_im)],
                    [acc((S5_P, S5_NS))] * 2, [R(0, 0, acc=True), R(1, 1, acc=True)])
    g["s5_c_re"], g["s5_c_im"] = c_natural(res[0]), c_natural(res[1])
    hb, w_in = sv["hb"], p["w_in"]
    dh = mm("mm_in_dx", NT, [(Op(dproj), Op(w_in), PW_MAIN, 1), (Op(d_dt), Op(w_in, 0, PW_MAIN), LANES, 1)],
            seq, D_MODEL, 512, 512)
    dw = mm("mm_in_dw", TN, [(Op(hb), Op(dproj), seq, 1)], D_MODEL, PW_MAIN, 1024, 2048, out_dtype=bf16, out_width=PW)
    g["w_in"] = mm("mm_in_dwdt", TN, [(Op(hb), Op(d_dt), seq, 1)], D_MODEL, LANES, 256, LANES, out=dw, out_col=C_DT)
    dx, dnw = block_bwd("rms_in_bwd", f_rms_res, seq // t4, [_tb(sv["x"], t4, D_MODEL, 0), _param(p["norm_w"].reshape(1, D_MODEL))],
                        [_tb(dh, t4, D_MODEL, 0), _tb(d_out, t4, D_MODEL, 0)],
                        [O((seq, D_MODEL), f32, (t4, D_MODEL), lambda i: (i, 0)), acc((1, D_MODEL))],
                        [R(0, 0), R(1, 1, acc=True)])
    g["norm_w"] = dnw.reshape(D_MODEL)
    return dx, g


def loss_head(x, w, target):
    seq = x.shape[0]
    t = min(512, seq)

    def body(x_ref, w_ref, t_ref, loss_ref, dx_ref, dw_ref):
        step = pl.program_id(0)

        def f(xv, wv):
            err = _rms(xv, wv) - t_ref[...]
            return 0.5 * jnp.sum(jnp.mean(err * err, axis=-1, keepdims=True), axis=0, keepdims=True)

        val, vjp_fn = jax.vjp(f, x_ref[...], w_ref[...])
        dx, dw = vjp_fn(jnp.ones((1, 1), f32))
        dx_ref[...] = dx

        @pl.when(step == 0)
        def _():
            loss_ref[...] = jnp.broadcast_to(val, loss_ref.shape)
            dw_ref[...] = dw

        @pl.when(step > 0)
        def _():
            loss_ref[...] += jnp.broadcast_to(val, loss_ref.shape)
            dw_ref[...] += dw

    blk = pl.BlockSpec((t, D_MODEL), lambda i: (i, 0))
    row = pl.BlockSpec((1, D_MODEL), lambda i: (0, 0))
    return pl.pallas_call(
        body, name="loss_head", grid=(seq // t,),
        in_specs=[blk, row, blk],
        out_specs=[pl.BlockSpec((1, LANES), lambda i: (0, 0)), blk, row],
        out_shape=[jax.ShapeDtypeStruct((1, LANES), f32), jax.ShapeDtypeStruct((seq, D_MODEL), f32),
                   jax.ShapeDtypeStruct((1, D_MODEL), f32)],
        compiler_params=_cparams(1),
    )(x, w.reshape(1, D_MODEL), target)


LAYER_KEYS = ("norm_w", "w_in", "s5_lambda_re", "s5_lambda_im", "s5_b_re", "s5_b_im", "s5_c_re", "s5_c_im", "s5_d",
              "s5_log_step", "s5_w_glu", "sgu_ln_w", "sgu_ln_b", "sgu_w", "sgu_b", "m2_conv_w", "m2_conv_b",
              "m2_dt_bias", "m2_a_log", "m2_d", "m2_norm_w", "sc_conv_w", "merge_b", "w_branch", "w_out")


def local_step(x, target, layers, final_norm_w):
    saved = []
    for p in layers:
        x, sv = layer_fwd(x, p)
        saved.append(sv)
    loss, dx, dfw = loss_head(x, final_norm_w, target)
    grads = []
    for p, sv in zip(reversed(layers), reversed(saved)):
        dx, g = layer_bwd(dx, p, sv)
        grads.append(g)
    return loss[0, 0], dx, grads[::-1], dfw.reshape(D_MODEL)


MESH = pl.DeviceIdType.MESH
ANY = pl.BlockSpec(memory_space=pl.ANY)


def _me():
    return lax.axis_index("x"), lax.axis_index("y"), lax.axis_index("c")


def _other_chips(x, y):
    return [(1 - x, y), (x, 1 - y), (1 - x, 1 - y)]


def _rcopy(src, dst, send, recv, dev):
    return pltpu.make_async_remote_copy(src_ref=src, dst_ref=dst, send_sem=send, recv_sem=recv,
                                        device_id=dev, device_id_type=MESH)


def _route_cut(rows, dtype):
    tile = 2 * SUBL * (4 // jnp.dtype(dtype).itemsize)
    return rows // 2 if rows % tile == 0 else rows


def _comm_call(name, body, arrs, out_shape, n_remote, aliases=None):
    n = len(arrs)
    return pl.pallas_call(
        body, name=name, in_specs=[ANY] * n, out_specs=[ANY] * len(out_shape), out_shape=out_shape,
        scratch_shapes=[pltpu.SemaphoreType.DMA((n, n_remote)), pltpu.SemaphoreType.DMA((n, n_remote))],
        input_output_aliases=aliases or {},
        compiler_params=pltpu.CompilerParams(has_side_effects=True),
    )(*arrs)


def gather_chips(name, arrs):
    n = len(arrs)
    cut = [_route_cut(a.shape[1], a.dtype) for a in arrs]

    def body(*refs):
        ins, outs = refs[:n], refs[n:2 * n]
        send, recv = refs[2 * n:]
        x, y, c = _me()
        jme, jx, jy, jd = 2 * x + y, 2 * (1 - x) + y, 2 * x + 1 - y, 2 * (1 - x) + 1 - y
        to_x, to_y, sib = (1 - x, y, c), (x, 1 - y, c), (x, y, 1 - c)

        def part(ref, a, hi):
            return ref.at[pl.ds(cut[a], ref.shape[0] - cut[a])] if hi else ref.at[pl.ds(0, cut[a])]

        def cp(a, k, ref, dev):
            return _rcopy(ref, ref, send.at[a, k], recv.at[a, k], dev)

        split = [a for a in range(n) if cut[a] < arrs[a].shape[1]]
        sent = [_rcopy(ins[a].at[c], outs[a].at[jme, c], send.at[a, k], recv.at[a, k], dev)
                for a in range(n) for k, dev in ((0, to_x), (1, to_y))]
        for s in sent:
            s.start()
        for a in range(n):
            blk = outs[a].at[jx, c]
            cp(a, 0, blk, to_x).wait_recv()
            sent += [cp(a, 2, part(blk, a, False), to_y), cp(a, 4, blk, sib)]
            sent[-2].start()
            sent[-1].start()
        for a in range(n):
            blk = outs[a].at[jy, c]
            cp(a, 1, blk, to_y).wait_recv()
            sent.append(cp(a, 5, blk, sib))
            sent[-1].start()
            if a in split:
                sent.append(cp(a, 3, part(blk, a, True), to_x))
                sent[-1].start()
        for a in range(n):
            lo = part(outs[a].at[jd, c], a, False)
            cp(a, 2, lo, to_y).wait_recv()
            sent.append(cp(a, 6, lo, sib))
            sent[-1].start()
        for a in split:
            hi = part(outs[a].at[jd, c], a, True)
            cp(a, 3, hi, to_x).wait_recv()
            sent.append(cp(a, 7, hi, sib))
            sent[-1].start()
        for a in range(n):
            cp(a, 4, outs[a].at[jx, 1 - c], sib).wait_recv()
            cp(a, 5, outs[a].at[jy, 1 - c], sib).wait_recv()
            cp(a, 6, part(outs[a].at[jd, 1 - c], a, False), sib).wait_recv()
        for a in split:
            cp(a, 7, part(outs[a].at[jd, 1 - c], a, True), sib).wait_recv()
        for s in sent:
            s.wait_send()

    out_shape = [jax.ShapeDtypeStruct((4,) + a.shape, a.dtype) for a in arrs]
    got = _comm_call(name, body, arrs, out_shape, 8)
    jme = 2 * lax.axis_index("x") + lax.axis_index("y")
    return [lax.dynamic_update_index_in_dim(g, a, jme, 0) for g, a in zip(got, arrs)]


def swap_halves(name, arrs):
    n = len(arrs)

    def body(*refs):
        ins, outs = refs[:n], refs[n:2 * n]
        send, recv = refs[2 * n:]
        x, y, c = _me()
        remote = [_rcopy(ins[a].at[1 - c], outs[a], send.at[a, 0], recv.at[a, 0], (x, y, 1 - c)) for a in range(n)]
        for cp in remote:
            cp.start()
        for cp in remote:
            cp.wait()

    return _comm_call(name, body, arrs, [jax.ShapeDtypeStruct(a.shape[1:], a.dtype) for a in arrs], 1)


def exchange_chips(name, arrs):
    n = len(arrs)

    def body(*refs):
        ins, outs = refs[:n], refs[n:2 * n]
        send, recv = refs[2 * n:]
        x, y, c = _me()
        remote = [_rcopy(ins[a].at[2 * cx + cy], outs[a].at[k], send.at[a, k], recv.at[a, k], (cx, cy, c))
                  for a in range(n) for k, (cx, cy) in enumerate(_other_chips(x, y))]
        for cp in remote:
            cp.start()
        for cp in remote:
            cp.wait()

    return _comm_call(name, body, arrs, [jax.ShapeDtypeStruct((3,) + a.shape[1:], a.dtype) for a in arrs], 3)


def gather_cores(name, arrs):
    n = len(arrs)

    def body(*refs):
        bufs = refs[n:2 * n]
        send, recv = refs[2 * n:]
        x, y, c = _me()
        remote = [_rcopy(bufs[a].at[c], bufs[a].at[c], send.at[a, 0], recv.at[a, 0], (x, y, 1 - c)) for a in range(n)]
        for cp in remote:
            cp.start()
        for a in range(n):
            _rcopy(bufs[a].at[1 - c], bufs[a].at[1 - c], send.at[a, 0], recv.at[a, 0], (x, y, 1 - c)).wait_recv()
        for cp in remote:
            cp.wait_send()

    return _comm_call(name, body, arrs, [jax.ShapeDtypeStruct(a.shape, a.dtype) for a in arrs], 1,
                      aliases={a: a for a in range(n)})


ROW_BLOCK = 512


def esum(name, terms, rows, width, out_dtype, out_slots=None):
    tr = next((t for t in range(min(rows, ROW_BLOCK), 0, -SUBL) if rows % t == 0 and t % SUBL == 0), rows)
    where =jnp.stack([lax.axis_index("c"), 2 * lax.axis_index("x") + lax.axis_index("y")]).astype(jnp.int32)
    pick = {"c": 0, "j": 1}

    def body(s_ref, *refs):
        acc = refs[0][...].astype(f32)
        for r in refs[1:-1]:
            acc = acc + r[...].astype(f32)
        refs[-1][...] = acc.astype(out_dtype)

    specs = []
    for arr, lead in terms:
        if lead is None:
            specs.append(pl.BlockSpec((tr, width), lambda i, s: (i, 0)))
        elif isinstance(lead, str):
            specs.append(pl.BlockSpec((None, tr, width), lambda i, s, lead=lead: (s[pick[lead]], i, 0)))
        else:
            specs.append(pl.BlockSpec((None, tr, width), lambda i, s, lead=lead: (lead, i, 0)))
    if out_slots is None:
        out_spec = pl.BlockSpec((tr, width), lambda i, s: (i, 0))
        out_shape = jax.ShapeDtypeStruct((rows, width), out_dtype)
    else:
        out_spec = pl.BlockSpec((None, tr, width), lambda i, s: (s[0], i, 0))
        out_shape = jax.ShapeDtypeStruct((out_slots, rows, width), out_dtype)
    return pl.pallas_call(
        body, name=name,
        grid_spec=pltpu.PrefetchScalarGridSpec(num_scalar_prefetch=1, grid=(rows // tr,), in_specs=specs, out_specs=out_spec),
        out_shape=out_shape, compiler_params=_cparams(1),
    )(where, *[t[0] for t in terms])


def reduce_to_shards(parts):
    tags = [str(k) for k in range(len(parts))]
    theirs = swap_halves("rs_swap", parts)
    t1 = []
    for tag, p, th in zip(tags, parts, theirs):
        _, _, h, w = p.shape
        t1.append(esum("rs_add_cores" + tag, [(p.reshape(2, 4 * h, w), "c"), (th.reshape(4 * h, w), None)],
                       4 * h, w, p.dtype).reshape(4, h, w))
    landed = exchange_chips("rs_exchange", t1)
    red = []
    for tag, p, t, got in zip(tags, parts, t1, landed):
        _, _, h, w = p.shape
        red.append(esum("rs_add_chips" + tag, [(t, "j"), (got, 0), (got, 1), (got, 2)], h, w, f32, out_slots=2))
    return gather_cores("rs_gather", red)


def _adamw_step(w_ref, g_ref, m_ref, v_ref, d_ref, nm_ref, nv_ref):
    gv = g_ref[...]
    nm = ADAM_B1 * m_ref[...] + (1.0 - ADAM_B1) * gv
    nv = ADAM_B2 * v_ref[...] + (1.0 - ADAM_B2) * jnp.square(gv)
    m_hat = nm / (1.0 - ADAM_B1 ** ADAM_STEP)
    v_hat = nv / (1.0 - ADAM_B2 ** ADAM_STEP)
    d_ref[...] = -ADAM_LR * (m_hat / (jnp.sqrt(v_hat) + ADAM_EPS) + ADAM_WD * w_ref[...])
    nm_ref[...] = nm
    nv_ref[...] = nv


def adamw(name, w, g, m, v, tr=None):
    rows, rest = w.shape[0], w.shape[1:]
    if tr is None:
        tr = ROW_BLOCK if rows % ROW_BLOCK == 0 else rows
    assert rows % tr == 0

    def body(*refs):
        _adamw_step(*refs)

    spec = pl.BlockSpec((tr,) + rest, lambda i: (i,) + (0,) * len(rest))
    return pl.pallas_call(
        body, name=name, grid=(rows // tr,), in_specs=[spec] * 4, out_specs=[spec] * 3,
        out_shape=[jax.ShapeDtypeStruct(w.shape, f32)] * 3, compiler_params=_cparams(1),
    )(w, g, m, v)


def adamw_many(name, ws, gs, ms, vs):
    n = len(ws)

    def body(*refs):
        ins, outs = refs[:4 * n], refs[4 * n:]
        for k in range(n):
            _adamw_step(ins[k], ins[n + k], ins[2 * n + k], ins[3 * n + k], outs[k], outs[n + k], outs[2 * n + k])

    vmem = pl.BlockSpec(memory_space=pltpu.VMEM)
    res = pl.pallas_call(
        body, name=name, in_specs=[vmem] * (4 * n), out_specs=[vmem] * (3 * n),
        out_shape=[jax.ShapeDtypeStruct(w.shape, f32) for w in ws] * 3,
        compiler_params=pltpu.CompilerParams(vmem_limit_bytes=VMEM_LIMIT),
    )(*ws, *gs, *ms, *vs)
    return res[:n], res[n:2 * n], res[2 * n:]


PACK_W = 1024


def _pack(parts, halves, row_mult):
    flat = jnp.concatenate([p.reshape(-1) for p in parts])
    per = halves * row_mult * PACK_W
    total = -(-flat.size // per) * per
    flat = jnp.pad(flat, (0, total - flat.size))
    return flat.reshape(halves, total // (halves * PACK_W), PACK_W)


def _unpack(flat, shapes):
    out, pos = [], 0
    for s in shapes:
        n = int(np.prod(s))
        out.append(flat[pos:pos + n].reshape(s))
        pos += n
    return out


SHARDED_BIG = ("w_in", "w_branch", "w_out", "s5_w_glu")
SHARDED_SMALL = ("m2_conv_w", "sc_conv_w", "merge_b")
SHARD_AXIS = {"w_in": 2, "w_branch": 3, "w_out": 1, "s5_w_glu": 1, "m2_conv_w": 2, "sc_conv_w": 2, "merge_b": 2}
REPLICATED = ("norm_w", "s5_lambda_re", "s5_lambda_im", "s5_b_re", "s5_b_im", "s5_c_re", "s5_c_im", "s5_d", "s5_log_step",
              "sgu_ln_w", "sgu_ln_b", "sgu_w", "sgu_b", "m2_conv_b", "m2_dt_bias", "m2_a_log", "m2_d", "m2_norm_w")
WEIGHTS = ("norm_w", "w_in", "s5_lambda_re", "s5_lambda_im", "s5_b_re", "s5_b_im", "s5_c_re", "s5_c_im", "s5_d",
           "s5_log_step", "s5_w_glu", "sgu_ln_w", "sgu_ln_b", "sgu_w", "sgu_b", "m2_conv_w", "m2_conv_b", "m2_dt_bias",
           "m2_a_log", "m2_d", "m2_norm_w", "sc_conv_w", "merge_b", "w_branch", "w_out", "final_norm_w")
N_LAYERS = 2


SHARD_W = IN_DIM // 4
SHARD_PAD = -(-SHARD_W // LANES) * LANES
REGROUP_W = 3 * LANES


def _kernel_pieces():
    out, pos = [], 0
    for s, n in _col_segments():
        while n:
            take = min(n, SHARD_W - s % SHARD_W)
            out.append((pos, s, take))
            pos, s, n = pos + take, s + take, n - take
    return out


def regroup_cols(name, src, steps, out_shape, out=None):
    n_src, rows, width = src.shape
    ow, win = REGROUP_W, REGROUP_W + LANES
    k_max = max(len(p) for _, _, p in steps)
    assert width % LANES == 0

    def body(*refs):
        src_ref, out_ref = refs[0], refs[-5]
        wbuf, obuf, sem_in, sem_out = refs[-4:]

        def fetch(q, slot):
            started = []
            for p, (s, col, lo, hi) in enumerate(steps[q][2]):
                w0 = col // LANES * LANES
                wlen = min(win, width - w0)
                cp = pltpu.make_async_copy(src_ref.at[s, :, pl.ds(w0, wlen)], wbuf.at[slot, p, :, pl.ds(0, wlen)], sem_in.at[slot, p])
                cp.start()
                started.append((cp, wlen, col - w0 - lo, lo, hi))
            return started

        pend, writes = fetch(0, 0), [None, None]
        for q, (t, b, _) in enumerate(steps):
            slot = q % 2
            nxt = fetch(q + 1, 1 - slot) if q + 1 < len(steps) else []
            acc = [jnp.zeros((rows, LANES), f32) for _ in range(ow // LANES)]
            for p, (cp, wlen, shift, lo, hi) in enumerate(pend):
                cp.wait()
                for k in range(ow // LANES):
                    c_lo, c_hi = max(lo, k * LANES), min(hi, (k + 1) * LANES)
                    if c_lo >= c_hi:
                        continue
                    wb = (c_lo + shift) // LANES
                    alen = min(2 * LANES, wlen - wb * LANES)
                    a = wbuf[slot, p, :, wb * LANES:wb * LANES + alen]
                    r, c = _rows((alen, LANES)), _cols((alen, LANES))
                    sh = shift + (k - wb) * LANES
                    sel = (r == c + sh) & (c >= c_lo - k * LANES) & (c < c_hi - k * LANES)
                    acc[k] = acc[k] + jnp.dot(a, sel.astype(bf16), preferred_element_type=f32)
            if writes[slot] is not None:
                writes[slot].wait()
            for k in range(ow // LANES):
                obuf[slot, :, k * LANES:(k + 1) * LANES] = acc[k].astype(bf16)
            writes[slot] = pltpu.make_async_copy(obuf.at[slot], out_ref.at[t, :, pl.ds(b * ow, ow)], sem_out.at[slot])
            writes[slot].start()
            pend = nxt
        for wr in writes:
            if wr is not None:
                wr.wait()

    operands, io_alias = [src], {}
    if out is not None:
        operands.append(out)
        io_alias = {1: 0}
    return pl.pallas_call(
        body, name=name, in_specs=[ANY] * len(operands), out_specs=ANY,
        out_shape=jax.ShapeDtypeStruct(out_shape, bf16), input_output_aliases=io_alias,
        scratch_shapes=[pltpu.VMEM((2, k_max, rows, win), bf16), pltpu.VMEM((2, rows, ow), bf16),
                        pltpu.SemaphoreType.DMA((2, k_max)), pltpu.SemaphoreType.DMA((2,))],
        compiler_params=pltpu.CompilerParams(vmem_limit_bytes=VMEM_LIMIT),
    )(*operands)


def _steps_to_kernel_cols(layer):
    steps = []
    for o in range(PW // REGROUP_W):
        pieces = []
        for pos, s, n in _kernel_pieces():
            lo, hi = max(pos, o * REGROUP_W), min(pos + n, (o + 1) * REGROUP_W)
            if lo < hi:
                ref_col = s + lo - pos
                pieces.append(((ref_col // SHARD_W) * N_LAYERS + layer, ref_col % SHARD_W, lo - o * REGROUP_W, hi - o * REGROUP_W))
        steps.append((0, o, pieces))
    return steps


def _steps_to_shards(layer):
    steps = []
    for j in range(4):
        for b in range(SHARD_PAD // REGROUP_W):
            start, stop = j * SHARD_W + b * REGROUP_W, min(j * SHARD_W + (b + 1) * REGROUP_W, (j + 1) * SHARD_W)
            pieces = []
            for pos, s, n in _kernel_pieces():
                lo, hi = max(s, start), min(s + n, stop)
                if lo < hi:
                    pieces.append((0, pos + lo - s, lo - start, hi - start))
            steps.append((layer * 4 + j, b, pieces))
    return steps


def _gather_weights(w):
    w_in = jnp.pad(w["w_in"].astype(bf16), ((0, 0), (0, 0), (0, SHARD_PAD - SHARD_W)))
    arrs = [w_in, w["w_branch"].reshape(N_LAYERS, N_BRANCH * BW, -1).astype(bf16),
            w["w_out"].astype(bf16), w["s5_w_glu"].astype(bf16), w["m2_conv_w"], w["sc_conv_w"], w["merge_b"]]
    got = gather_chips("ag_weights", arrs)
    cols = lambda t: jnp.transpose(t, (1, 0, 2)).reshape(t.shape[1], -1)
    layers = []
    for i in range(N_LAYERS):
        p = {k: w[k][i] for k in REPLICATED}
        p["w_in"] = regroup_cols(f"w_in_cols{i}", got[0].reshape(4 * N_LAYERS, D_MODEL, SHARD_PAD), _steps_to_kernel_cols(i),
                                 (1, D_MODEL, PW))[0]
        p["w_branch"] = cols(got[1][:, i])
        p["w_out"] = got[2][:, i].reshape(D_MODEL, D_MODEL)
        p["s5_w_glu"] = got[3][:, i].reshape(BW, BW)
        p["m2_conv_w"], p["sc_conv_w"], p["merge_b"] = cols(got[4][:, i]), cols(got[5][:, i]), cols(got[6][:, i])
        layers.append(p)
    return layers


def _reduce_grads(grads, d_final, w):
    to_chips = lambda t: jnp.transpose(t.reshape(t.shape[0], 4, -1), (1, 0, 2))
    stack = lambda f: jnp.stack([f(g) for g in grads])
    dw_in = None
    for i, g in enumerate(grads):
        dw_in = regroup_cols(f"w_in_shards{i}", g["w_in"][None], _steps_to_shards(i), (4 * N_LAYERS, D_MODEL, SHARD_PAD), out=dw_in)
    parts = [dw_in.reshape(N_LAYERS, 4, D_MODEL, SHARD_PAD),
             stack(lambda g: to_chips(g["w_branch"].reshape(N_BRANCH * BW, D_MODEL)).astype(bf16)),
             stack(lambda g: g["w_out"].reshape(4, D_MODEL // 4, D_MODEL).astype(bf16)),
             stack(lambda g: g["s5_w_glu"].reshape(4, BW // 4, BW).astype(bf16))]
    rep = jnp.concatenate([stack(lambda g: g[k]).reshape(-1) for k in REPLICATED] + [d_final.reshape(-1)])
    quarter = -(-rep.size // (4 * 2 * SUBL * PACK_W)) * (2 * SUBL * PACK_W)
    rep = jnp.pad(rep, (0, 4 * quarter - rep.size))
    small = []
    for j in range(4):
        sharded = [stack(lambda g: to_chips(g[k])[j]) for k in SHARDED_SMALL]
        small.append(_pack(sharded + [rep[j * quarter:(j + 1) * quarter]], 2, SUBL))
    parts.append(jnp.stack(small, axis=1))
    red = reduce_to_shards(parts)
    out = {"w_in": red[0][:, :, :SHARD_W], "w_branch": red[1].reshape(w["w_branch"].shape), "w_out": red[2], "s5_w_glu": red[3]}
    small_flat = red[4].reshape(-1)
    n_small = sum(int(np.prod(w[k].shape)) for k in SHARDED_SMALL)
    out.update(zip(SHARDED_SMALL, _unpack(small_flat, [w[k].shape for k in SHARDED_SMALL])))
    mine = small_flat[n_small:n_small + quarter].reshape(2, quarter // (2 * PACK_W), PACK_W)
    rep_all = gather_chips("ag_small_grads", [mine])[0].reshape(-1)
    names = REPLICATED + ("final_norm_w",)
    out.update(zip(names, _unpack(rep_all, [w[k].shape for k in names])))
    return out


def _update(w, g, m, v):
    d, nm, nv = {}, {}, {}
    flat2 = lambda a: a.reshape(-1, a.shape[-1])
    cols_major = lambda a: jnp.transpose(a, (2, 0, 1))
    res = adamw("adamw_w_in", *[cols_major(t["w_in"]) for t in (w, g, m, v)], tr=SHARD_W // 14)
    d["w_in"], nm["w_in"], nv["w_in"] = (jnp.transpose(r, (1, 2, 0)) for r in res)
    for k in SHARDED_BIG[1:]:
        res = adamw("adamw_" + k, *[flat2(t[k]) for t in (w, g, m, v)])
        d[k], nm[k], nv[k] = (r.reshape(w[k].shape) for r in res)
    for k in ("s5_b_re", "s5_b_im"):
        res = adamw("adamw_" + k, *[flat2(t[k]) for t in (w, g, m, v)])
        d[k], nm[k], nv[k] = (r.reshape(w[k].shape) for r in res)
    rest = [k for k in WEIGHTS if k not in SHARDED_BIG + ("s5_b_re", "s5_b_im")]
    two_d = lambda a: a.reshape(1, -1) if a.ndim == 1 else a
    res = adamw_many("adamw_rest", *[[two_d(t[k]) for k in rest] for t in (w, g, m, v)])
    for tgt, rs in zip((d, nm, nv), res):
        tgt.update({k: r.reshape(w[k].shape) for k, r in zip(rest, rs)})
    return d, nm, nv


def kernel(x, norm_w, w_in, s5_lambda_re, s5_lambda_im, s5_b_re, s5_b_im, s5_c_re, s5_c_im, s5_d, s5_log_step, s5_w_glu, sgu_ln_w, sgu_ln_b, sgu_w, sgu_b, m2_conv_w, m2_conv_b, m2_dt_bias, m2_a_log, m2_d, m2_norm_w, sc_conv_w, merge_b, w_branch, w_out, final_norm_w, loss_target, m_norm_w, m_w_in, m_s5_lambda_re, m_s5_lambda_im, m_s5_b_re, m_s5_b_im, m_s5_c_re, m_s5_c_im, m_s5_d, m_s5_log_step, m_s5_w_glu, m_sgu_ln_w, m_sgu_ln_b, m_sgu_w, m_sgu_b, m_m2_conv_w, m_m2_conv_b, m_m2_dt_bias, m_m2_a_log, m_m2_d, m_m2_norm_w, m_sc_conv_w, m_merge_b, m_w_branch, m_w_out, m_final_norm_w, v_norm_w, v_w_in, v_s5_lambda_re, v_s5_lambda_im, v_s5_b_re, v_s5_b_im, v_s5_c_re, v_s5_c_im, v_s5_d, v_s5_log_step, v_s5_w_glu, v_sgu_ln_w, v_sgu_ln_b, v_sgu_w, v_sgu_b, v_m2_conv_w, v_m2_conv_b, v_m2_dt_bias, v_m2_a_log, v_m2_d, v_m2_norm_w, v_sc_conv_w, v_merge_b, v_w_branch, v_w_out, v_final_norm_w):
    given = dict(locals())
    w = {k: given[k] for k in WEIGHTS}
    m = {k: given["m_" + k] for k in WEIGHTS}
    v = {k: given["v_" + k] for k in WEIGHTS}
    layers = _gather_weights(w)
    loss, dx, grads, d_final = local_step(x[0], loss_target[0], layers, final_norm_w)
    loss = lax.psum(loss, ("x", "y", "c"))
    g = _reduce_grads(grads, d_final, w)
    d, nm, nv = _update(w, g, m, v)
    return (loss, dx[None], *[g[k] for k in WEIGHTS], *[d[k] for k in WEIGHTS],
            *[nm[k] for k in WEIGHTS], *[nv[k] for k in WEIGHTS])
```

```python
import functools
from typing import Any, Callable, NamedTuple

import numpy as np
import jax
import jax.numpy as jnp
from jax import lax
from jax.experimental import pallas as pl
from jax.experimental.pallas import tpu as pltpu

f32 = jnp.float32
bf16 = jnp.bfloat16

D_MODEL = 1024
BW = 512
N_BRANCH = 4
EPS = 1e-6
S5_GROUPS, S5_P, S5_N = 32, 16, 64
S5_NS = S5_GROUPS * S5_N
CHUNK = 128
M2_HEADS, M2_HEAD_DIM, M2_GROUPS, M2_STATE = 8, 64, 2, 128
IN_DIM = 10248
PW = 10368
PW_MAIN = 10240
LANES = 128
VMEM_LIMIT = 60 * 1024 * 1024

ADAM_LR, ADAM_B1, ADAM_B2, ADAM_EPS, ADAM_WD, ADAM_STEP = 0.001, 0.9, 0.999, 1e-08, 0.01, 10

C_MERGE = 0
C_SC = 4096
C_SGU = 6144
C_S5G = 7680
C_S5U = 8192
C_M2Z = 8704
C_XBC = 9216
C_DT = 10240


def _col_segments():
    segs = [(6152, 4096)]
    for j in range(4):
        segs += [(4104 + 128 * j, 128), (4616 + 128 * j, 128), (5128 + 128 * j, 128), (5640 + 128 * j, 128)]
    segs += [(1024, 1536), (512, 512), (0, 512), (2560, 512), (3072, 1024), (4096, 8)]
    return segs


NN = ((1,), (0,))
NT = ((1,), (1,))
TN = ((0,), (0,))


def _bd(a, b, dims):
    return lax.dot_general(a.astype(bf16), b.astype(bf16), (dims, ((), ())), preferred_element_type=f32)


def _hd(a, b, dims):
    return lax.dot_general(a, b, (dims, ((), ())), precision=lax.Precision.HIGHEST, preferred_element_type=f32)


def _make_dots(raw):
    @jax.custom_vjp
    def nn(a, b):
        return raw(a, b, NN)
    nn.defvjp(lambda a, b: (raw(a, b, NN), (a, b)), lambda r, g: (raw(g, r[1], NT), raw(r[0], g, TN)))

    @jax.custom_vjp
    def nt(a, b):
        return raw(a, b, NT)
    nt.defvjp(lambda a, b: (raw(a, b, NT), (a, b)), lambda r, g: (raw(g, r[1], NN), raw(g, r[0], TN)))

    @jax.custom_vjp
    def tn(a, b):
        return raw(a, b, TN)
    tn.defvjp(lambda a, b: (raw(a, b, TN), (a, b)), lambda r, g: (raw(r[1], g, NT), raw(r[0], g, NN)))
    return nn, nt, tn


bdot, bdot_nt, bdot_tn = _make_dots(_bd)
hdot, hdot_nt, hdot_tn = _make_dots(_hd)


@jax.custom_vjp
def bdot_w(a, w, shadow):
    return _bd(a, w, NN)


bdot_w.defvjp(lambda a, w, s: (_bd(a, w, NN), (a, w)),
              lambda r, g: (_bd(g, r[1], NT), jnp.zeros_like(r[1]), _bd(r[0], g, TN)))


def _rows(shape):
    return lax.broadcasted_iota(jnp.int32, shape, 0)


def _cols(shape):
    return lax.broadcasted_iota(jnp.int32, shape, 1)


def _shift_down(x, s):
    return jnp.where(_rows(x.shape) < s, 0.0, pltpu.roll(x, s, 0))


def _shift_up(x, s):
    n = x.shape[0]
    return jnp.where(_rows(x.shape) >= n - s, 0.0, pltpu.roll(x, n - s, 0))


@functools.partial(jax.custom_vjp, nondiff_argnums=(1,))
def shift(x, s):
    return _shift_down(x, s) if s else x


shift.defvjp(lambda x, s: (shift(x, s), None), lambda s, _, g: (_shift_up(g, s) if s else g,))


def _row_of(w, k):
    return jnp.sum(jnp.where(_rows(w.shape) == k, w, 0.0), axis=0, keepdims=True)


def _lane_mask(width, lo, hi):
    c = _cols((1, width))
    return ((c >= lo) & (c < hi)).astype(f32)


def _expand(rows, width, per):
    return (_cols((rows, width)) // per == _rows((rows, width))).astype(f32)


class A(NamedTuple):
    arr: Any
    block: tuple
    imap: Callable
    shadow: bool = False


class O(NamedTuple):
    shape: tuple
    dtype: Any
    block: tuple
    imap: Callable
    alias: Any = None


class R(NamedTuple):
    arg: int
    out: int
    off: Any = None
    acc: bool = False


def _cparams(n_grid):
    return pltpu.CompilerParams(dimension_semantics=("arbitrary",) * n_grid, vmem_limit_bytes=VMEM_LIMIT)


def _ispec(block, imap, n, rev):
    if rev:
        return pl.BlockSpec(block, lambda i: imap(n - 1 - i))
    return pl.BlockSpec(block, imap)


def _load(ref, a):
    v = ref[...]
    if a.shadow:
        return (v, jnp.zeros(v.shape, f32))
    return v.astype(f32)


def _save_spec(shape, n, rev):
    nd = len(shape)
    return _ispec((None,) + tuple(shape), lambda i: (i,) + (0,) * nd, n, rev)


def block_fwd(name, f, n, args, outs, carries=()):
    n_in, n_out, n_c = len(args), len(outs), len(carries)

    def body(*refs):
        ins, out_r = refs[:n_in], refs[n_in:n_in + n_out]
        saves, cs = refs[n_in + n_out:n_in + n_out + n_c], refs[n_in + n_out + n_c:]
        if n_c:
            @pl.when(pl.program_id(0) == 0)
            def _():
                for c in cs:
                    c[...] = jnp.zeros(c.shape, f32)
        vals = [_load(r, a) for r, a in zip(ins, args)]
        cv = [c[...] for c in cs]
        for s, v in zip(saves, cv):
            s[...] = v
        res = f(*vals, *cv)
        for r, v in zip(out_r, res[:n_out]):
            r[...] = v.astype(r.dtype)
        for c, v in zip(cs, res[n_out:]):
            c[...] = v

    out_shape = [jax.ShapeDtypeStruct(o.shape, o.dtype) for o in outs]
    out_specs = [pl.BlockSpec(o.block, o.imap) for o in outs]
    for shp in carries:
        out_shape.append(jax.ShapeDtypeStruct((n,) + tuple(shp), f32))
        out_specs.append(_save_spec(shp, n, False))
    return pl.pallas_call(
        body, name=name, grid=(n,),
        in_specs=[pl.BlockSpec(a.block, a.imap) for a in args],
        out_specs=out_specs, out_shape=out_shape,
        scratch_shapes=[pltpu.VMEM(tuple(shp), f32) for shp in carries],
        compiler_params=_cparams(1),
    )(*[a.arr for a in args])


def block_bwd(name, f, n, args, cots, gouts, routes, saved=(), rev=False):
    n_in, n_cot, n_c, n_go = len(args), len(cots), len(saved), len(gouts)
    diff = []
    for r in routes:
        if r.arg not in diff:
            diff.append(r.arg)
    aliases = [(k, o.alias) for k, o in enumerate(gouts) if o.alias is not None]

    def body(*refs):
        ins = refs[:n_in]
        cot_r = refs[n_in:n_in + n_cot]
        sav_r = refs[n_in + n_cot:n_in + n_cot + n_c]
        base = n_in + n_cot + n_c + len(aliases)
        go_r = refs[base:base + n_go]
        dcs = refs[base + n_go:]
        step = pl.program_id(0)
        if n_c:
            @pl.when(step == 0)
            def _():
                for d in dcs:
                    d[...] = jnp.zeros(d.shape, f32)
        vals = [_load(r, a) for r, a in zip(ins, args)]
        cv = [s[...] for s in sav_r]
        nd = len(diff)

        def g(*dv):
            full = list(vals)
            for idx, v in zip(diff, dv[:nd]):
                full[idx] = (vals[idx][0], v) if args[idx].shadow else v
            return tuple(f(*full, *dv[nd:]))

        primals = [vals[i][1] if args[i].shadow else vals[i] for i in diff] + cv
        _, vjp_fn = jax.vjp(g, *primals)
        ct = tuple([r[...].astype(f32) for r in cot_r] + [d[...] for d in dcs])
        grads = vjp_fn(ct)
        for r in routes:
            gr = grads[diff.index(r.arg)]
            ref = go_r[r.out]
            if r.acc:
                @pl.when(step == 0)
                def _(ref=ref, gr=gr):
                    ref[...] = gr.astype(ref.dtype)

                @pl.when(step > 0)
                def _(ref=ref, gr=gr):
                    ref[...] += gr.astype(ref.dtype)
            elif r.off is None:
                ref[...] = gr.astype(ref.dtype)
            else:
                ref[:, r.off:r.off + gr.shape[1]] = gr.astype(ref.dtype)
        for d, gr in zip(dcs, grads[nd:]):
            d[...] = gr

    in_specs = [_ispec(a.block, a.imap, n, rev) for a in list(args) + list(cots)]
    in_specs += [_save_spec(s.shape[1:], n, rev) for s in saved]
    in_specs += [pl.BlockSpec(memory_space=pl.ANY) for _ in aliases]
    operands = [a.arr for a in list(args) + list(cots)] + list(saved) + [arr for _, arr in aliases]
    io_alias = {n_in + n_cot + n_c + j: k for j, (k, _) in enumerate(aliases)}
    return pl.pallas_call(
        body, name=name, grid=(n,),
        in_specs=in_specs,
        out_specs=[_ispec(o.block, o.imap, n, rev) for o in gouts],
        out_shape=[jax.ShapeDtypeStruct(o.shape, o.dtype) for o in gouts],
        scratch_shapes=[pltpu.VMEM(tuple(s.shape[1:]), f32) for s in saved],
        input_output_aliases=io_alias,
        compiler_params=_cparams(1),
    )(*operands)


class Op(NamedTuple):
    arr: Any
    row: int = 0
    col: int = 0


def mm(name, mode, pairs, m, n, tm, tn, out_dtype=f32, add=None, out=None, out_col=0, out_width=None):
    tm, tn = min(tm, m), min(tn, n)
    assert m % tm == 0 and n % tn == 0
    in_specs, operands = [], []
    for a, b, k, _ in pairs:
        if mode == TN:
            assert a.row % k == 0 and a.col % tm == 0
            in_specs.append(pl.BlockSpec((k, tm), lambda j, i, a=a, k=k: (a.row // k, i + a.col // tm)))
        else:
            assert a.col % k == 0 and a.row % tm == 0
            in_specs.append(pl.BlockSpec((tm, k), lambda j, i, a=a, k=k: (i + a.row // tm, a.col // k)))
        if mode == NT:
            assert b.col % k == 0 and b.row % tn == 0
            in_specs.append(pl.BlockSpec((tn, k), lambda j, i, b=b, k=k: (j + b.row // tn, b.col // k)))
        else:
            assert b.row % k == 0 and b.col % tn == 0
            in_specs.append(pl.BlockSpec((k, tn), lambda j, i, b=b, k=k: (b.row // k, j + b.col // tn)))
        operands += [a.arr, b.arr]
    n_p = len(pairs)
    if add is not None:
        assert add.col % tn == 0
        in_specs.append(pl.BlockSpec((tm, tn), lambda j, i: (i, j + add.col // tn)))
        operands.append(add.arr)
    io_alias = {}
    if out is not None:
        assert out_col % tn == 0
        in_specs.append(pl.BlockSpec(memory_space=pl.ANY))
        operands.append(out)
        io_alias = {len(operands) - 1: 0}
        out_shape = jax.ShapeDtypeStruct(out.shape, out.dtype)
    else:
        out_shape = jax.ShapeDtypeStruct((m, out_width or n), out_dtype)
    signs = [p[3] for p in pairs]

    def body(*refs):
        o = refs[-1]
        acc = None
        for p in range(n_p):
            t = _bd(refs[2 * p][...], refs[2 * p + 1][...], mode)
            t = t if signs[p] > 0 else -t
            acc = t if acc is None else acc + t
        if add is not None:
            acc = acc + refs[2 * n_p][...].astype(f32)
        o[...] = acc.astype(o.dtype)

    return pl.pallas_call(
        body, name=name, grid=(n // tn, m // tm),
        in_specs=in_specs,
        out_specs=pl.BlockSpec((tm, tn), lambda j, i: (i, j + out_col // tn)),
        out_shape=out_shape, input_output_aliases=io_alias,
        compiler_params=_cparams(2),
    )(*operands)


SCAN_LANES = 512
SUBL = 8


def _cmul(p, q):
    return (p[0] * q[0] - p[1] * q[1], p[0] * q[1] + p[1] * q[0])


def _powers(a):
    a2 = _cmul(a, a)
    a4 = _cmul(a2, a2)
    a6 = _cmul(a4, a2)
    return [a, a2, _cmul(a2, a), a4, _cmul(a4, a), a6, _cmul(a6, a), _cmul(a4, a4)]


def _table(pw, order, w):
    row = _rows((SUBL, w))
    re = sum(jnp.where(row == t, pw[k][0], 0.0) for t, k in enumerate(order))
    im = sum(jnp.where(row == t, pw[k][1], 0.0) for t, k in enumerate(order))
    return re, im


def s5_scan_fwd(bu_re, bu_im, a_re, a_im):
    seq, ns = bu_re.shape
    w, nb = SCAN_LANES, ns // SCAN_LANES

    def body(b_re, b_im, ar, ai, s_re, s_im):
        a = (ar[...], ai[...])
        pw = _powers(a)
        tab = _table(pw, list(range(SUBL)), w)
        row = _rows((SUBL, w))

        def step(i, carry):
            t0 = pl.multiple_of(i * SUBL, SUBL)
            x = (b_re[pl.ds(t0, SUBL), :], b_im[pl.ds(t0, SUBL), :])
            for d, k in ((1, 0), (2, 1), (4, 3)):
                sh = (jnp.where(row < d, 0.0, pltpu.roll(x[0], d, 0)), jnp.where(row < d, 0.0, pltpu.roll(x[1], d, 0)))
                t = _cmul(pw[k], sh)
                x = (x[0] + t[0], x[1] + t[1])
            t = _cmul(tab, carry)
            x = (x[0] + t[0], x[1] + t[1])
            s_re[pl.ds(t0, SUBL), :] = x[0]
            s_im[pl.ds(t0, SUBL), :] = x[1]
            return (x[0][SUBL - 1:, :], x[1][SUBL - 1:, :])

        z = jnp.zeros((1, w), f32)
        lax.fori_loop(0, seq // SUBL, step, (z, z), unroll=4)

    strip = pl.BlockSpec((seq, w), lambda j: (0, j))
    lane = pl.BlockSpec((1, w), lambda j: (0, j))
    return pl.pallas_call(
        body, name="s5_scan_fwd", grid=(nb,),
        in_specs=[strip, strip, lane, lane],
        out_specs=[strip, strip],
        out_shape=[jax.ShapeDtypeStruct((seq, ns), f32)] * 2,
        compiler_params=_cparams(1),
    )(bu_re, bu_im, a_re, a_im)


def s5_scan_bwd(ds_re, ds_im, s_re, s_im, a_re, a_im):
    seq, ns = ds_re.shape
    w, nb = SCAN_LANES, ns // SCAN_LANES
    nblk = seq // SUBL

    def body(g_re, g_im, sr, si, ar, ai, l_re, l_im, da_re, da_im):
        a = (ar[...], -ai[...])
        pw = _powers(a)
        tab = _table(pw, [SUBL - 1 - t for t in range(SUBL)], w)
        row = _rows((SUBL, w))

        def step(kk, carry):
            c_re, c_im, acc_re, acc_im = carry
            i = nblk - 1 - kk
            t0 = pl.multiple_of(i * SUBL, SUBL)
            x = (g_re[pl.ds(t0, SUBL), :], g_im[pl.ds(t0, SUBL), :])
            for d, k in ((1, 0), (2, 1), (4, 3)):
                sh = (jnp.where(row >= SUBL - d, 0.0, pltpu.roll(x[0], SUBL - d, 0)),
                      jnp.where(row >= SUBL - d, 0.0, pltpu.roll(x[1], SUBL - d, 0)))
                t = _cmul(pw[k], sh)
                x = (x[0] + t[0], x[1] + t[1])
            t = _cmul(tab, (c_re, c_im))
            x = (x[0] + t[0], x[1] + t[1])
            l_re[pl.ds(t0, SUBL), :] = x[0]
            l_im[pl.ds(t0, SUBL), :] = x[1]
            tp = jnp.maximum(t0 - 1, 0)
            live = (i > 0).astype(f32)
            p_re = sr[pl.ds(tp, 1), :] * live
            p_im = si[pl.ds(tp, 1), :] * live
            sp_re = jnp.where(row == 0, p_re, pltpu.roll(sr[pl.ds(t0, SUBL), :], 1, 0))
            sp_im = jnp.where(row == 0, p_im, pltpu.roll(si[pl.ds(t0, SUBL), :], 1, 0))
            acc_re = acc_re + x[0] * sp_re + x[1] * sp_im
            acc_im = acc_im + x[1] * sp_re - x[0] * sp_im
            return (x[0][:1, :], x[1][:1, :], acc_re, acc_im)

        z1 = jnp.zeros((1, w), f32)
        z8 = jnp.zeros((SUBL, w), f32)
        _, _, acc_re, acc_im = lax.fori_loop(0, nblk, step, (z1, z1, z8, z8), unroll=4)
        da_re[...] = jnp.sum(acc_re, axis=0, keepdims=True)
        da_im[...] = jnp.sum(acc_im, axis=0, keepdims=True)

    strip = pl.BlockSpec((seq, w), lambda j: (0, j))
    lane = pl.BlockSpec((1, w), lambda j: (0, j))
    return pl.pallas_call(
        body, name="s5_scan_bwd", grid=(nb,),
        in_specs=[strip, strip, strip, strip, lane, lane],
        out_specs=[strip, strip, lane, lane],
        out_shape=[jax.ShapeDtypeStruct((seq, ns), f32)] * 2 + [jax.ShapeDtypeStruct((1, ns), f32)] * 2,
        compiler_params=_cparams(1),
    )(ds_re, ds_im, s_re, s_im, a_re, a_im)


def _rms(x, w):
    return x * lax.rsqrt(jnp.mean(x * x, axis=-1, keepdims=True) + EPS) * w


def f_rms(x, w):
    return (_rms(x, w),)


def f_rms_res(x, w):
    return (_rms(x, w), x)


def f_s5_prep(lam_re, lam_im, log_step, b_re, b_im):
    e = _expand(log_step.shape[1], S5_NS, S5_N)
    step = hdot(jnp.exp(log_step), e)
    mag = jnp.exp(lam_re * step)
    ab_re, ab_im = mag * jnp.cos(lam_im * step), mag * jnp.sin(lam_im * step)
    den = lam_re * lam_re + lam_im * lam_im
    nr = ab_re - 1.0
    coef_re = (nr * lam_re + ab_im * lam_im) / den
    coef_im = (ab_im * lam_re - nr * lam_im) / den
    bb_re, bb_im = coef_re * b_re - coef_im * b_im, coef_re * b_im + coef_im * b_re
    sel = (_rows((BW, S5_P)) % S5_P == _cols((BW, S5_P))).astype(f32)
    blk = _rows((BW, S5_NS)) // S5_P == _cols((BW, S5_NS)) // S5_N
    rows_bd = lambda t: jnp.where(blk, hdot(sel, t), 0.0)
    return (ab_re, ab_im, rows_bd(bb_re), rows_bd(bb_im))


def f_s5_c(c_re, c_im):
    sel_t = (_cols((S5_P, BW)) % S5_P == _rows((S5_P, BW))).astype(f32)
    blk_t = _rows((S5_NS, BW)) // S5_N == _cols((S5_NS, BW)) // S5_P
    cols_bd = lambda t: jnp.where(blk_t, hdot_tn(t, sel_t), 0.0)
    return (cols_bd(c_re), cols_bd(c_im))


def f_s5_act(y_lin, u, gate, d, w_glu):
    y = jax.nn.gelu(y_lin + d * u)
    y = y * jax.nn.sigmoid(bdot_w(y, *w_glu))
    return (y * jax.nn.silu(gate),)


def f_sgu(u, v, gate, ln_w, ln_b, *rest):
    w_s, b_pad = rest[:8], rest[8]
    t = u.shape[0]
    u32, v32 = jax.nn.gelu(u), jax.nn.gelu(v)
    mu = jnp.mean(v32, axis=-1, keepdims=True)
    var = jnp.mean(jnp.square(v32 - mu), axis=-1, keepdims=True)
    vn = (v32 - mu) * lax.rsqrt(var + EPS) * ln_w + ln_b
    tri = _rows((t, t)) >= _cols((t, t))
    s = hdot_tn(b_pad, _expand(LANES, BW, BW // 8))
    for h in range(8):
        s = s + bdot(jnp.where(tri, w_s[h], 0.0), vn) * _lane_mask(BW, 64 * h, 64 * h + 64)
    return (u32 * s * jax.nn.silu(gate),)


def f_m2_conv(x, w, b):
    return (sum(_row_of(w, k) * shift(x, 3 - k) for k in range(4)) + b,)


def f_sc(bg, cg, h, gate, w):
    z = cg * h
    conv = sum(_row_of(w, k) * shift(z, 2 - k) for k in range(3))
    return (bg * conv * jax.nn.silu(gate),)


def f_m2(z, xc, b0, b1, c0, c1, dt_raw, dt_bias, a_log, d_par, norm_w, st):
    q = z.shape[0]
    x = jax.nn.silu(xc)
    bm, cm = (jax.nn.silu(b0), jax.nn.silu(b1)), (jax.nn.silu(c0), jax.nn.silu(c1))
    dt = jax.nn.softplus(dt_raw + dt_bias)
    da = dt * (-jnp.exp(a_log))
    tri = _rows((q, q)) >= _cols((q, q))
    acs = hdot(tri.astype(f32), da)
    e = _expand(LANES, BW, M2_HEAD_DIM)
    dt_f, acs_f = hdot(dt, e), hdot(acs, e)
    last = _rows((q, BW)) == q - 1
    alast_f = jnp.sum(jnp.where(last, acs_f, 0.0), axis=0, keepdims=True)
    xdt = x * dt_f
    xdec = xdt * jnp.exp(alast_f - acs_f)
    acs_t = acs.T
    st_new = st * jnp.exp(alast_f)
    y_diag, y_off = 0.0, 0.0
    for g in range(M2_GROUPS):
        gm = _lane_mask(BW, 256 * g, 256 * g + 256)
        cb = bdot_nt(cm[g], bm[g])
        st_new = st_new + bdot_tn(bm[g], xdec * gm)
        y_off = y_off + bdot(cm[g], st) * gm
        for hh in range(M2_HEADS // M2_GROUPS):
            h = g * (M2_HEADS // M2_GROUPS) + hh
            col = jnp.sum(jnp.where(_cols((q, LANES)) == h, acs, 0.0), axis=1, keepdims=True)
            row = jnp.sum(jnp.where(_rows((LANES, q)) == h, acs_t, 0.0), axis=0, keepdims=True)
            decay = jnp.exp(jnp.where(tri, col - row, -1e30))
            y_diag = y_diag + bdot(cb * decay, xdt) * _lane_mask(BW, 64 * h, 64 * h + 64)
    d_f = sum(jnp.sum(jnp.where(_cols((1, LANES)) == h, d_par, 0.0), axis=1, keepdims=True)
              * _lane_mask(BW, 64 * h, 64 * h + 64) for h in range(M2_HEADS))
    y = y_diag + y_off * jnp.exp(acs_f) + d_f * x
    y = y * jax.nn.silu(z)
    return (_rms(y, norm_w), st_new)


def f_branch_mix(*v):
    nb = len(v) // 4
    ys, lg, wb, mb = v[0:nb], v[nb:2 * nb], v[2 * nb:3 * nb], v[3 * nb:]
    return (sum(jax.nn.sigmoid(lg[k] + mb[k]) * bdot_w(ys[k], *wb[k]) for k in range(nb)),)


def _param(arr, shadow=False):
    return A(arr, tuple(arr.shape), lambda i, nd=arr.ndim: (0,) * nd, shadow)


def _tb(arr, t, width, colblk):
    return A(arr, (t, width), lambda i: (i, colblk))


def _strip(arr, seq, colblk0, stride=1):
    return A(arr, (seq, LANES), lambda j: (0, colblk0 + stride * j))


def _s5_prep_args(p):
    lam_re = p["s5_lambda_re"].reshape(1, S5_NS)
    lam_im = p["s5_lambda_im"].reshape(1, S5_NS)
    log_step = jnp.pad(p["s5_log_step"].reshape(1, S5_GROUPS), ((0, 0), (0, LANES - S5_GROUPS)))
    b_lanes = lambda b: jnp.transpose(b, (2, 0, 1)).reshape(S5_P, S5_NS)
    return [_param(v) for v in (lam_re, lam_im, log_step, b_lanes(p["s5_b_re"]), b_lanes(p["s5_b_im"]))]


def _s5_c_args(p):
    c_lanes = lambda c: jnp.transpose(c, (1, 0, 2)).reshape(S5_P, S5_NS)
    return [_param(c_lanes(p["s5_c_re"])), _param(c_lanes(p["s5_c_im"]))]


def layer_fwd(x, p):
    seq = x.shape[0]
    nt = seq // CHUNK
    t2 = 256
    sv = {}
    t4 = min(512, seq)
    hb = block_fwd("rms_in", f_rms, seq // t4, [_tb(x, t4, D_MODEL, 0), _param(p["norm_w"].reshape(1, D_MODEL))],
                   [O((seq, D_MODEL), bf16, (t4, D_MODEL), lambda i: (i, 0))])[0]
    proj = mm("mm_in", NN, [(Op(hb), Op(p["w_in"]), D_MODEL, 1)], seq, PW, 2048, 1152)
    whole = lambda shape, dt: O(shape, dt, shape, lambda i: (0, 0))
    ab_re, ab_im, bb_re, bb_im = block_fwd("s5_prep", f_s5_prep, 1, _s5_prep_args(p),
                                           [whole((1, S5_NS), f32)] * 2 + [whole((BW, S5_NS), bf16)] * 2)
    cc_re, cc_im = block_fwd("s5_c", f_s5_c, 1, _s5_c_args(p), [whole((S5_NS, BW), bf16)] * 2)
    bu_re = mm("mm_bu_re", NN, [(Op(proj, 0, C_S5U), Op(bb_re), BW, 1)], seq, S5_NS, 1024, 1024)
    bu_im = mm("mm_bu_im", NN, [(Op(proj, 0, C_S5U), Op(bb_im), BW, 1)], seq, S5_NS, 1024, 1024)
    s_re, s_im = s5_scan_fwd(bu_re, bu_im, ab_re, ab_im)
    y_lin = mm("mm_s5y", NN, [(Op(s_re), Op(cc_re), S5_NS, 1), (Op(s_im), Op(cc_im), S5_NS, -1)], seq, BW, 512, BW)
    s5_act_args = [_tb(y_lin, t4, BW, 0), _tb(proj, t4, BW, C_S5U // BW), _tb(proj, t4, BW, C_S5G // BW),
                   _param(p["s5_d"].reshape(1, BW)), _param(p["s5_w_glu"], True)]
    out_bw =O((seq, BW), f32, (CHUNK, BW), lambda i: (i, 0))
    y_a = block_fwd("s5_act", f_s5_act, seq // t4, s5_act_args, [O((seq, BW), f32, (t4, BW), lambda i: (i, 0))])[0]
    y_b = block_fwd("sgu", f_sgu, nt, _sgu_args(proj, p), [out_bw])[0]
    xc = block_fwd("m2_conv", f_m2_conv, 8, _m2_conv_args(proj, p, seq),
                   [O((seq, 2 * BW), f32, (seq, LANES), lambda j: (0, j))])[0]
    y_c, st_saved = block_fwd("m2_ssd", f_m2, nt, _m2_args(proj, xc, p), [out_bw], carries=[(M2_STATE, BW)])
    y_d = block_fwd("sc", f_sc, 4, _sc_args(proj, p, seq), [O((seq, BW), f32, (seq, LANES), lambda j: (0, j))])[0]
    ys = [y_a, y_b, y_c, y_d]
    merged = block_fwd("branch_mix", f_branch_mix, seq // t2, _mix_args(ys, proj, p, t2),
                       [O((seq, D_MODEL), bf16, (t2, D_MODEL), lambda i: (i, 0))])[0]
    x_new = mm("mm_out", NN, [(Op(merged), Op(p["w_out"]), D_MODEL, 1)], seq, D_MODEL, 1024, D_MODEL, add=Op(x))
    sv.update(x=x, hb=hb, proj=proj, ab=(ab_re, ab_im), bb=(bb_re, bb_im), cc=(cc_re, cc_im), s=(s_re, s_im), y_lin=y_lin,
              xc=xc, st=st_saved, ys=ys, merged=merged)
    return x_new, sv


def _sgu_args(proj, p):
    c0 = C_SGU // BW
    args = [_tb(proj, CHUNK, BW, c0), _tb(proj, CHUNK, BW, c0 + 1), _tb(proj, CHUNK, BW, c0 + 2),
            _param(p["sgu_ln_w"].reshape(1, BW)), _param(p["sgu_ln_b"].reshape(1, BW))]
    args += [A(p["sgu_w"], (None, CHUNK, CHUNK), lambda i, h=h: (h, 0, 0)) for h in range(8)]
    args.append(_param(jnp.pad(p["sgu_b"], ((0, LANES - 8), (0, 0)))))
    return args


def _m2_conv_args(proj, p, seq):
    return [_strip(proj, seq, C_XBC // LANES), A(p["m2_conv_w"], (4, LANES), lambda j: (0, j)),
            A(p["m2_conv_b"].reshape(1, 2 * BW), (1, LANES), lambda j: (0, j))]


def _pad_lanes(v):
    return jnp.pad(v.reshape(1, -1), ((0, 0), (0, LANES - v.size)))


def _m2_args(proj, xc, p):
    args = [_tb(proj, CHUNK, BW, C_M2Z // BW), _tb(xc, CHUNK, BW, 0)]
    args += [_tb(xc, CHUNK, LANES, 4 + k) for k in range(4)]
    args.append(_tb(proj, CHUNK, LANES, C_DT // LANES))
    args += [_param(_pad_lanes(p["m2_dt_bias"])), _param(_pad_lanes(p["m2_a_log"])), _param(_pad_lanes(p["m2_d"])),
             _param(p["m2_norm_w"].reshape(1, BW))]
    return args


def _sc_args(proj, p, seq):
    c0 = C_SC // LANES
    return [_strip(proj, seq, c0 + k, 4) for k in range(4)] + [A(p["sc_conv_w"], (3, LANES), lambda j: (0, j))]


def _mix_args(ys, proj, p, t, ks=range(N_BRANCH)):
    args = [_tb(ys[k], t, BW, 0) for k in ks]
    args += [_tb(proj, t, D_MODEL, k) for k in ks]
    args += [A(p["w_branch"], (BW, D_MODEL), lambda i, k=k: (k, 0), True) for k in ks]
    mb = p["merge_b"].reshape(N_BRANCH, 1, D_MODEL)
    args += [A(mb, (None, 1, D_MODEL), lambda i, k=k: (k, 0, 0)) for k in ks]
    return args


def layer_bwd(d_out, p, sv):
    seq = d_out.shape[0]
    nt = seq // CHUNK
    t4 = min(512, seq)
    proj, ys = sv["proj"], sv["ys"]
    g = {}
    acc = lambda shape: O(tuple(shape), f32, tuple(shape), lambda i, nd=len(shape): (0,) * nd)
    d_merged = mm("mm_out_dx", NT, [(Op(d_out), Op(p["w_out"]), D_MODEL, 1)], seq, D_MODEL, 1024, D_MODEL, out_dtype=f32)
    g["w_out"] = mm("mm_out_dw", TN, [(Op(sv["merged"]), Op(d_out), seq, 1)], D_MODEL, D_MODEL, 512, D_MODEL, out_dtype=bf16)
    dys, dwb, dmb, dproj, tb = [], [], [], None, min(256, seq)
    for half in range(2):
        gouts = [O((seq, BW), f32, (tb, BW), lambda i: (i, 0)) for _ in range(2)]
        gouts.append(O((seq, PW_MAIN), bf16, (tb, 2 * D_MODEL), lambda i, half=half: (i, half), alias=dproj))
        gouts += [acc((BW, D_MODEL)) for _ in range(2)] + [acc((1, D_MODEL)) for _ in range(2)]
        routes = [R(k, k) for k in range(2)] + [R(2 + k, 2, k * D_MODEL) for k in range(2)]
        routes += [R(4 + k, 3 + k, acc=True) for k in range(2)] + [R(6 + k, 5 + k, acc=True) for k in range(2)]
        res = block_bwd(f"branch_mix_bwd{half}", f_branch_mix, seq // tb, _mix_args(ys, proj, p, tb, (2 * half, 2 * half + 1)),
                        [_tb(d_merged, tb, D_MODEL, 0)], gouts, routes)
        dys, dproj, dwb, dmb = dys + list(res[:2]), res[2], dwb + list(res[3:5]), dmb + list(res[5:7])
    g["w_branch"] = jnp.stack(dwb)
    g["merge_b"] = jnp.concatenate(dmb, axis=0)
    res = block_bwd("sc_bwd", f_sc, 4, _sc_args(proj, p, seq), [_strip(dys[3], seq, 0)],
                    [O((seq, PW_MAIN), bf16, (seq, 4 * LANES), lambda j: (0, C_SC // (4 * LANES) + j), alias=dproj),
                     O((3, BW), f32, (3, LANES), lambda j: (0, j))],
                    [R(k, 0, k * LANES) for k in range(4)] + [R(4, 1)])
    dproj, g["sc_conv_w"] = res
    res = block_bwd("m2_ssd_bwd", f_m2, nt, _m2_args(proj, sv["xc"], p), [_tb(dys[2], CHUNK, BW, 0)],
                    [O((seq, PW_MAIN), bf16, (CHUNK, BW), lambda i: (i, C_M2Z // BW), alias=dproj),
                     O((seq, 2 * BW), f32, (CHUNK, 2 * BW), lambda i: (i, 0)),
                     O((seq, LANES), bf16, (CHUNK, LANES), lambda i: (i, 0)),
                     acc((1, LANES)), acc((1, LANES)), acc((1, LANES)), acc((1, BW))],
                    [R(0, 0), R(1, 1, 0)] + [R(2 + k, 1, BW + k * LANES) for k in range(4)] + [R(6, 2)]
                    + [R(7, 3, acc=True), R(8, 4, acc=True), R(9, 5, acc=True), R(10, 6, acc=True)],
                    saved=[sv["st"]], rev=True)
    dproj, dxc, d_dt = res[0], res[1], res[2]
    g["m2_dt_bias"], g["m2_a_log"], g["m2_d"] = (r[0, :M2_HEADS] for r in res[3:6])
    g["m2_norm_w"] = res[6].reshape(BW)
    res = block_bwd("m2_conv_bwd", f_m2_conv, 8, _m2_conv_args(proj, p, seq), [_strip(dxc, seq, 0)],
                    [O((seq, PW_MAIN), bf16, (seq, LANES), lambda j: (0, C_XBC // LANES + j), alias=dproj),
                     O((4, 2 * BW), f32, (4, LANES), lambda j: (0, j)), O((1, 2 * BW), f32, (1, LANES), lambda j: (0, j))],
                    [R(0, 0), R(1, 1), R(2, 2)])
    dproj, g["m2_conv_w"], cb = res
    g["m2_conv_b"] = cb.reshape(2 * BW)
    res = block_bwd("sgu_bwd", f_sgu, nt, _sgu_args(proj, p), [_tb(dys[1], CHUNK, BW, 0)],
                    [O((seq, PW_MAIN), bf16, (CHUNK, 3 * BW), lambda i: (i, C_SGU // (3 * BW)), alias=dproj),
                     acc((1, BW)), acc((1, BW))] + [acc((CHUNK, CHUNK)) for _ in range(8)] + [acc((LANES, CHUNK))],
                    [R(0, 0, 0), R(1, 0, BW), R(2, 0, 2 * BW), R(3, 1, acc=True), R(4, 2, acc=True)]
                    + [R(5 + h, 3 + h, acc=True) for h in range(8)] + [R(13, 11, acc=True)])
    dproj = res[0]
    g["sgu_ln_w"], g["sgu_ln_b"] = res[1].reshape(BW), res[2].reshape(BW)
    g["sgu_w"] = jnp.stack(res[3:11])
    g["sgu_b"] = res[11][:8]
    y_lin, (s_re, s_im), (ab_re, ab_im) = sv["y_lin"], sv["s"], sv["ab"]
    s5_act_args = [_tb(y_lin, t4, BW, 0), _tb(proj, t4, BW, C_S5U // BW), _tb(proj, t4, BW, C_S5G // BW),
                   _param(p["s5_d"].reshape(1, BW)), _param(p["s5_w_glu"], True)]
    res = block_bwd("s5_act_bwd", f_s5_act, seq // t4, s5_act_args, [_tb(dys[0], t4, BW, 0)],
                    [O((seq, BW), bf16, (t4, BW), lambda i: (i, 0)), O((seq, BW), f32, (t4, BW), lambda i: (i, 0)),
                     O((seq, PW_MAIN), bf16, (t4, BW), lambda i: (i, C_S5G // BW), alias=dproj),
                     acc((1, BW)), acc((BW, BW))],
                    [R(0, 0), R(1, 1), R(2, 2), R(3, 3, acc=True), R(4, 4, acc=True)])
    dy_lin, du1, dproj = res[0], res[1], res[2]
    g["s5_d"] = res[3].reshape(S5_GROUPS, S5_P)
    g["s5_w_glu"] = res[4]
    (bb_re, bb_im), (cc_re, cc_im) = sv["bb"], sv["cc"]
    ds_re = mm("mm_s5y_dre", NT, [(Op(dy_lin), Op(cc_re), BW, 1)], seq, S5_NS, 1024, 1024)
    ds_im = mm("mm_s5y_dim", NT, [(Op(dy_lin), Op(cc_im), BW, -1)], seq, S5_NS, 1024, 1024)
    dc_re = mm("mm_s5y_dcre", TN, [(Op(s_re), Op(dy_lin), seq, 1)], S5_NS, BW, 512, BW)
    dc_im = mm("mm_s5y_dcim", TN, [(Op(s_im), Op(dy_lin), seq, -1)], S5_NS, BW, 512, BW)
    l_re, l_im, da_re, da_im = s5_scan_bwd(ds_re, ds_im, s_re, s_im, ab_re, ab_im)
    dproj = mm("mm_bu_dx", NT, [(Op(l_re), Op(bb_re), S5_NS, 1), (Op(l_im), Op(bb_im), S5_NS, 1)],
               seq, BW, 512, BW, add=Op(du1), out=dproj, out_col=C_S5U)
    dbb = [mm(f"mm_bu_dw{n}", TN, [(Op(proj, 0, C_S5U), Op(l), seq, 1)], BW, S5_NS, 512, 1024)
           for n, l in (("re", l_re), ("im", l_im))]
    gouts = [acc((1, S5_NS)), acc((1, S5_NS)), acc((1, LANES))] + [acc((S5_P, S5_NS))] * 2
    res = block_bwd("s5_prep_bwd", f_s5_prep, 1, _s5_prep_args(p),
                    [_param(v) for v in (da_re, da_im, dbb[0], dbb[1])], gouts, [R(k, k, acc=True) for k in range(5)])
    g["s5_lambda_re"], g["s5_lambda_im"] = res[0].reshape(S5_GROUPS, S5_N), res[1].reshape(S5_GROUPS, S5_N)
    g["s5_log_step"] = res[2][0, :S5_GROUPS]
    b_natural = lambda b: jnp.transpose(b.reshape(S5_P, S5_GROUPS, S5_N), (1, 2, 0))
    c_natural = lambda c: jnp.transpose(c.reshape(S5_P, S5_GROUPS, S5_N), (1, 0, 2))
    g["s5_b_re"], g["s5_b_im"] = b_natural(res[3]), b_natural(res[4])
    res = block_bwd("s5_c_bwd", f_s5_c, 1, _s5_c_args(p), [_param(dc_re), _param(dc_im)],
                    [acc((S5_P, S5_NS))] * 2, [R(0, 0, acc=True), R(1, 1, acc=True)])
    g["s5_c_re"], g["s5_c_im"] = c_natural(res[0]), c_natural(res[1])
    hb, w_in = sv["hb"], p["w_in"]
    dh = mm("mm_in_dx", NT, [(Op(dproj), Op(w_in), PW_MAIN, 1), (Op(d_dt), Op(w_in, 0, PW_MAIN), LANES, 1)],
            seq, D_MODEL, 512, 512)
    dw = mm("mm_in_dw", TN, [(Op(hb), Op(dproj), seq, 1)], D_MODEL, PW_MAIN, 1024, 2048, out_dtype=bf16, out_width=PW)
    g["w_in"] = mm("mm_in_dwdt", TN, [(Op(hb), Op(d_dt), seq, 1)], D_MODEL, LANES, 256, LANES, out=dw, out_col=C_DT)
    dx, dnw = block_bwd("rms_in_bwd", f_rms_res, seq // t4, [_tb(sv["x"], t4, D_MODEL, 0), _param(p["norm_w"].reshape(1, D_MODEL))],
                        [_tb(dh, t4, D_MODEL, 0), _tb(d_out, t4, D_MODEL, 0)],
                        [O((seq, D_MODEL), f32, (t4, D_MODEL), lambda i: (i, 0)), acc((1, D_MODEL))],
                        [R(0, 0), R(1, 1, acc=True)])
    g["norm_w"] = dnw.reshape(D_MODEL)
    return dx, g


def loss_head(x, w, target):
    seq = x.shape[0]
    t = min(512, seq)

    def body(x_ref, w_ref, t_ref, loss_ref, dx_ref, dw_ref):
        step = pl.program_id(0)

        def f(xv, wv):
            err = _rms(xv, wv) - t_ref[...]
            return 0.5 * jnp.sum(jnp.mean(err * err, axis=-1, keepdims=True), axis=0, keepdims=True)

        val, vjp_fn = jax.vjp(f, x_ref[...], w_ref[...])
        dx, dw = vjp_fn(jnp.ones((1, 1), f32))
        dx_ref[...] = dx

        @pl.when(step == 0)
        def _():
            loss_ref[...] = jnp.broadcast_to(val, loss_ref.shape)
            dw_ref[...] = dw

        @pl.when(step > 0)
        def _():
            loss_ref[...] += jnp.broadcast_to(val, loss_ref.shape)
            dw_ref[...] += dw

    blk = pl.BlockSpec((t, D_MODEL), lambda i: (i, 0))
    row = pl.BlockSpec((1, D_MODEL), lambda i: (0, 0))
    return pl.pallas_call(
        body, name="loss_head", grid=(seq // t,),
        in_specs=[blk, row, blk],
        out_specs=[pl.BlockSpec((1, LANES), lambda i: (0, 0)), blk, row],
        out_shape=[jax.ShapeDtypeStruct((1, LANES), f32), jax.ShapeDtypeStruct((seq, D_MODEL), f32),
                   jax.ShapeDtypeStruct((1, D_MODEL), f32)],
        compiler_params=_cparams(1),
    )(x, w.reshape(1, D_MODEL), target)


LAYER_KEYS = ("norm_w", "w_in", "s5_lambda_re", "s5_lambda_im", "s5_b_re", "s5_b_im", "s5_c_re", "s5_c_im", "s5_d",
              "s5_log_step", "s5_w_glu", "sgu_ln_w", "sgu_ln_b", "sgu_w", "sgu_b", "m2_conv_w", "m2_conv_b",
              "m2_dt_bias", "m2_a_log", "m2_d", "m2_norm_w", "sc_conv_w", "merge_b", "w_branch", "w_out")


def local_step(x, target, layers, final_norm_w):
    saved = []
    for p in layers:
        x, sv = layer_fwd(x, p)
        saved.append(sv)
    loss, dx, dfw = loss_head(x, final_norm_w, target)
    grads = []
    for p, sv in zip(reversed(layers), reversed(saved)):
        dx, g = layer_bwd(dx, p, sv)
        grads.append(g)
    return loss[0, 0], dx, grads[::-1], dfw.reshape(D_MODEL)


MESH = pl.DeviceIdType.MESH
ANY = pl.BlockSpec(memory_space=pl.ANY)


def _me():
    return lax.axis_index("x"), lax.axis_index("y"), lax.axis_index("c")


def _other_chips(x, y):
    return [(1 - x, y), (x, 1 - y), (1 - x, 1 - y)]


def _rcopy(src, dst, send, recv, dev):
    return pltpu.make_async_remote_copy(src_ref=src, dst_ref=dst, send_sem=send, recv_sem=recv,
                                        device_id=dev, device_id_type=MESH)


def _route_cut(rows, dtype):
    tile = 2 * SUBL * (4 // jnp.dtype(dtype).itemsize)
    return rows // 2 if rows % tile == 0 else rows


def _comm_call(name, body, arrs, out_shape, n_remote, aliases=None):
    n = len(arrs)
    return pl.pallas_call(
        body, name=name, in_specs=[ANY] * n, out_specs=[ANY] * len(out_shape), out_shape=out_shape,
        scratch_shapes=[pltpu.SemaphoreType.DMA((n, n_remote)), pltpu.SemaphoreType.DMA((n, n_remote))],
        input_output_aliases=aliases or {},
        compiler_params=pltpu.CompilerParams(has_side_effects=True),
    )(*arrs)


def gather_chips(name, arrs):
    n = len(arrs)
    cut = [_route_cut(a.shape[1], a.dtype) for a in arrs]

    def body(*refs):
        ins, outs = refs[:n], refs[n:2 * n]
        send, recv = refs[2 * n:]
        x, y, c = _me()
        jme, jx, jy, jd = 2 * x + y, 2 * (1 - x) + y, 2 * x + 1 - y, 2 * (1 - x) + 1 - y
        to_x, to_y, sib = (1 - x, y, c), (x, 1 - y, c), (x, y, 1 - c)

        def part(ref, a, hi):
            return ref.at[pl.ds(cut[a], ref.shape[0] - cut[a])] if hi else ref.at[pl.ds(0, cut[a])]

        def cp(a, k, ref, dev):
            return _rcopy(ref, ref, send.at[a, k], recv.at[a, k], dev)

        split = [a for a in range(n) if cut[a] < arrs[a].shape[1]]
        sent = [_rcopy(ins[a].at[c], outs[a].at[jme, c], send.at[a, k], recv.at[a, k], dev)
                for a in range(n) for k, dev in ((0, to_x), (1, to_y))]
        for s in sent:
            s.start()
        for a in range(n):
            blk = outs[a].at[jx, c]
            cp(a, 0, blk, to_x).wait_recv()
            sent += [cp(a, 2, part(blk, a, False), to_y), cp(a, 4, blk, sib)]
            sent[-2].start()
            sent[-1].start()
        for a in range(n):
            blk = outs[a].at[jy, c]
            cp(a, 1, blk, to_y).wait_recv()
            sent.append(cp(a, 5, blk, sib))
            sent[-1].start()
            if a in split:
                sent.append(cp(a, 3, part(blk, a, True), to_x))
                sent[-1].start()
        for a in range(n):
            lo = part(outs[a].at[jd, c], a, False)
            cp(a, 2, lo, to_y).wait_recv()
            sent.append(cp(a, 6, lo, sib))
            sent[-1].start()
        for a in split:
            hi = part(outs[a].at[jd, c], a, True)
            cp(a, 3, hi, to_x).wait_recv()
            sent.append(cp(a, 7, hi, sib))
            sent[-1].start()
        for a in range(n):
            cp(a, 4, outs[a].at[jx, 1 - c], sib).wait_recv()
            cp(a, 5, outs[a].at[jy, 1 - c], sib).wait_recv()
            cp(a, 6, part(outs[a].at[jd, 1 - c], a, False), sib).wait_recv()
        for a in split:
            cp(a, 7, part(outs[a].at[jd, 1 - c], a, True), sib).wait_recv()
        for s in sent:
            s.wait_send()

    out_shape = [jax.ShapeDtypeStruct((4,) + a.shape, a.dtype) for a in arrs]
    got = _comm_call(name, body, arrs, out_shape, 8)
    jme = 2 * lax.axis_index("x") + lax.axis_index("y")
    return [lax.dynamic_update_index_in_dim(g, a, jme, 0) for g, a in zip(got, arrs)]


def swap_halves(name, arrs):
    n = len(arrs)

    def body(*refs):
        ins, outs = refs[:n], refs[n:2 * n]
        send, recv = refs[2 * n:]
        x, y, c = _me()
        remote = [_rcopy(ins[a].at[1 - c], outs[a], send.at[a, 0], recv.at[a, 0], (x, y, 1 - c)) for a in range(n)]
        for cp in remote:
            cp.start()
        for cp in remote:
            cp.wait()

    return _comm_call(name, body, arrs, [jax.ShapeDtypeStruct(a.shape[1:], a.dtype) for a in arrs], 1)


def exchange_chips(name, arrs):
    n = len(arrs)

    def body(*refs):
        ins, outs = refs[:n], refs[n:2 * n]
        send, recv = refs[2 * n:]
        x, y, c = _me()
        remote = [_rcopy(ins[a].at[2 * cx + cy], outs[a].at[k], send.at[a, k], recv.at[a, k], (cx, cy, c))
                  for a in range(n) for k, (cx, cy) in enumerate(_other_chips(x, y))]
        for cp in remote:
            cp.start()
        for cp in remote:
            cp.wait()

    return _comm_call(name, body, arrs, [jax.ShapeDtypeStruct((3,) + a.shape[1:], a.dtype) for a in arrs], 3)


def gather_cores(name, arrs):
    n = len(arrs)

    def body(*refs):
        bufs = refs[n:2 * n]
        send, recv = refs[2 * n:]
        x, y, c = _me()
        remote = [_rcopy(bufs[a].at[c], bufs[a].at[c], send.at[a, 0], recv.at[a, 0], (x, y, 1 - c)) for a in range(n)]
        for cp in remote:
            cp.start()
        for a in range(n):
            _rcopy(bufs[a].at[1 - c], bufs[a].at[1 - c], send.at[a, 0], recv.at[a, 0], (x, y, 1 - c)).wait_recv()
        for cp in remote:
            cp.wait_send()

    return _comm_call(name, body, arrs, [jax.ShapeDtypeStruct(a.shape, a.dtype) for a in arrs], 1,
                      aliases={a: a for a in range(n)})


ROW_BLOCK = 512


def esum(name, terms, rows, width, out_dtype, out_slots=None):
    tr = next((t for t in range(min(rows, ROW_BLOCK), 0, -SUBL) if rows % t == 0 and t % SUBL == 0), rows)
    where =jnp.stack([lax.axis_index("c"), 2 * lax.axis_index("x") + lax.axis_index("y")]).astype(jnp.int32)
    pick = {"c": 0, "j": 1}

    def body(s_ref, *refs):
        acc = refs[0][...].astype(f32)
        for r in refs[1:-1]:
            acc = acc + r[...].astype(f32)
        refs[-1][...] = acc.astype(out_dtype)

    specs = []
    for arr, lead in terms:
        if lead is None:
            specs.append(pl.BlockSpec((tr, width), lambda i, s: (i, 0)))
        elif isinstance(lead, str):
            specs.append(pl.BlockSpec((None, tr, width), lambda i, s, lead=lead: (s[pick[lead]], i, 0)))
        else:
            specs.append(pl.BlockSpec((None, tr, width), lambda i, s, lead=lead: (lead, i, 0)))
    if out_slots is None:
        out_spec = pl.BlockSpec((tr, width), lambda i, s: (i, 0))
        out_shape = jax.ShapeDtypeStruct((rows, width), out_dtype)
    else:
        out_spec = pl.BlockSpec((None, tr, width), lambda i, s: (s[0], i, 0))
        out_shape = jax.ShapeDtypeStruct((out_slots, rows, width), out_dtype)
    return pl.pallas_call(
        body, name=name,
        grid_spec=pltpu.PrefetchScalarGridSpec(num_scalar_prefetch=1, grid=(rows // tr,), in_specs=specs, out_specs=out_spec),
        out_shape=out_shape, compiler_params=_cparams(1),
    )(where, *[t[0] for t in terms])


def reduce_to_shards(parts):
    tags = [str(k) for k in range(len(parts))]
    theirs = swap_halves("rs_swap", parts)
    t1 = []
    for tag, p, th in zip(tags, parts, theirs):
        _, _, h, w = p.shape
        t1.append(esum("rs_add_cores" + tag, [(p.reshape(2, 4 * h, w), "c"), (th.reshape(4 * h, w), None)],
                       4 * h, w, p.dtype).reshape(4, h, w))
    landed = exchange_chips("rs_exchange", t1)
    red = []
    for tag, p, t, got in zip(tags, parts, t1, landed):
        _, _, h, w = p.shape
        red.append(esum("rs_add_chips" + tag, [(t, "j"), (got, 0), (got, 1), (got, 2)], h, w, f32, out_slots=2))
    return gather_cores("rs_gather", red)


def _adamw_step(w_ref, g_ref, m_ref, v_ref, d_ref, nm_ref, nv_ref):
    gv = g_ref[...]
    nm = ADAM_B1 * m_ref[...] + (1.0 - ADAM_B1) * gv
    nv = ADAM_B2 * v_ref[...] + (1.0 - ADAM_B2) * jnp.square(gv)
    m_hat = nm / (1.0 - ADAM_B1 ** ADAM_STEP)
    v_hat = nv / (1.0 - ADAM_B2 ** ADAM_STEP)
    d_ref[...] = -ADAM_LR * (m_hat / (jnp.sqrt(v_hat) + ADAM_EPS) + ADAM_WD * w_ref[...])
    nm_ref[...] = nm
    nv_ref[...] = nv


def adamw(name, w, g, m, v, tr=None):
    rows, rest = w.shape[0], w.shape[1:]
    if tr is None:
        tr = ROW_BLOCK if rows % ROW_BLOCK == 0 else rows
    assert rows % tr == 0

    def body(*refs):
        _adamw_step(*refs)

    spec = pl.BlockSpec((tr,) + rest, lambda i: (i,) + (0,) * len(rest))
    return pl.pallas_call(
        body, name=name, grid=(rows // tr,), in_specs=[spec] * 4, out_specs=[spec] * 3,
        out_shape=[jax.ShapeDtypeStruct(w.shape, f32)] * 3, compiler_params=_cparams(1),
    )(w, g, m, v)


def adamw_many(name, ws, gs, ms, vs):
    n = len(ws)

    def body(*refs):
        ins, outs = refs[:4 * n], refs[4 * n:]
        for k in range(n):
            _adamw_step(ins[k], ins[n + k], ins[2 * n + k], ins[3 * n + k], outs[k], outs[n + k], outs[2 * n + k])

    vmem = pl.BlockSpec(memory_space=pltpu.VMEM)
    res = pl.pallas_call(
        body, name=name, in_specs=[vmem] * (4 * n), out_specs=[vmem] * (3 * n),
        out_shape=[jax.ShapeDtypeStruct(w.shape, f32) for w in ws] * 3,
        compiler_params=pltpu.CompilerParams(vmem_limit_bytes=VMEM_LIMIT),
    )(*ws, *gs, *ms, *vs)
    return res[:n], res[n:2 * n], res[2 * n:]


PACK_W = 1024


def _pack(parts, halves, row_mult):
    flat = jnp.concatenate([p.reshape(-1) for p in parts])
    per = halves * row_mult * PACK_W
    total = -(-flat.size // per) * per
    flat = jnp.pad(flat, (0, total - flat.size))
    return flat.reshape(halves, total // (halves * PACK_W), PACK_W)


def _unpack(flat, shapes):
    out, pos = [], 0
    for s in shapes:
        n = int(np.prod(s))
        out.append(flat[pos:pos + n].reshape(s))
        pos += n
    return out


SHARDED_BIG = ("w_in", "w_branch", "w_out", "s5_w_glu")
SHARDED_SMALL = ("m2_conv_w", "sc_conv_w", "merge_b")
SHARD_AXIS = {"w_in": 2, "w_branch": 3, "w_out": 1, "s5_w_glu": 1, "m2_conv_w": 2, "sc_conv_w": 2, "merge_b": 2}
REPLICATED = ("norm_w", "s5_lambda_re", "s5_lambda_im", "s5_b_re", "s5_b_im", "s5_c_re", "s5_c_im", "s5_d", "s5_log_step",
              "sgu_ln_w", "sgu_ln_b", "sgu_w", "sgu_b", "m2_conv_b", "m2_dt_bias", "m2_a_log", "m2_d", "m2_norm_w")
WEIGHTS = ("norm_w", "w_in", "s5_lambda_re", "s5_lambda_im", "s5_b_re", "s5_b_im", "s5_c_re", "s5_c_im", "s5_d",
           "s5_log_step", "s5_w_glu", "sgu_ln_w", "sgu_ln_b", "sgu_w", "sgu_b", "m2_conv_w", "m2_conv_b", "m2_dt_bias",
           "m2_a_log", "m2_d", "m2_norm_w", "sc_conv_w", "merge_b", "w_branch", "w_out", "final_norm_w")
N_LAYERS = 2


SHARD_W = IN_DIM // 4
SHARD_PAD = -(-SHARD_W // LANES) * LANES
REGROUP_W = 3 * LANES


def _kernel_pieces():
    out, pos = [], 0
    for s, n in _col_segments():
        while n:
            take = min(n, SHARD_W - s % SHARD_W)
            out.append((pos, s, take))
            pos, s, n = pos + take, s + take, n - take
    return out


def regroup_cols(name, src, steps, out_shape, out=None):
    n_src, rows, width = src.shape
    ow, win = REGROUP_W, REGROUP_W + LANES
    k_max = max(len(p) for _, _, p in steps)
    assert width % LANES == 0

    def body(*refs):
        src_ref, out_ref = refs[0], refs[-5]
        wbuf, obuf, sem_in, sem_out = refs[-4:]

        def fetch(q, slot):
            started = []
            for p, (s, col, lo, hi) in enumerate(steps[q][2]):
                w0 = col // LANES * LANES
                wlen = min(win, width - w0)
                cp = pltpu.make_async_copy(src_ref.at[s, :, pl.ds(w0, wlen)], wbuf.at[slot, p, :, pl.ds(0, wlen)], sem_in.at[slot, p])
                cp.start()
                started.append((cp, wlen, col - w0 - lo, lo, hi))
            return started

        pend, writes = fetch(0, 0), [None, None]
        for q, (t, b, _) in enumerate(steps):
            slot = q % 2
            nxt = fetch(q + 1, 1 - slot) if q + 1 < len(steps) else []
            acc = [jnp.zeros((rows, LANES), f32) for _ in range(ow // LANES)]
            for p, (cp, wlen, shift, lo, hi) in enumerate(pend):
                cp.wait()
                for k in range(ow // LANES):
                    c_lo, c_hi = max(lo, k * LANES), min(hi, (k + 1) * LANES)
                    if c_lo >= c_hi:
                        continue
                    wb = (c_lo + shift) // LANES
                    alen = min(2 * LANES, wlen - wb * LANES)
                    a = wbuf[slot, p, :, wb * LANES:wb * LANES + alen]
                    r, c = _rows((alen, LANES)), _cols((alen, LANES))
                    sh = shift + (k - wb) * LANES
                    sel = (r == c + sh) & (c >= c_lo - k * LANES) & (c < c_hi - k * LANES)
                    acc[k] = acc[k] + jnp.dot(a, sel.astype(bf16), preferred_element_type=f32)
            if writes[slot] is not None:
                writes[slot].wait()
            for k in range(ow // LANES):
                obuf[slot, :, k * LANES:(k + 1) * LANES] = acc[k].astype(bf16)
            writes[slot] = pltpu.make_async_copy(obuf.at[slot], out_ref.at[t, :, pl.ds(b * ow, ow)], sem_out.at[slot])
            writes[slot].start()
            pend = nxt
        for wr in writes:
            if wr is not None:
                wr.wait()

    operands, io_alias = [src], {}
    if out is not None:
        operands.append(out)
        io_alias = {1: 0}
    return pl.pallas_call(
        body, name=name, in_specs=[ANY] * len(operands), out_specs=ANY,
        out_shape=jax.ShapeDtypeStruct(out_shape, bf16), input_output_aliases=io_alias,
        scratch_shapes=[pltpu.VMEM((2, k_max, rows, win), bf16), pltpu.VMEM((2, rows, ow), bf16),
                        pltpu.SemaphoreType.DMA((2, k_max)), pltpu.SemaphoreType.DMA((2,))],
        compiler_params=pltpu.CompilerParams(vmem_limit_bytes=VMEM_LIMIT),
    )(*operands)


def _steps_to_kernel_cols(layer):
    steps = []
    for o in range(PW // REGROUP_W):
        pieces = []
        for pos, s, n in _kernel_pieces():
            lo, hi = max(pos, o * REGROUP_W), min(pos + n, (o + 1) * REGROUP_W)
            if lo < hi:
                ref_col = s + lo - pos
                pieces.append(((ref_col // SHARD_W) * N_LAYERS + layer, ref_col % SHARD_W, lo - o * REGROUP_W, hi - o * REGROUP_W))
        steps.append((0, o, pieces))
    return steps


def _steps_to_shards(layer):
    steps = []
    for j in range(4):
        for b in range(SHARD_PAD // REGROUP_W):
            start, stop = j * SHARD_W + b * REGROUP_W, min(j * SHARD_W + (b + 1) * REGROUP_W, (j + 1) * SHARD_W)
            pieces = []
            for pos, s, n in _kernel_pieces():
                lo, hi = max(s, start), min(s + n, stop)
                if lo < hi:
                    pieces.append((0, pos + lo - s, lo - start, hi - start))
            steps.append((layer * 4 + j, b, pieces))
    return steps


def _gather_weights(w):
    w_in = jnp.pad(w["w_in"].astype(bf16), ((0, 0), (0, 0), (0, SHARD_PAD - SHARD_W)))
    arrs = [w_in, w["w_branch"].reshape(N_LAYERS, N_BRANCH * BW, -1).astype(bf16),
            w["w_out"].astype(bf16), w["s5_w_glu"].astype(bf16), w["m2_conv_w"], w["sc_conv_w"], w["merge_b"]]
    got = gather_chips("ag_weights", arrs)
    cols = lambda t: jnp.transpose(t, (1, 0, 2)).reshape(t.shape[1], -1)
    layers = []
    for i in range(N_LAYERS):
        p = {k: w[k][i] for k in REPLICATED}
        p["w_in"] = regroup_cols(f"w_in_cols{i}", got[0].reshape(4 * N_LAYERS, D_MODEL, SHARD_PAD), _steps_to_kernel_cols(i),
                                 (1, D_MODEL, PW))[0]
        p["w_branch"] = cols(got[1][:, i])
        p["w_out"] = got[2][:, i].reshape(D_MODEL, D_MODEL)
        p["s5_w_glu"] = got[3][:, i].reshape(BW, BW)
        p["m2_conv_w"], p["sc_conv_w"], p["merge_b"] = cols(got[4][:, i]), cols(got[5][:, i]), cols(got[6][:, i])
        layers.append(p)
    return layers


def _reduce_grads(grads, d_final, w):
    to_chips = lambda t: jnp.transpose(t.reshape(t.shape[0], 4, -1), (1, 0, 2))
    stack = lambda f: jnp.stack([f(g) for g in grads])
    dw_in = None
    for i, g in enumerate(grads):
        dw_in = regroup_cols(f"w_in_shards{i}", g["w_in"][None], _steps_to_shards(i), (4 * N_LAYERS, D_MODEL, SHARD_PAD), out=dw_in)
    parts = [dw_in.reshape(N_LAYERS, 4, D_MODEL, SHARD_PAD),
             stack(lambda g: to_chips(g["w_branch"].reshape(N_BRANCH * BW, D_MODEL)).astype(bf16)),
             stack(lambda g: g["w_out"].reshape(4, D_MODEL // 4, D_MODEL).astype(bf16)),
             stack(lambda g: g["s5_w_glu"].reshape(4, BW // 4, BW).astype(bf16))]
    rep = jnp.concatenate([stack(lambda g: g[k]).reshape(-1) for k in REPLICATED] + [d_final.reshape(-1)])
    quarter = -(-rep.size // (4 * 2 * SUBL * PACK_W)) * (2 * SUBL * PACK_W)
    rep = jnp.pad(rep, (0, 4 * quarter - rep.size))
    small = []
    for j in range(4):
        sharded = [stack(lambda g: to_chips(g[k])[j]) for k in SHARDED_SMALL]
        small.append(_pack(sharded + [rep[j * quarter:(j + 1) * quarter]], 2, SUBL))
    parts.append(jnp.stack(small, axis=1))
    red = reduce_to_shards(parts)
    out = {"w_in": red[0][:, :, :SHARD_W], "w_branch": red[1].reshape(w["w_branch"].shape), "w_out": red[2], "s5_w_glu": red[3]}
    small_flat = red[4].reshape(-1)
    n_small = sum(int(np.prod(w[k].shape)) for k in SHARDED_SMALL)
    out.update(zip(SHARDED_SMALL, _unpack(small_flat, [w[k].shape for k in SHARDED_SMALL])))
    mine = small_flat[n_small:n_small + quarter].reshape(2, quarter // (2 * PACK_W), PACK_W)
    rep_all = gather_chips("ag_small_grads", [mine])[0].reshape(-1)
    names = REPLICATED + ("final_norm_w",)
    out.update(zip(names, _unpack(rep_all, [w[k].shape for k in names])))
    return out


def _update(w, g, m, v):
    d, nm, nv = {}, {}, {}
    flat2 = lambda a: a.reshape(-1, a.shape[-1])
    cols_major = lambda a: jnp.transpose(a, (2, 0, 1))
    res = adamw("adamw_w_in", *[cols_major(t["w_in"]) for t in (w, g, m, v)], tr=SHARD_W // 14)
    d["w_in"], nm["w_in"], nv["w_in"] = (jnp.transpose(r, (1, 2, 0)) for r in res)
    for k in SHARDED_BIG[1:]:
        res = adamw("adamw_" + k, *[flat2(t[k]) for t in (w, g, m, v)])
        d[k], nm[k], nv[k] = (r.reshape(w[k].shape) for r in res)
    for k in ("s5_b_re", "s5_b_im"):
        res = adamw("adamw_" + k, *[flat2(t[k]) for t in (w, g, m, v)])
        d[k], nm[k], nv[k] = (r.reshape(w[k].shape) for r in res)
    rest = [k for k in WEIGHTS if k not in SHARDED_BIG + ("s5_b_re", "s5_b_im")]
    two_d = lambda a: a.reshape(1, -1) if a.ndim == 1 else a
    res = adamw_many("adamw_rest", *[[two_d(t[k]) for k in rest] for t in (w, g, m, v)])
    for tgt, rs in zip((d, nm, nv), res):
        tgt.update({k: r.reshape(w[k].shape) for k, r in zip(rest, rs)})
    return d, nm, nv


def kernel(x, norm_w, w_in, s5_lambda_re, s5_lambda_im, s5_b_re, s5_b_im, s5_c_re, s5_c_im, s5_d, s5_log_step, s5_w_glu, sgu_ln_w, sgu_ln_b, sgu_w, sgu_b, m2_conv_w, m2_conv_b, m2_dt_bias, m2_a_log, m2_d, m2_norm_w, sc_conv_w, merge_b, w_branch, w_out, final_norm_w, loss_target, m_norm_w, m_w_in, m_s5_lambda_re, m_s5_lambda_im, m_s5_b_re, m_s5_b_im, m_s5_c_re, m_s5_c_im, m_s5_d, m_s5_log_step, m_s5_w_glu, m_sgu_ln_w, m_sgu_ln_b, m_sgu_w, m_sgu_b, m_m2_conv_w, m_m2_conv_b, m_m2_dt_bias, m_m2_a_log, m_m2_d, m_m2_norm_w, m_sc_conv_w, m_merge_b, m_w_branch, m_w_out, m_final_norm_w, v_norm_w, v_w_in, v_s5_lambda_re, v_s5_lambda_im, v_s5_b_re, v_s5_b_im, v_s5_c_re, v_s5_c_im, v_s5_d, v_s5_log_step, v_s5_w_glu, v_sgu_ln_w, v_sgu_ln_b, v_sgu_w, v_sgu_b, v_m2_conv_w, v_m2_conv_b, v_m2_dt_bias, v_m2_a_log, v_m2_d, v_m2_norm_w, v_sc_conv_w, v_merge_b, v_w_branch, v_w_out, v_final_norm_w):
    given = dict(locals())
    w = {k: given[k] for k in WEIGHTS}
    m = {k: given["m_" + k] for k in WEIGHTS}
    v = {k: given["v_" + k] for k in WEIGHTS}
    layers = _gather_weights(w)
    loss, dx, grads, d_final = local_step(x[0], loss_target[0], layers, final_norm_w)
    loss = lax.psum(loss, ("x", "y", "c"))
    g = _reduce_grads(grads, d_final, w)
    d, nm, nv = _update(w, g, m, v)
    return (loss, dx[None], *[g[k] for k in WEIGHTS], *[d[k] for k in WEIGHTS],
            *[nm[k] for k in WEIGHTS], *[nv[k] for k in WEIGHTS])
```

```python
import functools
from typing import Any, Callable, NamedTuple

import numpy as np
import jax
import jax.numpy as jnp
from jax import lax
from jax.experimental import pallas as pl
from jax.experimental.pallas import tpu as pltpu

f32 = jnp.float32
bf16 = jnp.bfloat16

D_MODEL = 1024
BW = 512
N_BRANCH = 4
EPS = 1e-6
S5_GROUPS, S5_P, S5_N = 32, 16, 64
S5_NS = S5_GROUPS * S5_N
CHUNK = 128
M2_HEADS, M2_HEAD_DIM, M2_GROUPS, M2_STATE = 8, 64, 2, 128
IN_DIM = 10248
PW = 10368
PW_MAIN = 10240
LANES = 128
VMEM_LIMIT = 60 * 1024 * 1024

ADAM_LR, ADAM_B1, ADAM_B2, ADAM_EPS, ADAM_WD, ADAM_STEP = 0.001, 0.9, 0.999, 1e-08, 0.01, 10

C_MERGE = 0
C_SC = 4096
C_SGU = 6144
C_S5G = 7680
C_S5U = 8192
C_M2Z = 8704
C_XBC = 9216
C_DT = 10240


def _col_segments():
    segs = [(6152, 4096)]
    for j in range(4):
        segs += [(4104 + 128 * j, 128), (4616 + 128 * j, 128), (5128 + 128 * j, 128), (5640 + 128 * j, 128)]
    segs += [(1024, 1536), (512, 512), (0, 512), (2560, 512), (3072, 1024), (4096, 8)]
    return segs


NN = ((1,), (0,))
NT = ((1,), (1,))
TN = ((0,), (0,))


def _bd(a, b, dims):
    return lax.dot_general(a.astype(bf16), b.astype(bf16), (dims, ((), ())), preferred_element_type=f32)


def _hd(a, b, dims):
    return lax.dot_general(a, b, (dims, ((), ())), precision=lax.Precision.HIGHEST, preferred_element_type=f32)


def _make_dots(raw):
    @jax.custom_vjp
    def nn(a, b):
        return raw(a, b, NN)
    nn.defvjp(lambda a, b: (raw(a, b, NN), (a, b)), lambda r, g: (raw(g, r[1], NT), raw(r[0], g, TN)))

    @jax.custom_vjp
    def nt(a, b):
        return raw(a, b, NT)
    nt.defvjp(lambda a, b: (raw(a, b, NT), (a, b)), lambda r, g: (raw(g, r[1], NN), raw(g, r[0], TN)))

    @jax.custom_vjp
    def tn(a, b):
        return raw(a, b, TN)
    tn.defvjp(lambda a, b: (raw(a, b, TN), (a, b)), lambda r, g: (raw(r[1], g, NT), raw(r[0], g, NN)))
    return nn, nt, tn


bdot, bdot_nt, bdot_tn = _make_dots(_bd)
hdot, hdot_nt, hdot_tn = _make_dots(_hd)


@jax.custom_vjp
def bdot_w(a, w, shadow):
    return _bd(a, w, NN)


bdot_w.defvjp(lambda a, w, s: (_bd(a, w, NN), (a, w)),
              lambda r, g: (_bd(g, r[1], NT), jnp.zeros_like(r[1]), _bd(r[0], g, TN)))


def _rows(shape):
    return lax.broadcasted_iota(jnp.int32, shape, 0)


def _cols(shape):
    return lax.broadcasted_iota(jnp.int32, shape, 1)


def _shift_down(x, s):
    return jnp.where(_rows(x.shape) < s, 0.0, pltpu.roll(x, s, 0))


def _shift_up(x, s):
    n = x.shape[0]
    return jnp.where(_rows(x.shape) >= n - s, 0.0, pltpu.roll(x, n - s, 0))


@functools.partial(jax.custom_vjp, nondiff_argnums=(1,))
def shift(x, s):
    return _shift_down(x, s) if s else x


shift.defvjp(lambda x, s: (shift(x, s), None), lambda s, _, g: (_shift_up(g, s) if s else g,))


def _row_of(w, k):
    return jnp.sum(jnp.where(_rows(w.shape) == k, w, 0.0), axis=0, keepdims=True)


def _lane_mask(width, lo, hi):
    c = _cols((1, width))
    return ((c >= lo) & (c < hi)).astype(f32)


def _expand(rows, width, per):
    return (_cols((rows, width)) // per == _rows((rows, width))).astype(f32)


class A(NamedTuple):
    arr: Any
    block: tuple
    imap: Callable
    shadow: bool = False


class O(NamedTuple):
    shape: tuple
    dtype: Any
    block: tuple
    imap: Callable
    alias: Any = None


class R(NamedTuple):
    arg: int
    out: int
    off: Any = None
    acc: bool = False


def _cparams(n_grid):
    return pltpu.CompilerParams(dimension_semantics=("arbitrary",) * n_grid, vmem_limit_bytes=VMEM_LIMIT)


def _ispec(block, imap, n, rev):
    if rev:
        return pl.BlockSpec(block, lambda i: imap(n - 1 - i))
    return pl.BlockSpec(block, imap)


def _load(ref, a):
    v = ref[...]
    if a.shadow:
        return (v, jnp.zeros(v.shape, f32))
    return v.astype(f32)


def _save_spec(shape, n, rev):
    nd = len(shape)
    return _ispec((None,) + tuple(shape), lambda i: (i,) + (0,) * nd, n, rev)


def block_fwd(name, f, n, args, outs, carries=()):
    n_in, n_out, n_c = len(args), len(outs), len(carries)

    def body(*refs):
        ins, out_r = refs[:n_in], refs[n_in:n_in + n_out]
        saves, cs = refs[n_in + n_out:n_in + n_out + n_c], refs[n_in + n_out + n_c:]
        if n_c:
            @pl.when(pl.program_id(0) == 0)
            def _():
                for c in cs:
                    c[...] = jnp.zeros(c.shape, f32)
        vals = [_load(r, a) for r, a in zip(ins, args)]
        cv = [c[...] for c in cs]
        for s, v in zip(saves, cv):
            s[...] = v
        res = f(*vals, *cv)
        for r, v in zip(out_r, res[:n_out]):
            r[...] = v.astype(r.dtype)
        for c, v in zip(cs, res[n_out:]):
            c[...] = v

    out_shape = [jax.ShapeDtypeStruct(o.shape, o.dtype) for o in outs]
    out_specs = [pl.BlockSpec(o.block, o.imap) for o in outs]
    for shp in carries:
        out_shape.append(jax.ShapeDtypeStruct((n,) + tuple(shp), f32))
        out_specs.append(_save_spec(shp, n, False))
    return pl.pallas_call(
        body, name=name, grid=(n,),
        in_specs=[pl.BlockSpec(a.block, a.imap) for a in args],
        out_specs=out_specs, out_shape=out_shape,
        scratch_shapes=[pltpu.VMEM(tuple(shp), f32) for shp in carries],
        compiler_params=_cparams(1),
    )(*[a.arr for a in args])


def block_bwd(name, f, n, args, cots, gouts, routes, saved=(), rev=False):
    n_in, n_cot, n_c, n_go = len(args), len(cots), len(saved), len(gouts)
    diff = []
    for r in routes:
        if r.arg not in diff:
            diff.append(r.arg)
    aliases = [(k, o.alias) for k, o in enumerate(gouts) if o.alias is not None]

    def body(*refs):
        ins = refs[:n_in]
        cot_r = refs[n_in:n_in + n_cot]
        sav_r = refs[n_in + n_cot:n_in + n_cot + n_c]
        base = n_in + n_cot + n_c + len(aliases)
        go_r = refs[base:base + n_go]
        dcs = refs[base + n_go:]
        step = pl.program_id(0)
        if n_c:
            @pl.when(step == 0)
            def _():
                for d in dcs:
                    d[...] = jnp.zeros(d.shape, f32)
        vals = [_load(r, a) for r, a in zip(ins, args)]
        cv = [s[...] for s in sav_r]
        nd = len(diff)

        def g(*dv):
            full = list(vals)
            for idx, v in zip(diff, dv[:nd]):
                full[idx] = (vals[idx][0], v) if args[idx].shadow else v
            return tuple(f(*full, *dv[nd:]))

        primals = [vals[i][1] if args[i].shadow else vals[i] for i in diff] + cv
        _, vjp_fn = jax.vjp(g, *primals)
        ct = tuple([r[...].astype(f32) for r in cot_r] + [d[...] for d in dcs])
        grads = vjp_fn(ct)
        for r in routes:
            gr = grads[diff.index(r.arg)]
            ref = go_r[r.out]
            if r.acc:
                @pl.when(step == 0)
                def _(ref=ref, gr=gr):
                    ref[...] = gr.astype(ref.dtype)

                @pl.when(step > 0)
                def _(ref=ref, gr=gr):
                    ref[...] += gr.astype(ref.dtype)
            elif r.off is None:
                ref[...] = gr.astype(ref.dtype)
            else:
                ref[:, r.off:r.off + gr.shape[1]] = gr.astype(ref.dtype)
        for d, gr in zip(dcs, grads[nd:]):
            d[...] = gr

    in_specs = [_ispec(a.block, a.imap, n, rev) for a in list(args) + list(cots)]
    in_specs += [_save_spec(s.shape[1:], n, rev) for s in saved]
    in_specs += [pl.BlockSpec(memory_space=pl.ANY) for _ in aliases]
    operands = [a.arr for a in list(args) + list(cots)] + list(saved) + [arr for _, arr in aliases]
    io_alias = {n_in + n_cot + n_c + j: k for j, (k, _) in enumerate(aliases)}
    return pl.pallas_call(
        body, name=name, grid=(n,),
        in_specs=in_specs,
        out_specs=[_ispec(o.block, o.imap, n, rev) for o in gouts],
        out_shape=[jax.ShapeDtypeStruct(o.shape, o.dtype) for o in gouts],
        scratch_shapes=[pltpu.VMEM(tuple(s.shape[1:]), f32) for s in saved],
        input_output_aliases=io_alias,
        compiler_params=_cparams(1),
    )(*operands)


class Op(NamedTuple):
    arr: Any
    row: int = 0
    col: int = 0


def mm(name, mode, pairs, m, n, tm, tn, out_dtype=f32, add=None, out=None, out_col=0, out_width=None):
    tm, tn = min(tm, m), min(tn, n)
    assert m % tm == 0 and n % tn == 0
    in_specs, operands = [], []
    for a, b, k, _ in pairs:
        if mode == TN:
            assert a.row % k == 0 and a.col % tm == 0
            in_specs.append(pl.BlockSpec((k, tm), lambda j, i, a=a, k=k: (a.row // k, i + a.col // tm)))
        else:
            assert a.col % k == 0 and a.row % tm == 0
            in_specs.append(pl.BlockSpec((tm, k), lambda j, i, a=a, k=k: (i + a.row // tm, a.col // k)))
        if mode == NT:
            assert b.col % k == 0 and b.row % tn == 0
            in_specs.append(pl.BlockSpec((tn, k), lambda j, i, b=b, k=k: (j + b.row // tn, b.col // k)))
        else:
            assert b.row % k == 0 and b.col % tn == 0
            in_specs.append(pl.BlockSpec((k, tn), lambda j, i, b=b, k=k: (b.row // k, j + b.col // tn)))
        operands += [a.arr, b.arr]
    n_p = len(pairs)
    if add is not None:
        assert add.col % tn == 0
        in_specs.append(pl.BlockSpec((tm, tn), lambda j, i: (i, j + add.col // tn)))
        operands.append(add.arr)
    io_alias = {}
    if out is not None:
        assert out_col % tn == 0
        in_specs.append(pl.BlockSpec(memory_space=pl.ANY))
        operands.append(out)
        io_alias = {len(operands) - 1: 0}
        out_shape = jax.ShapeDtypeStruct(out.shape, out.dtype)
    else:
        out_shape = jax.ShapeDtypeStruct((m, out_width or n), out_dtype)
    signs = [p[3] for p in pairs]

    def body(*refs):
        o = refs[-1]
        acc = None
        for p in range(n_p):
            t = _bd(refs[2 * p][...], refs[2 * p + 1][...], mode)
            t = t if signs[p] > 0 else -t
            acc = t if acc is None else acc + t
        if add is not None:
            acc = acc + refs[2 * n_p][...].astype(f32)
        o[...] = acc.astype(o.dtype)

    return pl.pallas_call(
        body, name=name, grid=(n // tn, m // tm),
        in_specs=in_specs,
        out_specs=pl.BlockSpec((tm, tn), lambda j, i: (i, j + out_col // tn)),
        out_shape=out_shape, input_output_aliases=io_alias,
        compiler_params=_cparams(2),
    )(*operands)


SCAN_LANES = 512
SUBL = 8


def _cmul(p, q):
    return (p[0] * q[0] - p[1] * q[1], p[0] * q[1] + p[1] * q[0])


def _powers(a):
    a2 = _cmul(a, a)
    a4 = _cmul(a2, a2)
    a6 = _cmul(a4, a2)
    return [a, a2, _cmul(a2, a), a4, _cmul(a4, a), a6, _cmul(a6, a), _cmul(a4, a4)]


def _table(pw, order, w):
    row = _rows((SUBL, w))
    re = sum(jnp.where(row == t, pw[k][0], 0.0) for t, k in enumerate(order))
    im = sum(jnp.where(row == t, pw[k][1], 0.0) for t, k in enumerate(order))
    return re, im


def s5_scan_fwd(bu_re, bu_im, a_re, a_im):
    seq, ns = bu_re.shape
    w, nb = SCAN_LANES, ns // SCAN_LANES

    def body(b_re, b_im, ar, ai, s_re, s_im):
        a = (ar[...], ai[...])
        pw = _powers(a)
        tab = _table(pw, list(range(SUBL)), w)
        row = _rows((SUBL, w))

        def step(i, carry):
            t0 = pl.multiple_of(i * SUBL, SUBL)
            x = (b_re[pl.ds(t0, SUBL), :], b_im[pl.ds(t0, SUBL), :])
            for d, k in ((1, 0), (2, 1), (4, 3)):
                sh = (jnp.where(row < d, 0.0, pltpu.roll(x[0], d, 0)), jnp.where(row < d, 0.0, pltpu.roll(x[1], d, 0)))
                t = _cmul(pw[k], sh)
                x = (x[0] + t[0], x[1] + t[1])
            t = _cmul(tab, carry)
            x = (x[0] + t[0], x[1] + t[1])
            s_re[pl.ds(t0, SUBL), :] = x[0]
            s_im[pl.ds(t0, SUBL), :] = x[1]
            return (x[0][SUBL - 1:, :], x[1][SUBL - 1:, :])

        z = jnp.zeros((1, w), f32)
        lax.fori_loop(0, seq // SUBL, step, (z, z), unroll=2)

    strip = pl.BlockSpec((seq, w), lambda j: (0, j))
    lane = pl.BlockSpec((1, w), lambda j: (0, j))
    return pl.pallas_call(
        body, name="s5_scan_fwd", grid=(nb,),
        in_specs=[strip, strip, lane, lane],
        out_specs=[strip, strip],
        out_shape=[jax.ShapeDtypeStruct((seq, ns), f32)] * 2,
        compiler_params=_cparams(1),
    )(bu_re, bu_im, a_re, a_im)


def s5_scan_bwd(ds_re, ds_im, s_re, s_im, a_re, a_im):
    seq, ns = ds_re.shape
    w, nb = SCAN_LANES, ns // SCAN_LANES
    nblk = seq // SUBL

    def body(g_re, g_im, sr, si, ar, ai, l_re, l_im, da_re, da_im):
        a = (ar[...], -ai[...])
        pw = _powers(a)
        tab = _table(pw, [SUBL - 1 - t for t in range(SUBL)], w)
        row = _rows((SUBL, w))

        def step(kk, carry):
            c_re, c_im, acc_re, acc_im = carry
            i = nblk - 1 - kk
            t0 = pl.multiple_of(i * SUBL, SUBL)
            x = (g_re[pl.ds(t0, SUBL), :], g_im[pl.ds(t0, SUBL), :])
            for d, k in ((1, 0), (2, 1), (4, 3)):
                sh = (jnp.where(row >= SUBL - d, 0.0, pltpu.roll(x[0], SUBL - d, 0)),
                      jnp.where(row >= SUBL - d, 0.0, pltpu.roll(x[1], SUBL - d, 0)))
                t = _cmul(pw[k], sh)
                x = (x[0] + t[0], x[1] + t[1])
            t = _cmul(tab, (c_re, c_im))
            x = (x[0] + t[0], x[1] + t[1])
            l_re[pl.ds(t0, SUBL), :] = x[0]
            l_im[pl.ds(t0, SUBL), :] = x[1]
            tp = jnp.maximum(t0 - 1, 0)
            live = (i > 0).astype(f32)
            p_re = sr[pl.ds(tp, 1), :] * live
            p_im = si[pl.ds(tp, 1), :] * live
            sp_re = jnp.where(row == 0, p_re, pltpu.roll(sr[pl.ds(t0, SUBL), :], 1, 0))
            sp_im = jnp.where(row == 0, p_im, pltpu.roll(si[pl.ds(t0, SUBL), :], 1, 0))
            acc_re = acc_re + x[0] * sp_re + x[1] * sp_im
            acc_im = acc_im + x[1] * sp_re - x[0] * sp_im
            return (x[0][:1, :], x[1][:1, :], acc_re, acc_im)

        z1 = jnp.zeros((1, w), f32)
        z8 = jnp.zeros((SUBL, w), f32)
        _, _, acc_re, acc_im = lax.fori_loop(0, nblk, step, (z1, z1, z8, z8), unroll=2)
        da_re[...] = jnp.sum(acc_re, axis=0, keepdims=True)
        da_im[...] = jnp.sum(acc_im, axis=0, keepdims=True)

    strip = pl.BlockSpec((seq, w), lambda j: (0, j))
    lane = pl.BlockSpec((1, w), lambda j: (0, j))
    return pl.pallas_call(
        body, name="s5_scan_bwd", grid=(nb,),
        in_specs=[strip, strip, strip, strip, lane, lane],
        out_specs=[strip, strip, lane, lane],
        out_shape=[jax.ShapeDtypeStruct((seq, ns), f32)] * 2 + [jax.ShapeDtypeStruct((1, ns), f32)] * 2,
        compiler_params=_cparams(1),
    )(ds_re, ds_im, s_re, s_im, a_re, a_im)


def _rms(x, w):
    return x * lax.rsqrt(jnp.mean(x * x, axis=-1, keepdims=True) + EPS) * w


def f_rms(x, w):
    return (_rms(x, w),)


def f_rms_res(x, w):
    return (_rms(x, w), x)


def f_s5_prep(lam_re, lam_im, log_step, b_re, b_im):
    e = _expand(log_step.shape[1], S5_NS, S5_N)
    step = hdot(jnp.exp(log_step), e)
    mag = jnp.exp(lam_re * step)
    ab_re, ab_im = mag * jnp.cos(lam_im * step), mag * jnp.sin(lam_im * step)
    den = lam_re * lam_re + lam_im * lam_im
    nr = ab_re - 1.0
    coef_re = (nr * lam_re + ab_im * lam_im) / den
    coef_im = (ab_im * lam_re - nr * lam_im) / den
    bb_re, bb_im = coef_re * b_re - coef_im * b_im, coef_re * b_im + coef_im * b_re
    sel = (_rows((BW, S5_P)) % S5_P == _cols((BW, S5_P))).astype(f32)
    blk = _rows((BW, S5_NS)) // S5_P == _cols((BW, S5_NS)) // S5_N
    rows_bd = lambda t: jnp.where(blk, hdot(sel, t), 0.0)
    return (ab_re, ab_im, rows_bd(bb_re), rows_bd(bb_im))


def f_s5_c(c_re, c_im):
    sel_t = (_cols((S5_P, BW)) % S5_P == _rows((S5_P, BW))).astype(f32)
    blk_t = _rows((S5_NS, BW)) // S5_N == _cols((S5_NS, BW)) // S5_P
    cols_bd = lambda t: jnp.where(blk_t, hdot_tn(t, sel_t), 0.0)
    return (cols_bd(c_re), cols_bd(c_im))


def f_s5_act(y_lin, u, gate, d, w_glu):
    y = jax.nn.gelu(y_lin + d * u)
    y = y * jax.nn.sigmoid(bdot_w(y, *w_glu))
    return (y * jax.nn.silu(gate),)


def f_sgu(u, v, gate, ln_w, ln_b, *rest):
    w_s, b_pad = rest[:8], rest[8]
    t = u.shape[0]
    u32, v32 = jax.nn.gelu(u), jax.nn.gelu(v)
    mu = jnp.mean(v32, axis=-1, keepdims=True)
    var = jnp.mean(jnp.square(v32 - mu), axis=-1, keepdims=True)
    vn = (v32 - mu) * lax.rsqrt(var + EPS) * ln_w + ln_b
    tri = _rows((t, t)) >= _cols((t, t))
    s = hdot_tn(b_pad, _expand(LANES, BW, BW // 8))
    for h in range(8):
        s = s + bdot(jnp.where(tri, w_s[h], 0.0), vn) * _lane_mask(BW, 64 * h, 64 * h + 64)
    return (u32 * s * jax.nn.silu(gate),)


def f_m2_conv(x, w, b):
    return (sum(_row_of(w, k) * shift(x, 3 - k) for k in range(4)) + b,)


def f_sc(bg, cg, h, gate, w):
    z = cg * h
    conv = sum(_row_of(w, k) * shift(z, 2 - k) for k in range(3))
    return (bg * conv * jax.nn.silu(gate),)


def f_m2(z, xc, b0, b1, c0, c1, dt_raw, dt_bias, a_log, d_par, norm_w, st):
    q = z.shape[0]
    x = jax.nn.silu(xc)
    bm, cm = (jax.nn.silu(b0), jax.nn.silu(b1)), (jax.nn.silu(c0), jax.nn.silu(c1))
    dt = jax.nn.softplus(dt_raw + dt_bias)
    da = dt * (-jnp.exp(a_log))
    tri = _rows((q, q)) >= _cols((q, q))
    acs = hdot(tri.astype(f32), da)
    e = _expand(LANES, BW, M2_HEAD_DIM)
    dt_f, acs_f = hdot(dt, e), hdot(acs, e)
    last = _rows((q, BW)) == q - 1
    alast_f = jnp.sum(jnp.where(last, acs_f, 0.0), axis=0, keepdims=True)
    xdt = x * dt_f
    xdec = xdt * jnp.exp(alast_f - acs_f)
    acs_t = acs.T
    st_new = st * jnp.exp(alast_f)
    y_diag, y_off = 0.0, 0.0
    for g in range(M2_GROUPS):
        gm = _lane_mask(BW, 256 * g, 256 * g + 256)
        cb = bdot_nt(cm[g], bm[g])
        st_new = st_new + bdot_tn(bm[g], xdec * gm)
        y_off = y_off + bdot(cm[g], st) * gm
        for hh in range(M2_HEADS // M2_GROUPS):
            h = g * (M2_HEADS // M2_GROUPS) + hh
            col = jnp.sum(jnp.where(_cols((q, LANES)) == h, acs, 0.0), axis=1, keepdims=True)
            row = jnp.sum(jnp.where(_rows((LANES, q)) == h, acs_t, 0.0), axis=0, keepdims=True)
            decay = jnp.exp(jnp.where(tri, col - row, -1e30))
            y_diag = y_diag + bdot(cb * decay, xdt) * _lane_mask(BW, 64 * h, 64 * h + 64)
    d_f = sum(jnp.sum(jnp.where(_cols((1, LANES)) == h, d_par, 0.0), axis=1, keepdims=True)
              * _lane_mask(BW, 64 * h, 64 * h + 64) for h in range(M2_HEADS))
    y = y_diag + y_off * jnp.exp(acs_f) + d_f * x
    y = y * jax.nn.silu(z)
    return (_rms(y, norm_w), st_new)


def f_branch_mix(*v):
    nb = len(v) // 4
    ys, lg, wb, mb = v[0:nb], v[nb:2 * nb], v[2 * nb:3 * nb], v[3 * nb:]
    return (sum(jax.nn.sigmoid(lg[k] + mb[k]) * bdot_w(ys[k], *wb[k]) for k in range(nb)),)


def _param(arr, shadow=False):
    return A(arr, tuple(arr.shape), lambda i, nd=arr.ndim: (0,) * nd, shadow)


def _tb(arr, t, width, colblk):
    return A(arr, (t, width), lambda i: (i, colblk))


def _strip(arr, seq, colblk0, stride=1):
    return A(arr, (seq, LANES), lambda j: (0, colblk0 + stride * j))


def _s5_prep_args(p):
    lam_re = p["s5_lambda_re"].reshape(1, S5_NS)
    lam_im = p["s5_lambda_im"].reshape(1, S5_NS)
    log_step = jnp.pad(p["s5_log_step"].reshape(1, S5_GROUPS), ((0, 0), (0, LANES - S5_GROUPS)))
    b_lanes = lambda b: jnp.transpose(b, (2, 0, 1)).reshape(S5_P, S5_NS)
    return [_param(v) for v in (lam_re, lam_im, log_step, b_lanes(p["s5_b_re"]), b_lanes(p["s5_b_im"]))]


def _s5_c_args(p):
    c_lanes = lambda c: jnp.transpose(c, (1, 0, 2)).reshape(S5_P, S5_NS)
    return [_param(c_lanes(p["s5_c_re"])), _param(c_lanes(p["s5_c_im"]))]


def layer_fwd(x, p):
    seq = x.shape[0]
    nt = seq // CHUNK
    t2 = 256
    sv = {}
    t4 = min(512, seq)
    hb = block_fwd("rms_in", f_rms, seq // t4, [_tb(x, t4, D_MODEL, 0), _param(p["norm_w"].reshape(1, D_MODEL))],
                   [O((seq, D_MODEL), bf16, (t4, D_MODEL), lambda i: (i, 0))])[0]
    proj = mm("mm_in", NN, [(Op(hb), Op(p["w_in"]), D_MODEL, 1)], seq, PW, 2048, 1152)
    whole = lambda shape, dt: O(shape, dt, shape, lambda i: (0, 0))
    ab_re, ab_im, bb_re, bb_im = block_fwd("s5_prep", f_s5_prep, 1, _s5_prep_args(p),
                                           [whole((1, S5_NS), f32)] * 2 + [whole((BW, S5_NS), bf16)] * 2)
    cc_re, cc_im = block_fwd("s5_c", f_s5_c, 1, _s5_c_args(p), [whole((S5_NS, BW), bf16)] * 2)
    bu_re = mm("mm_bu_re", NN, [(Op(proj, 0, C_S5U), Op(bb_re), BW, 1)], seq, S5_NS, 1024, 1024)
    bu_im = mm("mm_bu_im", NN, [(Op(proj, 0, C_S5U), Op(bb_im), BW, 1)], seq, S5_NS, 1024, 1024)
    s_re, s_im = s5_scan_fwd(bu_re, bu_im, ab_re, ab_im)
    y_lin = mm("mm_s5y", NN, [(Op(s_re), Op(cc_re), S5_NS, 1), (Op(s_im), Op(cc_im), S5_NS, -1)], seq, BW, 512, BW)
    s5_act_args = [_tb(y_lin, t4, BW, 0), _tb(proj, t4, BW, C_S5U // BW), _tb(proj, t4, BW, C_S5G // BW),
                   _param(p["s5_d"].reshape(1, BW)), _param(p["s5_w_glu"], True)]
    out_bw =O((seq, BW), f32, (CHUNK, BW), lambda i: (i, 0))
    y_a = block_fwd("s5_act", f_s5_act, seq // t4, s5_act_args, [O((seq, BW), f32, (t4, BW), lambda i: (i, 0))])[0]
    y_b = block_fwd("sgu", f_sgu, nt, _sgu_args(proj, p), [out_bw])[0]
    xc = block_fwd("m2_conv", f_m2_conv, 8, _m2_conv_args(proj, p, seq),
                   [O((seq, 2 * BW), f32, (seq, LANES), lambda j: (0, j))])[0]
    y_c, st_saved = block_fwd("m2_ssd", f_m2, nt, _m2_args(proj, xc, p), [out_bw], carries=[(M2_STATE, BW)])
    y_d = block_fwd("sc", f_sc, 4, _sc_args(proj, p, seq), [O((seq, BW), f32, (seq, LANES), lambda j: (0, j))])[0]
    ys = [y_a, y_b, y_c, y_d]
    merged = block_fwd("branch_mix", f_branch_mix, seq // t2, _mix_args(ys, proj, p, t2),
                       [O((seq, D_MODEL), bf16, (t2, D_MODEL), lambda i: (i, 0))])[0]
    x_new = mm("mm_out", NN, [(Op(merged), Op(p["w_out"]), D_MODEL, 1)], seq, D_MODEL, 1024, D_MODEL, add=Op(x))
    sv.update(x=x, hb=hb, proj=proj, ab=(ab_re, ab_im), bb=(bb_re, bb_im), cc=(cc_re, cc_im), s=(s_re, s_im), y_lin=y_lin,
              xc=xc, st=st_saved, ys=ys, merged=merged)
    return x_new, sv


def _sgu_args(proj, p):
    c0 = C_SGU // BW
    args = [_tb(proj, CHUNK, BW, c0), _tb(proj, CHUNK, BW, c0 + 1), _tb(proj, CHUNK, BW, c0 + 2),
            _param(p["sgu_ln_w"].reshape(1, BW)), _param(p["sgu_ln_b"].reshape(1, BW))]
    args += [A(p["sgu_w"], (None, CHUNK, CHUNK), lambda i, h=h: (h, 0, 0)) for h in range(8)]
    args.append(_param(jnp.pad(p["sgu_b"], ((0, LANES - 8), (0, 0)))))
    return args


def _m2_conv_args(proj, p, seq):
    return [_strip(proj, seq, C_XBC // LANES), A(p["m2_conv_w"], (4, LANES), lambda j: (0, j)),
            A(p["m2_conv_b"].reshape(1, 2 * BW), (1, LANES), lambda j: (0, j))]


def _pad_lanes(v):
    return jnp.pad(v.reshape(1, -1), ((0, 0), (0, LANES - v.size)))


def _m2_args(proj, xc, p):
    args = [_tb(proj, CHUNK, BW, C_M2Z // BW), _tb(xc, CHUNK, BW, 0)]
    args += [_tb(xc, CHUNK, LANES, 4 + k) for k in range(4)]
    args.append(_tb(proj, CHUNK, LANES, C_DT // LANES))
    args += [_param(_pad_lanes(p["m2_dt_bias"])), _param(_pad_lanes(p["m2_a_log"])), _param(_pad_lanes(p["m2_d"])),
             _param(p["m2_norm_w"].reshape(1, BW))]
    return args


def _sc_args(proj, p, seq):
    c0 = C_SC // LANES
    return [_strip(proj, seq, c0 + k, 4) for k in range(4)] + [A(p["sc_conv_w"], (3, LANES), lambda j: (0, j))]


def _mix_args(ys, proj, p, t, ks=range(N_BRANCH)):
    args = [_tb(ys[k], t, BW, 0) for k in ks]
    args += [_tb(proj, t, D_MODEL, k) for k in ks]
    args += [A(p["w_branch"], (BW, D_MODEL), lambda i, k=k: (k, 0), True) for k in ks]
    mb = p["merge_b"].reshape(N_BRANCH, 1, D_MODEL)
    args += [A(mb, (None, 1, D_MODEL), lambda i, k=k: (k, 0, 0)) for k in ks]
    return args


def layer_bwd(d_out, p, sv):
    seq = d_out.shape[0]
    nt = seq // CHUNK
    t4 = min(512, seq)
    proj, ys = sv["proj"], sv["ys"]
    g = {}
    acc = lambda shape: O(tuple(shape), f32, tuple(shape), lambda i, nd=len(shape): (0,) * nd)
    d_merged = mm("mm_out_dx", NT, [(Op(d_out), Op(p["w_out"]), D_MODEL, 1)], seq, D_MODEL, 1024, D_MODEL, out_dtype=f32)
    g["w_out"] = mm("mm_out_dw", TN, [(Op(sv["merged"]), Op(d_out), seq, 1)], D_MODEL, D_MODEL, 512, D_MODEL, out_dtype=bf16)
    dys, dwb, dmb, dproj, tb = [], [], [], None, min(256, seq)
    for half in range(2):
        gouts = [O((seq, BW), f32, (tb, BW), lambda i: (i, 0)) for _ in range(2)]
        gouts.append(O((seq, PW_MAIN), bf16, (tb, 2 * D_MODEL), lambda i, half=half: (i, half), alias=dproj))
        gouts += [acc((BW, D_MODEL)) for _ in range(2)] + [acc((1, D_MODEL)) for _ in range(2)]
        routes = [R(k, k) for k in range(2)] + [R(2 + k, 2, k * D_MODEL) for k in range(2)]
        routes += [R(4 + k, 3 + k, acc=True) for k in range(2)] + [R(6 + k, 5 + k, acc=True) for k in range(2)]
        res = block_bwd(f"branch_mix_bwd{half}", f_branch_mix, seq // tb, _mix_args(ys, proj, p, tb, (2 * half, 2 * half + 1)),
                        [_tb(d_merged, tb, D_MODEL, 0)], gouts, routes)
        dys, dproj, dwb, dmb = dys + list(res[:2]), res[2], dwb + list(res[3:5]), dmb + list(res[5:7])
    g["w_branch"] = jnp.stack(dwb)
    g["merge_b"] = jnp.concatenate(dmb, axis=0)
    res = block_bwd("sc_bwd", f_sc, 4, _sc_args(proj, p, seq), [_strip(dys[3], seq, 0)],
                    [O((seq, PW_MAIN), bf16, (seq, 4 * LANES), lambda j: (0, C_SC // (4 * LANES) + j), alias=dproj),
                     O((3, BW), f32, (3, LANES), lambda j: (0, j))],
                    [R(k, 0, k * LANES) for k in range(4)] + [R(4, 1)])
    dproj, g["sc_conv_w"] = res
    res = block_bwd("m2_ssd_bwd", f_m2, nt, _m2_args(proj, sv["xc"], p), [_tb(dys[2], CHUNK, BW, 0)],
                    [O((seq, PW_MAIN), bf16, (CHUNK, BW), lambda i: (i, C_M2Z // BW), alias=dproj),
                     O((seq, 2 * BW), f32, (CHUNK, 2 * BW), lambda i: (i, 0)),
                     O((seq, LANES), bf16, (CHUNK, LANES), lambda i: (i, 0)),
                     acc((1, LANES)), acc((1, LANES)), acc((1, LANES)), acc((1, BW))],
                    [R(0, 0), R(1, 1, 0)] + [R(2 + k, 1, BW + k * LANES) for k in range(4)] + [R(6, 2)]
                    + [R(7, 3, acc=True), R(8, 4, acc=True), R(9, 5, acc=True), R(10, 6, acc=True)],
                    saved=[sv["st"]], rev=True)
    dproj, dxc, d_dt = res[0], res[1], res[2]
    g["m2_dt_bias"], g["m2_a_log"], g["m2_d"] = (r[0, :M2_HEADS] for r in res[3:6])
    g["m2_norm_w"] = res[6].reshape(BW)
    res = block_bwd("m2_conv_bwd", f_m2_conv, 8, _m2_conv_args(proj, p, seq), [_strip(dxc, seq, 0)],
                    [O((seq, PW_MAIN), bf16, (seq, LANES), lambda j: (0, C_XBC // LANES + j), alias=dproj),
                     O((4, 2 * BW), f32, (4, LANES), lambda j: (0, j)), O((1, 2 * BW), f32, (1, LANES), lambda j: (0, j))],
                    [R(0, 0), R(1, 1), R(2, 2)])
    dproj, g["m2_conv_w"], cb = res
    g["m2_conv_b"] = cb.reshape(2 * BW)
    res = block_bwd("sgu_bwd", f_sgu, nt, _sgu_args(proj, p), [_tb(dys[1], CHUNK, BW, 0)],
                    [O((seq, PW_MAIN), bf16, (CHUNK, 3 * BW), lambda i: (i, C_SGU // (3 * BW)), alias=dproj),
                     acc((1, BW)), acc((1, BW))] + [acc((CHUNK, CHUNK)) for _ in range(8)] + [acc((LANES, CHUNK))],
                    [R(0, 0, 0), R(1, 0, BW), R(2, 0, 2 * BW), R(3, 1, acc=True), R(4, 2, acc=True)]
                    + [R(5 + h, 3 + h, acc=True) for h in range(8)] + [R(13, 11, acc=True)])
    dproj = res[0]
    g["sgu_ln_w"], g["sgu_ln_b"] = res[1].reshape(BW), res[2].reshape(BW)
    g["sgu_w"] = jnp.stack(res[3:11])
    g["sgu_b"] = res[11][:8]
    y_lin, (s_re, s_im), (ab_re, ab_im) = sv["y_lin"], sv["s"], sv["ab"]
    s5_act_args = [_tb(y_lin, t4, BW, 0), _tb(proj, t4, BW, C_S5U // BW), _tb(proj, t4, BW, C_S5G // BW),
                   _param(p["s5_d"].reshape(1, BW)), _param(p["s5_w_glu"], True)]
    res = block_bwd("s5_act_bwd", f_s5_act, seq // t4, s5_act_args, [_tb(dys[0], t4, BW, 0)],
                    [O((seq, BW), bf16, (t4, BW), lambda i: (i, 0)), O((seq, BW), f32, (t4, BW), lambda i: (i, 0)),
                     O((seq, PW_MAIN), bf16, (t4, BW), lambda i: (i, C_S5G // BW), alias=dproj),
                     acc((1, BW)), acc((BW, BW))],
                    [R(0, 0), R(1, 1), R(2, 2), R(3, 3, acc=True), R(4, 4, acc=True)])
    dy_lin, du1, dproj = res[0], res[1], res[2]
    g["s5_d"] = res[3].reshape(S5_GROUPS, S5_P)
    g["s5_w_glu"] = res[4]
    (bb_re, bb_im), (cc_re, cc_im) = sv["bb"], sv["cc"]
    ds_re = mm("mm_s5y_dre", NT, [(Op(dy_lin), Op(cc_re), BW, 1)], seq, S5_NS, 1024, 1024)
    ds_im = mm("mm_s5y_dim", NT, [(Op(dy_lin), Op(cc_im), BW, -1)], seq, S5_NS, 1024, 1024)
    dc_re = mm("mm_s5y_dcre", TN, [(Op(s_re), Op(dy_lin), seq, 1)], S5_NS, BW, 512, BW)
    dc_im = mm("mm_s5y_dcim", TN, [(Op(s_im), Op(dy_lin), seq, -1)], S5_NS, BW, 512, BW)
    l_re, l_im, da_re, da_im = s5_scan_bwd(ds_re, ds_im, s_re, s_im, ab_re, ab_im)
    dproj = mm("mm_bu_dx", NT, [(Op(l_re), Op(bb_re), S5_NS, 1), (Op(l_im), Op(bb_im), S5_NS, 1)],
               seq, BW, 512, BW, add=Op(du1), out=dproj, out_col=C_S5U)
    dbb = [mm(f"mm_bu_dw{n}", TN, [(Op(proj, 0, C_S5U), Op(l), seq, 1)], BW, S5_NS, 512, 1024)
           for n, l in (("re", l_re), ("im", l_im))]
    gouts = [acc((1, S5_NS)), acc((1, S5_NS)), acc((1, LANES))] + [acc((S5_P, S5_NS))] * 2
    res = block_bwd("s5_prep_bwd", f_s5_prep, 1, _s5_prep_args(p),
                    [_param(v) for v in (da_re, da_im, dbb[0], dbb[1])], gouts, [R(k, k, acc=True) for k in range(5)])
    g["s5_lambda_re"], g["s5_lambda_im"] = res[0].reshape(S5_GROUPS, S5_N), res[1].reshape(S5_GROUPS, S5_N)
    g["s5_log_step"] = res[2][0, :S5_GROUPS]
    b_natural = lambda b: jnp.transpose(b.reshape(S5_P, S5_GROUPS, S5_N), (1, 2, 0))
    c_natural = lambda c: jnp.transpose(c.reshape(S5_P, S5_GROUPS, S5_N), (1, 0, 2))
    g["s5_b_re"], g["s5_b_im"] = b_natural(res[3]), b_natural(res[4])
    res = block_bwd("s5_c_bwd", f_s5_c, 1, _s5_c_args(p), [_param(dc_re), _param(dc_im)],
                    [acc((S5_P, S5_NS))] * 2, [R(0, 0, acc=True), R(1, 1, acc=True)])
    g["s5_c_re"], g["s5_c_im"] = c_natural(res[0]), c_natural(res[1])
    hb, w_in = sv["hb"], p["w_in"]
    dh = mm("mm_in_dx", NT, [(Op(dproj), Op(w_in), PW_MAIN, 1), (Op(d_dt), Op(w_in, 0, PW_MAIN), LANES, 1)],
            seq, D_MODEL, 512, 512)
    dw = mm("mm_in_dw", TN, [(Op(hb), Op(dproj), seq, 1)], D_MODEL, PW_MAIN, 1024, 2048, out_dtype=bf16, out_width=PW)
    g["w_in"] = mm("mm_in_dwdt", TN, [(Op(hb), Op(d_dt), seq, 1)], D_MODEL, LANES, 256, LANES, out=dw, out_col=C_DT)
    dx, dnw = block_bwd("rms_in_bwd", f_rms_res, seq // t4, [_tb(sv["x"], t4, D_MODEL, 0), _param(p["norm_w"].reshape(1, D_MODEL))],
                        [_tb(dh, t4, D_MODEL, 0), _tb(d_out, t4, D_MODEL, 0)],
                        [O((seq, D_MODEL), f32, (t4, D_MODEL), lambda i: (i, 0)), acc((1, D_MODEL))],
                        [R(0, 0), R(1, 1, acc=True)])
    g["norm_w"] = dnw.reshape(D_MODEL)
    return dx, g


def loss_head(x, w, target):
    seq = x.shape[0]
    t = min(512, seq)

    def body(x_ref, w_ref, t_ref, loss_ref, dx_ref, dw_ref):
        step = pl.program_id(0)

        def f(xv, wv):
            err = _rms(xv, wv) - t_ref[...]
            return 0.5 * jnp.sum(jnp.mean(err * err, axis=-1, keepdims=True), axis=0, keepdims=True)

        val, vjp_fn = jax.vjp(f, x_ref[...], w_ref[...])
        dx, dw = vjp_fn(jnp.ones((1, 1), f32))
        dx_ref[...] = dx

        @pl.when(step == 0)
        def _():
            loss_ref[...] = jnp.broadcast_to(val, loss_ref.shape)
            dw_ref[...] = dw

        @pl.when(step > 0)
        def _():
            loss_ref[...] += jnp.broadcast_to(val, loss_ref.shape)
            dw_ref[...] += dw

    blk = pl.BlockSpec((t, D_MODEL), lambda i: (i, 0))
    row = pl.BlockSpec((1, D_MODEL), lambda i: (0, 0))
    return pl.pallas_call(
        body, name="loss_head", grid=(seq // t,),
        in_specs=[blk, row, blk],
        out_specs=[pl.BlockSpec((1, LANES), lambda i: (0, 0)), blk, row],
        out_shape=[jax.ShapeDtypeStruct((1, LANES), f32), jax.ShapeDtypeStruct((seq, D_MODEL), f32),
                   jax.ShapeDtypeStruct((1, D_MODEL), f32)],
        compiler_params=_cparams(1),
    )(x, w.reshape(1, D_MODEL), target)


LAYER_KEYS = ("norm_w", "w_in", "s5_lambda_re", "s5_lambda_im", "s5_b_re", "s5_b_im", "s5_c_re", "s5_c_im", "s5_d",
              "s5_log_step", "s5_w_glu", "sgu_ln_w", "sgu_ln_b", "sgu_w", "sgu_b", "m2_conv_w", "m2_conv_b",
              "m2_dt_bias", "m2_a_log", "m2_d", "m2_norm_w", "sc_conv_w", "merge_b", "w_branch", "w_out")


def local_step(x, target, layers, final_norm_w):
    saved = []
    for p in layers:
        x, sv = layer_fwd(x, p)
        saved.append(sv)
    loss, dx, dfw = loss_head(x, final_norm_w, target)
    grads = []
    for p, sv in zip(reversed(layers), reversed(saved)):
        dx, g = layer_bwd(dx, p, sv)
        grads.append(g)
    return loss[0, 0], dx, grads[::-1], dfw.reshape(D_MODEL)


MESH = pl.DeviceIdType.MESH
ANY = pl.BlockSpec(memory_space=pl.ANY)


def _me():
    return lax.axis_index("x"), lax.axis_index("y"), lax.axis_index("c")


def _other_chips(x, y):
    return [(1 - x, y), (x, 1 - y), (1 - x, 1 - y)]


def _rcopy(src, dst, send, recv, dev):
    return pltpu.make_async_remote_copy(src_ref=src, dst_ref=dst, send_sem=send, recv_sem=recv,
                                        device_id=dev, device_id_type=MESH)


def _route_cut(rows, dtype):
    tile = 2 * SUBL * (4 // jnp.dtype(dtype).itemsize)
    return rows // 2 if rows % tile == 0 else rows


def _comm_call(name, body, arrs, out_shape, n_remote, aliases=None):
    n = len(arrs)
    return pl.pallas_call(
        body, name=name, in_specs=[ANY] * n, out_specs=[ANY] * len(out_shape), out_shape=out_shape,
        scratch_shapes=[pltpu.SemaphoreType.DMA((n, n_remote)), pltpu.SemaphoreType.DMA((n, n_remote))],
        input_output_aliases=aliases or {},
        compiler_params=pltpu.CompilerParams(has_side_effects=True),
    )(*arrs)


def gather_chips(name, arrs):
    n = len(arrs)
    cut = [_route_cut(a.shape[1], a.dtype) for a in arrs]

    def body(*refs):
        ins, outs = refs[:n], refs[n:2 * n]
        send, recv = refs[2 * n:]
        x, y, c = _me()
        jme, jx, jy, jd = 2 * x + y, 2 * (1 - x) + y, 2 * x + 1 - y, 2 * (1 - x) + 1 - y
        to_x, to_y, sib = (1 - x, y, c), (x, 1 - y, c), (x, y, 1 - c)

        def part(ref, a, hi):
            return ref.at[pl.ds(cut[a], ref.shape[0] - cut[a])] if hi else ref.at[pl.ds(0, cut[a])]

        def cp(a, k, ref, dev):
            return _rcopy(ref, ref, send.at[a, k], recv.at[a, k], dev)

        split = [a for a in range(n) if cut[a] < arrs[a].shape[1]]
        sent = [_rcopy(ins[a].at[c], outs[a].at[jme, c], send.at[a, k], recv.at[a, k], dev)
                for a in range(n) for k, dev in ((0, to_x), (1, to_y))]
        for s in sent:
            s.start()
        for a in range(n):
            blk = outs[a].at[jx, c]
            cp(a, 0, blk, to_x).wait_recv()
            sent += [cp(a, 2, part(blk, a, False), to_y), cp(a, 4, blk, sib)]
            sent[-2].start()
            sent[-1].start()
        for a in range(n):
            blk = outs[a].at[jy, c]
            cp(a, 1, blk, to_y).wait_recv()
            sent.append(cp(a, 5, blk, sib))
            sent[-1].start()
            if a in split:
                sent.append(cp(a, 3, part(blk, a, True), to_x))
                sent[-1].start()
        for a in range(n):
            lo = part(outs[a].at[jd, c], a, False)
            cp(a, 2, lo, to_y).wait_recv()
            sent.append(cp(a, 6, lo, sib))
            sent[-1].start()
        for a in split:
            hi = part(outs[a].at[jd, c], a, True)
            cp(a, 3, hi, to_x).wait_recv()
            sent.append(cp(a, 7, hi, sib))
            sent[-1].start()
        for a in range(n):
            cp(a, 4, outs[a].at[jx, 1 - c], sib).wait_recv()
            cp(a, 5, outs[a].at[jy, 1 - c], sib).wait_recv()
            cp(a, 6, part(outs[a].at[jd, 1 - c], a, False), sib).wait_recv()
        for a in split:
            cp(a, 7, part(outs[a].at[jd, 1 - c], a, True), sib).wait_recv()
        for s in sent:
            s.wait_send()

    out_shape = [jax.ShapeDtypeStruct((4,) + a.shape, a.dtype) for a in arrs]
    got = _comm_call(name, body, arrs, out_shape, 8)
    jme = 2 * lax.axis_index("x") + lax.axis_index("y")
    return [lax.dynamic_update_index_in_dim(g, a, jme, 0) for g, a in zip(got, arrs)]


def swap_halves(name, arrs):
    n = len(arrs)

    def body(*refs):
        ins, outs = refs[:n], refs[n:2 * n]
        send, recv = refs[2 * n:]
        x, y, c = _me()
        remote = [_rcopy(ins[a].at[1 - c], outs[a], send.at[a, 0], recv.at[a, 0], (x, y, 1 - c)) for a in range(n)]
        for cp in remote:
            cp.start()
        for cp in remote:
            cp.wait()

    return _comm_call(name, body, arrs, [jax.ShapeDtypeStruct(a.shape[1:], a.dtype) for a in arrs], 1)


def exchange_chips(name, arrs):
    n = len(arrs)

    def body(*refs):
        ins, outs = refs[:n], refs[n:2 * n]
        send, recv = refs[2 * n:]
        x, y, c = _me()
        remote = [_rcopy(ins[a].at[2 * cx + cy], outs[a].at[k], send.at[a, k], recv.at[a, k], (cx, cy, c))
                  for a in range(n) for k, (cx, cy) in enumerate(_other_chips(x, y))]
        for cp in remote:
            cp.start()
        for cp in remote:
            cp.wait()

    return _comm_call(name, body, arrs, [jax.ShapeDtypeStruct((3,) + a.shape[1:], a.dtype) for a in arrs], 3)


def gather_cores(name, arrs):
    n = len(arrs)

    def body(*refs):
        bufs = refs[n:2 * n]
        send, recv = refs[2 * n:]
        x, y, c = _me()
        remote = [_rcopy(bufs[a].at[c], bufs[a].at[c], send.at[a, 0], recv.at[a, 0], (x, y, 1 - c)) for a in range(n)]
        for cp in remote:
            cp.start()
        for a in range(n):
            _rcopy(bufs[a].at[1 - c], bufs[a].at[1 - c], send.at[a, 0], recv.at[a, 0], (x, y, 1 - c)).wait_recv()
        for cp in remote:
            cp.wait_send()

    return _comm_call(name, body, arrs, [jax.ShapeDtypeStruct(a.shape, a.dtype) for a in arrs], 1,
                      aliases={a: a for a in range(n)})


ROW_BLOCK = 512


def esum(name, terms, rows, width, out_dtype, out_slots=None):
    tr = next((t for t in range(min(rows, ROW_BLOCK), 0, -SUBL) if rows % t == 0 and t % SUBL == 0), rows)
    where =jnp.stack([lax.axis_index("c"), 2 * lax.axis_index("x") + lax.axis_index("y")]).astype(jnp.int32)
    pick = {"c": 0, "j": 1}

    def body(s_ref, *refs):
        acc = refs[0][...].astype(f32)
        for r in refs[1:-1]:
            acc = acc + r[...].astype(f32)
        refs[-1][...] = acc.astype(out_dtype)

    specs = []
    for arr, lead in terms:
        if lead is None:
            specs.append(pl.BlockSpec((tr, width), lambda i, s: (i, 0)))
        elif isinstance(lead, str):
            specs.append(pl.BlockSpec((None, tr, width), lambda i, s, lead=lead: (s[pick[lead]], i, 0)))
        else:
            specs.append(pl.BlockSpec((None, tr, width), lambda i, s, lead=lead: (lead, i, 0)))
    if out_slots is None:
        out_spec = pl.BlockSpec((tr, width), lambda i, s: (i, 0))
        out_shape = jax.ShapeDtypeStruct((rows, width), out_dtype)
    else:
        out_spec = pl.BlockSpec((None, tr, width), lambda i, s: (s[0], i, 0))
        out_shape = jax.ShapeDtypeStruct((out_slots, rows, width), out_dtype)
    return pl.pallas_call(
        body, name=name,
        grid_spec=pltpu.PrefetchScalarGridSpec(num_scalar_prefetch=1, grid=(rows // tr,), in_specs=specs, out_specs=out_spec),
        out_shape=out_shape, compiler_params=_cparams(1),
    )(where, *[t[0] for t in terms])


def reduce_to_shards(parts):
    tags = [str(k) for k in range(len(parts))]
    theirs = swap_halves("rs_swap", parts)
    t1 = []
    for tag, p, th in zip(tags, parts, theirs):
        _, _, h, w = p.shape
        t1.append(esum("rs_add_cores" + tag, [(p.reshape(2, 4 * h, w), "c"), (th.reshape(4 * h, w), None)],
                       4 * h, w, p.dtype).reshape(4, h, w))
    landed = exchange_chips("rs_exchange", t1)
    red = []
    for tag, p, t, got in zip(tags, parts, t1, landed):
        _, _, h, w = p.shape
        red.append(esum("rs_add_chips" + tag, [(t, "j"), (got, 0), (got, 1), (got, 2)], h, w, f32, out_slots=2))
    return gather_cores("rs_gather", red)


def _adamw_step(w_ref, g_ref, m_ref, v_ref, d_ref, nm_ref, nv_ref):
    gv = g_ref[...]
    nm = ADAM_B1 * m_ref[...] + (1.0 - ADAM_B1) * gv
    nv = ADAM_B2 * v_ref[...] + (1.0 - ADAM_B2) * jnp.square(gv)
    m_hat = nm / (1.0 - ADAM_B1 ** ADAM_STEP)
    v_hat = nv / (1.0 - ADAM_B2 ** ADAM_STEP)
    d_ref[...] = -ADAM_LR * (m_hat / (jnp.sqrt(v_hat) + ADAM_EPS) + ADAM_WD * w_ref[...])
    nm_ref[...] = nm
    nv_ref[...] = nv


def adamw(name, w, g, m, v, tr=None):
    rows, rest = w.shape[0], w.shape[1:]
    if tr is None:
        tr = ROW_BLOCK if rows % ROW_BLOCK == 0 else rows
    assert rows % tr == 0

    def body(*refs):
        _adamw_step(*refs)

    spec = pl.BlockSpec((tr,) + rest, lambda i: (i,) + (0,) * len(rest))
    return pl.pallas_call(
        body, name=name, grid=(rows // tr,), in_specs=[spec] * 4, out_specs=[spec] * 3,
        out_shape=[jax.ShapeDtypeStruct(w.shape, f32)] * 3, compiler_params=_cparams(1),
    )(w, g, m, v)


def adamw_many(name, ws, gs, ms, vs):
    n = len(ws)

    def body(*refs):
        ins, outs = refs[:4 * n], refs[4 * n:]
        for k in range(n):
            _adamw_step(ins[k], ins[n + k], ins[2 * n + k], ins[3 * n + k], outs[k], outs[n + k], outs[2 * n + k])

    vmem = pl.BlockSpec(memory_space=pltpu.VMEM)
    res = pl.pallas_call(
        body, name=name, in_specs=[vmem] * (4 * n), out_specs=[vmem] * (3 * n),
        out_shape=[jax.ShapeDtypeStruct(w.shape, f32) for w in ws] * 3,
        compiler_params=pltpu.CompilerParams(vmem_limit_bytes=VMEM_LIMIT),
    )(*ws, *gs, *ms, *vs)
    return res[:n], res[n:2 * n], res[2 * n:]


PACK_W = 1024


def _pack(parts, halves, row_mult):
    flat = jnp.concatenate([p.reshape(-1) for p in parts])
    per = halves * row_mult * PACK_W
    total = -(-flat.size // per) * per
    flat = jnp.pad(flat, (0, total - flat.size))
    return flat.reshape(halves, total // (halves * PACK_W), PACK_W)


def _unpack(flat, shapes):
    out, pos = [], 0
    for s in shapes:
        n = int(np.prod(s))
        out.append(flat[pos:pos + n].reshape(s))
        pos += n
    return out


SHARDED_BIG = ("w_in", "w_branch", "w_out", "s5_w_glu")
SHARDED_SMALL = ("m2_conv_w", "sc_conv_w", "merge_b")
SHARD_AXIS = {"w_in": 2, "w_branch": 3, "w_out": 1, "s5_w_glu": 1, "m2_conv_w": 2, "sc_conv_w": 2, "merge_b": 2}
REPLICATED = ("norm_w", "s5_lambda_re", "s5_lambda_im", "s5_b_re", "s5_b_im", "s5_c_re", "s5_c_im", "s5_d", "s5_log_step",
              "sgu_ln_w", "sgu_ln_b", "sgu_w", "sgu_b", "m2_conv_b", "m2_dt_bias", "m2_a_log", "m2_d", "m2_norm_w")
WEIGHTS = ("norm_w", "w_in", "s5_lambda_re", "s5_lambda_im", "s5_b_re", "s5_b_im", "s5_c_re", "s5_c_im", "s5_d",
           "s5_log_step", "s5_w_glu", "sgu_ln_w", "sgu_ln_b", "sgu_w", "sgu_b", "m2_conv_w", "m2_conv_b", "m2_dt_bias",
           "m2_a_log", "m2_d", "m2_norm_w", "sc_conv_w", "merge_b", "w_branch", "w_out", "final_norm_w")
N_LAYERS = 2


SHARD_W = IN_DIM // 4
SHARD_PAD = -(-SHARD_W // LANES) * LANES
REGROUP_W = 3 * LANES
REGROUP_PARTS = 4


def _kernel_pieces():
    out, pos = [], 0
    for s, n in _col_segments():
        while n:
            take = min(n, SHARD_W - s % SHARD_W)
            out.append((pos, s, take))
            pos, s, n = pos + take, s + take, n - take
    return out


def regroup_cols(name, src, steps, out_shape, out=None):
    n_src, rows, width = src.shape
    ow, win = REGROUP_W, REGROUP_W + LANES
    k_max = max(len(p) for _, _, p in steps)
    assert width % LANES == 0 and rows % REGROUP_PARTS == 0
    row_parts = [pl.ds(i * (rows // REGROUP_PARTS), rows // REGROUP_PARTS) for i in range(REGROUP_PARTS)]

    def body(*refs):
        src_ref, out_ref = refs[0], refs[-5]
        wbuf, obuf, sem_in, sem_out = refs[-4:]

        def fetch(q, slot):
            started = []
            for p, (s, col, lo, hi) in enumerate(steps[q][2]):
                w0 = col // LANES * LANES
                wlen = min(win, width - w0)
                cps = [pltpu.make_async_copy(src_ref.at[s, rp, pl.ds(w0, wlen)], wbuf.at[slot, p, rp, pl.ds(0, wlen)],
                                             sem_in.at[slot, p, i]) for i, rp in enumerate(row_parts)]
                for cp in cps:
                    cp.start()
                started.append((cps, wlen, col - w0 - lo, lo, hi))
            return started

        pend, writes = fetch(0, 0), [None, None]
        for q, (t, b, _) in enumerate(steps):
            slot = q % 2
            nxt = fetch(q + 1, 1 - slot) if q + 1 < len(steps) else []
            acc = [jnp.zeros((rows, LANES), f32) for _ in range(ow // LANES)]
            for p, (cps, wlen, shift, lo, hi) in enumerate(pend):
                for cp in cps:
                    cp.wait()
                for k in range(ow // LANES):
                    c_lo, c_hi = max(lo, k * LANES), min(hi, (k + 1) * LANES)
                    if c_lo >= c_hi:
                        continue
                    wb = (c_lo + shift) // LANES
                    alen = min(2 * LANES, wlen - wb * LANES)
                    a = wbuf[slot, p, :, wb * LANES:wb * LANES + alen]
                    r, c = _rows((alen, LANES)), _cols((alen, LANES))
                    sh = shift + (k - wb) * LANES
                    sel = (r == c + sh) & (c >= c_lo - k * LANES) & (c < c_hi - k * LANES)
                    acc[k] = acc[k] + jnp.dot(a, sel.astype(bf16), preferred_element_type=f32)
            for wr in writes[slot] or []:
                wr.wait()
            for k in range(ow // LANES):
                obuf[slot, :, k * LANES:(k + 1) * LANES] = acc[k].astype(bf16)
            writes[slot] = [pltpu.make_async_copy(obuf.at[slot, rp], out_ref.at[t, rp, pl.ds(b * ow, ow)], sem_out.at[slot, i])
                            for i, rp in enumerate(row_parts)]
            for wr in writes[slot]:
                wr.start()
            pend = nxt
        for wrs in writes:
            for wr in wrs or []:
                wr.wait()

    operands, io_alias = [src], {}
    if out is not None:
        operands.append(out)
        io_alias = {1: 0}
    return pl.pallas_call(
        body, name=name, in_specs=[ANY] * len(operands), out_specs=ANY,
        out_shape=jax.ShapeDtypeStruct(out_shape, bf16), input_output_aliases=io_alias,
        scratch_shapes=[pltpu.VMEM((2, k_max, rows, win), bf16), pltpu.VMEM((2, rows, ow), bf16),
                        pltpu.SemaphoreType.DMA((2, k_max, REGROUP_PARTS)), pltpu.SemaphoreType.DMA((2, REGROUP_PARTS))],
        compiler_params=pltpu.CompilerParams(vmem_limit_bytes=VMEM_LIMIT),
    )(*operands)


def _steps_to_kernel_cols(layer):
    steps = []
    for o in range(PW // REGROUP_W):
        pieces = []
        for pos, s, n in _kernel_pieces():
            lo, hi = max(pos, o * REGROUP_W), min(pos + n, (o + 1) * REGROUP_W)
            if lo < hi:
                ref_col = s + lo - pos
                pieces.append(((ref_col // SHARD_W) * N_LAYERS + layer, ref_col % SHARD_W, lo - o * REGROUP_W, hi - o * REGROUP_W))
        steps.append((0, o, pieces))
    return steps


def _steps_to_shards(layer):
    steps = []
    for j in range(4):
        for b in range(SHARD_PAD // REGROUP_W):
            start, stop = j * SHARD_W + b * REGROUP_W, min(j * SHARD_W + (b + 1) * REGROUP_W, (j + 1) * SHARD_W)
            pieces = []
            for pos, s, n in _kernel_pieces():
                lo, hi = max(s, start), min(s + n, stop)
                if lo < hi:
                    pieces.append((0, pos + lo - s, lo - start, hi - start))
            steps.append((layer * 4 + j, b, pieces))
    return steps


def _gather_weights(w):
    w_in = jnp.pad(w["w_in"].astype(bf16), ((0, 0), (0, 0), (0, SHARD_PAD - SHARD_W)))
    arrs = [w_in, w["w_branch"].reshape(N_LAYERS, N_BRANCH * BW, -1).astype(bf16),
            w["w_out"].astype(bf16), w["s5_w_glu"].astype(bf16), w["m2_conv_w"], w["sc_conv_w"], w["merge_b"]]
    got = gather_chips("ag_weights", arrs)
    cols = lambda t: jnp.transpose(t, (1, 0, 2)).reshape(t.shape[1], -1)
    layers = []
    for i in range(N_LAYERS):
        p = {k: w[k][i] for k in REPLICATED}
        p["w_in"] = regroup_cols(f"w_in_cols{i}", got[0].reshape(4 * N_LAYERS, D_MODEL, SHARD_PAD), _steps_to_kernel_cols(i),
                                 (1, D_MODEL, PW))[0]
        p["w_branch"] = cols(got[1][:, i])
        p["w_out"] = got[2][:, i].reshape(D_MODEL, D_MODEL)
        p["s5_w_glu"] = got[3][:, i].reshape(BW, BW)
        p["m2_conv_w"], p["sc_conv_w"], p["merge_b"] = cols(got[4][:, i]), cols(got[5][:, i]), cols(got[6][:, i])
        layers.append(p)
    return layers


def _reduce_grads(grads, d_final, w):
    to_chips = lambda t: jnp.transpose(t.reshape(t.shape[0], 4, -1), (1, 0, 2))
    stack = lambda f: jnp.stack([f(g) for g in grads])
    dw_in = None
    for i, g in enumerate(grads):
        dw_in = regroup_cols(f"w_in_shards{i}", g["w_in"][None], _steps_to_shards(i), (4 * N_LAYERS, D_MODEL, SHARD_PAD), out=dw_in)
    parts = [dw_in.reshape(N_LAYERS, 4, D_MODEL, SHARD_PAD),
             stack(lambda g: to_chips(g["w_branch"].reshape(N_BRANCH * BW, D_MODEL)).astype(bf16)),
             stack(lambda g: g["w_out"].reshape(4, D_MODEL // 4, D_MODEL).astype(bf16)),
             stack(lambda g: g["s5_w_glu"].reshape(4, BW // 4, BW).astype(bf16))]
    rep = jnp.concatenate([stack(lambda g: g[k]).reshape(-1) for k in REPLICATED] + [d_final.reshape(-1)])
    quarter = -(-rep.size // (4 * 2 * SUBL * PACK_W)) * (2 * SUBL * PACK_W)
    rep = jnp.pad(rep, (0, 4 * quarter - rep.size))
    small = []
    for j in range(4):
        sharded = [stack(lambda g: to_chips(g[k])[j]) for k in SHARDED_SMALL]
        small.append(_pack(sharded + [rep[j * quarter:(j + 1) * quarter]], 2, SUBL))
    parts.append(jnp.stack(small, axis=1))
    red = reduce_to_shards(parts)
    out = {"w_in": red[0][:, :, :SHARD_W], "w_branch": red[1].reshape(w["w_branch"].shape), "w_out": red[2], "s5_w_glu": red[3]}
    small_flat = red[4].reshape(-1)
    n_small = sum(int(np.prod(w[k].shape)) for k in SHARDED_SMALL)
    out.update(zip(SHARDED_SMALL, _unpack(small_flat, [w[k].shape for k in SHARDED_SMALL])))
    mine = small_flat[n_small:n_small + quarter].reshape(2, quarter // (2 * PACK_W), PACK_W)
    rep_all = gather_chips("ag_small_grads", [mine])[0].reshape(-1)
    names = REPLICATED + ("final_norm_w",)
    out.update(zip(names, _unpack(rep_all, [w[k].shape for k in names])))
    return out


def _update(w, g, m, v):
    d, nm, nv = {}, {}, {}
    flat2 = lambda a: a.reshape(-1, a.shape[-1])
    cols_major = lambda a: jnp.transpose(a, (2, 0, 1))
    res = adamw("adamw_w_in", *[cols_major(t["w_in"]) for t in (w, g, m, v)], tr=SHARD_W // 14)
    d["w_in"], nm["w_in"], nv["w_in"] = (jnp.transpose(r, (1, 2, 0)) for r in res)
    for k in SHARDED_BIG[1:]:
        res = adamw("adamw_" + k, *[flat2(t[k]) for t in (w, g, m, v)])
        d[k], nm[k], nv[k] = (r.reshape(w[k].shape) for r in res)
    for k in ("s5_b_re", "s5_b_im"):
        res = adamw("adamw_" + k, *[flat2(t[k]) for t in (w, g, m, v)])
        d[k], nm[k], nv[k] = (r.reshape(w[k].shape) for r in res)
    rest = [k for k in WEIGHTS if k not in SHARDED_BIG + ("s5_b_re", "s5_b_im")]
    two_d = lambda a: a.reshape(1, -1) if a.ndim == 1 else a
    res = adamw_many("adamw_rest", *[[two_d(t[k]) for k in rest] for t in (w, g, m, v)])
    for tgt, rs in zip((d, nm, nv), res):
        tgt.update({k: r.reshape(w[k].shape) for k, r in zip(rest, rs)})
    return d, nm, nv


def kernel(x, norm_w, w_in, s5_lambda_re, s5_lambda_im, s5_b_re, s5_b_im, s5_c_re, s5_c_im, s5_d, s5_log_step, s5_w_glu, sgu_ln_w, sgu_ln_b, sgu_w, sgu_b, m2_conv_w, m2_conv_b, m2_dt_bias, m2_a_log, m2_d, m2_norm_w, sc_conv_w, merge_b, w_branch, w_out, final_norm_w, loss_target, m_norm_w, m_w_in, m_s5_lambda_re, m_s5_lambda_im, m_s5_b_re, m_s5_b_im, m_s5_c_re, m_s5_c_im, m_s5_d, m_s5_log_step, m_s5_w_glu, m_sgu_ln_w, m_sgu_ln_b, m_sgu_w, m_sgu_b, m_m2_conv_w, m_m2_conv_b, m_m2_dt_bias, m_m2_a_log, m_m2_d, m_m2_norm_w, m_sc_conv_w, m_merge_b, m_w_branch, m_w_out, m_final_norm_w, v_norm_w, v_w_in, v_s5_lambda_re, v_s5_lambda_im, v_s5_b_re, v_s5_b_im, v_s5_c_re, v_s5_c_im, v_s5_d, v_s5_log_step, v_s5_w_glu, v_sgu_ln_w, v_sgu_ln_b, v_sgu_w, v_sgu_b, v_m2_conv_w, v_m2_conv_b, v_m2_dt_bias, v_m2_a_log, v_m2_d, v_m2_norm_w, v_sc_conv_w, v_merge_b, v_w_branch, v_w_out, v_final_norm_w):
    given = dict(locals())
    w = {k: given[k] for k in WEIGHTS}
    m = {k: given["m_" + k] for k in WEIGHTS}
    v = {k: given["v_" + k] for k in WEIGHTS}
    layers = _gather_weights(w)
    loss, dx, grads, d_final = local_step(x[0], loss_target[0], layers, final_norm_w)
    loss = lax.psum(loss, ("x", "y", "c"))
    g = _reduce_grads(grads, d_final, w)
    d, nm, nv = _update(w, g, m, v)
    return (loss, dx[None], *[g[k] for k in WEIGHTS], *[d[k] for k in WEIGHTS],
            *[nm[k] for k in WEIGHTS], *[nv[k] for k in WEIGHTS])
```

```python
import functools
from typing import Any, Callable, NamedTuple

import numpy as np
import jax
import jax.numpy as jnp
from jax import lax
from jax.experimental import pallas as pl
from jax.experimental.pallas import tpu as pltpu

f32 = jnp.float32
bf16 = jnp.bfloat16

D_MODEL = 1024
BW = 512
N_BRANCH = 4
EPS = 1e-6
S5_GROUPS, S5_P, S5_N = 32, 16, 64
S5_NS = S5_GROUPS * S5_N
CHUNK = 128
M2_HEADS, M2_HEAD_DIM, M2_GROUPS, M2_STATE = 8, 64, 2, 128
IN_DIM = 10248
PW = 10368
PW_MAIN = 10240
LANES = 128
VMEM_LIMIT = 60 * 1024 * 1024

ADAM_LR, ADAM_B1, ADAM_B2, ADAM_EPS, ADAM_WD, ADAM_STEP = 0.001, 0.9, 0.999, 1e-08, 0.01, 10

C_MERGE = 0
C_SC = 4096
C_SGU = 6144
C_S5G = 7680
C_S5U = 8192
C_M2Z = 8704
C_XBC = 9216
C_DT = 10240


def _col_segments():
    segs = [(6152, 4096)]
    for j in range(4):
        segs += [(4104 + 128 * j, 128), (4616 + 128 * j, 128), (5128 + 128 * j, 128), (5640 + 128 * j, 128)]
    segs += [(1024, 1536), (512, 512), (0, 512), (2560, 512), (3072, 1024), (4096, 8)]
    return segs


NN = ((1,), (0,))
NT = ((1,), (1,))
TN = ((0,), (0,))


def _bd(a, b, dims):
    return lax.dot_general(a.astype(bf16), b.astype(bf16), (dims, ((), ())), preferred_element_type=f32)


def _hd(a, b, dims):
    return lax.dot_general(a, b, (dims, ((), ())), precision=lax.Precision.HIGHEST, preferred_element_type=f32)


def _make_dots(raw):
    @jax.custom_vjp
    def nn(a, b):
        return raw(a, b, NN)
    nn.defvjp(lambda a, b: (raw(a, b, NN), (a, b)), lambda r, g: (raw(g, r[1], NT), raw(r[0], g, TN)))

    @jax.custom_vjp
    def nt(a, b):
        return raw(a, b, NT)
    nt.defvjp(lambda a, b: (raw(a, b, NT), (a, b)), lambda r, g: (raw(g, r[1], NN), raw(g, r[0], TN)))

    @jax.custom_vjp
    def tn(a, b):
        return raw(a, b, TN)
    tn.defvjp(lambda a, b: (raw(a, b, TN), (a, b)), lambda r, g: (raw(r[1], g, NT), raw(r[0], g, NN)))
    return nn, nt, tn


bdot, bdot_nt, bdot_tn = _make_dots(_bd)
hdot, hdot_nt, hdot_tn = _make_dots(_hd)


@jax.custom_vjp
def bdot_w(a, w, shadow):
    return _bd(a, w, NN)


bdot_w.defvjp(lambda a, w, s: (_bd(a, w, NN), (a, w)),
              lambda r, g: (_bd(g, r[1], NT), jnp.zeros_like(r[1]), _bd(r[0], g, TN)))


def _rows(shape):
    return lax.broadcasted_iota(jnp.int32, shape, 0)


def _cols(shape):
    return lax.broadcasted_iota(jnp.int32, shape, 1)


def _shift_down(x, s):
    return jnp.where(_rows(x.shape) < s, 0.0, pltpu.roll(x, s, 0))


def _shift_up(x, s):
    n = x.shape[0]
    return jnp.where(_rows(x.shape) >= n - s, 0.0, pltpu.roll(x, n - s, 0))


@functools.partial(jax.custom_vjp, nondiff_argnums=(1,))
def shift(x, s):
    return _shift_down(x, s) if s else x


shift.defvjp(lambda x, s: (shift(x, s), None), lambda s, _, g: (_shift_up(g, s) if s else g,))


def _row_of(w, k):
    return jnp.sum(jnp.where(_rows(w.shape) == k, w, 0.0), axis=0, keepdims=True)


def _lane_mask(width, lo, hi):
    c = _cols((1, width))
    return ((c >= lo) & (c < hi)).astype(f32)


def _expand(rows, width, per):
    return (_cols((rows, width)) // per == _rows((rows, width))).astype(f32)


class A(NamedTuple):
    arr: Any
    block: tuple
    imap: Callable
    shadow: bool = False


class O(NamedTuple):
    shape: tuple
    dtype: Any
    block: tuple
    imap: Callable
    alias: Any = None


class R(NamedTuple):
    arg: int
    out: int
    off: Any = None
    acc: bool = False


def _cparams(n_grid):
    return pltpu.CompilerParams(dimension_semantics=("arbitrary",) * n_grid, vmem_limit_bytes=VMEM_LIMIT)


def _ispec(block, imap, n, rev):
    if rev:
        return pl.BlockSpec(block, lambda i: imap(n - 1 - i))
    return pl.BlockSpec(block, imap)


def _load(ref, a):
    v = ref[...]
    if a.shadow:
        return (v, jnp.zeros(v.shape, f32))
    return v.astype(f32)


def _save_spec(shape, n, rev):
    nd = len(shape)
    return _ispec((None,) + tuple(shape), lambda i: (i,) + (0,) * nd, n, rev)


def block_fwd(name, f, n, args, outs, carries=()):
    n_in, n_out, n_c = len(args), len(outs), len(carries)

    def body(*refs):
        ins, out_r = refs[:n_in], refs[n_in:n_in + n_out]
        saves, cs = refs[n_in + n_out:n_in + n_out + n_c], refs[n_in + n_out + n_c:]
        if n_c:
            @pl.when(pl.program_id(0) == 0)
            def _():
                for c in cs:
                    c[...] = jnp.zeros(c.shape, f32)
        vals = [_load(r, a) for r, a in zip(ins, args)]
        cv = [c[...] for c in cs]
        for s, v in zip(saves, cv):
            s[...] = v
        res = f(*vals, *cv)
        for r, v in zip(out_r, res[:n_out]):
            r[...] = v.astype(r.dtype)
        for c, v in zip(cs, res[n_out:]):
            c[...] = v

    out_shape = [jax.ShapeDtypeStruct(o.shape, o.dtype) for o in outs]
    out_specs = [pl.BlockSpec(o.block, o.imap) for o in outs]
    for shp in carries:
        out_shape.append(jax.ShapeDtypeStruct((n,) + tuple(shp), f32))
        out_specs.append(_save_spec(shp, n, False))
    return pl.pallas_call(
        body, name=name, grid=(n,),
        in_specs=[pl.BlockSpec(a.block, a.imap) for a in args],
        out_specs=out_specs, out_shape=out_shape,
        scratch_shapes=[pltpu.VMEM(tuple(shp), f32) for shp in carries],
        compiler_params=_cparams(1),
    )(*[a.arr for a in args])


def block_bwd(name, f, n, args, cots, gouts, routes, saved=(), rev=False):
    n_in, n_cot, n_c, n_go = len(args), len(cots), len(saved), len(gouts)
    diff = []
    for r in routes:
        if r.arg not in diff:
            diff.append(r.arg)
    aliases = [(k, o.alias) for k, o in enumerate(gouts) if o.alias is not None]

    def body(*refs):
        ins = refs[:n_in]
        cot_r = refs[n_in:n_in + n_cot]
        sav_r = refs[n_in + n_cot:n_in + n_cot + n_c]
        base = n_in + n_cot + n_c + len(aliases)
        go_r = refs[base:base + n_go]
        dcs = refs[base + n_go:]
        step = pl.program_id(0)
        if n_c:
            @pl.when(step == 0)
            def _():
                for d in dcs:
                    d[...] = jnp.zeros(d.shape, f32)
        vals = [_load(r, a) for r, a in zip(ins, args)]
        cv = [s[...] for s in sav_r]
        nd = len(diff)

        def g(*dv):
            full = list(vals)
            for idx, v in zip(diff, dv[:nd]):
                full[idx] = (vals[idx][0], v) if args[idx].shadow else v
            return tuple(f(*full, *dv[nd:]))

        primals = [vals[i][1] if args[i].shadow else vals[i] for i in diff] + cv
        _, vjp_fn = jax.vjp(g, *primals)
        ct = tuple([r[...].astype(f32) for r in cot_r] + [d[...] for d in dcs])
        grads = vjp_fn(ct)
        for r in routes:
            gr = grads[diff.index(r.arg)]
            ref = go_r[r.out]
            if r.acc:
                @pl.when(step == 0)
                def _(ref=ref, gr=gr):
                    ref[...] = gr.astype(ref.dtype)

                @pl.when(step > 0)
                def _(ref=ref, gr=gr):
                    ref[...] += gr.astype(ref.dtype)
            elif r.off is None:
                ref[...] = gr.astype(ref.dtype)
            else:
                ref[:, r.off:r.off + gr.shape[1]] = gr.astype(ref.dtype)
        for d, gr in zip(dcs, grads[nd:]):
            d[...] = gr

    in_specs = [_ispec(a.block, a.imap, n, rev) for a in list(args) + list(cots)]
    in_specs += [_save_spec(s.shape[1:], n, rev) for s in saved]
    in_specs += [pl.BlockSpec(memory_space=pl.ANY) for _ in aliases]
    operands = [a.arr for a in list(args) + list(cots)] + list(saved) + [arr for _, arr in aliases]
    io_alias = {n_in + n_cot + n_c + j: k for j, (k, _) in enumerate(aliases)}
    return pl.pallas_call(
        body, name=name, grid=(n,),
        in_specs=in_specs,
        out_specs=[_ispec(o.block, o.imap, n, rev) for o in gouts],
        out_shape=[jax.ShapeDtypeStruct(o.shape, o.dtype) for o in gouts],
        scratch_shapes=[pltpu.VMEM(tuple(s.shape[1:]), f32) for s in saved],
        input_output_aliases=io_alias,
        compiler_params=_cparams(1),
    )(*operands)


class Op(NamedTuple):
    arr: Any
    row: int = 0
    col: int = 0


def mm(name, mode, pairs, m, n, tm, tn, out_dtype=f32, add=None, out=None, out_col=0, out_width=None):
    tm, tn = min(tm, m), min(tn, n)
    assert m % tm == 0 and n % tn == 0
    in_specs, operands = [], []
    for a, b, k, _ in pairs:
        if mode == TN:
            assert a.row % k == 0 and a.col % tm == 0
            in_specs.append(pl.BlockSpec((k, tm), lambda j, i, a=a, k=k: (a.row // k, i + a.col // tm)))
        else:
            assert a.col % k == 0 and a.row % tm == 0
            in_specs.append(pl.BlockSpec((tm, k), lambda j, i, a=a, k=k: (i + a.row // tm, a.col // k)))
        if mode == NT:
            assert b.col % k == 0 and b.row % tn == 0
            in_specs.append(pl.BlockSpec((tn, k), lambda j, i, b=b, k=k: (j + b.row // tn, b.col // k)))
        else:
            assert b.row % k == 0 and b.col % tn == 0
            in_specs.append(pl.BlockSpec((k, tn), lambda j, i, b=b, k=k: (b.row // k, j + b.col // tn)))
        operands += [a.arr, b.arr]
    n_p = len(pairs)
    if add is not None:
        assert add.col % tn == 0
        in_specs.append(pl.BlockSpec((tm, tn), lambda j, i: (i, j + add.col // tn)))
        operands.append(add.arr)
    io_alias = {}
    if out is not None:
        assert out_col % tn == 0
        in_specs.append(pl.BlockSpec(memory_space=pl.ANY))
        operands.append(out)
        io_alias = {len(operands) - 1: 0}
        out_shape = jax.ShapeDtypeStruct(out.shape, out.dtype)
    else:
        out_shape = jax.ShapeDtypeStruct((m, out_width or n), out_dtype)
    signs = [p[3] for p in pairs]

    def body(*refs):
        o = refs[-1]
        acc = None
        for p in range(n_p):
            t = _bd(refs[2 * p][...], refs[2 * p + 1][...], mode)
            t = t if signs[p] > 0 else -t
            acc = t if acc is None else acc + t
        if add is not None:
            acc = acc + refs[2 * n_p][...].astype(f32)
        o[...] = acc.astype(o.dtype)

    return pl.pallas_call(
        body, name=name, grid=(n // tn, m // tm),
        in_specs=in_specs,
        out_specs=pl.BlockSpec((tm, tn), lambda j, i: (i, j + out_col // tn)),
        out_shape=out_shape, input_output_aliases=io_alias,
        compiler_params=_cparams(2),
    )(*operands)


SCAN_LANES = 512
SUBL = 8


def _cmul(p, q):
    return (p[0] * q[0] - p[1] * q[1], p[0] * q[1] + p[1] * q[0])


def _powers(a):
    a2 = _cmul(a, a)
    a4 = _cmul(a2, a2)
    a6 = _cmul(a4, a2)
    return [a, a2, _cmul(a2, a), a4, _cmul(a4, a), a6, _cmul(a6, a), _cmul(a4, a4)]


def _table(pw, order, w):
    row = _rows((SUBL, w))
    re = sum(jnp.where(row == t, pw[k][0], 0.0) for t, k in enumerate(order))
    im = sum(jnp.where(row == t, pw[k][1], 0.0) for t, k in enumerate(order))
    return re, im


def s5_scan_fwd(bu_re, bu_im, a_re, a_im):
    seq, ns = bu_re.shape
    w, nb = SCAN_LANES, ns // SCAN_LANES

    def body(b_re, b_im, ar, ai, s_re, s_im):
        a = (ar[...], ai[...])
        pw = _powers(a)
        tab = _table(pw, list(range(SUBL)), w)
        row = _rows((SUBL, w))

        def step(i, carry):
            t0 = pl.multiple_of(i * SUBL, SUBL)
            x = (b_re[pl.ds(t0, SUBL), :], b_im[pl.ds(t0, SUBL), :])
            for d, k in ((1, 0), (2, 1), (4, 3)):
                sh = (jnp.where(row < d, 0.0, pltpu.roll(x[0], d, 0)), jnp.where(row < d, 0.0, pltpu.roll(x[1], d, 0)))
                t = _cmul(pw[k], sh)
                x = (x[0] + t[0], x[1] + t[1])
            t = _cmul(tab, carry)
            x = (x[0] + t[0], x[1] + t[1])
            s_re[pl.ds(t0, SUBL), :] = x[0]
            s_im[pl.ds(t0, SUBL), :] = x[1]
            return (x[0][SUBL - 1:, :], x[1][SUBL - 1:, :])

        z = jnp.zeros((1, w), f32)
        lax.fori_loop(0, seq // SUBL, step, (z, z), unroll=2)

    strip = pl.BlockSpec((seq, w), lambda j: (0, j))
    lane = pl.BlockSpec((1, w), lambda j: (0, j))
    return pl.pallas_call(
        body, name="s5_scan_fwd", grid=(nb,),
        in_specs=[strip, strip, lane, lane],
        out_specs=[strip, strip],
        out_shape=[jax.ShapeDtypeStruct((seq, ns), f32)] * 2,
        compiler_params=_cparams(1),
    )(bu_re, bu_im, a_re, a_im)


def s5_scan_bwd(ds_re, ds_im, s_re, s_im, a_re, a_im):
    seq, ns = ds_re.shape
    w, nb = SCAN_LANES, ns // SCAN_LANES
    nblk = seq // SUBL

    def body(g_re, g_im, sr, si, ar, ai, l_re, l_im, da_re, da_im):
        a = (ar[...], -ai[...])
        pw = _powers(a)
        tab = _table(pw, [SUBL - 1 - t for t in range(SUBL)], w)
        row = _rows((SUBL, w))

        def step(kk, carry):
            c_re, c_im, acc_re, acc_im = carry
            i = nblk - 1 - kk
            t0 = pl.multiple_of(i * SUBL, SUBL)
            x = (g_re[pl.ds(t0, SUBL), :], g_im[pl.ds(t0, SUBL), :])
            for d, k in ((1, 0), (2, 1), (4, 3)):
                sh = (jnp.where(row >= SUBL - d, 0.0, pltpu.roll(x[0], SUBL - d, 0)),
                      jnp.where(row >= SUBL - d, 0.0, pltpu.roll(x[1], SUBL - d, 0)))
                t = _cmul(pw[k], sh)
                x = (x[0] + t[0], x[1] + t[1])
            t = _cmul(tab, (c_re, c_im))
            x = (x[0] + t[0], x[1] + t[1])
            l_re[pl.ds(t0, SUBL), :] = x[0]
            l_im[pl.ds(t0, SUBL), :] = x[1]
            tp = jnp.maximum(t0 - 1, 0)
            live = (i > 0).astype(f32)
            p_re = sr[pl.ds(tp, 1), :] * live
            p_im = si[pl.ds(tp, 1), :] * live
            sp_re = jnp.where(row == 0, p_re, pltpu.roll(sr[pl.ds(t0, SUBL), :], 1, 0))
            sp_im = jnp.where(row == 0, p_im, pltpu.roll(si[pl.ds(t0, SUBL), :], 1, 0))
            acc_re = acc_re + x[0] * sp_re + x[1] * sp_im
            acc_im = acc_im + x[1] * sp_re - x[0] * sp_im
            return (x[0][:1, :], x[1][:1, :], acc_re, acc_im)

        z1 = jnp.zeros((1, w), f32)
        z8 = jnp.zeros((SUBL, w), f32)
        _, _, acc_re, acc_im = lax.fori_loop(0, nblk, step, (z1, z1, z8, z8), unroll=2)
        da_re[...] = jnp.sum(acc_re, axis=0, keepdims=True)
        da_im[...] = jnp.sum(acc_im, axis=0, keepdims=True)

    strip = pl.BlockSpec((seq, w), lambda j: (0, j))
    lane = pl.BlockSpec((1, w), lambda j: (0, j))
    return pl.pallas_call(
        body, name="s5_scan_bwd", grid=(nb,),
        in_specs=[strip, strip, strip, strip, lane, lane],
        out_specs=[strip, strip, lane, lane],
        out_shape=[jax.ShapeDtypeStruct((seq, ns), f32)] * 2 + [jax.ShapeDtypeStruct((1, ns), f32)] * 2,
        compiler_params=_cparams(1),
    )(ds_re, ds_im, s_re, s_im, a_re, a_im)


def _rms(x, w):
    return x * lax.rsqrt(jnp.mean(x * x, axis=-1, keepdims=True) + EPS) * w


def f_rms(x, w):
    return (_rms(x, w),)


def f_rms_res(x, w):
    return (_rms(x, w), x)


def f_s5_prep(lam_re, lam_im, log_step, b_re, b_im):
    e = _expand(log_step.shape[1], S5_NS, S5_N)
    step = hdot(jnp.exp(log_step), e)
    mag = jnp.exp(lam_re * step)
    ab_re, ab_im = mag * jnp.cos(lam_im * step), mag * jnp.sin(lam_im * step)
    den = lam_re * lam_re + lam_im * lam_im
    nr = ab_re - 1.0
    coef_re = (nr * lam_re + ab_im * lam_im) / den
    coef_im = (ab_im * lam_re - nr * lam_im) / den
    bb_re, bb_im = coef_re * b_re - coef_im * b_im, coef_re * b_im + coef_im * b_re
    sel = (_rows((BW, S5_P)) % S5_P == _cols((BW, S5_P))).astype(f32)
    blk = _rows((BW, S5_NS)) // S5_P == _cols((BW, S5_NS)) // S5_N
    rows_bd = lambda t: jnp.where(blk, hdot(sel, t), 0.0)
    return (ab_re, ab_im, rows_bd(bb_re), rows_bd(bb_im))


def f_s5_c(c_re, c_im):
    sel_t = (_cols((S5_P, BW)) % S5_P == _rows((S5_P, BW))).astype(f32)
    blk_t = _rows((S5_NS, BW)) // S5_N == _cols((S5_NS, BW)) // S5_P
    cols_bd = lambda t: jnp.where(blk_t, hdot_tn(t, sel_t), 0.0)
    return (cols_bd(c_re), cols_bd(c_im))


def f_s5_act(y_lin, u, gate, d, w_glu):
    y = jax.nn.gelu(y_lin + d * u)
    y = y * jax.nn.sigmoid(bdot_w(y, *w_glu))
    return (y * jax.nn.silu(gate),)


def f_sgu(u, v, gate, ln_w, ln_b, *rest):
    w_s, b_pad = rest[:8], rest[8]
    t = u.shape[0]
    u32, v32 = jax.nn.gelu(u), jax.nn.gelu(v)
    mu = jnp.mean(v32, axis=-1, keepdims=True)
    var = jnp.mean(jnp.square(v32 - mu), axis=-1, keepdims=True)
    vn = (v32 - mu) * lax.rsqrt(var + EPS) * ln_w + ln_b
    tri = _rows((t, t)) >= _cols((t, t))
    s = hdot_tn(b_pad, _expand(LANES, BW, BW // 8))
    for h in range(8):
        s = s + bdot(jnp.where(tri, w_s[h], 0.0), vn) * _lane_mask(BW, 64 * h, 64 * h + 64)
    return (u32 * s * jax.nn.silu(gate),)


def f_m2_conv(x, w, b):
    return (sum(_row_of(w, k) * shift(x, 3 - k) for k in range(4)) + b,)


def f_sc(bg, cg, h, gate, w):
    z = cg * h
    conv = sum(_row_of(w, k) * shift(z, 2 - k) for k in range(3))
    return (bg * conv * jax.nn.silu(gate),)


def f_m2(z, xc, b0, b1, c0, c1, dt_raw, dt_bias, a_log, d_par, norm_w, st):
    q = z.shape[0]
    x = jax.nn.silu(xc)
    bm, cm = (jax.nn.silu(b0), jax.nn.silu(b1)), (jax.nn.silu(c0), jax.nn.silu(c1))
    dt = jax.nn.softplus(dt_raw + dt_bias)
    da = dt * (-jnp.exp(a_log))
    tri = _rows((q, q)) >= _cols((q, q))
    acs = hdot(tri.astype(f32), da)
    e = _expand(LANES, BW, M2_HEAD_DIM)
    dt_f, acs_f = hdot(dt, e), hdot(acs, e)
    last = _rows((q, BW)) == q - 1
    alast_f = jnp.sum(jnp.where(last, acs_f, 0.0), axis=0, keepdims=True)
    xdt = x * dt_f
    xdec = xdt * jnp.exp(alast_f - acs_f)
    acs_t = acs.T
    st_new = st * jnp.exp(alast_f)
    y_diag, y_off = 0.0, 0.0
    for g in range(M2_GROUPS):
        gm = _lane_mask(BW, 256 * g, 256 * g + 256)
        cb = bdot_nt(cm[g], bm[g])
        st_new = st_new + bdot_tn(bm[g], xdec * gm)
        y_off = y_off + bdot(cm[g], st) * gm
        for hh in range(M2_HEADS // M2_GROUPS):
            h = g * (M2_HEADS // M2_GROUPS) + hh
            col = jnp.sum(jnp.where(_cols((q, LANES)) == h, acs, 0.0), axis=1, keepdims=True)
            row = jnp.sum(jnp.where(_rows((LANES, q)) == h, acs_t, 0.0), axis=0, keepdims=True)
            decay = jnp.exp(jnp.where(tri, col - row, -1e30))
            y_diag = y_diag + bdot(cb * decay, xdt) * _lane_mask(BW, 64 * h, 64 * h + 64)
    d_f = sum(jnp.sum(jnp.where(_cols((1, LANES)) == h, d_par, 0.0), axis=1, keepdims=True)
              * _lane_mask(BW, 64 * h, 64 * h + 64) for h in range(M2_HEADS))
    y = y_diag + y_off * jnp.exp(acs_f) + d_f * x
    y = y * jax.nn.silu(z)
    return (_rms(y, norm_w), st_new)


def f_branch_mix(*v):
    nb = len(v) // 4
    ys, lg, wb, mb = v[0:nb], v[nb:2 * nb], v[2 * nb:3 * nb], v[3 * nb:]
    return (sum(jax.nn.sigmoid(lg[k] + mb[k]) * bdot_w(ys[k], *wb[k]) for k in range(nb)),)


def _param(arr, shadow=False):
    return A(arr, tuple(arr.shape), lambda i, nd=arr.ndim: (0,) * nd, shadow)


def _tb(arr, t, width, colblk):
    return A(arr, (t, width), lambda i: (i, colblk))


def _strip(arr, seq, colblk0, stride=1):
    return A(arr, (seq, LANES), lambda j: (0, colblk0 + stride * j))


def _s5_prep_args(p):
    lam_re = p["s5_lambda_re"].reshape(1, S5_NS)
    lam_im = p["s5_lambda_im"].reshape(1, S5_NS)
    log_step = jnp.pad(p["s5_log_step"].reshape(1, S5_GROUPS), ((0, 0), (0, LANES - S5_GROUPS)))
    b_lanes = lambda b: jnp.transpose(b, (2, 0, 1)).reshape(S5_P, S5_NS)
    return [_param(v) for v in (lam_re, lam_im, log_step, b_lanes(p["s5_b_re"]), b_lanes(p["s5_b_im"]))]


def _s5_c_args(p):
    c_lanes = lambda c: jnp.transpose(c, (1, 0, 2)).reshape(S5_P, S5_NS)
    return [_param(c_lanes(p["s5_c_re"])), _param(c_lanes(p["s5_c_im"]))]


def layer_fwd(x, p):
    seq = x.shape[0]
    nt = seq // CHUNK
    t2 = 256
    sv = {}
    t4 = min(512, seq)
    hb = block_fwd("rms_in", f_rms, seq // t4, [_tb(x, t4, D_MODEL, 0), _param(p["norm_w"].reshape(1, D_MODEL))],
                   [O((seq, D_MODEL), bf16, (t4, D_MODEL), lambda i: (i, 0))])[0]
    proj = mm("mm_in", NN, [(Op(hb), Op(p["w_in"]), D_MODEL, 1)], seq, PW, 2048, 1152)
    whole = lambda shape, dt: O(shape, dt, shape, lambda i: (0, 0))
    ab_re, ab_im, bb_re, bb_im = block_fwd("s5_prep", f_s5_prep, 1, _s5_prep_args(p),
                                           [whole((1, S5_NS), f32)] * 2 + [whole((BW, S5_NS), bf16)] * 2)
    cc_re, cc_im = block_fwd("s5_c", f_s5_c, 1, _s5_c_args(p), [whole((S5_NS, BW), bf16)] * 2)
    bu_re = mm("mm_bu_re", NN, [(Op(proj, 0, C_S5U), Op(bb_re), BW, 1)], seq, S5_NS, 1024, 1024)
    bu_im = mm("mm_bu_im", NN, [(Op(proj, 0, C_S5U), Op(bb_im), BW, 1)], seq, S5_NS, 1024, 1024)
    s_re, s_im = s5_scan_fwd(bu_re, bu_im, ab_re, ab_im)
    y_lin = mm("mm_s5y", NN, [(Op(s_re), Op(cc_re), S5_NS, 1), (Op(s_im), Op(cc_im), S5_NS, -1)], seq, BW, 512, BW)
    s5_act_args = [_tb(y_lin, t4, BW, 0), _tb(proj, t4, BW, C_S5U // BW), _tb(proj, t4, BW, C_S5G // BW),
                   _param(p["s5_d"].reshape(1, BW)), _param(p["s5_w_glu"], True)]
    out_bw =O((seq, BW), f32, (CHUNK, BW), lambda i: (i, 0))
    y_a = block_fwd("s5_act", f_s5_act, seq // t4, s5_act_args, [O((seq, BW), f32, (t4, BW), lambda i: (i, 0))])[0]
    y_b = block_fwd("sgu", f_sgu, nt, _sgu_args(proj, p), [out_bw])[0]
    xc = block_fwd("m2_conv", f_m2_conv, 8, _m2_conv_args(proj, p, seq),
                   [O((seq, 2 * BW), f32, (seq, LANES), lambda j: (0, j))])[0]
    y_c, st_saved = block_fwd("m2_ssd", f_m2, nt, _m2_args(proj, xc, p), [out_bw], carries=[(M2_STATE, BW)])
    y_d = block_fwd("sc", f_sc, 4, _sc_args(proj, p, seq), [O((seq, BW), f32, (seq, LANES), lambda j: (0, j))])[0]
    ys = [y_a, y_b, y_c, y_d]
    merged = block_fwd("branch_mix", f_branch_mix, seq // t2, _mix_args(ys, proj, p, t2),
                       [O((seq, D_MODEL), bf16, (t2, D_MODEL), lambda i: (i, 0))])[0]
    x_new = mm("mm_out", NN, [(Op(merged), Op(p["w_out"]), D_MODEL, 1)], seq, D_MODEL, 1024, D_MODEL, add=Op(x))
    sv.update(x=x, hb=hb, proj=proj, ab=(ab_re, ab_im), bb=(bb_re, bb_im), cc=(cc_re, cc_im), s=(s_re, s_im), y_lin=y_lin,
              xc=xc, st=st_saved, ys=ys, merged=merged)
    return x_new, sv


def _sgu_args(proj, p):
    c0 = C_SGU // BW
    args = [_tb(proj, CHUNK, BW, c0), _tb(proj, CHUNK, BW, c0 + 1), _tb(proj, CHUNK, BW, c0 + 2),
            _param(p["sgu_ln_w"].reshape(1, BW)), _param(p["sgu_ln_b"].reshape(1, BW))]
    args += [A(p["sgu_w"], (None, CHUNK, CHUNK), lambda i, h=h: (h, 0, 0)) for h in range(8)]
    args.append(_param(jnp.pad(p["sgu_b"], ((0, LANES - 8), (0, 0)))))
    return args


def _m2_conv_args(proj, p, seq):
    return [_strip(proj, seq, C_XBC // LANES), A(p["m2_conv_w"], (4, LANES), lambda j: (0, j)),
            A(p["m2_conv_b"].reshape(1, 2 * BW), (1, LANES), lambda j: (0, j))]


def _pad_lanes(v):
    return jnp.pad(v.reshape(1, -1), ((0, 0), (0, LANES - v.size)))


def _m2_args(proj, xc, p):
    args = [_tb(proj, CHUNK, BW, C_M2Z // BW), _tb(xc, CHUNK, BW, 0)]
    args += [_tb(xc, CHUNK, LANES, 4 + k) for k in range(4)]
    args.append(_tb(proj, CHUNK, LANES, C_DT // LANES))
    args += [_param(_pad_lanes(p["m2_dt_bias"])), _param(_pad_lanes(p["m2_a_log"])), _param(_pad_lanes(p["m2_d"])),
             _param(p["m2_norm_w"].reshape(1, BW))]
    return args


def _sc_args(proj, p, seq):
    c0 = C_SC // LANES
    return [_strip(proj, seq, c0 + k, 4) for k in range(4)] + [A(p["sc_conv_w"], (3, LANES), lambda j: (0, j))]


def _mix_args(ys, proj, p, t, ks=range(N_BRANCH)):
    args = [_tb(ys[k], t, BW, 0) for k in ks]
    args += [_tb(proj, t, D_MODEL, k) for k in ks]
    args += [A(p["w_branch"], (BW, D_MODEL), lambda i, k=k: (k, 0), True) for k in ks]
    mb = p["merge_b"].reshape(N_BRANCH, 1, D_MODEL)
    args += [A(mb, (None, 1, D_MODEL), lambda i, k=k: (k, 0, 0)) for k in ks]
    return args


def layer_bwd(d_out, p, sv):
    seq = d_out.shape[0]
    nt = seq // CHUNK
    t4 = min(512, seq)
    proj, ys = sv["proj"], sv["ys"]
    g = {}
    acc = lambda shape: O(tuple(shape), f32, tuple(shape), lambda i, nd=len(shape): (0,) * nd)
    d_merged = mm("mm_out_dx", NT, [(Op(d_out), Op(p["w_out"]), D_MODEL, 1)], seq, D_MODEL, 1024, D_MODEL, out_dtype=f32)
    g["w_out"] = mm("mm_out_dw", TN, [(Op(sv["merged"]), Op(d_out), seq, 1)], D_MODEL, D_MODEL, 512, D_MODEL, out_dtype=bf16)
    dys, dwb, dmb, dproj, tb = [], [], [], None, min(256, seq)
    for half in range(2):
        gouts = [O((seq, BW), f32, (tb, BW), lambda i: (i, 0)) for _ in range(2)]
        gouts.append(O((seq, PW_MAIN), bf16, (tb, 2 * D_MODEL), lambda i, half=half: (i, half), alias=dproj))
        gouts += [acc((BW, D_MODEL)) for _ in range(2)] + [acc((1, D_MODEL)) for _ in range(2)]
        routes = [R(k, k) for k in range(2)] + [R(2 + k, 2, k * D_MODEL) for k in range(2)]
        routes += [R(4 + k, 3 + k, acc=True) for k in range(2)] + [R(6 + k, 5 + k, acc=True) for k in range(2)]
        res = block_bwd(f"branch_mix_bwd{half}", f_branch_mix, seq // tb, _mix_args(ys, proj, p, tb, (2 * half, 2 * half + 1)),
                        [_tb(d_merged, tb, D_MODEL, 0)], gouts, routes)
        dys, dproj, dwb, dmb = dys + list(res[:2]), res[2], dwb + list(res[3:5]), dmb + list(res[5:7])
    g["w_branch"] = jnp.stack(dwb)
    g["merge_b"] = jnp.concatenate(dmb, axis=0)
    res = block_bwd("sc_bwd", f_sc, 4, _sc_args(proj, p, seq), [_strip(dys[3], seq, 0)],
                    [O((seq, PW_MAIN), bf16, (seq, 4 * LANES), lambda j: (0, C_SC // (4 * LANES) + j), alias=dproj),
                     O((3, BW), f32, (3, LANES), lambda j: (0, j))],
                    [R(k, 0, k * LANES) for k in range(4)] + [R(4, 1)])
    dproj, g["sc_conv_w"] = res
    res = block_bwd("m2_ssd_bwd", f_m2, nt, _m2_args(proj, sv["xc"], p), [_tb(dys[2], CHUNK, BW, 0)],
                    [O((seq, PW_MAIN), bf16, (CHUNK, BW), lambda i: (i, C_M2Z // BW), alias=dproj),
                     O((seq, 2 * BW), f32, (CHUNK, 2 * BW), lambda i: (i, 0)),
                     O((seq, LANES), bf16, (CHUNK, LANES), lambda i: (i, 0)),
                     acc((1, LANES)), acc((1, LANES)), acc((1, LANES)), acc((1, BW))],
                    [R(0, 0), R(1, 1, 0)] + [R(2 + k, 1, BW + k * LANES) for k in range(4)] + [R(6, 2)]
                    + [R(7, 3, acc=True), R(8, 4, acc=True), R(9, 5, acc=True), R(10, 6, acc=True)],
                    saved=[sv["st"]], rev=True)
    dproj, dxc, d_dt = res[0], res[1], res[2]
    g["m2_dt_bias"], g["m2_a_log"], g["m2_d"] = (r[0, :M2_HEADS] for r in res[3:6])
    g["m2_norm_w"] = res[6].reshape(BW)
    res = block_bwd("m2_conv_bwd", f_m2_conv, 8, _m2_conv_args(proj, p, seq), [_strip(dxc, seq, 0)],
                    [O((seq, PW_MAIN), bf16, (seq, LANES), lambda j: (0, C_XBC // LANES + j), alias=dproj),
                     O((4, 2 * BW), f32, (4, LANES), lambda j: (0, j)), O((1, 2 * BW), f32, (1, LANES), lambda j: (0, j))],
                    [R(0, 0), R(1, 1), R(2, 2)])
    dproj, g["m2_conv_w"], cb = res
    g["m2_conv_b"] = cb.reshape(2 * BW)
    res = block_bwd("sgu_bwd", f_sgu, nt, _sgu_args(proj, p), [_tb(dys[1], CHUNK, BW, 0)],
                    [O((seq, PW_MAIN), bf16, (CHUNK, 3 * BW), lambda i: (i, C_SGU // (3 * BW)), alias=dproj),
                     acc((1, BW)), acc((1, BW))] + [acc((CHUNK, CHUNK)) for _ in range(8)] + [acc((LANES, CHUNK))],
                    [R(0, 0, 0), R(1, 0, BW), R(2, 0, 2 * BW), R(3, 1, acc=True), R(4, 2, acc=True)]
                    + [R(5 + h, 3 + h, acc=True) for h in range(8)] + [R(13, 11, acc=True)])
    dproj = res[0]
    g["sgu_ln_w"], g["sgu_ln_b"] = res[1].reshape(BW), res[2].reshape(BW)
    g["sgu_w"] = jnp.stack(res[3:11])
    g["sgu_b"] = res[11][:8]
    y_lin, (s_re, s_im), (ab_re, ab_im) = sv["y_lin"], sv["s"], sv["ab"]
    s5_act_args = [_tb(y_lin, t4, BW, 0), _tb(proj, t4, BW, C_S5U // BW), _tb(proj, t4, BW, C_S5G // BW),
                   _param(p["s5_d"].reshape(1, BW)), _param(p["s5_w_glu"], True)]
    res = block_bwd("s5_act_bwd", f_s5_act, seq // t4, s5_act_args, [_tb(dys[0], t4, BW, 0)],
                    [O((seq, BW), bf16, (t4, BW), lambda i: (i, 0)), O((seq, BW), f32, (t4, BW), lambda i: (i, 0)),
                     O((seq, PW_MAIN), bf16, (t4, BW), lambda i: (i, C_S5G // BW), alias=dproj),
                     acc((1, BW)), acc((BW, BW))],
                    [R(0, 0), R(1, 1), R(2, 2), R(3, 3, acc=True), R(4, 4, acc=True)])
    dy_lin, du1, dproj = res[0], res[1], res[2]
    g["s5_d"] = res[3].reshape(S5_GROUPS, S5_P)
    g["s5_w_glu"] = res[4]
    (bb_re, bb_im), (cc_re, cc_im) = sv["bb"], sv["cc"]
    ds_re = mm("mm_s5y_dre", NT, [(Op(dy_lin), Op(cc_re), BW, 1)], seq, S5_NS, 1024, 1024)
    ds_im = mm("mm_s5y_dim", NT, [(Op(dy_lin), Op(cc_im), BW, -1)], seq, S5_NS, 1024, 1024)
    dc_re = mm("mm_s5y_dcre", TN, [(Op(s_re), Op(dy_lin), seq, 1)], S5_NS, BW, 512, BW)
    dc_im = mm("mm_s5y_dcim", TN, [(Op(s_im), Op(dy_lin), seq, -1)], S5_NS, BW, 512, BW)
    l_re, l_im, da_re, da_im = s5_scan_bwd(ds_re, ds_im, s_re, s_im, ab_re, ab_im)
    dproj = mm("mm_bu_dx", NT, [(Op(l_re), Op(bb_re), S5_NS, 1), (Op(l_im), Op(bb_im), S5_NS, 1)],
               seq, BW, 512, BW, add=Op(du1), out=dproj, out_col=C_S5U)
    dbb = [mm(f"mm_bu_dw{n}", TN, [(Op(proj, 0, C_S5U), Op(l), seq, 1)], BW, S5_NS, 512, 1024)
           for n, l in (("re", l_re), ("im", l_im))]
    gouts = [acc((1, S5_NS)), acc((1, S5_NS)), acc((1, LANES))] + [acc((S5_P, S5_NS))] * 2
    res = block_bwd("s5_prep_bwd", f_s5_prep, 1, _s5_prep_args(p),
                    [_param(v) for v in (da_re, da_im, dbb[0], dbb[1])], gouts, [R(k, k, acc=True) for k in range(5)])
    g["s5_lambda_re"], g["s5_lambda_im"] = res[0].reshape(S5_GROUPS, S5_N), res[1].reshape(S5_GROUPS, S5_N)
    g["s5_log_step"] = res[2][0, :S5_GROUPS]
    b_natural = lambda b: jnp.transpose(b.reshape(S5_P, S5_GROUPS, S5_N), (1, 2, 0))
    c_natural = lambda c: jnp.transpose(c.reshape(S5_P, S5_GROUPS, S5_N), (1, 0, 2))
    g["s5_b_re"], g["s5_b_im"] = b_natural(res[3]), b_natural(res[4])
    res = block_bwd("s5_c_bwd", f_s5_c, 1, _s5_c_args(p), [_param(dc_re), _param(dc_im)],
                    [acc((S5_P, S5_NS))] * 2, [R(0, 0, acc=True), R(1, 1, acc=True)])
    g["s5_c_re"], g["s5_c_im"] = c_natural(res[0]), c_natural(res[1])
    hb, w_in = sv["hb"], p["w_in"]
    dh = mm("mm_in_dx", NT, [(Op(dproj), Op(w_in), PW_MAIN, 1), (Op(d_dt), Op(w_in, 0, PW_MAIN), LANES, 1)],
            seq, D_MODEL, 256, 1024)
    dw = mm("mm_in_dw", TN, [(Op(hb), Op(dproj), seq, 1)], D_MODEL, PW_MAIN, 1024, 2048, out_dtype=bf16, out_width=PW)
    g["w_in"] = mm("mm_in_dwdt", TN, [(Op(hb), Op(d_dt), seq, 1)], D_MODEL, LANES, 256, LANES, out=dw, out_col=C_DT)
    dx, dnw = block_bwd("rms_in_bwd", f_rms_res, seq // t4, [_tb(sv["x"], t4, D_MODEL, 0), _param(p["norm_w"].reshape(1, D_MODEL))],
                        [_tb(dh, t4, D_MODEL, 0), _tb(d_out, t4, D_MODEL, 0)],
                        [O((seq, D_MODEL), f32, (t4, D_MODEL), lambda i: (i, 0)), acc((1, D_MODEL))],
                        [R(0, 0), R(1, 1, acc=True)])
    g["norm_w"] = dnw.reshape(D_MODEL)
    return dx, g


def loss_head(x, w, target):
    seq = x.shape[0]
    t = min(512, seq)

    def body(x_ref, w_ref, t_ref, loss_ref, dx_ref, dw_ref):
        step = pl.program_id(0)

        def f(xv, wv):
            err = _rms(xv, wv) - t_ref[...]
            return 0.5 * jnp.sum(jnp.mean(err * err, axis=-1, keepdims=True), axis=0, keepdims=True)

        val, vjp_fn = jax.vjp(f, x_ref[...], w_ref[...])
        dx, dw = vjp_fn(jnp.ones((1, 1), f32))
        dx_ref[...] = dx

        @pl.when(step == 0)
        def _():
            loss_ref[...] = jnp.broadcast_to(val, loss_ref.shape)
            dw_ref[...] = dw

        @pl.when(step > 0)
        def _():
            loss_ref[...] += jnp.broadcast_to(val, loss_ref.shape)
            dw_ref[...] += dw

    blk = pl.BlockSpec((t, D_MODEL), lambda i: (i, 0))
    row = pl.BlockSpec((1, D_MODEL), lambda i: (0, 0))
    return pl.pallas_call(
        body, name="loss_head", grid=(seq // t,),
        in_specs=[blk, row, blk],
        out_specs=[pl.BlockSpec((1, LANES), lambda i: (0, 0)), blk, row],
        out_shape=[jax.ShapeDtypeStruct((1, LANES), f32), jax.ShapeDtypeStruct((seq, D_MODEL), f32),
                   jax.ShapeDtypeStruct((1, D_MODEL), f32)],
        compiler_params=_cparams(1),
    )(x, w.reshape(1, D_MODEL), target)


LAYER_KEYS = ("norm_w", "w_in", "s5_lambda_re", "s5_lambda_im", "s5_b_re", "s5_b_im", "s5_c_re", "s5_c_im", "s5_d",
              "s5_log_step", "s5_w_glu", "sgu_ln_w", "sgu_ln_b", "sgu_w", "sgu_b", "m2_conv_w", "m2_conv_b",
              "m2_dt_bias", "m2_a_log", "m2_d", "m2_norm_w", "sc_conv_w", "merge_b", "w_branch", "w_out")


def local_step(x, target, layers, final_norm_w):
    saved = []
    for p in layers:
        x, sv = layer_fwd(x, p)
        saved.append(sv)
    loss, dx, dfw = loss_head(x, final_norm_w, target)
    grads = []
    for p, sv in zip(reversed(layers), reversed(saved)):
        dx, g = layer_bwd(dx, p, sv)
        grads.append(g)
    return loss[0, 0], dx, grads[::-1], dfw.reshape(D_MODEL)


MESH = pl.DeviceIdType.MESH
ANY = pl.BlockSpec(memory_space=pl.ANY)


def _me():
    return lax.axis_index("x"), lax.axis_index("y"), lax.axis_index("c")


def _other_chips(x, y):
    return [(1 - x, y), (x, 1 - y), (1 - x, 1 - y)]


def _rcopy(src, dst, send, recv, dev):
    return pltpu.make_async_remote_copy(src_ref=src, dst_ref=dst, send_sem=send, recv_sem=recv,
                                        device_id=dev, device_id_type=MESH)


def _route_cut(rows, dtype):
    tile = 2 * SUBL * (4 // jnp.dtype(dtype).itemsize)
    return rows // 2 if rows % tile == 0 else rows


def _comm_call(name, body, arrs, out_shape, n_remote, aliases=None):
    n = len(arrs)
    return pl.pallas_call(
        body, name=name, in_specs=[ANY] * n, out_specs=[ANY] * len(out_shape), out_shape=out_shape,
        scratch_shapes=[pltpu.SemaphoreType.DMA((n, n_remote)), pltpu.SemaphoreType.DMA((n, n_remote))],
        input_output_aliases=aliases or {},
        compiler_params=pltpu.CompilerParams(has_side_effects=True),
    )(*arrs)


def gather_chips(name, arrs):
    n = len(arrs)
    cut = [_route_cut(a.shape[1], a.dtype) for a in arrs]

    def body(*refs):
        ins, outs = refs[:n], refs[n:2 * n]
        send, recv = refs[2 * n:]
        x, y, c = _me()
        jme, jx, jy, jd = 2 * x + y, 2 * (1 - x) + y, 2 * x + 1 - y, 2 * (1 - x) + 1 - y
        to_x, to_y, sib = (1 - x, y, c), (x, 1 - y, c), (x, y, 1 - c)

        def part(ref, a, hi):
            return ref.at[pl.ds(cut[a], ref.shape[0] - cut[a])] if hi else ref.at[pl.ds(0, cut[a])]

        def cp(a, k, ref, dev):
            return _rcopy(ref, ref, send.at[a, k], recv.at[a, k], dev)

        split = [a for a in range(n) if cut[a] < arrs[a].shape[1]]
        sent = [_rcopy(ins[a].at[c], outs[a].at[jme, c], send.at[a, k], recv.at[a, k], dev)
                for a in range(n) for k, dev in ((0, to_x), (1, to_y))]
        for s in sent:
            s.start()
        for a in range(n):
            blk = outs[a].at[jx, c]
            cp(a, 0, blk, to_x).wait_recv()
            sent += [cp(a, 2, part(blk, a, False), to_y), cp(a, 4, blk, sib)]
            sent[-2].start()
            sent[-1].start()
        for a in range(n):
            blk = outs[a].at[jy, c]
            cp(a, 1, blk, to_y).wait_recv()
            sent.append(cp(a, 5, blk, sib))
            sent[-1].start()
            if a in split:
                sent.append(cp(a, 3, part(blk, a, True), to_x))
                sent[-1].start()
        for a in range(n):
            lo = part(outs[a].at[jd, c], a, False)
            cp(a, 2, lo, to_y).wait_recv()
            sent.append(cp(a, 6, lo, sib))
            sent[-1].start()
        for a in split:
            hi = part(outs[a].at[jd, c], a, True)
            cp(a, 3, hi, to_x).wait_recv()
            sent.append(cp(a, 7, hi, sib))
            sent[-1].start()
        for a in range(n):
            cp(a, 4, outs[a].at[jx, 1 - c], sib).wait_recv()
            cp(a, 5, outs[a].at[jy, 1 - c], sib).wait_recv()
            cp(a, 6, part(outs[a].at[jd, 1 - c], a, False), sib).wait_recv()
        for a in split:
            cp(a, 7, part(outs[a].at[jd, 1 - c], a, True), sib).wait_recv()
        for s in sent:
            s.wait_send()

    out_shape = [jax.ShapeDtypeStruct((4,) + a.shape, a.dtype) for a in arrs]
    got = _comm_call(name, body, arrs, out_shape, 8)
    jme = 2 * lax.axis_index("x") + lax.axis_index("y")
    return [lax.dynamic_update_index_in_dim(g, a, jme, 0) for g, a in zip(got, arrs)]


def swap_halves(name, arrs):
    n = len(arrs)

    def body(*refs):
        ins, outs = refs[:n], refs[n:2 * n]
        send, recv = refs[2 * n:]
        x, y, c = _me()
        remote = [_rcopy(ins[a].at[1 - c], outs[a], send.at[a, 0], recv.at[a, 0], (x, y, 1 - c)) for a in range(n)]
        for cp in remote:
            cp.start()
        for cp in remote:
            cp.wait()

    return _comm_call(name, body, arrs, [jax.ShapeDtypeStruct(a.shape[1:], a.dtype) for a in arrs], 1)


def exchange_chips(name, arrs):
    n = len(arrs)

    def body(*refs):
        ins, outs = refs[:n], refs[n:2 * n]
        send, recv = refs[2 * n:]
        x, y, c = _me()
        remote = [_rcopy(ins[a].at[2 * cx + cy], outs[a].at[k], send.at[a, k], recv.at[a, k], (cx, cy, c))
                  for a in range(n) for k, (cx, cy) in enumerate(_other_chips(x, y))]
        for cp in remote:
            cp.start()
        for cp in remote:
            cp.wait()

    return _comm_call(name, body, arrs, [jax.ShapeDtypeStruct((3,) + a.shape[1:], a.dtype) for a in arrs], 3)


def gather_cores(name, arrs):
    n = len(arrs)

    def body(*refs):
        bufs = refs[n:2 * n]
        send, recv = refs[2 * n:]
        x, y, c = _me()
        remote = [_rcopy(bufs[a].at[c], bufs[a].at[c], send.at[a, 0], recv.at[a, 0], (x, y, 1 - c)) for a in range(n)]
        for cp in remote:
            cp.start()
        for a in range(n):
            _rcopy(bufs[a].at[1 - c], bufs[a].at[1 - c], send.at[a, 0], recv.at[a, 0], (x, y, 1 - c)).wait_recv()
        for cp in remote:
            cp.wait_send()

    return _comm_call(name, body, arrs, [jax.ShapeDtypeStruct(a.shape, a.dtype) for a in arrs], 1,
                      aliases={a: a for a in range(n)})


ROW_BLOCK = 512


def esum(name, terms, rows, width, out_dtype, out_slots=None):
    tr = next((t for t in range(min(rows, ROW_BLOCK), 0, -SUBL) if rows % t == 0 and t % SUBL == 0), rows)
    where =jnp.stack([lax.axis_index("c"), 2 * lax.axis_index("x") + lax.axis_index("y")]).astype(jnp.int32)
    pick = {"c": 0, "j": 1}

    def body(s_ref, *refs):
        acc = refs[0][...].astype(f32)
        for r in refs[1:-1]:
            acc = acc + r[...].astype(f32)
        refs[-1][...] = acc.astype(out_dtype)

    specs = []
    for arr, lead in terms:
        if lead is None:
            specs.append(pl.BlockSpec((tr, width), lambda i, s: (i, 0)))
        elif isinstance(lead, str):
            specs.append(pl.BlockSpec((None, tr, width), lambda i, s, lead=lead: (s[pick[lead]], i, 0)))
        else:
            specs.append(pl.BlockSpec((None, tr, width), lambda i, s, lead=lead: (lead, i, 0)))
    if out_slots is None:
        out_spec = pl.BlockSpec((tr, width), lambda i, s: (i, 0))
        out_shape = jax.ShapeDtypeStruct((rows, width), out_dtype)
    else:
        out_spec = pl.BlockSpec((None, tr, width), lambda i, s: (s[0], i, 0))
        out_shape = jax.ShapeDtypeStruct((out_slots, rows, width), out_dtype)
    return pl.pallas_call(
        body, name=name,
        grid_spec=pltpu.PrefetchScalarGridSpec(num_scalar_prefetch=1, grid=(rows // tr,), in_specs=specs, out_specs=out_spec),
        out_shape=out_shape, compiler_params=_cparams(1),
    )(where, *[t[0] for t in terms])


def reduce_to_shards(parts):
    tags = [str(k) for k in range(len(parts))]
    theirs = swap_halves("rs_swap", parts)
    t1 = []
    for tag, p, th in zip(tags, parts, theirs):
        _, _, h, w = p.shape
        t1.append(esum("rs_add_cores" + tag, [(p.reshape(2, 4 * h, w), "c"), (th.reshape(4 * h, w), None)],
                       4 * h, w, p.dtype).reshape(4, h, w))
    landed = exchange_chips("rs_exchange", t1)
    red = []
    for tag, p, t, got in zip(tags, parts, t1, landed):
        _, _, h, w = p.shape
        red.append(esum("rs_add_chips" + tag, [(t, "j"), (got, 0), (got, 1), (got, 2)], h, w, f32, out_slots=2))
    return gather_cores("rs_gather", red)


def _adamw_step(w_ref, g_ref, m_ref, v_ref, d_ref, nm_ref, nv_ref):
    gv = g_ref[...]
    nm = ADAM_B1 * m_ref[...] + (1.0 - ADAM_B1) * gv
    nv = ADAM_B2 * v_ref[...] + (1.0 - ADAM_B2) * jnp.square(gv)
    m_hat = nm / (1.0 - ADAM_B1 ** ADAM_STEP)
    v_hat = nv / (1.0 - ADAM_B2 ** ADAM_STEP)
    d_ref[...] = -ADAM_LR * (m_hat / (jnp.sqrt(v_hat) + ADAM_EPS) + ADAM_WD * w_ref[...])
    nm_ref[...] = nm
    nv_ref[...] = nv


def adamw(name, w, g, m, v, tr=None):
    rows, rest = w.shape[0], w.shape[1:]
    if tr is None:
        tr = ROW_BLOCK if rows % ROW_BLOCK == 0 else rows
    assert rows % tr == 0

    def body(*refs):
        _adamw_step(*refs)

    spec = pl.BlockSpec((tr,) + rest, lambda i: (i,) + (0,) * len(rest))
    return pl.pallas_call(
        body, name=name, grid=(rows // tr,), in_specs=[spec] * 4, out_specs=[spec] * 3,
        out_shape=[jax.ShapeDtypeStruct(w.shape, f32)] * 3, compiler_params=_cparams(1),
    )(w, g, m, v)


def adamw_many(name, ws, gs, ms, vs):
    n = len(ws)

    def body(*refs):
        ins, outs = refs[:4 * n], refs[4 * n:]
        for k in range(n):
            _adamw_step(ins[k], ins[n + k], ins[2 * n + k], ins[3 * n + k], outs[k], outs[n + k], outs[2 * n + k])

    vmem = pl.BlockSpec(memory_space=pltpu.VMEM)
    res = pl.pallas_call(
        body, name=name, in_specs=[vmem] * (4 * n), out_specs=[vmem] * (3 * n),
        out_shape=[jax.ShapeDtypeStruct(w.shape, f32) for w in ws] * 3,
        compiler_params=pltpu.CompilerParams(vmem_limit_bytes=VMEM_LIMIT),
    )(*ws, *gs, *ms, *vs)
    return res[:n], res[n:2 * n], res[2 * n:]


PACK_W = 1024


def _pack(parts, halves, row_mult):
    flat = jnp.concatenate([p.reshape(-1) for p in parts])
    per = halves * row_mult * PACK_W
    total = -(-flat.size // per) * per
    flat = jnp.pad(flat, (0, total - flat.size))
    return flat.reshape(halves, total // (halves * PACK_W), PACK_W)


def _unpack(flat, shapes):
    out, pos = [], 0
    for s in shapes:
        n = int(np.prod(s))
        out.append(flat[pos:pos + n].reshape(s))
        pos += n
    return out


SHARDED_BIG = ("w_in", "w_branch", "w_out", "s5_w_glu")
SHARDED_SMALL = ("m2_conv_w", "sc_conv_w", "merge_b")
SHARD_AXIS = {"w_in": 2, "w_branch": 3, "w_out": 1, "s5_w_glu": 1, "m2_conv_w": 2, "sc_conv_w": 2, "merge_b": 2}
REPLICATED = ("norm_w", "s5_lambda_re", "s5_lambda_im", "s5_b_re", "s5_b_im", "s5_c_re", "s5_c_im", "s5_d", "s5_log_step",
              "sgu_ln_w", "sgu_ln_b", "sgu_w", "sgu_b", "m2_conv_b", "m2_dt_bias", "m2_a_log", "m2_d", "m2_norm_w")
WEIGHTS = ("norm_w", "w_in", "s5_lambda_re", "s5_lambda_im", "s5_b_re", "s5_b_im", "s5_c_re", "s5_c_im", "s5_d",
           "s5_log_step", "s5_w_glu", "sgu_ln_w", "sgu_ln_b", "sgu_w", "sgu_b", "m2_conv_w", "m2_conv_b", "m2_dt_bias",
           "m2_a_log", "m2_d", "m2_norm_w", "sc_conv_w", "merge_b", "w_branch", "w_out", "final_norm_w")
N_LAYERS = 2


SHARD_W = IN_DIM // 4
SHARD_PAD = -(-SHARD_W // LANES) * LANES
REGROUP_W = 3 * LANES


def _kernel_pieces():
    out, pos = [], 0
    for s, n in _col_segments():
        while n:
            take = min(n, SHARD_W - s % SHARD_W)
            out.append((pos, s, take))
            pos, s, n = pos + take, s + take, n - take
    return out


def regroup_cols(name, src, steps, out_shape, out=None):
    n_src, rows, width = src.shape
    ow, win = REGROUP_W, REGROUP_W + LANES
    k_max = max(len(p) for _, _, p in steps)
    assert width % LANES == 0

    def body(*refs):
        src_ref, out_ref = refs[0], refs[-5]
        wbuf, obuf, sem_in, sem_out = refs[-4:]

        def fetch(q, slot):
            started = []
            for p, (s, col, lo, hi) in enumerate(steps[q][2]):
                w0 = col // LANES * LANES
                wlen = min(win, width - w0)
                cp = pltpu.make_async_copy(src_ref.at[s, :, pl.ds(w0, wlen)], wbuf.at[slot, p, :, pl.ds(0, wlen)], sem_in.at[slot, p])
                cp.start()
                started.append((cp, wlen, col - w0 - lo, lo, hi))
            return started

        pend, writes = fetch(0, 0), [None, None]
        for q, (t, b, _) in enumerate(steps):
            slot = q % 2
            nxt = fetch(q + 1, 1 - slot) if q + 1 < len(steps) else []
            acc = [jnp.zeros((rows, LANES), f32) for _ in range(ow // LANES)]
            for p, (cp, wlen, shift, lo, hi) in enumerate(pend):
                cp.wait()
                for k in range(ow // LANES):
                    c_lo, c_hi = max(lo, k * LANES), min(hi, (k + 1) * LANES)
                    if c_lo >= c_hi:
                        continue
                    wb = (c_lo + shift) // LANES
                    alen = min(2 * LANES, wlen - wb * LANES)
                    a = wbuf[slot, p, :, wb * LANES:wb * LANES + alen]
                    r, c = _rows((alen, LANES)), _cols((alen, LANES))
                    sh = shift + (k - wb) * LANES
                    sel = (r == c + sh) & (c >= c_lo - k * LANES) & (c < c_hi - k * LANES)
                    acc[k] = acc[k] + jnp.dot(a, sel.astype(bf16), preferred_element_type=f32)
            if writes[slot] is not None:
                writes[slot].wait()
            for k in range(ow // LANES):
                obuf[slot, :, k * LANES:(k + 1) * LANES] = acc[k].astype(bf16)
            writes[slot] = pltpu.make_async_copy(obuf.at[slot], out_ref.at[t, :, pl.ds(b * ow, ow)], sem_out.at[slot])
            writes[slot].start()
            pend = nxt
        for wr in writes:
            if wr is not None:
                wr.wait()

    operands, io_alias = [src], {}
    if out is not None:
        operands.append(out)
        io_alias = {1: 0}
    return pl.pallas_call(
        body, name=name, in_specs=[ANY] * len(operands), out_specs=ANY,
        out_shape=jax.ShapeDtypeStruct(out_shape, bf16), input_output_aliases=io_alias,
        scratch_shapes=[pltpu.VMEM((2, k_max, rows, win), bf16), pltpu.VMEM((2, rows, ow), bf16),
                        pltpu.SemaphoreType.DMA((2, k_max)), pltpu.SemaphoreType.DMA((2,))],
        compiler_params=pltpu.CompilerParams(vmem_limit_bytes=VMEM_LIMIT),
    )(*operands)


def _steps_to_kernel_cols(layer):
    steps = []
    for o in range(PW // REGROUP_W):
        pieces = []
        for pos, s, n in _kernel_pieces():
            lo, hi = max(pos, o * REGROUP_W), min(pos + n, (o + 1) * REGROUP_W)
            if lo < hi:
                ref_col = s + lo - pos
                pieces.append(((ref_col // SHARD_W) * N_LAYERS + layer, ref_col % SHARD_W, lo - o * REGROUP_W, hi - o * REGROUP_W))
        steps.append((0, o, pieces))
    return steps


def _steps_to_shards(layer):
    steps = []
    for j in range(4):
        for b in range(SHARD_PAD // REGROUP_W):
            start, stop = j * SHARD_W + b * REGROUP_W, min(j * SHARD_W + (b + 1) * REGROUP_W, (j + 1) * SHARD_W)
            pieces = []
            for pos, s, n in _kernel_pieces():
                lo, hi = max(s, start), min(s + n, stop)
                if lo < hi:
                    pieces.append((0, pos + lo - s, lo - start, hi - start))
            steps.append((layer * 4 + j, b, pieces))
    return steps


def _gather_weights(w):
    w_in = jnp.pad(w["w_in"].astype(bf16), ((0, 0), (0, 0), (0, SHARD_PAD - SHARD_W)))
    arrs = [w_in, w["w_branch"].reshape(N_LAYERS, N_BRANCH * BW, -1).astype(bf16),
            w["w_out"].astype(bf16), w["s5_w_glu"].astype(bf16), w["m2_conv_w"], w["sc_conv_w"], w["merge_b"]]
    got = gather_chips("ag_weights", arrs)
    cols = lambda t: jnp.transpose(t, (1, 0, 2)).reshape(t.shape[1], -1)
    layers = []
    for i in range(N_LAYERS):
        p = {k: w[k][i] for k in REPLICATED}
        p["w_in"] = regroup_cols(f"w_in_cols{i}", got[0].reshape(4 * N_LAYERS, D_MODEL, SHARD_PAD), _steps_to_kernel_cols(i),
                                 (1, D_MODEL, PW))[0]
        p["w_branch"] = cols(got[1][:, i])
        p["w_out"] = got[2][:, i].reshape(D_MODEL, D_MODEL)
        p["s5_w_glu"] = got[3][:, i].reshape(BW, BW)
        p["m2_conv_w"], p["sc_conv_w"], p["merge_b"] = cols(got[4][:, i]), cols(got[5][:, i]), cols(got[6][:, i])
        layers.append(p)
    return layers


def _reduce_grads(grads, d_final, w):
    to_chips = lambda t: jnp.transpose(t.reshape(t.shape[0], 4, -1), (1, 0, 2))
    stack = lambda f: jnp.stack([f(g) for g in grads])
    dw_in = None
    for i, g in enumerate(grads):
        dw_in = regroup_cols(f"w_in_shards{i}", g["w_in"][None], _steps_to_shards(i), (4 * N_LAYERS, D_MODEL, SHARD_PAD), out=dw_in)
    parts = [dw_in.reshape(N_LAYERS, 4, D_MODEL, SHARD_PAD),
             stack(lambda g: to_chips(g["w_branch"].reshape(N_BRANCH * BW, D_MODEL)).astype(bf16)),
             stack(lambda g: g["w_out"].reshape(4, D_MODEL // 4, D_MODEL).astype(bf16)),
             stack(lambda g: g["s5_w_glu"].reshape(4, BW // 4, BW).astype(bf16))]
    rep = jnp.concatenate([stack(lambda g: g[k]).reshape(-1) for k in REPLICATED] + [d_final.reshape(-1)])
    quarter = -(-rep.size // (4 * 2 * SUBL * PACK_W)) * (2 * SUBL * PACK_W)
    rep = jnp.pad(rep, (0, 4 * quarter - rep.size))
    small = []
    for j in range(4):
        sharded = [stack(lambda g: to_chips(g[k])[j]) for k in SHARDED_SMALL]
        small.append(_pack(sharded + [rep[j * quarter:(j + 1) * quarter]], 2, SUBL))
    parts.append(jnp.stack(small, axis=1))
    red = reduce_to_shards(parts)
    out = {"w_in": red[0][:, :, :SHARD_W], "w_branch": red[1].reshape(w["w_branch"].shape), "w_out": red[2], "s5_w_glu": red[3]}
    small_flat = red[4].reshape(-1)
    n_small = sum(int(np.prod(w[k].shape)) for k in SHARDED_SMALL)
    out.update(zip(SHARDED_SMALL, _unpack(small_flat, [w[k].shape for k in SHARDED_SMALL])))
    mine = small_flat[n_small:n_small + quarter].reshape(2, quarter // (2 * PACK_W), PACK_W)
    rep_all = gather_chips("ag_small_grads", [mine])[0].reshape(-1)
    names = REPLICATED + ("final_norm_w",)
    out.update(zip(names, _unpack(rep_all, [w[k].shape for k in names])))
    return out


def _update(w, g, m, v):
    d, nm, nv = {}, {}, {}
    flat2 = lambda a: a.reshape(-1, a.shape[-1])
    cols_major = lambda a: jnp.transpose(a, (2, 0, 1))
    res = adamw("adamw_w_in", *[cols_major(t["w_in"]) for t in (w, g, m, v)], tr=SHARD_W // 14)
    d["w_in"], nm["w_in"], nv["w_in"] = (jnp.transpose(r, (1, 2, 0)) for r in res)
    for k in SHARDED_BIG[1:]:
        res = adamw("adamw_" + k, *[flat2(t[k]) for t in (w, g, m, v)])
        d[k], nm[k], nv[k] = (r.reshape(w[k].shape) for r in res)
    for k in ("s5_b_re", "s5_b_im"):
        res = adamw("adamw_" + k, *[flat2(t[k]) for t in (w, g, m, v)])
        d[k], nm[k], nv[k] = (r.reshape(w[k].shape) for r in res)
    rest = [k for k in WEIGHTS if k not in SHARDED_BIG + ("s5_b_re", "s5_b_im")]
    two_d = lambda a: a.reshape(1, -1) if a.ndim == 1 else a
    res = adamw_many("adamw_rest", *[[two_d(t[k]) for k in rest] for t in (w, g, m, v)])
    for tgt, rs in zip((d, nm, nv), res):
        tgt.update({k: r.reshape(w[k].shape) for k, r in zip(rest, rs)})
    return d, nm, nv


def kernel(x, norm_w, w_in, s5_lambda_re, s5_lambda_im, s5_b_re, s5_b_im, s5_c_re, s5_c_im, s5_d, s5_log_step, s5_w_glu, sgu_ln_w, sgu_ln_b, sgu_w, sgu_b, m2_conv_w, m2_conv_b, m2_dt_bias, m2_a_log, m2_d, m2_norm_w, sc_conv_w, merge_b, w_branch, w_out, final_norm_w, loss_target, m_norm_w, m_w_in, m_s5_lambda_re, m_s5_lambda_im, m_s5_b_re, m_s5_b_im, m_s5_c_re, m_s5_c_im, m_s5_d, m_s5_log_step, m_s5_w_glu, m_sgu_ln_w, m_sgu_ln_b, m_sgu_w, m_sgu_b, m_m2_conv_w, m_m2_conv_b, m_m2_dt_bias, m_m2_a_log, m_m2_d, m_m2_norm_w, m_sc_conv_w, m_merge_b, m_w_branch, m_w_out, m_final_norm_w, v_norm_w, v_w_in, v_s5_lambda_re, v_s5_lambda_im, v_s5_b_re, v_s5_b_im, v_s5_c_re, v_s5_c_im, v_s5_d, v_s5_log_step, v_s5_w_glu, v_sgu_ln_w, v_sgu_ln_b, v_sgu_w, v_sgu_b, v_m2_conv_w, v_m2_conv_b, v_m2_dt_bias, v_m2_a_log, v_m2_d, v_m2_norm_w, v_sc_conv_w, v_merge_b, v_w_branch, v_w_out, v_final_norm_w):
    given = dict(locals())
    w = {k: given[k] for k in WEIGHTS}
    m = {k: given["m_" + k] for k in WEIGHTS}
    v = {k: given["v_" + k] for k in WEIGHTS}
    layers = _gather_weights(w)
    loss, dx, grads, d_final = local_step(x[0], loss_target[0], layers, final_norm_w)
    loss = lax.psum(loss, ("x", "y", "c"))
    g = _reduce_grads(grads, d_final, w)
    d, nm, nv = _update(w, g, m, v)
    return (loss, dx[None], *[g[k] for k in WEIGHTS], *[d[k] for k in WEIGHTS],
            *[nm[k] for k in WEIGHTS], *[nv[k] for k in WEIGHTS])
```

```python
import functools
from typing import Any, Callable, NamedTuple

import numpy as np
import jax
import jax.numpy as jnp
from jax import lax
from jax.experimental import pallas as pl
from jax.experimental.pallas import tpu as pltpu

f32 = jnp.float32
bf16 = jnp.bfloat16

D_MODEL = 1024
BW = 512
N_BRANCH = 4
EPS = 1e-6
S5_GROUPS, S5_P, S5_N = 32, 16, 64
S5_NS = S5_GROUPS * S5_N
CHUNK = 128
M2_HEADS, M2_HEAD_DIM, M2_GROUPS, M2_STATE = 8, 64, 2, 128
IN_DIM = 10248
PW = 10368
PW_MAIN = 10240
LANES = 128
VMEM_LIMIT = 60 * 1024 * 1024

ADAM_LR, ADAM_B1, ADAM_B2, ADAM_EPS, ADAM_WD, ADAM_STEP = 0.001, 0.9, 0.999, 1e-08, 0.01, 10

C_MERGE = 0
C_SC = 4096
C_SGU = 6144
C_S5G = 7680
C_S5U = 8192
C_M2Z = 8704
C_XBC = 9216
C_DT = 10240


def _col_segments():
    segs = [(6152, 4096)]
    for j in range(4):
        segs += [(4104 + 128 * j, 128), (4616 + 128 * j, 128), (5128 + 128 * j, 128), (5640 + 128 * j, 128)]
    segs += [(1024, 1536), (512, 512), (0, 512), (2560, 512), (3072, 1024), (4096, 8)]
    return segs


NN = ((1,), (0,))
NT = ((1,), (1,))
TN = ((0,), (0,))


def _bd(a, b, dims):
    return lax.dot_general(a.astype(bf16), b.astype(bf16), (dims, ((), ())), preferred_element_type=f32)


def _hd(a, b, dims):
    return lax.dot_general(a, b, (dims, ((), ())), precision=lax.Precision.HIGHEST, preferred_element_type=f32)


def _make_dots(raw):
    @jax.custom_vjp
    def nn(a, b):
        return raw(a, b, NN)
    nn.defvjp(lambda a, b: (raw(a, b, NN), (a, b)), lambda r, g: (raw(g, r[1], NT), raw(r[0], g, TN)))

    @jax.custom_vjp
    def nt(a, b):
        return raw(a, b, NT)
    nt.defvjp(lambda a, b: (raw(a, b, NT), (a, b)), lambda r, g: (raw(g, r[1], NN), raw(g, r[0], TN)))

    @jax.custom_vjp
    def tn(a, b):
        return raw(a, b, TN)
    tn.defvjp(lambda a, b: (raw(a, b, TN), (a, b)), lambda r, g: (raw(r[1], g, NT), raw(r[0], g, NN)))
    return nn, nt, tn


bdot, bdot_nt, bdot_tn = _make_dots(_bd)
hdot, hdot_nt, hdot_tn = _make_dots(_hd)


@jax.custom_vjp
def bdot_w(a, w, shadow):
    return _bd(a, w, NN)


bdot_w.defvjp(lambda a, w, s: (_bd(a, w, NN), (a, w)),
              lambda r, g: (_bd(g, r[1], NT), jnp.zeros_like(r[1]), _bd(r[0], g, TN)))


def _rows(shape):
    return lax.broadcasted_iota(jnp.int32, shape, 0)


def _cols(shape):
    return lax.broadcasted_iota(jnp.int32, shape, 1)


def _shift_down(x, s):
    return jnp.where(_rows(x.shape) < s, 0.0, pltpu.roll(x, s, 0))


def _shift_up(x, s):
    n = x.shape[0]
    return jnp.where(_rows(x.shape) >= n - s, 0.0, pltpu.roll(x, n - s, 0))


@functools.partial(jax.custom_vjp, nondiff_argnums=(1,))
def shift(x, s):
    return _shift_down(x, s) if s else x


shift.defvjp(lambda x, s: (shift(x, s), None), lambda s, _, g: (_shift_up(g, s) if s else g,))


def _row_of(w, k):
    return jnp.sum(jnp.where(_rows(w.shape) == k, w, 0.0), axis=0, keepdims=True)


def _lane_mask(width, lo, hi):
    c = _cols((1, width))
    return ((c >= lo) & (c < hi)).astype(f32)


def _expand(rows, width, per):
    return (_cols((rows, width)) // per == _rows((rows, width))).astype(f32)


class A(NamedTuple):
    arr: Any
    block: tuple
    imap: Callable
    shadow: bool = False


class O(NamedTuple):
    shape: tuple
    dtype: Any
    block: tuple
    imap: Callable
    alias: Any = None


class R(NamedTuple):
    arg: int
    out: int
    off: Any = None
    acc: bool = False


def _cparams(n_grid):
    return pltpu.CompilerParams(dimension_semantics=("arbitrary",) * n_grid, vmem_limit_bytes=VMEM_LIMIT)


def _ispec(block, imap, n, rev):
    if rev:
        return pl.BlockSpec(block, lambda i: imap(n - 1 - i))
    return pl.BlockSpec(block, imap)


def _load(ref, a):
    v = ref[...]
    if a.shadow:
        return (v, jnp.zeros(v.shape, f32))
    return v.astype(f32)


def _save_spec(shape, n, rev):
    nd = len(shape)
    return _ispec((None,) + tuple(shape), lambda i: (i,) + (0,) * nd, n, rev)


def block_fwd(name, f, n, args, outs, carries=()):
    n_in, n_out, n_c = len(args), len(outs), len(carries)

    def body(*refs):
        ins, out_r = refs[:n_in], refs[n_in:n_in + n_out]
        saves, cs = refs[n_in + n_out:n_in + n_out + n_c], refs[n_in + n_out + n_c:]
        if n_c:
            @pl.when(pl.program_id(0) == 0)
            def _():
                for c in cs:
                    c[...] = jnp.zeros(c.shape, f32)
        vals = [_load(r, a) for r, a in zip(ins, args)]
        cv = [c[...] for c in cs]
        for s, v in zip(saves, cv):
            s[...] = v
        res = f(*vals, *cv)
        for r, v in zip(out_r, res[:n_out]):
            r[...] = v.astype(r.dtype)
        for c, v in zip(cs, res[n_out:]):
            c[...] = v

    out_shape = [jax.ShapeDtypeStruct(o.shape, o.dtype) for o in outs]
    out_specs = [pl.BlockSpec(o.block, o.imap) for o in outs]
    for shp in carries:
        out_shape.append(jax.ShapeDtypeStruct((n,) + tuple(shp), f32))
        out_specs.append(_save_spec(shp, n, False))
    return pl.pallas_call(
        body, name=name, grid=(n,),
        in_specs=[pl.BlockSpec(a.block, a.imap) for a in args],
        out_specs=out_specs, out_shape=out_shape,
        scratch_shapes=[pltpu.VMEM(tuple(shp), f32) for shp in carries],
        compiler_params=_cparams(1),
    )(*[a.arr for a in args])


def block_bwd(name, f, n, args, cots, gouts, routes, saved=(), rev=False):
    n_in, n_cot, n_c, n_go = len(args), len(cots), len(saved), len(gouts)
    diff = []
    for r in routes:
        if r.arg not in diff:
            diff.append(r.arg)
    aliases = [(k, o.alias) for k, o in enumerate(gouts) if o.alias is not None]

    def body(*refs):
        ins = refs[:n_in]
        cot_r = refs[n_in:n_in + n_cot]
        sav_r = refs[n_in + n_cot:n_in + n_cot + n_c]
        base = n_in + n_cot + n_c + len(aliases)
        go_r = refs[base:base + n_go]
        dcs = refs[base + n_go:]
        step = pl.program_id(0)
        if n_c:
            @pl.when(step == 0)
            def _():
                for d in dcs:
                    d[...] = jnp.zeros(d.shape, f32)
        vals = [_load(r, a) for r, a in zip(ins, args)]
        cv = [s[...] for s in sav_r]
        nd = len(diff)

        def g(*dv):
            full = list(vals)
            for idx, v in zip(diff, dv[:nd]):
                full[idx] = (vals[idx][0], v) if args[idx].shadow else v
            return tuple(f(*full, *dv[nd:]))

        primals = [vals[i][1] if args[i].shadow else vals[i] for i in diff] + cv
        _, vjp_fn = jax.vjp(g, *primals)
        ct = tuple([r[...].astype(f32) for r in cot_r] + [d[...] for d in dcs])
        grads = vjp_fn(ct)
        for r in routes:
            gr = grads[diff.index(r.arg)]
            ref = go_r[r.out]
            if r.acc:
                @pl.when(step == 0)
                def _(ref=ref, gr=gr):
                    ref[...] = gr.astype(ref.dtype)

                @pl.when(step > 0)
                def _(ref=ref, gr=gr):
                    ref[...] += gr.astype(ref.dtype)
            elif r.off is None:
                ref[...] = gr.astype(ref.dtype)
            else:
                ref[:, r.off:r.off + gr.shape[1]] = gr.astype(ref.dtype)
        for d, gr in zip(dcs, grads[nd:]):
            d[...] = gr

    in_specs = [_ispec(a.block, a.imap, n, rev) for a in list(args) + list(cots)]
    in_specs += [_save_spec(s.shape[1:], n, rev) for s in saved]
    in_specs += [pl.BlockSpec(memory_space=pl.ANY) for _ in aliases]
    operands = [a.arr for a in list(args) + list(cots)] + list(saved) + [arr for _, arr in aliases]
    io_alias = {n_in + n_cot + n_c + j: k for j, (k, _) in enumerate(aliases)}
    return pl.pallas_call(
        body, name=name, grid=(n,),
        in_specs=in_specs,
        out_specs=[_ispec(o.block, o.imap, n, rev) for o in gouts],
        out_shape=[jax.ShapeDtypeStruct(o.shape, o.dtype) for o in gouts],
        scratch_shapes=[pltpu.VMEM(tuple(s.shape[1:]), f32) for s in saved],
        input_output_aliases=io_alias,
        compiler_params=_cparams(1),
    )(*operands)


class Op(NamedTuple):
    arr: Any
    row: int = 0
    col: int = 0


def mm(name, mode, pairs, m, n, tm, tn, out_dtype=f32, add=None, out=None, out_col=0, out_width=None):
    tm, tn = min(tm, m), min(tn, n)
    assert m % tm == 0 and n % tn == 0
    in_specs, operands = [], []
    for a, b, k, _ in pairs:
        if mode == TN:
            assert a.row % k == 0 and a.col % tm == 0
            in_specs.append(pl.BlockSpec((k, tm), lambda j, i, a=a, k=k: (a.row // k, i + a.col // tm)))
        else:
            assert a.col % k == 0 and a.row % tm == 0
            in_specs.append(pl.BlockSpec((tm, k), lambda j, i, a=a, k=k: (i + a.row // tm, a.col // k)))
        if mode == NT:
            assert b.col % k == 0 and b.row % tn == 0
            in_specs.append(pl.BlockSpec((tn, k), lambda j, i, b=b, k=k: (j + b.row // tn, b.col // k)))
        else:
            assert b.row % k == 0 and b.col % tn == 0
            in_specs.append(pl.BlockSpec((k, tn), lambda j, i, b=b, k=k: (b.row // k, j + b.col // tn)))
        operands += [a.arr, b.arr]
    n_p = len(pairs)
    if add is not None:
        assert add.col % tn == 0
        in_specs.append(pl.BlockSpec((tm, tn), lambda j, i: (i, j + add.col // tn)))
        operands.append(add.arr)
    io_alias = {}
    if out is not None:
        assert out_col % tn == 0
        in_specs.append(pl.BlockSpec(memory_space=pl.ANY))
        operands.append(out)
        io_alias = {len(operands) - 1: 0}
        out_shape = jax.ShapeDtypeStruct(out.shape, out.dtype)
    else:
        out_shape = jax.ShapeDtypeStruct((m, out_width or n), out_dtype)
    signs = [p[3] for p in pairs]

    def body(*refs):
        o = refs[-1]
        acc = None
        for p in range(n_p):
            t = _bd(refs[2 * p][...], refs[2 * p + 1][...], mode)
            t = t if signs[p] > 0 else -t
            acc = t if acc is None else acc + t
        if add is not None:
            acc = acc + refs[2 * n_p][...].astype(f32)
        o[...] = acc.astype(o.dtype)

    return pl.pallas_call(
        body, name=name, grid=(n // tn, m // tm),
        in_specs=in_specs,
        out_specs=pl.BlockSpec((tm, tn), lambda j, i: (i, j + out_col // tn)),
        out_shape=out_shape, input_output_aliases=io_alias,
        compiler_params=_cparams(2),
    )(*operands)


SCAN_LANES = 512
SUBL = 8


def _cmul(p, q):
    return (p[0] * q[0] - p[1] * q[1], p[0] * q[1] + p[1] * q[0])


def _powers(a):
    a2 = _cmul(a, a)
    a4 = _cmul(a2, a2)
    a6 = _cmul(a4, a2)
    return [a, a2, _cmul(a2, a), a4, _cmul(a4, a), a6, _cmul(a6, a), _cmul(a4, a4)]


def _table(pw, order, w):
    row = _rows((SUBL, w))
    re = sum(jnp.where(row == t, pw[k][0], 0.0) for t, k in enumerate(order))
    im = sum(jnp.where(row == t, pw[k][1], 0.0) for t, k in enumerate(order))
    return re, im


def s5_scan_fwd(bu_re, bu_im, a_re, a_im):
    seq, ns = bu_re.shape
    w, nb = SCAN_LANES, ns // SCAN_LANES

    def body(b_re, b_im, ar, ai, s_re, s_im):
        a = (ar[...], ai[...])
        pw = _powers(a)
        tab = _table(pw, list(range(SUBL)), w)
        row = _rows((SUBL, w))

        def step(i, carry):
            t0 = pl.multiple_of(i * SUBL, SUBL)
            x = (b_re[pl.ds(t0, SUBL), :], b_im[pl.ds(t0, SUBL), :])
            for d, k in ((1, 0), (2, 1), (4, 3)):
                sh = (jnp.where(row < d, 0.0, pltpu.roll(x[0], d, 0)), jnp.where(row < d, 0.0, pltpu.roll(x[1], d, 0)))
                t = _cmul(pw[k], sh)
                x = (x[0] + t[0], x[1] + t[1])
            t = _cmul(tab, carry)
            x = (x[0] + t[0], x[1] + t[1])
            s_re[pl.ds(t0, SUBL), :] = x[0]
            s_im[pl.ds(t0, SUBL), :] = x[1]
            return (x[0][SUBL - 1:, :], x[1][SUBL - 1:, :])

        z = jnp.zeros((1, w), f32)
        lax.fori_loop(0, seq // SUBL, step, (z, z), unroll=2)

    strip = pl.BlockSpec((seq, w), lambda j: (0, j))
    lane = pl.BlockSpec((1, w), lambda j: (0, j))
    return pl.pallas_call(
        body, name="s5_scan_fwd", grid=(nb,),
        in_specs=[strip, strip, lane, lane],
        out_specs=[strip, strip],
        out_shape=[jax.ShapeDtypeStruct((seq, ns), f32)] * 2,
        compiler_params=_cparams(1),
    )(bu_re, bu_im, a_re, a_im)


def s5_scan_bwd(ds_re, ds_im, s_re, s_im, a_re, a_im):
    seq, ns = ds_re.shape
    w, nb = SCAN_LANES, ns // SCAN_LANES
    nblk = seq // SUBL

    def body(g_re, g_im, sr, si, ar, ai, l_re, l_im, da_re, da_im):
        a = (ar[...], -ai[...])
        pw = _powers(a)
        tab = _table(pw, [SUBL - 1 - t for t in range(SUBL)], w)
        row = _rows((SUBL, w))

        def step(kk, carry):
            c_re, c_im, acc_re, acc_im = carry
            i = nblk - 1 - kk
            t0 = pl.multiple_of(i * SUBL, SUBL)
            x = (g_re[pl.ds(t0, SUBL), :], g_im[pl.ds(t0, SUBL), :])
            for d, k in ((1, 0), (2, 1), (4, 3)):
                sh = (jnp.where(row >= SUBL - d, 0.0, pltpu.roll(x[0], SUBL - d, 0)),
                      jnp.where(row >= SUBL - d, 0.0, pltpu.roll(x[1], SUBL - d, 0)))
                t = _cmul(pw[k], sh)
                x = (x[0] + t[0], x[1] + t[1])
            t = _cmul(tab, (c_re, c_im))
            x = (x[0] + t[0], x[1] + t[1])
            l_re[pl.ds(t0, SUBL), :] = x[0]
            l_im[pl.ds(t0, SUBL), :] = x[1]
            tp = jnp.maximum(t0 - 1, 0)
            live = (i > 0).astype(f32)
            p_re = sr[pl.ds(tp, 1), :] * live
            p_im = si[pl.ds(tp, 1), :] * live
            sp_re = jnp.where(row == 0, p_re, pltpu.roll(sr[pl.ds(t0, SUBL), :], 1, 0))
            sp_im = jnp.where(row == 0, p_im, pltpu.roll(si[pl.ds(t0, SUBL), :], 1, 0))
            acc_re = acc_re + x[0] * sp_re + x[1] * sp_im
            acc_im = acc_im + x[1] * sp_re - x[0] * sp_im
            return (x[0][:1, :], x[1][:1, :], acc_re, acc_im)

        z1 = jnp.zeros((1, w), f32)
        z8 = jnp.zeros((SUBL, w), f32)
        _, _, acc_re, acc_im = lax.fori_loop(0, nblk, step, (z1, z1, z8, z8), unroll=2)
        da_re[...] = jnp.sum(acc_re, axis=0, keepdims=True)
        da_im[...] = jnp.sum(acc_im, axis=0, keepdims=True)

    strip = pl.BlockSpec((seq, w), lambda j: (0, j))
    lane = pl.BlockSpec((1, w), lambda j: (0, j))
    return pl.pallas_call(
        body, name="s5_scan_bwd", grid=(nb,),
        in_specs=[strip, strip, strip, strip, lane, lane],
        out_specs=[strip, strip, lane, lane],
        out_shape=[jax.ShapeDtypeStruct((seq, ns), f32)] * 2 + [jax.ShapeDtypeStruct((1, ns), f32)] * 2,
        compiler_params=_cparams(1),
    )(ds_re, ds_im, s_re, s_im, a_re, a_im)


def _rms(x, w):
    return x * lax.rsqrt(jnp.mean(x * x, axis=-1, keepdims=True) + EPS) * w


def f_rms(x, w):
    return (_rms(x, w),)


def f_rms_res(x, w):
    return (_rms(x, w), x)


def f_s5_prep(lam_re, lam_im, log_step, b_re, b_im):
    e = _expand(log_step.shape[1], S5_NS, S5_N)
    step = hdot(jnp.exp(log_step), e)
    mag = jnp.exp(lam_re * step)
    ab_re, ab_im = mag * jnp.cos(lam_im * step), mag * jnp.sin(lam_im * step)
    den = lam_re * lam_re + lam_im * lam_im
    nr = ab_re - 1.0
    coef_re = (nr * lam_re + ab_im * lam_im) / den
    coef_im = (ab_im * lam_re - nr * lam_im) / den
    bb_re, bb_im = coef_re * b_re - coef_im * b_im, coef_re * b_im + coef_im * b_re
    sel = (_rows((BW, S5_P)) % S5_P == _cols((BW, S5_P))).astype(f32)
    blk = _rows((BW, S5_NS)) // S5_P == _cols((BW, S5_NS)) // S5_N
    rows_bd = lambda t: jnp.where(blk, hdot(sel, t), 0.0)
    return (ab_re, ab_im, rows_bd(bb_re), rows_bd(bb_im))


def f_s5_c(c_re, c_im):
    sel_t = (_cols((S5_P, BW)) % S5_P == _rows((S5_P, BW))).astype(f32)
    blk_t = _rows((S5_NS, BW)) // S5_N == _cols((S5_NS, BW)) // S5_P
    cols_bd = lambda t: jnp.where(blk_t, hdot_tn(t, sel_t), 0.0)
    return (cols_bd(c_re), cols_bd(c_im))


def f_s5_act(y_lin, u, gate, d, w_glu):
    y = jax.nn.gelu(y_lin + d * u)
    y = y * jax.nn.sigmoid(bdot_w(y, *w_glu))
    return (y * jax.nn.silu(gate),)


def f_sgu(u, v, gate, ln_w, ln_b, *rest):
    w_s, b_pad = rest[:8], rest[8]
    t = u.shape[0]
    u32, v32 = jax.nn.gelu(u), jax.nn.gelu(v)
    mu = jnp.mean(v32, axis=-1, keepdims=True)
    var = jnp.mean(jnp.square(v32 - mu), axis=-1, keepdims=True)
    vn = (v32 - mu) * lax.rsqrt(var + EPS) * ln_w + ln_b
    tri = _rows((t, t)) >= _cols((t, t))
    s = hdot_tn(b_pad, _expand(LANES, BW, BW // 8))
    for h in range(8):
        s = s + bdot(jnp.where(tri, w_s[h], 0.0), vn) * _lane_mask(BW, 64 * h, 64 * h + 64)
    return (u32 * s * jax.nn.silu(gate),)


def f_m2_conv(x, w, b):
    return (sum(_row_of(w, k) * shift(x, 3 - k) for k in range(4)) + b,)


def f_sc(bg, cg, h, gate, w):
    z = cg * h
    conv = sum(_row_of(w, k) * shift(z, 2 - k) for k in range(3))
    return (bg * conv * jax.nn.silu(gate),)


def f_m2(z, xc, b0, b1, c0, c1, dt_raw, dt_bias, a_log, d_par, norm_w, st):
    q = z.shape[0]
    x = jax.nn.silu(xc)
    bm, cm = (jax.nn.silu(b0), jax.nn.silu(b1)), (jax.nn.silu(c0), jax.nn.silu(c1))
    dt = jax.nn.softplus(dt_raw + dt_bias)
    da = dt * (-jnp.exp(a_log))
    tri = _rows((q, q)) >= _cols((q, q))
    acs = hdot(tri.astype(f32), da)
    e = _expand(LANES, BW, M2_HEAD_DIM)
    dt_f, acs_f = hdot(dt, e), hdot(acs, e)
    last = _rows((q, BW)) == q - 1
    alast_f = jnp.sum(jnp.where(last, acs_f, 0.0), axis=0, keepdims=True)
    xdt = x * dt_f
    xdec = xdt * jnp.exp(alast_f - acs_f)
    acs_t = acs.T
    st_new = st * jnp.exp(alast_f)
    y_diag, y_off = 0.0, 0.0
    for g in range(M2_GROUPS):
        gm = _lane_mask(BW, 256 * g, 256 * g + 256)
        cb = bdot_nt(cm[g], bm[g])
        st_new = st_new + bdot_tn(bm[g], xdec * gm)
        y_off = y_off + bdot(cm[g], st) * gm
        for hh in range(M2_HEADS // M2_GROUPS):
            h = g * (M2_HEADS // M2_GROUPS) + hh
            col = jnp.sum(jnp.where(_cols((q, LANES)) == h, acs, 0.0), axis=1, keepdims=True)
            row = jnp.sum(jnp.where(_rows((LANES, q)) == h, acs_t, 0.0), axis=0, keepdims=True)
            decay = jnp.exp(jnp.where(tri, col - row, -1e30))
            y_diag = y_diag + bdot(cb * decay, xdt) * _lane_mask(BW, 64 * h, 64 * h + 64)
    d_f = sum(jnp.sum(jnp.where(_cols((1, LANES)) == h, d_par, 0.0), axis=1, keepdims=True)
              * _lane_mask(BW, 64 * h, 64 * h + 64) for h in range(M2_HEADS))
    y = y_diag + y_off * jnp.exp(acs_f) + d_f * x
    y = y * jax.nn.silu(z)
    return (_rms(y, norm_w), st_new)


def f_branch_mix(*v):
    nb = len(v) // 4
    ys, lg, wb, mb = v[0:nb], v[nb:2 * nb], v[2 * nb:3 * nb], v[3 * nb:]
    return (sum(jax.nn.sigmoid(lg[k] + mb[k]) * bdot_w(ys[k], *wb[k]) for k in range(nb)),)


def _param(arr, shadow=False):
    return A(arr, tuple(arr.shape), lambda i, nd=arr.ndim: (0,) * nd, shadow)


def _tb(arr, t, width, colblk):
    return A(arr, (t, width), lambda i: (i, colblk))


def _strip(arr, seq, colblk0, stride=1):
    return A(arr, (seq, LANES), lambda j: (0, colblk0 + stride * j))


def _s5_prep_args(p):
    lam_re = p["s5_lambda_re"].reshape(1, S5_NS)
    lam_im = p["s5_lambda_im"].reshape(1, S5_NS)
    log_step = jnp.pad(p["s5_log_step"].reshape(1, S5_GROUPS), ((0, 0), (0, LANES - S5_GROUPS)))
    b_lanes = lambda b: jnp.transpose(b, (2, 0, 1)).reshape(S5_P, S5_NS)
    return [_param(v) for v in (lam_re, lam_im, log_step, b_lanes(p["s5_b_re"]), b_lanes(p["s5_b_im"]))]


def _s5_c_args(p):
    c_lanes = lambda c: jnp.transpose(c, (1, 0, 2)).reshape(S5_P, S5_NS)
    return [_param(c_lanes(p["s5_c_re"])), _param(c_lanes(p["s5_c_im"]))]


def layer_fwd(x, p):
    seq = x.shape[0]
    nt = seq // CHUNK
    t2 = 256
    sv = {}
    t4 = min(512, seq)
    hb = block_fwd("rms_in", f_rms, seq // t4, [_tb(x, t4, D_MODEL, 0), _param(p["norm_w"].reshape(1, D_MODEL))],
                   [O((seq, D_MODEL), bf16, (t4, D_MODEL), lambda i: (i, 0))])[0]
    proj = mm("mm_in", NN, [(Op(hb), Op(p["w_in"]), D_MODEL, 1)], seq, PW, 2048, 1152)
    whole = lambda shape, dt: O(shape, dt, shape, lambda i: (0, 0))
    ab_re, ab_im, bb_re, bb_im = block_fwd("s5_prep", f_s5_prep, 1, _s5_prep_args(p),
                                           [whole((1, S5_NS), f32)] * 2 + [whole((BW, S5_NS), bf16)] * 2)
    cc_re, cc_im = block_fwd("s5_c", f_s5_c, 1, _s5_c_args(p), [whole((S5_NS, BW), bf16)] * 2)
    bu_re = mm("mm_bu_re", NN, [(Op(proj, 0, C_S5U), Op(bb_re), BW, 1)], seq, S5_NS, 1024, 1024)
    bu_im = mm("mm_bu_im", NN, [(Op(proj, 0, C_S5U), Op(bb_im), BW, 1)], seq, S5_NS, 1024, 1024)
    s_re, s_im = s5_scan_fwd(bu_re, bu_im, ab_re, ab_im)
    y_lin = mm("mm_s5y", NN, [(Op(s_re), Op(cc_re), S5_NS, 1), (Op(s_im), Op(cc_im), S5_NS, -1)], seq, BW, 512, BW)
    s5_act_args = [_tb(y_lin, t4, BW, 0), _tb(proj, t4, BW, C_S5U // BW), _tb(proj, t4, BW, C_S5G // BW),
                   _param(p["s5_d"].reshape(1, BW)), _param(p["s5_w_glu"], True)]
    out_bw =O((seq, BW), f32, (CHUNK, BW), lambda i: (i, 0))
    y_a = block_fwd("s5_act", f_s5_act, seq // t4, s5_act_args, [O((seq, BW), f32, (t4, BW), lambda i: (i, 0))])[0]
    y_b = block_fwd("sgu", f_sgu, nt, _sgu_args(proj, p), [out_bw])[0]
    xc = block_fwd("m2_conv", f_m2_conv, 8, _m2_conv_args(proj, p, seq),
                   [O((seq, 2 * BW), f32, (seq, LANES), lambda j: (0, j))])[0]
    y_c, st_saved = block_fwd("m2_ssd", f_m2, nt, _m2_args(proj, xc, p), [out_bw], carries=[(M2_STATE, BW)])
    y_d = block_fwd("sc", f_sc, 4, _sc_args(proj, p, seq), [O((seq, BW), f32, (seq, LANES), lambda j: (0, j))])[0]
    ys = [y_a, y_b, y_c, y_d]
    merged = block_fwd("branch_mix", f_branch_mix, seq // t2, _mix_args(ys, proj, p, t2),
                       [O((seq, D_MODEL), bf16, (t2, D_MODEL), lambda i: (i, 0))])[0]
    x_new = mm("mm_out", NN, [(Op(merged), Op(p["w_out"]), D_MODEL, 1)], seq, D_MODEL, 1024, D_MODEL, add=Op(x))
    sv.update(x=x, hb=hb, proj=proj, ab=(ab_re, ab_im), bb=(bb_re, bb_im), cc=(cc_re, cc_im), s=(s_re, s_im), y_lin=y_lin,
              xc=xc, st=st_saved, ys=ys, merged=merged)
    return x_new, sv


def _sgu_args(proj, p):
    c0 = C_SGU // BW
    args = [_tb(proj, CHUNK, BW, c0), _tb(proj, CHUNK, BW, c0 + 1), _tb(proj, CHUNK, BW, c0 + 2),
            _param(p["sgu_ln_w"].reshape(1, BW)), _param(p["sgu_ln_b"].reshape(1, BW))]
    args += [A(p["sgu_w"], (None, CHUNK, CHUNK), lambda i, h=h: (h, 0, 0)) for h in range(8)]
    args.append(_param(jnp.pad(p["sgu_b"], ((0, LANES - 8), (0, 0)))))
    return args


def _m2_conv_args(proj, p, seq):
    return [_strip(proj, seq, C_XBC // LANES), A(p["m2_conv_w"], (4, LANES), lambda j: (0, j)),
            A(p["m2_conv_b"].reshape(1, 2 * BW), (1, LANES), lambda j: (0, j))]


def _pad_lanes(v):
    return jnp.pad(v.reshape(1, -1), ((0, 0), (0, LANES - v.size)))


def _m2_args(proj, xc, p):
    args = [_tb(proj, CHUNK, BW, C_M2Z // BW), _tb(xc, CHUNK, BW, 0)]
    args += [_tb(xc, CHUNK, LANES, 4 + k) for k in range(4)]
    args.append(_tb(proj, CHUNK, LANES, C_DT // LANES))
    args += [_param(_pad_lanes(p["m2_dt_bias"])), _param(_pad_lanes(p["m2_a_log"])), _param(_pad_lanes(p["m2_d"])),
             _param(p["m2_norm_w"].reshape(1, BW))]
    return args


def _sc_args(proj, p, seq):
    c0 = C_SC // LANES
    return [_strip(proj, seq, c0 + k, 4) for k in range(4)] + [A(p["sc_conv_w"], (3, LANES), lambda j: (0, j))]


def _mix_args(ys, proj, p, t, ks=range(N_BRANCH)):
    args = [_tb(ys[k], t, BW, 0) for k in ks]
    args += [_tb(proj, t, D_MODEL, k) for k in ks]
    args += [A(p["w_branch"], (BW, D_MODEL), lambda i, k=k: (k, 0), True) for k in ks]
    mb = p["merge_b"].reshape(N_BRANCH, 1, D_MODEL)
    args += [A(mb, (None, 1, D_MODEL), lambda i, k=k: (k, 0, 0)) for k in ks]
    return args


def layer_bwd(d_out, p, sv):
    seq = d_out.shape[0]
    nt = seq // CHUNK
    t4 = min(512, seq)
    proj, ys = sv["proj"], sv["ys"]
    g = {}
    acc = lambda shape: O(tuple(shape), f32, tuple(shape), lambda i, nd=len(shape): (0,) * nd)
    d_merged = mm("mm_out_dx", NT, [(Op(d_out), Op(p["w_out"]), D_MODEL, 1)], seq, D_MODEL, 1024, D_MODEL, out_dtype=f32)
    g["w_out"] = mm("mm_out_dw", TN, [(Op(sv["merged"]), Op(d_out), seq, 1)], D_MODEL, D_MODEL, 512, D_MODEL, out_dtype=bf16)
    dys, dwb, dmb, dproj, tb = [], [], [], None, min(256, seq)
    for half in range(2):
        gouts = [O((seq, BW), f32, (tb, BW), lambda i: (i, 0)) for _ in range(2)]
        gouts.append(O((seq, PW_MAIN), bf16, (tb, 2 * D_MODEL), lambda i, half=half: (i, half), alias=dproj))
        gouts += [acc((BW, D_MODEL)) for _ in range(2)] + [acc((1, D_MODEL)) for _ in range(2)]
        routes = [R(k, k) for k in range(2)] + [R(2 + k, 2, k * D_MODEL) for k in range(2)]
        routes += [R(4 + k, 3 + k, acc=True) for k in range(2)] + [R(6 + k, 5 + k, acc=True) for k in range(2)]
        res = block_bwd(f"branch_mix_bwd{half}", f_branch_mix, seq // tb, _mix_args(ys, proj, p, tb, (2 * half, 2 * half + 1)),
                        [_tb(d_merged, tb, D_MODEL, 0)], gouts, routes)
        dys, dproj, dwb, dmb = dys + list(res[:2]), res[2], dwb + list(res[3:5]), dmb + list(res[5:7])
    g["w_branch"] = jnp.stack(dwb)
    g["merge_b"] = jnp.concatenate(dmb, axis=0)
    res = block_bwd("sc_bwd", f_sc, 4, _sc_args(proj, p, seq), [_strip(dys[3], seq, 0)],
                    [O((seq, PW_MAIN), bf16, (seq, 4 * LANES), lambda j: (0, C_SC // (4 * LANES) + j), alias=dproj),
                     O((3, BW), f32, (3, LANES), lambda j: (0, j))],
                    [R(k, 0, k * LANES) for k in range(4)] + [R(4, 1)])
    dproj, g["sc_conv_w"] = res
    res = block_bwd("m2_ssd_bwd", f_m2, nt, _m2_args(proj, sv["xc"], p), [_tb(dys[2], CHUNK, BW, 0)],
                    [O((seq, PW_MAIN), bf16, (CHUNK, BW), lambda i: (i, C_M2Z // BW), alias=dproj),
                     O((seq, 2 * BW), f32, (CHUNK, 2 * BW), lambda i: (i, 0)),
                     O((seq, LANES), bf16, (CHUNK, LANES), lambda i: (i, 0)),
                     acc((1, LANES)), acc((1, LANES)), acc((1, LANES)), acc((1, BW))],
                    [R(0, 0), R(1, 1, 0)] + [R(2 + k, 1, BW + k * LANES) for k in range(4)] + [R(6, 2)]
                    + [R(7, 3, acc=True), R(8, 4, acc=True), R(9, 5, acc=True), R(10, 6, acc=True)],
                    saved=[sv["st"]], rev=True)
    dproj, dxc, d_dt = res[0], res[1], res[2]
    g["m2_dt_bias"], g["m2_a_log"], g["m2_d"] = (r[0, :M2_HEADS] for r in res[3:6])
    g["m2_norm_w"] = res[6].reshape(BW)
    res = block_bwd("m2_conv_bwd", f_m2_conv, 8, _m2_conv_args(proj, p, seq), [_strip(dxc, seq, 0)],
                    [O((seq, PW_MAIN), bf16, (seq, LANES), lambda j: (0, C_XBC // LANES + j), alias=dproj),
                     O((4, 2 * BW), f32, (4, LANES), lambda j: (0, j)), O((1, 2 * BW), f32, (1, LANES), lambda j: (0, j))],
                    [R(0, 0), R(1, 1), R(2, 2)])
    dproj, g["m2_conv_w"], cb = res
    g["m2_conv_b"] = cb.reshape(2 * BW)
    res = block_bwd("sgu_bwd", f_sgu, nt, _sgu_args(proj, p), [_tb(dys[1], CHUNK, BW, 0)],
                    [O((seq, PW_MAIN), bf16, (CHUNK, 3 * BW), lambda i: (i, C_SGU // (3 * BW)), alias=dproj),
                     acc((1, BW)), acc((1, BW))] + [acc((CHUNK, CHUNK)) for _ in range(8)] + [acc((LANES, CHUNK))],
                    [R(0, 0, 0), R(1, 0, BW), R(2, 0, 2 * BW), R(3, 1, acc=True), R(4, 2, acc=True)]
                    + [R(5 + h, 3 + h, acc=True) for h in range(8)] + [R(13, 11, acc=True)])
    dproj = res[0]
    g["sgu_ln_w"], g["sgu_ln_b"] = res[1].reshape(BW), res[2].reshape(BW)
    g["sgu_w"] = jnp.stack(res[3:11])
    g["sgu_b"] = res[11][:8]
    y_lin, (s_re, s_im), (ab_re, ab_im) = sv["y_lin"], sv["s"], sv["ab"]
    s5_act_args = [_tb(y_lin, t4, BW, 0), _tb(proj, t4, BW, C_S5U // BW), _tb(proj, t4, BW, C_S5G // BW),
                   _param(p["s5_d"].reshape(1, BW)), _param(p["s5_w_glu"], True)]
    res = block_bwd("s5_act_bwd", f_s5_act, seq // t4, s5_act_args, [_tb(dys[0], t4, BW, 0)],
                    [O((seq, BW), bf16, (t4, BW), lambda i: (i, 0)), O((seq, BW), f32, (t4, BW), lambda i: (i, 0)),
                     O((seq, PW_MAIN), bf16, (t4, BW), lambda i: (i, C_S5G // BW), alias=dproj),
                     acc((1, BW)), acc((BW, BW))],
                    [R(0, 0), R(1, 1), R(2, 2), R(3, 3, acc=True), R(4, 4, acc=True)])
    dy_lin, du1, dproj = res[0], res[1], res[2]
    g["s5_d"] = res[3].reshape(S5_GROUPS, S5_P)
    g["s5_w_glu"] = res[4]
    (bb_re, bb_im), (cc_re, cc_im) = sv["bb"], sv["cc"]
    ds_re = mm("mm_s5y_dre", NT, [(Op(dy_lin), Op(cc_re), BW, 1)], seq, S5_NS, 1024, 1024)
    ds_im = mm("mm_s5y_dim", NT, [(Op(dy_lin), Op(cc_im), BW, -1)], seq, S5_NS, 1024, 1024)
    dc_re = mm("mm_s5y_dcre", TN, [(Op(s_re), Op(dy_lin), seq, 1)], S5_NS, BW, 512, BW)
    dc_im = mm("mm_s5y_dcim", TN, [(Op(s_im), Op(dy_lin), seq, -1)], S5_NS, BW, 512, BW)
    l_re, l_im, da_re, da_im = s5_scan_bwd(ds_re, ds_im, s_re, s_im, ab_re, ab_im)
    dproj = mm("mm_bu_dx", NT, [(Op(l_re), Op(bb_re), S5_NS, 1), (Op(l_im), Op(bb_im), S5_NS, 1)],
               seq, BW, 512, BW, add=Op(du1), out=dproj, out_col=C_S5U)
    dbb = [mm(f"mm_bu_dw{n}", TN, [(Op(proj, 0, C_S5U), Op(l), seq, 1)], BW, S5_NS, 512, 1024)
           for n, l in (("re", l_re), ("im", l_im))]
    gouts = [acc((1, S5_NS)), acc((1, S5_NS)), acc((1, LANES))] + [acc((S5_P, S5_NS))] * 2
    res = block_bwd("s5_prep_bwd", f_s5_prep, 1, _s5_prep_args(p),
                    [_param(v) for v in (da_re, da_im, dbb[0], dbb[1])], gouts, [R(k, k, acc=True) for k in range(5)])
    g["s5_lambda_re"], g["s5_lambda_im"] = res[0].reshape(S5_GROUPS, S5_N), res[1].reshape(S5_GROUPS, S5_N)
    g["s5_log_step"] = res[2][0, :S5_GROUPS]
    b_natural = lambda b: jnp.transpose(b.reshape(S5_P, S5_GROUPS, S5_N), (1, 2, 0))
    c_natural = lambda c: jnp.transpose(c.reshape(S5_P, S5_GROUPS, S5_N), (1, 0, 2))
    g["s5_b_re"], g["s5_b_im"] = b_natural(res[3]), b_natural(res[4])
    res = block_bwd("s5_c_bwd", f_s5_c, 1, _s5_c_args(p), [_param(dc_re), _param(dc_im)],
                    [acc((S5_P, S5_NS))] * 2, [R(0, 0, acc=True), R(1, 1, acc=True)])
    g["s5_c_re"], g["s5_c_im"] = c_natural(res[0]), c_natural(res[1])
    hb, w_in = sv["hb"], p["w_in"]
    dh = mm("mm_in_dx", NT, [(Op(dproj), Op(w_in), PW_MAIN, 1), (Op(d_dt), Op(w_in, 0, PW_MAIN), LANES, 1)],
            seq, D_MODEL, 512, 512)
    dw = mm("mm_in_dw", TN, [(Op(hb), Op(dproj), seq, 1)], D_MODEL, PW_MAIN, 1024, 2048, out_dtype=bf16, out_width=PW)
    g["w_in"] = mm("mm_in_dwdt", TN, [(Op(hb), Op(d_dt), seq, 1)], D_MODEL, LANES, 256, LANES, out=dw, out_col=C_DT)
    dx, dnw = block_bwd("rms_in_bwd", f_rms_res, seq // t4, [_tb(sv["x"], t4, D_MODEL, 0), _param(p["norm_w"].reshape(1, D_MODEL))],
                        [_tb(dh, t4, D_MODEL, 0), _tb(d_out, t4, D_MODEL, 0)],
                        [O((seq, D_MODEL), f32, (t4, D_MODEL), lambda i: (i, 0)), acc((1, D_MODEL))],
                        [R(0, 0), R(1, 1, acc=True)])
    g["norm_w"] = dnw.reshape(D_MODEL)
    return dx, g


def loss_head(x, w, target):
    seq = x.shape[0]
    t = min(512, seq)

    def body(x_ref, w_ref, t_ref, loss_ref, dx_ref, dw_ref):
        step = pl.program_id(0)

        def f(xv, wv):
            err = _rms(xv, wv) - t_ref[...]
            return 0.5 * jnp.sum(jnp.mean(err * err, axis=-1, keepdims=True), axis=0, keepdims=True)

        val, vjp_fn = jax.vjp(f, x_ref[...], w_ref[...])
        dx, dw = vjp_fn(jnp.ones((1, 1), f32))
        dx_ref[...] = dx

        @pl.when(step == 0)
        def _():
            loss_ref[...] = jnp.broadcast_to(val, loss_ref.shape)
            dw_ref[...] = dw

        @pl.when(step > 0)
        def _():
            loss_ref[...] += jnp.broadcast_to(val, loss_ref.shape)
            dw_ref[...] += dw

    blk = pl.BlockSpec((t, D_MODEL), lambda i: (i, 0))
    row = pl.BlockSpec((1, D_MODEL), lambda i: (0, 0))
    return pl.pallas_call(
        body, name="loss_head", grid=(seq // t,),
        in_specs=[blk, row, blk],
        out_specs=[pl.BlockSpec((1, LANES), lambda i: (0, 0)), blk, row],
        out_shape=[jax.ShapeDtypeStruct((1, LANES), f32), jax.ShapeDtypeStruct((seq, D_MODEL), f32),
                   jax.ShapeDtypeStruct((1, D_MODEL), f32)],
        compiler_params=_cparams(1),
    )(x, w.reshape(1, D_MODEL), target)


LAYER_KEYS = ("norm_w", "w_in", "s5_lambda_re", "s5_lambda_im", "s5_b_re", "s5_b_im", "s5_c_re", "s5_c_im", "s5_d",
              "s5_log_step", "s5_w_glu", "sgu_ln_w", "sgu_ln_b", "sgu_w", "sgu_b", "m2_conv_w", "m2_conv_b",
              "m2_dt_bias", "m2_a_log", "m2_d", "m2_norm_w", "sc_conv_w", "merge_b", "w_branch", "w_out")


def local_step(x, target, layers, final_norm_w):
    saved = []
    for p in layers:
        x, sv = layer_fwd(x, p)
        saved.append(sv)
    loss, dx, dfw = loss_head(x, final_norm_w, target)
    grads = []
    for p, sv in zip(reversed(layers), reversed(saved)):
        dx, g = layer_bwd(dx, p, sv)
        grads.append(g)
    return loss[0, 0], dx, grads[::-1], dfw.reshape(D_MODEL)


MESH = pl.DeviceIdType.MESH
ANY = pl.BlockSpec(memory_space=pl.ANY)


def _me():
    return lax.axis_index("x"), lax.axis_index("y"), lax.axis_index("c")


def _other_chips(x, y):
    return [(1 - x, y), (x, 1 - y), (1 - x, 1 - y)]


def _rcopy(src, dst, send, recv, dev):
    return pltpu.make_async_remote_copy(src_ref=src, dst_ref=dst, send_sem=send, recv_sem=recv,
                                        device_id=dev, device_id_type=MESH)


def _route_cut(rows, dtype):
    tile = 2 * SUBL * (4 // jnp.dtype(dtype).itemsize)
    return rows // 2 if rows % tile == 0 else rows


def _comm_call(name, body, arrs, out_shape, n_remote, aliases=None):
    n = len(arrs)
    return pl.pallas_call(
        body, name=name, in_specs=[ANY] * n, out_specs=[ANY] * len(out_shape), out_shape=out_shape,
        scratch_shapes=[pltpu.SemaphoreType.DMA((n, n_remote)), pltpu.SemaphoreType.DMA((n, n_remote))],
        input_output_aliases=aliases or {},
        compiler_params=pltpu.CompilerParams(has_side_effects=True),
    )(*arrs)


def gather_chips(name, arrs):
    n = len(arrs)
    cut = [_route_cut(a.shape[1], a.dtype) for a in arrs]

    def body(*refs):
        ins, outs = refs[:n], refs[n:2 * n]
        send, recv = refs[2 * n:]
        x, y, c = _me()
        jme, jx, jy, jd = 2 * x + y, 2 * (1 - x) + y, 2 * x + 1 - y, 2 * (1 - x) + 1 - y
        to_x, to_y, sib = (1 - x, y, c), (x, 1 - y, c), (x, y, 1 - c)

        def part(ref, a, hi):
            return ref.at[pl.ds(cut[a], ref.shape[0] - cut[a])] if hi else ref.at[pl.ds(0, cut[a])]

        def cp(a, k, ref, dev):
            return _rcopy(ref, ref, send.at[a, k], recv.at[a, k], dev)

        split = [a for a in range(n) if cut[a] < arrs[a].shape[1]]
        sent = [_rcopy(ins[a].at[c], outs[a].at[jme, c], send.at[a, k], recv.at[a, k], dev)
                for a in range(n) for k, dev in ((0, to_x), (1, to_y))]
        for s in sent:
            s.start()
        for a in range(n):
            blk = outs[a].at[jx, c]
            cp(a, 0, blk, to_x).wait_recv()
            sent += [cp(a, 2, part(blk, a, False), to_y), cp(a, 4, blk, sib)]
            sent[-2].start()
            sent[-1].start()
        for a in range(n):
            blk = outs[a].at[jy, c]
            cp(a, 1, blk, to_y).wait_recv()
            sent.append(cp(a, 5, blk, sib))
            sent[-1].start()
            if a in split:
                sent.append(cp(a, 3, part(blk, a, True), to_x))
                sent[-1].start()
        for a in range(n):
            lo = part(outs[a].at[jd, c], a, False)
            cp(a, 2, lo, to_y).wait_recv()
            sent.append(cp(a, 6, lo, sib))
            sent[-1].start()
        for a in split:
            hi = part(outs[a].at[jd, c], a, True)
            cp(a, 3, hi, to_x).wait_recv()
            sent.append(cp(a, 7, hi, sib))
            sent[-1].start()
        for a in range(n):
            cp(a, 4, outs[a].at[jx, 1 - c], sib).wait_recv()
            cp(a, 5, outs[a].at[jy, 1 - c], sib).wait_recv()
            cp(a, 6, part(outs[a].at[jd, 1 - c], a, False), sib).wait_recv()
        for a in split:
            cp(a, 7, part(outs[a].at[jd, 1 - c], a, True), sib).wait_recv()
        for s in sent:
            s.wait_send()

    out_shape = [jax.ShapeDtypeStruct((4,) + a.shape, a.dtype) for a in arrs]
    got = _comm_call(name, body, arrs, out_shape, 8)
    jme = 2 * lax.axis_index("x") + lax.axis_index("y")
    return [lax.dynamic_update_index_in_dim(g, a, jme, 0) for g, a in zip(got, arrs)]


def swap_halves(name, arrs):
    n = len(arrs)

    def body(*refs):
        ins, outs = refs[:n], refs[n:2 * n]
        send, recv = refs[2 * n:]
        x, y, c = _me()
        remote = [_rcopy(ins[a].at[1 - c], outs[a], send.at[a, 0], recv.at[a, 0], (x, y, 1 - c)) for a in range(n)]
        for cp in remote:
            cp.start()
        for cp in remote:
            cp.wait()

    return _comm_call(name, body, arrs, [jax.ShapeDtypeStruct(a.shape[1:], a.dtype) for a in arrs], 1)


def exchange_chips(name, arrs):
    n = len(arrs)

    def body(*refs):
        ins, outs = refs[:n], refs[n:2 * n]
        send, recv = refs[2 * n:]
        x, y, c = _me()
        remote = [_rcopy(ins[a].at[2 * cx + cy], outs[a].at[k], send.at[a, k], recv.at[a, k], (cx, cy, c))
                  for a in range(n) for k, (cx, cy) in enumerate(_other_chips(x, y))]
        for cp in remote:
            cp.start()
        for cp in remote:
            cp.wait()

    return _comm_call(name, body, arrs, [jax.ShapeDtypeStruct((3,) + a.shape[1:], a.dtype) for a in arrs], 3)


def gather_cores(name, arrs):
    n = len(arrs)
    r = n - 1

    def body(*refs):
        bufs, spread = refs[n:2 * n], refs[2 * n]
        send, recv = refs[2 * n + 1:]
        x, y, c = _me()
        jme, sib = 2 * x + y, (x, y, 1 - c)
        chips = _other_chips(x, y)
        sent = [_rcopy(bufs[a].at[c], bufs[a].at[c], send.at[a, 0], recv.at[a, 0], sib) for a in range(n)]
        sent += [_rcopy(bufs[r].at[c], spread.at[jme, c], send.at[r, 1 + k], recv.at[r, 1 + k], (cx, cy, c))
                 for k, (cx, cy) in enumerate(chips)]
        for cp in sent:
            cp.start()
        for k, (cx, cy) in enumerate(chips):
            landed = spread.at[2 * cx + cy, c]
            _rcopy(landed, landed, send.at[r, 1 + k], recv.at[r, 1 + k], (cx, cy, c)).wait_recv()
            sent.append(_rcopy(landed, landed, send.at[r, 4 + k], recv.at[r, 4 + k], sib))
            sent[-1].start()
        for a in range(n):
            _rcopy(bufs[a].at[1 - c], bufs[a].at[1 - c], send.at[a, 0], recv.at[a, 0], sib).wait_recv()
        for k, (cx, cy) in enumerate(chips):
            passed = spread.at[2 * cx + cy, 1 - c]
            _rcopy(passed, passed, send.at[r, 4 + k], recv.at[r, 4 + k], sib).wait_recv()
        for cp in sent:
            cp.wait_send()

    out_shape = [jax.ShapeDtypeStruct(a.shape, a.dtype) for a in arrs]
    out_shape.append(jax.ShapeDtypeStruct((4,) + arrs[r].shape, arrs[r].dtype))
    res = _comm_call(name, body, arrs, out_shape, 7, aliases={a: a for a in range(n)})
    jme = 2 * lax.axis_index("x") + lax.axis_index("y")
    return res[:n], lax.dynamic_update_index_in_dim(res[n], res[r], jme, 0)


ROW_BLOCK = 512


def esum(name, terms, rows, width, out_dtype, out_slots=None):
    tr = next((t for t in range(min(rows, ROW_BLOCK), 0, -SUBL) if rows % t == 0 and t % SUBL == 0), rows)
    where =jnp.stack([lax.axis_index("c"), 2 * lax.axis_index("x") + lax.axis_index("y")]).astype(jnp.int32)
    pick = {"c": 0, "j": 1}

    def body(s_ref, *refs):
        acc = refs[0][...].astype(f32)
        for r in refs[1:-1]:
            acc = acc + r[...].astype(f32)
        refs[-1][...] = acc.astype(out_dtype)

    specs = []
    for arr, lead in terms:
        if lead is None:
            specs.append(pl.BlockSpec((tr, width), lambda i, s: (i, 0)))
        elif isinstance(lead, str):
            specs.append(pl.BlockSpec((None, tr, width), lambda i, s, lead=lead: (s[pick[lead]], i, 0)))
        else:
            specs.append(pl.BlockSpec((None, tr, width), lambda i, s, lead=lead: (lead, i, 0)))
    if out_slots is None:
        out_spec = pl.BlockSpec((tr, width), lambda i, s: (i, 0))
        out_shape = jax.ShapeDtypeStruct((rows, width), out_dtype)
    else:
        out_spec = pl.BlockSpec((None, tr, width), lambda i, s: (s[0], i, 0))
        out_shape = jax.ShapeDtypeStruct((out_slots, rows, width), out_dtype)
    return pl.pallas_call(
        body, name=name,
        grid_spec=pltpu.PrefetchScalarGridSpec(num_scalar_prefetch=1, grid=(rows // tr,), in_specs=specs, out_specs=out_spec),
        out_shape=out_shape, compiler_params=_cparams(1),
    )(where, *[t[0] for t in terms])


def reduce_to_shards(parts):
    tags = [str(k) for k in range(len(parts))]
    theirs = swap_halves("rs_swap", parts)
    t1 = []
    for tag, p, th in zip(tags, parts, theirs):
        _, _, h, w = p.shape
        t1.append(esum("rs_add_cores" + tag, [(p.reshape(2, 4 * h, w), "c"), (th.reshape(4 * h, w), None)],
                       4 * h, w, p.dtype).reshape(4, h, w))
    landed = exchange_chips("rs_exchange", t1)
    red = []
    for tag, p, t, got in zip(tags, parts, t1, landed):
        _, _, h, w = p.shape
        red.append(esum("rs_add_chips" + tag, [(t, "j"), (got, 0), (got, 1), (got, 2)], h, w, f32, out_slots=2))
    return gather_cores("rs_gather", red)


def _adamw_step(w_ref, g_ref, m_ref, v_ref, d_ref, nm_ref, nv_ref):
    gv = g_ref[...]
    nm = ADAM_B1 * m_ref[...] + (1.0 - ADAM_B1) * gv
    nv = ADAM_B2 * v_ref[...] + (1.0 - ADAM_B2) * jnp.square(gv)
    m_hat = nm / (1.0 - ADAM_B1 ** ADAM_STEP)
    v_hat = nv / (1.0 - ADAM_B2 ** ADAM_STEP)
    d_ref[...] = -ADAM_LR * (m_hat / (jnp.sqrt(v_hat) + ADAM_EPS) + ADAM_WD * w_ref[...])
    nm_ref[...] = nm
    nv_ref[...] = nv


def adamw(name, w, g, m, v, tr=None):
    rows, rest = w.shape[0], w.shape[1:]
    if tr is None:
        tr = ROW_BLOCK if rows % ROW_BLOCK == 0 else rows
    assert rows % tr == 0

    def body(*refs):
        _adamw_step(*refs)

    spec = pl.BlockSpec((tr,) + rest, lambda i: (i,) + (0,) * len(rest))
    return pl.pallas_call(
        body, name=name, grid=(rows // tr,), in_specs=[spec] * 4, out_specs=[spec] * 3,
        out_shape=[jax.ShapeDtypeStruct(w.shape, f32)] * 3, compiler_params=_cparams(1),
    )(w, g, m, v)


def adamw_many(name, ws, gs, ms, vs):
    n = len(ws)

    def body(*refs):
        ins, outs = refs[:4 * n], refs[4 * n:]
        for k in range(n):
            _adamw_step(ins[k], ins[n + k], ins[2 * n + k], ins[3 * n + k], outs[k], outs[n + k], outs[2 * n + k])

    vmem = pl.BlockSpec(memory_space=pltpu.VMEM)
    res = pl.pallas_call(
        body, name=name, in_specs=[vmem] * (4 * n), out_specs=[vmem] * (3 * n),
        out_shape=[jax.ShapeDtypeStruct(w.shape, f32) for w in ws] * 3,
        compiler_params=pltpu.CompilerParams(vmem_limit_bytes=VMEM_LIMIT),
    )(*ws, *gs, *ms, *vs)
    return res[:n], res[n:2 * n], res[2 * n:]


PACK_W = 1024


def _pack(parts, halves, row_mult):
    flat = jnp.concatenate([p.reshape(-1) for p in parts])
    per = halves * row_mult * PACK_W
    total = -(-flat.size // per) * per
    flat = jnp.pad(flat, (0, total - flat.size))
    return flat.reshape(halves, total // (halves * PACK_W), PACK_W)


def _unpack(flat, shapes):
    out, pos = [], 0
    for s in shapes:
        n = int(np.prod(s))
        out.append(flat[pos:pos + n].reshape(s))
        pos += n
    return out


SHARDED_BIG = ("w_in", "w_branch", "w_out", "s5_w_glu")
SHARDED_SMALL = ("m2_conv_w", "sc_conv_w", "merge_b")
SHARD_AXIS = {"w_in": 2, "w_branch": 3, "w_out": 1, "s5_w_glu": 1, "m2_conv_w": 2, "sc_conv_w": 2, "merge_b": 2}
REPLICATED = ("norm_w", "s5_lambda_re", "s5_lambda_im", "s5_b_re", "s5_b_im", "s5_c_re", "s5_c_im", "s5_d", "s5_log_step",
              "sgu_ln_w", "sgu_ln_b", "sgu_w", "sgu_b", "m2_conv_b", "m2_dt_bias", "m2_a_log", "m2_d", "m2_norm_w")
WEIGHTS = ("norm_w", "w_in", "s5_lambda_re", "s5_lambda_im", "s5_b_re", "s5_b_im", "s5_c_re", "s5_c_im", "s5_d",
           "s5_log_step", "s5_w_glu", "sgu_ln_w", "sgu_ln_b", "sgu_w", "sgu_b", "m2_conv_w", "m2_conv_b", "m2_dt_bias",
           "m2_a_log", "m2_d", "m2_norm_w", "sc_conv_w", "merge_b", "w_branch", "w_out", "final_norm_w")
N_LAYERS = 2


SHARD_W = IN_DIM // 4
SHARD_PAD = -(-SHARD_W // LANES) * LANES
REGROUP_W = 3 * LANES


def _kernel_pieces():
    out, pos = [], 0
    for s, n in _col_segments():
        while n:
            take = min(n, SHARD_W - s % SHARD_W)
            out.append((pos, s, take))
            pos, s, n = pos + take, s + take, n - take
    return out


def regroup_cols(name, src, steps, out_shape, out=None):
    n_src, rows, width = src.shape
    ow, win = REGROUP_W, REGROUP_W + LANES
    k_max = max(len(p) for _, _, p in steps)
    assert width % LANES == 0

    def body(*refs):
        src_ref, out_ref = refs[0], refs[-5]
        wbuf, obuf, sem_in, sem_out = refs[-4:]

        def fetch(q, slot):
            started = []
            for p, (s, col, lo, hi) in enumerate(steps[q][2]):
                w0 = col // LANES * LANES
                wlen = min(win, width - w0)
                cp = pltpu.make_async_copy(src_ref.at[s, :, pl.ds(w0, wlen)], wbuf.at[slot, p, :, pl.ds(0, wlen)], sem_in.at[slot, p])
                cp.start()
                started.append((cp, wlen, col - w0 - lo, lo, hi))
            return started

        pend, writes = fetch(0, 0), [None, None]
        for q, (t, b, _) in enumerate(steps):
            slot = q % 2
            nxt = fetch(q + 1, 1 - slot) if q + 1 < len(steps) else []
            acc = [jnp.zeros((rows, LANES), f32) for _ in range(ow // LANES)]
            for p, (cp, wlen, shift, lo, hi) in enumerate(pend):
                cp.wait()
                for k in range(ow // LANES):
                    c_lo, c_hi = max(lo, k * LANES), min(hi, (k + 1) * LANES)
                    if c_lo >= c_hi:
                        continue
                    wb = (c_lo + shift) // LANES
                    alen = min(2 * LANES, wlen - wb * LANES)
                    a = wbuf[slot, p, :, wb * LANES:wb * LANES + alen]
                    r, c = _rows((alen, LANES)), _cols((alen, LANES))
                    sh = shift + (k - wb) * LANES
                    sel = (r == c + sh) & (c >= c_lo - k * LANES) & (c < c_hi - k * LANES)
                    acc[k] = acc[k] + jnp.dot(a, sel.astype(bf16), preferred_element_type=f32)
            if writes[slot] is not None:
                writes[slot].wait()
            for k in range(ow // LANES):
                obuf[slot, :, k * LANES:(k + 1) * LANES] = acc[k].astype(bf16)
            writes[slot] = pltpu.make_async_copy(obuf.at[slot], out_ref.at[t, :, pl.ds(b * ow, ow)], sem_out.at[slot])
            writes[slot].start()
            pend = nxt
        for wr in writes:
            if wr is not None:
                wr.wait()

    operands, io_alias = [src], {}
    if out is not None:
        operands.append(out)
        io_alias = {1: 0}
    return pl.pallas_call(
        body, name=name, in_specs=[ANY] * len(operands), out_specs=ANY,
        out_shape=jax.ShapeDtypeStruct(out_shape, bf16), input_output_aliases=io_alias,
        scratch_shapes=[pltpu.VMEM((2, k_max, rows, win), bf16), pltpu.VMEM((2, rows, ow), bf16),
                        pltpu.SemaphoreType.DMA((2, k_max)), pltpu.SemaphoreType.DMA((2,))],
        compiler_params=pltpu.CompilerParams(vmem_limit_bytes=VMEM_LIMIT),
    )(*operands)


def _steps_to_kernel_cols(layer):
    steps = []
    for o in range(PW // REGROUP_W):
        pieces = []
        for pos, s, n in _kernel_pieces():
            lo, hi = max(pos, o * REGROUP_W), min(pos + n, (o + 1) * REGROUP_W)
            if lo < hi:
                ref_col = s + lo - pos
                pieces.append(((ref_col // SHARD_W) * N_LAYERS + layer, ref_col % SHARD_W, lo - o * REGROUP_W, hi - o * REGROUP_W))
        steps.append((0, o, pieces))
    return steps


def _steps_to_shards(layer):
    steps = []
    for j in range(4):
        for b in range(SHARD_PAD // REGROUP_W):
            start, stop = j * SHARD_W + b * REGROUP_W, min(j * SHARD_W + (b + 1) * REGROUP_W, (j + 1) * SHARD_W)
            pieces = []
            for pos, s, n in _kernel_pieces():
                lo, hi = max(s, start), min(s + n, stop)
                if lo < hi:
                    pieces.append((0, pos + lo - s, lo - start, hi - start))
            steps.append((layer * 4 + j, b, pieces))
    return steps


def _gather_weights(w):
    w_in = jnp.pad(w["w_in"].astype(bf16), ((0, 0), (0, 0), (0, SHARD_PAD - SHARD_W)))
    arrs = [w_in, w["w_branch"].reshape(N_LAYERS, N_BRANCH * BW, -1).astype(bf16),
            w["w_out"].astype(bf16), w["s5_w_glu"].astype(bf16), w["m2_conv_w"], w["sc_conv_w"], w["merge_b"]]
    got = gather_chips("ag_weights", arrs)
    cols = lambda t: jnp.transpose(t, (1, 0, 2)).reshape(t.shape[1], -1)
    layers = []
    for i in range(N_LAYERS):
        p = {k: w[k][i] for k in REPLICATED}
        p["w_in"] = regroup_cols(f"w_in_cols{i}", got[0].reshape(4 * N_LAYERS, D_MODEL, SHARD_PAD), _steps_to_kernel_cols(i),
                                 (1, D_MODEL, PW))[0]
        p["w_branch"] = cols(got[1][:, i])
        p["w_out"] = got[2][:, i].reshape(D_MODEL, D_MODEL)
        p["s5_w_glu"] = got[3][:, i].reshape(BW, BW)
        p["m2_conv_w"], p["sc_conv_w"], p["merge_b"] = cols(got[4][:, i]), cols(got[5][:, i]), cols(got[6][:, i])
        layers.append(p)
    return layers


def _reduce_grads(grads, d_final, w):
    to_chips = lambda t: jnp.transpose(t.reshape(t.shape[0], 4, -1), (1, 0, 2))
    stack = lambda f: jnp.stack([f(g) for g in grads])
    dw_in = None
    for i, g in enumerate(grads):
        dw_in = regroup_cols(f"w_in_shards{i}", g["w_in"][None], _steps_to_shards(i), (4 * N_LAYERS, D_MODEL, SHARD_PAD), out=dw_in)
    parts = [dw_in.reshape(N_LAYERS, 4, D_MODEL, SHARD_PAD),
             stack(lambda g: to_chips(g["w_branch"].reshape(N_BRANCH * BW, D_MODEL)).astype(bf16)),
             stack(lambda g: g["w_out"].reshape(4, D_MODEL // 4, D_MODEL).astype(bf16)),
             stack(lambda g: g["s5_w_glu"].reshape(4, BW // 4, BW).astype(bf16))]
    rep = jnp.concatenate([stack(lambda g: g[k]).reshape(-1) for k in REPLICATED] + [d_final.reshape(-1)])
    quarter = -(-rep.size // (4 * 2 * SUBL * PACK_W)) * (2 * SUBL * PACK_W)
    rep = jnp.pad(rep, (0, 4 * quarter - rep.size))
    small = []
    for j in range(4):
        sharded = [stack(lambda g: to_chips(g[k])[j]) for k in SHARDED_SMALL]
        small.append(_pack(sharded, 2, SUBL))
    parts.append(jnp.stack(small, axis=1))
    parts.append(jnp.transpose(rep.reshape(4, 2, quarter // (2 * PACK_W), PACK_W), (1, 0, 2, 3)))
    red, rep_all = reduce_to_shards(parts)
    out = {"w_in": red[0][:, :, :SHARD_W], "w_branch": red[1].reshape(w["w_branch"].shape), "w_out": red[2], "s5_w_glu": red[3]}
    out.update(zip(SHARDED_SMALL, _unpack(red[4].reshape(-1), [w[k].shape for k in SHARDED_SMALL])))
    names = REPLICATED + ("final_norm_w",)
    out.update(zip(names, _unpack(rep_all.reshape(-1), [w[k].shape for k in names])))
    return out


def _update(w, g, m, v):
    d, nm, nv = {}, {}, {}
    flat2 = lambda a: a.reshape(-1, a.shape[-1])
    cols_major = lambda a: jnp.transpose(a, (2, 0, 1))
    res = adamw("adamw_w_in", *[cols_major(t["w_in"]) for t in (w, g, m, v)], tr=SHARD_W // 14)
    d["w_in"], nm["w_in"], nv["w_in"] = (jnp.transpose(r, (1, 2, 0)) for r in res)
    for k in SHARDED_BIG[1:]:
        res = adamw("adamw_" + k, *[flat2(t[k]) for t in (w, g, m, v)])
        d[k], nm[k], nv[k] = (r.reshape(w[k].shape) for r in res)
    for k in ("s5_b_re", "s5_b_im"):
        res = adamw("adamw_" + k, *[flat2(t[k]) for t in (w, g, m, v)])
        d[k], nm[k], nv[k] = (r.reshape(w[k].shape) for r in res)
    rest = [k for k in WEIGHTS if k not in SHARDED_BIG + ("s5_b_re", "s5_b_im")]
    two_d = lambda a: a.reshape(1, -1) if a.ndim == 1 else a
    res = adamw_many("adamw_rest", *[[two_d(t[k]) for k in rest] for t in (w, g, m, v)])
    for tgt, rs in zip((d, nm, nv), res):
        tgt.update({k: r.reshape(w[k].shape) for k, r in zip(rest, rs)})
    return d, nm, nv


def kernel(x, norm_w, w_in, s5_lambda_re, s5_lambda_im, s5_b_re, s5_b_im, s5_c_re, s5_c_im, s5_d, s5_log_step, s5_w_glu, sgu_ln_w, sgu_ln_b, sgu_w, sgu_b, m2_conv_w, m2_conv_b, m2_dt_bias, m2_a_log, m2_d, m2_norm_w, sc_conv_w, merge_b, w_branch, w_out, final_norm_w, loss_target, m_norm_w, m_w_in, m_s5_lambda_re, m_s5_lambda_im, m_s5_b_re, m_s5_b_im, m_s5_c_re, m_s5_c_im, m_s5_d, m_s5_log_step, m_s5_w_glu, m_sgu_ln_w, m_sgu_ln_b, m_sgu_w, m_sgu_b, m_m2_conv_w, m_m2_conv_b, m_m2_dt_bias, m_m2_a_log, m_m2_d, m_m2_norm_w, m_sc_conv_w, m_merge_b, m_w_branch, m_w_out, m_final_norm_w, v_norm_w, v_w_in, v_s5_lambda_re, v_s5_lambda_im, v_s5_b_re, v_s5_b_im, v_s5_c_re, v_s5_c_im, v_s5_d, v_s5_log_step, v_s5_w_glu, v_sgu_ln_w, v_sgu_ln_b, v_sgu_w, v_sgu_b, v_m2_conv_w, v_m2_conv_b, v_m2_dt_bias, v_m2_a_log, v_m2_d, v_m2_norm_w, v_sc_conv_w, v_merge_b, v_w_branch, v_w_out, v_final_norm_w):
    given = dict(locals())
    w = {k: given[k] for k in WEIGHTS}
    m = {k: given["m_" + k] for k in WEIGHTS}
    v = {k: given["v_" + k] for k in WEIGHTS}
    layers = _gather_weights(w)
    loss, dx, grads, d_final = local_step(x[0], loss_target[0], layers, final_norm_w)
    loss = lax.psum(loss, ("x", "y", "c"))
    g = _reduce_grads(grads, d_final, w)
    d, nm, nv = _update(w, g, m, v)
    return (loss, dx[None], *[g[k] for k in WEIGHTS], *[d[k] for k in WEIGHTS],
            *[nm[k] for k in WEIGHTS], *[nv[k] for k in WEIGHTS])
```

```python
import functools
from typing import Any, Callable, NamedTuple

import numpy as np
import jax
import jax.numpy as jnp
from jax import lax
from jax.experimental import pallas as pl
from jax.experimental.pallas import tpu as pltpu

f32 = jnp.float32
bf16 = jnp.bfloat16

D_MODEL = 1024
BW = 512
N_BRANCH = 4
EPS = 1e-6
S5_GROUPS, S5_P, S5_N = 32, 16, 64
S5_NS = S5_GROUPS * S5_N
CHUNK = 128
M2_HEADS, M2_HEAD_DIM, M2_GROUPS, M2_STATE = 8, 64, 2, 128
IN_DIM = 10248
PW = 10368
PW_MAIN = 10240
LANES = 128
VMEM_LIMIT = 60 * 1024 * 1024

ADAM_LR, ADAM_B1, ADAM_B2, ADAM_EPS, ADAM_WD, ADAM_STEP = 0.001, 0.9, 0.999, 1e-08, 0.01, 10

C_MERGE = 0
C_SC = 4096
C_SGU = 6144
C_S5G = 7680
C_S5U = 8192
C_M2Z = 8704
C_XBC = 9216
C_DT = 10240


def _col_segments():
    segs = [(6152, 4096)]
    for j in range(4):
        segs += [(4104 + 128 * j, 128), (4616 + 128 * j, 128), (5128 + 128 * j, 128), (5640 + 128 * j, 128)]
    segs += [(1024, 1536), (512, 512), (0, 512), (2560, 512), (3072, 1024), (4096, 8)]
    return segs


NN = ((1,), (0,))
NT = ((1,), (1,))
TN = ((0,), (0,))


def _bd(a, b, dims):
    return lax.dot_general(a.astype(bf16), b.astype(bf16), (dims, ((), ())), preferred_element_type=f32)


def _hd(a, b, dims):
    return lax.dot_general(a, b, (dims, ((), ())), precision=lax.Precision.HIGHEST, preferred_element_type=f32)


def _make_dots(raw):
    @jax.custom_vjp
    def nn(a, b):
        return raw(a, b, NN)
    nn.defvjp(lambda a, b: (raw(a, b, NN), (a, b)), lambda r, g: (raw(g, r[1], NT), raw(r[0], g, TN)))

    @jax.custom_vjp
    def nt(a, b):
        return raw(a, b, NT)
    nt.defvjp(lambda a, b: (raw(a, b, NT), (a, b)), lambda r, g: (raw(g, r[1], NN), raw(g, r[0], TN)))

    @jax.custom_vjp
    def tn(a, b):
        return raw(a, b, TN)
    tn.defvjp(lambda a, b: (raw(a, b, TN), (a, b)), lambda r, g: (raw(r[1], g, NT), raw(r[0], g, NN)))
    return nn, nt, tn


bdot, bdot_nt, bdot_tn = _make_dots(_bd)
hdot, hdot_nt, hdot_tn = _make_dots(_hd)


@jax.custom_vjp
def bdot_w(a, w, shadow):
    return _bd(a, w, NN)


bdot_w.defvjp(lambda a, w, s: (_bd(a, w, NN), (a, w)),
              lambda r, g: (_bd(g, r[1], NT), jnp.zeros_like(r[1]), _bd(r[0], g, TN)))


def _rows(shape):
    return lax.broadcasted_iota(jnp.int32, shape, 0)


def _cols(shape):
    return lax.broadcasted_iota(jnp.int32, shape, 1)


def _shift_down(x, s):
    return jnp.where(_rows(x.shape) < s, 0.0, pltpu.roll(x, s, 0))


def _shift_up(x, s):
    n = x.shape[0]
    return jnp.where(_rows(x.shape) >= n - s, 0.0, pltpu.roll(x, n - s, 0))


@functools.partial(jax.custom_vjp, nondiff_argnums=(1,))
def shift(x, s):
    return _shift_down(x, s) if s else x


shift.defvjp(lambda x, s: (shift(x, s), None), lambda s, _, g: (_shift_up(g, s) if s else g,))


def _row_of(w, k):
    return jnp.sum(jnp.where(_rows(w.shape) == k, w, 0.0), axis=0, keepdims=True)


def _lane_mask(width, lo, hi):
    c = _cols((1, width))
    return ((c >= lo) & (c < hi)).astype(f32)


def _expand(rows, width, per):
    return (_cols((rows, width)) // per == _rows((rows, width))).astype(f32)


class A(NamedTuple):
    arr: Any
    block: tuple
    imap: Callable
    shadow: bool = False


class O(NamedTuple):
    shape: tuple
    dtype: Any
    block: tuple
    imap: Callable
    alias: Any = None


class R(NamedTuple):
    arg: int
    out: int
    off: Any = None
    acc: bool = False


def _cparams(n_grid):
    return pltpu.CompilerParams(dimension_semantics=("arbitrary",) * n_grid, vmem_limit_bytes=VMEM_LIMIT)


def _ispec(block, imap, n, rev):
    if rev:
        return pl.BlockSpec(block, lambda i: imap(n - 1 - i))
    return pl.BlockSpec(block, imap)


def _load(ref, a):
    v = ref[...]
    if a.shadow:
        return (v, jnp.zeros(v.shape, f32))
    return v.astype(f32)


def _save_spec(shape, n, rev):
    nd = len(shape)
    return _ispec((None,) + tuple(shape), lambda i: (i,) + (0,) * nd, n, rev)


def block_fwd(name, f, n, args, outs, carries=()):
    n_in, n_out, n_c = len(args), len(outs), len(carries)

    def body(*refs):
        ins, out_r = refs[:n_in], refs[n_in:n_in + n_out]
        saves, cs = refs[n_in + n_out:n_in + n_out + n_c], refs[n_in + n_out + n_c:]
        if n_c:
            @pl.when(pl.program_id(0) == 0)
            def _():
                for c in cs:
                    c[...] = jnp.zeros(c.shape, f32)
        vals = [_load(r, a) for r, a in zip(ins, args)]
        cv = [c[...] for c in cs]
        for s, v in zip(saves, cv):
            s[...] = v
        res = f(*vals, *cv)
        for r, v in zip(out_r, res[:n_out]):
            r[...] = v.astype(r.dtype)
        for c, v in zip(cs, res[n_out:]):
            c[...] = v

    out_shape = [jax.ShapeDtypeStruct(o.shape, o.dtype) for o in outs]
    out_specs = [pl.BlockSpec(o.block, o.imap) for o in outs]
    for shp in carries:
        out_shape.append(jax.ShapeDtypeStruct((n,) + tuple(shp), f32))
        out_specs.append(_save_spec(shp, n, False))
    return pl.pallas_call(
        body, name=name, grid=(n,),
        in_specs=[pl.BlockSpec(a.block, a.imap) for a in args],
        out_specs=out_specs, out_shape=out_shape,
        scratch_shapes=[pltpu.VMEM(tuple(shp), f32) for shp in carries],
        compiler_params=_cparams(1),
    )(*[a.arr for a in args])


def block_bwd(name, f, n, args, cots, gouts, routes, saved=(), rev=False):
    n_in, n_cot, n_c, n_go = len(args), len(cots), len(saved), len(gouts)
    diff = []
    for r in routes:
        if r.arg not in diff:
            diff.append(r.arg)
    aliases = [(k, o.alias) for k, o in enumerate(gouts) if o.alias is not None]

    def body(*refs):
        ins = refs[:n_in]
        cot_r = refs[n_in:n_in + n_cot]
        sav_r = refs[n_in + n_cot:n_in + n_cot + n_c]
        base = n_in + n_cot + n_c + len(aliases)
        go_r = refs[base:base + n_go]
        dcs = refs[base + n_go:]
        step = pl.program_id(0)
        if n_c:
            @pl.when(step == 0)
            def _():
                for d in dcs:
                    d[...] = jnp.zeros(d.shape, f32)
        vals = [_load(r, a) for r, a in zip(ins, args)]
        cv = [s[...] for s in sav_r]
        nd = len(diff)

        def g(*dv):
            full = list(vals)
            for idx, v in zip(diff, dv[:nd]):
                full[idx] = (vals[idx][0], v) if args[idx].shadow else v
            return tuple(f(*full, *dv[nd:]))

        primals = [vals[i][1] if args[i].shadow else vals[i] for i in diff] + cv
        _, vjp_fn = jax.vjp(g, *primals)
        ct = tuple([r[...].astype(f32) for r in cot_r] + [d[...] for d in dcs])
        grads = vjp_fn(ct)
        for r in routes:
            gr = grads[diff.index(r.arg)]
            ref = go_r[r.out]
            if r.acc:
                @pl.when(step == 0)
                def _(ref=ref, gr=gr):
                    ref[...] = gr.astype(ref.dtype)

                @pl.when(step > 0)
                def _(ref=ref, gr=gr):
                    ref[...] += gr.astype(ref.dtype)
            elif r.off is None:
                ref[...] = gr.astype(ref.dtype)
            else:
                ref[:, r.off:r.off + gr.shape[1]] = gr.astype(ref.dtype)
        for d, gr in zip(dcs, grads[nd:]):
            d[...] = gr

    in_specs = [_ispec(a.block, a.imap, n, rev) for a in list(args) + list(cots)]
    in_specs += [_save_spec(s.shape[1:], n, rev) for s in saved]
    in_specs += [pl.BlockSpec(memory_space=pl.ANY) for _ in aliases]
    operands = [a.arr for a in list(args) + list(cots)] + list(saved) + [arr for _, arr in aliases]
    io_alias = {n_in + n_cot + n_c + j: k for j, (k, _) in enumerate(aliases)}
    return pl.pallas_call(
        body, name=name, grid=(n,),
        in_specs=in_specs,
        out_specs=[_ispec(o.block, o.imap, n, rev) for o in gouts],
        out_shape=[jax.ShapeDtypeStruct(o.shape, o.dtype) for o in gouts],
        scratch_shapes=[pltpu.VMEM(tuple(s.shape[1:]), f32) for s in saved],
        input_output_aliases=io_alias,
        compiler_params=_cparams(1),
    )(*operands)


class Op(NamedTuple):
    arr: Any
    row: int = 0
    col: int = 0


def mm(name, mode, pairs, m, n, tm, tn, out_dtype=f32, add=None, out=None, out_col=0, out_width=None):
    tm, tn = min(tm, m), min(tn, n)
    assert m % tm == 0 and n % tn == 0
    in_specs, operands = [], []
    for a, b, k, _ in pairs:
        if mode == TN:
            assert a.row % k == 0 and a.col % tm == 0
            in_specs.append(pl.BlockSpec((k, tm), lambda j, i, a=a, k=k: (a.row // k, i + a.col // tm)))
        else:
            assert a.col % k == 0 and a.row % tm == 0
            in_specs.append(pl.BlockSpec((tm, k), lambda j, i, a=a, k=k: (i + a.row // tm, a.col // k)))
        if mode == NT:
            assert b.col % k == 0 and b.row % tn == 0
            in_specs.append(pl.BlockSpec((tn, k), lambda j, i, b=b, k=k: (j + b.row // tn, b.col // k)))
        else:
            assert b.row % k == 0 and b.col % tn == 0
            in_specs.append(pl.BlockSpec((k, tn), lambda j, i, b=b, k=k: (b.row // k, j + b.col // tn)))
        operands += [a.arr, b.arr]
    n_p = len(pairs)
    if add is not None:
        assert add.col % tn == 0
        in_specs.append(pl.BlockSpec((tm, tn), lambda j, i: (i, j + add.col // tn)))
        operands.append(add.arr)
    io_alias = {}
    if out is not None:
        assert out_col % tn == 0
        in_specs.append(pl.BlockSpec(memory_space=pl.ANY))
        operands.append(out)
        io_alias = {len(operands) - 1: 0}
        out_shape = jax.ShapeDtypeStruct(out.shape, out.dtype)
    else:
        out_shape = jax.ShapeDtypeStruct((m, out_width or n), out_dtype)
    signs = [p[3] for p in pairs]

    def body(*refs):
        o = refs[-1]
        acc = None
        for p in range(n_p):
            t = _bd(refs[2 * p][...], refs[2 * p + 1][...], mode)
            t = t if signs[p] > 0 else -t
            acc = t if acc is None else acc + t
        if add is not None:
            acc = acc + refs[2 * n_p][...].astype(f32)
        o[...] = acc.astype(o.dtype)

    return pl.pallas_call(
        body, name=name, grid=(n // tn, m // tm),
        in_specs=in_specs,
        out_specs=pl.BlockSpec((tm, tn), lambda j, i: (i, j + out_col // tn)),
        out_shape=out_shape, input_output_aliases=io_alias,
        compiler_params=_cparams(2),
    )(*operands)


SCAN_LANES = 512
SUBL = 8


def _cmul(p, q):
    return (p[0] * q[0] - p[1] * q[1], p[0] * q[1] + p[1] * q[0])


def _powers(a):
    a2 = _cmul(a, a)
    a4 = _cmul(a2, a2)
    a6 = _cmul(a4, a2)
    return [a, a2, _cmul(a2, a), a4, _cmul(a4, a), a6, _cmul(a6, a), _cmul(a4, a4)]


def _table(pw, order, w):
    row = _rows((SUBL, w))
    re = sum(jnp.where(row == t, pw[k][0], 0.0) for t, k in enumerate(order))
    im = sum(jnp.where(row == t, pw[k][1], 0.0) for t, k in enumerate(order))
    return re, im


def s5_scan_fwd(bu_re, bu_im, a_re, a_im):
    seq, ns = bu_re.shape
    w, nb = SCAN_LANES, ns // SCAN_LANES

    def body(b_re, b_im, ar, ai, s_re, s_im):
        a = (ar[...], ai[...])
        pw = _powers(a)
        tab = _table(pw, list(range(SUBL)), w)
        row = _rows((SUBL, w))

        def step(i, carry):
            t0 = pl.multiple_of(i * SUBL, SUBL)
            x = (b_re[pl.ds(t0, SUBL), :], b_im[pl.ds(t0, SUBL), :])
            for d, k in ((1, 0), (2, 1), (4, 3)):
                sh = (jnp.where(row < d, 0.0, pltpu.roll(x[0], d, 0)), jnp.where(row < d, 0.0, pltpu.roll(x[1], d, 0)))
                t = _cmul(pw[k], sh)
                x = (x[0] + t[0], x[1] + t[1])
            t = _cmul(tab, carry)
            x = (x[0] + t[0], x[1] + t[1])
            s_re[pl.ds(t0, SUBL), :] = x[0]
            s_im[pl.ds(t0, SUBL), :] = x[1]
            return (x[0][SUBL - 1:, :], x[1][SUBL - 1:, :])

        z = jnp.zeros((1, w), f32)
        lax.fori_loop(0, seq // SUBL, step, (z, z), unroll=2)

    strip = pl.BlockSpec((seq, w), lambda j: (0, j))
    lane = pl.BlockSpec((1, w), lambda j: (0, j))
    return pl.pallas_call(
        body, name="s5_scan_fwd", grid=(nb,),
        in_specs=[strip, strip, lane, lane],
        out_specs=[strip, strip],
        out_shape=[jax.ShapeDtypeStruct((seq, ns), f32)] * 2,
        compiler_params=_cparams(1),
    )(bu_re, bu_im, a_re, a_im)


def s5_scan_bwd(ds_re, ds_im, s_re, s_im, a_re, a_im):
    seq, ns = ds_re.shape
    w, nb = SCAN_LANES, ns // SCAN_LANES
    nblk = seq // SUBL

    def body(g_re, g_im, sr, si, ar, ai, l_re, l_im, da_re, da_im):
        a = (ar[...], -ai[...])
        pw = _powers(a)
        tab = _table(pw, [SUBL - 1 - t for t in range(SUBL)], w)
        row = _rows((SUBL, w))

        def step(kk, carry):
            c_re, c_im, acc_re, acc_im = carry
            i = nblk - 1 - kk
            t0 = pl.multiple_of(i * SUBL, SUBL)
            x = (g_re[pl.ds(t0, SUBL), :], g_im[pl.ds(t0, SUBL), :])
            for d, k in ((1, 0), (2, 1), (4, 3)):
                sh = (jnp.where(row >= SUBL - d, 0.0, pltpu.roll(x[0], SUBL - d, 0)),
                      jnp.where(row >= SUBL - d, 0.0, pltpu.roll(x[1], SUBL - d, 0)))
                t = _cmul(pw[k], sh)
                x = (x[0] + t[0], x[1] + t[1])
            t = _cmul(tab, (c_re, c_im))
            x = (x[0] + t[0], x[1] + t[1])
            l_re[pl.ds(t0, SUBL), :] = x[0]
            l_im[pl.ds(t0, SUBL), :] = x[1]
            tp = jnp.maximum(t0 - 1, 0)
            live = (i > 0).astype(f32)
            p_re = sr[pl.ds(tp, 1), :] * live
            p_im = si[pl.ds(tp, 1), :] * live
            sp_re = jnp.where(row == 0, p_re, pltpu.roll(sr[pl.ds(t0, SUBL), :], 1, 0))
            sp_im = jnp.where(row == 0, p_im, pltpu.roll(si[pl.ds(t0, SUBL), :], 1, 0))
            acc_re = acc_re + x[0] * sp_re + x[1] * sp_im
            acc_im = acc_im + x[1] * sp_re - x[0] * sp_im
            return (x[0][:1, :], x[1][:1, :], acc_re, acc_im)

        z1 = jnp.zeros((1, w), f32)
        z8 = jnp.zeros((SUBL, w), f32)
        _, _, acc_re, acc_im = lax.fori_loop(0, nblk, step, (z1, z1, z8, z8), unroll=2)
        da_re[...] = jnp.sum(acc_re, axis=0, keepdims=True)
        da_im[...] = jnp.sum(acc_im, axis=0, keepdims=True)

    strip = pl.BlockSpec((seq, w), lambda j: (0, j))
    lane = pl.BlockSpec((1, w), lambda j: (0, j))
    return pl.pallas_call(
        body, name="s5_scan_bwd", grid=(nb,),
        in_specs=[strip, strip, strip, strip, lane, lane],
        out_specs=[strip, strip, lane, lane],
        out_shape=[jax.ShapeDtypeStruct((seq, ns), f32)] * 2 + [jax.ShapeDtypeStruct((1, ns), f32)] * 2,
        compiler_params=_cparams(1),
    )(ds_re, ds_im, s_re, s_im, a_re, a_im)


def _rms(x, w):
    return x * lax.rsqrt(jnp.mean(x * x, axis=-1, keepdims=True) + EPS) * w


def f_rms(x, w):
    return (_rms(x, w),)


def f_rms_res(x, w):
    return (_rms(x, w), x)


def f_s5_prep(lam_re, lam_im, log_step, b_re, b_im):
    e = _expand(log_step.shape[1], S5_NS, S5_N)
    step = hdot(jnp.exp(log_step), e)
    mag = jnp.exp(lam_re * step)
    ab_re, ab_im = mag * jnp.cos(lam_im * step), mag * jnp.sin(lam_im * step)
    den = lam_re * lam_re + lam_im * lam_im
    nr = ab_re - 1.0
    coef_re = (nr * lam_re + ab_im * lam_im) / den
    coef_im = (ab_im * lam_re - nr * lam_im) / den
    bb_re, bb_im = coef_re * b_re - coef_im * b_im, coef_re * b_im + coef_im * b_re
    sel = (_rows((BW, S5_P)) % S5_P == _cols((BW, S5_P))).astype(f32)
    blk = _rows((BW, S5_NS)) // S5_P == _cols((BW, S5_NS)) // S5_N
    rows_bd = lambda t: jnp.where(blk, hdot(sel, t), 0.0)
    return (ab_re, ab_im, rows_bd(bb_re), rows_bd(bb_im))


def f_s5_c(c_re, c_im):
    sel_t = (_cols((S5_P, BW)) % S5_P == _rows((S5_P, BW))).astype(f32)
    blk_t = _rows((S5_NS, BW)) // S5_N == _cols((S5_NS, BW)) // S5_P
    cols_bd = lambda t: jnp.where(blk_t, hdot_tn(t, sel_t), 0.0)
    return (cols_bd(c_re), cols_bd(c_im))


def f_s5_act(y_lin, u, gate, d, w_glu):
    y = jax.nn.gelu(y_lin + d * u)
    y = y * jax.nn.sigmoid(bdot_w(y, *w_glu))
    return (y * jax.nn.silu(gate),)


def f_sgu(u, v, gate, ln_w, ln_b, *rest):
    w_s, b_pad = rest[:8], rest[8]
    t = u.shape[0]
    u32, v32 = jax.nn.gelu(u), jax.nn.gelu(v)
    mu = jnp.mean(v32, axis=-1, keepdims=True)
    var = jnp.mean(jnp.square(v32 - mu), axis=-1, keepdims=True)
    vn = (v32 - mu) * lax.rsqrt(var + EPS) * ln_w + ln_b
    tri = _rows((t, t)) >= _cols((t, t))
    s = hdot_tn(b_pad, _expand(LANES, BW, BW // 8))
    for h in range(8):
        s = s + bdot(jnp.where(tri, w_s[h], 0.0), vn) * _lane_mask(BW, 64 * h, 64 * h + 64)
    return (u32 * s * jax.nn.silu(gate),)


def f_m2_conv(x, w, b):
    return (sum(_row_of(w, k) * shift(x, 3 - k) for k in range(4)) + b,)


def f_sc(bg, cg, h, gate, w):
    z = cg * h
    conv = sum(_row_of(w, k) * shift(z, 2 - k) for k in range(3))
    return (bg * conv * jax.nn.silu(gate),)


def f_m2(z, xc, b0, b1, c0, c1, dt_raw, dt_bias, a_log, d_par, norm_w, st):
    q = z.shape[0]
    x = jax.nn.silu(xc)
    bm, cm = (jax.nn.silu(b0), jax.nn.silu(b1)), (jax.nn.silu(c0), jax.nn.silu(c1))
    dt = jax.nn.softplus(dt_raw + dt_bias)
    da = dt * (-jnp.exp(a_log))
    tri = _rows((q, q)) >= _cols((q, q))
    acs = hdot(tri.astype(f32), da)
    e = _expand(LANES, BW, M2_HEAD_DIM)
    dt_f, acs_f = hdot(dt, e), hdot(acs, e)
    last = _rows((q, BW)) == q - 1
    alast_f = jnp.sum(jnp.where(last, acs_f, 0.0), axis=0, keepdims=True)
    xdt = x * dt_f
    xdec = xdt * jnp.exp(alast_f - acs_f)
    acs_t = acs.T
    st_new = st * jnp.exp(alast_f)
    y_diag, y_off = 0.0, 0.0
    for g in range(M2_GROUPS):
        gm = _lane_mask(BW, 256 * g, 256 * g + 256)
        cb = bdot_nt(cm[g], bm[g])
        st_new = st_new + bdot_tn(bm[g], xdec * gm)
        y_off = y_off + bdot(cm[g], st) * gm
        for hh in range(M2_HEADS // M2_GROUPS):
            h = g * (M2_HEADS // M2_GROUPS) + hh
            col = jnp.sum(jnp.where(_cols((q, LANES)) == h, acs, 0.0), axis=1, keepdims=True)
            row = jnp.sum(jnp.where(_rows((LANES, q)) == h, acs_t, 0.0), axis=0, keepdims=True)
            decay = jnp.exp(jnp.where(tri, col - row, -1e30))
            y_diag = y_diag + bdot(cb * decay, xdt) * _lane_mask(BW, 64 * h, 64 * h + 64)
    d_f = sum(jnp.sum(jnp.where(_cols((1, LANES)) == h, d_par, 0.0), axis=1, keepdims=True)
              * _lane_mask(BW, 64 * h, 64 * h + 64) for h in range(M2_HEADS))
    y = y_diag + y_off * jnp.exp(acs_f) + d_f * x
    y = y * jax.nn.silu(z)
    return (_rms(y, norm_w), st_new)


def f_branch_mix(*v):
    nb = len(v) // 4
    ys, lg, wb, mb = v[0:nb], v[nb:2 * nb], v[2 * nb:3 * nb], v[3 * nb:]
    return (sum(jax.nn.sigmoid(lg[k] + mb[k]) * bdot_w(ys[k], *wb[k]) for k in range(nb)),)


def _param(arr, shadow=False):
    return A(arr, tuple(arr.shape), lambda i, nd=arr.ndim: (0,) * nd, shadow)


def _tb(arr, t, width, colblk):
    return A(arr, (t, width), lambda i: (i, colblk))


def _strip(arr, seq, colblk0, stride=1):
    return A(arr, (seq, LANES), lambda j: (0, colblk0 + stride * j))


def _s5_prep_args(p):
    lam_re = p["s5_lambda_re"].reshape(1, S5_NS)
    lam_im = p["s5_lambda_im"].reshape(1, S5_NS)
    log_step = jnp.pad(p["s5_log_step"].reshape(1, S5_GROUPS), ((0, 0), (0, LANES - S5_GROUPS)))
    b_lanes = lambda b: jnp.transpose(b, (2, 0, 1)).reshape(S5_P, S5_NS)
    return [_param(v) for v in (lam_re, lam_im, log_step, b_lanes(p["s5_b_re"]), b_lanes(p["s5_b_im"]))]


def _s5_c_args(p):
    c_lanes = lambda c: jnp.transpose(c, (1, 0, 2)).reshape(S5_P, S5_NS)
    return [_param(c_lanes(p["s5_c_re"])), _param(c_lanes(p["s5_c_im"]))]


def layer_fwd(x, p):
    seq = x.shape[0]
    nt = seq // CHUNK
    t2 = 256
    sv = {}
    t4 = min(512, seq)
    hb = block_fwd("rms_in", f_rms, seq // t4, [_tb(x, t4, D_MODEL, 0), _param(p["norm_w"].reshape(1, D_MODEL))],
                   [O((seq, D_MODEL), bf16, (t4, D_MODEL), lambda i: (i, 0))])[0]
    proj = mm("mm_in", NN, [(Op(hb), Op(p["w_in"]), D_MODEL, 1)], seq, PW, 2048, 1152)
    whole = lambda shape, dt: O(shape, dt, shape, lambda i: (0, 0))
    ab_re, ab_im, bb_re, bb_im = block_fwd("s5_prep", f_s5_prep, 1, _s5_prep_args(p),
                                           [whole((1, S5_NS), f32)] * 2 + [whole((BW, S5_NS), bf16)] * 2)
    cc_re, cc_im = block_fwd("s5_c", f_s5_c, 1, _s5_c_args(p), [whole((S5_NS, BW), bf16)] * 2)
    bu_re = mm("mm_bu_re", NN, [(Op(proj, 0, C_S5U), Op(bb_re), BW, 1)], seq, S5_NS, 1024, 1024)
    bu_im = mm("mm_bu_im", NN, [(Op(proj, 0, C_S5U), Op(bb_im), BW, 1)], seq, S5_NS, 1024, 1024)
    s_re, s_im = s5_scan_fwd(bu_re, bu_im, ab_re, ab_im)
    y_lin = mm("mm_s5y", NN, [(Op(s_re), Op(cc_re), S5_NS, 1), (Op(s_im), Op(cc_im), S5_NS, -1)], seq, BW, 512, BW)
    s5_act_args = [_tb(y_lin, t4, BW, 0), _tb(proj, t4, BW, C_S5U // BW), _tb(proj, t4, BW, C_S5G // BW),
                   _param(p["s5_d"].reshape(1, BW)), _param(p["s5_w_glu"], True)]
    out_bw =O((seq, BW), f32, (CHUNK, BW), lambda i: (i, 0))
    y_a = block_fwd("s5_act", f_s5_act, seq // t4, s5_act_args, [O((seq, BW), f32, (t4, BW), lambda i: (i, 0))])[0]
    y_b = block_fwd("sgu", f_sgu, nt, _sgu_args(proj, p), [out_bw])[0]
    xc = block_fwd("m2_conv", f_m2_conv, 8, _m2_conv_args(proj, p, seq),
                   [O((seq, 2 * BW), f32, (seq, LANES), lambda j: (0, j))])[0]
    y_c, st_saved = block_fwd("m2_ssd", f_m2, nt, _m2_args(proj, xc, p), [out_bw], carries=[(M2_STATE, BW)])
    y_d = block_fwd("sc", f_sc, 4, _sc_args(proj, p, seq), [O((seq, BW), f32, (seq, LANES), lambda j: (0, j))])[0]
    ys = [y_a, y_b, y_c, y_d]
    merged = block_fwd("branch_mix", f_branch_mix, seq // t2, _mix_args(ys, proj, p, t2),
                       [O((seq, D_MODEL), bf16, (t2, D_MODEL), lambda i: (i, 0))])[0]
    x_new = mm("mm_out", NN, [(Op(merged), Op(p["w_out"]), D_MODEL, 1)], seq, D_MODEL, 1024, D_MODEL, add=Op(x))
    sv.update(x=x, hb=hb, proj=proj, ab=(ab_re, ab_im), bb=(bb_re, bb_im), cc=(cc_re, cc_im), s=(s_re, s_im), y_lin=y_lin,
              xc=xc, st=st_saved, ys=ys, merged=merged)
    return x_new, sv


def _sgu_args(proj, p):
    c0 = C_SGU // BW
    args = [_tb(proj, CHUNK, BW, c0), _tb(proj, CHUNK, BW, c0 + 1), _tb(proj, CHUNK, BW, c0 + 2),
            _param(p["sgu_ln_w"].reshape(1, BW)), _param(p["sgu_ln_b"].reshape(1, BW))]
    args += [A(p["sgu_w"], (None, CHUNK, CHUNK), lambda i, h=h: (h, 0, 0)) for h in range(8)]
    args.append(_param(jnp.pad(p["sgu_b"], ((0, LANES - 8), (0, 0)))))
    return args


def _m2_conv_args(proj, p, seq):
    return [_strip(proj, seq, C_XBC // LANES), A(p["m2_conv_w"], (4, LANES), lambda j: (0, j)),
            A(p["m2_conv_b"].reshape(1, 2 * BW), (1, LANES), lambda j: (0, j))]


def _pad_lanes(v):
    return jnp.pad(v.reshape(1, -1), ((0, 0), (0, LANES - v.size)))


def _m2_args(proj, xc, p):
    args = [_tb(proj, CHUNK, BW, C_M2Z // BW), _tb(xc, CHUNK, BW, 0)]
    args += [_tb(xc, CHUNK, LANES, 4 + k) for k in range(4)]
    args.append(_tb(proj, CHUNK, LANES, C_DT // LANES))
    args += [_param(_pad_lanes(p["m2_dt_bias"])), _param(_pad_lanes(p["m2_a_log"])), _param(_pad_lanes(p["m2_d"])),
             _param(p["m2_norm_w"].reshape(1, BW))]
    return args


def _sc_args(proj, p, seq):
    c0 = C_SC // LANES
    return [_strip(proj, seq, c0 + k, 4) for k in range(4)] + [A(p["sc_conv_w"], (3, LANES), lambda j: (0, j))]


def _mix_args(ys, proj, p, t, ks=range(N_BRANCH)):
    args = [_tb(ys[k], t, BW, 0) for k in ks]
    args += [_tb(proj, t, D_MODEL, k) for k in ks]
    args += [A(p["w_branch"], (BW, D_MODEL), lambda i, k=k: (k, 0), True) for k in ks]
    mb = p["merge_b"].reshape(N_BRANCH, 1, D_MODEL)
    args += [A(mb, (None, 1, D_MODEL), lambda i, k=k: (k, 0, 0)) for k in ks]
    return args


def layer_bwd(d_out, p, sv):
    seq = d_out.shape[0]
    nt = seq // CHUNK
    t4 = min(512, seq)
    proj, ys = sv["proj"], sv["ys"]
    g = {}
    acc = lambda shape: O(tuple(shape), f32, tuple(shape), lambda i, nd=len(shape): (0,) * nd)
    d_merged = mm("mm_out_dx", NT, [(Op(d_out), Op(p["w_out"]), D_MODEL, 1)], seq, D_MODEL, 1024, D_MODEL, out_dtype=f32)
    g["w_out"] = mm("mm_out_dw", TN, [(Op(sv["merged"]), Op(d_out), seq, 1)], D_MODEL, D_MODEL, 512, D_MODEL, out_dtype=bf16)
    dys, dwb, dmb, dproj, tb = [], [], [], None, min(256, seq)
    for half in range(2):
        gouts = [O((seq, BW), f32, (tb, BW), lambda i: (i, 0)) for _ in range(2)]
        gouts.append(O((seq, PW_MAIN), bf16, (tb, 2 * D_MODEL), lambda i, half=half: (i, half), alias=dproj))
        gouts += [acc((BW, D_MODEL)) for _ in range(2)] + [acc((1, D_MODEL)) for _ in range(2)]
        routes = [R(k, k) for k in range(2)] + [R(2 + k, 2, k * D_MODEL) for k in range(2)]
        routes += [R(4 + k, 3 + k, acc=True) for k in range(2)] + [R(6 + k, 5 + k, acc=True) for k in range(2)]
        res = block_bwd(f"branch_mix_bwd{half}", f_branch_mix, seq // tb, _mix_args(ys, proj, p, tb, (2 * half, 2 * half + 1)),
                        [_tb(d_merged, tb, D_MODEL, 0)], gouts, routes)
        dys, dproj, dwb, dmb = dys + list(res[:2]), res[2], dwb + list(res[3:5]), dmb + list(res[5:7])
    g["w_branch"] = jnp.stack(dwb)
    g["merge_b"] = jnp.concatenate(dmb, axis=0)
    res = block_bwd("sc_bwd", f_sc, 4, _sc_args(proj, p, seq), [_strip(dys[3], seq, 0)],
                    [O((seq, PW_MAIN), bf16, (seq, 4 * LANES), lambda j: (0, C_SC // (4 * LANES) + j), alias=dproj),
                     O((3, BW), f32, (3, LANES), lambda j: (0, j))],
                    [R(k, 0, k * LANES) for k in range(4)] + [R(4, 1)])
    dproj, g["sc_conv_w"] = res
    res = block_bwd("m2_ssd_bwd", f_m2, nt, _m2_args(proj, sv["xc"], p), [_tb(dys[2], CHUNK, BW, 0)],
                    [O((seq, PW_MAIN), bf16, (CHUNK, BW), lambda i: (i, C_M2Z // BW), alias=dproj),
                     O((seq, 2 * BW), f32, (CHUNK, 2 * BW), lambda i: (i, 0)),
                     O((seq, LANES), bf16, (CHUNK, LANES), lambda i: (i, 0)),
                     acc((1, LANES)), acc((1, LANES)), acc((1, LANES)), acc((1, BW))],
                    [R(0, 0), R(1, 1, 0)] + [R(2 + k, 1, BW + k * LANES) for k in range(4)] + [R(6, 2)]
                    + [R(7, 3, acc=True), R(8, 4, acc=True), R(9, 5, acc=True), R(10, 6, acc=True)],
                    saved=[sv["st"]], rev=True)
    dproj, dxc, d_dt = res[0], res[1], res[2]
    g["m2_dt_bias"], g["m2_a_log"], g["m2_d"] = (r[0, :M2_HEADS] for r in res[3:6])
    g["m2_norm_w"] = res[6].reshape(BW)
    res = block_bwd("m2_conv_bwd", f_m2_conv, 8, _m2_conv_args(proj, p, seq), [_strip(dxc, seq, 0)],
                    [O((seq, PW_MAIN), bf16, (seq, LANES), lambda j: (0, C_XBC // LANES + j), alias=dproj),
                     O((4, 2 * BW), f32, (4, LANES), lambda j: (0, j)), O((1, 2 * BW), f32, (1, LANES), lambda j: (0, j))],
                    [R(0, 0), R(1, 1), R(2, 2)])
    dproj, g["m2_conv_w"], cb = res
    g["m2_conv_b"] = cb.reshape(2 * BW)
    res = block_bwd("sgu_bwd", f_sgu, nt, _sgu_args(proj, p), [_tb(dys[1], CHUNK, BW, 0)],
                    [O((seq, PW_MAIN), bf16, (CHUNK, 3 * BW), lambda i: (i, C_SGU // (3 * BW)), alias=dproj),
                     acc((1, BW)), acc((1, BW))] + [acc((CHUNK, CHUNK)) for _ in range(8)] + [acc((LANES, CHUNK))],
                    [R(0, 0, 0), R(1, 0, BW), R(2, 0, 2 * BW), R(3, 1, acc=True), R(4, 2, acc=True)]
                    + [R(5 + h, 3 + h, acc=True) for h in range(8)] + [R(13, 11, acc=True)])
    dproj = res[0]
    g["sgu_ln_w"], g["sgu_ln_b"] = res[1].reshape(BW), res[2].reshape(BW)
    g["sgu_w"] = jnp.stack(res[3:11])
    g["sgu_b"] = res[11][:8]
    y_lin, (s_re, s_im), (ab_re, ab_im) = sv["y_lin"], sv["s"], sv["ab"]
    s5_act_args = [_tb(y_lin, t4, BW, 0), _tb(proj, t4, BW, C_S5U // BW), _tb(proj, t4, BW, C_S5G // BW),
                   _param(p["s5_d"].reshape(1, BW)), _param(p["s5_w_glu"], True)]
    res = block_bwd("s5_act_bwd", f_s5_act, seq // t4, s5_act_args, [_tb(dys[0], t4, BW, 0)],
                    [O((seq, BW), bf16, (t4, BW), lambda i: (i, 0)), O((seq, BW), f32, (t4, BW), lambda i: (i, 0)),
                     O((seq, PW_MAIN), bf16, (t4, BW), lambda i: (i, C_S5G // BW), alias=dproj),
                     acc((1, BW)), acc((BW, BW))],
                    [R(0, 0), R(1, 1), R(2, 2), R(3, 3, acc=True), R(4, 4, acc=True)])
    dy_lin, du1, dproj = res[0], res[1], res[2]
    g["s5_d"] = res[3].reshape(S5_GROUPS, S5_P)
    g["s5_w_glu"] = res[4]
    (bb_re, bb_im), (cc_re, cc_im) = sv["bb"], sv["cc"]
    ds_re = mm("mm_s5y_dre", NT, [(Op(dy_lin), Op(cc_re), BW, 1)], seq, S5_NS, 1024, 1024)
    ds_im = mm("mm_s5y_dim", NT, [(Op(dy_lin), Op(cc_im), BW, -1)], seq, S5_NS, 1024, 1024)
    dc_re = mm("mm_s5y_dcre", TN, [(Op(s_re), Op(dy_lin), seq, 1)], S5_NS, BW, 512, BW)
    dc_im = mm("mm_s5y_dcim", TN, [(Op(s_im), Op(dy_lin), seq, -1)], S5_NS, BW, 512, BW)
    l_re, l_im, da_re, da_im = s5_scan_bwd(ds_re, ds_im, s_re, s_im, ab_re, ab_im)
    dproj = mm("mm_bu_dx", NT, [(Op(l_re), Op(bb_re), S5_NS, 1), (Op(l_im), Op(bb_im), S5_NS, 1)],
               seq, BW, 512, BW, add=Op(du1), out=dproj, out_col=C_S5U)
    dbb = [mm(f"mm_bu_dw{n}", TN, [(Op(proj, 0, C_S5U), Op(l), seq, 1)], BW, S5_NS, 512, 1024)
           for n, l in (("re", l_re), ("im", l_im))]
    gouts = [acc((1, S5_NS)), acc((1, S5_NS)), acc((1, LANES))] + [acc((S5_P, S5_NS))] * 2
    res = block_bwd("s5_prep_bwd", f_s5_prep, 1, _s5_prep_args(p),
                    [_param(v) for v in (da_re, da_im, dbb[0], dbb[1])], gouts, [R(k, k, acc=True) for k in range(5)])
    g["s5_lambda_re"], g["s5_lambda_im"] = res[0].reshape(S5_GROUPS, S5_N), res[1].reshape(S5_GROUPS, S5_N)
    g["s5_log_step"] = res[2][0, :S5_GROUPS]
    b_natural = lambda b: jnp.transpose(b.reshape(S5_P, S5_GROUPS, S5_N), (1, 2, 0))
    c_natural = lambda c: jnp.transpose(c.reshape(S5_P, S5_GROUPS, S5_N), (1, 0, 2))
    g["s5_b_re"], g["s5_b_im"] = b_natural(res[3]), b_natural(res[4])
    res = block_bwd("s5_c_bwd", f_s5_c, 1, _s5_c_args(p), [_param(dc_re), _param(dc_im)],
                    [acc((S5_P, S5_NS))] * 2, [R(0, 0, acc=True), R(1, 1, acc=True)])
    g["s5_c_re"], g["s5_c_im"] = c_natural(res[0]), c_natural(res[1])
    hb, w_in = sv["hb"], p["w_in"]
    dh = mm("mm_in_dx", NT, [(Op(dproj), Op(w_in), PW_MAIN, 1), (Op(d_dt), Op(w_in, 0, PW_MAIN), LANES, 1)],
            seq, D_MODEL, 512, 512)
    dw = mm("mm_in_dw", TN, [(Op(hb), Op(dproj), seq, 1)], D_MODEL, PW_MAIN, 1024, 2048, out_dtype=bf16, out_width=PW)
    g["w_in"] = mm("mm_in_dwdt", TN, [(Op(hb), Op(d_dt), seq, 1)], D_MODEL, LANES, 256, LANES, out=dw, out_col=C_DT)
    dx, dnw = block_bwd("rms_in_bwd", f_rms_res, seq // t4, [_tb(sv["x"], t4, D_MODEL, 0), _param(p["norm_w"].reshape(1, D_MODEL))],
                        [_tb(dh, t4, D_MODEL, 0), _tb(d_out, t4, D_MODEL, 0)],
                        [O((seq, D_MODEL), f32, (t4, D_MODEL), lambda i: (i, 0)), acc((1, D_MODEL))],
                        [R(0, 0), R(1, 1, acc=True)])
    g["norm_w"] = dnw.reshape(D_MODEL)
    return dx, g


def loss_head(x, w, target):
    seq = x.shape[0]
    t = min(512, seq)

    def body(x_ref, w_ref, t_ref, loss_ref, dx_ref, dw_ref):
        step = pl.program_id(0)

        def f(xv, wv):
            err = _rms(xv, wv) - t_ref[...]
            return 0.5 * jnp.sum(jnp.mean(err * err, axis=-1, keepdims=True), axis=0, keepdims=True)

        val, vjp_fn = jax.vjp(f, x_ref[...], w_ref[...])
        dx, dw = vjp_fn(jnp.ones((1, 1), f32))
        dx_ref[...] = dx

        @pl.when(step == 0)
        def _():
            loss_ref[...] = jnp.broadcast_to(val, loss_ref.shape)
            dw_ref[...] = dw

        @pl.when(step > 0)
        def _():
            loss_ref[...] += jnp.broadcast_to(val, loss_ref.shape)
            dw_ref[...] += dw

    blk = pl.BlockSpec((t, D_MODEL), lambda i: (i, 0))
    row = pl.BlockSpec((1, D_MODEL), lambda i: (0, 0))
    return pl.pallas_call(
        body, name="loss_head", grid=(seq // t,),
        in_specs=[blk, row, blk],
        out_specs=[pl.BlockSpec((1, LANES), lambda i: (0, 0)), blk, row],
        out_shape=[jax.ShapeDtypeStruct((1, LANES), f32), jax.ShapeDtypeStruct((seq, D_MODEL), f32),
                   jax.ShapeDtypeStruct((1, D_MODEL), f32)],
        compiler_params=_cparams(1),
    )(x, w.reshape(1, D_MODEL), target)


LAYER_KEYS = ("norm_w", "w_in", "s5_lambda_re", "s5_lambda_im", "s5_b_re", "s5_b_im", "s5_c_re", "s5_c_im", "s5_d",
              "s5_log_step", "s5_w_glu", "sgu_ln_w", "sgu_ln_b", "sgu_w", "sgu_b", "m2_conv_w", "m2_conv_b",
              "m2_dt_bias", "m2_a_log", "m2_d", "m2_norm_w", "sc_conv_w", "merge_b", "w_branch", "w_out")


def local_step(x, target, layers, final_norm_w):
    saved = []
    for p in layers:
        x, sv = layer_fwd(x, p)
        saved.append(sv)
    loss, dx, dfw = loss_head(x, final_norm_w, target)
    grads = []
    for p, sv in zip(reversed(layers), reversed(saved)):
        dx, g = layer_bwd(dx, p, sv)
        grads.append(g)
    return loss[0, 0], dx, grads[::-1], dfw.reshape(D_MODEL)


MESH = pl.DeviceIdType.MESH
ANY = pl.BlockSpec(memory_space=pl.ANY)


def _me():
    return lax.axis_index("x"), lax.axis_index("y"), lax.axis_index("c")


def _other_chips(x, y):
    return [(1 - x, y), (x, 1 - y), (1 - x, 1 - y)]


def _rcopy(src, dst, send, recv, dev):
    return pltpu.make_async_remote_copy(src_ref=src, dst_ref=dst, send_sem=send, recv_sem=recv,
                                        device_id=dev, device_id_type=MESH)


def _route_cut(rows, dtype):
    tile = 2 * SUBL * (4 // jnp.dtype(dtype).itemsize)
    return rows // 2 if rows % tile == 0 else rows


def _comm_call(name, body, arrs, out_shape, n_remote, aliases=None):
    n = len(arrs)
    return pl.pallas_call(
        body, name=name, in_specs=[ANY] * n, out_specs=[ANY] * len(out_shape), out_shape=out_shape,
        scratch_shapes=[pltpu.SemaphoreType.DMA((n, n_remote)), pltpu.SemaphoreType.DMA((n, n_remote))],
        input_output_aliases=aliases or {},
        compiler_params=pltpu.CompilerParams(has_side_effects=True),
    )(*arrs)


def gather_chips(name, arrs):
    n = len(arrs)
    cut = [_route_cut(a.shape[1], a.dtype) for a in arrs]

    def body(*refs):
        ins, outs = refs[:n], refs[n:2 * n]
        send, recv = refs[2 * n:]
        x, y, c = _me()
        jme, jx, jy, jd = 2 * x + y, 2 * (1 - x) + y, 2 * x + 1 - y, 2 * (1 - x) + 1 - y
        to_x, to_y, sib = (1 - x, y, c), (x, 1 - y, c), (x, y, 1 - c)

        def part(ref, a, hi):
            return ref.at[pl.ds(cut[a], ref.shape[0] - cut[a])] if hi else ref.at[pl.ds(0, cut[a])]

        def cp(a, k, ref, dev):
            return _rcopy(ref, ref, send.at[a, k], recv.at[a, k], dev)

        split = [a for a in range(n) if cut[a] < arrs[a].shape[1]]
        sent = [_rcopy(ins[a].at[c], outs[a].at[jme, c], send.at[a, k], recv.at[a, k], dev)
                for a in range(n) for k, dev in ((0, to_x), (1, to_y))]
        for s in sent:
            s.start()
        for a in range(n):
            blk = outs[a].at[jx, c]
            cp(a, 0, blk, to_x).wait_recv()
            sent += [cp(a, 2, part(blk, a, False), to_y), cp(a, 4, blk, sib)]
            sent[-2].start()
            sent[-1].start()
        for a in range(n):
            blk = outs[a].at[jy, c]
            cp(a, 1, blk, to_y).wait_recv()
            sent.append(cp(a, 5, blk, sib))
            sent[-1].start()
            if a in split:
                sent.append(cp(a, 3, part(blk, a, True), to_x))
                sent[-1].start()
        for a in range(n):
            lo = part(outs[a].at[jd, c], a, False)
            cp(a, 2, lo, to_y).wait_recv()
            sent.append(cp(a, 6, lo, sib))
            sent[-1].start()
        for a in split:
            hi = part(outs[a].at[jd, c], a, True)
            cp(a, 3, hi, to_x).wait_recv()
            sent.append(cp(a, 7, hi, sib))
            sent[-1].start()
        for a in range(n):
            cp(a, 4, outs[a].at[jx, 1 - c], sib).wait_recv()
            cp(a, 5, outs[a].at[jy, 1 - c], sib).wait_recv()
            cp(a, 6, part(outs[a].at[jd, 1 - c], a, False), sib).wait_recv()
        for a in split:
            cp(a, 7, part(outs[a].at[jd, 1 - c], a, True), sib).wait_recv()
        for s in sent:
            s.wait_send()

    out_shape = [jax.ShapeDtypeStruct((4,) + a.shape, a.dtype) for a in arrs]
    got = _comm_call(name, body, arrs, out_shape, 8)
    jme = 2 * lax.axis_index("x") + lax.axis_index("y")
    return [lax.dynamic_update_index_in_dim(g, a, jme, 0) for g, a in zip(got, arrs)]


def swap_halves(name, arrs):
    n = len(arrs)

    def body(*refs):
        ins, outs = refs[:n], refs[n:2 * n]
        send, recv = refs[2 * n:]
        x, y, c = _me()
        remote = [_rcopy(ins[a].at[1 - c], outs[a], send.at[a, 0], recv.at[a, 0], (x, y, 1 - c)) for a in range(n)]
        for cp in remote:
            cp.start()
        for cp in remote:
            cp.wait()

    return _comm_call(name, body, arrs, [jax.ShapeDtypeStruct(a.shape[1:], a.dtype) for a in arrs], 1)


def exchange_chips(name, arrs):
    n = len(arrs)

    def body(*refs):
        ins, outs = refs[:n], refs[n:2 * n]
        send, recv = refs[2 * n:]
        x, y, c = _me()
        remote = [_rcopy(ins[a].at[2 * cx + cy], outs[a].at[k], send.at[a, k], recv.at[a, k], (cx, cy, c))
                  for a in range(n) for k, (cx, cy) in enumerate(_other_chips(x, y))]
        for cp in remote:
            cp.start()
        for cp in remote:
            cp.wait()

    return _comm_call(name, body, arrs, [jax.ShapeDtypeStruct((3,) + a.shape[1:], a.dtype) for a in arrs], 3)


def gather_cores(name, arrs):
    n = len(arrs)
    r = n - 1

    def body(*refs):
        bufs, spread = refs[n:2 * n], refs[2 * n]
        send, recv = refs[2 * n + 1:]
        x, y, c = _me()
        jme, sib = 2 * x + y, (x, y, 1 - c)
        chips = _other_chips(x, y)
        sent = [_rcopy(bufs[a].at[c], bufs[a].at[c], send.at[a, 0], recv.at[a, 0], sib) for a in range(n)]
        sent += [_rcopy(bufs[r].at[c], spread.at[jme, c], send.at[r, 1 + k], recv.at[r, 1 + k], (cx, cy, c))
                 for k, (cx, cy) in enumerate(chips)]
        for cp in sent:
            cp.start()
        for k, (cx, cy) in enumerate(chips):
            landed = spread.at[2 * cx + cy, c]
            _rcopy(landed, landed, send.at[r, 1 + k], recv.at[r, 1 + k], (cx, cy, c)).wait_recv()
            sent.append(_rcopy(landed, landed, send.at[r, 4 + k], recv.at[r, 4 + k], sib))
            sent[-1].start()
        for a in range(n):
            _rcopy(bufs[a].at[1 - c], bufs[a].at[1 - c], send.at[a, 0], recv.at[a, 0], sib).wait_recv()
        for k, (cx, cy) in enumerate(chips):
            passed = spread.at[2 * cx + cy, 1 - c]
            _rcopy(passed, passed, send.at[r, 4 + k], recv.at[r, 4 + k], sib).wait_recv()
        for cp in sent:
            cp.wait_send()

    out_shape = [jax.ShapeDtypeStruct(a.shape, a.dtype) for a in arrs]
    out_shape.append(jax.ShapeDtypeStruct((4,) + arrs[r].shape, arrs[r].dtype))
    res = _comm_call(name, body, arrs, out_shape, 7, aliases={a: a for a in range(n)})
    jme = 2 * lax.axis_index("x") + lax.axis_index("y")
    return res[:n], lax.dynamic_update_index_in_dim(res[n], res[r], jme, 0)


ROW_BLOCK = 512


def esum(name, terms, rows, width, out_dtype, out_slots=None):
    tr = next((t for t in range(min(rows, ROW_BLOCK), 0, -SUBL) if rows % t == 0 and t % SUBL == 0), rows)
    where =jnp.stack([lax.axis_index("c"), 2 * lax.axis_index("x") + lax.axis_index("y")]).astype(jnp.int32)
    pick = {"c": 0, "j": 1}

    def body(s_ref, *refs):
        acc = refs[0][...].astype(f32)
        for r in refs[1:-1]:
            acc = acc + r[...].astype(f32)
        refs[-1][...] = acc.astype(out_dtype)

    specs = []
    for arr, lead in terms:
        if lead is None:
            specs.append(pl.BlockSpec((tr, width), lambda i, s: (i, 0)))
        elif isinstance(lead, str):
            specs.append(pl.BlockSpec((None, tr, width), lambda i, s, lead=lead: (s[pick[lead]], i, 0)))
        else:
            specs.append(pl.BlockSpec((None, tr, width), lambda i, s, lead=lead: (lead, i, 0)))
    if out_slots is None:
        out_spec = pl.BlockSpec((tr, width), lambda i, s: (i, 0))
        out_shape = jax.ShapeDtypeStruct((rows, width), out_dtype)
    else:
        out_spec = pl.BlockSpec((None, tr, width), lambda i, s: (s[0], i, 0))
        out_shape = jax.ShapeDtypeStruct((out_slots, rows, width), out_dtype)
    return pl.pallas_call(
        body, name=name,
        grid_spec=pltpu.PrefetchScalarGridSpec(num_scalar_prefetch=1, grid=(rows // tr,), in_specs=specs, out_specs=out_spec),
        out_shape=out_shape, compiler_params=_cparams(1),
    )(where, *[t[0] for t in terms])


def reduce_to_shards(parts):
    tags = [str(k) for k in range(len(parts))]
    theirs = swap_halves("rs_swap", parts)
    t1 = []
    for tag, p, th in zip(tags, parts, theirs):
        _, _, h, w = p.shape
        t1.append(esum("rs_add_cores" + tag, [(p.reshape(2, 4 * h, w), "c"), (th.reshape(4 * h, w), None)],
                       4 * h, w, p.dtype).reshape(4, h, w))
    landed = exchange_chips("rs_exchange", t1)
    red = []
    for tag, p, t, got in zip(tags, parts, t1, landed):
        _, _, h, w = p.shape
        red.append(esum("rs_add_chips" + tag, [(t, "j"), (got, 0), (got, 1), (got, 2)], h, w, f32, out_slots=2))
    return gather_cores("rs_gather", red)


def _adamw_step(w_ref, g_ref, m_ref, v_ref, d_ref, nm_ref, nv_ref):
    gv = g_ref[...]
    nm = ADAM_B1 * m_ref[...] + (1.0 - ADAM_B1) * gv
    nv = ADAM_B2 * v_ref[...] + (1.0 - ADAM_B2) * jnp.square(gv)
    m_hat = nm / (1.0 - ADAM_B1 ** ADAM_STEP)
    v_hat = nv / (1.0 - ADAM_B2 ** ADAM_STEP)
    d_ref[...] = -ADAM_LR * (m_hat / (jnp.sqrt(v_hat) + ADAM_EPS) + ADAM_WD * w_ref[...])
    nm_ref[...] = nm
    nv_ref[...] = nv


def adamw(name, w, g, m, v, tr=None):
    rows, rest = w.shape[0], w.shape[1:]
    if tr is None:
        tr = ROW_BLOCK if rows % ROW_BLOCK == 0 else rows
    assert rows % tr == 0

    def body(*refs):
        _adamw_step(*refs)

    spec = pl.BlockSpec((tr,) + rest, lambda i: (i,) + (0,) * len(rest))
    return pl.pallas_call(
        body, name=name, grid=(rows // tr,), in_specs=[spec] * 4, out_specs=[spec] * 3,
        out_shape=[jax.ShapeDtypeStruct(w.shape, f32)] * 3, compiler_params=_cparams(1),
    )(w, g, m, v)


def adamw_many(name, ws, gs, ms, vs):
    n = len(ws)

    def body(*refs):
        ins, outs = refs[:4 * n], refs[4 * n:]
        for k in range(n):
            _adamw_step(ins[k], ins[n + k], ins[2 * n + k], ins[3 * n + k], outs[k], outs[n + k], outs[2 * n + k])

    vmem = pl.BlockSpec(memory_space=pltpu.VMEM)
    res = pl.pallas_call(
        body, name=name, in_specs=[vmem] * (4 * n), out_specs=[vmem] * (3 * n),
        out_shape=[jax.ShapeDtypeStruct(w.shape, f32) for w in ws] * 3,
        compiler_params=pltpu.CompilerParams(vmem_limit_bytes=VMEM_LIMIT),
    )(*ws, *gs, *ms, *vs)
    return res[:n], res[n:2 * n], res[2 * n:]


PACK_W = 1024


def _pack(parts, halves, row_mult):
    flat = jnp.concatenate([p.reshape(-1) for p in parts])
    per = halves * row_mult * PACK_W
    total = -(-flat.size // per) * per
    flat = jnp.pad(flat, (0, total - flat.size))
    return flat.reshape(halves, total // (halves * PACK_W), PACK_W)


def _unpack(flat, shapes):
    out, pos = [], 0
    for s in shapes:
        n = int(np.prod(s))
        out.append(flat[pos:pos + n].reshape(s))
        pos += n
    return out


SHARDED_BIG = ("w_in", "w_branch", "w_out", "s5_w_glu")
SHARDED_SMALL = ("m2_conv_w", "sc_conv_w", "merge_b")
SHARD_AXIS = {"w_in": 2, "w_branch": 3, "w_out": 1, "s5_w_glu": 1, "m2_conv_w": 2, "sc_conv_w": 2, "merge_b": 2}
REPLICATED = ("norm_w", "s5_lambda_re", "s5_lambda_im", "s5_b_re", "s5_b_im", "s5_c_re", "s5_c_im", "s5_d", "s5_log_step",
              "sgu_ln_w", "sgu_ln_b", "sgu_w", "sgu_b", "m2_conv_b", "m2_dt_bias", "m2_a_log", "m2_d", "m2_norm_w")
WEIGHTS = ("norm_w", "w_in", "s5_lambda_re", "s5_lambda_im", "s5_b_re", "s5_b_im", "s5_c_re", "s5_c_im", "s5_d",
           "s5_log_step", "s5_w_glu", "sgu_ln_w", "sgu_ln_b", "sgu_w", "sgu_b", "m2_conv_w", "m2_conv_b", "m2_dt_bias",
           "m2_a_log", "m2_d", "m2_norm_w", "sc_conv_w", "merge_b", "w_branch", "w_out", "final_norm_w")
N_LAYERS = 2


SHARD_W = IN_DIM // 4
SHARD_PAD = -(-SHARD_W // LANES) * LANES
REGROUP_W = 3 * LANES


def _kernel_pieces():
    out, pos = [], 0
    for s, n in _col_segments():
        while n:
            take = min(n, SHARD_W - s % SHARD_W)
            out.append((pos, s, take))
            pos, s, n = pos + take, s + take, n - take
    return out


def regroup_cols(name, src, steps, out_shape, out=None):
    n_src, rows, width = src.shape
    ow, win = REGROUP_W, REGROUP_W + LANES
    k_max = max(len(p) for _, _, p in steps)
    assert width % LANES == 0

    def body(*refs):
        src_ref, out_ref = refs[0], refs[-5]
        wbuf, obuf, sem_in, sem_out = refs[-4:]

        def fetch(q, slot):
            started = []
            for p, (s, col, lo, hi) in enumerate(steps[q][2]):
                w0 = col // LANES * LANES
                wlen = min(win, width - w0)
                cp = pltpu.make_async_copy(src_ref.at[s, :, pl.ds(w0, wlen)], wbuf.at[slot, p, :, pl.ds(0, wlen)], sem_in.at[slot, p])
                cp.start()
                started.append((cp, wlen, col - w0 - lo, lo, hi))
            return started

        pend, writes = fetch(0, 0), [None, None]
        for q, (t, b, _) in enumerate(steps):
            slot = q % 2
            nxt = fetch(q + 1, 1 - slot) if q + 1 < len(steps) else []
            acc = [jnp.zeros((rows, LANES), f32) for _ in range(ow // LANES)]
            for p, (cp, wlen, shift, lo, hi) in enumerate(pend):
                cp.wait()
                for k in range(ow // LANES):
                    c_lo, c_hi = max(lo, k * LANES), min(hi, (k + 1) * LANES)
                    if c_lo >= c_hi:
                        continue
                    wb = (c_lo + shift) // LANES
                    alen = min(2 * LANES, wlen - wb * LANES)
                    a = wbuf[slot, p, :, wb * LANES:wb * LANES + alen]
                    r, c = _rows((alen, LANES)), _cols((alen, LANES))
                    sh = shift + (k - wb) * LANES
                    sel = (r == c + sh) & (c >= c_lo - k * LANES) & (c < c_hi - k * LANES)
                    acc[k] = acc[k] + jnp.dot(a, sel.astype(bf16), preferred_element_type=f32)
            if writes[slot] is not None:
                writes[slot].wait()
            for k in range(ow // LANES):
                obuf[slot, :, k * LANES:(k + 1) * LANES] = acc[k].astype(bf16)
            writes[slot] = pltpu.make_async_copy(obuf.at[slot], out_ref.at[t, :, pl.ds(b * ow, ow)], sem_out.at[slot])
            writes[slot].start()
            pend = nxt
        for wr in writes:
            if wr is not None:
                wr.wait()

    operands, io_alias = [src], {}
    if out is not None:
        operands.append(out)
        io_alias = {1: 0}
    return pl.pallas_call(
        body, name=name, in_specs=[ANY] * len(operands), out_specs=ANY,
        out_shape=jax.ShapeDtypeStruct(out_shape, bf16), input_output_aliases=io_alias,
        scratch_shapes=[pltpu.VMEM((2, k_max, rows, win), bf16), pltpu.VMEM((2, rows, ow), bf16),
                        pltpu.SemaphoreType.DMA((2, k_max)), pltpu.SemaphoreType.DMA((2,))],
        compiler_params=pltpu.CompilerParams(vmem_limit_bytes=VMEM_LIMIT),
    )(*operands)


def _steps_to_kernel_cols(layer):
    steps = []
    for o in range(PW // REGROUP_W):
        pieces = []
        for pos, s, n in _kernel_pieces():
            lo, hi = max(pos, o * REGROUP_W), min(pos + n, (o + 1) * REGROUP_W)
            if lo < hi:
                ref_col = s + lo - pos
                pieces.append(((ref_col // SHARD_W) * N_LAYERS + layer, ref_col % SHARD_W, lo - o * REGROUP_W, hi - o * REGROUP_W))
        steps.append((0, o, pieces))
    return steps


def _steps_to_shards(layer):
    steps = []
    for j in range(4):
        for b in range(SHARD_PAD // REGROUP_W):
            start, stop = j * SHARD_W + b * REGROUP_W, min(j * SHARD_W + (b + 1) * REGROUP_W, (j + 1) * SHARD_W)
            pieces = []
            for pos, s, n in _kernel_pieces():
                lo, hi = max(s, start), min(s + n, stop)
                if lo < hi:
                    pieces.append((0, pos + lo - s, lo - start, hi - start))
            steps.append((layer * 4 + j, b, pieces))
    return steps


def _gather_weights(w):
    w_in = jnp.pad(w["w_in"].astype(bf16), ((0, 0), (0, 0), (0, SHARD_PAD - SHARD_W)))
    arrs = [w_in, w["w_branch"].reshape(N_LAYERS, N_BRANCH * BW, -1).astype(bf16),
            w["w_out"].astype(bf16), w["s5_w_glu"].astype(bf16), w["m2_conv_w"], w["sc_conv_w"], w["merge_b"]]
    got = gather_chips("ag_weights", arrs)
    cols = lambda t: jnp.transpose(t, (1, 0, 2)).reshape(t.shape[1], -1)
    layers = []
    for i in range(N_LAYERS):
        p = {k: w[k][i] for k in REPLICATED}
        p["w_in"] = regroup_cols(f"w_in_cols{i}", got[0].reshape(4 * N_LAYERS, D_MODEL, SHARD_PAD), _steps_to_kernel_cols(i),
                                 (1, D_MODEL, PW))[0]
        p["w_branch"] = cols(got[1][:, i])
        p["w_out"] = got[2][:, i].reshape(D_MODEL, D_MODEL)
        p["s5_w_glu"] = got[3][:, i].reshape(BW, BW)
        p["m2_conv_w"], p["sc_conv_w"], p["merge_b"] = cols(got[4][:, i]), cols(got[5][:, i]), cols(got[6][:, i])
        layers.append(p)
    return layers


def _reduce_grads(grads, d_final, w):
    to_chips = lambda t: jnp.transpose(t.reshape(t.shape[0], 4, -1), (1, 0, 2))
    stack = lambda f: jnp.stack([f(g) for g in grads])
    dw_in = None
    for i, g in enumerate(grads):
        dw_in = regroup_cols(f"w_in_shards{i}", g["w_in"][None], _steps_to_shards(i), (4 * N_LAYERS, D_MODEL, SHARD_PAD), out=dw_in)
    parts = [dw_in.reshape(N_LAYERS, 4, D_MODEL, SHARD_PAD),
             stack(lambda g: to_chips(g["w_branch"].reshape(N_BRANCH * BW, D_MODEL)).astype(bf16)),
             stack(lambda g: g["w_out"].reshape(4, D_MODEL // 4, D_MODEL).astype(bf16)),
             stack(lambda g: g["s5_w_glu"].reshape(4, BW // 4, BW).astype(bf16))]
    rep = jnp.concatenate([stack(lambda g: g[k]).reshape(-1) for k in REPLICATED] + [d_final.reshape(-1)])
    quarter = -(-rep.size // (4 * 2 * SUBL * PACK_W)) * (2 * SUBL * PACK_W)
    rep = jnp.pad(rep, (0, 4 * quarter - rep.size))
    small = []
    for j in range(4):
        sharded = [stack(lambda g: to_chips(g[k])[j]) for k in SHARDED_SMALL]
        small.append(_pack(sharded, 2, SUBL))
    parts.append(jnp.stack(small, axis=1))
    parts.append(jnp.transpose(rep.reshape(4, 2, quarter // (2 * PACK_W), PACK_W), (1, 0, 2, 3)))
    red, rep_all = reduce_to_shards(parts)
    out = {"w_in": red[0][:, :, :SHARD_W], "w_branch": red[1].reshape(w["w_branch"].shape), "w_out": red[2], "s5_w_glu": red[3]}
    out.update(zip(SHARDED_SMALL, _unpack(red[4].reshape(-1), [w[k].shape for k in SHARDED_SMALL])))
    names = REPLICATED + ("final_norm_w",)
    out.update(zip(names, _unpack(rep_all.reshape(-1), [w[k].shape for k in names])))
    return out


def _update(w, g, m, v):
    d, nm, nv = {}, {}, {}
    flat2 = lambda a: a.reshape(-1, a.shape[-1])
    cols_major = lambda a: jnp.transpose(a, (2, 0, 1))
    res = adamw("adamw_w_in", *[cols_major(t["w_in"]) for t in (w, g, m, v)], tr=SHARD_W // 14)
    d["w_in"], nm["w_in"], nv["w_in"] = (jnp.transpose(r, (1, 2, 0)) for r in res)
    for k in SHARDED_BIG[1:]:
        res = adamw("adamw_" + k, *[flat2(t[k]) for t in (w, g, m, v)])
        d[k], nm[k], nv[k] = (r.reshape(w[k].shape) for r in res)
    for k in ("s5_b_re", "s5_b_im"):
        wide = lambda a: jnp.swapaxes(a, 2, 3).reshape(-1, S5_N)
        res = adamw("adamw_" + k, *[wide(t[k]) for t in (w, g, m, v)])
        d[k], nm[k], nv[k] = (jnp.swapaxes(r.reshape(N_LAYERS, S5_GROUPS, S5_P, S5_N), 2, 3) for r in res)
    rest = [k for k in WEIGHTS if k not in SHARDED_BIG + ("s5_b_re", "s5_b_im")]
    two_d = lambda a: a.reshape(1, -1) if a.ndim == 1 else a
    res = adamw_many("adamw_rest", *[[two_d(t[k]) for k in rest] for t in (w, g, m, v)])
    for tgt, rs in zip((d, nm, nv), res):
        tgt.update({k: r.reshape(w[k].shape) for k, r in zip(rest, rs)})
    return d, nm, nv


def kernel(x, norm_w, w_in, s5_lambda_re, s5_lambda_im, s5_b_re, s5_b_im, s5_c_re, s5_c_im, s5_d, s5_log_step, s5_w_glu, sgu_ln_w, sgu_ln_b, sgu_w, sgu_b, m2_conv_w, m2_conv_b, m2_dt_bias, m2_a_log, m2_d, m2_norm_w, sc_conv_w, merge_b, w_branch, w_out, final_norm_w, loss_target, m_norm_w, m_w_in, m_s5_lambda_re, m_s5_lambda_im, m_s5_b_re, m_s5_b_im, m_s5_c_re, m_s5_c_im, m_s5_d, m_s5_log_step, m_s5_w_glu, m_sgu_ln_w, m_sgu_ln_b, m_sgu_w, m_sgu_b, m_m2_conv_w, m_m2_conv_b, m_m2_dt_bias, m_m2_a_log, m_m2_d, m_m2_norm_w, m_sc_conv_w, m_merge_b, m_w_branch, m_w_out, m_final_norm_w, v_norm_w, v_w_in, v_s5_lambda_re, v_s5_lambda_im, v_s5_b_re, v_s5_b_im, v_s5_c_re, v_s5_c_im, v_s5_d, v_s5_log_step, v_s5_w_glu, v_sgu_ln_w, v_sgu_ln_b, v_sgu_w, v_sgu_b, v_m2_conv_w, v_m2_conv_b, v_m2_dt_bias, v_m2_a_log, v_m2_d, v_m2_norm_w, v_sc_conv_w, v_merge_b, v_w_branch, v_w_out, v_final_norm_w):
    given = dict(locals())
    w = {k: given[k] for k in WEIGHTS}
    m = {k: given["m_" + k] for k in WEIGHTS}
    v = {k: given["v_" + k] for k in WEIGHTS}
    layers = _gather_weights(w)
    loss, dx, grads, d_final = local_step(x[0], loss_target[0], layers, final_norm_w)
    loss = lax.psum(loss, ("x", "y", "c"))
    g = _reduce_grads(grads, d_final, w)
    d, nm, nv = _update(w, g, m, v)
    return (loss, dx[None], *[g[k] for k in WEIGHTS], *[d[k] for k in WEIGHTS],
            *[nm[k] for k in WEIGHTS], *[nv[k] for k in WEIGHTS])
```
